```python
import math
import jax, jax.numpy as jnp
from jax import lax
import numpy as np

D_MODEL = 1024
BATCH = 16
SEQ = 256
DEPTH = 2
DEC_BATCH = 4
DEC_SEQ = 2048
PAST_LEN = 256

GRID_W = 64
N_DIR = 2
S5_WIDTH = D_MODEL // 2
S5_GROUP_CH = 16
S5_GROUPS = S5_WIDTH // S5_GROUP_CH
S5_STATE = 64
GLA_WIDTH = D_MODEL // 2
GLA_HEADS = 4
GLA_DV = GLA_WIDTH // GLA_HEADS
GLA_DK = GLA_DV // 2
GLA_KEY_WIDTH = GLA_HEADS * GLA_DK
GLA_GATE_RANK = 16
GLA_GATE_NORM = 16.0
GLA_CHUNK = 32
EPS = 1e-6
IN_WIDTHS = (S5_WIDTH, S5_WIDTH, GLA_KEY_WIDTH, GLA_KEY_WIDTH, GLA_WIDTH, GLA_WIDTH, GLA_GATE_RANK, D_MODEL, D_MODEL)
D_IN = sum(IN_WIDTHS)

kernel_name = 'hybrid_s5_gla_flow_step'


def rmsnorm(x, g):
    xf = x.astype(jnp.float32)
    y = xf * lax.rsqrt(jnp.mean(xf * xf, axis=-1, keepdims=True) + EPS)
    return (y * g.astype(jnp.float32)).astype(x.dtype)


def _split_points():
    pts, acc = [], 0
    for w in IN_WIDTHS[:-1]:
        acc += w
        pts.append(acc)
    return pts


def grid_pos_embed(length, dim, dtype):
    rows = length // GRID_W
    quarter = dim // 4
    freqs = jnp.exp(-math.log(10000.0) * jnp.arange(quarter, dtype=jnp.float32) / quarter)

    def sincos(pos):
        ang = pos.astype(jnp.float32)[:, None] * freqs[None, :]
        return jnp.concatenate([jnp.sin(ang), jnp.cos(ang)], axis=-1)

    er = sincos(jnp.arange(rows))
    ec = sincos(jnp.arange(GRID_W))
    pe = jnp.concatenate([jnp.broadcast_to(er[:, None, :], (rows, GRID_W, dim // 2)),
                          jnp.broadcast_to(ec[None, :, :], (rows, GRID_W, dim // 2))], axis=-1)
    return pe.reshape(rows * GRID_W, dim).astype(dtype)


def _lin_combine(e1, e2):
    a1, b1 = e1
    a2, b2 = e2
    return a2 * a1, a2 * b1 + b2


def s5_direction(u, lam_re, lam_im, log_dt, b_c, h0):
    lam = lax.complex(lam_re.astype(jnp.float32), lam_im.astype(jnp.float32))
    dt = jnp.exp(log_dt.astype(jnp.float32))[:, None]
    a_bar = jnp.exp(lam * dt)
    b_bar = ((a_bar - 1.0) / lam)[..., None] * b_c
    bu = jnp.einsum('gpc,blgc->blgp', b_bar, u.astype(jnp.complex64))
    bu = bu.at[:, 0].add(a_bar[None] * h0)
    a = jnp.broadcast_to(a_bar, bu.shape)
    _, h = lax.associative_scan(_lin_combine, (a, bu), axis=1)
    return h


def s5_branch(u, p, h0_re, h0_im):
    bsz, length, _ = u.shape
    uf = u.astype(jnp.float32).reshape(bsz, length, S5_GROUPS, S5_GROUP_CH)
    b_c = lax.complex(p['s5_b_re'].astype(jnp.float32), p['s5_b_im'].astype(jnp.float32))
    c_c = lax.complex(p['s5_c_re'].astype(jnp.float32), p['s5_c_im'].astype(jnp.float32))
    h0 = lax.complex(h0_re.astype(jnp.float32), h0_im.astype(jnp.float32))
    h_f = s5_direction(uf, p['s5_lam_re'][0], p['s5_lam_im'][0], p['s5_log_dt'][0], b_c, h0[:, 0])
    h_b = jnp.flip(s5_direction(jnp.flip(uf, 1), p['s5_lam_re'][1], p['s5_lam_im'][1],
                                p['s5_log_dt'][1], b_c, h0[:, 1]), 1)
    y = jnp.einsum('gcp,blgp->blgc', c_c, h_f + h_b).real.reshape(bsz, length, S5_WIDTH)
    y = y + p['s5_d'].astype(jnp.float32) * uf.reshape(bsz, length, S5_WIDTH)
    y = jax.nn.gelu(y).astype(u.dtype)
    y = y * jax.nn.sigmoid(y @ p['w_glu'] + p['b_glu'])
    h_final = jnp.stack([h_f[:, -1], h_b[:, 0]], axis=1)
    return y, h_final.real, h_final.imag


def gla_chunked(q, k, v, g, s0):
    bsz, length, nh, dk = q.shape
    dv = v.shape[-1]
    n = length // GLA_CHUNK
    q = q.astype(jnp.float32).reshape(bsz, n, GLA_CHUNK, nh, dk)
    k = k.astype(jnp.float32).reshape(bsz, n, GLA_CHUNK, nh, dk)
    v = v.astype(jnp.float32).reshape(bsz, n, GLA_CHUNK, nh, dv)
    g = g.astype(jnp.float32).reshape(bsz, n, GLA_CHUNK, nh, dk)
    b = jnp.cumsum(g, axis=2)
    b_last = b[:, :, -1]
    causal = jnp.tril(jnp.ones((GLA_CHUNK, GLA_CHUNK), dtype=bool))[None, None, :, :, None, None]
    diff = b[:, :, :, None] - b[:, :, None, :]
    decay = jnp.exp(jnp.where(causal, diff, -jnp.inf))
    scores = jnp.einsum('bnihd,bnijhd,bnjhd->bnhij', q, decay, k)
    o_intra = jnp.einsum('bnhij,bnjhe->bnihe', scores, v)
    k_to_end = k * jnp.exp(b_last[:, :, None] - b)
    u_chunk = jnp.einsum('bnjhd,bnjhe->bnhde', k_to_end, v)
    a_chunk = jnp.exp(b_last)

    def step(s, inp):
        a_c, u_c = inp
        return a_c[..., None] * s + u_c, s

    s_final, s_start = lax.scan(step, s0.astype(jnp.float32),
                                (jnp.moveaxis(a_chunk, 1, 0), jnp.moveaxis(u_chunk, 1, 0)))
    s_start = jnp.moveaxis(s_start, 0, 1)
    o_inter = jnp.einsum('bnihd,bnhde->bnihe', q * jnp.exp(b), s_start)
    o = (o_intra + o_inter).reshape(bsz, length, nh, dv)
    return o, s_final


def gla_branch(q_in, k_in, v_in, g_low, p, s0):
    bsz, length, _ = q_in.shape
    q = q_in.reshape(bsz, length, GLA_HEADS, GLA_DK) * (GLA_DK ** -0.5)
    k = k_in.reshape(bsz, length, GLA_HEADS, GLA_DK)
    v = v_in.reshape(bsz, length, GLA_HEADS, GLA_DV)

    def log_decay(d):
        logits = (g_low @ p['gla_wg_up'][d] + p['gla_bg'][d]).astype(jnp.float32)
        return (jax.nn.log_sigmoid(logits) / GLA_GATE_NORM).reshape(bsz, length, GLA_HEADS, GLA_DK)

    o_f, s_f = gla_chunked(q, k, v, log_decay(0), s0[:, 0])
    o_b, s_b = gla_chunked(jnp.flip(q, 1), jnp.flip(k, 1), jnp.flip(v, 1), jnp.flip(log_decay(1), 1), s0[:, 1])
    o = o_f + jnp.flip(o_b, 1)
    o = rmsnorm(o, p['gla_norm_g'].reshape(GLA_HEADS, GLA_DV)).reshape(bsz, length, GLA_WIDTH)
    return o.astype(q_in.dtype), jnp.stack([s_f, s_b], axis=1)


def trunk_layer(x, cond, p, s5_re0, s5_im0, gla0):
    mod = jax.nn.silu(cond) @ p['w_mod'] + p['b_mod']
    shift, scale, gate = jnp.split(mod[:, None, :], 3, axis=-1)
    h = rmsnorm(x, p['norm_g']) * (1.0 + scale) + shift
    proj = h @ p['w_in']
    u_a, gate_a, q, k, v, gate_b, g_low, m_a, m_b = jnp.split(proj, _split_points(), axis=-1)
    y_a, s5_re, s5_im = s5_branch(u_a, p, s5_re0, s5_im0)
    y_a = y_a * jax.nn.silu(gate_a)
    y_b, gla_s = gla_branch(q, k, v, g_low, p, gla0)
    y_b = y_b * jax.nn.silu(gate_b)
    merged = jax.nn.sigmoid(m_a) * (y_a @ p['w_pa']) + jax.nn.sigmoid(m_b) * (y_b @ p['w_pb'])
    x = x + gate * (merged @ p['w_o'])
    return x, s5_re, s5_im, gla_s


def setup_inputs(seed: int = 0) -> dict:
    key = jax.random.key(seed)
    ks = jax.random.split(key, 32)
    nrm = lambda i, shape, s=1.0: s * jax.random.normal(ks[i], shape, jnp.float32)
    d = D_MODEL
    lam_im_base = math.pi * jnp.arange(S5_STATE, dtype=jnp.float32)
    return {
        'x_prompt': nrm(0, (BATCH, SEQ, d)),
        'x_sample': nrm(1, (DEC_BATCH, DEC_SEQ, d)),
        'c': nrm(2, (DEC_BATCH, d)),
        'state_s5_re': nrm(3, (DEC_BATCH, DEPTH, N_DIR, S5_GROUPS, S5_STATE), 0.3),
        'state_s5_im': nrm(4, (DEC_BATCH, DEPTH, N_DIR, S5_GROUPS, S5_STATE), 0.3),
        'state_gla': nrm(5, (DEC_BATCH, DEPTH, N_DIR, GLA_HEADS, GLA_DK, GLA_DV), 0.3),
        'c_ctx': nrm(6, (d,)),
        'norm_g': 1.0 + nrm(7, (DEPTH, d), 0.02),
        'w_mod': nrm(8, (DEPTH, d, 3 * d), d ** -0.5),
        'b_mod': nrm(9, (DEPTH, 3 * d), 0.02),
        'w_in': nrm(10, (DEPTH, d, D_IN), d ** -0.5),
        'gla_wg_up': nrm(11, (DEPTH, N_DIR, GLA_GATE_RANK, GLA_KEY_WIDTH), GLA_GATE_RANK ** -0.5),
        'gla_bg': nrm(12, (DEPTH, N_DIR, GLA_KEY_WIDTH), 0.1),
        'gla_norm_g': 1.0 + nrm(13, (DEPTH, GLA_WIDTH), 0.02),
        's5_lam_re': -0.5 + nrm(14, (DEPTH, N_DIR, S5_GROUPS, S5_STATE), 0.01),
        's5_lam_im': lam_im_base + nrm(15, (DEPTH, N_DIR, S5_GROUPS, S5_STATE), 0.01),
        's5_log_dt': jax.random.uniform(ks[16], (DEPTH, N_DIR, S5_GROUPS), jnp.float32,
                                        math.log(1e-3), math.log(1e-1)),
        's5_b_re': nrm(17, (DEPTH, S5_GROUPS, S5_STATE, S5_GROUP_CH), (2 * S5_GROUP_CH) ** -0.5),
        's5_b_im': nrm(18, (DEPTH, S5_GROUPS, S5_STATE, S5_GROUP_CH), (2 * S5_GROUP_CH) ** -0.5),
        's5_c_re': nrm(19, (DEPTH, S5_GROUPS, S5_GROUP_CH, S5_STATE), S5_STATE ** -0.5),
        's5_c_im': nrm(20, (DEPTH, S5_GROUPS, S5_GROUP_CH, S5_STATE), S5_STATE ** -0.5),
        's5_d': nrm(21, (DEPTH, S5_WIDTH)),
        'w_glu': nrm(22, (DEPTH, S5_WIDTH, S5_WIDTH), S5_WIDTH ** -0.5),
        'b_glu': nrm(23, (DEPTH, S5_WIDTH), 0.02),
        'w_pa': nrm(24, (DEPTH, S5_WIDTH, d), S5_WIDTH ** -0.5),
        'w_pb': nrm(25, (DEPTH, GLA_WIDTH, d), GLA_WIDTH ** -0.5),
        'w_o': nrm(26, (DEPTH, d, d), d ** -0.5),
        'final_norm_g': 1.0 + nrm(27, (d,), 0.02),
    }


def reference(x_prompt, x_sample, c, state_s5_re, state_s5_im, state_gla, c_ctx, norm_g, w_mod, b_mod,
              w_in, gla_wg_up, gla_bg, gla_norm_g, s5_lam_re, s5_lam_im, s5_log_dt, s5_b_re, s5_b_im,
              s5_c_re, s5_c_im, s5_d, w_glu, b_glu, w_pa, w_pb, w_o, final_norm_g):
    bp = x_prompt.shape[0]
    xp = x_prompt
    xs = x_sample + grid_pos_embed(x_sample.shape[1], D_MODEL, x_sample.dtype)[None]
    zero_s5 = jnp.zeros((bp, N_DIR, S5_GROUPS, S5_STATE), jnp.float32)
    zero_gla = jnp.zeros((bp, N_DIR, GLA_HEADS, GLA_DK, GLA_DV), jnp.float32)
    cond_ctx = c_ctx[None]
    new_re, new_im, new_gla = [], [], []
    for l in range(DEPTH):
        p = {'norm_g': norm_g[l], 'w_mod': w_mod[l], 'b_mod': b_mod[l], 'w_in': w_in[l],
             'gla_wg_up': gla_wg_up[l], 'gla_bg': gla_bg[l], 'gla_norm_g': gla_norm_g[l],
             's5_lam_re': s5_lam_re[l], 's5_lam_im': s5_lam_im[l], 's5_log_dt': s5_log_dt[l],
             's5_b_re': s5_b_re[l], 's5_b_im': s5_b_im[l], 's5_c_re': s5_c_re[l], 's5_c_im': s5_c_im[l],
             's5_d': s5_d[l], 'w_glu': w_glu[l], 'b_glu': b_glu[l], 'w_pa': w_pa[l], 'w_pb': w_pb[l],
             'w_o': w_o[l]}
        xp, re_l, im_l, gla_l = trunk_layer(xp, cond_ctx, p, zero_s5, zero_s5, zero_gla)
        new_re.append(re_l)
        new_im.append(im_l)
        new_gla.append(gla_l)
        xs, _, _, _ = trunk_layer(xs, c, p, state_s5_re[:, l], state_s5_im[:, l], state_gla[:, l])
    y_prompt = rmsnorm(xp, final_norm_g)
    y_sample = rmsnorm(xs, final_norm_g)
    new_s5_re = jnp.stack(new_re, axis=1).astype(x_prompt.dtype)
    new_s5_im = jnp.stack(new_im, axis=1).astype(x_prompt.dtype)
    new_gla_state = jnp.stack(new_gla, axis=1).astype(x_prompt.dtype)
    return (y_prompt, y_sample, new_s5_re, new_s5_im, new_gla_state)
```

```python
import functools
import math

import numpy as np
import jax
import jax.numpy as jnp
from jax import lax
from jax.experimental import pallas as pl
from jax.experimental.pallas import tpu as pltpu

F32 = jnp.float32
BF16 = jnp.bfloat16

D_MODEL = 1024
GRID_W = 64
S5_WIDTH = 512
S5_GROUPS = 32
S5_CH = 16
S5_STATE = 64
S5_NC = S5_GROUPS * S5_STATE
S5_SLABS = 4
GLA_HEADS = 4
GLA_DK = 64
GLA_DV = 128
GLA_KEY = GLA_HEADS * GLA_DK
GLA_WIDTH = GLA_HEADS * GLA_DV
GLA_RANK = 16
GLA_GATE_NORM = 16.0
GLA_CHUNK = 128
EPS = 1e-6
LANES = 128
PROJ_W = S5_WIDTH + 2 * GLA_KEY + GLA_WIDTH + LANES
GATE_W = 2 * S5_WIDTH + 2 * D_MODEL
VMEM_LIMIT = 52 * 1024 * 1024


def _bdot(a, b):
    return jnp.dot(a.astype(BF16), b.astype(BF16), preferred_element_type=F32)


def _split2(x):
    hi = x.astype(BF16)
    lo = (x - hi.astype(F32)).astype(BF16)
    return hi, lo


def _sigmoid(x):
    return 1.0 / (1.0 + jnp.exp(-x))


def _silu(x):
    return x * _sigmoid(x)


def _gelu_tanh(x):
    c = math.sqrt(2.0 / math.pi)
    return 0.5 * x * (1.0 + jnp.tanh(c * (x + 0.044715 * (x * x * x))))


def _log_sigmoid(x):
    return jnp.minimum(x, 0.0) - jnp.log(1.0 + jnp.exp(-jnp.abs(x)))


def _modulated_norm(x, ng, scale1p, shift):
    ms = jnp.mean(x * x, axis=-1, keepdims=True)
    return (x * lax.rsqrt(ms + EPS) * ng) * scale1p + shift


def _mod_kernel(cond_ref, w_ref, b_ref, out_ref):
    c = cond_ref[...]
    s_hi, s_lo = _split2(_silu(c))
    w_hi, w_lo = _split2(w_ref[0])
    acc = jnp.dot(s_hi, w_hi, preferred_element_type=F32)
    acc += jnp.dot(s_lo, w_hi, preferred_element_type=F32)
    acc += jnp.dot(s_hi, w_lo, preferred_element_type=F32)
    out_ref[0] = acc + b_ref[0]


def _mod_call(cond8, w_mod, b_mod):
    depth = w_mod.shape[0]
    nb = 768
    return pl.pallas_call(
        _mod_kernel,
        grid=(depth, 3 * D_MODEL // nb),
        in_specs=[
            pl.BlockSpec((8, D_MODEL), lambda l, j: (0, 0)),
            pl.BlockSpec((1, D_MODEL, nb), lambda l, j: (l, 0, j)),
            pl.BlockSpec((1, 1, nb), lambda l, j: (l, 0, j)),
        ],
        out_specs=pl.BlockSpec((1, 8, nb), lambda l, j: (l, 0, j)),
        out_shape=jax.ShapeDtypeStruct((depth, 8, 3 * D_MODEL), F32),
        compiler_params=pltpu.CompilerParams(dimension_semantics=("arbitrary", "arbitrary")),
        name="adaln_mod",
    )(cond8, w_mod, b_mod.reshape(depth, 1, 3 * D_MODEL))


def _zoh_kernel(lre_ref, lim_ref, ldt_ref, bre_ref, bim_ref, are_ref, aim_ref, bbre_ref, bbim_ref):
    lre = lre_ref[...]
    lim = lim_ref[...]
    dt = jnp.exp(ldt_ref[...])
    mag = jnp.exp(lre * dt)
    a_re = mag * jnp.cos(lim * dt)
    a_im = mag * jnp.sin(lim * dt)
    n_re = a_re - 1.0
    inv = 1.0 / (lre * lre + lim * lim)
    c_re = (n_re * lre + a_im * lim) * inv
    c_im = (a_im * lre - n_re * lim) * inv
    bre = bre_ref[...]
    bim = bim_ref[...]
    are_ref[...] = a_re
    aim_ref[...] = a_im
    bbre_ref[...] = c_re * bre - c_im * bim
    bbim_ref[...] = c_re * bim + c_im * bre


def _zoh_call(lre, lim, ldt, bre, bim):
    shp = jax.ShapeDtypeStruct(lre.shape, F32)
    return pl.pallas_call(
        _zoh_kernel,
        out_shape=(shp, shp, shp, shp),
        name="s5_zoh",
    )(lre, lim, ldt, bre, bim)


def _s5_params(s5_lam_re, s5_lam_im, s5_log_dt, s5_b_re, s5_b_im, s5_c_re, s5_c_im):
    depth = s5_lam_re.shape[0]
    full = (depth, 2, S5_GROUPS, S5_CH, S5_STATE)
    expand = lambda a: jnp.broadcast_to(a, full).reshape(-1, S5_STATE)
    lre = expand(s5_lam_re[:, :, :, None, :])
    lim = expand(s5_lam_im[:, :, :, None, :])
    ldt = expand(s5_log_dt[:, :, :, None, None])
    bre = expand(jnp.swapaxes(s5_b_re, -1, -2)[:, None])
    bim = expand(jnp.swapaxes(s5_b_im, -1, -2)[:, None])
    are, aim, bbre, bbim = [a.reshape(full) for a in _zoh_call(lre, lim, ldt, bre, bim)]
    a_re = are[:, :, :, 0, :].reshape(depth, 2, S5_NC)
    a_im = aim[:, :, :, 0, :].reshape(depth, 2, S5_NC)
    eye = jnp.eye(8, dtype=F32)
    t = jnp.stack([bbre, bbim], axis=-2)
    t = t.reshape(depth, 2, S5_SLABS, 8, S5_CH, 2, S5_STATE)
    t = jnp.transpose(t, (0, 2, 1, 3, 4, 5, 6))
    w_bu = t[:, :, :, :, :, :, None, :] * eye[None, None, None, :, None, None, :, None]
    w_bu = w_bu.reshape(depth, S5_SLABS, 2 * LANES, 2 * 8 * S5_STATE).astype(BF16)
    v = jnp.stack([s5_c_re, -s5_c_im], axis=1)
    v = v.reshape(depth, 2, S5_SLABS, 8, S5_CH, S5_STATE)
    v = jnp.transpose(v, (0, 2, 1, 3, 5, 4))
    w_c = v[:, :, :, :, :, None, :] * eye[None, None, None, :, None, :, None]
    w_c = w_c.reshape(depth, S5_SLABS, 2 * 8 * S5_STATE, LANES).astype(BF16)
    return a_re, a_im, w_bu, w_c


def _s5_state_to_cols(re, im):
    lead = re.shape[:-2]
    st = jnp.stack([re.reshape(lead + (S5_SLABS, 512)), im.reshape(lead + (S5_SLABS, 512))], axis=-2)
    return st.reshape(lead + (2 * S5_NC,))


def _s5_cols_to_state(cols):
    lead = cols.shape[:-1]
    st = cols.reshape(lead + (S5_SLABS, 2, 512))
    re = st[..., 0, :].reshape(lead + (S5_GROUPS, S5_STATE))
    im = st[..., 1, :].reshape(lead + (S5_GROUPS, S5_STATE))
    return re, im


def _proj_kernel(*refs, has_pe):
    if has_pe:
        x_ref, pe_ref, shift_ref, scale_ref, ng_ref, w_ref, u_ref, q_ref, k_ref, v_ref, gl_ref = refs
        x = x_ref[0] + pe_ref[...]
    else:
        x_ref, shift_ref, scale_ref, ng_ref, w_ref, u_ref, q_ref, k_ref, v_ref, gl_ref = refs
        x = x_ref[0]
    h = _modulated_norm(x, ng_ref[...], scale_ref[0], shift_ref[0])
    p = _bdot(h, w_ref[...])
    o = 0
    u_ref[0] = p[:, o:o + S5_WIDTH]
    o += S5_WIDTH
    q_ref[0] = p[:, o:o + GLA_KEY]
    o += GLA_KEY
    k_ref[0] = p[:, o:o + GLA_KEY]
    o += GLA_KEY
    v_ref[0] = p[:, o:o + GLA_WIDTH].astype(BF16)
    o += GLA_WIDTH
    gl_ref[0] = p[:, o:o + LANES]


def _proj_call(x, pe, shift, scale1p, ng, w, tm):
    nseq, length, _ = x.shape
    row = lambda width: pl.BlockSpec((1, tm, width), lambda b, j: (b, j, 0))
    vec = pl.BlockSpec((1, 1, D_MODEL), lambda b, j: (b, 0, 0))
    in_specs = [row(D_MODEL)]
    args = [x]
    if pe is not None:
        in_specs.append(pl.BlockSpec((tm, D_MODEL), lambda b, j: (j, 0)))
        args.append(pe)
    in_specs += [vec, vec, pl.BlockSpec((1, D_MODEL), lambda b, j: (0, 0)),
                 pl.BlockSpec((D_MODEL, PROJ_W), lambda b, j: (0, 0))]
    args += [shift, scale1p, ng, w]
    sds = lambda width, dt: jax.ShapeDtypeStruct((nseq, length, width), dt)
    return pl.pallas_call(
        functools.partial(_proj_kernel, has_pe=pe is not None),
        grid=(nseq, length // tm),
        in_specs=in_specs,
        out_specs=(row(S5_WIDTH), row(GLA_KEY), row(GLA_KEY), row(GLA_WIDTH), row(LANES)),
        out_shape=(sds(S5_WIDTH, F32), sds(GLA_KEY, F32), sds(GLA_KEY, F32), sds(GLA_WIDTH, BF16), sds(LANES, F32)),
        compiler_params=pltpu.CompilerParams(dimension_semantics=("arbitrary", "arbitrary"),
                                             vmem_limit_bytes=VMEM_LIMIT),
        name="proj",
    )(*args)


def _s5_perms(nb, tc):
    r = nb * tc
    pf = np.zeros((2 * r, r), np.float32)
    pb = np.zeros((2 * r, r), np.float32)
    for t in range(tc):
        for b in range(nb):
            pf[t * 2 * nb + b, b * tc + t] = 1.0
            pb[t * 2 * nb + nb + b, b * tc + (tc - 1 - t)] = 1.0
    return pf, pb


def _s5_kernel(uf_ref, ub_ref, pf_ref, pb_ref, pft_ref, pbt_ref, wbu_ref, wc_ref, are_ref, aim_ref, h0_ref,
               yf_ref, yb_ref, hfin_ref, bu_ref, st_ref, *, nb, tc, tiles_per_pass, unroll):
    i = pl.program_id(0)
    rows = nb * tc
    rs = 2 * nb

    @pl.when(i == 0)
    def _():
        st_ref[...] = h0_ref[...]

    uf = uf_ref[...].reshape(rows, S5_WIDTH).astype(BF16)
    ub = ub_ref[...].reshape(rows, S5_WIDTH).astype(BF16)
    u2f = jnp.dot(pf_ref[...], uf, preferred_element_type=F32).astype(BF16)
    u2b = jnp.dot(pb_ref[...], ub, preferred_element_type=F32).astype(BF16)
    for s in range(S5_SLABS):
        lhs = jnp.concatenate([u2f[:, s * LANES:(s + 1) * LANES], u2b[:, s * LANES:(s + 1) * LANES]], axis=1)
        bu_ref[:, s * 1024:(s + 1) * 1024] = jnp.dot(lhs, wbu_ref[s], preferred_element_type=F32)

    n_ct = S5_NC // LANES
    for c0 in range(0, n_ct, tiles_per_pass):
        cts = list(range(c0, c0 + tiles_per_pass))
        cre = [(ct // 4) * 1024 + (ct % 4) * LANES for ct in cts]
        cim = [c + 512 for c in cre]
        a_re = [are_ref[:, ct * LANES:(ct + 1) * LANES] for ct in cts]
        a_im = [aim_ref[:, ct * LANES:(ct + 1) * LANES] for ct in cts]
        n = len(cts)
        init = tuple(st_ref[:, c:c + LANES] for c in cre) + tuple(st_ref[:, c:c + LANES] for c in cim)

        def step(t, carry, cre=cre, cim=cim, a_re=a_re, a_im=a_im, n=n):
            r0 = pl.multiple_of(t * rs, rs)
            new_re, new_im = [], []
            for j in range(n):
                h_re, h_im = carry[j], carry[n + j]
                b_re = bu_ref[pl.ds(r0, rs), cre[j]:cre[j] + LANES]
                b_im = bu_ref[pl.ds(r0, rs), cim[j]:cim[j] + LANES]
                n_re = a_re[j] * h_re - a_im[j] * h_im + b_re
                n_im = a_re[j] * h_im + a_im[j] * h_re + b_im
                bu_ref[pl.ds(r0, rs), cre[j]:cre[j] + LANES] = n_re
                bu_ref[pl.ds(r0, rs), cim[j]:cim[j] + LANES] = n_im
                new_re.append(n_re)
                new_im.append(n_im)
            return tuple(new_re) + tuple(new_im)

        fin = lax.fori_loop(0, tc, step, init, unroll=unroll)
        for j in range(n):
            st_ref[:, cre[j]:cre[j] + LANES] = fin[j]
            st_ref[:, cim[j]:cim[j] + LANES] = fin[n + j]

    ys = [jnp.dot(bu_ref[:, s * 1024:(s + 1) * 1024].astype(BF16), wc_ref[s], preferred_element_type=F32)
          for s in range(S5_SLABS)]
    y2 = jnp.concatenate(ys, axis=1)
    y_hi, y_lo = _split2(y2)
    yf = jnp.dot(pft_ref[...], y_hi, preferred_element_type=F32) + jnp.dot(pft_ref[...], y_lo, preferred_element_type=F32)
    yb = jnp.dot(pbt_ref[...], y_hi, preferred_element_type=F32) + jnp.dot(pbt_ref[...], y_lo, preferred_element_type=F32)
    yf_ref[...] = yf.reshape(nb, tc, S5_WIDTH)
    yb_ref[...] = yb.reshape(nb, tc, S5_WIDTH)
    hfin_ref[...] = st_ref[...]


def _s5_call(u, w_bu, w_c, a_re, a_im, h0, tc, tiles_per_pass, unroll):
    nb, length, _ = u.shape
    n = length // tc
    rows = nb * tc
    pf, pb = _s5_perms(nb, tc)
    are = jnp.repeat(a_re, nb, axis=0)
    aim = jnp.repeat(a_im, nb, axis=0)
    const = lambda shape: pl.BlockSpec(shape, lambda i: (0,) * len(shape))
    blk_f = pl.BlockSpec((nb, tc, S5_WIDTH), lambda i: (0, i, 0))
    blk_b = pl.BlockSpec((nb, tc, S5_WIDTH), lambda i: (0, n - 1 - i, 0))
    return pl.pallas_call(
        functools.partial(_s5_kernel, nb=nb, tc=tc, tiles_per_pass=tiles_per_pass, unroll=unroll),
        grid=(n,),
        in_specs=[blk_f, blk_b,
                  const((2 * rows, rows)), const((2 * rows, rows)), const((rows, 2 * rows)), const((rows, 2 * rows)),
                  const((S5_SLABS, 2 * LANES, 1024)), const((S5_SLABS, 1024, LANES)),
                  const((2 * nb, S5_NC)), const((2 * nb, S5_NC)), const((2 * nb, 2 * S5_NC))],
        out_specs=(blk_f, blk_b, const((2 * nb, 2 * S5_NC))),
        out_shape=(jax.ShapeDtypeStruct(u.shape, F32), jax.ShapeDtypeStruct(u.shape, F32),
                   jax.ShapeDtypeStruct((2 * nb, 2 * S5_NC), F32)),
        scratch_shapes=[pltpu.VMEM((2 * rows, 2 * S5_NC), F32), pltpu.VMEM((2 * nb, 2 * S5_NC), F32)],
        compiler_params=pltpu.CompilerParams(dimension_semantics=("arbitrary",), vmem_limit_bytes=VMEM_LIMIT),
        name="s5_scan",
    )(u, u, jnp.asarray(pf, BF16), jnp.asarray(pb, BF16), jnp.asarray(pf.T, BF16), jnp.asarray(pb.T, BF16),
      w_bu, w_c, are, aim, h0)


def _gla_kernel(*refs, length, has_s0):
    if has_s0:
        q_ref, k_ref, v_ref, gl_ref, wg_ref, bg_ref, ng_ref, s0_ref, o_ref, sfin_ref, s_ref = refs
    else:
        q_ref, k_ref, v_ref, gl_ref, wg_ref, bg_ref, ng_ref, o_ref, sfin_ref, s_ref = refs
    c = GLA_CHUNK
    n = length // c
    ri = lax.broadcasted_iota(jnp.int32, (c, c), 0)
    ci = lax.broadcasted_iota(jnp.int32, (c, c), 1)
    lane = lax.broadcasted_iota(jnp.int32, (1, GLA_KEY), 1)
    head_mask = [((lane >= h * GLA_DK) & (lane < (h + 1) * GLA_DK)).astype(F32) for h in range(GLA_HEADS)]
    srow = lax.shift_right_logical(lax.broadcasted_iota(jnp.int32, (GLA_KEY, GLA_WIDTH), 0), 6)
    scol = lax.shift_right_logical(lax.broadcasted_iota(jnp.int32, (GLA_KEY, GLA_WIDTH), 1), 7)
    bd_mask = (srow == scol).astype(F32)
    ng = ng_ref[...]

    for d in range(2):
        causal = (ci <= ri) if d == 0 else (ci >= ri)
        tri = causal.astype(BF16)
        end = c - 1 if d == 0 else 0
        s_ref[...] = jnp.zeros_like(s_ref)
        if has_s0:
            for h in range(GLA_HEADS):
                s_ref[h * GLA_DK:(h + 1) * GLA_DK, h * GLA_DV:(h + 1) * GLA_DV] = s0_ref[0, d, h]
        wg = wg_ref[d]
        bg = bg_ref[d]

        def chunk(step, carry, d=d, causal=causal, tri=tri, end=end, wg=wg, bg=bg):
            blk = step if d == 0 else n - 1 - step
            r0 = pl.multiple_of(blk * c, c)
            qc = q_ref[0, pl.ds(r0, c), :] * (GLA_DK ** -0.5)
            kc = k_ref[0, pl.ds(r0, c), :]
            vc = v_ref[0, pl.ds(r0, c), :]
            logits = _bdot(gl_ref[0, pl.ds(r0, c), :], wg) + bg
            g = _log_sigmoid(logits) * (1.0 / GLA_GATE_NORM)
            g_hi, g_lo = _split2(g)
            b = jnp.dot(tri, g_hi, preferred_element_type=F32) + jnp.dot(tri, g_lo, preferred_element_type=F32)
            bt = b.T
            kt = kc.T
            b_mid = b[c // 2:c // 2 + 1, :]
            bt_mid = bt[:, c // 2:c // 2 + 1]
            bt_end = bt[:, end:end + 1]
            q_in = (qc * jnp.exp(b)).astype(BF16)
            q_mid = qc * jnp.exp(b - b_mid)
            k_mid_t = (kt * jnp.exp(bt_mid - bt)).astype(BF16)
            k_end_t = (kt * jnp.exp(bt_end - bt)).astype(BF16)
            q_st = jnp.concatenate([q_mid * head_mask[h] for h in range(GLA_HEADS)], axis=0).astype(BF16)
            a = jnp.dot(q_st, k_mid_t, preferred_element_type=F32)
            s_old = s_ref[...]
            o_inter = jnp.dot(q_in, s_old.astype(BF16), preferred_element_type=F32)
            parts = []
            for h in range(GLA_HEADS):
                a_h = jnp.where(causal, a[h * c:(h + 1) * c, :], 0.0).astype(BF16)
                parts.append(jnp.dot(a_h, vc[:, h * GLA_DV:(h + 1) * GLA_DV], preferred_element_type=F32))
            o = o_inter + jnp.concatenate(parts, axis=1)
            s_ref[...] = jnp.exp(bt_end) * s_old + jnp.dot(k_end_t, vc, preferred_element_type=F32) * bd_mask
            if d == 0:
                o_ref[0, pl.ds(r0, c), :] = o
            else:
                o = o + o_ref[0, pl.ds(r0, c), :]
                normed = []
                for h in range(GLA_HEADS):
                    oh = o[:, h * GLA_DV:(h + 1) * GLA_DV]
                    ms = jnp.mean(oh * oh, axis=-1, keepdims=True)
                    normed.append(oh * lax.rsqrt(ms + EPS))
                o_ref[0, pl.ds(r0, c), :] = jnp.concatenate(normed, axis=1) * ng
            return carry

        lax.fori_loop(0, n, chunk, 0)
        for h in range(GLA_HEADS):
            sfin_ref[0, d, h] = s_ref[h * GLA_DK:(h + 1) * GLA_DK, h * GLA_DV:(h + 1) * GLA_DV]


def _gla_call(q, k, v, gl, wg, bg, ng, s0):
    nseq, length, _ = q.shape
    seq = lambda width: pl.BlockSpec((1, length, width), lambda b: (b, 0, 0))
    const = lambda shape: pl.BlockSpec(shape, lambda b: (0,) * len(shape))
    st = pl.BlockSpec((1, 2, GLA_HEADS, GLA_DK, GLA_DV), lambda b: (b, 0, 0, 0, 0))
    in_specs = [seq(GLA_KEY), seq(GLA_KEY), seq(GLA_WIDTH), seq(LANES),
                const((2, LANES, GLA_KEY)), const((2, 1, GLA_KEY)), const((1, GLA_WIDTH))]
    args = [q, k, v, gl, wg, bg, ng]
    if s0 is not None:
        in_specs.append(st)
        args.append(s0)
    return pl.pallas_call(
        functools.partial(_gla_kernel, length=length, has_s0=s0 is not None),
        grid=(nseq,),
        in_specs=in_specs,
        out_specs=(seq(GLA_WIDTH), st),
        out_shape=(jax.ShapeDtypeStruct((nseq, length, GLA_WIDTH), F32),
                   jax.ShapeDtypeStruct((nseq, 2, GLA_HEADS, GLA_DK, GLA_DV), F32)),
        scratch_shapes=[pltpu.VMEM((GLA_KEY, GLA_WIDTH), F32)],
        compiler_params=pltpu.CompilerParams(dimension_semantics=("arbitrary",), vmem_limit_bytes=VMEM_LIMIT),
        name="gla",
    )(*args)


def _mix_kernel(*refs, has_pe, last):
    if has_pe:
        (x_ref, pe_ref, yf_ref, yb_ref, u_ref, o_ref, shift_ref, scale_ref, gate_ref, ng_ref, wg_ref, d_ref,
         wglu_ref, bglu_ref, wpa_ref, wpb_ref, wo_ref, fng_ref, out_ref) = refs
        x = x_ref[0] + pe_ref[...]
    else:
        (x_ref, yf_ref, yb_ref, u_ref, o_ref, shift_ref, scale_ref, gate_ref, ng_ref, wg_ref, d_ref,
         wglu_ref, bglu_ref, wpa_ref, wpb_ref, wo_ref, fng_ref, out_ref) = refs
        x = x_ref[0]
    h = _modulated_norm(x, ng_ref[...], scale_ref[0], shift_ref[0])
    gts = _bdot(h, wg_ref[...])
    gate_a = gts[:, :S5_WIDTH]
    gate_b = gts[:, S5_WIDTH:2 * S5_WIDTH]
    m_a = gts[:, 2 * S5_WIDTH:2 * S5_WIDTH + D_MODEL]
    m_b = gts[:, 2 * S5_WIDTH + D_MODEL:]
    y = _gelu_tanh(yf_ref[0] + yb_ref[0] + d_ref[...] * u_ref[0])
    y = y * _sigmoid(_bdot(y, wglu_ref[...]) + bglu_ref[...])
    y_a = y * _silu(gate_a)
    y_b = o_ref[0] * _silu(gate_b)
    merged = _sigmoid(m_a) * _bdot(y_a, wpa_ref[...]) + _sigmoid(m_b) * _bdot(y_b, wpb_ref[...])
    xn = x + gate_ref[0] * _bdot(merged, wo_ref[...])
    if last:
        ms = jnp.mean(xn * xn, axis=-1, keepdims=True)
        xn = xn * lax.rsqrt(ms + EPS) * fng_ref[...]
    out_ref[0] = xn


def _mix_call(x, pe, yf, yb, u, o, shift, scale1p, gate, ng, wg, s5_d, wglu, bglu, wpa, wpb, wo, fng, tm, last):
    nseq, length, _ = x.shape
    row = lambda width: pl.BlockSpec((1, tm, width), lambda b, j: (b, j, 0))
    vec = pl.BlockSpec((1, 1, D_MODEL), lambda b, j: (b, 0, 0))
    const = lambda shape: pl.BlockSpec(shape, lambda b, j: (0,) * len(shape))
    in_specs = [row(D_MODEL)]
    args = [x]
    if pe is not None:
        in_specs.append(pl.BlockSpec((tm, D_MODEL), lambda b, j: (j, 0)))
        args.append(pe)
    in_specs += [row(S5_WIDTH), row(S5_WIDTH), row(S5_WIDTH), row(GLA_WIDTH), vec, vec, vec,
                 const((1, D_MODEL)), const((D_MODEL, GATE_W)), const((1, S5_WIDTH)),
                 const((S5_WIDTH, S5_WIDTH)), const((1, S5_WIDTH)), const((S5_WIDTH, D_MODEL)),
                 const((GLA_WIDTH, D_MODEL)), const((D_MODEL, D_MODEL)), const((1, D_MODEL))]
    args += [yf, yb, u, o, shift, scale1p, gate, ng, wg, s5_d, wglu, bglu, wpa, wpb, wo, fng]
    return pl.pallas_call(
        functools.partial(_mix_kernel, has_pe=pe is not None, last=last),
        grid=(nseq, length // tm),
        in_specs=in_specs,
        out_specs=row(D_MODEL),
        out_shape=jax.ShapeDtypeStruct(x.shape, F32),
        compiler_params=pltpu.CompilerParams(dimension_semantics=("arbitrary", "arbitrary"),
                                             vmem_limit_bytes=VMEM_LIMIT),
        name="mix",
    )(*args)


def _grid_pos_embed(length, dim):
    rows = length // GRID_W
    quarter = dim // 4
    freqs = jnp.exp(-math.log(10000.0) * jnp.arange(quarter, dtype=F32) / quarter)

    def sincos(pos):
        ang = pos.astype(F32)[:, None] * freqs[None, :]
        return jnp.concatenate([jnp.sin(ang), jnp.cos(ang)], axis=-1)

    er = sincos(jnp.arange(rows))
    ec = sincos(jnp.arange(GRID_W))
    pe = jnp.concatenate([jnp.broadcast_to(er[:, None, :], (rows, GRID_W, dim // 2)),
                          jnp.broadcast_to(ec[None, :, :], (rows, GRID_W, dim // 2))], axis=-1)
    return pe.reshape(rows * GRID_W, dim)


def kernel(x_prompt, x_sample, c, state_s5_re, state_s5_im, state_gla, c_ctx, norm_g, w_mod, b_mod, w_in, gla_wg_up,
           gla_bg, gla_norm_g, s5_lam_re, s5_lam_im, s5_log_dt, s5_b_re, s5_b_im, s5_c_re, s5_c_im, s5_d, w_glu,
           b_glu, w_pa, w_pb, w_o, final_norm_g):
    depth = w_in.shape[0]
    bp, lp, _ = x_prompt.shape
    bs, ls, _ = x_sample.shape

    cond8 = jnp.zeros((8, D_MODEL), F32).at[0].set(c_ctx).at[1:1 + bs].set(c)
    mod = _mod_call(cond8, w_mod, b_mod)

    a_re, a_im, w_bu, w_c = _s5_params(s5_lam_re, s5_lam_im, s5_log_dt, s5_b_re, s5_b_im, s5_c_re, s5_c_im)

    offs = np.cumsum([0, S5_WIDTH, S5_WIDTH, GLA_KEY, GLA_KEY, GLA_WIDTH, GLA_WIDTH, GLA_RANK, D_MODEL, D_MODEL])
    col = lambda i: slice(int(offs[i]), int(offs[i + 1]))
    pad = jnp.zeros((depth, D_MODEL, LANES - GLA_RANK), F32)
    w_proj = jnp.concatenate([w_in[:, :, col(0)], w_in[:, :, col(2)], w_in[:, :, col(3)], w_in[:, :, col(4)],
                              w_in[:, :, col(6)], pad], axis=-1).astype(BF16)
    w_gate = jnp.concatenate([w_in[:, :, col(1)], w_in[:, :, col(5)], w_in[:, :, col(7)], w_in[:, :, col(8)]],
                             axis=-1).astype(BF16)
    wg_up = jnp.concatenate([gla_wg_up, jnp.zeros((depth, 2, LANES - GLA_RANK, GLA_KEY), F32)], axis=2).astype(BF16)
    w_glu_b, w_pa_b, w_pb_b, w_o_b = (w.astype(BF16) for w in (w_glu, w_pa, w_pb, w_o))

    pe = _grid_pos_embed(ls, D_MODEL)
    zero_h0 = jnp.zeros((2 * bp, 2 * S5_NC), F32)
    xp, xs = x_prompt, x_sample
    new_re, new_im, new_gla = [], [], []
    for l in range(depth):
        last = l == depth - 1
        m = mod[l]
        shift, scale1p, gate = m[:, :D_MODEL], 1.0 + m[:, D_MODEL:2 * D_MODEL], m[:, 2 * D_MODEL:]
        vec_p = lambda a: jnp.broadcast_to(a[0][None, None, :], (bp, 1, D_MODEL))
        vec_s = lambda a: a[1:1 + bs][:, None, :]
        ng = norm_g[l][None]
        gng = gla_norm_g[l][None]
        bg = gla_bg[l][:, None, :]
        common = (ng, w_gate[l], s5_d[l][None], w_glu_b[l], b_glu[l][None], w_pa_b[l], w_pb_b[l], w_o_b[l],
                  final_norm_g[None])

        u, q, k, v, gl = _proj_call(xp, None, vec_p(shift), vec_p(scale1p), ng, w_proj[l], tm=lp)
        yf, yb, hfin = _s5_call(u, w_bu[l], w_c[l], a_re[l], a_im[l], zero_h0, tc=8, tiles_per_pass=2, unroll=True)
        o, sfin = _gla_call(q, k, v, gl, wg_up[l], bg, gng, None)
        xp = _mix_call(xp, None, yf, yb, u, o, vec_p(shift), vec_p(scale1p), vec_p(gate), *common, tm=lp, last=last)
        re_l, im_l = _s5_cols_to_state(hfin.reshape(2, bp, 2 * S5_NC))
        new_re.append(jnp.swapaxes(re_l, 0, 1))
        new_im.append(jnp.swapaxes(im_l, 0, 1))
        new_gla.append(sfin)

        pe_l = pe if l == 0 else None
        u, q, k, v, gl = _proj_call(xs, pe_l, vec_s(shift), vec_s(scale1p), ng, w_proj[l], tm=512)
        h0 = _s5_state_to_cols(jnp.swapaxes(state_s5_re[:, l], 0, 1), jnp.swapaxes(state_s5_im[:, l], 0, 1))
        yf, yb, _ = _s5_call(u, w_bu[l], w_c[l], a_re[l], a_im[l], h0.reshape(2 * bs, 2 * S5_NC), tc=32,
                             tiles_per_pass=8, unroll=4)
        o, _ = _gla_call(q, k, v, gl, wg_up[l], bg, gng, state_gla[:, l])
        xs = _mix_call(xs, pe_l, yf, yb, u, o, vec_s(shift), vec_s(scale1p), vec_s(gate), *common, tm=512, last=last)

    return (xp, xs, jnp.stack(new_re, axis=1), jnp.stack(new_im, axis=1), jnp.stack(new_gla, axis=1))
```

```python
import functools
import math

import numpy as np
import jax
import jax.numpy as jnp
from jax import lax
from jax.experimental import pallas as pl
from jax.experimental.pallas import tpu as pltpu

F32 = jnp.float32
BF16 = jnp.bfloat16

D_MODEL = 1024
GRID_W = 64
S5_WIDTH = 512
S5_GROUPS = 32
S5_CH = 16
S5_STATE = 64
S5_NC = S5_GROUPS * S5_STATE
S5_SLABS = 4
GLA_HEADS = 4
GLA_DK = 64
GLA_DV = 128
GLA_KEY = GLA_HEADS * GLA_DK
GLA_WIDTH = GLA_HEADS * GLA_DV
GLA_RANK = 16
GLA_GATE_NORM = 16.0
GLA_CHUNK = 128
EPS = 1e-6
LANES = 128
PROJ_W = S5_WIDTH + 2 * GLA_KEY + GLA_WIDTH + LANES
GATE_W = 2 * S5_WIDTH + 2 * D_MODEL
VMEM_LIMIT = 52 * 1024 * 1024


def _bdot(a, b):
    return jnp.dot(a.astype(BF16), b.astype(BF16), preferred_element_type=F32)


def _split2(x):
    hi = x.astype(BF16)
    lo = (x - hi.astype(F32)).astype(BF16)
    return hi, lo


def _sigmoid(x):
    return 1.0 / (1.0 + jnp.exp(-x))


def _silu(x):
    return x * _sigmoid(x)


def _gelu_tanh(x):
    c = math.sqrt(2.0 / math.pi)
    return 0.5 * x * (1.0 + jnp.tanh(c * (x + 0.044715 * (x * x * x))))


def _log_sigmoid(x):
    return jnp.minimum(x, 0.0) - jnp.log(1.0 + jnp.exp(-jnp.abs(x)))


def _modulated_norm(x, ng, scale1p, shift):
    ms = jnp.mean(x * x, axis=-1, keepdims=True)
    return (x * lax.rsqrt(ms + EPS) * ng) * scale1p + shift


def _mod_kernel(cond_ref, w_ref, b_ref, out_ref):
    c = cond_ref[...]
    s_hi, s_lo = _split2(_silu(c))
    w_hi, w_lo = _split2(w_ref[0])
    acc = jnp.dot(s_hi, w_hi, preferred_element_type=F32)
    acc += jnp.dot(s_lo, w_hi, preferred_element_type=F32)
    acc += jnp.dot(s_hi, w_lo, preferred_element_type=F32)
    out_ref[0] = acc + b_ref[0]


def _mod_call(cond8, w_mod, b_mod):
    depth = w_mod.shape[0]
    nb = 768
    return pl.pallas_call(
        _mod_kernel,
        grid=(depth, 3 * D_MODEL // nb),
        in_specs=[
            pl.BlockSpec((8, D_MODEL), lambda l, j: (0, 0)),
            pl.BlockSpec((1, D_MODEL, nb), lambda l, j: (l, 0, j)),
            pl.BlockSpec((1, 1, nb), lambda l, j: (l, 0, j)),
        ],
        out_specs=pl.BlockSpec((1, 8, nb), lambda l, j: (l, 0, j)),
        out_shape=jax.ShapeDtypeStruct((depth, 8, 3 * D_MODEL), F32),
        compiler_params=pltpu.CompilerParams(dimension_semantics=("arbitrary", "arbitrary")),
        name="adaln_mod",
    )(cond8, w_mod, b_mod.reshape(depth, 1, 3 * D_MODEL))


def _zoh_kernel(lre_ref, lim_ref, ldt_ref, bre_ref, bim_ref, are_ref, aim_ref, bbre_ref, bbim_ref):
    lre = lre_ref[...]
    lim = lim_ref[...]
    dt = jnp.exp(ldt_ref[...])
    mag = jnp.exp(lre * dt)
    a_re = mag * jnp.cos(lim * dt)
    a_im = mag * jnp.sin(lim * dt)
    n_re = a_re - 1.0
    inv = 1.0 / (lre * lre + lim * lim)
    c_re = (n_re * lre + a_im * lim) * inv
    c_im = (a_im * lre - n_re * lim) * inv
    bre = bre_ref[...]
    bim = bim_ref[...]
    are_ref[...] = a_re
    aim_ref[...] = a_im
    bbre_ref[...] = c_re * bre - c_im * bim
    bbim_ref[...] = c_re * bim + c_im * bre


def _zoh_call(lre, lim, ldt, bre, bim):
    shp = jax.ShapeDtypeStruct(lre.shape, F32)
    return pl.pallas_call(
        _zoh_kernel,
        out_shape=(shp, shp, shp, shp),
        name="s5_zoh",
    )(lre, lim, ldt, bre, bim)


def _s5_params(s5_lam_re, s5_lam_im, s5_log_dt, s5_b_re, s5_b_im, s5_c_re, s5_c_im):
    depth = s5_lam_re.shape[0]
    full = (depth, 2, S5_GROUPS, S5_CH, S5_STATE)
    expand = lambda a: jnp.broadcast_to(a, full).reshape(-1, S5_STATE)
    lre = expand(s5_lam_re[:, :, :, None, :])
    lim = expand(s5_lam_im[:, :, :, None, :])
    ldt = expand(s5_log_dt[:, :, :, None, None])
    bre = expand(jnp.swapaxes(s5_b_re, -1, -2)[:, None])
    bim = expand(jnp.swapaxes(s5_b_im, -1, -2)[:, None])
    are, aim, bbre, bbim = [a.reshape(full) for a in _zoh_call(lre, lim, ldt, bre, bim)]
    a_re = are[:, :, :, 0, :].reshape(depth, 2, S5_NC)
    a_im = aim[:, :, :, 0, :].reshape(depth, 2, S5_NC)
    eye = jnp.eye(8, dtype=F32)
    t = jnp.stack([bbre, bbim], axis=-2)
    t = t.reshape(depth, 2, S5_SLABS, 8, S5_CH, 2, S5_STATE)
    t = jnp.transpose(t, (0, 2, 1, 3, 4, 5, 6))
    w_bu = t[:, :, :, :, :, :, None, :] * eye[None, None, None, :, None, None, :, None]
    w_bu = w_bu.reshape(depth, S5_SLABS, 2 * LANES, 2 * 8 * S5_STATE).astype(BF16)
    v = jnp.stack([s5_c_re, -s5_c_im], axis=1)
    v = v.reshape(depth, 2, S5_SLABS, 8, S5_CH, S5_STATE)
    v = jnp.transpose(v, (0, 2, 1, 3, 5, 4))
    w_c = v[:, :, :, :, :, None, :] * eye[None, None, None, :, None, :, None]
    w_c = w_c.reshape(depth, S5_SLABS, 2 * 8 * S5_STATE, LANES).astype(BF16)
    return a_re, a_im, w_bu, w_c


def _s5_state_to_cols(re, im):
    lead = re.shape[:-2]
    st = jnp.stack([re.reshape(lead + (S5_SLABS, 512)), im.reshape(lead + (S5_SLABS, 512))], axis=-2)
    return st.reshape(lead + (2 * S5_NC,))


def _s5_cols_to_state(cols):
    lead = cols.shape[:-1]
    st = cols.reshape(lead + (S5_SLABS, 2, 512))
    re = st[..., 0, :].reshape(lead + (S5_GROUPS, S5_STATE))
    im = st[..., 1, :].reshape(lead + (S5_GROUPS, S5_STATE))
    return re, im


def _proj_kernel(*refs, has_pe):
    if has_pe:
        x_ref, pe_ref, shift_ref, scale_ref, ng_ref, w_ref, u_ref, q_ref, k_ref, v_ref, gl_ref = refs
        x = x_ref[0] + pe_ref[...]
    else:
        x_ref, shift_ref, scale_ref, ng_ref, w_ref, u_ref, q_ref, k_ref, v_ref, gl_ref = refs
        x = x_ref[0]
    h = _modulated_norm(x, ng_ref[...], scale_ref[0], shift_ref[0])
    p = _bdot(h, w_ref[...])
    o = 0
    u_ref[0] = p[:, o:o + S5_WIDTH]
    o += S5_WIDTH
    q_ref[0] = p[:, o:o + GLA_KEY]
    o += GLA_KEY
    k_ref[0] = p[:, o:o + GLA_KEY]
    o += GLA_KEY
    v_ref[0] = p[:, o:o + GLA_WIDTH].astype(BF16)
    o += GLA_WIDTH
    gl_ref[0] = p[:, o:o + LANES]


def _proj_call(x, pe, shift, scale1p, ng, w, tm):
    nseq, length, _ = x.shape
    row = lambda width: pl.BlockSpec((1, tm, width), lambda b, j: (b, j, 0))
    vec = pl.BlockSpec((1, 1, D_MODEL), lambda b, j: (b, 0, 0))
    in_specs = [row(D_MODEL)]
    args = [x]
    if pe is not None:
        in_specs.append(pl.BlockSpec((tm, D_MODEL), lambda b, j: (j, 0)))
        args.append(pe)
    in_specs += [vec, vec, pl.BlockSpec((1, D_MODEL), lambda b, j: (0, 0)),
                 pl.BlockSpec((D_MODEL, PROJ_W), lambda b, j: (0, 0))]
    args += [shift, scale1p, ng, w]
    sds = lambda width, dt: jax.ShapeDtypeStruct((nseq, length, width), dt)
    return pl.pallas_call(
        functools.partial(_proj_kernel, has_pe=pe is not None),
        grid=(nseq, length // tm),
        in_specs=in_specs,
        out_specs=(row(S5_WIDTH), row(GLA_KEY), row(GLA_KEY), row(GLA_WIDTH), row(LANES)),
        out_shape=(sds(S5_WIDTH, F32), sds(GLA_KEY, F32), sds(GLA_KEY, F32), sds(GLA_WIDTH, BF16), sds(LANES, F32)),
        compiler_params=pltpu.CompilerParams(dimension_semantics=("arbitrary", "arbitrary"),
                                             vmem_limit_bytes=VMEM_LIMIT),
        name="proj",
    )(*args)


def _s5_perms(nb, tc):
    r = nb * tc
    pf = np.zeros((2 * r, r), np.float32)
    pb = np.zeros((2 * r, r), np.float32)
    for t in range(tc):
        for b in range(nb):
            pf[t * 2 * nb + b, b * tc + t] = 1.0
            pb[t * 2 * nb + nb + b, b * tc + (tc - 1 - t)] = 1.0
    return pf, pb


def _s5_kernel(uf_ref, ub_ref, pf_ref, pb_ref, pft_ref, pbt_ref, wbu_ref, wc_ref, are_ref, aim_ref, h0_ref,
               yf_ref, yb_ref, hfin_ref, bu_ref, st_ref, *, nb, tc, tiles_per_pass, unroll):
    i = pl.program_id(0)
    rows = nb * tc
    rs = 2 * nb

    @pl.when(i == 0)
    def _():
        st_ref[...] = h0_ref[...]

    uf = uf_ref[...].reshape(rows, S5_WIDTH).astype(BF16)
    ub = ub_ref[...].reshape(rows, S5_WIDTH).astype(BF16)
    u2f = jnp.dot(pf_ref[...], uf, preferred_element_type=F32).astype(BF16)
    u2b = jnp.dot(pb_ref[...], ub, preferred_element_type=F32).astype(BF16)
    for s in range(S5_SLABS):
        lhs = jnp.concatenate([u2f[:, s * LANES:(s + 1) * LANES], u2b[:, s * LANES:(s + 1) * LANES]], axis=1)
        bu_ref[:, s * 1024:(s + 1) * 1024] = jnp.dot(lhs, wbu_ref[s], preferred_element_type=F32)

    n_ct = S5_NC // LANES
    for c0 in range(0, n_ct, tiles_per_pass):
        cts = list(range(c0, c0 + tiles_per_pass))
        cre = [(ct // 4) * 1024 + (ct % 4) * LANES for ct in cts]
        cim = [c + 512 for c in cre]
        a_re = [are_ref[:, ct * LANES:(ct + 1) * LANES] for ct in cts]
        a_im = [aim_ref[:, ct * LANES:(ct + 1) * LANES] for ct in cts]
        n = len(cts)
        init = tuple(st_ref[:, c:c + LANES] for c in cre) + tuple(st_ref[:, c:c + LANES] for c in cim)

        def step(t, carry, cre=cre, cim=cim, a_re=a_re, a_im=a_im, n=n):
            r0 = pl.multiple_of(t * rs, rs)
            new_re, new_im = [], []
            for j in range(n):
                h_re, h_im = carry[j], carry[n + j]
                b_re = bu_ref[pl.ds(r0, rs), cre[j]:cre[j] + LANES]
                b_im = bu_ref[pl.ds(r0, rs), cim[j]:cim[j] + LANES]
                n_re = a_re[j] * h_re - a_im[j] * h_im + b_re
                n_im = a_re[j] * h_im + a_im[j] * h_re + b_im
                bu_ref[pl.ds(r0, rs), cre[j]:cre[j] + LANES] = n_re
                bu_ref[pl.ds(r0, rs), cim[j]:cim[j] + LANES] = n_im
                new_re.append(n_re)
                new_im.append(n_im)
            return tuple(new_re) + tuple(new_im)

        fin = lax.fori_loop(0, tc, step, init, unroll=unroll)
        for j in range(n):
            st_ref[:, cre[j]:cre[j] + LANES] = fin[j]
            st_ref[:, cim[j]:cim[j] + LANES] = fin[n + j]

    ys = [jnp.dot(bu_ref[:, s * 1024:(s + 1) * 1024].astype(BF16), wc_ref[s], preferred_element_type=F32)
          for s in range(S5_SLABS)]
    y2 = jnp.concatenate(ys, axis=1)
    y_hi, y_lo = _split2(y2)
    yf = jnp.dot(pft_ref[...], y_hi, preferred_element_type=F32) + jnp.dot(pft_ref[...], y_lo, preferred_element_type=F32)
    yb = jnp.dot(pbt_ref[...], y_hi, preferred_element_type=F32) + jnp.dot(pbt_ref[...], y_lo, preferred_element_type=F32)
    yf_ref[...] = yf.reshape(nb, tc, S5_WIDTH)
    yb_ref[...] = yb.reshape(nb, tc, S5_WIDTH)
    hfin_ref[...] = st_ref[...]


def _s5_call(u, w_bu, w_c, a_re, a_im, h0, tc, tiles_per_pass, unroll):
    nb, length, _ = u.shape
    n = length // tc
    rows = nb * tc
    pf, pb = _s5_perms(nb, tc)
    are = jnp.repeat(a_re, nb, axis=0)
    aim = jnp.repeat(a_im, nb, axis=0)
    const = lambda shape: pl.BlockSpec(shape, lambda i: (0,) * len(shape))
    blk_f = pl.BlockSpec((nb, tc, S5_WIDTH), lambda i: (0, i, 0))
    blk_b = pl.BlockSpec((nb, tc, S5_WIDTH), lambda i: (0, n - 1 - i, 0))
    return pl.pallas_call(
        functools.partial(_s5_kernel, nb=nb, tc=tc, tiles_per_pass=tiles_per_pass, unroll=unroll),
        grid=(n,),
        in_specs=[blk_f, blk_b,
                  const((2 * rows, rows)), const((2 * rows, rows)), const((rows, 2 * rows)), const((rows, 2 * rows)),
                  const((S5_SLABS, 2 * LANES, 1024)), const((S5_SLABS, 1024, LANES)),
                  const((2 * nb, S5_NC)), const((2 * nb, S5_NC)), const((2 * nb, 2 * S5_NC))],
        out_specs=(blk_f, blk_b, const((2 * nb, 2 * S5_NC))),
        out_shape=(jax.ShapeDtypeStruct(u.shape, F32), jax.ShapeDtypeStruct(u.shape, F32),
                   jax.ShapeDtypeStruct((2 * nb, 2 * S5_NC), F32)),
        scratch_shapes=[pltpu.VMEM((2 * rows, 2 * S5_NC), F32), pltpu.VMEM((2 * nb, 2 * S5_NC), F32)],
        compiler_params=pltpu.CompilerParams(dimension_semantics=("arbitrary",), vmem_limit_bytes=VMEM_LIMIT),
        name="s5_scan",
    )(u, u, jnp.asarray(pf, BF16), jnp.asarray(pb, BF16), jnp.asarray(pf.T, BF16), jnp.asarray(pb.T, BF16),
      w_bu, w_c, are, aim, h0)


def _gla_kernel(*refs, length, has_s0):
    if has_s0:
        q_ref, k_ref, v_ref, gl_ref, wg_ref, bg_ref, ng_ref, s0_ref, o_ref, sfin_ref, s_ref = refs
    else:
        q_ref, k_ref, v_ref, gl_ref, wg_ref, bg_ref, ng_ref, o_ref, sfin_ref, s_ref = refs
    c = GLA_CHUNK
    n = length // c
    ri = lax.broadcasted_iota(jnp.int32, (c, c), 0)
    ci = lax.broadcasted_iota(jnp.int32, (c, c), 1)
    lane = lax.broadcasted_iota(jnp.int32, (1, GLA_KEY), 1)
    head_mask = [((lane >= h * GLA_DK) & (lane < (h + 1) * GLA_DK)).astype(F32) for h in range(GLA_HEADS)]
    srow = lax.shift_right_logical(lax.broadcasted_iota(jnp.int32, (GLA_KEY, GLA_WIDTH), 0), 6)
    scol = lax.shift_right_logical(lax.broadcasted_iota(jnp.int32, (GLA_KEY, GLA_WIDTH), 1), 7)
    bd_mask = (srow == scol).astype(F32)
    ng = ng_ref[...]
    causal = [ci <= ri, ci >= ri]
    tri = [m.astype(BF16) for m in causal]

    s_ref[...] = jnp.zeros_like(s_ref)
    if has_s0:
        for d in range(2):
            for h in range(GLA_HEADS):
                s_ref[d, h * GLA_DK:(h + 1) * GLA_DK, h * GLA_DV:(h + 1) * GLA_DV] = s0_ref[0, d, h]

    def chunk_pair(r):
        dirs = (0, 1)
        end = (c - 1, 0)
        qc = [q_ref[0, pl.ds(r[d], c), :] * (GLA_DK ** -0.5) for d in dirs]
        kc = [k_ref[0, pl.ds(r[d], c), :] for d in dirs]
        vc = [v_ref[0, pl.ds(r[d], c), :] for d in dirs]
        logits = [_bdot(gl_ref[0, pl.ds(r[d], c), :], wg_ref[d]) + bg_ref[d] for d in dirs]
        g = [_split2(_log_sigmoid(logits[d]) * (1.0 / GLA_GATE_NORM)) for d in dirs]
        b = [jnp.dot(tri[d], g[d][0], preferred_element_type=F32) + jnp.dot(tri[d], g[d][1], preferred_element_type=F32)
             for d in dirs]
        bt = [b[d].T for d in dirs]
        kt = [kc[d].T for d in dirs]
        b_mid = [b[d][c // 2:c // 2 + 1, :] for d in dirs]
        bt_mid = [bt[d][:, c // 2:c // 2 + 1] for d in dirs]
        bt_end = [bt[d][:, end[d]:end[d] + 1] for d in dirs]
        q_in = [(qc[d] * jnp.exp(b[d])).astype(BF16) for d in dirs]
        q_mid = [qc[d] * jnp.exp(b[d] - b_mid[d]) for d in dirs]
        k_mid_t = [(kt[d] * jnp.exp(bt_mid[d] - bt[d])).astype(BF16) for d in dirs]
        k_end_t = [(kt[d] * jnp.exp(bt_end[d] - bt[d])).astype(BF16) for d in dirs]
        q_st = [jnp.concatenate([q_mid[d] * head_mask[h] for h in range(GLA_HEADS)], axis=0).astype(BF16) for d in dirs]
        a = [jnp.dot(q_st[d], k_mid_t[d], preferred_element_type=F32) for d in dirs]
        s_old = [s_ref[d] for d in dirs]
        o_inter = [jnp.dot(q_in[d], s_old[d].astype(BF16), preferred_element_type=F32) for d in dirs]
        parts = [[], []]
        for h in range(GLA_HEADS):
            for d in dirs:
                a_h = jnp.where(causal[d], a[d][h * c:(h + 1) * c, :], 0.0).astype(BF16)
                parts[d].append(jnp.dot(a_h, vc[d][:, h * GLA_DV:(h + 1) * GLA_DV], preferred_element_type=F32))
        upd = [jnp.dot(k_end_t[d], vc[d], preferred_element_type=F32) for d in dirs]
        for d in dirs:
            s_ref[d] = jnp.exp(bt_end[d]) * s_old[d] + upd[d] * bd_mask
        return [o_inter[d] + jnp.concatenate(parts[d], axis=1) for d in dirs]

    def finish(o, r0):
        o = o + o_ref[0, pl.ds(r0, c), :]
        normed = []
        for h in range(GLA_HEADS):
            oh = o[:, h * GLA_DV:(h + 1) * GLA_DV]
            ms = jnp.mean(oh * oh, axis=-1, keepdims=True)
            normed.append(oh * lax.rsqrt(ms + EPS))
        o_ref[0, pl.ds(r0, c), :] = jnp.concatenate(normed, axis=1) * ng

    def first_half(step, carry):
        rf = pl.multiple_of(step * c, c)
        rb = pl.multiple_of((n - 1 - step) * c, c)
        o_f, o_b = chunk_pair((rf, rb))
        o_ref[0, pl.ds(rf, c), :] = o_f
        o_ref[0, pl.ds(rb, c), :] = o_b
        return carry

    def second_half(step, carry):
        rf = pl.multiple_of(step * c, c)
        rb = pl.multiple_of((n - 1 - step) * c, c)
        o_f, o_b = chunk_pair((rf, rb))
        finish(o_f, rf)
        finish(o_b, rb)
        return carry

    lax.fori_loop(0, n // 2, first_half, 0)
    lax.fori_loop(n // 2, n, second_half, 0)
    for d in range(2):
        for h in range(GLA_HEADS):
            sfin_ref[0, d, h] = s_ref[d, h * GLA_DK:(h + 1) * GLA_DK, h * GLA_DV:(h + 1) * GLA_DV]


def _gla_call(q, k, v, gl, wg, bg, ng, s0):
    nseq, length, _ = q.shape
    seq = lambda width: pl.BlockSpec((1, length, width), lambda b: (b, 0, 0))
    const = lambda shape: pl.BlockSpec(shape, lambda b: (0,) * len(shape))
    st = pl.BlockSpec((1, 2, GLA_HEADS, GLA_DK, GLA_DV), lambda b: (b, 0, 0, 0, 0))
    in_specs = [seq(GLA_KEY), seq(GLA_KEY), seq(GLA_WIDTH), seq(LANES),
                const((2, LANES, GLA_KEY)), const((2, 1, GLA_KEY)), const((1, GLA_WIDTH))]
    args = [q, k, v, gl, wg, bg, ng]
    if s0 is not None:
        in_specs.append(st)
        args.append(s0)
    return pl.pallas_call(
        functools.partial(_gla_kernel, length=length, has_s0=s0 is not None),
        grid=(nseq,),
        in_specs=in_specs,
        out_specs=(seq(GLA_WIDTH), st),
        out_shape=(jax.ShapeDtypeStruct((nseq, length, GLA_WIDTH), F32),
                   jax.ShapeDtypeStruct((nseq, 2, GLA_HEADS, GLA_DK, GLA_DV), F32)),
        scratch_shapes=[pltpu.VMEM((2, GLA_KEY, GLA_WIDTH), F32)],
        compiler_params=pltpu.CompilerParams(dimension_semantics=("arbitrary",), vmem_limit_bytes=VMEM_LIMIT),
        name="gla",
    )(*args)


def _mix_kernel(*refs, has_pe, last):
    if has_pe:
        (x_ref, pe_ref, yf_ref, yb_ref, u_ref, o_ref, shift_ref, scale_ref, gate_ref, ng_ref, wg_ref, d_ref,
         wglu_ref, bglu_ref, wpa_ref, wpb_ref, wo_ref, fng_ref, out_ref) = refs
        x = x_ref[0] + pe_ref[...]
    else:
        (x_ref, yf_ref, yb_ref, u_ref, o_ref, shift_ref, scale_ref, gate_ref, ng_ref, wg_ref, d_ref,
         wglu_ref, bglu_ref, wpa_ref, wpb_ref, wo_ref, fng_ref, out_ref) = refs
        x = x_ref[0]
    h = _modulated_norm(x, ng_ref[...], scale_ref[0], shift_ref[0])
    gts = _bdot(h, wg_ref[...])
    gate_a = gts[:, :S5_WIDTH]
    gate_b = gts[:, S5_WIDTH:2 * S5_WIDTH]
    m_a = gts[:, 2 * S5_WIDTH:2 * S5_WIDTH + D_MODEL]
    m_b = gts[:, 2 * S5_WIDTH + D_MODEL:]
    y = _gelu_tanh(yf_ref[0] + yb_ref[0] + d_ref[...] * u_ref[0])
    y = y * _sigmoid(_bdot(y, wglu_ref[...]) + bglu_ref[...])
    y_a = y * _silu(gate_a)
    y_b = o_ref[0] * _silu(gate_b)
    merged = _sigmoid(m_a) * _bdot(y_a, wpa_ref[...]) + _sigmoid(m_b) * _bdot(y_b, wpb_ref[...])
    xn = x + gate_ref[0] * _bdot(merged, wo_ref[...])
    if last:
        ms = jnp.mean(xn * xn, axis=-1, keepdims=True)
        xn = xn * lax.rsqrt(ms + EPS) * fng_ref[...]
    out_ref[0] = xn


def _mix_call(x, pe, yf, yb, u, o, shift, scale1p, gate, ng, wg, s5_d, wglu, bglu, wpa, wpb, wo, fng, tm, last):
    nseq, length, _ = x.shape
    row = lambda width: pl.BlockSpec((1, tm, width), lambda b, j: (b, j, 0))
    vec = pl.BlockSpec((1, 1, D_MODEL), lambda b, j: (b, 0, 0))
    const = lambda shape: pl.BlockSpec(shape, lambda b, j: (0,) * len(shape))
    in_specs = [row(D_MODEL)]
    args = [x]
    if pe is not None:
        in_specs.append(pl.BlockSpec((tm, D_MODEL), lambda b, j: (j, 0)))
        args.append(pe)
    in_specs += [row(S5_WIDTH), row(S5_WIDTH), row(S5_WIDTH), row(GLA_WIDTH), vec, vec, vec,
                 const((1, D_MODEL)), const((D_MODEL, GATE_W)), const((1, S5_WIDTH)),
                 const((S5_WIDTH, S5_WIDTH)), const((1, S5_WIDTH)), const((S5_WIDTH, D_MODEL)),
                 const((GLA_WIDTH, D_MODEL)), const((D_MODEL, D_MODEL)), const((1, D_MODEL))]
    args += [yf, yb, u, o, shift, scale1p, gate, ng, wg, s5_d, wglu, bglu, wpa, wpb, wo, fng]
    return pl.pallas_call(
        functools.partial(_mix_kernel, has_pe=pe is not None, last=last),
        grid=(nseq, length // tm),
        in_specs=in_specs,
        out_specs=row(D_MODEL),
        out_shape=jax.ShapeDtypeStruct(x.shape, F32),
        compiler_params=pltpu.CompilerParams(dimension_semantics=("arbitrary", "arbitrary"),
                                             vmem_limit_bytes=VMEM_LIMIT),
        name="mix",
    )(*args)


def _grid_pos_embed(length, dim):
    rows = length // GRID_W
    quarter = dim // 4
    freqs = jnp.exp(-math.log(10000.0) * jnp.arange(quarter, dtype=F32) / quarter)

    def sincos(pos):
        ang = pos.astype(F32)[:, None] * freqs[None, :]
        return jnp.concatenate([jnp.sin(ang), jnp.cos(ang)], axis=-1)

    er = sincos(jnp.arange(rows))
    ec = sincos(jnp.arange(GRID_W))
    pe = jnp.concatenate([jnp.broadcast_to(er[:, None, :], (rows, GRID_W, dim // 2)),
                          jnp.broadcast_to(ec[None, :, :], (rows, GRID_W, dim // 2))], axis=-1)
    return pe.reshape(rows * GRID_W, dim)


def kernel(x_prompt, x_sample, c, state_s5_re, state_s5_im, state_gla, c_ctx, norm_g, w_mod, b_mod, w_in, gla_wg_up,
           gla_bg, gla_norm_g, s5_lam_re, s5_lam_im, s5_log_dt, s5_b_re, s5_b_im, s5_c_re, s5_c_im, s5_d, w_glu,
           b_glu, w_pa, w_pb, w_o, final_norm_g):
    depth = w_in.shape[0]
    bp, lp, _ = x_prompt.shape
    bs, ls, _ = x_sample.shape

    cond8 = jnp.zeros((8, D_MODEL), F32).at[0].set(c_ctx).at[1:1 + bs].set(c)
    mod = _mod_call(cond8, w_mod, b_mod)

    a_re, a_im, w_bu, w_c = _s5_params(s5_lam_re, s5_lam_im, s5_log_dt, s5_b_re, s5_b_im, s5_c_re, s5_c_im)

    offs = np.cumsum([0, S5_WIDTH, S5_WIDTH, GLA_KEY, GLA_KEY, GLA_WIDTH, GLA_WIDTH, GLA_RANK, D_MODEL, D_MODEL])
    col = lambda i: slice(int(offs[i]), int(offs[i + 1]))
    pad = jnp.zeros((depth, D_MODEL, LANES - GLA_RANK), F32)
    w_proj = jnp.concatenate([w_in[:, :, col(0)], w_in[:, :, col(2)], w_in[:, :, col(3)], w_in[:, :, col(4)],
                              w_in[:, :, col(6)], pad], axis=-1).astype(BF16)
    w_gate = jnp.concatenate([w_in[:, :, col(1)], w_in[:, :, col(5)], w_in[:, :, col(7)], w_in[:, :, col(8)]],
                             axis=-1).astype(BF16)
    wg_up = jnp.concatenate([gla_wg_up, jnp.zeros((depth, 2, LANES - GLA_RANK, GLA_KEY), F32)], axis=2).astype(BF16)
    w_glu_b, w_pa_b, w_pb_b, w_o_b = (w.astype(BF16) for w in (w_glu, w_pa, w_pb, w_o))

    pe = _grid_pos_embed(ls, D_MODEL)
    zero_h0 = jnp.zeros((2 * bp, 2 * S5_NC), F32)
    xp, xs = x_prompt, x_sample
    new_re, new_im, new_gla = [], [], []
    for l in range(depth):
        last = l == depth - 1
        m = mod[l]
        shift, scale1p, gate = m[:, :D_MODEL], 1.0 + m[:, D_MODEL:2 * D_MODEL], m[:, 2 * D_MODEL:]
        vec_p = lambda a: jnp.broadcast_to(a[0][None, None, :], (bp, 1, D_MODEL))
        vec_s = lambda a: a[1:1 + bs][:, None, :]
        ng = norm_g[l][None]
        gng = gla_norm_g[l][None]
        bg = gla_bg[l][:, None, :]
        common = (ng, w_gate[l], s5_d[l][None], w_glu_b[l], b_glu[l][None], w_pa_b[l], w_pb_b[l], w_o_b[l],
                  final_norm_g[None])

        u, q, k, v, gl = _proj_call(xp, None, vec_p(shift), vec_p(scale1p), ng, w_proj[l], tm=lp)
        yf, yb, hfin = _s5_call(u, w_bu[l], w_c[l], a_re[l], a_im[l], zero_h0, tc=8, tiles_per_pass=2, unroll=True)
        o, sfin = _gla_call(q, k, v, gl, wg_up[l], bg, gng, None)
        xp = _mix_call(xp, None, yf, yb, u, o, vec_p(shift), vec_p(scale1p), vec_p(gate), *common, tm=lp, last=last)
        re_l, im_l = _s5_cols_to_state(hfin.reshape(2, bp, 2 * S5_NC))
        new_re.append(jnp.swapaxes(re_l, 0, 1))
        new_im.append(jnp.swapaxes(im_l, 0, 1))
        new_gla.append(sfin)

        pe_l = pe if l == 0 else None
        u, q, k, v, gl = _proj_call(xs, pe_l, vec_s(shift), vec_s(scale1p), ng, w_proj[l], tm=512)
        h0 = _s5_state_to_cols(jnp.swapaxes(state_s5_re[:, l], 0, 1), jnp.swapaxes(state_s5_im[:, l], 0, 1))
        yf, yb, _ = _s5_call(u, w_bu[l], w_c[l], a_re[l], a_im[l], h0.reshape(2 * bs, 2 * S5_NC), tc=32,
                             tiles_per_pass=8, unroll=4)
        o, _ = _gla_call(q, k, v, gl, wg_up[l], bg, gng, state_gla[:, l])
        xs = _mix_call(xs, pe_l, yf, yb, u, o, vec_s(shift), vec_s(scale1p), vec_s(gate), *common, tm=512, last=last)

    return (xp, xs, jnp.stack(new_re, axis=1), jnp.stack(new_im, axis=1), jnp.stack(new_gla, axis=1))
```

```python
import functools
import math

import numpy as np
import jax
import jax.numpy as jnp
from jax import lax
from jax.experimental import pallas as pl
from jax.experimental.pallas import tpu as pltpu

F32 = jnp.float32
BF16 = jnp.bfloat16

D_MODEL = 1024
GRID_W = 64
S5_WIDTH = 512
S5_GROUPS = 32
S5_CH = 16
S5_STATE = 64
S5_NC = S5_GROUPS * S5_STATE
S5_SLABS = 4
GLA_HEADS = 4
GLA_DK = 64
GLA_DV = 128
GLA_KEY = GLA_HEADS * GLA_DK
GLA_WIDTH = GLA_HEADS * GLA_DV
GLA_RANK = 16
GLA_GATE_NORM = 16.0
GLA_CHUNK = 128
EPS = 1e-6
LANES = 128
PROJ_W = S5_WIDTH + 2 * GLA_KEY + GLA_WIDTH + LANES
GATE_W = 2 * S5_WIDTH + 2 * D_MODEL
VMEM_LIMIT = 52 * 1024 * 1024


def _bdot(a, b):
    return jnp.dot(a.astype(BF16), b.astype(BF16), preferred_element_type=F32)


def _split2(x):
    hi = x.astype(BF16)
    lo = (x - hi.astype(F32)).astype(BF16)
    return hi, lo


def _sigmoid(x):
    return 1.0 / (1.0 + jnp.exp(-x))


def _silu(x):
    return x * _sigmoid(x)


def _gelu_tanh(x):
    c = math.sqrt(2.0 / math.pi)
    return 0.5 * x * (1.0 + jnp.tanh(c * (x + 0.044715 * (x * x * x))))


def _log_sigmoid(x):
    return jnp.minimum(x, 0.0) - jnp.log(1.0 + jnp.exp(-jnp.abs(x)))


def _modulated_norm(x, ng, scale1p, shift):
    ms = jnp.mean(x * x, axis=-1, keepdims=True)
    return (x * lax.rsqrt(ms + EPS) * ng) * scale1p + shift


def _mod_kernel(cond_ref, w_ref, b_ref, out_ref):
    c = cond_ref[...]
    s_hi, s_lo = _split2(_silu(c))
    w_hi, w_lo = _split2(w_ref[0])
    acc = jnp.dot(s_hi, w_hi, preferred_element_type=F32)
    acc += jnp.dot(s_lo, w_hi, preferred_element_type=F32)
    acc += jnp.dot(s_hi, w_lo, preferred_element_type=F32)
    out_ref[0] = acc + b_ref[0]


def _mod_call(cond8, w_mod, b_mod):
    depth = w_mod.shape[0]
    nb = 768
    return pl.pallas_call(
        _mod_kernel,
        grid=(depth, 3 * D_MODEL // nb),
        in_specs=[
            pl.BlockSpec((8, D_MODEL), lambda l, j: (0, 0)),
            pl.BlockSpec((1, D_MODEL, nb), lambda l, j: (l, 0, j)),
            pl.BlockSpec((1, 1, nb), lambda l, j: (l, 0, j)),
        ],
        out_specs=pl.BlockSpec((1, 8, nb), lambda l, j: (l, 0, j)),
        out_shape=jax.ShapeDtypeStruct((depth, 8, 3 * D_MODEL), F32),
        compiler_params=pltpu.CompilerParams(dimension_semantics=("arbitrary", "arbitrary")),
        name="adaln_mod",
    )(cond8, w_mod, b_mod.reshape(depth, 1, 3 * D_MODEL))


def _zoh_kernel(lre_ref, lim_ref, ldt_ref, bre_ref, bim_ref, are_ref, aim_ref, bbre_ref, bbim_ref):
    lre = lre_ref[...]
    lim = lim_ref[...]
    dt = jnp.exp(ldt_ref[...])
    mag = jnp.exp(lre * dt)
    a_re = mag * jnp.cos(lim * dt)
    a_im = mag * jnp.sin(lim * dt)
    n_re = a_re - 1.0
    inv = 1.0 / (lre * lre + lim * lim)
    c_re = (n_re * lre + a_im * lim) * inv
    c_im = (a_im * lre - n_re * lim) * inv
    bre = bre_ref[...]
    bim = bim_ref[...]
    are_ref[...] = a_re
    aim_ref[...] = a_im
    bbre_ref[...] = c_re * bre - c_im * bim
    bbim_ref[...] = c_re * bim + c_im * bre


def _zoh_call(lre, lim, ldt, bre, bim):
    shp = jax.ShapeDtypeStruct(lre.shape, F32)
    return pl.pallas_call(
        _zoh_kernel,
        out_shape=(shp, shp, shp, shp),
        name="s5_zoh",
    )(lre, lim, ldt, bre, bim)


def _s5_params(s5_lam_re, s5_lam_im, s5_log_dt, s5_b_re, s5_b_im, s5_c_re, s5_c_im):
    depth = s5_lam_re.shape[0]
    full = (depth, 2, S5_GROUPS, S5_CH, S5_STATE)
    expand = lambda a: jnp.broadcast_to(a, full).reshape(-1, S5_STATE)
    lre = expand(s5_lam_re[:, :, :, None, :])
    lim = expand(s5_lam_im[:, :, :, None, :])
    ldt = expand(s5_log_dt[:, :, :, None, None])
    bre = expand(jnp.swapaxes(s5_b_re, -1, -2)[:, None])
    bim = expand(jnp.swapaxes(s5_b_im, -1, -2)[:, None])
    are, aim, bbre, bbim = [a.reshape(full) for a in _zoh_call(lre, lim, ldt, bre, bim)]
    a_re = are[:, :, :, 0, :].reshape(depth, 2, S5_NC)
    a_im = aim[:, :, :, 0, :].reshape(depth, 2, S5_NC)
    eye = jnp.eye(8, dtype=F32)
    t = jnp.stack([bbre, bbim], axis=-2)
    t = t.reshape(depth, 2, S5_SLABS, 8, S5_CH, 2, S5_STATE)
    t = jnp.transpose(t, (0, 2, 1, 3, 4, 5, 6))
    w_bu = t[:, :, :, :, :, :, None, :] * eye[None, None, None, :, None, None, :, None]
    w_bu = w_bu.reshape(depth, S5_SLABS, 2 * LANES, 2 * 8 * S5_STATE).astype(BF16)
    v = jnp.stack([s5_c_re, -s5_c_im], axis=1)
    v = v.reshape(depth, 2, S5_SLABS, 8, S5_CH, S5_STATE)
    v = jnp.transpose(v, (0, 2, 1, 3, 5, 4))
    w_c = v[:, :, :, :, :, None, :] * eye[None, None, None, :, None, :, None]
    w_c = w_c.reshape(depth, S5_SLABS, 2 * 8 * S5_STATE, LANES).astype(BF16)
    return a_re, a_im, w_bu, w_c


def _s5_state_to_cols(re, im):
    lead = re.shape[:-2]
    st = jnp.stack([re.reshape(lead + (S5_SLABS, 512)), im.reshape(lead + (S5_SLABS, 512))], axis=-2)
    return st.reshape(lead + (2 * S5_NC,))


def _s5_cols_to_state(cols):
    lead = cols.shape[:-1]
    st = cols.reshape(lead + (S5_SLABS, 2, 512))
    re = st[..., 0, :].reshape(lead + (S5_GROUPS, S5_STATE))
    im = st[..., 1, :].reshape(lead + (S5_GROUPS, S5_STATE))
    return re, im


def _proj_kernel(*refs, has_pe):
    if has_pe:
        x_ref, pe_ref, shift_ref, scale_ref, ng_ref, w_ref, u_ref, q_ref, k_ref, v_ref, gl_ref = refs
        x = x_ref[0] + pe_ref[...]
    else:
        x_ref, shift_ref, scale_ref, ng_ref, w_ref, u_ref, q_ref, k_ref, v_ref, gl_ref = refs
        x = x_ref[0]
    h = _modulated_norm(x, ng_ref[...], scale_ref[0], shift_ref[0])
    p = _bdot(h, w_ref[...])
    o = 0
    u_ref[0] = p[:, o:o + S5_WIDTH]
    o += S5_WIDTH
    q_ref[0] = p[:, o:o + GLA_KEY]
    o += GLA_KEY
    k_ref[0] = p[:, o:o + GLA_KEY]
    o += GLA_KEY
    v_ref[0] = p[:, o:o + GLA_WIDTH].astype(BF16)
    o += GLA_WIDTH
    gl_ref[0] = p[:, o:o + LANES]


def _proj_call(x, pe, shift, scale1p, ng, w, tm):
    nseq, length, _ = x.shape
    row = lambda width: pl.BlockSpec((1, tm, width), lambda b, j: (b, j, 0))
    vec = pl.BlockSpec((1, 1, D_MODEL), lambda b, j: (b, 0, 0))
    in_specs = [row(D_MODEL)]
    args = [x]
    if pe is not None:
        in_specs.append(pl.BlockSpec((tm, D_MODEL), lambda b, j: (j, 0)))
        args.append(pe)
    in_specs += [vec, vec, pl.BlockSpec((1, D_MODEL), lambda b, j: (0, 0)),
                 pl.BlockSpec((D_MODEL, PROJ_W), lambda b, j: (0, 0))]
    args += [shift, scale1p, ng, w]
    sds = lambda width, dt: jax.ShapeDtypeStruct((nseq, length, width), dt)
    return pl.pallas_call(
        functools.partial(_proj_kernel, has_pe=pe is not None),
        grid=(nseq, length // tm),
        in_specs=in_specs,
        out_specs=(row(S5_WIDTH), row(GLA_KEY), row(GLA_KEY), row(GLA_WIDTH), row(LANES)),
        out_shape=(sds(S5_WIDTH, F32), sds(GLA_KEY, F32), sds(GLA_KEY, F32), sds(GLA_WIDTH, BF16), sds(LANES, F32)),
        compiler_params=pltpu.CompilerParams(dimension_semantics=("arbitrary", "arbitrary"),
                                             vmem_limit_bytes=VMEM_LIMIT),
        name="proj",
    )(*args)


def _s5_perms(nb, tc):
    r = nb * tc
    pf = np.zeros((2 * r, r), np.float32)
    pb = np.zeros((2 * r, r), np.float32)
    for t in range(tc):
        for b in range(nb):
            pf[t * 2 * nb + b, b * tc + t] = 1.0
            pb[t * 2 * nb + nb + b, b * tc + (tc - 1 - t)] = 1.0
    return pf, pb


def _s5_kernel(uf_ref, ub_ref, pf_ref, pb_ref, pft_ref, pbt_ref, wbu_ref, wc_ref, are_ref, aim_ref, h0_ref,
               yf_ref, yb_ref, hfin_ref, bu_ref, hb_ref, st_ref, *, nb, tc, tiles_per_pass):
    i = pl.program_id(0)
    rows = nb * tc
    rs = 2 * nb

    @pl.when(i == 0)
    def _():
        st_ref[...] = h0_ref[...]

    uf = uf_ref[...].reshape(rows, S5_WIDTH).astype(BF16)
    ub = ub_ref[...].reshape(rows, S5_WIDTH).astype(BF16)
    u2f = jnp.dot(pf_ref[...], uf, preferred_element_type=F32).astype(BF16)
    u2b = jnp.dot(pb_ref[...], ub, preferred_element_type=F32).astype(BF16)
    for s in range(S5_SLABS):
        lhs = jnp.concatenate([u2f[:, s * LANES:(s + 1) * LANES], u2b[:, s * LANES:(s + 1) * LANES]], axis=1)
        bu_ref[:, s * 1024:(s + 1) * 1024] = jnp.dot(lhs, wbu_ref[s], preferred_element_type=F32)

    grp = max(1, 16 // rs)
    ys = []
    for s in range(S5_SLABS):
        for c0 in range(4 * s, 4 * s + 4, tiles_per_pass):
            cts = list(range(c0, c0 + tiles_per_pass))
            cre = [(ct // 4) * 1024 + (ct % 4) * LANES for ct in cts]
            cim = [c + 512 for c in cre]
            a_re = [are_ref[:, ct * LANES:(ct + 1) * LANES] for ct in cts]
            a_im = [aim_ref[:, ct * LANES:(ct + 1) * LANES] for ct in cts]
            h_re = [st_ref[:, c:c + LANES] for c in cre]
            h_im = [st_ref[:, c:c + LANES] for c in cim]
            for t0 in range(0, tc, grp):
                out_re = [[] for _ in cts]
                out_im = [[] for _ in cts]
                for t in range(t0, t0 + grp):
                    r0 = t * rs
                    for j in range(len(cts)):
                        b_re = bu_ref[r0:r0 + rs, cre[j]:cre[j] + LANES]
                        b_im = bu_ref[r0:r0 + rs, cim[j]:cim[j] + LANES]
                        n_re = a_re[j] * h_re[j] - a_im[j] * h_im[j] + b_re
                        n_im = a_re[j] * h_im[j] + a_im[j] * h_re[j] + b_im
                        h_re[j], h_im[j] = n_re, n_im
                        out_re[j].append(n_re)
                        out_im[j].append(n_im)
                for j in range(len(cts)):
                    blk_re = out_re[j][0] if grp == 1 else jnp.concatenate(out_re[j], axis=0)
                    blk_im = out_im[j][0] if grp == 1 else jnp.concatenate(out_im[j], axis=0)
                    hb_ref[t0 * rs:(t0 + grp) * rs, cre[j]:cre[j] + LANES] = blk_re.astype(BF16)
                    hb_ref[t0 * rs:(t0 + grp) * rs, cim[j]:cim[j] + LANES] = blk_im.astype(BF16)
            for j in range(len(cts)):
                st_ref[:, cre[j]:cre[j] + LANES] = h_re[j]
                st_ref[:, cim[j]:cim[j] + LANES] = h_im[j]
        ys.append(jnp.dot(hb_ref[:, s * 1024:(s + 1) * 1024], wc_ref[s], preferred_element_type=F32))
    y2 = jnp.concatenate(ys, axis=1)
    y_hi, y_lo = _split2(y2)
    yf = jnp.dot(pft_ref[...], y_hi, preferred_element_type=F32) + jnp.dot(pft_ref[...], y_lo, preferred_element_type=F32)
    yb = jnp.dot(pbt_ref[...], y_hi, preferred_element_type=F32) + jnp.dot(pbt_ref[...], y_lo, preferred_element_type=F32)
    yf_ref[...] = yf.reshape(nb, tc, S5_WIDTH)
    yb_ref[...] = yb.reshape(nb, tc, S5_WIDTH)
    hfin_ref[...] = st_ref[...]


def _s5_call(u, w_bu, w_c, a_re, a_im, h0, tc, tiles_per_pass):
    nb, length, _ = u.shape
    n = length // tc
    rows = nb * tc
    pf, pb = _s5_perms(nb, tc)
    are = jnp.repeat(a_re, nb, axis=0)
    aim = jnp.repeat(a_im, nb, axis=0)
    const = lambda shape: pl.BlockSpec(shape, lambda i: (0,) * len(shape))
    blk_f = pl.BlockSpec((nb, tc, S5_WIDTH), lambda i: (0, i, 0))
    blk_b = pl.BlockSpec((nb, tc, S5_WIDTH), lambda i: (0, n - 1 - i, 0))
    return pl.pallas_call(
        functools.partial(_s5_kernel, nb=nb, tc=tc, tiles_per_pass=tiles_per_pass),
        grid=(n,),
        in_specs=[blk_f, blk_b,
                  const((2 * rows, rows)), const((2 * rows, rows)), const((rows, 2 * rows)), const((rows, 2 * rows)),
                  const((S5_SLABS, 2 * LANES, 1024)), const((S5_SLABS, 1024, LANES)),
                  const((2 * nb, S5_NC)), const((2 * nb, S5_NC)), const((2 * nb, 2 * S5_NC))],
        out_specs=(blk_f, blk_b, const((2 * nb, 2 * S5_NC))),
        out_shape=(jax.ShapeDtypeStruct(u.shape, F32), jax.ShapeDtypeStruct(u.shape, F32),
                   jax.ShapeDtypeStruct((2 * nb, 2 * S5_NC), F32)),
        scratch_shapes=[pltpu.VMEM((2 * rows, 2 * S5_NC), F32), pltpu.VMEM((2 * rows, 2 * S5_NC), BF16),
                        pltpu.VMEM((2 * nb, 2 * S5_NC), F32)],
        compiler_params=pltpu.CompilerParams(dimension_semantics=("arbitrary",), vmem_limit_bytes=VMEM_LIMIT),
        name="s5_scan",
    )(u, u, jnp.asarray(pf, BF16), jnp.asarray(pb, BF16), jnp.asarray(pf.T, BF16), jnp.asarray(pb.T, BF16),
      w_bu, w_c, are, aim, h0)


def _gla_kernel(*refs, length, has_s0):
    if has_s0:
        q_ref, k_ref, v_ref, gl_ref, wg_ref, bg_ref, ng_ref, s0_ref, o_ref, sfin_ref, s_ref = refs
    else:
        q_ref, k_ref, v_ref, gl_ref, wg_ref, bg_ref, ng_ref, o_ref, sfin_ref, s_ref = refs
    c = GLA_CHUNK
    n = length // c
    ri = lax.broadcasted_iota(jnp.int32, (c, c), 0)
    ci = lax.broadcasted_iota(jnp.int32, (c, c), 1)
    lane = lax.broadcasted_iota(jnp.int32, (1, GLA_KEY), 1)
    head_mask = [((lane >= h * GLA_DK) & (lane < (h + 1) * GLA_DK)).astype(F32) for h in range(GLA_HEADS)]
    srow = lax.shift_right_logical(lax.broadcasted_iota(jnp.int32, (GLA_KEY, GLA_WIDTH), 0), 6)
    scol = lax.shift_right_logical(lax.broadcasted_iota(jnp.int32, (GLA_KEY, GLA_WIDTH), 1), 7)
    bd_mask = (srow == scol).astype(F32)
    ng = ng_ref[...]
    causal = [ci <= ri, ci >= ri]
    tri = [m.astype(BF16) for m in causal]

    s_ref[...] = jnp.zeros_like(s_ref)
    if has_s0:
        for d in range(2):
            for h in range(GLA_HEADS):
                s_ref[d, h * GLA_DK:(h + 1) * GLA_DK, h * GLA_DV:(h + 1) * GLA_DV] = s0_ref[0, d, h]

    def chunk_pair(r):
        dirs = (0, 1)
        end = (c - 1, 0)
        qc = [q_ref[0, pl.ds(r[d], c), :] * (GLA_DK ** -0.5) for d in dirs]
        kc = [k_ref[0, pl.ds(r[d], c), :] for d in dirs]
        vc = [v_ref[0, pl.ds(r[d], c), :] for d in dirs]
        logits = [_bdot(gl_ref[0, pl.ds(r[d], c), :], wg_ref[d]) + bg_ref[d] for d in dirs]
        g = [_split2(_log_sigmoid(logits[d]) * (1.0 / GLA_GATE_NORM)) for d in dirs]
        b = [jnp.dot(tri[d], g[d][0], preferred_element_type=F32) + jnp.dot(tri[d], g[d][1], preferred_element_type=F32)
             for d in dirs]
        bt = [b[d].T for d in dirs]
        kt = [kc[d].T for d in dirs]
        b_mid = [b[d][c // 2:c // 2 + 1, :] for d in dirs]
        bt_mid = [bt[d][:, c // 2:c // 2 + 1] for d in dirs]
        bt_end = [bt[d][:, end[d]:end[d] + 1] for d in dirs]
        q_in = [(qc[d] * jnp.exp(b[d])).astype(BF16) for d in dirs]
        q_mid = [qc[d] * jnp.exp(b[d] - b_mid[d]) for d in dirs]
        k_mid_t = [(kt[d] * jnp.exp(bt_mid[d] - bt[d])).astype(BF16) for d in dirs]
        k_end_t = [(kt[d] * jnp.exp(bt_end[d] - bt[d])).astype(BF16) for d in dirs]
        q_st = [jnp.concatenate([q_mid[d] * head_mask[h] for h in range(GLA_HEADS)], axis=0).astype(BF16) for d in dirs]
        a = [jnp.dot(q_st[d], k_mid_t[d], preferred_element_type=F32) for d in dirs]
        s_old = [s_ref[d] for d in dirs]
        o_inter = [jnp.dot(q_in[d], s_old[d].astype(BF16), preferred_element_type=F32) for d in dirs]
        parts = [[], []]
        for h in range(GLA_HEADS):
            for d in dirs:
                a_h = jnp.where(causal[d], a[d][h * c:(h + 1) * c, :], 0.0).astype(BF16)
                parts[d].append(jnp.dot(a_h, vc[d][:, h * GLA_DV:(h + 1) * GLA_DV], preferred_element_type=F32))
        upd = [jnp.dot(k_end_t[d], vc[d], preferred_element_type=F32) for d in dirs]
        for d in dirs:
            s_ref[d] = jnp.exp(bt_end[d]) * s_old[d] + upd[d] * bd_mask
        return [o_inter[d] + jnp.concatenate(parts[d], axis=1) for d in dirs]

    def finish(o, r0):
        o = o + o_ref[0, pl.ds(r0, c), :]
        normed = []
        for h in range(GLA_HEADS):
            oh = o[:, h * GLA_DV:(h + 1) * GLA_DV]
            ms = jnp.mean(oh * oh, axis=-1, keepdims=True)
            normed.append(oh * lax.rsqrt(ms + EPS))
        o_ref[0, pl.ds(r0, c), :] = jnp.concatenate(normed, axis=1) * ng

    def first_half(step, carry):
        rf = pl.multiple_of(step * c, c)
        rb = pl.multiple_of((n - 1 - step) * c, c)
        o_f, o_b = chunk_pair((rf, rb))
        o_ref[0, pl.ds(rf, c), :] = o_f
        o_ref[0, pl.ds(rb, c), :] = o_b
        return carry

    def second_half(step, carry):
        rf = pl.multiple_of(step * c, c)
        rb = pl.multiple_of((n - 1 - step) * c, c)
        o_f, o_b = chunk_pair((rf, rb))
        finish(o_f, rf)
        finish(o_b, rb)
        return carry

    lax.fori_loop(0, n // 2, first_half, 0)
    lax.fori_loop(n // 2, n, second_half, 0)
    for d in range(2):
        for h in range(GLA_HEADS):
            sfin_ref[0, d, h] = s_ref[d, h * GLA_DK:(h + 1) * GLA_DK, h * GLA_DV:(h + 1) * GLA_DV]


def _gla_call(q, k, v, gl, wg, bg, ng, s0):
    nseq, length, _ = q.shape
    seq = lambda width: pl.BlockSpec((1, length, width), lambda b: (b, 0, 0))
    const = lambda shape: pl.BlockSpec(shape, lambda b: (0,) * len(shape))
    st = pl.BlockSpec((1, 2, GLA_HEADS, GLA_DK, GLA_DV), lambda b: (b, 0, 0, 0, 0))
    in_specs = [seq(GLA_KEY), seq(GLA_KEY), seq(GLA_WIDTH), seq(LANES),
                const((2, LANES, GLA_KEY)), const((2, 1, GLA_KEY)), const((1, GLA_WIDTH))]
    args = [q, k, v, gl, wg, bg, ng]
    if s0 is not None:
        in_specs.append(st)
        args.append(s0)
    return pl.pallas_call(
        functools.partial(_gla_kernel, length=length, has_s0=s0 is not None),
        grid=(nseq,),
        in_specs=in_specs,
        out_specs=(seq(GLA_WIDTH), st),
        out_shape=(jax.ShapeDtypeStruct((nseq, length, GLA_WIDTH), F32),
                   jax.ShapeDtypeStruct((nseq, 2, GLA_HEADS, GLA_DK, GLA_DV), F32)),
        scratch_shapes=[pltpu.VMEM((2, GLA_KEY, GLA_WIDTH), F32)],
        compiler_params=pltpu.CompilerParams(dimension_semantics=("arbitrary",), vmem_limit_bytes=VMEM_LIMIT),
        name="gla",
    )(*args)


def _mix_kernel(*refs, has_pe, last):
    if has_pe:
        (x_ref, pe_ref, yf_ref, yb_ref, u_ref, o_ref, shift_ref, scale_ref, gate_ref, ng_ref, wg_ref, d_ref,
         wglu_ref, bglu_ref, wpa_ref, wpb_ref, wo_ref, fng_ref, out_ref) = refs
        x = x_ref[0] + pe_ref[...]
    else:
        (x_ref, yf_ref, yb_ref, u_ref, o_ref, shift_ref, scale_ref, gate_ref, ng_ref, wg_ref, d_ref,
         wglu_ref, bglu_ref, wpa_ref, wpb_ref, wo_ref, fng_ref, out_ref) = refs
        x = x_ref[0]
    h = _modulated_norm(x, ng_ref[...], scale_ref[0], shift_ref[0])
    gts = _bdot(h, wg_ref[...])
    gate_a = gts[:, :S5_WIDTH]
    gate_b = gts[:, S5_WIDTH:2 * S5_WIDTH]
    m_a = gts[:, 2 * S5_WIDTH:2 * S5_WIDTH + D_MODEL]
    m_b = gts[:, 2 * S5_WIDTH + D_MODEL:]
    y = _gelu_tanh(yf_ref[0] + yb_ref[0] + d_ref[...] * u_ref[0])
    y = y * _sigmoid(_bdot(y, wglu_ref[...]) + bglu_ref[...])
    y_a = y * _silu(gate_a)
    y_b = o_ref[0] * _silu(gate_b)
    merged = _sigmoid(m_a) * _bdot(y_a, wpa_ref[...]) + _sigmoid(m_b) * _bdot(y_b, wpb_ref[...])
    xn = x + gate_ref[0] * _bdot(merged, wo_ref[...])
    if last:
        ms = jnp.mean(xn * xn, axis=-1, keepdims=True)
        xn = xn * lax.rsqrt(ms + EPS) * fng_ref[...]
    out_ref[0] = xn


def _mix_call(x, pe, yf, yb, u, o, shift, scale1p, gate, ng, wg, s5_d, wglu, bglu, wpa, wpb, wo, fng, tm, last):
    nseq, length, _ = x.shape
    row = lambda width: pl.BlockSpec((1, tm, width), lambda b, j: (b, j, 0))
    vec = pl.BlockSpec((1, 1, D_MODEL), lambda b, j: (b, 0, 0))
    const = lambda shape: pl.BlockSpec(shape, lambda b, j: (0,) * len(shape))
    in_specs = [row(D_MODEL)]
    args = [x]
    if pe is not None:
        in_specs.append(pl.BlockSpec((tm, D_MODEL), lambda b, j: (j, 0)))
        args.append(pe)
    in_specs += [row(S5_WIDTH), row(S5_WIDTH), row(S5_WIDTH), row(GLA_WIDTH), vec, vec, vec,
                 const((1, D_MODEL)), const((D_MODEL, GATE_W)), const((1, S5_WIDTH)),
                 const((S5_WIDTH, S5_WIDTH)), const((1, S5_WIDTH)), const((S5_WIDTH, D_MODEL)),
                 const((GLA_WIDTH, D_MODEL)), const((D_MODEL, D_MODEL)), const((1, D_MODEL))]
    args += [yf, yb, u, o, shift, scale1p, gate, ng, wg, s5_d, wglu, bglu, wpa, wpb, wo, fng]
    return pl.pallas_call(
        functools.partial(_mix_kernel, has_pe=pe is not None, last=last),
        grid=(nseq, length // tm),
        in_specs=in_specs,
        out_specs=row(D_MODEL),
        out_shape=jax.ShapeDtypeStruct(x.shape, F32),
        compiler_params=pltpu.CompilerParams(dimension_semantics=("arbitrary", "arbitrary"),
                                             vmem_limit_bytes=VMEM_LIMIT),
        name="mix",
    )(*args)


def _grid_pos_embed(length, dim):
    rows = length // GRID_W
    quarter = dim // 4
    freqs = jnp.exp(-math.log(10000.0) * jnp.arange(quarter, dtype=F32) / quarter)

    def sincos(pos):
        ang = pos.astype(F32)[:, None] * freqs[None, :]
        return jnp.concatenate([jnp.sin(ang), jnp.cos(ang)], axis=-1)

    er = sincos(jnp.arange(rows))
    ec = sincos(jnp.arange(GRID_W))
    pe = jnp.concatenate([jnp.broadcast_to(er[:, None, :], (rows, GRID_W, dim // 2)),
                          jnp.broadcast_to(ec[None, :, :], (rows, GRID_W, dim // 2))], axis=-1)
    return pe.reshape(rows * GRID_W, dim)


def kernel(x_prompt, x_sample, c, state_s5_re, state_s5_im, state_gla, c_ctx, norm_g, w_mod, b_mod, w_in, gla_wg_up,
           gla_bg, gla_norm_g, s5_lam_re, s5_lam_im, s5_log_dt, s5_b_re, s5_b_im, s5_c_re, s5_c_im, s5_d, w_glu,
           b_glu, w_pa, w_pb, w_o, final_norm_g):
    depth = w_in.shape[0]
    bp, lp, _ = x_prompt.shape
    bs, ls, _ = x_sample.shape

    cond8 = jnp.zeros((8, D_MODEL), F32).at[0].set(c_ctx).at[1:1 + bs].set(c)
    mod = _mod_call(cond8, w_mod, b_mod)

    a_re, a_im, w_bu, w_c = _s5_params(s5_lam_re, s5_lam_im, s5_log_dt, s5_b_re, s5_b_im, s5_c_re, s5_c_im)

    offs = np.cumsum([0, S5_WIDTH, S5_WIDTH, GLA_KEY, GLA_KEY, GLA_WIDTH, GLA_WIDTH, GLA_RANK, D_MODEL, D_MODEL])
    col = lambda i: slice(int(offs[i]), int(offs[i + 1]))
    pad = jnp.zeros((depth, D_MODEL, LANES - GLA_RANK), F32)
    w_proj = jnp.concatenate([w_in[:, :, col(0)], w_in[:, :, col(2)], w_in[:, :, col(3)], w_in[:, :, col(4)],
                              w_in[:, :, col(6)], pad], axis=-1).astype(BF16)
    w_gate = jnp.concatenate([w_in[:, :, col(1)], w_in[:, :, col(5)], w_in[:, :, col(7)], w_in[:, :, col(8)]],
                             axis=-1).astype(BF16)
    wg_up = jnp.concatenate([gla_wg_up, jnp.zeros((depth, 2, LANES - GLA_RANK, GLA_KEY), F32)], axis=2).astype(BF16)
    w_glu_b, w_pa_b, w_pb_b, w_o_b = (w.astype(BF16) for w in (w_glu, w_pa, w_pb, w_o))

    pe = _grid_pos_embed(ls, D_MODEL)
    zero_h0 = jnp.zeros((2 * bp, 2 * S5_NC), F32)
    xp, xs = x_prompt, x_sample
    new_re, new_im, new_gla = [], [], []
    for l in range(depth):
        last = l == depth - 1
        m = mod[l]
        shift, scale1p, gate = m[:, :D_MODEL], 1.0 + m[:, D_MODEL:2 * D_MODEL], m[:, 2 * D_MODEL:]
        vec_p = lambda a: jnp.broadcast_to(a[0][None, None, :], (bp, 1, D_MODEL))
        vec_s = lambda a: a[1:1 + bs][:, None, :]
        ng = norm_g[l][None]
        gng = gla_norm_g[l][None]
        bg = gla_bg[l][:, None, :]
        common = (ng, w_gate[l], s5_d[l][None], w_glu_b[l], b_glu[l][None], w_pa_b[l], w_pb_b[l], w_o_b[l],
                  final_norm_g[None])

        u, q, k, v, gl = _proj_call(xp, None, vec_p(shift), vec_p(scale1p), ng, w_proj[l], tm=lp)
        yf, yb, hfin = _s5_call(u, w_bu[l], w_c[l], a_re[l], a_im[l], zero_h0, tc=8, tiles_per_pass=2)
        o, sfin = _gla_call(q, k, v, gl, wg_up[l], bg, gng, None)
        xp = _mix_call(xp, None, yf, yb, u, o, vec_p(shift), vec_p(scale1p), vec_p(gate), *common, tm=lp, last=last)
        re_l, im_l = _s5_cols_to_state(hfin.reshape(2, bp, 2 * S5_NC))
        new_re.append(jnp.swapaxes(re_l, 0, 1))
        new_im.append(jnp.swapaxes(im_l, 0, 1))
        new_gla.append(sfin)

        pe_l = pe if l == 0 else None
        u, q, k, v, gl = _proj_call(xs, pe_l, vec_s(shift), vec_s(scale1p), ng, w_proj[l], tm=512)
        h0 = _s5_state_to_cols(jnp.swapaxes(state_s5_re[:, l], 0, 1), jnp.swapaxes(state_s5_im[:, l], 0, 1))
        yf, yb, _ = _s5_call(u, w_bu[l], w_c[l], a_re[l], a_im[l], h0.reshape(2 * bs, 2 * S5_NC), tc=32,
                             tiles_per_pass=4)
        o, _ = _gla_call(q, k, v, gl, wg_up[l], bg, gng, state_gla[:, l])
        xs = _mix_call(xs, pe_l, yf, yb, u, o, vec_s(shift), vec_s(scale1p), vec_s(gate), *common, tm=512, last=last)

    return (xp, xs, jnp.stack(new_re, axis=1), jnp.stack(new_im, axis=1), jnp.stack(new_gla, axis=1))
```

```python
import functools
import math

import numpy as np
import jax
import jax.numpy as jnp
from jax import lax
from jax.experimental import pallas as pl
from jax.experimental.pallas import tpu as pltpu

F32 = jnp.float32
BF16 = jnp.bfloat16

D_MODEL = 1024
GRID_W = 64
S5_WIDTH = 512
S5_GROUPS = 32
S5_CH = 16
S5_STATE = 64
S5_NC = S5_GROUPS * S5_STATE
S5_SLABS = 4
GLA_HEADS = 4
GLA_DK = 64
GLA_DV = 128
GLA_KEY = GLA_HEADS * GLA_DK
GLA_WIDTH = GLA_HEADS * GLA_DV
GLA_RANK = 16
GLA_GATE_NORM = 16.0
GLA_CHUNK = 128
EPS = 1e-6
LANES = 128
PROJ_W = S5_WIDTH + 2 * GLA_KEY + GLA_WIDTH + LANES
GATE_W = 2 * S5_WIDTH + 2 * D_MODEL
VMEM_LIMIT = 52 * 1024 * 1024


def _bdot(a, b):
    return jnp.dot(a.astype(BF16), b.astype(BF16), preferred_element_type=F32)


def _split2(x):
    hi = x.astype(BF16)
    lo = (x - hi.astype(F32)).astype(BF16)
    return hi, lo


def _sigmoid(x):
    return 1.0 / (1.0 + jnp.exp(-x))


def _silu(x):
    return x * _sigmoid(x)


def _gelu_tanh(x):
    c = math.sqrt(2.0 / math.pi)
    return 0.5 * x * (1.0 + jnp.tanh(c * (x + 0.044715 * (x * x * x))))


def _exact_zero_of(x):
    bits = lax.bitcast_convert_type(x, jnp.int32)
    return lax.shift_right_logical(lax.shift_right_logical(bits, 16), 16).astype(F32)


def _log_sigmoid(x):
    return jnp.minimum(x, 0.0) - jnp.log(1.0 + jnp.exp(-jnp.abs(x)))


def _modulated_norm(x, ng, scale1p, shift):
    ms = jnp.mean(x * x, axis=-1, keepdims=True)
    return (x * lax.rsqrt(ms + EPS) * ng) * scale1p + shift


def _mod_kernel(cond_ref, w_ref, b_ref, out_ref):
    c = cond_ref[...]
    s_hi, s_lo = _split2(_silu(c))
    w_hi, w_lo = _split2(w_ref[0])
    acc = jnp.dot(s_hi, w_hi, preferred_element_type=F32)
    acc += jnp.dot(s_lo, w_hi, preferred_element_type=F32)
    acc += jnp.dot(s_hi, w_lo, preferred_element_type=F32)
    out_ref[0] = acc + b_ref[0]


def _mod_call(cond8, w_mod, b_mod):
    depth = w_mod.shape[0]
    nb = 768
    return pl.pallas_call(
        _mod_kernel,
        grid=(depth, 3 * D_MODEL // nb),
        in_specs=[
            pl.BlockSpec((8, D_MODEL), lambda l, j: (0, 0)),
            pl.BlockSpec((1, D_MODEL, nb), lambda l, j: (l, 0, j)),
            pl.BlockSpec((1, 1, nb), lambda l, j: (l, 0, j)),
        ],
        out_specs=pl.BlockSpec((1, 8, nb), lambda l, j: (l, 0, j)),
        out_shape=jax.ShapeDtypeStruct((depth, 8, 3 * D_MODEL), F32),
        compiler_params=pltpu.CompilerParams(dimension_semantics=("arbitrary", "arbitrary")),
        name="adaln_mod",
    )(cond8, w_mod, b_mod.reshape(depth, 1, 3 * D_MODEL))


def _zoh_kernel(lre_ref, lim_ref, ldt_ref, bre_ref, bim_ref, are_ref, aim_ref, bbre_ref, bbim_ref):
    lre = lre_ref[...]
    lim = lim_ref[...]
    dt = jnp.exp(ldt_ref[...])
    mag = jnp.exp(lre * dt)
    a_re = mag * jnp.cos(lim * dt)
    a_im = mag * jnp.sin(lim * dt)
    n_re = a_re - 1.0
    inv = 1.0 / (lre * lre + lim * lim)
    c_re = (n_re * lre + a_im * lim) * inv
    c_im = (a_im * lre - n_re * lim) * inv
    bre = bre_ref[...]
    bim = bim_ref[...]
    are_ref[...] = a_re
    aim_ref[...] = a_im
    bbre_ref[...] = c_re * bre - c_im * bim
    bbim_ref[...] = c_re * bim + c_im * bre


def _zoh_call(lre, lim, ldt, bre, bim):
    shp = jax.ShapeDtypeStruct(lre.shape, F32)
    return pl.pallas_call(
        _zoh_kernel,
        out_shape=(shp, shp, shp, shp),
        name="s5_zoh",
    )(lre, lim, ldt, bre, bim)


def _s5_params(s5_lam_re, s5_lam_im, s5_log_dt, s5_b_re, s5_b_im, s5_c_re, s5_c_im):
    depth = s5_lam_re.shape[0]
    full = (depth, 2, S5_GROUPS, S5_CH, S5_STATE)
    expand = lambda a: jnp.broadcast_to(a, full).reshape(-1, S5_STATE)
    lre = expand(s5_lam_re[:, :, :, None, :])
    lim = expand(s5_lam_im[:, :, :, None, :])
    ldt = expand(s5_log_dt[:, :, :, None, None])
    bre = expand(jnp.swapaxes(s5_b_re, -1, -2)[:, None])
    bim = expand(jnp.swapaxes(s5_b_im, -1, -2)[:, None])
    are, aim, bbre, bbim = [a.reshape(full) for a in _zoh_call(lre, lim, ldt, bre, bim)]
    a_re = are[:, :, :, 0, :].reshape(depth, 2, S5_NC)
    a_im = aim[:, :, :, 0, :].reshape(depth, 2, S5_NC)
    eye = jnp.eye(8, dtype=F32)
    t = jnp.stack([bbre, bbim], axis=-2)
    t = t.reshape(depth, 2, S5_SLABS, 8, S5_CH, 2, S5_STATE)
    t = jnp.transpose(t, (0, 2, 1, 3, 4, 5, 6))
    w_bu = t[:, :, :, :, :, :, None, :] * eye[None, None, None, :, None, None, :, None]
    w_bu = w_bu.reshape(depth, S5_SLABS, 2 * LANES, 2 * 8 * S5_STATE).astype(BF16)
    v = jnp.stack([s5_c_re, -s5_c_im], axis=1)
    v = v.reshape(depth, 2, S5_SLABS, 8, S5_CH, S5_STATE)
    v = jnp.transpose(v, (0, 2, 1, 3, 5, 4))
    w_c = v[:, :, :, :, :, None, :] * eye[None, None, None, :, None, :, None]
    w_c = w_c.reshape(depth, S5_SLABS, 2 * 8 * S5_STATE, LANES).astype(BF16)
    return a_re, a_im, w_bu, w_c


def _s5_state_to_cols(re, im):
    lead = re.shape[:-2]
    st = jnp.stack([re.reshape(lead + (S5_SLABS, 512)), im.reshape(lead + (S5_SLABS, 512))], axis=-2)
    return st.reshape(lead + (2 * S5_NC,))


def _s5_cols_to_state(cols):
    lead = cols.shape[:-1]
    st = cols.reshape(lead + (S5_SLABS, 2, 512))
    re = st[..., 0, :].reshape(lead + (S5_GROUPS, S5_STATE))
    im = st[..., 1, :].reshape(lead + (S5_GROUPS, S5_STATE))
    return re, im


def _proj_kernel(*refs, has_pe):
    if has_pe:
        x_ref, pe_ref, shift_ref, scale_ref, ng_ref, w_ref, u_ref, q_ref, k_ref, v_ref, gl_ref = refs
        x = x_ref[0] + pe_ref[...]
    else:
        x_ref, shift_ref, scale_ref, ng_ref, w_ref, u_ref, q_ref, k_ref, v_ref, gl_ref = refs
        x = x_ref[0]
    h = _modulated_norm(x, ng_ref[...], scale_ref[0], shift_ref[0])
    p = _bdot(h, w_ref[...])
    o = 0
    u_ref[0] = p[:, o:o + S5_WIDTH]
    o += S5_WIDTH
    q_ref[0] = p[:, o:o + GLA_KEY]
    o += GLA_KEY
    k_ref[0] = p[:, o:o + GLA_KEY]
    o += GLA_KEY
    v_ref[0] = p[:, o:o + GLA_WIDTH].astype(BF16)
    o += GLA_WIDTH
    gl_ref[0] = p[:, o:o + LANES]


def _proj_call(x, pe, shift, scale1p, ng, w, tm):
    nseq, length, _ = x.shape
    row = lambda width: pl.BlockSpec((1, tm, width), lambda b, j: (b, j, 0))
    vec = pl.BlockSpec((1, 1, D_MODEL), lambda b, j: (b, 0, 0))
    in_specs = [row(D_MODEL)]
    args = [x]
    if pe is not None:
        in_specs.append(pl.BlockSpec((tm, D_MODEL), lambda b, j: (j, 0)))
        args.append(pe)
    in_specs += [vec, vec, pl.BlockSpec((1, D_MODEL), lambda b, j: (0, 0)),
                 pl.BlockSpec((D_MODEL, PROJ_W), lambda b, j: (0, 0))]
    args += [shift, scale1p, ng, w]
    sds = lambda width, dt: jax.ShapeDtypeStruct((nseq, length, width), dt)
    return pl.pallas_call(
        functools.partial(_proj_kernel, has_pe=pe is not None),
        grid=(nseq, length // tm),
        in_specs=in_specs,
        out_specs=(row(S5_WIDTH), row(GLA_KEY), row(GLA_KEY), row(GLA_WIDTH), row(LANES)),
        out_shape=(sds(S5_WIDTH, F32), sds(GLA_KEY, F32), sds(GLA_KEY, F32), sds(GLA_WIDTH, BF16), sds(LANES, F32)),
        compiler_params=pltpu.CompilerParams(dimension_semantics=("arbitrary", "arbitrary"),
                                             vmem_limit_bytes=VMEM_LIMIT),
        name="proj",
    )(*args)


def _s5_perms(nb, tc):
    r = nb * tc
    pf = np.zeros((2 * r, r), np.float32)
    pb = np.zeros((2 * r, r), np.float32)
    for t in range(tc):
        for b in range(nb):
            pf[t * 2 * nb + b, b * tc + t] = 1.0
            pb[t * 2 * nb + nb + b, b * tc + (tc - 1 - t)] = 1.0
    return pf, pb


def _s5_kernel(uf1_ref, ub1_ref, uf2_ref, ub2_ref, uf0_ref, ub0_ref, pf_ref, pb_ref, pft_ref, pbt_ref, wbu_ref, wc_ref,
               are_ref, aim_ref, h0_ref, yf_ref, yb_ref, hfin_ref, bua_ref, bub_ref, hb_ref, st_ref,
               *, nb, tc, tiles_per_pass):
    i = pl.program_id(0)
    rows = nb * tc
    rs = 2 * nb

    def packed_inputs(uf_ref, ub_ref):
        uf = uf_ref[...].reshape(rows, S5_WIDTH).astype(BF16)
        ub = ub_ref[...].reshape(rows, S5_WIDTH).astype(BF16)
        u2f = jnp.dot(pf_ref[...], uf, preferred_element_type=F32).astype(BF16)
        u2b = jnp.dot(pb_ref[...], ub, preferred_element_type=F32).astype(BF16)
        return [jnp.concatenate([u2f[:, s * LANES:(s + 1) * LANES], u2b[:, s * LANES:(s + 1) * LANES]], axis=1)
                for s in range(S5_SLABS)]

    def bu_slab(lhs, s, dst_ref):
        dst_ref[:, s * 1024:(s + 1) * 1024] = jnp.dot(lhs[s], wbu_ref[s], preferred_element_type=F32)

    @pl.when(i == 0)
    def _():
        st_ref[...] = h0_ref[...]
        lhs0 = packed_inputs(uf0_ref, ub0_ref)
        for s in range(S5_SLABS):
            bu_slab(lhs0, s, bua_ref)

    half_step = functools.partial(_s5_half_step, pft_ref=pft_ref, pbt_ref=pbt_ref, wc_ref=wc_ref, are_ref=are_ref,
                                  aim_ref=aim_ref, hb_ref=hb_ref, st_ref=st_ref, bu_slab=bu_slab, rs=rs, tc=tc,
                                  tiles_per_pass=tiles_per_pass)
    yf, yb, tail = half_step(bua_ref, bub_ref, packed_inputs(uf1_ref, ub1_ref))
    yf_ref[:, 0:tc, :] = yf.reshape(nb, tc, S5_WIDTH)
    yb_ref[:, tc:2 * tc, :] = yb.reshape(nb, tc, S5_WIDTH)
    yf, yb, _ = half_step(bub_ref, bua_ref, packed_inputs(uf2_ref, ub2_ref), after=tail)
    yf_ref[:, tc:2 * tc, :] = yf.reshape(nb, tc, S5_WIDTH)
    yb_ref[:, 0:tc, :] = yb.reshape(nb, tc, S5_WIDTH)
    hfin_ref[...] = st_ref[...]


def _s5_half_step(cur_ref, nxt_ref, lhs_next, *, pft_ref, pbt_ref, wc_ref, are_ref, aim_ref, hb_ref, st_ref, bu_slab,
                  rs, tc, tiles_per_pass, after=None):
    grp = max(1, 16 // rs)
    ys = []
    for s in range(S5_SLABS):
        bu_slab(lhs_next, s, nxt_ref)
        for c0 in range(4 * s, 4 * s + 4, tiles_per_pass):
            cts = list(range(c0, c0 + tiles_per_pass))
            cre = [(ct // 4) * 1024 + (ct % 4) * LANES for ct in cts]
            cim = [c + 512 for c in cre]
            a_re = [are_ref[:, ct * LANES:(ct + 1) * LANES] for ct in cts]
            a_im = [aim_ref[:, ct * LANES:(ct + 1) * LANES] for ct in cts]
            h_re = [st_ref[:, c:c + LANES] for c in cre]
            h_im = [st_ref[:, c:c + LANES] for c in cim]
            if after is not None and c0 == 0:
                h_re[0] = h_re[0] + _exact_zero_of(after)
            for t0 in range(0, tc, grp):
                out_re = [[] for _ in cts]
                out_im = [[] for _ in cts]
                for t in range(t0, t0 + grp):
                    r0 = t * rs
                    for j in range(len(cts)):
                        b_re = cur_ref[r0:r0 + rs, cre[j]:cre[j] + LANES]
                        b_im = cur_ref[r0:r0 + rs, cim[j]:cim[j] + LANES]
                        n_re = a_re[j] * h_re[j] - a_im[j] * h_im[j] + b_re
                        n_im = a_re[j] * h_im[j] + a_im[j] * h_re[j] + b_im
                        h_re[j], h_im[j] = n_re, n_im
                        out_re[j].append(n_re)
                        out_im[j].append(n_im)
                for j in range(len(cts)):
                    blk_re = out_re[j][0] if grp == 1 else jnp.concatenate(out_re[j], axis=0)
                    blk_im = out_im[j][0] if grp == 1 else jnp.concatenate(out_im[j], axis=0)
                    hb_ref[t0 * rs:(t0 + grp) * rs, cre[j]:cre[j] + LANES] = blk_re.astype(BF16)
                    hb_ref[t0 * rs:(t0 + grp) * rs, cim[j]:cim[j] + LANES] = blk_im.astype(BF16)
            for j in range(len(cts)):
                st_ref[:, cre[j]:cre[j] + LANES] = h_re[j]
                st_ref[:, cim[j]:cim[j] + LANES] = h_im[j]
        ys.append(jnp.dot(hb_ref[:, s * 1024:(s + 1) * 1024], wc_ref[s], preferred_element_type=F32))
    y2 = jnp.concatenate(ys, axis=1)
    y_hi, y_lo = _split2(y2)
    yf = jnp.dot(pft_ref[...], y_hi, preferred_element_type=F32) + jnp.dot(pft_ref[...], y_lo, preferred_element_type=F32)
    yb = jnp.dot(pbt_ref[...], y_hi, preferred_element_type=F32) + jnp.dot(pbt_ref[...], y_lo, preferred_element_type=F32)
    return yf, yb, h_re[-1]


def _s5_call(u, w_bu, w_c, a_re, a_im, h0, tc, tiles_per_pass):
    nb, length, _ = u.shape
    n = length // tc
    rows = nb * tc
    pf, pb = _s5_perms(nb, tc)
    are = jnp.repeat(a_re, nb, axis=0)
    aim = jnp.repeat(a_im, nb, axis=0)
    const = lambda shape: pl.BlockSpec(shape, lambda i: (0,) * len(shape))
    blk = lambda index_map: pl.BlockSpec((nb, tc, S5_WIDTH), index_map)
    out_f = pl.BlockSpec((nb, 2 * tc, S5_WIDTH), lambda i: (0, i, 0))
    out_b = pl.BlockSpec((nb, 2 * tc, S5_WIDTH), lambda i: (0, n // 2 - 1 - i, 0))
    second = lambda i: jnp.minimum(2 * i + 2, n - 1)
    return pl.pallas_call(
        functools.partial(_s5_kernel, nb=nb, tc=tc, tiles_per_pass=tiles_per_pass),
        grid=(n // 2,),
        in_specs=[blk(lambda i: (0, 2 * i + 1, 0)), blk(lambda i: (0, n - 2 - 2 * i, 0)),
                  blk(lambda i: (0, second(i), 0)), blk(lambda i: (0, n - 1 - second(i), 0)),
                  blk(lambda i: (0, 0, 0)), blk(lambda i: (0, n - 1, 0)),
                  const((2 * rows, rows)), const((2 * rows, rows)), const((rows, 2 * rows)), const((rows, 2 * rows)),
                  const((S5_SLABS, 2 * LANES, 1024)), const((S5_SLABS, 1024, LANES)),
                  const((2 * nb, S5_NC)), const((2 * nb, S5_NC)), const((2 * nb, 2 * S5_NC))],
        out_specs=(out_f, out_b, const((2 * nb, 2 * S5_NC))),
        out_shape=(jax.ShapeDtypeStruct(u.shape, F32), jax.ShapeDtypeStruct(u.shape, F32),
                   jax.ShapeDtypeStruct((2 * nb, 2 * S5_NC), F32)),
        scratch_shapes=[pltpu.VMEM((2 * rows, 2 * S5_NC), F32), pltpu.VMEM((2 * rows, 2 * S5_NC), F32),
                        pltpu.VMEM((2 * rows, 2 * S5_NC), BF16), pltpu.VMEM((2 * nb, 2 * S5_NC), F32)],
        compiler_params=pltpu.CompilerParams(dimension_semantics=("arbitrary",), vmem_limit_bytes=VMEM_LIMIT),
        name="s5_scan",
    )(u, u, u, u, u, u, jnp.asarray(pf, BF16), jnp.asarray(pb, BF16), jnp.asarray(pf.T, BF16), jnp.asarray(pb.T, BF16),
      w_bu, w_c, are, aim, h0)


def _gla_kernel(*refs, length, has_s0):
    if has_s0:
        q_ref, k_ref, v_ref, gl_ref, wg_ref, bg_ref, ng_ref, s0_ref, o_ref, sfin_ref, s_ref = refs
    else:
        q_ref, k_ref, v_ref, gl_ref, wg_ref, bg_ref, ng_ref, o_ref, sfin_ref, s_ref = refs
    c = GLA_CHUNK
    n = length // c
    ri = lax.broadcasted_iota(jnp.int32, (c, c), 0)
    ci = lax.broadcasted_iota(jnp.int32, (c, c), 1)
    lane = lax.broadcasted_iota(jnp.int32, (1, GLA_KEY), 1)
    head_mask = [((lane >= h * GLA_DK) & (lane < (h + 1) * GLA_DK)).astype(F32) for h in range(GLA_HEADS)]
    srow = lax.shift_right_logical(lax.broadcasted_iota(jnp.int32, (GLA_KEY, GLA_WIDTH), 0), 6)
    scol = lax.shift_right_logical(lax.broadcasted_iota(jnp.int32, (GLA_KEY, GLA_WIDTH), 1), 7)
    bd_mask = (srow == scol).astype(F32)
    ng = ng_ref[...]
    causal = [ci <= ri, ci >= ri]
    tri = [m.astype(BF16) for m in causal]

    s_ref[...] = jnp.zeros_like(s_ref)
    if has_s0:
        for d in range(2):
            for h in range(GLA_HEADS):
                s_ref[d, h * GLA_DK:(h + 1) * GLA_DK, h * GLA_DV:(h + 1) * GLA_DV] = s0_ref[0, d, h]

    def chunk_pair(r):
        dirs = (0, 1)
        end = (c - 1, 0)
        qc = [q_ref[0, pl.ds(r[d], c), :] * (GLA_DK ** -0.5) for d in dirs]
        kc = [k_ref[0, pl.ds(r[d], c), :] for d in dirs]
        vc = [v_ref[0, pl.ds(r[d], c), :] for d in dirs]
        logits = [_bdot(gl_ref[0, pl.ds(r[d], c), :], wg_ref[d]) + bg_ref[d] for d in dirs]
        g = [_split2(_log_sigmoid(logits[d]) * (1.0 / GLA_GATE_NORM)) for d in dirs]
        b = [jnp.dot(tri[d], g[d][0], preferred_element_type=F32) + jnp.dot(tri[d], g[d][1], preferred_element_type=F32)
             for d in dirs]
        bt = [b[d].T for d in dirs]
        kt = [kc[d].T for d in dirs]
        b_mid = [b[d][c // 2:c // 2 + 1, :] for d in dirs]
        bt_mid = [bt[d][:, c // 2:c // 2 + 1] for d in dirs]
        bt_end = [bt[d][:, end[d]:end[d] + 1] for d in dirs]
        q_in = [(qc[d] * jnp.exp(b[d])).astype(BF16) for d in dirs]
        q_mid = [qc[d] * jnp.exp(b[d] - b_mid[d]) for d in dirs]
        k_mid_t = [(kt[d] * jnp.exp(bt_mid[d] - bt[d])).astype(BF16) for d in dirs]
        k_end_t = [(kt[d] * jnp.exp(bt_end[d] - bt[d])).astype(BF16) for d in dirs]
        q_st = [jnp.concatenate([q_mid[d] * head_mask[h] for h in range(GLA_HEADS)], axis=0).astype(BF16) for d in dirs]
        a = [jnp.dot(q_st[d], k_mid_t[d], preferred_element_type=F32) for d in dirs]
        s_old = [s_ref[d] for d in dirs]
        o_inter = [jnp.dot(q_in[d], s_old[d].astype(BF16), preferred_element_type=F32) for d in dirs]
        parts = [[], []]
        for h in range(GLA_HEADS):
            for d in dirs:
                a_h = jnp.where(causal[d], a[d][h * c:(h + 1) * c, :], 0.0).astype(BF16)
                parts[d].append(jnp.dot(a_h, vc[d][:, h * GLA_DV:(h + 1) * GLA_DV], preferred_element_type=F32))
        upd = [jnp.dot(k_end_t[d], vc[d], preferred_element_type=F32) for d in dirs]
        for d in dirs:
            s_ref[d] = jnp.exp(bt_end[d]) * s_old[d] + upd[d] * bd_mask
        return [o_inter[d] + jnp.concatenate(parts[d], axis=1) for d in dirs]

    def finish(o, r0):
        o = o + o_ref[0, pl.ds(r0, c), :]
        normed = []
        for h in range(GLA_HEADS):
            oh = o[:, h * GLA_DV:(h + 1) * GLA_DV]
            ms = jnp.mean(oh * oh, axis=-1, keepdims=True)
            normed.append(oh * lax.rsqrt(ms + EPS))
        o_ref[0, pl.ds(r0, c), :] = jnp.concatenate(normed, axis=1) * ng

    def first_half(step, carry):
        rf = pl.multiple_of(step * c, c)
        rb = pl.multiple_of((n - 1 - step) * c, c)
        o_f, o_b = chunk_pair((rf, rb))
        o_ref[0, pl.ds(rf, c), :] = o_f
        o_ref[0, pl.ds(rb, c), :] = o_b
        return carry

    def second_half(step, carry):
        rf = pl.multiple_of(step * c, c)
        rb = pl.multiple_of((n - 1 - step) * c, c)
        o_f, o_b = chunk_pair((rf, rb))
        finish(o_f, rf)
        finish(o_b, rb)
        return carry

    lax.fori_loop(0, n // 2, first_half, 0)
    lax.fori_loop(n // 2, n, second_half, 0)
    for d in range(2):
        for h in range(GLA_HEADS):
            sfin_ref[0, d, h] = s_ref[d, h * GLA_DK:(h + 1) * GLA_DK, h * GLA_DV:(h + 1) * GLA_DV]


def _gla_call(q, k, v, gl, wg, bg, ng, s0):
    nseq, length, _ = q.shape
    seq = lambda width: pl.BlockSpec((1, length, width), lambda b: (b, 0, 0))
    const = lambda shape: pl.BlockSpec(shape, lambda b: (0,) * len(shape))
    st = pl.BlockSpec((1, 2, GLA_HEADS, GLA_DK, GLA_DV), lambda b: (b, 0, 0, 0, 0))
    in_specs = [seq(GLA_KEY), seq(GLA_KEY), seq(GLA_WIDTH), seq(LANES),
                const((2, LANES, GLA_KEY)), const((2, 1, GLA_KEY)), const((1, GLA_WIDTH))]
    args = [q, k, v, gl, wg, bg, ng]
    if s0 is not None:
        in_specs.append(st)
        args.append(s0)
    return pl.pallas_call(
        functools.partial(_gla_kernel, length=length, has_s0=s0 is not None),
        grid=(nseq,),
        in_specs=in_specs,
        out_specs=(seq(GLA_WIDTH), st),
        out_shape=(jax.ShapeDtypeStruct((nseq, length, GLA_WIDTH), F32),
                   jax.ShapeDtypeStruct((nseq, 2, GLA_HEADS, GLA_DK, GLA_DV), F32)),
        scratch_shapes=[pltpu.VMEM((2, GLA_KEY, GLA_WIDTH), F32)],
        compiler_params=pltpu.CompilerParams(dimension_semantics=("arbitrary",), vmem_limit_bytes=VMEM_LIMIT),
        name="gla",
    )(*args)


def _mix_kernel(*refs, has_pe, last):
    if has_pe:
        (x_ref, pe_ref, yf_ref, yb_ref, u_ref, o_ref, shift_ref, scale_ref, gate_ref, ng_ref, wg_ref, d_ref,
         wglu_ref, bglu_ref, wpa_ref, wpb_ref, wo_ref, fng_ref, out_ref) = refs
        x = x_ref[0] + pe_ref[...]
    else:
        (x_ref, yf_ref, yb_ref, u_ref, o_ref, shift_ref, scale_ref, gate_ref, ng_ref, wg_ref, d_ref,
         wglu_ref, bglu_ref, wpa_ref, wpb_ref, wo_ref, fng_ref, out_ref) = refs
        x = x_ref[0]
    h = _modulated_norm(x, ng_ref[...], scale_ref[0], shift_ref[0])
    gts = _bdot(h, wg_ref[...])
    gate_a = gts[:, :S5_WIDTH]
    gate_b = gts[:, S5_WIDTH:2 * S5_WIDTH]
    m_a = gts[:, 2 * S5_WIDTH:2 * S5_WIDTH + D_MODEL]
    m_b = gts[:, 2 * S5_WIDTH + D_MODEL:]
    y = _gelu_tanh(yf_ref[0] + yb_ref[0] + d_ref[...] * u_ref[0])
    y = y * _sigmoid(_bdot(y, wglu_ref[...]) + bglu_ref[...])
    y_a = y * _silu(gate_a)
    y_b = o_ref[0] * _silu(gate_b)
    merged = _sigmoid(m_a) * _bdot(y_a, wpa_ref[...]) + _sigmoid(m_b) * _bdot(y_b, wpb_ref[...])
    xn = x + gate_ref[0] * _bdot(merged, wo_ref[...])
    if last:
        ms = jnp.mean(xn * xn, axis=-1, keepdims=True)
        xn = xn * lax.rsqrt(ms + EPS) * fng_ref[...]
    out_ref[0] = xn


def _mix_call(x, pe, yf, yb, u, o, shift, scale1p, gate, ng, wg, s5_d, wglu, bglu, wpa, wpb, wo, fng, tm, last):
    nseq, length, _ = x.shape
    row = lambda width: pl.BlockSpec((1, tm, width), lambda b, j: (b, j, 0))
    vec = pl.BlockSpec((1, 1, D_MODEL), lambda b, j: (b, 0, 0))
    const = lambda shape: pl.BlockSpec(shape, lambda b, j: (0,) * len(shape))
    in_specs = [row(D_MODEL)]
    args = [x]
    if pe is not None:
        in_specs.append(pl.BlockSpec((tm, D_MODEL), lambda b, j: (j, 0)))
        args.append(pe)
    in_specs += [row(S5_WIDTH), row(S5_WIDTH), row(S5_WIDTH), row(GLA_WIDTH), vec, vec, vec,
                 const((1, D_MODEL)), const((D_MODEL, GATE_W)), const((1, S5_WIDTH)),
                 const((S5_WIDTH, S5_WIDTH)), const((1, S5_WIDTH)), const((S5_WIDTH, D_MODEL)),
                 const((GLA_WIDTH, D_MODEL)), const((D_MODEL, D_MODEL)), const((1, D_MODEL))]
    args += [yf, yb, u, o, shift, scale1p, gate, ng, wg, s5_d, wglu, bglu, wpa, wpb, wo, fng]
    return pl.pallas_call(
        functools.partial(_mix_kernel, has_pe=pe is not None, last=last),
        grid=(nseq, length // tm),
        in_specs=in_specs,
        out_specs=row(D_MODEL),
        out_shape=jax.ShapeDtypeStruct(x.shape, F32),
        compiler_params=pltpu.CompilerParams(dimension_semantics=("arbitrary", "arbitrary"),
                                             vmem_limit_bytes=VMEM_LIMIT),
        name="mix",
    )(*args)


def _grid_pos_embed(length, dim):
    rows = length // GRID_W
    quarter = dim // 4
    freqs = jnp.exp(-math.log(10000.0) * jnp.arange(quarter, dtype=F32) / quarter)

    def sincos(pos):
        ang = pos.astype(F32)[:, None] * freqs[None, :]
        return jnp.concatenate([jnp.sin(ang), jnp.cos(ang)], axis=-1)

    er = sincos(jnp.arange(rows))
    ec = sincos(jnp.arange(GRID_W))
    pe = jnp.concatenate([jnp.broadcast_to(er[:, None, :], (rows, GRID_W, dim // 2)),
                          jnp.broadcast_to(ec[None, :, :], (rows, GRID_W, dim // 2))], axis=-1)
    return pe.reshape(rows * GRID_W, dim)


def kernel(x_prompt, x_sample, c, state_s5_re, state_s5_im, state_gla, c_ctx, norm_g, w_mod, b_mod, w_in, gla_wg_up,
           gla_bg, gla_norm_g, s5_lam_re, s5_lam_im, s5_log_dt, s5_b_re, s5_b_im, s5_c_re, s5_c_im, s5_d, w_glu,
           b_glu, w_pa, w_pb, w_o, final_norm_g):
    depth = w_in.shape[0]
    bp, lp, _ = x_prompt.shape
    bs, ls, _ = x_sample.shape

    cond8 = jnp.zeros((8, D_MODEL), F32).at[0].set(c_ctx).at[1:1 + bs].set(c)
    mod = _mod_call(cond8, w_mod, b_mod)

    a_re, a_im, w_bu, w_c = _s5_params(s5_lam_re, s5_lam_im, s5_log_dt, s5_b_re, s5_b_im, s5_c_re, s5_c_im)

    offs = np.cumsum([0, S5_WIDTH, S5_WIDTH, GLA_KEY, GLA_KEY, GLA_WIDTH, GLA_WIDTH, GLA_RANK, D_MODEL, D_MODEL])
    col = lambda i: slice(int(offs[i]), int(offs[i + 1]))
    pad = jnp.zeros((depth, D_MODEL, LANES - GLA_RANK), F32)
    w_proj = jnp.concatenate([w_in[:, :, col(0)], w_in[:, :, col(2)], w_in[:, :, col(3)], w_in[:, :, col(4)],
                              w_in[:, :, col(6)], pad], axis=-1).astype(BF16)
    w_gate = jnp.concatenate([w_in[:, :, col(1)], w_in[:, :, col(5)], w_in[:, :, col(7)], w_in[:, :, col(8)]],
                             axis=-1).astype(BF16)
    wg_up = jnp.concatenate([gla_wg_up, jnp.zeros((depth, 2, LANES - GLA_RANK, GLA_KEY), F32)], axis=2).astype(BF16)
    w_glu_b, w_pa_b, w_pb_b, w_o_b = (w.astype(BF16) for w in (w_glu, w_pa, w_pb, w_o))

    pe = _grid_pos_embed(ls, D_MODEL)
    zero_h0 = jnp.zeros((2 * bp, 2 * S5_NC), F32)
    xp, xs = x_prompt, x_sample
    new_re, new_im, new_gla = [], [], []
    for l in range(depth):
        last = l == depth - 1
        m = mod[l]
        shift, scale1p, gate = m[:, :D_MODEL], 1.0 + m[:, D_MODEL:2 * D_MODEL], m[:, 2 * D_MODEL:]
        vec_p = lambda a: jnp.broadcast_to(a[0][None, None, :], (bp, 1, D_MODEL))
        vec_s = lambda a: a[1:1 + bs][:, None, :]
        ng = norm_g[l][None]
        gng = gla_norm_g[l][None]
        bg = gla_bg[l][:, None, :]
        common = (ng, w_gate[l], s5_d[l][None], w_glu_b[l], b_glu[l][None], w_pa_b[l], w_pb_b[l], w_o_b[l],
                  final_norm_g[None])

        u, q, k, v, gl = _proj_call(xp, None, vec_p(shift), vec_p(scale1p), ng, w_proj[l], tm=lp)
        yf, yb, hfin = _s5_call(u, w_bu[l], w_c[l], a_re[l], a_im[l], zero_h0, tc=8, tiles_per_pass=2)
        o, sfin = _gla_call(q, k, v, gl, wg_up[l], bg, gng, None)
        xp = _mix_call(xp, None, yf, yb, u, o, vec_p(shift), vec_p(scale1p), vec_p(gate), *common, tm=lp, last=last)
        re_l, im_l = _s5_cols_to_state(hfin.reshape(2, bp, 2 * S5_NC))
        new_re.append(jnp.swapaxes(re_l, 0, 1))
        new_im.append(jnp.swapaxes(im_l, 0, 1))
        new_gla.append(sfin)

        pe_l = pe if l == 0 else None
        u, q, k, v, gl = _proj_call(xs, pe_l, vec_s(shift), vec_s(scale1p), ng, w_proj[l], tm=512)
        h0 = _s5_state_to_cols(jnp.swapaxes(state_s5_re[:, l], 0, 1), jnp.swapaxes(state_s5_im[:, l], 0, 1))
        yf, yb, _ = _s5_call(u, w_bu[l], w_c[l], a_re[l], a_im[l], h0.reshape(2 * bs, 2 * S5_NC), tc=32,
                             tiles_per_pass=4)
        o, _ = _gla_call(q, k, v, gl, wg_up[l], bg, gng, state_gla[:, l])
        xs = _mix_call(xs, pe_l, yf, yb, u, o, vec_s(shift), vec_s(scale1p), vec_s(gate), *common, tm=512, last=last)

    return (xp, xs, jnp.stack(new_re, axis=1), jnp.stack(new_im, axis=1), jnp.stack(new_gla, axis=1))
```

```python
import functools
import math

import numpy as np
import jax
import jax.numpy as jnp
from jax import lax
from jax.experimental import pallas as pl
from jax.experimental.pallas import tpu as pltpu

F32 = jnp.float32
BF16 = jnp.bfloat16

D_MODEL = 1024
GRID_W = 64
S5_WIDTH = 512
S5_GROUPS = 32
S5_CH = 16
S5_STATE = 64
S5_NC = S5_GROUPS * S5_STATE
S5_SLABS = 4
GLA_HEADS = 4
GLA_DK = 64
GLA_DV = 128
GLA_KEY = GLA_HEADS * GLA_DK
GLA_WIDTH = GLA_HEADS * GLA_DV
GLA_RANK = 16
GLA_GATE_NORM = 16.0
GLA_CHUNK = 128
EPS = 1e-6
LANES = 128
PROJ_W = S5_WIDTH + 2 * GLA_KEY + GLA_WIDTH + LANES
GATE_W = 2 * S5_WIDTH + 2 * D_MODEL
VMEM_LIMIT = 52 * 1024 * 1024


def _bdot(a, b):
    return jnp.dot(a.astype(BF16), b.astype(BF16), preferred_element_type=F32)


def _split2(x):
    hi = x.astype(BF16)
    lo = (x - hi.astype(F32)).astype(BF16)
    return hi, lo


def _sigmoid(x):
    return 1.0 / (1.0 + jnp.exp(-x))


def _silu(x):
    return x * _sigmoid(x)


def _gelu_tanh(x):
    c = math.sqrt(2.0 / math.pi)
    return 0.5 * x * (1.0 + jnp.tanh(c * (x + 0.044715 * (x * x * x))))


def _exact_zero_of(x):
    bits = lax.bitcast_convert_type(x, jnp.int32)
    return lax.shift_right_logical(lax.shift_right_logical(bits, 16), 16).astype(F32)


def _log_sigmoid(x):
    return jnp.minimum(x, 0.0) - jnp.log(1.0 + jnp.exp(-jnp.abs(x)))


def _modulated_norm(x, ng, scale1p, shift):
    ms = jnp.mean(x * x, axis=-1, keepdims=True)
    return (x * lax.rsqrt(ms + EPS) * ng) * scale1p + shift


def _mod_kernel(cond_ref, w_ref, b_ref, out_ref):
    c = cond_ref[...]
    s_hi, s_lo = _split2(_silu(c))
    w_hi, w_lo = _split2(w_ref[0])
    acc = jnp.dot(s_hi, w_hi, preferred_element_type=F32)
    acc += jnp.dot(s_lo, w_hi, preferred_element_type=F32)
    acc += jnp.dot(s_hi, w_lo, preferred_element_type=F32)
    out_ref[0] = acc + b_ref[0]


def _mod_call(cond8, w_mod, b_mod):
    depth = w_mod.shape[0]
    nb = 768
    return pl.pallas_call(
        _mod_kernel,
        grid=(depth, 3 * D_MODEL // nb),
        in_specs=[
            pl.BlockSpec((8, D_MODEL), lambda l, j: (0, 0)),
            pl.BlockSpec((1, D_MODEL, nb), lambda l, j: (l, 0, j)),
            pl.BlockSpec((1, 1, nb), lambda l, j: (l, 0, j)),
        ],
        out_specs=pl.BlockSpec((1, 8, nb), lambda l, j: (l, 0, j)),
        out_shape=jax.ShapeDtypeStruct((depth, 8, 3 * D_MODEL), F32),
        compiler_params=pltpu.CompilerParams(dimension_semantics=("arbitrary", "arbitrary")),
        name="adaln_mod",
    )(cond8, w_mod, b_mod.reshape(depth, 1, 3 * D_MODEL))


def _zoh(lre, lim, ldt):
    dt = jnp.exp(ldt)
    mag = jnp.exp(lre * dt)
    a_re = mag * jnp.cos(lim * dt)
    a_im = mag * jnp.sin(lim * dt)
    n_re = a_re - 1.0
    inv = 1.0 / (lre * lre + lim * lim)
    return a_re, a_im, (n_re * lre + a_im * lim) * inv, (a_im * lre - n_re * lim) * inv


def _zoh_kernel(lre_ref, lim_ref, ldt_ref, lre_x_ref, lim_x_ref, ldt_x_ref, bre_ref, bim_ref, cre_ref, cim_ref,
                are_ref, aim_ref, wbu_ref, wc_ref, *, depth):
    a_re, a_im, _, _ = _zoh(lre_ref[...], lim_ref[...], ldt_ref[...])
    are_ref[...] = a_re
    aim_ref[...] = a_im
    _, _, c_re, c_im = _zoh(lre_x_ref[...], lim_x_ref[...], ldt_x_ref[...])
    bre = bre_ref[...]
    bim = bim_ref[...]
    bbar = [c_re * bre - c_im * bim, c_re * bim + c_im * bre]
    cmat = [cre_ref[...], -cim_ref[...]]
    wbu_ref[...] = jnp.zeros_like(wbu_ref)
    wc_ref[...] = jnp.zeros_like(wc_ref)
    for l in range(depth):
        for s in range(S5_SLABS):
            for gl in range(8):
                g = 8 * s + gl
                for ri in range(2):
                    col = ri * 8 * S5_STATE + gl * S5_STATE
                    for d in range(2):
                        src = ((l * 2 + d) * S5_GROUPS + g) * S5_CH
                        row = (d * 8 + gl) * S5_CH
                        wbu_ref[l, s, row:row + S5_CH, col:col + S5_STATE] = bbar[ri][src:src + S5_CH, :].astype(BF16)
                    src = (l * S5_GROUPS + g) * S5_STATE
                    wc_ref[l, s, col:col + S5_STATE, gl * S5_CH:(gl + 1) * S5_CH] = (
                        cmat[ri][src:src + S5_STATE, :].astype(BF16))


def _s5_params(s5_lam_re, s5_lam_im, s5_log_dt, s5_b_re, s5_b_im, s5_c_re, s5_c_im):
    depth = s5_lam_re.shape[0]
    full = (depth, 2, S5_GROUPS, S5_CH, S5_STATE)
    expand = lambda a: jnp.broadcast_to(a, full).reshape(-1, S5_STATE)
    small = lambda a: a.reshape(depth * 2 * S5_GROUPS, -1)
    ct = lambda a: jnp.swapaxes(a, -1, -2).reshape(-1, S5_CH)
    a_re, a_im, w_bu, w_c = pl.pallas_call(
        functools.partial(_zoh_kernel, depth=depth),
        out_shape=(jax.ShapeDtypeStruct((depth * 2 * S5_GROUPS, S5_STATE), F32),
                   jax.ShapeDtypeStruct((depth * 2 * S5_GROUPS, S5_STATE), F32),
                   jax.ShapeDtypeStruct((depth, S5_SLABS, 2 * LANES, 2 * 8 * S5_STATE), BF16),
                   jax.ShapeDtypeStruct((depth, S5_SLABS, 2 * 8 * S5_STATE, LANES), BF16)),
        name="s5_zoh",
    )(small(s5_lam_re), small(s5_lam_im), small(s5_log_dt),
      expand(s5_lam_re[:, :, :, None, :]), expand(s5_lam_im[:, :, :, None, :]), expand(s5_log_dt[:, :, :, None, None]),
      expand(jnp.swapaxes(s5_b_re, -1, -2)[:, None]), expand(jnp.swapaxes(s5_b_im, -1, -2)[:, None]),
      ct(s5_c_re), ct(s5_c_im))
    return a_re.reshape(depth, 2, S5_NC), a_im.reshape(depth, 2, S5_NC), w_bu, w_c


def _s5_state_to_cols(re, im):
    lead = re.shape[:-2]
    st = jnp.stack([re.reshape(lead + (S5_SLABS, 512)), im.reshape(lead + (S5_SLABS, 512))], axis=-2)
    return st.reshape(lead + (2 * S5_NC,))


def _s5_cols_to_state(cols):
    lead = cols.shape[:-1]
    st = cols.reshape(lead + (S5_SLABS, 2, 512))
    re = st[..., 0, :].reshape(lead + (S5_GROUPS, S5_STATE))
    im = st[..., 1, :].reshape(lead + (S5_GROUPS, S5_STATE))
    return re, im


_IN_OFFS = [int(o) for o in np.cumsum([0, S5_WIDTH, S5_WIDTH, GLA_KEY, GLA_KEY, GLA_WIDTH, GLA_WIDTH, GLA_RANK,
                                       D_MODEL, D_MODEL])]


def _split_w_in_kernel(w_ref, proj_ref, gate_ref):
    o = _IN_OFFS
    lane = lax.broadcasted_iota(jnp.int32, (1, LANES), 1)
    code = jnp.where(lane < GLA_RANK, w_ref[0, :, o[6]:o[6] + LANES], 0.0)
    proj_ref[0] = jnp.concatenate([w_ref[0, :, o[0]:o[1]], w_ref[0, :, o[2]:o[5]], code], axis=1).astype(BF16)
    gate_ref[0] = jnp.concatenate([w_ref[0, :, o[1]:o[2]], w_ref[0, :, o[5]:o[6]], w_ref[0, :, o[7]:o[9]]],
                                  axis=1).astype(BF16)


def _split_w_in(w_in):
    depth, d_model, d_in = w_in.shape
    tr = 256
    return pl.pallas_call(
        _split_w_in_kernel,
        grid=(depth, d_model // tr),
        in_specs=[pl.BlockSpec((1, tr, d_in), lambda l, j: (l, j, 0))],
        out_specs=(pl.BlockSpec((1, tr, PROJ_W), lambda l, j: (l, j, 0)),
                   pl.BlockSpec((1, tr, GATE_W), lambda l, j: (l, j, 0))),
        out_shape=(jax.ShapeDtypeStruct((depth, d_model, PROJ_W), BF16),
                   jax.ShapeDtypeStruct((depth, d_model, GATE_W), BF16)),
        compiler_params=pltpu.CompilerParams(dimension_semantics=("arbitrary", "arbitrary")),
        name="split_w_in",
    )(w_in)


def _proj_kernel(*refs, has_pe):
    if has_pe:
        x_ref, pe_ref, shift_ref, scale_ref, ng_ref, w_ref, u_ref, q_ref, k_ref, v_ref, gl_ref = refs
        x = x_ref[0] + pe_ref[...]
    else:
        x_ref, shift_ref, scale_ref, ng_ref, w_ref, u_ref, q_ref, k_ref, v_ref, gl_ref = refs
        x = x_ref[0]
    h = _modulated_norm(x, ng_ref[...], 1.0 + scale_ref[...], shift_ref[...])
    p = _bdot(h, w_ref[...])
    o = 0
    u_ref[0] = p[:, o:o + S5_WIDTH]
    o += S5_WIDTH
    q_ref[0] = p[:, o:o + GLA_KEY]
    o += GLA_KEY
    k_ref[0] = p[:, o:o + GLA_KEY]
    o += GLA_KEY
    v_ref[0] = p[:, o:o + GLA_WIDTH].astype(BF16)
    o += GLA_WIDTH
    gl_ref[0] = p[:, o:o + LANES]


def _mod_spec(l, cond_row, part):
    return pl.BlockSpec((None, None, 1, D_MODEL), lambda b, j: (l, cond_row(b), 0, part))


def _layer_spec(l, shape):
    return pl.BlockSpec((None,) + shape, lambda b, j: (l,) + (0,) * len(shape))


def _proj_call(x, pe, mod, l, cond_row, ng, w, tm):
    nseq, length, _ = x.shape
    row = lambda width: pl.BlockSpec((1, tm, width), lambda b, j: (b, j, 0))
    in_specs = [row(D_MODEL)]
    args = [x]
    if pe is not None:
        in_specs.append(pl.BlockSpec((tm, D_MODEL), lambda b, j: (j, 0)))
        args.append(pe)
    in_specs += [_mod_spec(l, cond_row, 0), _mod_spec(l, cond_row, 1), _layer_spec(l, (1, D_MODEL)),
                 _layer_spec(l, (D_MODEL, PROJ_W))]
    args += [mod, mod, ng, w]
    sds = lambda width, dt: jax.ShapeDtypeStruct((nseq, length, width), dt)
    return pl.pallas_call(
        functools.partial(_proj_kernel, has_pe=pe is not None),
        grid=(nseq, length // tm),
        in_specs=in_specs,
        out_specs=(row(S5_WIDTH), row(GLA_KEY), row(GLA_KEY), row(GLA_WIDTH), row(LANES)),
        out_shape=(sds(S5_WIDTH, F32), sds(GLA_KEY, F32), sds(GLA_KEY, F32), sds(GLA_WIDTH, BF16), sds(LANES, F32)),
        compiler_params=pltpu.CompilerParams(dimension_semantics=("arbitrary", "arbitrary"),
                                             vmem_limit_bytes=VMEM_LIMIT),
        name="proj",
    )(*args)


def _s5_perms(nb, tc):
    r = nb * tc
    pf = np.zeros((2 * r, r), np.float32)
    pb = np.zeros((2 * r, r), np.float32)
    for t in range(tc):
        for b in range(nb):
            pf[t * 2 * nb + b, b * tc + t] = 1.0
            pb[t * 2 * nb + nb + b, b * tc + (tc - 1 - t)] = 1.0
    return pf, pb


def _s5_kernel(uf1_ref, ub1_ref, uf2_ref, ub2_ref, uf0_ref, ub0_ref, pf_ref, pb_ref, pft_ref, pbt_ref, wbu_ref, wc_ref,
               are_ref, aim_ref, h0_ref, yf_ref, yb_ref, hfin_ref, bua_ref, bub_ref, hb_ref, st_ref,
               *, nb, tc, tiles_per_pass):
    i = pl.program_id(0)
    rows = nb * tc
    rs = 2 * nb

    def packed_inputs(uf_ref, ub_ref):
        uf = uf_ref[...].reshape(rows, S5_WIDTH).astype(BF16)
        ub = ub_ref[...].reshape(rows, S5_WIDTH).astype(BF16)
        u2f = jnp.dot(pf_ref[...], uf, preferred_element_type=F32).astype(BF16)
        u2b = jnp.dot(pb_ref[...], ub, preferred_element_type=F32).astype(BF16)
        return [jnp.concatenate([u2f[:, s * LANES:(s + 1) * LANES], u2b[:, s * LANES:(s + 1) * LANES]], axis=1)
                for s in range(S5_SLABS)]

    def bu_slab(lhs, s, dst_ref):
        dst_ref[:, s * 1024:(s + 1) * 1024] = jnp.dot(lhs[s], wbu_ref[s], preferred_element_type=F32)

    @pl.when(i == 0)
    def _():
        st_ref[...] = h0_ref[...]
        lhs0 = packed_inputs(uf0_ref, ub0_ref)
        for s in range(S5_SLABS):
            bu_slab(lhs0, s, bua_ref)

    half_step = functools.partial(_s5_half_step, pft_ref=pft_ref, pbt_ref=pbt_ref, wc_ref=wc_ref, are_ref=are_ref,
                                  aim_ref=aim_ref, hb_ref=hb_ref, st_ref=st_ref, bu_slab=bu_slab, rs=rs, tc=tc,
                                  tiles_per_pass=tiles_per_pass)
    yf, yb, tail = half_step(bua_ref, bub_ref, packed_inputs(uf1_ref, ub1_ref))
    yf_ref[:, 0:tc, :] = yf.reshape(nb, tc, S5_WIDTH)
    yb_ref[:, tc:2 * tc, :] = yb.reshape(nb, tc, S5_WIDTH)
    yf, yb, _ = half_step(bub_ref, bua_ref, packed_inputs(uf2_ref, ub2_ref), after=tail)
    yf_ref[:, tc:2 * tc, :] = yf.reshape(nb, tc, S5_WIDTH)
    yb_ref[:, 0:tc, :] = yb.reshape(nb, tc, S5_WIDTH)
    hfin_ref[...] = st_ref[...]


def _s5_half_step(cur_ref, nxt_ref, lhs_next, *, pft_ref, pbt_ref, wc_ref, are_ref, aim_ref, hb_ref, st_ref, bu_slab,
                  rs, tc, tiles_per_pass, after=None):
    grp = max(1, 16 // rs)
    ys = []
    for s in range(S5_SLABS):
        bu_slab(lhs_next, s, nxt_ref)
        for c0 in range(4 * s, 4 * s + 4, tiles_per_pass):
            cts = list(range(c0, c0 + tiles_per_pass))
            cre = [(ct // 4) * 1024 + (ct % 4) * LANES for ct in cts]
            cim = [c + 512 for c in cre]
            a_re = [are_ref[:, ct * LANES:(ct + 1) * LANES] for ct in cts]
            a_im = [aim_ref[:, ct * LANES:(ct + 1) * LANES] for ct in cts]
            h_re = [st_ref[:, c:c + LANES] for c in cre]
            h_im = [st_ref[:, c:c + LANES] for c in cim]
            if after is not None and c0 == 0:
                h_re[0] = h_re[0] + _exact_zero_of(after)
            for t0 in range(0, tc, grp):
                out_re = [[] for _ in cts]
                out_im = [[] for _ in cts]
                for t in range(t0, t0 + grp):
                    r0 = t * rs
                    for j in range(len(cts)):
                        b_re = cur_ref[r0:r0 + rs, cre[j]:cre[j] + LANES]
                        b_im = cur_ref[r0:r0 + rs, cim[j]:cim[j] + LANES]
                        n_re = a_re[j] * h_re[j] - a_im[j] * h_im[j] + b_re
                        n_im = a_re[j] * h_im[j] + a_im[j] * h_re[j] + b_im
                        h_re[j], h_im[j] = n_re, n_im
                        out_re[j].append(n_re)
                        out_im[j].append(n_im)
                for j in range(len(cts)):
                    blk_re = out_re[j][0] if grp == 1 else jnp.concatenate(out_re[j], axis=0)
                    blk_im = out_im[j][0] if grp == 1 else jnp.concatenate(out_im[j], axis=0)
                    hb_ref[t0 * rs:(t0 + grp) * rs, cre[j]:cre[j] + LANES] = blk_re.astype(BF16)
                    hb_ref[t0 * rs:(t0 + grp) * rs, cim[j]:cim[j] + LANES] = blk_im.astype(BF16)
            for j in range(len(cts)):
                st_ref[:, cre[j]:cre[j] + LANES] = h_re[j]
                st_ref[:, cim[j]:cim[j] + LANES] = h_im[j]
        ys.append(jnp.dot(hb_ref[:, s * 1024:(s + 1) * 1024], wc_ref[s], preferred_element_type=F32))
    y2 = jnp.concatenate(ys, axis=1)
    y_hi, y_lo = _split2(y2)
    yf = jnp.dot(pft_ref[...], y_hi, preferred_element_type=F32) + jnp.dot(pft_ref[...], y_lo, preferred_element_type=F32)
    yb = jnp.dot(pbt_ref[...], y_hi, preferred_element_type=F32) + jnp.dot(pbt_ref[...], y_lo, preferred_element_type=F32)
    return yf, yb, h_re[-1]


def _s5_call(u, l, w_bu, w_c, are, aim, h0, tc, tiles_per_pass):
    nb, length, _ = u.shape
    n = length // tc
    rows = nb * tc
    pf, pb = _s5_perms(nb, tc)
    const = lambda shape: pl.BlockSpec(shape, lambda i: (0,) * len(shape))
    layer = lambda shape: pl.BlockSpec((None,) + shape, lambda i: (l,) + (0,) * len(shape))
    blk = lambda index_map: pl.BlockSpec((nb, tc, S5_WIDTH), index_map)
    out_f = pl.BlockSpec((nb, 2 * tc, S5_WIDTH), lambda i: (0, i, 0))
    out_b = pl.BlockSpec((nb, 2 * tc, S5_WIDTH), lambda i: (0, n // 2 - 1 - i, 0))
    second = lambda i: jnp.minimum(2 * i + 2, n - 1)
    return pl.pallas_call(
        functools.partial(_s5_kernel, nb=nb, tc=tc, tiles_per_pass=tiles_per_pass),
        grid=(n // 2,),
        in_specs=[blk(lambda i: (0, 2 * i + 1, 0)), blk(lambda i: (0, n - 2 - 2 * i, 0)),
                  blk(lambda i: (0, second(i), 0)), blk(lambda i: (0, n - 1 - second(i), 0)),
                  blk(lambda i: (0, 0, 0)), blk(lambda i: (0, n - 1, 0)),
                  const((2 * rows, rows)), const((2 * rows, rows)), const((rows, 2 * rows)), const((rows, 2 * rows)),
                  layer((S5_SLABS, 2 * LANES, 1024)), layer((S5_SLABS, 1024, LANES)),
                  layer((2 * nb, S5_NC)), layer((2 * nb, S5_NC)), const((2 * nb, 2 * S5_NC))],
        out_specs=(out_f, out_b, const((2 * nb, 2 * S5_NC))),
        out_shape=(jax.ShapeDtypeStruct(u.shape, F32), jax.ShapeDtypeStruct(u.shape, F32),
                   jax.ShapeDtypeStruct((2 * nb, 2 * S5_NC), F32)),
        scratch_shapes=[pltpu.VMEM((2 * rows, 2 * S5_NC), F32), pltpu.VMEM((2 * rows, 2 * S5_NC), F32),
                        pltpu.VMEM((2 * rows, 2 * S5_NC), BF16), pltpu.VMEM((2 * nb, 2 * S5_NC), F32)],
        compiler_params=pltpu.CompilerParams(dimension_semantics=("arbitrary",), vmem_limit_bytes=VMEM_LIMIT),
        name="s5_scan",
    )(u, u, u, u, u, u, jnp.asarray(pf, BF16), jnp.asarray(pb, BF16), jnp.asarray(pf.T, BF16), jnp.asarray(pb.T, BF16),
      w_bu, w_c, are, aim, h0)


def _gla_kernel(*refs, length, has_s0):
    q_ref, k_ref, v_ref, gl_ref, wg_ref, bg_ref, ng_ref = refs[:7]
    s0_ref = refs[7] if has_s0 else None
    o_ref, sfin_ref, s_ref = refs[-3:]
    c = GLA_CHUNK
    n = length // c
    ri = lax.broadcasted_iota(jnp.int32, (c, c), 0)
    ci = lax.broadcasted_iota(jnp.int32, (c, c), 1)
    lane = lax.broadcasted_iota(jnp.int32, (1, GLA_KEY), 1)
    head_mask = [((lane >= h * GLA_DK) & (lane < (h + 1) * GLA_DK)).astype(F32) for h in range(GLA_HEADS)]
    srow = lax.shift_right_logical(lax.broadcasted_iota(jnp.int32, (GLA_KEY, GLA_WIDTH), 0), 6)
    scol = lax.shift_right_logical(lax.broadcasted_iota(jnp.int32, (GLA_KEY, GLA_WIDTH), 1), 7)
    bd_mask = (srow == scol).astype(F32)
    ng = ng_ref[...]
    causal = [ci <= ri, ci >= ri]
    tri = [m.astype(BF16) for m in causal]

    s_ref[...] = jnp.zeros_like(s_ref)
    if has_s0:
        for d in range(2):
            for h in range(GLA_HEADS):
                s_ref[d, h * GLA_DK:(h + 1) * GLA_DK, h * GLA_DV:(h + 1) * GLA_DV] = s0_ref[0, d, h]

    def chunk_pair(r):
        dirs = (0, 1)
        end = (c - 1, 0)
        qc = [q_ref[0, pl.ds(r[d], c), :] * (GLA_DK ** -0.5) for d in dirs]
        kc = [k_ref[0, pl.ds(r[d], c), :] for d in dirs]
        vc = [v_ref[0, pl.ds(r[d], c), :] for d in dirs]
        logits = [_bdot(gl_ref[0, pl.ds(r[d], c), :], wg_ref[d]) + bg_ref[d] for d in dirs]
        g = [_split2(_log_sigmoid(logits[d]) * (1.0 / GLA_GATE_NORM)) for d in dirs]
        b = [jnp.dot(tri[d], g[d][0], preferred_element_type=F32) + jnp.dot(tri[d], g[d][1], preferred_element_type=F32)
             for d in dirs]
        bt = [b[d].T for d in dirs]
        kt = [kc[d].T for d in dirs]
        b_mid = [b[d][c // 2:c // 2 + 1, :] for d in dirs]
        bt_mid = [bt[d][:, c // 2:c // 2 + 1] for d in dirs]
        bt_end = [bt[d][:, end[d]:end[d] + 1] for d in dirs]
        q_in = [(qc[d] * jnp.exp(b[d])).astype(BF16) for d in dirs]
        q_mid = [qc[d] * jnp.exp(b[d] - b_mid[d]) for d in dirs]
        k_mid_t = [(kt[d] * jnp.exp(bt_mid[d] - bt[d])).astype(BF16) for d in dirs]
        k_end_t = [(kt[d] * jnp.exp(bt_end[d] - bt[d])).astype(BF16) for d in dirs]
        q_st = [jnp.concatenate([q_mid[d] * head_mask[h] for h in range(GLA_HEADS)], axis=0).astype(BF16) for d in dirs]
        a = [jnp.dot(q_st[d], k_mid_t[d], preferred_element_type=F32) for d in dirs]
        s_old = [s_ref[d] for d in dirs]
        o_inter = [jnp.dot(q_in[d], s_old[d].astype(BF16), preferred_element_type=F32) for d in dirs]
        parts = [[], []]
        for h in range(GLA_HEADS):
            for d in dirs:
                a_h = jnp.where(causal[d], a[d][h * c:(h + 1) * c, :], 0.0).astype(BF16)
                parts[d].append(jnp.dot(a_h, vc[d][:, h * GLA_DV:(h + 1) * GLA_DV], preferred_element_type=F32))
        upd = [jnp.dot(k_end_t[d], vc[d], preferred_element_type=F32) for d in dirs]
        for d in dirs:
            s_ref[d] = jnp.exp(bt_end[d]) * s_old[d] + upd[d] * bd_mask
        return [o_inter[d] + jnp.concatenate(parts[d], axis=1) for d in dirs]

    def finish(o, r0):
        o = o + o_ref[0, pl.ds(r0, c), :]
        normed = []
        for h in range(GLA_HEADS):
            oh = o[:, h * GLA_DV:(h + 1) * GLA_DV]
            ms = jnp.mean(oh * oh, axis=-1, keepdims=True)
            normed.append(oh * lax.rsqrt(ms + EPS))
        o_ref[0, pl.ds(r0, c), :] = jnp.concatenate(normed, axis=1) * ng

    def first_half(step, carry):
        rf = pl.multiple_of(step * c, c)
        rb = pl.multiple_of((n - 1 - step) * c, c)
        o_f, o_b = chunk_pair((rf, rb))
        o_ref[0, pl.ds(rf, c), :] = o_f
        o_ref[0, pl.ds(rb, c), :] = o_b
        return carry

    def second_half(step, carry):
        rf = pl.multiple_of(step * c, c)
        rb = pl.multiple_of((n - 1 - step) * c, c)
        o_f, o_b = chunk_pair((rf, rb))
        finish(o_f, rf)
        finish(o_b, rb)
        return carry

    lax.fori_loop(0, n // 2, first_half, 0)
    lax.fori_loop(n // 2, n, second_half, 0)
    for d in range(2):
        for h in range(GLA_HEADS):
            sfin_ref[0, d, h] = s_ref[d, h * GLA_DK:(h + 1) * GLA_DK, h * GLA_DV:(h + 1) * GLA_DV]


def _gla_call(q, k, v, gl, l, wg, bg, ng, s0):
    nseq, length, _ = q.shape
    seq = lambda width: pl.BlockSpec((1, length, width), lambda b: (b, 0, 0))
    layer = lambda shape: pl.BlockSpec((None,) + shape, lambda b: (l,) + (0,) * len(shape))
    in_specs = [seq(GLA_KEY), seq(GLA_KEY), seq(GLA_WIDTH), seq(LANES),
                layer((2, LANES, GLA_KEY)), layer((2, 1, GLA_KEY)), layer((1, GLA_WIDTH))]
    args = [q, k, v, gl, wg, bg, ng]
    if s0 is not None:
        in_specs.append(pl.BlockSpec((1, None, 2, GLA_HEADS, GLA_DK, GLA_DV), lambda b: (b, l, 0, 0, 0, 0)))
        args.append(s0)
    return pl.pallas_call(
        functools.partial(_gla_kernel, length=length, has_s0=s0 is not None),
        grid=(nseq,),
        in_specs=in_specs,
        out_specs=(seq(GLA_WIDTH), pl.BlockSpec((1, 2, GLA_HEADS, GLA_DK, GLA_DV), lambda b: (b, 0, 0, 0, 0))),
        out_shape=(jax.ShapeDtypeStruct((nseq, length, GLA_WIDTH), F32),
                   jax.ShapeDtypeStruct((nseq, 2, GLA_HEADS, GLA_DK, GLA_DV), F32)),
        scratch_shapes=[pltpu.VMEM((2, GLA_KEY, GLA_WIDTH), F32)],
        compiler_params=pltpu.CompilerParams(dimension_semantics=("arbitrary",), vmem_limit_bytes=VMEM_LIMIT),
        name="gla",
    )(*args)


def _mix_kernel(*refs, has_pe, last):
    if has_pe:
        (x_ref, pe_ref, yf_ref, yb_ref, u_ref, o_ref, shift_ref, scale_ref, gate_ref, ng_ref, wg_ref, d_ref,
         wglu_ref, bglu_ref, wpa_ref, wpb_ref, wo_ref, fng_ref, out_ref) = refs
        x = x_ref[0] + pe_ref[...]
    else:
        (x_ref, yf_ref, yb_ref, u_ref, o_ref, shift_ref, scale_ref, gate_ref, ng_ref, wg_ref, d_ref,
         wglu_ref, bglu_ref, wpa_ref, wpb_ref, wo_ref, fng_ref, out_ref) = refs
        x = x_ref[0]
    h = _modulated_norm(x, ng_ref[...], 1.0 + scale_ref[...], shift_ref[...])
    gts = _bdot(h, wg_ref[...])
    gate_a = gts[:, :S5_WIDTH]
    gate_b = gts[:, S5_WIDTH:2 * S5_WIDTH]
    m_a = gts[:, 2 * S5_WIDTH:2 * S5_WIDTH + D_MODEL]
    m_b = gts[:, 2 * S5_WIDTH + D_MODEL:]
    y = _gelu_tanh(yf_ref[0] + yb_ref[0] + d_ref[...] * u_ref[0])
    y = y * _sigmoid(_bdot(y, wglu_ref[...]) + bglu_ref[...])
    y_a = y * _silu(gate_a)
    y_b = o_ref[0] * _silu(gate_b)
    merged = _sigmoid(m_a) * _bdot(y_a, wpa_ref[...]) + _sigmoid(m_b) * _bdot(y_b, wpb_ref[...])
    xn = x + gate_ref[...] * _bdot(merged, wo_ref[...])
    if last:
        ms = jnp.mean(xn * xn, axis=-1, keepdims=True)
        xn = xn * lax.rsqrt(ms + EPS) * fng_ref[...]
    out_ref[0] = xn


def _mix_call(x, pe, yf, yb, u, o, mod, l, cond_row, ng, wg, s5_d, wglu, bglu, wpa, wpb, wo, fng, tm, last):
    nseq, length, _ = x.shape
    row = lambda width: pl.BlockSpec((1, tm, width), lambda b, j: (b, j, 0))
    in_specs = [row(D_MODEL)]
    args = [x]
    if pe is not None:
        in_specs.append(pl.BlockSpec((tm, D_MODEL), lambda b, j: (j, 0)))
        args.append(pe)
    in_specs += [row(S5_WIDTH), row(S5_WIDTH), row(S5_WIDTH), row(GLA_WIDTH),
                 _mod_spec(l, cond_row, 0), _mod_spec(l, cond_row, 1), _mod_spec(l, cond_row, 2),
                 _layer_spec(l, (1, D_MODEL)), _layer_spec(l, (D_MODEL, GATE_W)), _layer_spec(l, (1, S5_WIDTH)),
                 _layer_spec(l, (S5_WIDTH, S5_WIDTH)), _layer_spec(l, (1, S5_WIDTH)),
                 _layer_spec(l, (S5_WIDTH, D_MODEL)), _layer_spec(l, (GLA_WIDTH, D_MODEL)),
                 _layer_spec(l, (D_MODEL, D_MODEL)), pl.BlockSpec((1, D_MODEL), lambda b, j: (0, 0))]
    args += [yf, yb, u, o, mod, mod, mod, ng, wg, s5_d, wglu, bglu, wpa, wpb, wo, fng]
    return pl.pallas_call(
        functools.partial(_mix_kernel, has_pe=pe is not None, last=last),
        grid=(nseq, length // tm),
        in_specs=in_specs,
        out_specs=row(D_MODEL),
        out_shape=jax.ShapeDtypeStruct(x.shape, F32),
        compiler_params=pltpu.CompilerParams(dimension_semantics=("arbitrary", "arbitrary"),
                                             vmem_limit_bytes=VMEM_LIMIT),
        name="mix",
    )(*args)


def _grid_pos_embed(length, dim):
    rows = length // GRID_W
    quarter = dim // 4
    freqs = jnp.exp(-math.log(10000.0) * jnp.arange(quarter, dtype=F32) / quarter)

    def sincos(pos):
        ang = pos.astype(F32)[:, None] * freqs[None, :]
        return jnp.concatenate([jnp.sin(ang), jnp.cos(ang)], axis=-1)

    er = sincos(jnp.arange(rows))
    ec = sincos(jnp.arange(GRID_W))
    pe = jnp.concatenate([jnp.broadcast_to(er[:, None, :], (rows, GRID_W, dim // 2)),
                          jnp.broadcast_to(ec[None, :, :], (rows, GRID_W, dim // 2))], axis=-1)
    return pe.reshape(rows * GRID_W, dim)


def kernel(x_prompt, x_sample, c, state_s5_re, state_s5_im, state_gla, c_ctx, norm_g, w_mod, b_mod, w_in, gla_wg_up,
           gla_bg, gla_norm_g, s5_lam_re, s5_lam_im, s5_log_dt, s5_b_re, s5_b_im, s5_c_re, s5_c_im, s5_d, w_glu,
           b_glu, w_pa, w_pb, w_o, final_norm_g):
    depth = w_in.shape[0]
    bp, lp, _ = x_prompt.shape
    bs, ls, _ = x_sample.shape

    cond8 = jnp.zeros((8, D_MODEL), F32).at[0].set(c_ctx).at[1:1 + bs].set(c)
    mod = _mod_call(cond8, w_mod, b_mod)

    a_re, a_im, w_bu, w_c = _s5_params(s5_lam_re, s5_lam_im, s5_log_dt, s5_b_re, s5_b_im, s5_c_re, s5_c_im)

    w_proj, w_gate = _split_w_in(w_in)
    wg_up = jnp.concatenate([gla_wg_up, jnp.zeros((depth, 2, LANES - GLA_RANK, GLA_KEY), F32)], axis=2).astype(BF16)
    w_glu_b, w_pa_b, w_pb_b, w_o_b = (w.astype(BF16) for w in (w_glu, w_pa, w_pb, w_o))

    vec = lambda a: a[:, None, :]
    mod = mod.reshape(depth, 8, 1, 3 * D_MODEL)
    ng, gng, bg = vec(norm_g), vec(gla_norm_g), gla_bg[:, :, None, :]
    common = (ng, w_gate, vec(s5_d), w_glu_b, vec(b_glu), w_pa_b, w_pb_b, w_o_b, final_norm_g[None])
    are_p, aim_p = jnp.repeat(a_re, bp, axis=1), jnp.repeat(a_im, bp, axis=1)
    are_s, aim_s = jnp.repeat(a_re, bs, axis=1), jnp.repeat(a_im, bs, axis=1)
    ctx_row = lambda b: 0
    cond_row = lambda b: b + 1

    pe = _grid_pos_embed(ls, D_MODEL)
    zero_h0 = jnp.zeros((2 * bp, 2 * S5_NC), F32)
    xp, xs = x_prompt, x_sample
    new_re, new_im, new_gla = [], [], []
    for l in range(depth):
        last = l == depth - 1

        u, q, k, v, gl = _proj_call(xp, None, mod, l, ctx_row, ng, w_proj, tm=lp)
        yf, yb, hfin = _s5_call(u, l, w_bu, w_c, are_p, aim_p, zero_h0, tc=8, tiles_per_pass=2)
        o, sfin = _gla_call(q, k, v, gl, l, wg_up, bg, gng, None)
        new_gla.append(sfin)
        xp = _mix_call(xp, None, yf, yb, u, o, mod, l, ctx_row, *common, tm=lp, last=last)
        re_l, im_l = _s5_cols_to_state(hfin.reshape(2, bp, 2 * S5_NC))
        new_re.append(jnp.swapaxes(re_l, 0, 1))
        new_im.append(jnp.swapaxes(im_l, 0, 1))

        pe_l = pe if l == 0 else None
        u, q, k, v, gl = _proj_call(xs, pe_l, mod, l, cond_row, ng, w_proj, tm=512)
        h0 = _s5_state_to_cols(jnp.swapaxes(state_s5_re[:, l], 0, 1), jnp.swapaxes(state_s5_im[:, l], 0, 1))
        yf, yb, _ = _s5_call(u, l, w_bu, w_c, are_s, aim_s, h0.reshape(2 * bs, 2 * S5_NC), tc=32, tiles_per_pass=4)
        o, _ = _gla_call(q, k, v, gl, l, wg_up, bg, gng, state_gla)
        xs = _mix_call(xs, pe_l, yf, yb, u, o, mod, l, cond_row, *common, tm=512, last=last)

    return (xp, xs, jnp.stack(new_re, axis=1), jnp.stack(new_im, axis=1), jnp.stack(new_gla, axis=1))
```

```python
import functools
import math

import numpy as np
import jax
import jax.numpy as jnp
from jax import lax
from jax.experimental import pallas as pl
from jax.experimental.pallas import tpu as pltpu

F32 = jnp.float32
BF16 = jnp.bfloat16

D_MODEL = 1024
GRID_W = 64
S5_WIDTH = 512
S5_GROUPS = 32
S5_CH = 16
S5_STATE = 64
S5_NC = S5_GROUPS * S5_STATE
S5_SLABS = 4
GLA_HEADS = 4
GLA_DK = 64
GLA_DV = 128
GLA_KEY = GLA_HEADS * GLA_DK
GLA_WIDTH = GLA_HEADS * GLA_DV
GLA_RANK = 16
GLA_GATE_NORM = 16.0
GLA_CHUNK = 128
EPS = 1e-6
LANES = 128
PROJ_W = S5_WIDTH + 2 * GLA_KEY + GLA_WIDTH + LANES
GATE_W = 2 * S5_WIDTH + 2 * D_MODEL
VMEM_LIMIT = 52 * 1024 * 1024


def _bdot(a, b):
    return jnp.dot(a.astype(BF16), b.astype(BF16), preferred_element_type=F32)


def _bdot_t(a, bt):
    return lax.dot_general(a.astype(BF16), bt.astype(BF16), (((1,), (1,)), ((), ())), preferred_element_type=F32)


def _split2(x):
    hi = x.astype(BF16)
    lo = (x - hi.astype(F32)).astype(BF16)
    return hi, lo


def _sigmoid(x):
    return 1.0 / (1.0 + jnp.exp(-x))


def _silu(x):
    return x * _sigmoid(x)


def _gelu_tanh(x):
    c = math.sqrt(2.0 / math.pi)
    return 0.5 * x * (1.0 + jnp.tanh(c * (x + 0.044715 * (x * x * x))))


def _exact_zero_of(x):
    bits = lax.bitcast_convert_type(x, jnp.int32)
    return lax.shift_right_logical(lax.shift_right_logical(bits, 16), 16).astype(F32)


def _log_sigmoid(x):
    return jnp.minimum(x, 0.0) - jnp.log(1.0 + jnp.exp(-jnp.abs(x)))


def _modulated_norm(x, ng, scale1p, shift):
    ms = jnp.mean(x * x, axis=-1, keepdims=True)
    return (x * lax.rsqrt(ms + EPS) * ng) * scale1p + shift


def _mod_kernel(cond_ref, w_ref, b_ref, out_ref):
    c = cond_ref[...]
    s_hi, s_lo = _split2(_silu(c))
    w_hi, w_lo = _split2(w_ref[0])
    acc = jnp.dot(s_hi, w_hi, preferred_element_type=F32)
    acc += jnp.dot(s_lo, w_hi, preferred_element_type=F32)
    acc += jnp.dot(s_hi, w_lo, preferred_element_type=F32)
    out_ref[0] = acc + b_ref[0]


def _mod_call(cond8, w_mod, b_mod):
    depth = w_mod.shape[0]
    nb = 768
    return pl.pallas_call(
        _mod_kernel,
        grid=(depth, 3 * D_MODEL // nb),
        in_specs=[
            pl.BlockSpec((8, D_MODEL), lambda l, j: (0, 0)),
            pl.BlockSpec((1, D_MODEL, nb), lambda l, j: (l, 0, j)),
            pl.BlockSpec((1, 1, nb), lambda l, j: (l, 0, j)),
        ],
        out_specs=pl.BlockSpec((1, 8, nb), lambda l, j: (l, 0, j)),
        out_shape=jax.ShapeDtypeStruct((depth, 8, 3 * D_MODEL), F32),
        compiler_params=pltpu.CompilerParams(dimension_semantics=("arbitrary", "arbitrary")),
        name="adaln_mod",
    )(cond8, w_mod, b_mod.reshape(depth, 1, 3 * D_MODEL))


def _zoh(lre, lim, ldt):
    dt = jnp.exp(ldt)
    mag = jnp.exp(lre * dt)
    a_re = mag * jnp.cos(lim * dt)
    a_im = mag * jnp.sin(lim * dt)
    n_re = a_re - 1.0
    inv = 1.0 / (lre * lre + lim * lim)
    return a_re, a_im, (n_re * lre + a_im * lim) * inv, (a_im * lre - n_re * lim) * inv


def _zoh_kernel(lre_ref, lim_ref, ldt_ref, lre_x_ref, lim_x_ref, ldt_x_ref, bre_ref, bim_ref, cre_ref, cim_ref,
                are_ref, aim_ref, wbu_ref, wc_ref, *, depth):
    a_re, a_im, _, _ = _zoh(lre_ref[...], lim_ref[...], ldt_ref[...])
    are_ref[...] = a_re
    aim_ref[...] = a_im
    _, _, c_re, c_im = _zoh(lre_x_ref[...], lim_x_ref[...], ldt_x_ref[...])
    bre = bre_ref[...]
    bim = bim_ref[...]
    bbar = [c_re * bre - c_im * bim, c_re * bim + c_im * bre]
    cmat = [cre_ref[...], -cim_ref[...]]
    wbu_ref[...] = jnp.zeros_like(wbu_ref)
    wc_ref[...] = jnp.zeros_like(wc_ref)
    for l in range(depth):
        for s in range(S5_SLABS):
            for gl in range(8):
                g = 8 * s + gl
                for ri in range(2):
                    col = ri * 8 * S5_STATE + gl * S5_STATE
                    for d in range(2):
                        src = ((l * 2 + d) * S5_GROUPS + g) * S5_CH
                        row = (d * 8 + gl) * S5_CH
                        wbu_ref[l, s, row:row + S5_CH, col:col + S5_STATE] = bbar[ri][src:src + S5_CH, :].astype(BF16)
                    src = (l * S5_GROUPS + g) * S5_STATE
                    wc_ref[l, s, col:col + S5_STATE, gl * S5_CH:(gl + 1) * S5_CH] = (
                        cmat[ri][src:src + S5_STATE, :].astype(BF16))


def _s5_params(s5_lam_re, s5_lam_im, s5_log_dt, s5_b_re, s5_b_im, s5_c_re, s5_c_im):
    depth = s5_lam_re.shape[0]
    full = (depth, 2, S5_GROUPS, S5_CH, S5_STATE)
    expand = lambda a: jnp.broadcast_to(a, full).reshape(-1, S5_STATE)
    small = lambda a: a.reshape(depth * 2 * S5_GROUPS, -1)
    ct = lambda a: jnp.swapaxes(a, -1, -2).reshape(-1, S5_CH)
    a_re, a_im, w_bu, w_c = pl.pallas_call(
        functools.partial(_zoh_kernel, depth=depth),
        out_shape=(jax.ShapeDtypeStruct((depth * 2 * S5_GROUPS, S5_STATE), F32),
                   jax.ShapeDtypeStruct((depth * 2 * S5_GROUPS, S5_STATE), F32),
                   jax.ShapeDtypeStruct((depth, S5_SLABS, 2 * LANES, 2 * 8 * S5_STATE), BF16),
                   jax.ShapeDtypeStruct((depth, S5_SLABS, 2 * 8 * S5_STATE, LANES), BF16)),
        name="s5_zoh",
    )(small(s5_lam_re), small(s5_lam_im), small(s5_log_dt),
      expand(s5_lam_re[:, :, :, None, :]), expand(s5_lam_im[:, :, :, None, :]), expand(s5_log_dt[:, :, :, None, None]),
      expand(jnp.swapaxes(s5_b_re, -1, -2)[:, None]), expand(jnp.swapaxes(s5_b_im, -1, -2)[:, None]),
      ct(s5_c_re), ct(s5_c_im))
    return a_re.reshape(depth, 2, S5_NC), a_im.reshape(depth, 2, S5_NC), w_bu, w_c


def _s5_state_to_cols(re, im):
    lead = re.shape[:-2]
    st = jnp.stack([re.reshape(lead + (S5_SLABS, 512)), im.reshape(lead + (S5_SLABS, 512))], axis=-2)
    return st.reshape(lead + (2 * S5_NC,))


def _s5_cols_to_state(cols):
    lead = cols.shape[:-1]
    st = cols.reshape(lead + (S5_SLABS, 2, 512))
    re = st[..., 0, :].reshape(lead + (S5_GROUPS, S5_STATE))
    im = st[..., 1, :].reshape(lead + (S5_GROUPS, S5_STATE))
    return re, im


_IN_OFFS = [int(o) for o in np.cumsum([0, S5_WIDTH, S5_WIDTH, GLA_KEY, GLA_KEY, GLA_WIDTH, GLA_WIDTH, GLA_RANK,
                                       D_MODEL, D_MODEL])]


def _split_w_in(w_in):
    o = _IN_OFFS
    depth = w_in.shape[0]
    wt = jnp.swapaxes(w_in, 1, 2)
    pad = jnp.zeros((depth, LANES - GLA_RANK, D_MODEL), w_in.dtype)
    proj_t = jnp.concatenate([wt[:, o[0]:o[1]], wt[:, o[2]:o[5]], wt[:, o[6]:o[7]], pad], axis=1)
    gate_t = jnp.concatenate([wt[:, o[1]:o[2]], wt[:, o[5]:o[6]], wt[:, o[7]:o[9]]], axis=1)
    return proj_t.astype(BF16), gate_t.astype(BF16)


def _proj_kernel(*refs, has_pe):
    if has_pe:
        x_ref, pe_ref, shift_ref, scale_ref, ng_ref, w_ref, u_ref, q_ref, k_ref, v_ref, gl_ref = refs
        x = x_ref[0] + pe_ref[...]
    else:
        x_ref, shift_ref, scale_ref, ng_ref, w_ref, u_ref, q_ref, k_ref, v_ref, gl_ref = refs
        x = x_ref[0]
    h = _modulated_norm(x, ng_ref[...], 1.0 + scale_ref[...], shift_ref[...])
    p = _bdot_t(h, w_ref[...])
    o = 0
    u_ref[0] = p[:, o:o + S5_WIDTH]
    o += S5_WIDTH
    q_ref[0] = p[:, o:o + GLA_KEY]
    o += GLA_KEY
    k_ref[0] = p[:, o:o + GLA_KEY]
    o += GLA_KEY
    v_ref[0] = p[:, o:o + GLA_WIDTH].astype(BF16)
    o += GLA_WIDTH
    gl_ref[0] = p[:, o:o + LANES]


def _mod_spec(l, cond_row, part):
    return pl.BlockSpec((None, None, 1, D_MODEL), lambda b, j: (l, cond_row(b), 0, part))


def _layer_spec(l, shape):
    return pl.BlockSpec((None,) + shape, lambda b, j: (l,) + (0,) * len(shape))


def _proj_call(x, pe, mod, l, cond_row, ng, w, tm):
    nseq, length, _ = x.shape
    row = lambda width: pl.BlockSpec((1, tm, width), lambda b, j: (b, j, 0))
    in_specs = [row(D_MODEL)]
    args = [x]
    if pe is not None:
        in_specs.append(pl.BlockSpec((tm, D_MODEL), lambda b, j: (j, 0)))
        args.append(pe)
    in_specs += [_mod_spec(l, cond_row, 0), _mod_spec(l, cond_row, 1), _layer_spec(l, (1, D_MODEL)),
                 _layer_spec(l, (PROJ_W, D_MODEL))]
    args += [mod, mod, ng, w]
    sds = lambda width, dt: jax.ShapeDtypeStruct((nseq, length, width), dt)
    return pl.pallas_call(
        functools.partial(_proj_kernel, has_pe=pe is not None),
        grid=(nseq, length // tm),
        in_specs=in_specs,
        out_specs=(row(S5_WIDTH), row(GLA_KEY), row(GLA_KEY), row(GLA_WIDTH), row(LANES)),
        out_shape=(sds(S5_WIDTH, F32), sds(GLA_KEY, F32), sds(GLA_KEY, F32), sds(GLA_WIDTH, BF16), sds(LANES, F32)),
        compiler_params=pltpu.CompilerParams(dimension_semantics=("arbitrary", "arbitrary"),
                                             vmem_limit_bytes=VMEM_LIMIT),
        name="proj",
    )(*args)


def _s5_perms(nb, tc):
    r = nb * tc
    pf = np.zeros((2 * r, r), np.float32)
    pb = np.zeros((2 * r, r), np.float32)
    for t in range(tc):
        for b in range(nb):
            pf[t * 2 * nb + b, b * tc + t] = 1.0
            pb[t * 2 * nb + nb + b, b * tc + (tc - 1 - t)] = 1.0
    return pf, pb


def _s5_kernel(uf1_ref, ub1_ref, uf2_ref, ub2_ref, uf0_ref, ub0_ref, pf_ref, pb_ref, pft_ref, pbt_ref, wbu_ref, wc_ref,
               are_ref, aim_ref, h0_ref, yf_ref, yb_ref, hfin_ref, bua_ref, bub_ref, hb_ref, st_ref,
               *, nb, tc, tiles_per_pass):
    i = pl.program_id(0)
    rows = nb * tc
    rs = 2 * nb

    def packed_inputs(uf_ref, ub_ref):
        uf = uf_ref[...].reshape(rows, S5_WIDTH).astype(BF16)
        ub = ub_ref[...].reshape(rows, S5_WIDTH).astype(BF16)
        u2f = jnp.dot(pf_ref[...], uf, preferred_element_type=F32).astype(BF16)
        u2b = jnp.dot(pb_ref[...], ub, preferred_element_type=F32).astype(BF16)
        return [jnp.concatenate([u2f[:, s * LANES:(s + 1) * LANES], u2b[:, s * LANES:(s + 1) * LANES]], axis=1)
                for s in range(S5_SLABS)]

    def bu_slab(lhs, s, dst_ref):
        dst_ref[:, s * 1024:(s + 1) * 1024] = jnp.dot(lhs[s], wbu_ref[s], preferred_element_type=F32)

    @pl.when(i == 0)
    def _():
        st_ref[...] = h0_ref[...]
        lhs0 = packed_inputs(uf0_ref, ub0_ref)
        for s in range(S5_SLABS):
            bu_slab(lhs0, s, bua_ref)

    half_step = functools.partial(_s5_half_step, pft_ref=pft_ref, pbt_ref=pbt_ref, wc_ref=wc_ref, are_ref=are_ref,
                                  aim_ref=aim_ref, hb_ref=hb_ref, st_ref=st_ref, bu_slab=bu_slab, rs=rs, tc=tc,
                                  tiles_per_pass=tiles_per_pass)
    yf, yb, tail = half_step(bua_ref, bub_ref, packed_inputs(uf1_ref, ub1_ref))
    yf_ref[:, 0:tc, :] = yf.reshape(nb, tc, S5_WIDTH)
    yb_ref[:, tc:2 * tc, :] = yb.reshape(nb, tc, S5_WIDTH)
    yf, yb, _ = half_step(bub_ref, bua_ref, packed_inputs(uf2_ref, ub2_ref), after=tail)
    yf_ref[:, tc:2 * tc, :] = yf.reshape(nb, tc, S5_WIDTH)
    yb_ref[:, 0:tc, :] = yb.reshape(nb, tc, S5_WIDTH)
    hfin_ref[...] = st_ref[...]


def _s5_half_step(cur_ref, nxt_ref, lhs_next, *, pft_ref, pbt_ref, wc_ref, are_ref, aim_ref, hb_ref, st_ref, bu_slab,
                  rs, tc, tiles_per_pass, after=None):
    grp = max(1, 16 // rs)
    ys = []
    for s in range(S5_SLABS):
        bu_slab(lhs_next, s, nxt_ref)
        for c0 in range(4 * s, 4 * s + 4, tiles_per_pass):
            cts = list(range(c0, c0 + tiles_per_pass))
            cre = [(ct // 4) * 1024 + (ct % 4) * LANES for ct in cts]
            cim = [c + 512 for c in cre]
            a_re = [are_ref[:, ct * LANES:(ct + 1) * LANES] for ct in cts]
            a_im = [aim_ref[:, ct * LANES:(ct + 1) * LANES] for ct in cts]
            h_re = [st_ref[:, c:c + LANES] for c in cre]
            h_im = [st_ref[:, c:c + LANES] for c in cim]
            if after is not None and c0 == 0:
                h_re[0] = h_re[0] + _exact_zero_of(after)
            for t0 in range(0, tc, grp):
                out_re = [[] for _ in cts]
                out_im = [[] for _ in cts]
                for t in range(t0, t0 + grp):
                    r0 = t * rs
                    for j in range(len(cts)):
                        b_re = cur_ref[r0:r0 + rs, cre[j]:cre[j] + LANES]
                        b_im = cur_ref[r0:r0 + rs, cim[j]:cim[j] + LANES]
                        n_re = a_re[j] * h_re[j] - a_im[j] * h_im[j] + b_re
                        n_im = a_re[j] * h_im[j] + a_im[j] * h_re[j] + b_im
                        h_re[j], h_im[j] = n_re, n_im
                        out_re[j].append(n_re)
                        out_im[j].append(n_im)
                for j in range(len(cts)):
                    blk_re = out_re[j][0] if grp == 1 else jnp.concatenate(out_re[j], axis=0)
                    blk_im = out_im[j][0] if grp == 1 else jnp.concatenate(out_im[j], axis=0)
                    hb_ref[t0 * rs:(t0 + grp) * rs, cre[j]:cre[j] + LANES] = blk_re.astype(BF16)
                    hb_ref[t0 * rs:(t0 + grp) * rs, cim[j]:cim[j] + LANES] = blk_im.astype(BF16)
            for j in range(len(cts)):
                st_ref[:, cre[j]:cre[j] + LANES] = h_re[j]
                st_ref[:, cim[j]:cim[j] + LANES] = h_im[j]
        ys.append(jnp.dot(hb_ref[:, s * 1024:(s + 1) * 1024], wc_ref[s], preferred_element_type=F32))
    y2 = jnp.concatenate(ys, axis=1)
    y_hi, y_lo = _split2(y2)
    yf = jnp.dot(pft_ref[...], y_hi, preferred_element_type=F32) + jnp.dot(pft_ref[...], y_lo, preferred_element_type=F32)
    yb = jnp.dot(pbt_ref[...], y_hi, preferred_element_type=F32) + jnp.dot(pbt_ref[...], y_lo, preferred_element_type=F32)
    return yf, yb, h_re[-1]


def _s5_call(u, l, w_bu, w_c, are, aim, h0, tc, tiles_per_pass):
    nb, length, _ = u.shape
    n = length // tc
    rows = nb * tc
    pf, pb = _s5_perms(nb, tc)
    const = lambda shape: pl.BlockSpec(shape, lambda i: (0,) * len(shape))
    layer = lambda shape: pl.BlockSpec((None,) + shape, lambda i: (l,) + (0,) * len(shape))
    blk = lambda index_map: pl.BlockSpec((nb, tc, S5_WIDTH), index_map)
    out_f = pl.BlockSpec((nb, 2 * tc, S5_WIDTH), lambda i: (0, i, 0))
    out_b = pl.BlockSpec((nb, 2 * tc, S5_WIDTH), lambda i: (0, n // 2 - 1 - i, 0))
    second = lambda i: jnp.minimum(2 * i + 2, n - 1)
    return pl.pallas_call(
        functools.partial(_s5_kernel, nb=nb, tc=tc, tiles_per_pass=tiles_per_pass),
        grid=(n // 2,),
        in_specs=[blk(lambda i: (0, 2 * i + 1, 0)), blk(lambda i: (0, n - 2 - 2 * i, 0)),
                  blk(lambda i: (0, second(i), 0)), blk(lambda i: (0, n - 1 - second(i), 0)),
                  blk(lambda i: (0, 0, 0)), blk(lambda i: (0, n - 1, 0)),
                  const((2 * rows, rows)), const((2 * rows, rows)), const((rows, 2 * rows)), const((rows, 2 * rows)),
                  layer((S5_SLABS, 2 * LANES, 1024)), layer((S5_SLABS, 1024, LANES)),
                  layer((2 * nb, S5_NC)), layer((2 * nb, S5_NC)), const((2 * nb, 2 * S5_NC))],
        out_specs=(out_f, out_b, const((2 * nb, 2 * S5_NC))),
        out_shape=(jax.ShapeDtypeStruct(u.shape, F32), jax.ShapeDtypeStruct(u.shape, F32),
                   jax.ShapeDtypeStruct((2 * nb, 2 * S5_NC), F32)),
        scratch_shapes=[pltpu.VMEM((2 * rows, 2 * S5_NC), F32), pltpu.VMEM((2 * rows, 2 * S5_NC), F32),
                        pltpu.VMEM((2 * rows, 2 * S5_NC), BF16), pltpu.VMEM((2 * nb, 2 * S5_NC), F32)],
        compiler_params=pltpu.CompilerParams(dimension_semantics=("arbitrary",), vmem_limit_bytes=VMEM_LIMIT),
        name="s5_scan",
    )(u, u, u, u, u, u, jnp.asarray(pf, BF16), jnp.asarray(pb, BF16), jnp.asarray(pf.T, BF16), jnp.asarray(pb.T, BF16),
      w_bu, w_c, are, aim, h0)


def _gla_kernel(*refs, length, has_s0):
    q_ref, k_ref, v_ref, gl_ref, wg_ref, bg_ref, ng_ref = refs[:7]
    s0_ref = refs[7] if has_s0 else None
    o_ref, sfin_ref, s_ref = refs[-3:]
    c = GLA_CHUNK
    n = length // c
    ri = lax.broadcasted_iota(jnp.int32, (c, c), 0)
    ci = lax.broadcasted_iota(jnp.int32, (c, c), 1)
    lane = lax.broadcasted_iota(jnp.int32, (1, GLA_KEY), 1)
    head_mask = [((lane >= h * GLA_DK) & (lane < (h + 1) * GLA_DK)).astype(F32) for h in range(GLA_HEADS)]
    srow = lax.shift_right_logical(lax.broadcasted_iota(jnp.int32, (GLA_KEY, GLA_WIDTH), 0), 6)
    scol = lax.shift_right_logical(lax.broadcasted_iota(jnp.int32, (GLA_KEY, GLA_WIDTH), 1), 7)
    bd_mask = (srow == scol).astype(F32)
    ng = ng_ref[...]
    causal = [ci <= ri, ci >= ri]
    tri = [m.astype(BF16) for m in causal]

    s_ref[...] = jnp.zeros_like(s_ref)
    if has_s0:
        for d in range(2):
            for h in range(GLA_HEADS):
                s_ref[d, h * GLA_DK:(h + 1) * GLA_DK, h * GLA_DV:(h + 1) * GLA_DV] = s0_ref[0, d, h]

    def chunk_pair(r):
        dirs = (0, 1)
        end = (c - 1, 0)
        qc = [q_ref[0, pl.ds(r[d], c), :] * (GLA_DK ** -0.5) for d in dirs]
        kc = [k_ref[0, pl.ds(r[d], c), :] for d in dirs]
        vc = [v_ref[0, pl.ds(r[d], c), :] for d in dirs]
        logits = [_bdot(gl_ref[0, pl.ds(r[d], c), :], wg_ref[d]) + bg_ref[d] for d in dirs]
        g = [_split2(_log_sigmoid(logits[d]) * (1.0 / GLA_GATE_NORM)) for d in dirs]
        b = [jnp.dot(tri[d], g[d][0], preferred_element_type=F32) + jnp.dot(tri[d], g[d][1], preferred_element_type=F32)
             for d in dirs]
        bt = [b[d].T for d in dirs]
        kt = [kc[d].T for d in dirs]
        b_mid = [b[d][c // 2:c // 2 + 1, :] for d in dirs]
        bt_mid = [bt[d][:, c // 2:c // 2 + 1] for d in dirs]
        bt_end = [bt[d][:, end[d]:end[d] + 1] for d in dirs]
        q_in = [(qc[d] * jnp.exp(b[d])).astype(BF16) for d in dirs]
        q_mid = [qc[d] * jnp.exp(b[d] - b_mid[d]) for d in dirs]
        k_mid_t = [(kt[d] * jnp.exp(bt_mid[d] - bt[d])).astype(BF16) for d in dirs]
        k_end_t = [(kt[d] * jnp.exp(bt_end[d] - bt[d])).astype(BF16) for d in dirs]
        q_st = [jnp.concatenate([q_mid[d] * head_mask[h] for h in range(GLA_HEADS)], axis=0).astype(BF16) for d in dirs]
        a = [jnp.dot(q_st[d], k_mid_t[d], preferred_element_type=F32) for d in dirs]
        s_old = [s_ref[d] for d in dirs]
        o_inter = [jnp.dot(q_in[d], s_old[d].astype(BF16), preferred_element_type=F32) for d in dirs]
        parts = [[], []]
        for h in range(GLA_HEADS):
            for d in dirs:
                a_h = jnp.where(causal[d], a[d][h * c:(h + 1) * c, :], 0.0).astype(BF16)
                parts[d].append(jnp.dot(a_h, vc[d][:, h * GLA_DV:(h + 1) * GLA_DV], preferred_element_type=F32))
        upd = [jnp.dot(k_end_t[d], vc[d], preferred_element_type=F32) for d in dirs]
        for d in dirs:
            s_ref[d] = jnp.exp(bt_end[d]) * s_old[d] + upd[d] * bd_mask
        return [o_inter[d] + jnp.concatenate(parts[d], axis=1) for d in dirs]

    def finish(o, r0):
        o = o + o_ref[0, pl.ds(r0, c), :]
        normed = []
        for h in range(GLA_HEADS):
            oh = o[:, h * GLA_DV:(h + 1) * GLA_DV]
            ms = jnp.mean(oh * oh, axis=-1, keepdims=True)
            normed.append(oh * lax.rsqrt(ms + EPS))
        o_ref[0, pl.ds(r0, c), :] = jnp.concatenate(normed, axis=1) * ng

    def first_half(step, carry):
        rf = pl.multiple_of(step * c, c)
        rb = pl.multiple_of((n - 1 - step) * c, c)
        o_f, o_b = chunk_pair((rf, rb))
        o_ref[0, pl.ds(rf, c), :] = o_f
        o_ref[0, pl.ds(rb, c), :] = o_b
        return carry

    def second_half(step, carry):
        rf = pl.multiple_of(step * c, c)
        rb = pl.multiple_of((n - 1 - step) * c, c)
        o_f, o_b = chunk_pair((rf, rb))
        finish(o_f, rf)
        finish(o_b, rb)
        return carry

    lax.fori_loop(0, n // 2, first_half, 0)
    lax.fori_loop(n // 2, n, second_half, 0)
    for d in range(2):
        for h in range(GLA_HEADS):
            sfin_ref[0, d, h] = s_ref[d, h * GLA_DK:(h + 1) * GLA_DK, h * GLA_DV:(h + 1) * GLA_DV]


def _gla_call(q, k, v, gl, l, wg, bg, ng, s0):
    nseq, length, _ = q.shape
    seq = lambda width: pl.BlockSpec((1, length, width), lambda b: (b, 0, 0))
    layer = lambda shape: pl.BlockSpec((None,) + shape, lambda b: (l,) + (0,) * len(shape))
    in_specs = [seq(GLA_KEY), seq(GLA_KEY), seq(GLA_WIDTH), seq(LANES),
                layer((2, LANES, GLA_KEY)), layer((2, 1, GLA_KEY)), layer((1, GLA_WIDTH))]
    args = [q, k, v, gl, wg, bg, ng]
    if s0 is not None:
        in_specs.append(pl.BlockSpec((1, None, 2, GLA_HEADS, GLA_DK, GLA_DV), lambda b: (b, l, 0, 0, 0, 0)))
        args.append(s0)
    return pl.pallas_call(
        functools.partial(_gla_kernel, length=length, has_s0=s0 is not None),
        grid=(nseq,),
        in_specs=in_specs,
        out_specs=(seq(GLA_WIDTH), pl.BlockSpec((1, 2, GLA_HEADS, GLA_DK, GLA_DV), lambda b: (b, 0, 0, 0, 0))),
        out_shape=(jax.ShapeDtypeStruct((nseq, length, GLA_WIDTH), F32),
                   jax.ShapeDtypeStruct((nseq, 2, GLA_HEADS, GLA_DK, GLA_DV), F32)),
        scratch_shapes=[pltpu.VMEM((2, GLA_KEY, GLA_WIDTH), F32)],
        compiler_params=pltpu.CompilerParams(dimension_semantics=("arbitrary",), vmem_limit_bytes=VMEM_LIMIT),
        name="gla",
    )(*args)


def _mix_kernel(*refs, has_pe, last):
    if has_pe:
        (x_ref, pe_ref, yf_ref, yb_ref, u_ref, o_ref, shift_ref, scale_ref, gate_ref, ng_ref, wg_ref, d_ref,
         wglu_ref, bglu_ref, wpa_ref, wpb_ref, wo_ref, fng_ref, out_ref) = refs
        x = x_ref[0] + pe_ref[...]
    else:
        (x_ref, yf_ref, yb_ref, u_ref, o_ref, shift_ref, scale_ref, gate_ref, ng_ref, wg_ref, d_ref,
         wglu_ref, bglu_ref, wpa_ref, wpb_ref, wo_ref, fng_ref, out_ref) = refs
        x = x_ref[0]
    h = _modulated_norm(x, ng_ref[...], 1.0 + scale_ref[...], shift_ref[...])
    gts = _bdot_t(h, wg_ref[...])
    gate_a = gts[:, :S5_WIDTH]
    gate_b = gts[:, S5_WIDTH:2 * S5_WIDTH]
    m_a = gts[:, 2 * S5_WIDTH:2 * S5_WIDTH + D_MODEL]
    m_b = gts[:, 2 * S5_WIDTH + D_MODEL:]
    y = _gelu_tanh(yf_ref[0] + yb_ref[0] + d_ref[...] * u_ref[0])
    y = y * _sigmoid(_bdot(y, wglu_ref[...]) + bglu_ref[...])
    y_a = y * _silu(gate_a)
    y_b = o_ref[0] * _silu(gate_b)
    merged = _sigmoid(m_a) * _bdot(y_a, wpa_ref[...]) + _sigmoid(m_b) * _bdot(y_b, wpb_ref[...])
    xn = x + gate_ref[...] * _bdot(merged, wo_ref[...])
    if last:
        ms = jnp.mean(xn * xn, axis=-1, keepdims=True)
        xn = xn * lax.rsqrt(ms + EPS) * fng_ref[...]
    out_ref[0] = xn


def _mix_call(x, pe, yf, yb, u, o, mod, l, cond_row, ng, wg, s5_d, wglu, bglu, wpa, wpb, wo, fng, tm, last):
    nseq, length, _ = x.shape
    row = lambda width: pl.BlockSpec((1, tm, width), lambda b, j: (b, j, 0))
    in_specs = [row(D_MODEL)]
    args = [x]
    if pe is not None:
        in_specs.append(pl.BlockSpec((tm, D_MODEL), lambda b, j: (j, 0)))
        args.append(pe)
    in_specs += [row(S5_WIDTH), row(S5_WIDTH), row(S5_WIDTH), row(GLA_WIDTH),
                 _mod_spec(l, cond_row, 0), _mod_spec(l, cond_row, 1), _mod_spec(l, cond_row, 2),
                 _layer_spec(l, (1, D_MODEL)), _layer_spec(l, (GATE_W, D_MODEL)), _layer_spec(l, (1, S5_WIDTH)),
                 _layer_spec(l, (S5_WIDTH, S5_WIDTH)), _layer_spec(l, (1, S5_WIDTH)),
                 _layer_spec(l, (S5_WIDTH, D_MODEL)), _layer_spec(l, (GLA_WIDTH, D_MODEL)),
                 _layer_spec(l, (D_MODEL, D_MODEL)), pl.BlockSpec((1, D_MODEL), lambda b, j: (0, 0))]
    args += [yf, yb, u, o, mod, mod, mod, ng, wg, s5_d, wglu, bglu, wpa, wpb, wo, fng]
    return pl.pallas_call(
        functools.partial(_mix_kernel, has_pe=pe is not None, last=last),
        grid=(nseq, length // tm),
        in_specs=in_specs,
        out_specs=row(D_MODEL),
        out_shape=jax.ShapeDtypeStruct(x.shape, F32),
        compiler_params=pltpu.CompilerParams(dimension_semantics=("arbitrary", "arbitrary"),
                                             vmem_limit_bytes=VMEM_LIMIT),
        name="mix",
    )(*args)


def _grid_pos_embed(length, dim):
    rows = length // GRID_W
    quarter = dim // 4
    freqs = jnp.exp(-math.log(10000.0) * jnp.arange(quarter, dtype=F32) / quarter)

    def sincos(pos):
        ang = pos.astype(F32)[:, None] * freqs[None, :]
        return jnp.concatenate([jnp.sin(ang), jnp.cos(ang)], axis=-1)

    er = sincos(jnp.arange(rows))
    ec = sincos(jnp.arange(GRID_W))
    pe = jnp.concatenate([jnp.broadcast_to(er[:, None, :], (rows, GRID_W, dim // 2)),
                          jnp.broadcast_to(ec[None, :, :], (rows, GRID_W, dim // 2))], axis=-1)
    return pe.reshape(rows * GRID_W, dim)


def kernel(x_prompt, x_sample, c, state_s5_re, state_s5_im, state_gla, c_ctx, norm_g, w_mod, b_mod, w_in, gla_wg_up,
           gla_bg, gla_norm_g, s5_lam_re, s5_lam_im, s5_log_dt, s5_b_re, s5_b_im, s5_c_re, s5_c_im, s5_d, w_glu,
           b_glu, w_pa, w_pb, w_o, final_norm_g):
    depth = w_in.shape[0]
    bp, lp, _ = x_prompt.shape
    bs, ls, _ = x_sample.shape

    cond8 = jnp.zeros((8, D_MODEL), F32).at[0].set(c_ctx).at[1:1 + bs].set(c)
    mod = _mod_call(cond8, w_mod, b_mod)

    a_re, a_im, w_bu, w_c = _s5_params(s5_lam_re, s5_lam_im, s5_log_dt, s5_b_re, s5_b_im, s5_c_re, s5_c_im)

    w_proj, w_gate = _split_w_in(w_in)
    wg_up = jnp.concatenate([gla_wg_up, jnp.zeros((depth, 2, LANES - GLA_RANK, GLA_KEY), F32)], axis=2).astype(BF16)
    w_glu_b, w_pa_b, w_pb_b, w_o_b = (w.astype(BF16) for w in (w_glu, w_pa, w_pb, w_o))

    vec = lambda a: a[:, None, :]
    mod = mod.reshape(depth, 8, 1, 3 * D_MODEL)
    ng, gng, bg = vec(norm_g), vec(gla_norm_g), gla_bg[:, :, None, :]
    common = (ng, w_gate, vec(s5_d), w_glu_b, vec(b_glu), w_pa_b, w_pb_b, w_o_b, final_norm_g[None])
    are_p, aim_p = jnp.repeat(a_re, bp, axis=1), jnp.repeat(a_im, bp, axis=1)
    are_s, aim_s = jnp.repeat(a_re, bs, axis=1), jnp.repeat(a_im, bs, axis=1)
    ctx_row = lambda b: 0
    cond_row = lambda b: b + 1

    pe = _grid_pos_embed(ls, D_MODEL)
    zero_h0 = jnp.zeros((2 * bp, 2 * S5_NC), F32)
    xp, xs = x_prompt, x_sample
    new_re, new_im, new_gla = [], [], []
    for l in range(depth):
        last = l == depth - 1

        u, q, k, v, gl = _proj_call(xp, None, mod, l, ctx_row, ng, w_proj, tm=lp)
        yf, yb, hfin = _s5_call(u, l, w_bu, w_c, are_p, aim_p, zero_h0, tc=8, tiles_per_pass=2)
        o, sfin = _gla_call(q, k, v, gl, l, wg_up, bg, gng, None)
        new_gla.append(sfin)
        xp = _mix_call(xp, None, yf, yb, u, o, mod, l, ctx_row, *common, tm=lp, last=last)
        re_l, im_l = _s5_cols_to_state(hfin.reshape(2, bp, 2 * S5_NC))
        new_re.append(jnp.swapaxes(re_l, 0, 1))
        new_im.append(jnp.swapaxes(im_l, 0, 1))

        pe_l = pe if l == 0 else None
        u, q, k, v, gl = _proj_call(xs, pe_l, mod, l, cond_row, ng, w_proj, tm=512)
        h0 = _s5_state_to_cols(jnp.swapaxes(state_s5_re[:, l], 0, 1), jnp.swapaxes(state_s5_im[:, l], 0, 1))
        yf, yb, _ = _s5_call(u, l, w_bu, w_c, are_s, aim_s, h0.reshape(2 * bs, 2 * S5_NC), tc=32, tiles_per_pass=4)
        o, _ = _gla_call(q, k, v, gl, l, wg_up, bg, gng, state_gla)
        xs = _mix_call(xs, pe_l, yf, yb, u, o, mod, l, cond_row, *common, tm=512, last=last)

    return (xp, xs, jnp.stack(new_re, axis=1), jnp.stack(new_im, axis=1), jnp.stack(new_gla, axis=1))
```

```python
import functools
import math

import numpy as np
import jax
import jax.numpy as jnp
from jax import lax
from jax.experimental import pallas as pl
from jax.experimental.pallas import tpu as pltpu

F32 = jnp.float32
BF16 = jnp.bfloat16

D_MODEL = 1024
GRID_W = 64
S5_WIDTH = 512
S5_GROUPS = 32
S5_CH = 16
S5_STATE = 64
S5_NC = S5_GROUPS * S5_STATE
S5_SLABS = 4
GLA_HEADS = 4
GLA_DK = 64
GLA_DV = 128
GLA_KEY = GLA_HEADS * GLA_DK
GLA_WIDTH = GLA_HEADS * GLA_DV
GLA_RANK = 16
GLA_GATE_NORM = 16.0
GLA_CHUNK = 128
EPS = 1e-6
LANES = 128
PROJ_W = S5_WIDTH + 2 * GLA_KEY + GLA_WIDTH + LANES
GATE_W = 2 * S5_WIDTH + 2 * D_MODEL
VMEM_LIMIT = 52 * 1024 * 1024


def _bdot(a, b):
    return jnp.dot(a.astype(BF16), b.astype(BF16), preferred_element_type=F32)


def _bdot_t(a, bt):
    return lax.dot_general(a.astype(BF16), bt.astype(BF16), (((1,), (1,)), ((), ())), preferred_element_type=F32)


def _split2(x):
    hi = x.astype(BF16)
    lo = (x - hi.astype(F32)).astype(BF16)
    return hi, lo


def _sigmoid(x):
    return 1.0 / (1.0 + jnp.exp(-x))


def _silu(x):
    return x * _sigmoid(x)


def _gelu_tanh(x):
    c = math.sqrt(2.0 / math.pi)
    return 0.5 * x * (1.0 + jnp.tanh(c * (x + 0.044715 * (x * x * x))))


def _exact_zero_of(x):
    bits = lax.bitcast_convert_type(x, jnp.int32)
    return lax.shift_right_logical(lax.shift_right_logical(bits, 16), 16).astype(F32)


def _log_sigmoid(x):
    return jnp.minimum(x, 0.0) - jnp.log(1.0 + jnp.exp(-jnp.abs(x)))


def _modulated_norm(x, ng, scale1p, shift):
    ms = jnp.mean(x * x, axis=-1, keepdims=True)
    return (x * lax.rsqrt(ms + EPS) * ng) * scale1p + shift


def _mod_kernel(cond_ref, w_ref, b_ref, out_ref):
    c = cond_ref[...]
    s_hi, s_lo = _split2(_silu(c))
    w_hi, w_lo = _split2(w_ref[0])
    acc = jnp.dot(s_hi, w_hi, preferred_element_type=F32)
    acc += jnp.dot(s_lo, w_hi, preferred_element_type=F32)
    acc += jnp.dot(s_hi, w_lo, preferred_element_type=F32)
    out_ref[0] = acc + b_ref[0]


def _mod_call(cond8, w_mod, b_mod):
    depth = w_mod.shape[0]
    nb = 768
    return pl.pallas_call(
        _mod_kernel,
        grid=(depth, 3 * D_MODEL // nb),
        in_specs=[
            pl.BlockSpec((8, D_MODEL), lambda l, j: (0, 0)),
            pl.BlockSpec((1, D_MODEL, nb), lambda l, j: (l, 0, j)),
            pl.BlockSpec((1, 1, nb), lambda l, j: (l, 0, j)),
        ],
        out_specs=pl.BlockSpec((1, 8, nb), lambda l, j: (l, 0, j)),
        out_shape=jax.ShapeDtypeStruct((depth, 8, 3 * D_MODEL), F32),
        compiler_params=pltpu.CompilerParams(dimension_semantics=("arbitrary", "arbitrary")),
        name="adaln_mod",
    )(cond8, w_mod, b_mod.reshape(depth, 1, 3 * D_MODEL))


def _zoh(lre, lim, ldt):
    dt = jnp.exp(ldt)
    mag = jnp.exp(lre * dt)
    a_re = mag * jnp.cos(lim * dt)
    a_im = mag * jnp.sin(lim * dt)
    n_re = a_re - 1.0
    inv = 1.0 / (lre * lre + lim * lim)
    return a_re, a_im, (n_re * lre + a_im * lim) * inv, (a_im * lre - n_re * lim) * inv


def _zoh_kernel(lre_ref, lim_ref, ldt_ref, lre_x_ref, lim_x_ref, ldt_x_ref, bre_ref, bim_ref, cre_ref, cim_ref,
                are_ref, aim_ref, wbu_ref, wc_ref, *, depth):
    a_re, a_im, _, _ = _zoh(lre_ref[...], lim_ref[...], ldt_ref[...])
    are_ref[...] = a_re
    aim_ref[...] = a_im
    _, _, c_re, c_im = _zoh(lre_x_ref[...], lim_x_ref[...], ldt_x_ref[...])
    bre = bre_ref[...]
    bim = bim_ref[...]
    bbar = [c_re * bre - c_im * bim, c_re * bim + c_im * bre]
    cmat = [cre_ref[...], -cim_ref[...]]
    wbu_ref[...] = jnp.zeros_like(wbu_ref)
    wc_ref[...] = jnp.zeros_like(wc_ref)
    for l in range(depth):
        for s in range(S5_SLABS):
            for gl in range(8):
                g = 8 * s + gl
                for ri in range(2):
                    col = ri * 8 * S5_STATE + gl * S5_STATE
                    for d in range(2):
                        src = ((l * 2 + d) * S5_GROUPS + g) * S5_CH
                        row = (d * 8 + gl) * S5_CH
                        wbu_ref[l, s, row:row + S5_CH, col:col + S5_STATE] = bbar[ri][src:src + S5_CH, :].astype(BF16)
                    src = (l * S5_GROUPS + g) * S5_STATE
                    wc_ref[l, s, col:col + S5_STATE, gl * S5_CH:(gl + 1) * S5_CH] = (
                        cmat[ri][src:src + S5_STATE, :].astype(BF16))


def _s5_params(s5_lam_re, s5_lam_im, s5_log_dt, s5_b_re, s5_b_im, s5_c_re, s5_c_im):
    depth = s5_lam_re.shape[0]
    full = (depth, 2, S5_GROUPS, S5_CH, S5_STATE)
    expand = lambda a: jnp.broadcast_to(a, full).reshape(-1, S5_STATE)
    small = lambda a: a.reshape(depth * 2 * S5_GROUPS, -1)
    ct = lambda a: jnp.swapaxes(a, -1, -2).reshape(-1, S5_CH)
    a_re, a_im, w_bu, w_c = pl.pallas_call(
        functools.partial(_zoh_kernel, depth=depth),
        out_shape=(jax.ShapeDtypeStruct((depth * 2 * S5_GROUPS, S5_STATE), F32),
                   jax.ShapeDtypeStruct((depth * 2 * S5_GROUPS, S5_STATE), F32),
                   jax.ShapeDtypeStruct((depth, S5_SLABS, 2 * LANES, 2 * 8 * S5_STATE), BF16),
                   jax.ShapeDtypeStruct((depth, S5_SLABS, 2 * 8 * S5_STATE, LANES), BF16)),
        name="s5_zoh",
    )(small(s5_lam_re), small(s5_lam_im), small(s5_log_dt),
      expand(s5_lam_re[:, :, :, None, :]), expand(s5_lam_im[:, :, :, None, :]), expand(s5_log_dt[:, :, :, None, None]),
      expand(jnp.swapaxes(s5_b_re, -1, -2)[:, None]), expand(jnp.swapaxes(s5_b_im, -1, -2)[:, None]),
      ct(s5_c_re), ct(s5_c_im))
    return a_re.reshape(depth, 2, S5_NC), a_im.reshape(depth, 2, S5_NC), w_bu, w_c


def _s5_state_to_cols(re, im):
    lead = re.shape[:-2]
    st = jnp.stack([re.reshape(lead + (S5_SLABS, 512)), im.reshape(lead + (S5_SLABS, 512))], axis=-2)
    return st.reshape(lead + (2 * S5_NC,))


def _s5_cols_to_state(cols):
    lead = cols.shape[:-1]
    st = cols.reshape(lead + (S5_SLABS, 2, 512))
    re = st[..., 0, :].reshape(lead + (S5_GROUPS, S5_STATE))
    im = st[..., 1, :].reshape(lead + (S5_GROUPS, S5_STATE))
    return re, im


_IN_OFFS = [int(o) for o in np.cumsum([0, S5_WIDTH, S5_WIDTH, GLA_KEY, GLA_KEY, GLA_WIDTH, GLA_WIDTH, GLA_RANK,
                                       D_MODEL, D_MODEL])]


def _split_w_in(w_in):
    o = _IN_OFFS
    depth = w_in.shape[0]
    wt = jnp.swapaxes(w_in, 1, 2)
    pad = jnp.zeros((depth, LANES - GLA_RANK, D_MODEL), w_in.dtype)
    proj_t = jnp.concatenate([wt[:, o[0]:o[1]], wt[:, o[2]:o[5]], wt[:, o[6]:o[7]], pad], axis=1)
    gate_t = jnp.concatenate([wt[:, o[1]:o[2]], wt[:, o[5]:o[6]], wt[:, o[7]:o[9]]], axis=1)
    return proj_t.astype(BF16), gate_t.astype(BF16)


def _proj_kernel(*refs, has_pe):
    if has_pe:
        x_ref, pe_ref, shift_ref, scale_ref, ng_ref, w_ref, u_ref, q_ref, k_ref, v_ref, gl_ref = refs
        x = x_ref[0] + pe_ref[...]
    else:
        x_ref, shift_ref, scale_ref, ng_ref, w_ref, u_ref, q_ref, k_ref, v_ref, gl_ref = refs
        x = x_ref[0]
    h = _modulated_norm(x, ng_ref[...], 1.0 + scale_ref[...], shift_ref[...])
    p = _bdot_t(h, w_ref[...])
    o = 0
    u_ref[0] = p[:, o:o + S5_WIDTH]
    o += S5_WIDTH
    q_ref[0] = p[:, o:o + GLA_KEY]
    o += GLA_KEY
    k_ref[0] = p[:, o:o + GLA_KEY]
    o += GLA_KEY
    v_ref[0] = p[:, o:o + GLA_WIDTH].astype(BF16)
    o += GLA_WIDTH
    gl_ref[0] = p[:, o:o + LANES]


def _mod_spec(l, cond_row, part):
    return pl.BlockSpec((None, None, 1, D_MODEL), lambda b, j: (l, cond_row(b), 0, part))


def _layer_spec(l, shape):
    return pl.BlockSpec((None,) + shape, lambda b, j: (l,) + (0,) * len(shape))


def _proj_call(x, pe, mod, l, cond_row, ng, w, tm):
    nseq, length, _ = x.shape
    row = lambda width: pl.BlockSpec((1, tm, width), lambda b, j: (b, j, 0))
    in_specs = [row(D_MODEL)]
    args = [x]
    if pe is not None:
        in_specs.append(pl.BlockSpec((tm, D_MODEL), lambda b, j: (j, 0)))
        args.append(pe)
    in_specs += [_mod_spec(l, cond_row, 0), _mod_spec(l, cond_row, 1), _layer_spec(l, (1, D_MODEL)),
                 _layer_spec(l, (PROJ_W, D_MODEL))]
    args += [mod, mod, ng, w]
    sds = lambda width, dt: jax.ShapeDtypeStruct((nseq, length, width), dt)
    return pl.pallas_call(
        functools.partial(_proj_kernel, has_pe=pe is not None),
        grid=(nseq, length // tm),
        in_specs=in_specs,
        out_specs=(row(S5_WIDTH), row(GLA_KEY), row(GLA_KEY), row(GLA_WIDTH), row(LANES)),
        out_shape=(sds(S5_WIDTH, F32), sds(GLA_KEY, F32), sds(GLA_KEY, F32), sds(GLA_WIDTH, BF16), sds(LANES, F32)),
        compiler_params=pltpu.CompilerParams(dimension_semantics=("arbitrary", "arbitrary"),
                                             vmem_limit_bytes=VMEM_LIMIT),
        name="proj",
    )(*args)


def _s5_perms(nb, tc):
    r = nb * tc
    pf = np.zeros((2 * r, r), np.float32)
    pb = np.zeros((2 * r, r), np.float32)
    for t in range(tc):
        for b in range(nb):
            pf[t * 2 * nb + b, b * tc + t] = 1.0
            pb[t * 2 * nb + nb + b, b * tc + (tc - 1 - t)] = 1.0
    return pf, pb


def _s5_kernel(uf1_ref, ub1_ref, uf2_ref, ub2_ref, uf0_ref, ub0_ref, pf_ref, pb_ref, pft_ref, pbt_ref, wbu_ref, wc_ref,
               are_ref, aim_ref, h0_ref, yf_ref, yb_ref, hfin_ref, bua_ref, bub_ref, hb_ref, st_ref,
               *, nb, tc, tiles_per_pass):
    i = pl.program_id(0)
    rows = nb * tc
    rs = 2 * nb

    def packed_inputs(uf_ref, ub_ref):
        uf = uf_ref[...].reshape(rows, S5_WIDTH).astype(BF16)
        ub = ub_ref[...].reshape(rows, S5_WIDTH).astype(BF16)
        u2f = jnp.dot(pf_ref[...], uf, preferred_element_type=F32).astype(BF16)
        u2b = jnp.dot(pb_ref[...], ub, preferred_element_type=F32).astype(BF16)
        return [jnp.concatenate([u2f[:, s * LANES:(s + 1) * LANES], u2b[:, s * LANES:(s + 1) * LANES]], axis=1)
                for s in range(S5_SLABS)]

    def bu_slab(lhs, s, dst_ref):
        dst_ref[:, s * 1024:(s + 1) * 1024] = jnp.dot(lhs[s], wbu_ref[s], preferred_element_type=F32)

    @pl.when(i == 0)
    def _():
        st_ref[...] = h0_ref[...]
        lhs0 = packed_inputs(uf0_ref, ub0_ref)
        for s in range(S5_SLABS):
            bu_slab(lhs0, s, bua_ref)

    half_step = functools.partial(_s5_half_step, pft_ref=pft_ref, pbt_ref=pbt_ref, wc_ref=wc_ref, are_ref=are_ref,
                                  aim_ref=aim_ref, hb_ref=hb_ref, st_ref=st_ref, bu_slab=bu_slab, rs=rs, tc=tc,
                                  tiles_per_pass=tiles_per_pass)
    yf, yb, tail = half_step(bua_ref, bub_ref, packed_inputs(uf1_ref, ub1_ref))
    yf_ref[:, 0:tc, :] = yf.reshape(nb, tc, S5_WIDTH)
    yb_ref[:, tc:2 * tc, :] = yb.reshape(nb, tc, S5_WIDTH)
    yf, yb, _ = half_step(bub_ref, bua_ref, packed_inputs(uf2_ref, ub2_ref), after=tail)
    yf_ref[:, tc:2 * tc, :] = yf.reshape(nb, tc, S5_WIDTH)
    yb_ref[:, 0:tc, :] = yb.reshape(nb, tc, S5_WIDTH)
    hfin_ref[...] = st_ref[...]


def _s5_half_step(cur_ref, nxt_ref, lhs_next, *, pft_ref, pbt_ref, wc_ref, are_ref, aim_ref, hb_ref, st_ref, bu_slab,
                  rs, tc, tiles_per_pass, after=None):
    grp = max(1, 16 // rs)
    ys = []
    for s in range(S5_SLABS):
        bu_slab(lhs_next, s, nxt_ref)
        for c0 in range(4 * s, 4 * s + 4, tiles_per_pass):
            cts = list(range(c0, c0 + tiles_per_pass))
            cre = [(ct // 4) * 1024 + (ct % 4) * LANES for ct in cts]
            cim = [c + 512 for c in cre]
            a_re = [are_ref[:, ct * LANES:(ct + 1) * LANES] for ct in cts]
            a_im = [aim_ref[:, ct * LANES:(ct + 1) * LANES] for ct in cts]
            h_re = [st_ref[:, c:c + LANES] for c in cre]
            h_im = [st_ref[:, c:c + LANES] for c in cim]
            if after is not None and c0 == 0:
                h_re[0] = h_re[0] + _exact_zero_of(after)
            for t0 in range(0, tc, grp):
                out_re = [[] for _ in cts]
                out_im = [[] for _ in cts]
                for t in range(t0, t0 + grp):
                    r0 = t * rs
                    for j in range(len(cts)):
                        b_re = cur_ref[r0:r0 + rs, cre[j]:cre[j] + LANES]
                        b_im = cur_ref[r0:r0 + rs, cim[j]:cim[j] + LANES]
                        n_re = a_re[j] * h_re[j] - a_im[j] * h_im[j] + b_re
                        n_im = a_re[j] * h_im[j] + a_im[j] * h_re[j] + b_im
                        h_re[j], h_im[j] = n_re, n_im
                        out_re[j].append(n_re)
                        out_im[j].append(n_im)
                for j in range(len(cts)):
                    blk_re = out_re[j][0] if grp == 1 else jnp.concatenate(out_re[j], axis=0)
                    blk_im = out_im[j][0] if grp == 1 else jnp.concatenate(out_im[j], axis=0)
                    hb_ref[t0 * rs:(t0 + grp) * rs, cre[j]:cre[j] + LANES] = blk_re.astype(BF16)
                    hb_ref[t0 * rs:(t0 + grp) * rs, cim[j]:cim[j] + LANES] = blk_im.astype(BF16)
            for j in range(len(cts)):
                st_ref[:, cre[j]:cre[j] + LANES] = h_re[j]
                st_ref[:, cim[j]:cim[j] + LANES] = h_im[j]
        ys.append(jnp.dot(hb_ref[:, s * 1024:(s + 1) * 1024], wc_ref[s], preferred_element_type=F32))
    y2 = jnp.concatenate(ys, axis=1)
    y_hi, y_lo = _split2(y2)
    yf = jnp.dot(pft_ref[...], y_hi, preferred_element_type=F32) + jnp.dot(pft_ref[...], y_lo, preferred_element_type=F32)
    yb = jnp.dot(pbt_ref[...], y_hi, preferred_element_type=F32) + jnp.dot(pbt_ref[...], y_lo, preferred_element_type=F32)
    return yf, yb, h_re[-1]


def _s5_call(u, l, w_bu, w_c, are, aim, h0, tc, tiles_per_pass):
    nb, length, _ = u.shape
    n = length // tc
    rows = nb * tc
    pf, pb = _s5_perms(nb, tc)
    const = lambda shape: pl.BlockSpec(shape, lambda i: (0,) * len(shape))
    layer = lambda shape: pl.BlockSpec((None,) + shape, lambda i: (l,) + (0,) * len(shape))
    blk = lambda index_map: pl.BlockSpec((nb, tc, S5_WIDTH), index_map)
    out_f = pl.BlockSpec((nb, 2 * tc, S5_WIDTH), lambda i: (0, i, 0))
    out_b = pl.BlockSpec((nb, 2 * tc, S5_WIDTH), lambda i: (0, n // 2 - 1 - i, 0))
    second = lambda i: jnp.minimum(2 * i + 2, n - 1)
    return pl.pallas_call(
        functools.partial(_s5_kernel, nb=nb, tc=tc, tiles_per_pass=tiles_per_pass),
        grid=(n // 2,),
        in_specs=[blk(lambda i: (0, 2 * i + 1, 0)), blk(lambda i: (0, n - 2 - 2 * i, 0)),
                  blk(lambda i: (0, second(i), 0)), blk(lambda i: (0, n - 1 - second(i), 0)),
                  blk(lambda i: (0, 0, 0)), blk(lambda i: (0, n - 1, 0)),
                  const((2 * rows, rows)), const((2 * rows, rows)), const((rows, 2 * rows)), const((rows, 2 * rows)),
                  layer((S5_SLABS, 2 * LANES, 1024)), layer((S5_SLABS, 1024, LANES)),
                  layer((2 * nb, S5_NC)), layer((2 * nb, S5_NC)), const((2 * nb, 2 * S5_NC))],
        out_specs=(out_f, out_b, const((2 * nb, 2 * S5_NC))),
        out_shape=(jax.ShapeDtypeStruct(u.shape, F32), jax.ShapeDtypeStruct(u.shape, F32),
                   jax.ShapeDtypeStruct((2 * nb, 2 * S5_NC), F32)),
        scratch_shapes=[pltpu.VMEM((2 * rows, 2 * S5_NC), F32), pltpu.VMEM((2 * rows, 2 * S5_NC), F32),
                        pltpu.VMEM((2 * rows, 2 * S5_NC), BF16), pltpu.VMEM((2 * nb, 2 * S5_NC), F32)],
        compiler_params=pltpu.CompilerParams(dimension_semantics=("arbitrary",), vmem_limit_bytes=VMEM_LIMIT),
        name="s5_scan",
    )(u, u, u, u, u, u, jnp.asarray(pf, BF16), jnp.asarray(pb, BF16), jnp.asarray(pf.T, BF16), jnp.asarray(pb.T, BF16),
      w_bu, w_c, are, aim, h0)


def _gla_kernel(*refs, length, has_s0, nsq):
    q_ref, k_ref, v_ref, gl_ref, wg_ref, bg_ref, ng_ref = refs[:7]
    s0_ref = refs[7] if has_s0 else None
    o_ref, sfin_ref, sf_ref, sb_ref = refs[-4:]
    c = GLA_CHUNK
    n = length // c
    ri = lax.broadcasted_iota(jnp.int32, (c, c), 0)
    ci = lax.broadcasted_iota(jnp.int32, (c, c), 1)
    lane = lax.broadcasted_iota(jnp.int32, (1, GLA_KEY), 1)
    head_mask = [(lane >= h * GLA_DK) & (lane < (h + 1) * GLA_DK) for h in range(GLA_HEADS)]
    ng = ng_ref[...]
    causal = [ci <= ri, ci >= ri]
    tri = [m.astype(BF16) for m in causal]
    end = (c - 1, 0)
    chains = [(sq, d) for sq in range(nsq) for d in (0, 1)]
    key_rows = lambda h: slice(h * GLA_DK, (h + 1) * GLA_DK)
    val_cols = lambda h: slice(h * GLA_DV, (h + 1) * GLA_DV)

    sb_ref[...] = jnp.zeros_like(sb_ref)
    for sq, d in chains:
        for h in range(GLA_HEADS):
            if has_s0:
                sf_ref[sq, d, h] = s0_ref[sq, d, h]
                sb_ref[sq, d, key_rows(h), val_cols(h)] = s0_ref[sq, d, h].astype(BF16)
            else:
                sf_ref[sq, d, h] = jnp.zeros((GLA_DK, GLA_DV), F32)

    def chunk_group(r):
        qc = [q_ref[sq, pl.ds(r[d], c), :] * (GLA_DK ** -0.5) for sq, d in chains]
        kc = [k_ref[sq, pl.ds(r[d], c), :] for sq, d in chains]
        vc = [v_ref[sq, pl.ds(r[d], c), :] for sq, d in chains]
        logits = [_bdot(gl_ref[sq, pl.ds(r[d], c), :], wg_ref[d]) + bg_ref[d] for sq, d in chains]
        g = [_split2(_log_sigmoid(x) * (1.0 / GLA_GATE_NORM)) for x in logits]
        b = [jnp.dot(tri[d], hi, preferred_element_type=F32) + jnp.dot(tri[d], lo, preferred_element_type=F32)
             for (sq, d), (hi, lo) in zip(chains, g)]
        bt = [x.T for x in b]
        kt = [x.T for x in kc]
        b_mid = [x[c // 2:c // 2 + 1, :] for x in b]
        bt_mid = [x[:, c // 2:c // 2 + 1] for x in bt]
        bt_end = [x[:, end[d]:end[d] + 1] for (sq, d), x in zip(chains, bt)]
        q_in = [(x * jnp.exp(y)).astype(BF16) for x, y in zip(qc, b)]
        q_mid = [(x * jnp.exp(y - m)).astype(BF16) for x, y, m in zip(qc, b, b_mid)]
        k_mid_t = [(x * jnp.exp(m - y)).astype(BF16) for x, y, m in zip(kt, bt, bt_mid)]
        k_end_t = [(x * jnp.exp(e - y)).astype(BF16) for x, y, e in zip(kt, bt, bt_end)]
        q_st = [jnp.concatenate([jnp.where(head_mask[h], x, jnp.zeros_like(x)) for h in range(GLA_HEADS)], axis=0)
                for x in q_mid]
        a = [jnp.dot(x, y, preferred_element_type=F32) for x, y in zip(q_st, k_mid_t)]
        o_inter = [jnp.dot(x, sb_ref[sq, d], preferred_element_type=F32) for (sq, d), x in zip(chains, q_in)]
        parts = [[] for _ in chains]
        for h in range(GLA_HEADS):
            for i, (sq, d) in enumerate(chains):
                a_h = jnp.where(causal[d], a[i][h * c:(h + 1) * c, :], 0.0).astype(BF16)
                parts[i].append(jnp.dot(a_h, vc[i][:, val_cols(h)], preferred_element_type=F32))
        for h in range(GLA_HEADS):
            for i, (sq, d) in enumerate(chains):
                upd = jnp.dot(k_end_t[i][key_rows(h), :], vc[i][:, val_cols(h)], preferred_element_type=F32)
                s_new = jnp.exp(bt_end[i][key_rows(h), :]) * sf_ref[sq, d, h] + upd
                sf_ref[sq, d, h] = s_new
                sb_ref[sq, d, key_rows(h), val_cols(h)] = s_new.astype(BF16)
        return [x + jnp.concatenate(p, axis=1) for x, p in zip(o_inter, parts)]

    def finish(o, sq, r0):
        o = o + o_ref[sq, pl.ds(r0, c), :]
        normed = []
        for h in range(GLA_HEADS):
            oh = o[:, val_cols(h)]
            ms = jnp.mean(oh * oh, axis=-1, keepdims=True)
            normed.append(oh * lax.rsqrt(ms + EPS))
        o_ref[sq, pl.ds(r0, c), :] = jnp.concatenate(normed, axis=1) * ng

    def first_half(step, carry):
        r = (pl.multiple_of(step * c, c), pl.multiple_of((n - 1 - step) * c, c))
        for (sq, d), o in zip(chains, chunk_group(r)):
            o_ref[sq, pl.ds(r[d], c), :] = o
        return carry

    def second_half(step, carry):
        r = (pl.multiple_of(step * c, c), pl.multiple_of((n - 1 - step) * c, c))
        for (sq, d), o in zip(chains, chunk_group(r)):
            finish(o, sq, r[d])
        return carry

    lax.fori_loop(0, n // 2, first_half, 0)
    lax.fori_loop(n // 2, n, second_half, 0)
    for sq, d in chains:
        for h in range(GLA_HEADS):
            sfin_ref[sq, d, h] = sf_ref[sq, d, h]


def _gla_call(q, k, v, gl, l, wg, bg, ng, s0, nsq=2):
    nseq, length, _ = q.shape
    seq = lambda width: pl.BlockSpec((nsq, length, width), lambda b: (b, 0, 0))
    layer = lambda shape: pl.BlockSpec((None,) + shape, lambda b: (l,) + (0,) * len(shape))
    in_specs = [seq(GLA_KEY), seq(GLA_KEY), seq(GLA_WIDTH), seq(LANES),
                layer((2, LANES, GLA_KEY)), layer((2, 1, GLA_KEY)), layer((1, GLA_WIDTH))]
    args = [q, k, v, gl, wg, bg, ng]
    if s0 is not None:
        in_specs.append(pl.BlockSpec((nsq, None, 2, GLA_HEADS, GLA_DK, GLA_DV), lambda b: (b, l, 0, 0, 0, 0)))
        args.append(s0)
    return pl.pallas_call(
        functools.partial(_gla_kernel, length=length, has_s0=s0 is not None, nsq=nsq),
        grid=(nseq // nsq,),
        in_specs=in_specs,
        out_specs=(seq(GLA_WIDTH), pl.BlockSpec((nsq, 2, GLA_HEADS, GLA_DK, GLA_DV), lambda b: (b, 0, 0, 0, 0))),
        out_shape=(jax.ShapeDtypeStruct((nseq, length, GLA_WIDTH), F32),
                   jax.ShapeDtypeStruct((nseq, 2, GLA_HEADS, GLA_DK, GLA_DV), F32)),
        scratch_shapes=[pltpu.VMEM((nsq, 2, GLA_HEADS, GLA_DK, GLA_DV), F32),
                        pltpu.VMEM((nsq, 2, GLA_KEY, GLA_WIDTH), BF16)],
        compiler_params=pltpu.CompilerParams(dimension_semantics=("arbitrary",), vmem_limit_bytes=VMEM_LIMIT),
        name="gla",
    )(*args)


def _mix_kernel(*refs, has_pe, last):
    if has_pe:
        (x_ref, pe_ref, yf_ref, yb_ref, u_ref, o_ref, shift_ref, scale_ref, gate_ref, ng_ref, wg_ref, d_ref,
         wglu_ref, bglu_ref, wpa_ref, wpb_ref, wo_ref, fng_ref, out_ref) = refs
        x = x_ref[0] + pe_ref[...]
    else:
        (x_ref, yf_ref, yb_ref, u_ref, o_ref, shift_ref, scale_ref, gate_ref, ng_ref, wg_ref, d_ref,
         wglu_ref, bglu_ref, wpa_ref, wpb_ref, wo_ref, fng_ref, out_ref) = refs
        x = x_ref[0]
    h = _modulated_norm(x, ng_ref[...], 1.0 + scale_ref[...], shift_ref[...])
    gts = _bdot_t(h, wg_ref[...])
    gate_a = gts[:, :S5_WIDTH]
    gate_b = gts[:, S5_WIDTH:2 * S5_WIDTH]
    m_a = gts[:, 2 * S5_WIDTH:2 * S5_WIDTH + D_MODEL]
    m_b = gts[:, 2 * S5_WIDTH + D_MODEL:]
    y = _gelu_tanh(yf_ref[0] + yb_ref[0] + d_ref[...] * u_ref[0])
    y = y * _sigmoid(_bdot(y, wglu_ref[...]) + bglu_ref[...])
    y_a = y * _silu(gate_a)
    y_b = o_ref[0] * _silu(gate_b)
    merged = _sigmoid(m_a) * _bdot(y_a, wpa_ref[...]) + _sigmoid(m_b) * _bdot(y_b, wpb_ref[...])
    xn = x + gate_ref[...] * _bdot(merged, wo_ref[...])
    if last:
        ms = jnp.mean(xn * xn, axis=-1, keepdims=True)
        xn = xn * lax.rsqrt(ms + EPS) * fng_ref[...]
    out_ref[0] = xn


def _mix_call(x, pe, yf, yb, u, o, mod, l, cond_row, ng, wg, s5_d, wglu, bglu, wpa, wpb, wo, fng, tm, last):
    nseq, length, _ = x.shape
    row = lambda width: pl.BlockSpec((1, tm, width), lambda b, j: (b, j, 0))
    in_specs = [row(D_MODEL)]
    args = [x]
    if pe is not None:
        in_specs.append(pl.BlockSpec((tm, D_MODEL), lambda b, j: (j, 0)))
        args.append(pe)
    in_specs += [row(S5_WIDTH), row(S5_WIDTH), row(S5_WIDTH), row(GLA_WIDTH),
                 _mod_spec(l, cond_row, 0), _mod_spec(l, cond_row, 1), _mod_spec(l, cond_row, 2),
                 _layer_spec(l, (1, D_MODEL)), _layer_spec(l, (GATE_W, D_MODEL)), _layer_spec(l, (1, S5_WIDTH)),
                 _layer_spec(l, (S5_WIDTH, S5_WIDTH)), _layer_spec(l, (1, S5_WIDTH)),
                 _layer_spec(l, (S5_WIDTH, D_MODEL)), _layer_spec(l, (GLA_WIDTH, D_MODEL)),
                 _layer_spec(l, (D_MODEL, D_MODEL)), pl.BlockSpec((1, D_MODEL), lambda b, j: (0, 0))]
    args += [yf, yb, u, o, mod, mod, mod, ng, wg, s5_d, wglu, bglu, wpa, wpb, wo, fng]
    return pl.pallas_call(
        functools.partial(_mix_kernel, has_pe=pe is not None, last=last),
        grid=(nseq, length // tm),
        in_specs=in_specs,
        out_specs=row(D_MODEL),
        out_shape=jax.ShapeDtypeStruct(x.shape, F32),
        compiler_params=pltpu.CompilerParams(dimension_semantics=("arbitrary", "arbitrary"),
                                             vmem_limit_bytes=VMEM_LIMIT),
        name="mix",
    )(*args)


def _grid_pos_embed(length, dim):
    rows = length // GRID_W
    quarter = dim // 4
    freqs = jnp.exp(-math.log(10000.0) * jnp.arange(quarter, dtype=F32) / quarter)

    def sincos(pos):
        ang = pos.astype(F32)[:, None] * freqs[None, :]
        return jnp.concatenate([jnp.sin(ang), jnp.cos(ang)], axis=-1)

    er = sincos(jnp.arange(rows))
    ec = sincos(jnp.arange(GRID_W))
    pe = jnp.concatenate([jnp.broadcast_to(er[:, None, :], (rows, GRID_W, dim // 2)),
                          jnp.broadcast_to(ec[None, :, :], (rows, GRID_W, dim // 2))], axis=-1)
    return pe.reshape(rows * GRID_W, dim)


def kernel(x_prompt, x_sample, c, state_s5_re, state_s5_im, state_gla, c_ctx, norm_g, w_mod, b_mod, w_in, gla_wg_up,
           gla_bg, gla_norm_g, s5_lam_re, s5_lam_im, s5_log_dt, s5_b_re, s5_b_im, s5_c_re, s5_c_im, s5_d, w_glu,
           b_glu, w_pa, w_pb, w_o, final_norm_g):
    depth = w_in.shape[0]
    bp, lp, _ = x_prompt.shape
    bs, ls, _ = x_sample.shape

    cond8 = jnp.zeros((8, D_MODEL), F32).at[0].set(c_ctx).at[1:1 + bs].set(c)
    mod = _mod_call(cond8, w_mod, b_mod)

    a_re, a_im, w_bu, w_c = _s5_params(s5_lam_re, s5_lam_im, s5_log_dt, s5_b_re, s5_b_im, s5_c_re, s5_c_im)

    w_proj, w_gate = _split_w_in(w_in)
    wg_up = jnp.concatenate([gla_wg_up, jnp.zeros((depth, 2, LANES - GLA_RANK, GLA_KEY), F32)], axis=2).astype(BF16)
    w_glu_b, w_pa_b, w_pb_b, w_o_b = (w.astype(BF16) for w in (w_glu, w_pa, w_pb, w_o))

    vec = lambda a: a[:, None, :]
    mod = mod.reshape(depth, 8, 1, 3 * D_MODEL)
    ng, gng, bg = vec(norm_g), vec(gla_norm_g), gla_bg[:, :, None, :]
    common = (ng, w_gate, vec(s5_d), w_glu_b, vec(b_glu), w_pa_b, w_pb_b, w_o_b, final_norm_g[None])
    are_p, aim_p = jnp.repeat(a_re, bp, axis=1), jnp.repeat(a_im, bp, axis=1)
    are_s, aim_s = jnp.repeat(a_re, bs, axis=1), jnp.repeat(a_im, bs, axis=1)
    ctx_row = lambda b: 0
    cond_row = lambda b: b + 1

    pe = _grid_pos_embed(ls, D_MODEL)
    zero_h0 = jnp.zeros((2 * bp, 2 * S5_NC), F32)
    xp, xs = x_prompt, x_sample
    new_re, new_im, new_gla = [], [], []
    for l in range(depth):
        last = l == depth - 1

        u, q, k, v, gl = _proj_call(xp, None, mod, l, ctx_row, ng, w_proj, tm=lp)
        yf, yb, hfin = _s5_call(u, l, w_bu, w_c, are_p, aim_p, zero_h0, tc=8, tiles_per_pass=2)
        o, sfin = _gla_call(q, k, v, gl, l, wg_up, bg, gng, None)
        new_gla.append(sfin)
        xp = _mix_call(xp, None, yf, yb, u, o, mod, l, ctx_row, *common, tm=lp, last=last)
        re_l, im_l = _s5_cols_to_state(hfin.reshape(2, bp, 2 * S5_NC))
        new_re.append(jnp.swapaxes(re_l, 0, 1))
        new_im.append(jnp.swapaxes(im_l, 0, 1))

        pe_l = pe if l == 0 else None
        u, q, k, v, gl = _proj_call(xs, pe_l, mod, l, cond_row, ng, w_proj, tm=512)
        h0 = _s5_state_to_cols(jnp.swapaxes(state_s5_re[:, l], 0, 1), jnp.swapaxes(state_s5_im[:, l], 0, 1))
        yf, yb, _ = _s5_call(u, l, w_bu, w_c, are_s, aim_s, h0.reshape(2 * bs, 2 * S5_NC), tc=32, tiles_per_pass=4)
        o, _ = _gla_call(q, k, v, gl, l, wg_up, bg, gng, state_gla)
        xs = _mix_call(xs, pe_l, yf, yb, u, o, mod, l, cond_row, *common, tm=512, last=last)

    return (xp, xs, jnp.stack(new_re, axis=1), jnp.stack(new_im, axis=1), jnp.stack(new_gla, axis=1))
```

```python
import functools
import math

import numpy as np
import jax
import jax.numpy as jnp
from jax import lax
from jax.experimental import pallas as pl
from jax.experimental.pallas import tpu as pltpu

F32 = jnp.float32
BF16 = jnp.bfloat16

D_MODEL = 1024
GRID_W = 64
S5_WIDTH = 512
S5_GROUPS = 32
S5_CH = 16
S5_STATE = 64
S5_NC = S5_GROUPS * S5_STATE
S5_SLABS = 4
GLA_HEADS = 4
GLA_DK = 64
GLA_DV = 128
GLA_KEY = GLA_HEADS * GLA_DK
GLA_WIDTH = GLA_HEADS * GLA_DV
GLA_RANK = 16
GLA_GATE_NORM = 16.0
GLA_CHUNK = 128
EPS = 1e-6
LANES = 128
PROJ_W = S5_WIDTH + 2 * GLA_KEY + GLA_WIDTH + LANES
GATE_W = 2 * S5_WIDTH + 2 * D_MODEL
VMEM_LIMIT = 52 * 1024 * 1024


def _bdot(a, b):
    return jnp.dot(a.astype(BF16), b.astype(BF16), preferred_element_type=F32)


def _bdot_t(a, bt):
    return lax.dot_general(a.astype(BF16), bt.astype(BF16), (((1,), (1,)), ((), ())), preferred_element_type=F32)


def _split2(x):
    hi = x.astype(BF16)
    lo = (x - hi.astype(F32)).astype(BF16)
    return hi, lo


def _sigmoid(x):
    return 1.0 / (1.0 + jnp.exp(-x))


def _silu(x):
    return x * _sigmoid(x)


def _gelu_tanh(x):
    c = math.sqrt(2.0 / math.pi)
    return 0.5 * x * (1.0 + jnp.tanh(c * (x + 0.044715 * (x * x * x))))


def _exact_zero_of(x):
    bits = lax.bitcast_convert_type(x, jnp.int32)
    return lax.shift_right_logical(lax.shift_right_logical(bits, 16), 16).astype(F32)


def _log_sigmoid(x):
    return jnp.minimum(x, 0.0) - jnp.log(1.0 + jnp.exp(-jnp.abs(x)))


def _modulated_norm(x, ng, scale1p, shift):
    ms = jnp.mean(x * x, axis=-1, keepdims=True)
    return (x * lax.rsqrt(ms + EPS) * ng) * scale1p + shift


def _mod_kernel(cond_ref, w_ref, b_ref, out_ref):
    c = cond_ref[...]
    s_hi, s_lo = _split2(_silu(c))
    w_hi, w_lo = _split2(w_ref[0])
    acc = jnp.dot(s_hi, w_hi, preferred_element_type=F32)
    acc += jnp.dot(s_lo, w_hi, preferred_element_type=F32)
    acc += jnp.dot(s_hi, w_lo, preferred_element_type=F32)
    out_ref[0] = acc + b_ref[0]


def _mod_call(cond8, w_mod, b_mod):
    depth = w_mod.shape[0]
    nb = 768
    return pl.pallas_call(
        _mod_kernel,
        grid=(depth, 3 * D_MODEL // nb),
        in_specs=[
            pl.BlockSpec((8, D_MODEL), lambda l, j: (0, 0)),
            pl.BlockSpec((1, D_MODEL, nb), lambda l, j: (l, 0, j)),
            pl.BlockSpec((1, 1, nb), lambda l, j: (l, 0, j)),
        ],
        out_specs=pl.BlockSpec((1, 8, nb), lambda l, j: (l, 0, j)),
        out_shape=jax.ShapeDtypeStruct((depth, 8, 3 * D_MODEL), F32),
        compiler_params=pltpu.CompilerParams(dimension_semantics=("arbitrary", "arbitrary")),
        name="adaln_mod",
    )(cond8, w_mod, b_mod.reshape(depth, 1, 3 * D_MODEL))


def _zoh(lre, lim, ldt):
    dt = jnp.exp(ldt)
    mag = jnp.exp(lre * dt)
    a_re = mag * jnp.cos(lim * dt)
    a_im = mag * jnp.sin(lim * dt)
    n_re = a_re - 1.0
    inv = 1.0 / (lre * lre + lim * lim)
    return a_re, a_im, (n_re * lre + a_im * lim) * inv, (a_im * lre - n_re * lim) * inv


def _zoh_kernel(lre_ref, lim_ref, ldt_ref, lre_x_ref, lim_x_ref, ldt_x_ref, bre_ref, bim_ref, cre_ref, cim_ref,
                are_ref, aim_ref, wbu_ref, wc_ref, *, depth):
    a_re, a_im, _, _ = _zoh(lre_ref[...], lim_ref[...], ldt_ref[...])
    are_ref[...] = a_re
    aim_ref[...] = a_im
    _, _, c_re, c_im = _zoh(lre_x_ref[...], lim_x_ref[...], ldt_x_ref[...])
    bre = bre_ref[...]
    bim = bim_ref[...]
    bbar = [c_re * bre - c_im * bim, c_re * bim + c_im * bre]
    cmat = [cre_ref[...], -cim_ref[...]]
    wbu_ref[...] = jnp.zeros_like(wbu_ref)
    wc_ref[...] = jnp.zeros_like(wc_ref)
    for l in range(depth):
        for s in range(S5_SLABS):
            for gl in range(8):
                g = 8 * s + gl
                for ri in range(2):
                    col = ri * 8 * S5_STATE + gl * S5_STATE
                    for d in range(2):
                        src = ((l * 2 + d) * S5_GROUPS + g) * S5_CH
                        row = (d * 8 + gl) * S5_CH
                        wbu_ref[l, s, row:row + S5_CH, col:col + S5_STATE] = bbar[ri][src:src + S5_CH, :].astype(BF16)
                    src = (l * S5_GROUPS + g) * S5_STATE
                    wc_ref[l, s, col:col + S5_STATE, gl * S5_CH:(gl + 1) * S5_CH] = (
                        cmat[ri][src:src + S5_STATE, :].astype(BF16))


def _s5_params(s5_lam_re, s5_lam_im, s5_log_dt, s5_b_re, s5_b_im, s5_c_re, s5_c_im):
    depth = s5_lam_re.shape[0]
    full = (depth, 2, S5_GROUPS, S5_CH, S5_STATE)
    expand = lambda a: jnp.broadcast_to(a, full).reshape(-1, S5_STATE)
    small = lambda a: a.reshape(depth * 2 * S5_GROUPS, -1)
    ct = lambda a: jnp.swapaxes(a, -1, -2).reshape(-1, S5_CH)
    a_re, a_im, w_bu, w_c = pl.pallas_call(
        functools.partial(_zoh_kernel, depth=depth),
        out_shape=(jax.ShapeDtypeStruct((depth * 2 * S5_GROUPS, S5_STATE), F32),
                   jax.ShapeDtypeStruct((depth * 2 * S5_GROUPS, S5_STATE), F32),
                   jax.ShapeDtypeStruct((depth, S5_SLABS, 2 * LANES, 2 * 8 * S5_STATE), BF16),
                   jax.ShapeDtypeStruct((depth, S5_SLABS, 2 * 8 * S5_STATE, LANES), BF16)),
        name="s5_zoh",
    )(small(s5_lam_re), small(s5_lam_im), small(s5_log_dt),
      expand(s5_lam_re[:, :, :, None, :]), expand(s5_lam_im[:, :, :, None, :]), expand(s5_log_dt[:, :, :, None, None]),
      expand(jnp.swapaxes(s5_b_re, -1, -2)[:, None]), expand(jnp.swapaxes(s5_b_im, -1, -2)[:, None]),
      ct(s5_c_re), ct(s5_c_im))
    return a_re.reshape(depth, 2, S5_NC), a_im.reshape(depth, 2, S5_NC), w_bu, w_c


def _s5_state_to_cols(re, im):
    lead = re.shape[:-2]
    st = jnp.stack([re.reshape(lead + (S5_SLABS, 512)), im.reshape(lead + (S5_SLABS, 512))], axis=-2)
    return st.reshape(lead + (2 * S5_NC,))


def _s5_cols_to_state(cols):
    lead = cols.shape[:-1]
    st = cols.reshape(lead + (S5_SLABS, 2, 512))
    re = st[..., 0, :].reshape(lead + (S5_GROUPS, S5_STATE))
    im = st[..., 1, :].reshape(lead + (S5_GROUPS, S5_STATE))
    return re, im


_IN_OFFS = [int(o) for o in np.cumsum([0, S5_WIDTH, S5_WIDTH, GLA_KEY, GLA_KEY, GLA_WIDTH, GLA_WIDTH, GLA_RANK,
                                       D_MODEL, D_MODEL])]


def _split_w_in(w_in):
    o = _IN_OFFS
    depth = w_in.shape[0]
    wt = jnp.swapaxes(w_in, 1, 2)
    pad = jnp.zeros((depth, LANES - GLA_RANK, D_MODEL), w_in.dtype)
    proj_t = jnp.concatenate([wt[:, o[0]:o[1]], wt[:, o[2]:o[5]], wt[:, o[6]:o[7]], pad], axis=1)
    gate_t = jnp.concatenate([wt[:, o[1]:o[2]], wt[:, o[5]:o[6]], wt[:, o[7]:o[9]]], axis=1)
    return proj_t.astype(BF16), gate_t.astype(BF16)


def _proj_kernel(*refs, has_pe):
    if has_pe:
        x_ref, pe_ref, shift_ref, scale_ref, ng_ref, w_ref, u_ref, q_ref, k_ref, v_ref, gl_ref = refs
        x = x_ref[...] + pe_ref[...]
    else:
        x_ref, shift_ref, scale_ref, ng_ref, w_ref, u_ref, q_ref, k_ref, v_ref, gl_ref = refs
        x = x_ref[...]
    nsb, tm, _ = x.shape
    x = x.reshape(nsb * tm, D_MODEL)
    h = _modulated_norm(x, ng_ref[...], 1.0 + scale_ref[...], shift_ref[...])
    p = _bdot_t(h, w_ref[...])
    o = 0
    for ref, width in ((u_ref, S5_WIDTH), (q_ref, GLA_KEY), (k_ref, GLA_KEY), (v_ref, GLA_WIDTH), (gl_ref, LANES)):
        ref[...] = p[:, o:o + width].reshape(nsb, tm, width).astype(ref.dtype)
        o += width


def _mod_spec(l, cond_row, part):
    return pl.BlockSpec((None, None, 1, D_MODEL), lambda b, j: (l, cond_row(b), 0, part))


def _layer_spec(l, shape):
    return pl.BlockSpec((None,) + shape, lambda b, j: (l,) + (0,) * len(shape))


def _proj_call(x, pe, mod, l, cond_row, ng, w, tm, nsb=1):
    nseq, length, _ = x.shape
    row = lambda width: pl.BlockSpec((nsb, tm, width), lambda b, j: (b, j, 0))
    in_specs = [row(D_MODEL)]
    args = [x]
    if pe is not None:
        in_specs.append(pl.BlockSpec((tm, D_MODEL), lambda b, j: (j, 0)))
        args.append(pe)
    in_specs += [_mod_spec(l, cond_row, 0), _mod_spec(l, cond_row, 1), _layer_spec(l, (1, D_MODEL)),
                 _layer_spec(l, (PROJ_W, D_MODEL))]
    args += [mod, mod, ng, w]
    sds = lambda width, dt: jax.ShapeDtypeStruct((nseq, length, width), dt)
    return pl.pallas_call(
        functools.partial(_proj_kernel, has_pe=pe is not None),
        grid=(nseq // nsb, length // tm),
        in_specs=in_specs,
        out_specs=(row(S5_WIDTH), row(GLA_KEY), row(GLA_KEY), row(GLA_WIDTH), row(LANES)),
        out_shape=(sds(S5_WIDTH, F32), sds(GLA_KEY, F32), sds(GLA_KEY, F32), sds(GLA_WIDTH, BF16), sds(LANES, F32)),
        compiler_params=pltpu.CompilerParams(dimension_semantics=("arbitrary", "arbitrary"),
                                             vmem_limit_bytes=VMEM_LIMIT),
        name="proj",
    )(*args)


def _s5_perms(nb, tc):
    r = nb * tc
    pf = np.zeros((2 * r, r), np.float32)
    pb = np.zeros((2 * r, r), np.float32)
    for t in range(tc):
        for b in range(nb):
            pf[t * 2 * nb + b, b * tc + t] = 1.0
            pb[t * 2 * nb + nb + b, b * tc + (tc - 1 - t)] = 1.0
    return pf, pb


def _s5_kernel(uf1_ref, ub1_ref, uf2_ref, ub2_ref, uf0_ref, ub0_ref, pf_ref, pb_ref, pft_ref, pbt_ref, wbu_ref, wc_ref,
               are_ref, aim_ref, h0_ref, yf_ref, yb_ref, hfin_ref, bua_ref, bub_ref, hb_ref, st_ref,
               *, nb, tc, tiles_per_pass):
    i = pl.program_id(0)
    rows = nb * tc
    rs = 2 * nb

    def packed_inputs(uf_ref, ub_ref):
        uf = uf_ref[...].reshape(rows, S5_WIDTH).astype(BF16)
        ub = ub_ref[...].reshape(rows, S5_WIDTH).astype(BF16)
        u2f = jnp.dot(pf_ref[...], uf, preferred_element_type=F32).astype(BF16)
        u2b = jnp.dot(pb_ref[...], ub, preferred_element_type=F32).astype(BF16)
        return [jnp.concatenate([u2f[:, s * LANES:(s + 1) * LANES], u2b[:, s * LANES:(s + 1) * LANES]], axis=1)
                for s in range(S5_SLABS)]

    def bu_slab(lhs, s, dst_ref):
        dst_ref[:, s * 1024:(s + 1) * 1024] = jnp.dot(lhs[s], wbu_ref[s], preferred_element_type=F32)

    @pl.when(i == 0)
    def _():
        st_ref[...] = h0_ref[...]
        lhs0 = packed_inputs(uf0_ref, ub0_ref)
        for s in range(S5_SLABS):
            bu_slab(lhs0, s, bua_ref)

    half_step = functools.partial(_s5_half_step, pft_ref=pft_ref, pbt_ref=pbt_ref, wc_ref=wc_ref, are_ref=are_ref,
                                  aim_ref=aim_ref, hb_ref=hb_ref, st_ref=st_ref, bu_slab=bu_slab, rs=rs, tc=tc,
                                  tiles_per_pass=tiles_per_pass)
    yf, yb, tail = half_step(bua_ref, bub_ref, packed_inputs(uf1_ref, ub1_ref))
    yf_ref[:, 0:tc, :] = yf.reshape(nb, tc, S5_WIDTH)
    yb_ref[:, tc:2 * tc, :] = yb.reshape(nb, tc, S5_WIDTH)
    yf, yb, _ = half_step(bub_ref, bua_ref, packed_inputs(uf2_ref, ub2_ref), after=tail)
    yf_ref[:, tc:2 * tc, :] = yf.reshape(nb, tc, S5_WIDTH)
    yb_ref[:, 0:tc, :] = yb.reshape(nb, tc, S5_WIDTH)
    hfin_ref[...] = st_ref[...]


def _s5_half_step(cur_ref, nxt_ref, lhs_next, *, pft_ref, pbt_ref, wc_ref, are_ref, aim_ref, hb_ref, st_ref, bu_slab,
                  rs, tc, tiles_per_pass, after=None):
    grp = max(1, 16 // rs)
    ys = []
    for s in range(S5_SLABS):
        bu_slab(lhs_next, s, nxt_ref)
        for c0 in range(4 * s, 4 * s + 4, tiles_per_pass):
            cts = list(range(c0, c0 + tiles_per_pass))
            cre = [(ct // 4) * 1024 + (ct % 4) * LANES for ct in cts]
            cim = [c + 512 for c in cre]
            a_re = [are_ref[:, ct * LANES:(ct + 1) * LANES] for ct in cts]
            a_im = [aim_ref[:, ct * LANES:(ct + 1) * LANES] for ct in cts]
            h_re = [st_ref[:, c:c + LANES] for c in cre]
            h_im = [st_ref[:, c:c + LANES] for c in cim]
            if after is not None and c0 == 0:
                h_re[0] = h_re[0] + _exact_zero_of(after)
            for t0 in range(0, tc, grp):
                out_re = [[] for _ in cts]
                out_im = [[] for _ in cts]
                for t in range(t0, t0 + grp):
                    r0 = t * rs
                    for j in range(len(cts)):
                        b_re = cur_ref[r0:r0 + rs, cre[j]:cre[j] + LANES]
                        b_im = cur_ref[r0:r0 + rs, cim[j]:cim[j] + LANES]
                        n_re = a_re[j] * h_re[j] - a_im[j] * h_im[j] + b_re
                        n_im = a_re[j] * h_im[j] + a_im[j] * h_re[j] + b_im
                        h_re[j], h_im[j] = n_re, n_im
                        out_re[j].append(n_re)
                        out_im[j].append(n_im)
                for j in range(len(cts)):
                    blk_re = out_re[j][0] if grp == 1 else jnp.concatenate(out_re[j], axis=0)
                    blk_im = out_im[j][0] if grp == 1 else jnp.concatenate(out_im[j], axis=0)
                    hb_ref[t0 * rs:(t0 + grp) * rs, cre[j]:cre[j] + LANES] = blk_re.astype(BF16)
                    hb_ref[t0 * rs:(t0 + grp) * rs, cim[j]:cim[j] + LANES] = blk_im.astype(BF16)
            for j in range(len(cts)):
                st_ref[:, cre[j]:cre[j] + LANES] = h_re[j]
                st_ref[:, cim[j]:cim[j] + LANES] = h_im[j]
        ys.append(jnp.dot(hb_ref[:, s * 1024:(s + 1) * 1024], wc_ref[s], preferred_element_type=F32))
    y2 = jnp.concatenate(ys, axis=1)
    y_hi, y_lo = _split2(y2)
    yf = jnp.dot(pft_ref[...], y_hi, preferred_element_type=F32) + jnp.dot(pft_ref[...], y_lo, preferred_element_type=F32)
    yb = jnp.dot(pbt_ref[...], y_hi, preferred_element_type=F32) + jnp.dot(pbt_ref[...], y_lo, preferred_element_type=F32)
    return yf, yb, h_re[-1]


def _s5_call(u, l, w_bu, w_c, are, aim, h0, tc, tiles_per_pass):
    nb, length, _ = u.shape
    n = length // tc
    rows = nb * tc
    pf, pb = _s5_perms(nb, tc)
    const = lambda shape: pl.BlockSpec(shape, lambda i: (0,) * len(shape))
    layer = lambda shape: pl.BlockSpec((None,) + shape, lambda i: (l,) + (0,) * len(shape))
    blk = lambda index_map: pl.BlockSpec((nb, tc, S5_WIDTH), index_map)
    out_f = pl.BlockSpec((nb, 2 * tc, S5_WIDTH), lambda i: (0, i, 0))
    out_b = pl.BlockSpec((nb, 2 * tc, S5_WIDTH), lambda i: (0, n // 2 - 1 - i, 0))
    second = lambda i: jnp.minimum(2 * i + 2, n - 1)
    return pl.pallas_call(
        functools.partial(_s5_kernel, nb=nb, tc=tc, tiles_per_pass=tiles_per_pass),
        grid=(n // 2,),
        in_specs=[blk(lambda i: (0, 2 * i + 1, 0)), blk(lambda i: (0, n - 2 - 2 * i, 0)),
                  blk(lambda i: (0, second(i), 0)), blk(lambda i: (0, n - 1 - second(i), 0)),
                  blk(lambda i: (0, 0, 0)), blk(lambda i: (0, n - 1, 0)),
                  const((2 * rows, rows)), const((2 * rows, rows)), const((rows, 2 * rows)), const((rows, 2 * rows)),
                  layer((S5_SLABS, 2 * LANES, 1024)), layer((S5_SLABS, 1024, LANES)),
                  layer((2 * nb, S5_NC)), layer((2 * nb, S5_NC)), const((2 * nb, 2 * S5_NC))],
        out_specs=(out_f, out_b, const((2 * nb, 2 * S5_NC))),
        out_shape=(jax.ShapeDtypeStruct(u.shape, F32), jax.ShapeDtypeStruct(u.shape, F32),
                   jax.ShapeDtypeStruct((2 * nb, 2 * S5_NC), F32)),
        scratch_shapes=[pltpu.VMEM((2 * rows, 2 * S5_NC), F32), pltpu.VMEM((2 * rows, 2 * S5_NC), F32),
                        pltpu.VMEM((2 * rows, 2 * S5_NC), BF16), pltpu.VMEM((2 * nb, 2 * S5_NC), F32)],
        compiler_params=pltpu.CompilerParams(dimension_semantics=("arbitrary",), vmem_limit_bytes=VMEM_LIMIT),
        name="s5_scan",
    )(u, u, u, u, u, u, jnp.asarray(pf, BF16), jnp.asarray(pb, BF16), jnp.asarray(pf.T, BF16), jnp.asarray(pb.T, BF16),
      w_bu, w_c, are, aim, h0)


def _gla_kernel(*refs, length, has_s0, nsq):
    q_ref, k_ref, v_ref, gl_ref, wg_ref, bg_ref, ng_ref = refs[:7]
    s0_ref = refs[7] if has_s0 else None
    o_ref, sfin_ref, sf_ref, sb_ref = refs[-4:]
    c = GLA_CHUNK
    n = length // c
    ri = lax.broadcasted_iota(jnp.int32, (c, c), 0)
    ci = lax.broadcasted_iota(jnp.int32, (c, c), 1)
    lane = lax.broadcasted_iota(jnp.int32, (1, GLA_KEY), 1)
    head_mask = [(lane >= h * GLA_DK) & (lane < (h + 1) * GLA_DK) for h in range(GLA_HEADS)]
    ng = ng_ref[...]
    causal = [ci <= ri, ci >= ri]
    tri = [m.astype(BF16) for m in causal]
    end = (c - 1, 0)
    chains = [(sq, d) for sq in range(nsq) for d in (0, 1)]
    key_rows = lambda h: slice(h * GLA_DK, (h + 1) * GLA_DK)
    val_cols = lambda h: slice(h * GLA_DV, (h + 1) * GLA_DV)

    sb_ref[...] = jnp.zeros_like(sb_ref)
    for sq, d in chains:
        for h in range(GLA_HEADS):
            if has_s0:
                sf_ref[sq, d, h] = s0_ref[sq, d, h]
                sb_ref[sq, d, key_rows(h), val_cols(h)] = s0_ref[sq, d, h].astype(BF16)
            else:
                sf_ref[sq, d, h] = jnp.zeros((GLA_DK, GLA_DV), F32)

    def chunk_group(r):
        qc = [q_ref[sq, pl.ds(r[d], c), :] * (GLA_DK ** -0.5) for sq, d in chains]
        kc = [k_ref[sq, pl.ds(r[d], c), :] for sq, d in chains]
        vc = [v_ref[sq, pl.ds(r[d], c), :] for sq, d in chains]
        logits = [_bdot(gl_ref[sq, pl.ds(r[d], c), :], wg_ref[d]) + bg_ref[d] for sq, d in chains]
        g = [_split2(_log_sigmoid(x) * (1.0 / GLA_GATE_NORM)) for x in logits]
        b = [jnp.dot(tri[d], hi, preferred_element_type=F32) + jnp.dot(tri[d], lo, preferred_element_type=F32)
             for (sq, d), (hi, lo) in zip(chains, g)]
        bt = [x.T for x in b]
        kt = [x.T for x in kc]
        b_mid = [x[c // 2:c // 2 + 1, :] for x in b]
        bt_mid = [x[:, c // 2:c // 2 + 1] for x in bt]
        bt_end = [x[:, end[d]:end[d] + 1] for (sq, d), x in zip(chains, bt)]
        q_in = [(x * jnp.exp(y)).astype(BF16) for x, y in zip(qc, b)]
        q_mid = [(x * jnp.exp(y - m)).astype(BF16) for x, y, m in zip(qc, b, b_mid)]
        k_mid_t = [(x * jnp.exp(m - y)).astype(BF16) for x, y, m in zip(kt, bt, bt_mid)]
        k_end_t = [(x * jnp.exp(e - y)).astype(BF16) for x, y, e in zip(kt, bt, bt_end)]
        q_st = [jnp.concatenate([jnp.where(head_mask[h], x, jnp.zeros_like(x)) for h in range(GLA_HEADS)], axis=0)
                for x in q_mid]
        a = [jnp.dot(x, y, preferred_element_type=F32) for x, y in zip(q_st, k_mid_t)]
        o_inter = [jnp.dot(x, sb_ref[sq, d], preferred_element_type=F32) for (sq, d), x in zip(chains, q_in)]
        parts = [[] for _ in chains]
        for h in range(GLA_HEADS):
            for i, (sq, d) in enumerate(chains):
                a_h = jnp.where(causal[d], a[i][h * c:(h + 1) * c, :], 0.0).astype(BF16)
                parts[i].append(jnp.dot(a_h, vc[i][:, val_cols(h)], preferred_element_type=F32))
        for h in range(GLA_HEADS):
            for i, (sq, d) in enumerate(chains):
                upd = jnp.dot(k_end_t[i][key_rows(h), :], vc[i][:, val_cols(h)], preferred_element_type=F32)
                s_new = jnp.exp(bt_end[i][key_rows(h), :]) * sf_ref[sq, d, h] + upd
                sf_ref[sq, d, h] = s_new
                sb_ref[sq, d, key_rows(h), val_cols(h)] = s_new.astype(BF16)
        return [x + jnp.concatenate(p, axis=1) for x, p in zip(o_inter, parts)]

    def finish(o, sq, r0):
        o = o + o_ref[sq, pl.ds(r0, c), :]
        normed = []
        for h in range(GLA_HEADS):
            oh = o[:, val_cols(h)]
            ms = jnp.mean(oh * oh, axis=-1, keepdims=True)
            normed.append(oh * lax.rsqrt(ms + EPS))
        o_ref[sq, pl.ds(r0, c), :] = jnp.concatenate(normed, axis=1) * ng

    def first_half(step, carry):
        r = (pl.multiple_of(step * c, c), pl.multiple_of((n - 1 - step) * c, c))
        for (sq, d), o in zip(chains, chunk_group(r)):
            o_ref[sq, pl.ds(r[d], c), :] = o
        return carry

    def second_half(step, carry):
        r = (pl.multiple_of(step * c, c), pl.multiple_of((n - 1 - step) * c, c))
        for (sq, d), o in zip(chains, chunk_group(r)):
            finish(o, sq, r[d])
        return carry

    lax.fori_loop(0, n // 2, first_half, 0)
    lax.fori_loop(n // 2, n, second_half, 0)
    for sq, d in chains:
        for h in range(GLA_HEADS):
            sfin_ref[sq, d, h] = sf_ref[sq, d, h]


def _gla_call(q, k, v, gl, l, wg, bg, ng, s0, nsq=2):
    nseq, length, _ = q.shape
    seq = lambda width: pl.BlockSpec((nsq, length, width), lambda b: (b, 0, 0))
    layer = lambda shape: pl.BlockSpec((None,) + shape, lambda b: (l,) + (0,) * len(shape))
    in_specs = [seq(GLA_KEY), seq(GLA_KEY), seq(GLA_WIDTH), seq(LANES),
                layer((2, LANES, GLA_KEY)), layer((2, 1, GLA_KEY)), layer((1, GLA_WIDTH))]
    args = [q, k, v, gl, wg, bg, ng]
    if s0 is not None:
        in_specs.append(pl.BlockSpec((nsq, None, 2, GLA_HEADS, GLA_DK, GLA_DV), lambda b: (b, l, 0, 0, 0, 0)))
        args.append(s0)
    return pl.pallas_call(
        functools.partial(_gla_kernel, length=length, has_s0=s0 is not None, nsq=nsq),
        grid=(nseq // nsq,),
        in_specs=in_specs,
        out_specs=(seq(GLA_WIDTH), pl.BlockSpec((nsq, 2, GLA_HEADS, GLA_DK, GLA_DV), lambda b: (b, 0, 0, 0, 0))),
        out_shape=(jax.ShapeDtypeStruct((nseq, length, GLA_WIDTH), F32),
                   jax.ShapeDtypeStruct((nseq, 2, GLA_HEADS, GLA_DK, GLA_DV), F32)),
        scratch_shapes=[pltpu.VMEM((nsq, 2, GLA_HEADS, GLA_DK, GLA_DV), F32),
                        pltpu.VMEM((nsq, 2, GLA_KEY, GLA_WIDTH), BF16)],
        compiler_params=pltpu.CompilerParams(dimension_semantics=("arbitrary",), vmem_limit_bytes=VMEM_LIMIT),
        name="gla",
    )(*args)


def _mix_kernel(*refs, has_pe, last):
    if has_pe:
        (x_ref, pe_ref, yf_ref, yb_ref, u_ref, o_ref, shift_ref, scale_ref, gate_ref, ng_ref, wg_ref, d_ref,
         wglu_ref, bglu_ref, wpa_ref, wpb_ref, wo_ref, fng_ref, out_ref) = refs
        x = x_ref[0] + pe_ref[...]
    else:
        (x_ref, yf_ref, yb_ref, u_ref, o_ref, shift_ref, scale_ref, gate_ref, ng_ref, wg_ref, d_ref,
         wglu_ref, bglu_ref, wpa_ref, wpb_ref, wo_ref, fng_ref, out_ref) = refs
        x = x_ref[0]
    h = _modulated_norm(x, ng_ref[...], 1.0 + scale_ref[...], shift_ref[...])
    gts = _bdot_t(h, wg_ref[...])
    gate_a = gts[:, :S5_WIDTH]
    gate_b = gts[:, S5_WIDTH:2 * S5_WIDTH]
    m_a = gts[:, 2 * S5_WIDTH:2 * S5_WIDTH + D_MODEL]
    m_b = gts[:, 2 * S5_WIDTH + D_MODEL:]
    y = _gelu_tanh(yf_ref[0] + yb_ref[0] + d_ref[...] * u_ref[0])
    y = y * _sigmoid(_bdot(y, wglu_ref[...]) + bglu_ref[...])
    y_a = y * _silu(gate_a)
    y_b = o_ref[0] * _silu(gate_b)
    merged = _sigmoid(m_a) * _bdot(y_a, wpa_ref[...]) + _sigmoid(m_b) * _bdot(y_b, wpb_ref[...])
    xn = x + gate_ref[...] * _bdot(merged, wo_ref[...])
    if last:
        ms = jnp.mean(xn * xn, axis=-1, keepdims=True)
        xn = xn * lax.rsqrt(ms + EPS) * fng_ref[...]
    out_ref[0] = xn


def _mix_call(x, pe, yf, yb, u, o, mod, l, cond_row, ng, wg, s5_d, wglu, bglu, wpa, wpb, wo, fng, tm, last):
    nseq, length, _ = x.shape
    row = lambda width: pl.BlockSpec((1, tm, width), lambda b, j: (b, j, 0))
    in_specs = [row(D_MODEL)]
    args = [x]
    if pe is not None:
        in_specs.append(pl.BlockSpec((tm, D_MODEL), lambda b, j: (j, 0)))
        args.append(pe)
    in_specs += [row(S5_WIDTH), row(S5_WIDTH), row(S5_WIDTH), row(GLA_WIDTH),
                 _mod_spec(l, cond_row, 0), _mod_spec(l, cond_row, 1), _mod_spec(l, cond_row, 2),
                 _layer_spec(l, (1, D_MODEL)), _layer_spec(l, (GATE_W, D_MODEL)), _layer_spec(l, (1, S5_WIDTH)),
                 _layer_spec(l, (S5_WIDTH, S5_WIDTH)), _layer_spec(l, (1, S5_WIDTH)),
                 _layer_spec(l, (S5_WIDTH, D_MODEL)), _layer_spec(l, (GLA_WIDTH, D_MODEL)),
                 _layer_spec(l, (D_MODEL, D_MODEL)), pl.BlockSpec((1, D_MODEL), lambda b, j: (0, 0))]
    args += [yf, yb, u, o, mod, mod, mod, ng, wg, s5_d, wglu, bglu, wpa, wpb, wo, fng]
    return pl.pallas_call(
        functools.partial(_mix_kernel, has_pe=pe is not None, last=last),
        grid=(nseq, length // tm),
        in_specs=in_specs,
        out_specs=row(D_MODEL),
        out_shape=jax.ShapeDtypeStruct(x.shape, F32),
        compiler_params=pltpu.CompilerParams(dimension_semantics=("arbitrary", "arbitrary"),
                                             vmem_limit_bytes=VMEM_LIMIT),
        name="mix",
    )(*args)


def _grid_pos_embed(length, dim):
    rows = length // GRID_W
    quarter = dim // 4
    freqs = jnp.exp(-math.log(10000.0) * jnp.arange(quarter, dtype=F32) / quarter)

    def sincos(pos):
        ang = pos.astype(F32)[:, None] * freqs[None, :]
        return jnp.concatenate([jnp.sin(ang), jnp.cos(ang)], axis=-1)

    er = sincos(jnp.arange(rows))
    ec = sincos(jnp.arange(GRID_W))
    pe = jnp.concatenate([jnp.broadcast_to(er[:, None, :], (rows, GRID_W, dim // 2)),
                          jnp.broadcast_to(ec[None, :, :], (rows, GRID_W, dim // 2))], axis=-1)
    return pe.reshape(rows * GRID_W, dim)


def kernel(x_prompt, x_sample, c, state_s5_re, state_s5_im, state_gla, c_ctx, norm_g, w_mod, b_mod, w_in, gla_wg_up,
           gla_bg, gla_norm_g, s5_lam_re, s5_lam_im, s5_log_dt, s5_b_re, s5_b_im, s5_c_re, s5_c_im, s5_d, w_glu,
           b_glu, w_pa, w_pb, w_o, final_norm_g):
    depth = w_in.shape[0]
    bp, lp, _ = x_prompt.shape
    bs, ls, _ = x_sample.shape

    cond8 = jnp.zeros((8, D_MODEL), F32).at[0].set(c_ctx).at[1:1 + bs].set(c)
    mod = _mod_call(cond8, w_mod, b_mod)

    a_re, a_im, w_bu, w_c = _s5_params(s5_lam_re, s5_lam_im, s5_log_dt, s5_b_re, s5_b_im, s5_c_re, s5_c_im)

    w_proj, w_gate = _split_w_in(w_in)
    wg_up = jnp.concatenate([gla_wg_up, jnp.zeros((depth, 2, LANES - GLA_RANK, GLA_KEY), F32)], axis=2).astype(BF16)
    w_glu_b, w_pa_b, w_pb_b, w_o_b = (w.astype(BF16) for w in (w_glu, w_pa, w_pb, w_o))

    vec = lambda a: a[:, None, :]
    mod = mod.reshape(depth, 8, 1, 3 * D_MODEL)
    ng, gng, bg = vec(norm_g), vec(gla_norm_g), gla_bg[:, :, None, :]
    common = (ng, w_gate, vec(s5_d), w_glu_b, vec(b_glu), w_pa_b, w_pb_b, w_o_b, final_norm_g[None])
    are_p, aim_p = jnp.repeat(a_re, bp, axis=1), jnp.repeat(a_im, bp, axis=1)
    are_s, aim_s = jnp.repeat(a_re, bs, axis=1), jnp.repeat(a_im, bs, axis=1)
    ctx_row = lambda b: 0
    cond_row = lambda b: b + 1

    pe = _grid_pos_embed(ls, D_MODEL)
    zero_h0 = jnp.zeros((2 * bp, 2 * S5_NC), F32)
    xp, xs = x_prompt, x_sample
    new_re, new_im, new_gla = [], [], []
    for l in range(depth):
        last = l == depth - 1

        u, q, k, v, gl = _proj_call(xp, None, mod, l, ctx_row, ng, w_proj, tm=lp, nsb=4)
        yf, yb, hfin = _s5_call(u, l, w_bu, w_c, are_p, aim_p, zero_h0, tc=8, tiles_per_pass=2)
        o, sfin = _gla_call(q, k, v, gl, l, wg_up, bg, gng, None)
        new_gla.append(sfin)
        xp = _mix_call(xp, None, yf, yb, u, o, mod, l, ctx_row, *common, tm=lp, last=last)
        re_l, im_l = _s5_cols_to_state(hfin.reshape(2, bp, 2 * S5_NC))
        new_re.append(jnp.swapaxes(re_l, 0, 1))
        new_im.append(jnp.swapaxes(im_l, 0, 1))

        pe_l = pe if l == 0 else None
        u, q, k, v, gl = _proj_call(xs, pe_l, mod, l, cond_row, ng, w_proj, tm=1024)
        h0 = _s5_state_to_cols(jnp.swapaxes(state_s5_re[:, l], 0, 1), jnp.swapaxes(state_s5_im[:, l], 0, 1))
        yf, yb, _ = _s5_call(u, l, w_bu, w_c, are_s, aim_s, h0.reshape(2 * bs, 2 * S5_NC), tc=32, tiles_per_pass=4)
        o, _ = _gla_call(q, k, v, gl, l, wg_up, bg, gng, state_gla)
        xs = _mix_call(xs, pe_l, yf, yb, u, o, mod, l, cond_row, *common, tm=512, last=last)

    return (xp, xs, jnp.stack(new_re, axis=1), jnp.stack(new_im, axis=1), jnp.stack(new_gla, axis=1))
```

```python
import functools
import math

import numpy as np
import jax
import jax.numpy as jnp
from jax import lax
from jax.experimental import pallas as pl
from jax.experimental.pallas import tpu as pltpu

F32 = jnp.float32
BF16 = jnp.bfloat16

D_MODEL = 1024
GRID_W = 64
S5_WIDTH = 512
S5_GROUPS = 32
S5_CH = 16
S5_STATE = 64
S5_NC = S5_GROUPS * S5_STATE
S5_SLABS = 4
GLA_HEADS = 4
GLA_DK = 64
GLA_DV = 128
GLA_KEY = GLA_HEADS * GLA_DK
GLA_WIDTH = GLA_HEADS * GLA_DV
GLA_RANK = 16
GLA_GATE_NORM = 16.0
GLA_CHUNK = 128
EPS = 1e-6
LANES = 128
PROJ_W = S5_WIDTH + 2 * GLA_KEY + GLA_WIDTH + LANES
GATE_W = 2 * S5_WIDTH + 2 * D_MODEL
VMEM_LIMIT = 52 * 1024 * 1024


def _bdot(a, b):
    return jnp.dot(a.astype(BF16), b.astype(BF16), preferred_element_type=F32)


def _bdot_t(a, bt):
    return lax.dot_general(a.astype(BF16), bt.astype(BF16), (((1,), (1,)), ((), ())), preferred_element_type=F32)


def _split2(x):
    hi = x.astype(BF16)
    lo = (x - hi.astype(F32)).astype(BF16)
    return hi, lo


def _sigmoid(x):
    return 1.0 / (1.0 + jnp.exp(-x))


def _silu(x):
    return x * _sigmoid(x)


def _gelu_tanh(x):
    c = math.sqrt(2.0 / math.pi)
    return 0.5 * x * (1.0 + jnp.tanh(c * (x + 0.044715 * (x * x * x))))


def _exact_zero_of(x):
    bits = lax.bitcast_convert_type(x, jnp.int32)
    return lax.shift_right_logical(lax.shift_right_logical(bits, 16), 16).astype(F32)


def _log_sigmoid(x):
    return jnp.minimum(x, 0.0) - jnp.log(1.0 + jnp.exp(-jnp.abs(x)))


def _modulated_norm(x, ng, scale1p, shift):
    ms = jnp.mean(x * x, axis=-1, keepdims=True)
    return (x * lax.rsqrt(ms + EPS) * ng) * scale1p + shift


def _mod_kernel(cond_ref, w_ref, b_ref, out_ref):
    c = cond_ref[...]
    s_hi, s_lo = _split2(_silu(c))
    w_hi, w_lo = _split2(w_ref[0])
    acc = jnp.dot(s_hi, w_hi, preferred_element_type=F32)
    acc += jnp.dot(s_lo, w_hi, preferred_element_type=F32)
    acc += jnp.dot(s_hi, w_lo, preferred_element_type=F32)
    out_ref[0] = acc + b_ref[0]


def _mod_call(cond8, w_mod, b_mod):
    depth = w_mod.shape[0]
    nb = 768
    return pl.pallas_call(
        _mod_kernel,
        grid=(depth, 3 * D_MODEL // nb),
        in_specs=[
            pl.BlockSpec((8, D_MODEL), lambda l, j: (0, 0)),
            pl.BlockSpec((1, D_MODEL, nb), lambda l, j: (l, 0, j)),
            pl.BlockSpec((1, 1, nb), lambda l, j: (l, 0, j)),
        ],
        out_specs=pl.BlockSpec((1, 8, nb), lambda l, j: (l, 0, j)),
        out_shape=jax.ShapeDtypeStruct((depth, 8, 3 * D_MODEL), F32),
        compiler_params=pltpu.CompilerParams(dimension_semantics=("arbitrary", "arbitrary")),
        name="adaln_mod",
    )(cond8, w_mod, b_mod.reshape(depth, 1, 3 * D_MODEL))


def _zoh(lre, lim, ldt):
    dt = jnp.exp(ldt)
    mag = jnp.exp(lre * dt)
    a_re = mag * jnp.cos(lim * dt)
    a_im = mag * jnp.sin(lim * dt)
    n_re = a_re - 1.0
    inv = 1.0 / (lre * lre + lim * lim)
    return a_re, a_im, (n_re * lre + a_im * lim) * inv, (a_im * lre - n_re * lim) * inv


def _zoh_kernel(lre_ref, lim_ref, ldt_ref, lre_x_ref, lim_x_ref, ldt_x_ref, bre_ref, bim_ref, cre_ref, cim_ref,
                are_ref, aim_ref, wbu_ref, wc_ref, *, depth):
    a_re, a_im, _, _ = _zoh(lre_ref[...], lim_ref[...], ldt_ref[...])
    are_ref[...] = a_re
    aim_ref[...] = a_im
    _, _, c_re, c_im = _zoh(lre_x_ref[...], lim_x_ref[...], ldt_x_ref[...])
    bre = bre_ref[...]
    bim = bim_ref[...]
    bbar = [c_re * bre - c_im * bim, c_re * bim + c_im * bre]
    cmat = [cre_ref[...], -cim_ref[...]]
    wbu_ref[...] = jnp.zeros_like(wbu_ref)
    wc_ref[...] = jnp.zeros_like(wc_ref)
    for l in range(depth):
        for s in range(S5_SLABS):
            for gl in range(8):
                g = 8 * s + gl
                for ri in range(2):
                    col = ri * 8 * S5_STATE + gl * S5_STATE
                    for d in range(2):
                        src = ((l * 2 + d) * S5_GROUPS + g) * S5_CH
                        row = (d * 8 + gl) * S5_CH
                        wbu_ref[l, s, row:row + S5_CH, col:col + S5_STATE] = bbar[ri][src:src + S5_CH, :].astype(BF16)
                    src = (l * S5_GROUPS + g) * S5_STATE
                    wc_ref[l, s, col:col + S5_STATE, gl * S5_CH:(gl + 1) * S5_CH] = (
                        cmat[ri][src:src + S5_STATE, :].astype(BF16))


def _s5_params(s5_lam_re, s5_lam_im, s5_log_dt, s5_b_re, s5_b_im, s5_c_re, s5_c_im):
    depth = s5_lam_re.shape[0]
    full = (depth, 2, S5_GROUPS, S5_CH, S5_STATE)
    expand = lambda a: jnp.broadcast_to(a, full).reshape(-1, S5_STATE)
    small = lambda a: a.reshape(depth * 2 * S5_GROUPS, -1)
    ct = lambda a: jnp.swapaxes(a, -1, -2).reshape(-1, S5_CH)
    a_re, a_im, w_bu, w_c = pl.pallas_call(
        functools.partial(_zoh_kernel, depth=depth),
        out_shape=(jax.ShapeDtypeStruct((depth * 2 * S5_GROUPS, S5_STATE), F32),
                   jax.ShapeDtypeStruct((depth * 2 * S5_GROUPS, S5_STATE), F32),
                   jax.ShapeDtypeStruct((depth, S5_SLABS, 2 * LANES, 2 * 8 * S5_STATE), BF16),
                   jax.ShapeDtypeStruct((depth, S5_SLABS, 2 * 8 * S5_STATE, LANES), BF16)),
        name="s5_zoh",
    )(small(s5_lam_re), small(s5_lam_im), small(s5_log_dt),
      expand(s5_lam_re[:, :, :, None, :]), expand(s5_lam_im[:, :, :, None, :]), expand(s5_log_dt[:, :, :, None, None]),
      expand(jnp.swapaxes(s5_b_re, -1, -2)[:, None]), expand(jnp.swapaxes(s5_b_im, -1, -2)[:, None]),
      ct(s5_c_re), ct(s5_c_im))
    return a_re.reshape(depth, 2, S5_NC), a_im.reshape(depth, 2, S5_NC), w_bu, w_c


def _s5_state_to_cols(re, im):
    lead = re.shape[:-2]
    st = jnp.stack([re.reshape(lead + (S5_SLABS, 512)), im.reshape(lead + (S5_SLABS, 512))], axis=-2)
    return st.reshape(lead + (2 * S5_NC,))


def _s5_cols_to_state(cols):
    lead = cols.shape[:-1]
    st = cols.reshape(lead + (S5_SLABS, 2, 512))
    re = st[..., 0, :].reshape(lead + (S5_GROUPS, S5_STATE))
    im = st[..., 1, :].reshape(lead + (S5_GROUPS, S5_STATE))
    return re, im


_IN_OFFS = [int(o) for o in np.cumsum([0, S5_WIDTH, S5_WIDTH, GLA_KEY, GLA_KEY, GLA_WIDTH, GLA_WIDTH, GLA_RANK,
                                       D_MODEL, D_MODEL])]


def _split_w_in_kernel(u_ref, qkv_ref, code_ref, ga_ref, gb_ref, m_ref, proj_ref, gate_ref):
    rows = 0
    for src in (u_ref, qkv_ref, code_ref):
        proj_ref[rows:rows + src.shape[1], :] = src[0].astype(BF16)
        rows += src.shape[1]
    proj_ref[rows:, :] = jnp.zeros((PROJ_W - rows, proj_ref.shape[1]), BF16)
    rows = 0
    for src in (ga_ref, gb_ref, m_ref):
        gate_ref[rows:rows + src.shape[1], :] = src[0].astype(BF16)
        rows += src.shape[1]


def _split_w_in(w_in):
    o = _IN_OFFS
    depth = w_in.shape[0]
    wt = jnp.swapaxes(w_in, 1, 2)
    tn = D_MODEL // 2
    rows = lambda lo, hi: pl.BlockSpec((pl.Element(1), pl.Element(hi - lo), pl.Element(tn)),
                                       lambda l, j: (l, lo, j * tn))
    out = lambda n: pl.BlockSpec((None, n, tn), lambda l, j: (l, 0, j))
    return pl.pallas_call(
        _split_w_in_kernel,
        grid=(depth, D_MODEL // tn),
        in_specs=[rows(o[0], o[1]), rows(o[2], o[5]), rows(o[6], o[7]), rows(o[1], o[2]), rows(o[5], o[6]),
                  rows(o[7], o[9])],
        out_specs=(out(PROJ_W), out(GATE_W)),
        out_shape=(jax.ShapeDtypeStruct((depth, PROJ_W, D_MODEL), BF16),
                   jax.ShapeDtypeStruct((depth, GATE_W, D_MODEL), BF16)),
        compiler_params=pltpu.CompilerParams(dimension_semantics=("arbitrary", "arbitrary"),
                                             vmem_limit_bytes=VMEM_LIMIT),
        name="split_w_in",
    )(wt, wt, wt, wt, wt, wt)


def _proj_kernel(*refs, has_pe):
    if has_pe:
        x_ref, pe_ref, shift_ref, scale_ref, ng_ref, w_ref, u_ref, q_ref, k_ref, v_ref, gl_ref = refs
        x = x_ref[...] + pe_ref[...]
    else:
        x_ref, shift_ref, scale_ref, ng_ref, w_ref, u_ref, q_ref, k_ref, v_ref, gl_ref = refs
        x = x_ref[...]
    nsb, tm, _ = x.shape
    x = x.reshape(nsb * tm, D_MODEL)
    h = _modulated_norm(x, ng_ref[...], 1.0 + scale_ref[...], shift_ref[...])
    p = _bdot_t(h, w_ref[...])
    o = 0
    for ref, width in ((u_ref, S5_WIDTH), (q_ref, GLA_KEY), (k_ref, GLA_KEY), (v_ref, GLA_WIDTH), (gl_ref, LANES)):
        ref[...] = p[:, o:o + width].reshape(nsb, tm, width).astype(ref.dtype)
        o += width


def _mod_spec(l, cond_row, part):
    return pl.BlockSpec((None, None, 1, D_MODEL), lambda b, j: (l, cond_row(b), 0, part))


def _layer_spec(l, shape):
    return pl.BlockSpec((None,) + shape, lambda b, j: (l,) + (0,) * len(shape))


def _proj_call(x, pe, mod, l, cond_row, ng, w, tm, nsb=1):
    nseq, length, _ = x.shape
    row = lambda width: pl.BlockSpec((nsb, tm, width), lambda b, j: (b, j, 0))
    in_specs = [row(D_MODEL)]
    args = [x]
    if pe is not None:
        in_specs.append(pl.BlockSpec((tm, D_MODEL), lambda b, j: (j, 0)))
        args.append(pe)
    in_specs += [_mod_spec(l, cond_row, 0), _mod_spec(l, cond_row, 1), _layer_spec(l, (1, D_MODEL)),
                 _layer_spec(l, (PROJ_W, D_MODEL))]
    args += [mod, mod, ng, w]
    sds = lambda width, dt: jax.ShapeDtypeStruct((nseq, length, width), dt)
    return pl.pallas_call(
        functools.partial(_proj_kernel, has_pe=pe is not None),
        grid=(nseq // nsb, length // tm),
        in_specs=in_specs,
        out_specs=(row(S5_WIDTH), row(GLA_KEY), row(GLA_KEY), row(GLA_WIDTH), row(LANES)),
        out_shape=(sds(S5_WIDTH, F32), sds(GLA_KEY, F32), sds(GLA_KEY, F32), sds(GLA_WIDTH, BF16), sds(LANES, F32)),
        compiler_params=pltpu.CompilerParams(dimension_semantics=("arbitrary", "arbitrary"),
                                             vmem_limit_bytes=VMEM_LIMIT),
        name="proj",
    )(*args)


def _s5_perms(nb, tc):
    r = nb * tc
    pf = np.zeros((2 * r, r), np.float32)
    pb = np.zeros((2 * r, r), np.float32)
    for t in range(tc):
        for b in range(nb):
            pf[t * 2 * nb + b, b * tc + t] = 1.0
            pb[t * 2 * nb + nb + b, b * tc + (tc - 1 - t)] = 1.0
    return pf, pb


def _s5_kernel(uf1_ref, ub1_ref, uf2_ref, ub2_ref, uf0_ref, ub0_ref, pf_ref, pb_ref, pft_ref, pbt_ref, wbu_ref, wc_ref,
               are_ref, aim_ref, h0_ref, yf_ref, yb_ref, hfin_ref, bua_ref, bub_ref, hb_ref, st_ref,
               *, nb, tc, tiles_per_pass):
    i = pl.program_id(0)
    rows = nb * tc
    rs = 2 * nb

    def packed_inputs(uf_ref, ub_ref):
        uf = uf_ref[...].reshape(rows, S5_WIDTH).astype(BF16)
        ub = ub_ref[...].reshape(rows, S5_WIDTH).astype(BF16)
        u2f = jnp.dot(pf_ref[...], uf, preferred_element_type=F32).astype(BF16)
        u2b = jnp.dot(pb_ref[...], ub, preferred_element_type=F32).astype(BF16)
        return [jnp.concatenate([u2f[:, s * LANES:(s + 1) * LANES], u2b[:, s * LANES:(s + 1) * LANES]], axis=1)
                for s in range(S5_SLABS)]

    def bu_slab(lhs, s, dst_ref):
        dst_ref[:, s * 1024:(s + 1) * 1024] = jnp.dot(lhs[s], wbu_ref[s], preferred_element_type=F32)

    @pl.when(i == 0)
    def _():
        st_ref[...] = h0_ref[...]
        lhs0 = packed_inputs(uf0_ref, ub0_ref)
        for s in range(S5_SLABS):
            bu_slab(lhs0, s, bua_ref)

    half_step = functools.partial(_s5_half_step, pft_ref=pft_ref, pbt_ref=pbt_ref, wc_ref=wc_ref, are_ref=are_ref,
                                  aim_ref=aim_ref, hb_ref=hb_ref, st_ref=st_ref, bu_slab=bu_slab, rs=rs, tc=tc,
                                  tiles_per_pass=tiles_per_pass)
    yf, yb, tail = half_step(bua_ref, bub_ref, packed_inputs(uf1_ref, ub1_ref))
    yf_ref[:, 0:tc, :] = yf.reshape(nb, tc, S5_WIDTH)
    yb_ref[:, tc:2 * tc, :] = yb.reshape(nb, tc, S5_WIDTH)
    yf, yb, _ = half_step(bub_ref, bua_ref, packed_inputs(uf2_ref, ub2_ref), after=tail)
    yf_ref[:, tc:2 * tc, :] = yf.reshape(nb, tc, S5_WIDTH)
    yb_ref[:, 0:tc, :] = yb.reshape(nb, tc, S5_WIDTH)
    hfin_ref[...] = st_ref[...]


def _s5_half_step(cur_ref, nxt_ref, lhs_next, *, pft_ref, pbt_ref, wc_ref, are_ref, aim_ref, hb_ref, st_ref, bu_slab,
                  rs, tc, tiles_per_pass, after=None):
    grp = max(1, 16 // rs)
    ys = []
    for s in range(S5_SLABS):
        bu_slab(lhs_next, s, nxt_ref)
        for c0 in range(4 * s, 4 * s + 4, tiles_per_pass):
            cts = list(range(c0, c0 + tiles_per_pass))
            cre = [(ct // 4) * 1024 + (ct % 4) * LANES for ct in cts]
            cim = [c + 512 for c in cre]
            a_re = [are_ref[:, ct * LANES:(ct + 1) * LANES] for ct in cts]
            a_im = [aim_ref[:, ct * LANES:(ct + 1) * LANES] for ct in cts]
            h_re = [st_ref[:, c:c + LANES] for c in cre]
            h_im = [st_ref[:, c:c + LANES] for c in cim]
            if after is not None and c0 == 0:
                h_re[0] = h_re[0] + _exact_zero_of(after)
            for t0 in range(0, tc, grp):
                out_re = [[] for _ in cts]
                out_im = [[] for _ in cts]
                for t in range(t0, t0 + grp):
                    r0 = t * rs
                    for j in range(len(cts)):
                        b_re = cur_ref[r0:r0 + rs, cre[j]:cre[j] + LANES]
                        b_im = cur_ref[r0:r0 + rs, cim[j]:cim[j] + LANES]
                        n_re = a_re[j] * h_re[j] - a_im[j] * h_im[j] + b_re
                        n_im = a_re[j] * h_im[j] + a_im[j] * h_re[j] + b_im
                        h_re[j], h_im[j] = n_re, n_im
                        out_re[j].append(n_re)
                        out_im[j].append(n_im)
                for j in range(len(cts)):
                    blk_re = out_re[j][0] if grp == 1 else jnp.concatenate(out_re[j], axis=0)
                    blk_im = out_im[j][0] if grp == 1 else jnp.concatenate(out_im[j], axis=0)
                    hb_ref[t0 * rs:(t0 + grp) * rs, cre[j]:cre[j] + LANES] = blk_re.astype(BF16)
                    hb_ref[t0 * rs:(t0 + grp) * rs, cim[j]:cim[j] + LANES] = blk_im.astype(BF16)
            for j in range(len(cts)):
                st_ref[:, cre[j]:cre[j] + LANES] = h_re[j]
                st_ref[:, cim[j]:cim[j] + LANES] = h_im[j]
        ys.append(jnp.dot(hb_ref[:, s * 1024:(s + 1) * 1024], wc_ref[s], preferred_element_type=F32))
    y2 = jnp.concatenate(ys, axis=1)
    y_hi, y_lo = _split2(y2)
    yf = jnp.dot(pft_ref[...], y_hi, preferred_element_type=F32) + jnp.dot(pft_ref[...], y_lo, preferred_element_type=F32)
    yb = jnp.dot(pbt_ref[...], y_hi, preferred_element_type=F32) + jnp.dot(pbt_ref[...], y_lo, preferred_element_type=F32)
    return yf, yb, h_re[-1]


def _s5_call(u, l, w_bu, w_c, are, aim, h0, tc, tiles_per_pass):
    nb, length, _ = u.shape
    n = length // tc
    rows = nb * tc
    pf, pb = _s5_perms(nb, tc)
    const = lambda shape: pl.BlockSpec(shape, lambda i: (0,) * len(shape))
    layer = lambda shape: pl.BlockSpec((None,) + shape, lambda i: (l,) + (0,) * len(shape))
    blk = lambda index_map: pl.BlockSpec((nb, tc, S5_WIDTH), index_map)
    out_f = pl.BlockSpec((nb, 2 * tc, S5_WIDTH), lambda i: (0, i, 0))
    out_b = pl.BlockSpec((nb, 2 * tc, S5_WIDTH), lambda i: (0, n // 2 - 1 - i, 0))
    second = lambda i: jnp.minimum(2 * i + 2, n - 1)
    return pl.pallas_call(
        functools.partial(_s5_kernel, nb=nb, tc=tc, tiles_per_pass=tiles_per_pass),
        grid=(n // 2,),
        in_specs=[blk(lambda i: (0, 2 * i + 1, 0)), blk(lambda i: (0, n - 2 - 2 * i, 0)),
                  blk(lambda i: (0, second(i), 0)), blk(lambda i: (0, n - 1 - second(i), 0)),
                  blk(lambda i: (0, 0, 0)), blk(lambda i: (0, n - 1, 0)),
                  const((2 * rows, rows)), const((2 * rows, rows)), const((rows, 2 * rows)), const((rows, 2 * rows)),
                  layer((S5_SLABS, 2 * LANES, 1024)), layer((S5_SLABS, 1024, LANES)),
                  layer((2 * nb, S5_NC)), layer((2 * nb, S5_NC)), const((2 * nb, 2 * S5_NC))],
        out_specs=(out_f, out_b, const((2 * nb, 2 * S5_NC))),
        out_shape=(jax.ShapeDtypeStruct(u.shape, F32), jax.ShapeDtypeStruct(u.shape, F32),
                   jax.ShapeDtypeStruct((2 * nb, 2 * S5_NC), F32)),
        scratch_shapes=[pltpu.VMEM((2 * rows, 2 * S5_NC), F32), pltpu.VMEM((2 * rows, 2 * S5_NC), F32),
                        pltpu.VMEM((2 * rows, 2 * S5_NC), BF16), pltpu.VMEM((2 * nb, 2 * S5_NC), F32)],
        compiler_params=pltpu.CompilerParams(dimension_semantics=("arbitrary",), vmem_limit_bytes=VMEM_LIMIT),
        name="s5_scan",
    )(u, u, u, u, u, u, jnp.asarray(pf, BF16), jnp.asarray(pb, BF16), jnp.asarray(pf.T, BF16), jnp.asarray(pb.T, BF16),
      w_bu, w_c, are, aim, h0)


def _gla_kernel(*refs, length, has_s0, nsq):
    q_ref, k_ref, v_ref, gl_ref, wg_ref, bg_ref, ng_ref = refs[:7]
    s0_ref = refs[7] if has_s0 else None
    o_ref, sfin_ref, sf_ref, sb_ref = refs[-4:]
    c = GLA_CHUNK
    n = length // c
    ri = lax.broadcasted_iota(jnp.int32, (c, c), 0)
    ci = lax.broadcasted_iota(jnp.int32, (c, c), 1)
    lane = lax.broadcasted_iota(jnp.int32, (1, GLA_KEY), 1)
    head_mask = [(lane >= h * GLA_DK) & (lane < (h + 1) * GLA_DK) for h in range(GLA_HEADS)]
    ng = ng_ref[...]
    causal = [ci <= ri, ci >= ri]
    tri = [m.astype(BF16) for m in causal]
    end = (c - 1, 0)
    chains = [(sq, d) for sq in range(nsq) for d in (0, 1)]
    key_rows = lambda h: slice(h * GLA_DK, (h + 1) * GLA_DK)
    val_cols = lambda h: slice(h * GLA_DV, (h + 1) * GLA_DV)

    sb_ref[...] = jnp.zeros_like(sb_ref)
    for sq, d in chains:
        for h in range(GLA_HEADS):
            if has_s0:
                sf_ref[sq, d, h] = s0_ref[sq, d, h]
                sb_ref[sq, d, key_rows(h), val_cols(h)] = s0_ref[sq, d, h].astype(BF16)
            else:
                sf_ref[sq, d, h] = jnp.zeros((GLA_DK, GLA_DV), F32)

    def chunk_group(r):
        qc = [q_ref[sq, pl.ds(r[d], c), :] * (GLA_DK ** -0.5) for sq, d in chains]
        kc = [k_ref[sq, pl.ds(r[d], c), :] for sq, d in chains]
        vc = [v_ref[sq, pl.ds(r[d], c), :] for sq, d in chains]
        logits = [_bdot(gl_ref[sq, pl.ds(r[d], c), :], wg_ref[d]) + bg_ref[d] for sq, d in chains]
        g = [_split2(_log_sigmoid(x) * (1.0 / GLA_GATE_NORM)) for x in logits]
        b = [jnp.dot(tri[d], hi, preferred_element_type=F32) + jnp.dot(tri[d], lo, preferred_element_type=F32)
             for (sq, d), (hi, lo) in zip(chains, g)]
        bt = [x.T for x in b]
        kt = [x.T for x in kc]
        b_mid = [x[c // 2:c // 2 + 1, :] for x in b]
        bt_mid = [x[:, c // 2:c // 2 + 1] for x in bt]
        bt_end = [x[:, end[d]:end[d] + 1] for (sq, d), x in zip(chains, bt)]
        q_in = [(x * jnp.exp(y)).astype(BF16) for x, y in zip(qc, b)]
        q_mid = [(x * jnp.exp(y - m)).astype(BF16) for x, y, m in zip(qc, b, b_mid)]
        k_mid_t = [(x * jnp.exp(m - y)).astype(BF16) for x, y, m in zip(kt, bt, bt_mid)]
        k_end_t = [(x * jnp.exp(e - y)).astype(BF16) for x, y, e in zip(kt, bt, bt_end)]
        q_st = [jnp.concatenate([jnp.where(head_mask[h], x, jnp.zeros_like(x)) for h in range(GLA_HEADS)], axis=0)
                for x in q_mid]
        a = [jnp.dot(x, y, preferred_element_type=F32) for x, y in zip(q_st, k_mid_t)]
        o_inter = [jnp.dot(x, sb_ref[sq, d], preferred_element_type=F32) for (sq, d), x in zip(chains, q_in)]
        parts = [[] for _ in chains]
        for h in range(GLA_HEADS):
            for i, (sq, d) in enumerate(chains):
                a_h = jnp.where(causal[d], a[i][h * c:(h + 1) * c, :], 0.0).astype(BF16)
                parts[i].append(jnp.dot(a_h, vc[i][:, val_cols(h)], preferred_element_type=F32))
        for h in range(GLA_HEADS):
            for i, (sq, d) in enumerate(chains):
                upd = jnp.dot(k_end_t[i][key_rows(h), :], vc[i][:, val_cols(h)], preferred_element_type=F32)
                s_new = jnp.exp(bt_end[i][key_rows(h), :]) * sf_ref[sq, d, h] + upd
                sf_ref[sq, d, h] = s_new
                sb_ref[sq, d, key_rows(h), val_cols(h)] = s_new.astype(BF16)
        return [x + jnp.concatenate(p, axis=1) for x, p in zip(o_inter, parts)]

    def finish(o, sq, r0):
        o = o + o_ref[sq, pl.ds(r0, c), :]
        normed = []
        for h in range(GLA_HEADS):
            oh = o[:, val_cols(h)]
            ms = jnp.mean(oh * oh, axis=-1, keepdims=True)
            normed.append(oh * lax.rsqrt(ms + EPS))
        o_ref[sq, pl.ds(r0, c), :] = jnp.concatenate(normed, axis=1) * ng

    def first_half(step, carry):
        r = (pl.multiple_of(step * c, c), pl.multiple_of((n - 1 - step) * c, c))
        for (sq, d), o in zip(chains, chunk_group(r)):
            o_ref[sq, pl.ds(r[d], c), :] = o
        return carry

    def second_half(step, carry):
        r = (pl.multiple_of(step * c, c), pl.multiple_of((n - 1 - step) * c, c))
        for (sq, d), o in zip(chains, chunk_group(r)):
            finish(o, sq, r[d])
        return carry

    lax.fori_loop(0, n // 2, first_half, 0)
    lax.fori_loop(n // 2, n, second_half, 0)
    for sq, d in chains:
        for h in range(GLA_HEADS):
            sfin_ref[sq, d, h] = sf_ref[sq, d, h]


def _gla_call(q, k, v, gl, l, wg, bg, ng, s0, nsq=2):
    nseq, length, _ = q.shape
    seq = lambda width: pl.BlockSpec((nsq, length, width), lambda b: (b, 0, 0))
    layer = lambda shape: pl.BlockSpec((None,) + shape, lambda b: (l,) + (0,) * len(shape))
    in_specs = [seq(GLA_KEY), seq(GLA_KEY), seq(GLA_WIDTH), seq(LANES),
                layer((2, LANES, GLA_KEY)), layer((2, 1, GLA_KEY)), layer((1, GLA_WIDTH))]
    args = [q, k, v, gl, wg, bg, ng]
    if s0 is not None:
        in_specs.append(pl.BlockSpec((nsq, None, 2, GLA_HEADS, GLA_DK, GLA_DV), lambda b: (b, l, 0, 0, 0, 0)))
        args.append(s0)
    return pl.pallas_call(
        functools.partial(_gla_kernel, length=length, has_s0=s0 is not None, nsq=nsq),
        grid=(nseq // nsq,),
        in_specs=in_specs,
        out_specs=(seq(GLA_WIDTH), pl.BlockSpec((nsq, 2, GLA_HEADS, GLA_DK, GLA_DV), lambda b: (b, 0, 0, 0, 0))),
        out_shape=(jax.ShapeDtypeStruct((nseq, length, GLA_WIDTH), F32),
                   jax.ShapeDtypeStruct((nseq, 2, GLA_HEADS, GLA_DK, GLA_DV), F32)),
        scratch_shapes=[pltpu.VMEM((nsq, 2, GLA_HEADS, GLA_DK, GLA_DV), F32),
                        pltpu.VMEM((nsq, 2, GLA_KEY, GLA_WIDTH), BF16)],
        compiler_params=pltpu.CompilerParams(dimension_semantics=("arbitrary",), vmem_limit_bytes=VMEM_LIMIT),
        name="gla",
    )(*args)


def _mix_kernel(*refs, has_pe, last):
    if has_pe:
        (x_ref, pe_ref, yf_ref, yb_ref, u_ref, o_ref, shift_ref, scale_ref, gate_ref, ng_ref, wg_ref, d_ref,
         wglu_ref, bglu_ref, wpa_ref, wpb_ref, wo_ref, fng_ref, out_ref) = refs
        x = x_ref[0] + pe_ref[...]
    else:
        (x_ref, yf_ref, yb_ref, u_ref, o_ref, shift_ref, scale_ref, gate_ref, ng_ref, wg_ref, d_ref,
         wglu_ref, bglu_ref, wpa_ref, wpb_ref, wo_ref, fng_ref, out_ref) = refs
        x = x_ref[0]
    h = _modulated_norm(x, ng_ref[...], 1.0 + scale_ref[...], shift_ref[...])
    gts = _bdot_t(h, wg_ref[...])
    gate_a = gts[:, :S5_WIDTH]
    gate_b = gts[:, S5_WIDTH:2 * S5_WIDTH]
    m_a = gts[:, 2 * S5_WIDTH:2 * S5_WIDTH + D_MODEL]
    m_b = gts[:, 2 * S5_WIDTH + D_MODEL:]
    y = _gelu_tanh(yf_ref[0] + yb_ref[0] + d_ref[...] * u_ref[0])
    y = y * _sigmoid(_bdot(y, wglu_ref[...]) + bglu_ref[...])
    y_a = y * _silu(gate_a)
    y_b = o_ref[0] * _silu(gate_b)
    merged = _sigmoid(m_a) * _bdot(y_a, wpa_ref[...]) + _sigmoid(m_b) * _bdot(y_b, wpb_ref[...])
    xn = x + gate_ref[...] * _bdot(merged, wo_ref[...])
    if last:
        ms = jnp.mean(xn * xn, axis=-1, keepdims=True)
        xn = xn * lax.rsqrt(ms + EPS) * fng_ref[...]
    out_ref[0] = xn


def _mix_call(x, pe, yf, yb, u, o, mod, l, cond_row, ng, wg, s5_d, wglu, bglu, wpa, wpb, wo, fng, tm, last):
    nseq, length, _ = x.shape
    row = lambda width: pl.BlockSpec((1, tm, width), lambda b, j: (b, j, 0))
    in_specs = [row(D_MODEL)]
    args = [x]
    if pe is not None:
        in_specs.append(pl.BlockSpec((tm, D_MODEL), lambda b, j: (j, 0)))
        args.append(pe)
    in_specs += [row(S5_WIDTH), row(S5_WIDTH), row(S5_WIDTH), row(GLA_WIDTH),
                 _mod_spec(l, cond_row, 0), _mod_spec(l, cond_row, 1), _mod_spec(l, cond_row, 2),
                 _layer_spec(l, (1, D_MODEL)), _layer_spec(l, (GATE_W, D_MODEL)), _layer_spec(l, (1, S5_WIDTH)),
                 _layer_spec(l, (S5_WIDTH, S5_WIDTH)), _layer_spec(l, (1, S5_WIDTH)),
                 _layer_spec(l, (S5_WIDTH, D_MODEL)), _layer_spec(l, (GLA_WIDTH, D_MODEL)),
                 _layer_spec(l, (D_MODEL, D_MODEL)), pl.BlockSpec((1, D_MODEL), lambda b, j: (0, 0))]
    args += [yf, yb, u, o, mod, mod, mod, ng, wg, s5_d, wglu, bglu, wpa, wpb, wo, fng]
    return pl.pallas_call(
        functools.partial(_mix_kernel, has_pe=pe is not None, last=last),
        grid=(nseq, length // tm),
        in_specs=in_specs,
        out_specs=row(D_MODEL),
        out_shape=jax.ShapeDtypeStruct(x.shape, F32),
        compiler_params=pltpu.CompilerParams(dimension_semantics=("arbitrary", "arbitrary"),
                                             vmem_limit_bytes=VMEM_LIMIT),
        name="mix",
    )(*args)


def _grid_pos_embed(length, dim):
    rows = length // GRID_W
    quarter = dim // 4
    freqs = jnp.exp(-math.log(10000.0) * jnp.arange(quarter, dtype=F32) / quarter)

    def sincos(pos):
        ang = pos.astype(F32)[:, None] * freqs[None, :]
        return jnp.concatenate([jnp.sin(ang), jnp.cos(ang)], axis=-1)

    er = sincos(jnp.arange(rows))
    ec = sincos(jnp.arange(GRID_W))
    pe = jnp.concatenate([jnp.broadcast_to(er[:, None, :], (rows, GRID_W, dim // 2)),
                          jnp.broadcast_to(ec[None, :, :], (rows, GRID_W, dim // 2))], axis=-1)
    return pe.reshape(rows * GRID_W, dim)


def kernel(x_prompt, x_sample, c, state_s5_re, state_s5_im, state_gla, c_ctx, norm_g, w_mod, b_mod, w_in, gla_wg_up,
           gla_bg, gla_norm_g, s5_lam_re, s5_lam_im, s5_log_dt, s5_b_re, s5_b_im, s5_c_re, s5_c_im, s5_d, w_glu,
           b_glu, w_pa, w_pb, w_o, final_norm_g):
    depth = w_in.shape[0]
    bp, lp, _ = x_prompt.shape
    bs, ls, _ = x_sample.shape

    cond8 = jnp.zeros((8, D_MODEL), F32).at[0].set(c_ctx).at[1:1 + bs].set(c)
    mod = _mod_call(cond8, w_mod, b_mod)

    a_re, a_im, w_bu, w_c = _s5_params(s5_lam_re, s5_lam_im, s5_log_dt, s5_b_re, s5_b_im, s5_c_re, s5_c_im)

    w_proj, w_gate = _split_w_in(w_in)
    wg_up = jnp.concatenate([gla_wg_up, jnp.zeros((depth, 2, LANES - GLA_RANK, GLA_KEY), F32)], axis=2).astype(BF16)
    w_glu_b, w_pa_b, w_pb_b, w_o_b = (w.astype(BF16) for w in (w_glu, w_pa, w_pb, w_o))

    vec = lambda a: a[:, None, :]
    mod = mod.reshape(depth, 8, 1, 3 * D_MODEL)
    ng, gng, bg = vec(norm_g), vec(gla_norm_g), gla_bg[:, :, None, :]
    common = (ng, w_gate, vec(s5_d), w_glu_b, vec(b_glu), w_pa_b, w_pb_b, w_o_b, final_norm_g[None])
    are_p, aim_p = jnp.repeat(a_re, bp, axis=1), jnp.repeat(a_im, bp, axis=1)
    are_s, aim_s = jnp.repeat(a_re, bs, axis=1), jnp.repeat(a_im, bs, axis=1)
    ctx_row = lambda b: 0
    cond_row = lambda b: b + 1

    pe = _grid_pos_embed(ls, D_MODEL)
    zero_h0 = jnp.zeros((2 * bp, 2 * S5_NC), F32)
    xp, xs = x_prompt, x_sample
    new_re, new_im, new_gla = [], [], []
    for l in range(depth):
        last = l == depth - 1

        u, q, k, v, gl = _proj_call(xp, None, mod, l, ctx_row, ng, w_proj, tm=lp, nsb=4)
        yf, yb, hfin = _s5_call(u, l, w_bu, w_c, are_p, aim_p, zero_h0, tc=8, tiles_per_pass=2)
        o, sfin = _gla_call(q, k, v, gl, l, wg_up, bg, gng, None)
        new_gla.append(sfin)
        xp = _mix_call(xp, None, yf, yb, u, o, mod, l, ctx_row, *common, tm=lp, last=last)
        re_l, im_l = _s5_cols_to_state(hfin.reshape(2, bp, 2 * S5_NC))
        new_re.append(jnp.swapaxes(re_l, 0, 1))
        new_im.append(jnp.swapaxes(im_l, 0, 1))

        pe_l = pe if l == 0 else None
        u, q, k, v, gl = _proj_call(xs, pe_l, mod, l, cond_row, ng, w_proj, tm=1024)
        h0 = _s5_state_to_cols(jnp.swapaxes(state_s5_re[:, l], 0, 1), jnp.swapaxes(state_s5_im[:, l], 0, 1))
        yf, yb, _ = _s5_call(u, l, w_bu, w_c, are_s, aim_s, h0.reshape(2 * bs, 2 * S5_NC), tc=32, tiles_per_pass=4)
        o, _ = _gla_call(q, k, v, gl, l, wg_up, bg, gng, state_gla)
        xs = _mix_call(xs, pe_l, yf, yb, u, o, mod, l, cond_row, *common, tm=512, last=last)

    return (xp, xs, jnp.stack(new_re, axis=1), jnp.stack(new_im, axis=1), jnp.stack(new_gla, axis=1))
```

```python
import functools
import math

import numpy as np
import jax
import jax.numpy as jnp
from jax import lax
from jax.experimental import pallas as pl
from jax.experimental.pallas import tpu as pltpu

F32 = jnp.float32
BF16 = jnp.bfloat16

D_MODEL = 1024
GRID_W = 64
S5_WIDTH = 512
S5_GROUPS = 32
S5_CH = 16
S5_STATE = 64
S5_NC = S5_GROUPS * S5_STATE
S5_SLABS = 4
GLA_HEADS = 4
GLA_DK = 64
GLA_DV = 128
GLA_KEY = GLA_HEADS * GLA_DK
GLA_WIDTH = GLA_HEADS * GLA_DV
GLA_RANK = 16
GLA_GATE_NORM = 16.0
GLA_CHUNK = 128
EPS = 1e-6
LANES = 128
PROJ_W = S5_WIDTH + 2 * GLA_KEY + GLA_WIDTH + LANES
GATE_W = 2 * S5_WIDTH + 2 * D_MODEL
VMEM_LIMIT = 52 * 1024 * 1024


def _bdot(a, b):
    return jnp.dot(a.astype(BF16), b.astype(BF16), preferred_element_type=F32)


def _bdot_t(a, bt):
    return lax.dot_general(a.astype(BF16), bt.astype(BF16), (((1,), (1,)), ((), ())), preferred_element_type=F32)


def _split2(x):
    hi = x.astype(BF16)
    lo = (x - hi.astype(F32)).astype(BF16)
    return hi, lo


def _sigmoid(x):
    return 1.0 / (1.0 + jnp.exp(-x))


def _silu(x):
    return x * _sigmoid(x)


def _gelu_tanh(x):
    c = math.sqrt(2.0 / math.pi)
    return 0.5 * x * (1.0 + jnp.tanh(c * (x + 0.044715 * (x * x * x))))


def _exact_zero_of(x):
    bits = lax.bitcast_convert_type(x, jnp.int32)
    return lax.shift_right_logical(lax.shift_right_logical(bits, 16), 16).astype(F32)


def _log_sigmoid(x):
    return jnp.minimum(x, 0.0) - jnp.log(1.0 + jnp.exp(-jnp.abs(x)))


def _modulated_norm(x, ng, scale1p, shift):
    ms = jnp.mean(x * x, axis=-1, keepdims=True)
    return (x * lax.rsqrt(ms + EPS) * ng) * scale1p + shift


def _mod_kernel(cond_ref, w_ref, b_ref, out_ref):
    c = cond_ref[...]
    s_hi, s_lo = _split2(_silu(c))
    w_hi, w_lo = _split2(w_ref[0])
    acc = jnp.dot(s_hi, w_hi, preferred_element_type=F32)
    acc += jnp.dot(s_lo, w_hi, preferred_element_type=F32)
    acc += jnp.dot(s_hi, w_lo, preferred_element_type=F32)
    out_ref[0] = acc + b_ref[0]


def _mod_call(cond8, w_mod, b_mod):
    depth = w_mod.shape[0]
    nb = 768
    return pl.pallas_call(
        _mod_kernel,
        grid=(depth, 3 * D_MODEL // nb),
        in_specs=[
            pl.BlockSpec((8, D_MODEL), lambda l, j: (0, 0)),
            pl.BlockSpec((1, D_MODEL, nb), lambda l, j: (l, 0, j)),
            pl.BlockSpec((1, 1, nb), lambda l, j: (l, 0, j)),
        ],
        out_specs=pl.BlockSpec((1, 8, nb), lambda l, j: (l, 0, j)),
        out_shape=jax.ShapeDtypeStruct((depth, 8, 3 * D_MODEL), F32),
        compiler_params=pltpu.CompilerParams(dimension_semantics=("arbitrary", "arbitrary")),
        name="adaln_mod",
    )(cond8, w_mod, b_mod.reshape(depth, 1, 3 * D_MODEL))


def _zoh(lre, lim, ldt):
    dt = jnp.exp(ldt)
    mag = jnp.exp(lre * dt)
    a_re = mag * jnp.cos(lim * dt)
    a_im = mag * jnp.sin(lim * dt)
    n_re = a_re - 1.0
    inv = 1.0 / (lre * lre + lim * lim)
    return a_re, a_im, (n_re * lre + a_im * lim) * inv, (a_im * lre - n_re * lim) * inv


def _zoh_kernel(lre_ref, lim_ref, ldt_ref, lre_x_ref, lim_x_ref, ldt_x_ref, bre_ref, bim_ref, cre_ref, cim_ref,
                are_ref, aim_ref, wbu_ref, wc_ref, *, depth):
    a_re, a_im, _, _ = _zoh(lre_ref[...], lim_ref[...], ldt_ref[...])
    are_ref[...] = a_re
    aim_ref[...] = a_im
    _, _, c_re, c_im = _zoh(lre_x_ref[...], lim_x_ref[...], ldt_x_ref[...])
    bre = bre_ref[...]
    bim = bim_ref[...]
    bbar = [c_re * bre - c_im * bim, c_re * bim + c_im * bre]
    cmat = [cre_ref[...], -cim_ref[...]]
    wbu_ref[...] = jnp.zeros_like(wbu_ref)
    wc_ref[...] = jnp.zeros_like(wc_ref)
    for l in range(depth):
        for s in range(S5_SLABS):
            for gl in range(8):
                g = 8 * s + gl
                for ri in range(2):
                    col = ri * 8 * S5_STATE + gl * S5_STATE
                    for d in range(2):
                        src = ((l * 2 + d) * S5_GROUPS + g) * S5_CH
                        row = (d * 8 + gl) * S5_CH
                        wbu_ref[l, s, row:row + S5_CH, col:col + S5_STATE] = bbar[ri][src:src + S5_CH, :].astype(BF16)
                    src = (l * S5_GROUPS + g) * S5_STATE
                    wc_ref[l, s, col:col + S5_STATE, gl * S5_CH:(gl + 1) * S5_CH] = (
                        cmat[ri][src:src + S5_STATE, :].astype(BF16))


def _s5_params(s5_lam_re, s5_lam_im, s5_log_dt, s5_b_re, s5_b_im, s5_c_re, s5_c_im):
    depth = s5_lam_re.shape[0]
    full = (depth, 2, S5_GROUPS, S5_CH, S5_STATE)
    expand = lambda a: jnp.broadcast_to(a, full).reshape(-1, S5_STATE)
    small = lambda a: a.reshape(depth * 2 * S5_GROUPS, -1)
    ct = lambda a: jnp.swapaxes(a, -1, -2).reshape(-1, S5_CH)
    a_re, a_im, w_bu, w_c = pl.pallas_call(
        functools.partial(_zoh_kernel, depth=depth),
        out_shape=(jax.ShapeDtypeStruct((depth * 2 * S5_GROUPS, S5_STATE), F32),
                   jax.ShapeDtypeStruct((depth * 2 * S5_GROUPS, S5_STATE), F32),
                   jax.ShapeDtypeStruct((depth, S5_SLABS, 2 * LANES, 2 * 8 * S5_STATE), BF16),
                   jax.ShapeDtypeStruct((depth, S5_SLABS, 2 * 8 * S5_STATE, LANES), BF16)),
        name="s5_zoh",
    )(small(s5_lam_re), small(s5_lam_im), small(s5_log_dt),
      expand(s5_lam_re[:, :, :, None, :]), expand(s5_lam_im[:, :, :, None, :]), expand(s5_log_dt[:, :, :, None, None]),
      expand(jnp.swapaxes(s5_b_re, -1, -2)[:, None]), expand(jnp.swapaxes(s5_b_im, -1, -2)[:, None]),
      ct(s5_c_re), ct(s5_c_im))
    return a_re.reshape(depth, 2, S5_NC), a_im.reshape(depth, 2, S5_NC), w_bu, w_c


def _s5_state_to_cols(re, im):
    lead = re.shape[:-2]
    st = jnp.stack([re.reshape(lead + (S5_SLABS, 512)), im.reshape(lead + (S5_SLABS, 512))], axis=-2)
    return st.reshape(lead + (2 * S5_NC,))


def _s5_cols_to_state(cols):
    lead = cols.shape[:-1]
    st = cols.reshape(lead + (S5_SLABS, 2, 512))
    re = st[..., 0, :].reshape(lead + (S5_GROUPS, S5_STATE))
    im = st[..., 1, :].reshape(lead + (S5_GROUPS, S5_STATE))
    return re, im


_IN_OFFS = [int(o) for o in np.cumsum([0, S5_WIDTH, S5_WIDTH, GLA_KEY, GLA_KEY, GLA_WIDTH, GLA_WIDTH, GLA_RANK,
                                       D_MODEL, D_MODEL])]


def _split_w_in_kernel(u_ref, qkv_ref, code_ref, ga_ref, gb_ref, m_ref, proj_ref, gate_ref):
    rows = 0
    for src in (u_ref, qkv_ref, code_ref):
        proj_ref[rows:rows + src.shape[1], :] = src[0].astype(BF16)
        rows += src.shape[1]
    proj_ref[rows:, :] = jnp.zeros((PROJ_W - rows, proj_ref.shape[1]), BF16)
    rows = 0
    for src in (ga_ref, gb_ref, m_ref):
        gate_ref[rows:rows + src.shape[1], :] = src[0].astype(BF16)
        rows += src.shape[1]


def _split_w_in(w_in):
    o = _IN_OFFS
    depth = w_in.shape[0]
    wt = jnp.swapaxes(w_in, 1, 2)
    tn = D_MODEL // 2
    rows = lambda lo, hi: pl.BlockSpec((pl.Element(1), pl.Element(hi - lo), pl.Element(tn)),
                                       lambda l, j: (l, lo, j * tn))
    out = lambda n: pl.BlockSpec((None, n, tn), lambda l, j: (l, 0, j))
    return pl.pallas_call(
        _split_w_in_kernel,
        grid=(depth, D_MODEL // tn),
        in_specs=[rows(o[0], o[1]), rows(o[2], o[5]), rows(o[6], o[7]), rows(o[1], o[2]), rows(o[5], o[6]),
                  rows(o[7], o[9])],
        out_specs=(out(PROJ_W), out(GATE_W)),
        out_shape=(jax.ShapeDtypeStruct((depth, PROJ_W, D_MODEL), BF16),
                   jax.ShapeDtypeStruct((depth, GATE_W, D_MODEL), BF16)),
        compiler_params=pltpu.CompilerParams(dimension_semantics=("arbitrary", "arbitrary"),
                                             vmem_limit_bytes=VMEM_LIMIT),
        name="split_w_in",
    )(wt, wt, wt, wt, wt, wt)


def _proj_kernel(*refs, has_pe):
    if has_pe:
        x_ref, pe_ref, shift_ref, scale_ref, ng_ref, w_ref, u_ref, q_ref, k_ref, v_ref, gl_ref = refs
        x = x_ref[...] + pe_ref[...]
    else:
        x_ref, shift_ref, scale_ref, ng_ref, w_ref, u_ref, q_ref, k_ref, v_ref, gl_ref = refs
        x = x_ref[...]
    nsb, tm, _ = x.shape
    x = x.reshape(nsb * tm, D_MODEL)
    h = _modulated_norm(x, ng_ref[...], 1.0 + scale_ref[...], shift_ref[...])
    p = _bdot_t(h, w_ref[...])
    for i in range(nsb):
        u_ref[:, i * S5_WIDTH:(i + 1) * S5_WIDTH] = p[i * tm:(i + 1) * tm, :S5_WIDTH]
    o = S5_WIDTH
    for ref, width in ((q_ref, GLA_KEY), (k_ref, GLA_KEY), (v_ref, GLA_WIDTH), (gl_ref, LANES)):
        ref[...] = p[:, o:o + width].reshape(nsb, tm, width).astype(ref.dtype)
        o += width


def _mod_spec(l, cond_row, part):
    return pl.BlockSpec((None, None, 1, D_MODEL), lambda b, j: (l, cond_row(b), 0, part))


def _layer_spec(l, shape):
    return pl.BlockSpec((None,) + shape, lambda b, j: (l,) + (0,) * len(shape))


def _proj_call(x, pe, mod, l, cond_row, ng, w, tm, nsb=1):
    nseq, length, _ = x.shape
    row = lambda width: pl.BlockSpec((nsb, tm, width), lambda b, j: (b, j, 0))
    in_specs = [row(D_MODEL)]
    args = [x]
    if pe is not None:
        in_specs.append(pl.BlockSpec((tm, D_MODEL), lambda b, j: (j, 0)))
        args.append(pe)
    in_specs += [_mod_spec(l, cond_row, 0), _mod_spec(l, cond_row, 1), _layer_spec(l, (1, D_MODEL)),
                 _layer_spec(l, (PROJ_W, D_MODEL))]
    args += [mod, mod, ng, w]
    sds = lambda width, dt: jax.ShapeDtypeStruct((nseq, length, width), dt)
    return pl.pallas_call(
        functools.partial(_proj_kernel, has_pe=pe is not None),
        grid=(nseq // nsb, length // tm),
        in_specs=in_specs,
        out_specs=(pl.BlockSpec((tm, nsb * S5_WIDTH), lambda b, j: (j, b)),
                   row(GLA_KEY), row(GLA_KEY), row(GLA_WIDTH), row(LANES)),
        out_shape=(jax.ShapeDtypeStruct((length, nseq * S5_WIDTH), F32),
                   sds(GLA_KEY, F32), sds(GLA_KEY, F32), sds(GLA_WIDTH, BF16), sds(LANES, F32)),
        compiler_params=pltpu.CompilerParams(dimension_semantics=("arbitrary", "arbitrary"),
                                             vmem_limit_bytes=VMEM_LIMIT),
        name="proj",
    )(*args)


def _s5_perms(nb, tc):
    r = nb * tc
    pf = np.zeros((2 * r, r), np.float32)
    pb = np.zeros((2 * r, r), np.float32)
    for t in range(tc):
        for b in range(nb):
            pf[t * 2 * nb + b, t * nb + b] = 1.0
            pb[t * 2 * nb + nb + b, (tc - 1 - t) * nb + b] = 1.0
    return pf, pb


def _s5_unpack(y2, nb, tc):
    rs = 2 * nb
    if nb % 8 == 0:
        yf = [y2[t * rs:t * rs + nb] for t in range(tc)]
        yb = [y2[(tc - 1 - t) * rs + nb:(tc - t) * rs] for t in range(tc)]
    else:
        assert nb == 4 and tc % 2 == 0
        low = lax.broadcasted_iota(jnp.int32, (8, y2.shape[1]), 0) < nb
        grp = lambda t: y2[t * rs:(t + 1) * rs]
        swap = lambda g: pltpu.roll(g, nb, axis=0)
        yf = [jnp.where(low, grp(2 * k), swap(grp(2 * k + 1))) for k in range(tc // 2)]
        yb = [jnp.where(low, swap(grp(tc - 1 - 2 * k)), grp(tc - 2 - 2 * k)) for k in range(tc // 2)]
    return jnp.concatenate(yf, axis=0), jnp.concatenate(yb, axis=0)


def _s5_kernel(*refs, nb, tc, tiles_per_pass):
    use_perm = nb % 8 != 0
    uf1_ref, ub1_ref, uf2_ref, ub2_ref, uf0_ref, ub0_ref = refs[:6]
    pf_ref, pb_ref = refs[6:8] if use_perm else (None, None)
    wbu_ref, wc_ref, are_ref, aim_ref, h0_ref, yf_ref, yb_ref, hfin_ref, bua_ref, bub_ref, hb_ref, st_ref = refs[-12:]
    i = pl.program_id(0)
    rows = nb * tc
    rs = 2 * nb

    def packed_inputs(uf_ref, ub_ref):
        uf = uf_ref[...].astype(BF16)
        ub = ub_ref[...].astype(BF16)
        if not use_perm:
            return uf, ub
        u2f = jnp.dot(pf_ref[...], uf, preferred_element_type=F32).astype(BF16)
        u2b = jnp.dot(pb_ref[...], ub, preferred_element_type=F32).astype(BF16)
        return [jnp.concatenate([u2f[:, s * LANES:(s + 1) * LANES], u2b[:, s * LANES:(s + 1) * LANES]], axis=1)
                for s in range(S5_SLABS)]

    def bu_slab(lhs, s, dst_ref):
        cols = slice(s * 1024, (s + 1) * 1024)
        if use_perm:
            dst_ref[:, cols] = jnp.dot(lhs[s], wbu_ref[s], preferred_element_type=F32)
            return
        uf, ub = lhs
        bf = jnp.dot(uf[:, s * LANES:(s + 1) * LANES], wbu_ref[s, :LANES, :], preferred_element_type=F32)
        bb = jnp.dot(ub[:, s * LANES:(s + 1) * LANES], wbu_ref[s, LANES:, :], preferred_element_type=F32)
        for t in range(tc):
            dst_ref[t * rs:t * rs + nb, cols] = bf[t * nb:(t + 1) * nb]
            dst_ref[t * rs + nb:(t + 1) * rs, cols] = bb[(tc - 1 - t) * nb:(tc - t) * nb]

    @pl.when(i == 0)
    def _():
        st_ref[...] = h0_ref[...]
        lhs0 = packed_inputs(uf0_ref, ub0_ref)
        for s in range(S5_SLABS):
            bu_slab(lhs0, s, bua_ref)

    half_step = functools.partial(_s5_half_step, wc_ref=wc_ref, are_ref=are_ref, aim_ref=aim_ref, hb_ref=hb_ref,
                                  st_ref=st_ref, bu_slab=bu_slab, rs=rs, tc=tc, tiles_per_pass=tiles_per_pass)
    y2, tail = half_step(bua_ref, bub_ref, packed_inputs(uf1_ref, ub1_ref))
    yf, yb = _s5_unpack(y2, nb, tc)
    yf_ref[:rows, :] = yf
    yb_ref[rows:, :] = yb
    y2, _ = half_step(bub_ref, bua_ref, packed_inputs(uf2_ref, ub2_ref), after=tail)
    yf, yb = _s5_unpack(y2, nb, tc)
    yf_ref[rows:, :] = yf
    yb_ref[:rows, :] = yb
    hfin_ref[...] = st_ref[...]


def _s5_half_step(cur_ref, nxt_ref, lhs_next, *, wc_ref, are_ref, aim_ref, hb_ref, st_ref, bu_slab,
                  rs, tc, tiles_per_pass, after=None):
    grp = max(1, 16 // rs)
    ys = []
    for s in range(S5_SLABS):
        bu_slab(lhs_next, s, nxt_ref)
        for c0 in range(4 * s, 4 * s + 4, tiles_per_pass):
            cts = list(range(c0, c0 + tiles_per_pass))
            cre = [(ct // 4) * 1024 + (ct % 4) * LANES for ct in cts]
            cim = [c + 512 for c in cre]
            a_re = [are_ref[:, ct * LANES:(ct + 1) * LANES] for ct in cts]
            a_im = [aim_ref[:, ct * LANES:(ct + 1) * LANES] for ct in cts]
            h_re = [st_ref[:, c:c + LANES] for c in cre]
            h_im = [st_ref[:, c:c + LANES] for c in cim]
            if after is not None and c0 == 0:
                h_re[0] = h_re[0] + _exact_zero_of(after)
            for t0 in range(0, tc, grp):
                out_re = [[] for _ in cts]
                out_im = [[] for _ in cts]
                for t in range(t0, t0 + grp):
                    r0 = t * rs
                    for j in range(len(cts)):
                        b_re = cur_ref[r0:r0 + rs, cre[j]:cre[j] + LANES]
                        b_im = cur_ref[r0:r0 + rs, cim[j]:cim[j] + LANES]
                        n_re = a_re[j] * h_re[j] - a_im[j] * h_im[j] + b_re
                        n_im = a_re[j] * h_im[j] + a_im[j] * h_re[j] + b_im
                        h_re[j], h_im[j] = n_re, n_im
                        out_re[j].append(n_re)
                        out_im[j].append(n_im)
                for j in range(len(cts)):
                    blk_re = out_re[j][0] if grp == 1 else jnp.concatenate(out_re[j], axis=0)
                    blk_im = out_im[j][0] if grp == 1 else jnp.concatenate(out_im[j], axis=0)
                    hb_ref[t0 * rs:(t0 + grp) * rs, cre[j]:cre[j] + LANES] = blk_re.astype(BF16)
                    hb_ref[t0 * rs:(t0 + grp) * rs, cim[j]:cim[j] + LANES] = blk_im.astype(BF16)
            for j in range(len(cts)):
                st_ref[:, cre[j]:cre[j] + LANES] = h_re[j]
                st_ref[:, cim[j]:cim[j] + LANES] = h_im[j]
        ys.append(jnp.dot(hb_ref[:, s * 1024:(s + 1) * 1024], wc_ref[s], preferred_element_type=F32))
    return jnp.concatenate(ys, axis=1), h_re[-1]


def _s5_call(u, nb, l, w_bu, w_c, are, aim, h0, tc, tiles_per_pass):
    n = u.shape[0] // (nb * tc)
    rows = nb * tc
    const = lambda shape: pl.BlockSpec(shape, lambda i: (0,) * len(shape))
    layer = lambda shape: pl.BlockSpec((None,) + shape, lambda i: (l,) + (0,) * len(shape))
    blk = lambda chunk: pl.BlockSpec((rows, S5_WIDTH), lambda i: (chunk(i), 0))
    out_f = pl.BlockSpec((2 * rows, S5_WIDTH), lambda i: (i, 0))
    out_b = pl.BlockSpec((2 * rows, S5_WIDTH), lambda i: (n // 2 - 1 - i, 0))
    second = lambda i: jnp.minimum(2 * i + 2, n - 1)
    in_specs = [blk(lambda i: 2 * i + 1), blk(lambda i: n - 2 - 2 * i), blk(second), blk(lambda i: n - 1 - second(i)),
                blk(lambda i: 0), blk(lambda i: n - 1)]
    args = [u] * 6
    if nb % 8 != 0:
        pf, pb = _s5_perms(nb, tc)
        in_specs += [const((2 * rows, rows)), const((2 * rows, rows))]
        args += [jnp.asarray(pf, BF16), jnp.asarray(pb, BF16)]
    in_specs += [layer((S5_SLABS, 2 * LANES, 1024)), layer((S5_SLABS, 1024, LANES)),
                 layer((2 * nb, S5_NC)), layer((2 * nb, S5_NC)), const((2 * nb, 2 * S5_NC))]
    args += [w_bu, w_c, are, aim, h0]
    return pl.pallas_call(
        functools.partial(_s5_kernel, nb=nb, tc=tc, tiles_per_pass=tiles_per_pass),
        grid=(n // 2,),
        in_specs=in_specs,
        out_specs=(out_f, out_b, const((2 * nb, 2 * S5_NC))),
        out_shape=(jax.ShapeDtypeStruct(u.shape, F32), jax.ShapeDtypeStruct(u.shape, F32),
                   jax.ShapeDtypeStruct((2 * nb, 2 * S5_NC), F32)),
        scratch_shapes=[pltpu.VMEM((2 * rows, 2 * S5_NC), F32), pltpu.VMEM((2 * rows, 2 * S5_NC), F32),
                        pltpu.VMEM((2 * rows, 2 * S5_NC), BF16), pltpu.VMEM((2 * nb, 2 * S5_NC), F32)],
        compiler_params=pltpu.CompilerParams(dimension_semantics=("arbitrary",), vmem_limit_bytes=VMEM_LIMIT),
        name="s5_scan",
    )(*args)


def _gla_kernel(*refs, length, has_s0, nsq):
    q_ref, k_ref, v_ref, gl_ref, wg_ref, bg_ref, ng_ref = refs[:7]
    s0_ref = refs[7] if has_s0 else None
    o_ref, sfin_ref, sf_ref, sb_ref = refs[-4:]
    c = GLA_CHUNK
    n = length // c
    ri = lax.broadcasted_iota(jnp.int32, (c, c), 0)
    ci = lax.broadcasted_iota(jnp.int32, (c, c), 1)
    lane = lax.broadcasted_iota(jnp.int32, (1, GLA_KEY), 1)
    head_mask = [(lane >= h * GLA_DK) & (lane < (h + 1) * GLA_DK) for h in range(GLA_HEADS)]
    ng = ng_ref[...]
    causal = [ci <= ri, ci >= ri]
    tri = [m.astype(BF16) for m in causal]
    end = (c - 1, 0)
    chains = [(sq, d) for sq in range(nsq) for d in (0, 1)]
    key_rows = lambda h: slice(h * GLA_DK, (h + 1) * GLA_DK)
    val_cols = lambda h: slice(h * GLA_DV, (h + 1) * GLA_DV)

    sb_ref[...] = jnp.zeros_like(sb_ref)
    for sq, d in chains:
        for h in range(GLA_HEADS):
            if has_s0:
                sf_ref[sq, d, h] = s0_ref[sq, d, h]
                sb_ref[sq, d, key_rows(h), val_cols(h)] = s0_ref[sq, d, h].astype(BF16)
            else:
                sf_ref[sq, d, h] = jnp.zeros((GLA_DK, GLA_DV), F32)

    def chunk_group(r):
        qc = [q_ref[sq, pl.ds(r[d], c), :] * (GLA_DK ** -0.5) for sq, d in chains]
        kc = [k_ref[sq, pl.ds(r[d], c), :] for sq, d in chains]
        vc = [v_ref[sq, pl.ds(r[d], c), :] for sq, d in chains]
        logits = [_bdot(gl_ref[sq, pl.ds(r[d], c), :], wg_ref[d]) + bg_ref[d] for sq, d in chains]
        g = [_split2(_log_sigmoid(x) * (1.0 / GLA_GATE_NORM)) for x in logits]
        b = [jnp.dot(tri[d], hi, preferred_element_type=F32) + jnp.dot(tri[d], lo, preferred_element_type=F32)
             for (sq, d), (hi, lo) in zip(chains, g)]
        bt = [x.T for x in b]
        kt = [x.T for x in kc]
        b_mid = [x[c // 2:c // 2 + 1, :] for x in b]
        bt_mid = [x[:, c // 2:c // 2 + 1] for x in bt]
        bt_end = [x[:, end[d]:end[d] + 1] for (sq, d), x in zip(chains, bt)]
        q_in = [(x * jnp.exp(y)).astype(BF16) for x, y in zip(qc, b)]
        q_mid = [(x * jnp.exp(y - m)).astype(BF16) for x, y, m in zip(qc, b, b_mid)]
        k_mid_t = [(x * jnp.exp(m - y)).astype(BF16) for x, y, m in zip(kt, bt, bt_mid)]
        k_end_t = [(x * jnp.exp(e - y)).astype(BF16) for x, y, e in zip(kt, bt, bt_end)]
        q_st = [jnp.concatenate([jnp.where(head_mask[h], x, jnp.zeros_like(x)) for h in range(GLA_HEADS)], axis=0)
                for x in q_mid]
        a = [jnp.dot(x, y, preferred_element_type=F32) for x, y in zip(q_st, k_mid_t)]
        o_inter = [jnp.dot(x, sb_ref[sq, d], preferred_element_type=F32) for (sq, d), x in zip(chains, q_in)]
        parts = [[] for _ in chains]
        for h in range(GLA_HEADS):
            for i, (sq, d) in enumerate(chains):
                a_h = jnp.where(causal[d], a[i][h * c:(h + 1) * c, :], 0.0).astype(BF16)
                parts[i].append(jnp.dot(a_h, vc[i][:, val_cols(h)], preferred_element_type=F32))
        for h in range(GLA_HEADS):
            for i, (sq, d) in enumerate(chains):
                upd = jnp.dot(k_end_t[i][key_rows(h), :], vc[i][:, val_cols(h)], preferred_element_type=F32)
                s_new = jnp.exp(bt_end[i][key_rows(h), :]) * sf_ref[sq, d, h] + upd
                sf_ref[sq, d, h] = s_new
                sb_ref[sq, d, key_rows(h), val_cols(h)] = s_new.astype(BF16)
        return [x + jnp.concatenate(p, axis=1) for x, p in zip(o_inter, parts)]

    def finish(o, sq, r0):
        o = o + o_ref[sq, pl.ds(r0, c), :]
        normed = []
        for h in range(GLA_HEADS):
            oh = o[:, val_cols(h)]
            ms = jnp.mean(oh * oh, axis=-1, keepdims=True)
            normed.append(oh * lax.rsqrt(ms + EPS))
        o_ref[sq, pl.ds(r0, c), :] = jnp.concatenate(normed, axis=1) * ng

    def first_half(step, carry):
        r = (pl.multiple_of(step * c, c), pl.multiple_of((n - 1 - step) * c, c))
        for (sq, d), o in zip(chains, chunk_group(r)):
            o_ref[sq, pl.ds(r[d], c), :] = o
        return carry

    def second_half(step, carry):
        r = (pl.multiple_of(step * c, c), pl.multiple_of((n - 1 - step) * c, c))
        for (sq, d), o in zip(chains, chunk_group(r)):
            finish(o, sq, r[d])
        return carry

    lax.fori_loop(0, n // 2, first_half, 0)
    lax.fori_loop(n // 2, n, second_half, 0)
    for sq, d in chains:
        for h in range(GLA_HEADS):
            sfin_ref[sq, d, h] = sf_ref[sq, d, h]


def _gla_call(q, k, v, gl, l, wg, bg, ng, s0, nsq=2):
    nseq, length, _ = q.shape
    seq = lambda width: pl.BlockSpec((nsq, length, width), lambda b: (b, 0, 0))
    layer = lambda shape: pl.BlockSpec((None,) + shape, lambda b: (l,) + (0,) * len(shape))
    in_specs = [seq(GLA_KEY), seq(GLA_KEY), seq(GLA_WIDTH), seq(LANES),
                layer((2, LANES, GLA_KEY)), layer((2, 1, GLA_KEY)), layer((1, GLA_WIDTH))]
    args = [q, k, v, gl, wg, bg, ng]
    if s0 is not None:
        in_specs.append(pl.BlockSpec((nsq, None, 2, GLA_HEADS, GLA_DK, GLA_DV), lambda b: (b, l, 0, 0, 0, 0)))
        args.append(s0)
    return pl.pallas_call(
        functools.partial(_gla_kernel, length=length, has_s0=s0 is not None, nsq=nsq),
        grid=(nseq // nsq,),
        in_specs=in_specs,
        out_specs=(seq(GLA_WIDTH), pl.BlockSpec((nsq, 2, GLA_HEADS, GLA_DK, GLA_DV), lambda b: (b, 0, 0, 0, 0))),
        out_shape=(jax.ShapeDtypeStruct((nseq, length, GLA_WIDTH), F32),
                   jax.ShapeDtypeStruct((nseq, 2, GLA_HEADS, GLA_DK, GLA_DV), F32)),
        scratch_shapes=[pltpu.VMEM((nsq, 2, GLA_HEADS, GLA_DK, GLA_DV), F32),
                        pltpu.VMEM((nsq, 2, GLA_KEY, GLA_WIDTH), BF16)],
        compiler_params=pltpu.CompilerParams(dimension_semantics=("arbitrary",), vmem_limit_bytes=VMEM_LIMIT),
        name="gla",
    )(*args)


def _mix_kernel(*refs, has_pe, last):
    if has_pe:
        (x_ref, pe_ref, yf_ref, yb_ref, u_ref, o_ref, shift_ref, scale_ref, gate_ref, ng_ref, wg_ref, d_ref,
         wglu_ref, bglu_ref, wpa_ref, wpb_ref, wo_ref, fng_ref, out_ref) = refs
        x = x_ref[0] + pe_ref[...]
    else:
        (x_ref, yf_ref, yb_ref, u_ref, o_ref, shift_ref, scale_ref, gate_ref, ng_ref, wg_ref, d_ref,
         wglu_ref, bglu_ref, wpa_ref, wpb_ref, wo_ref, fng_ref, out_ref) = refs
        x = x_ref[0]
    h = _modulated_norm(x, ng_ref[...], 1.0 + scale_ref[...], shift_ref[...])
    gts = _bdot_t(h, wg_ref[...])
    gate_a = gts[:, :S5_WIDTH]
    gate_b = gts[:, S5_WIDTH:2 * S5_WIDTH]
    m_a = gts[:, 2 * S5_WIDTH:2 * S5_WIDTH + D_MODEL]
    m_b = gts[:, 2 * S5_WIDTH + D_MODEL:]
    y = _gelu_tanh(yf_ref[...] + yb_ref[...] + d_ref[...] * u_ref[...])
    y = y * _sigmoid(_bdot(y, wglu_ref[...]) + bglu_ref[...])
    y_a = y * _silu(gate_a)
    y_b = o_ref[0] * _silu(gate_b)
    merged = _sigmoid(m_a) * _bdot(y_a, wpa_ref[...]) + _sigmoid(m_b) * _bdot(y_b, wpb_ref[...])
    xn = x + gate_ref[...] * _bdot(merged, wo_ref[...])
    if last:
        ms = jnp.mean(xn * xn, axis=-1, keepdims=True)
        xn = xn * lax.rsqrt(ms + EPS) * fng_ref[...]
    out_ref[0] = xn


def _mix_call(x, pe, yf, yb, u, o, mod, l, cond_row, ng, wg, s5_d, wglu, bglu, wpa, wpb, wo, fng, tm, last):
    nseq, length, _ = x.shape
    row = lambda width: pl.BlockSpec((1, tm, width), lambda b, j: (b, j, 0))
    in_specs = [row(D_MODEL)]
    args = [x]
    if pe is not None:
        in_specs.append(pl.BlockSpec((tm, D_MODEL), lambda b, j: (j, 0)))
        args.append(pe)
    tmaj = pl.BlockSpec((tm, S5_WIDTH), lambda b, j: (j, b))
    in_specs += [tmaj, tmaj, tmaj, row(GLA_WIDTH),
                 _mod_spec(l, cond_row, 0), _mod_spec(l, cond_row, 1), _mod_spec(l, cond_row, 2),
                 _layer_spec(l, (1, D_MODEL)), _layer_spec(l, (GATE_W, D_MODEL)), _layer_spec(l, (1, S5_WIDTH)),
                 _layer_spec(l, (S5_WIDTH, S5_WIDTH)), _layer_spec(l, (1, S5_WIDTH)),
                 _layer_spec(l, (S5_WIDTH, D_MODEL)), _layer_spec(l, (GLA_WIDTH, D_MODEL)),
                 _layer_spec(l, (D_MODEL, D_MODEL)), pl.BlockSpec((1, D_MODEL), lambda b, j: (0, 0))]
    args += [yf, yb, u, o, mod, mod, mod, ng, wg, s5_d, wglu, bglu, wpa, wpb, wo, fng]
    return pl.pallas_call(
        functools.partial(_mix_kernel, has_pe=pe is not None, last=last),
        grid=(nseq, length // tm),
        in_specs=in_specs,
        out_specs=row(D_MODEL),
        out_shape=jax.ShapeDtypeStruct(x.shape, F32),
        compiler_params=pltpu.CompilerParams(dimension_semantics=("arbitrary", "arbitrary"),
                                             vmem_limit_bytes=VMEM_LIMIT),
        name="mix",
    )(*args)


def _grid_pos_embed(length, dim):
    rows = length // GRID_W
    quarter = dim // 4
    freqs = jnp.exp(-math.log(10000.0) * jnp.arange(quarter, dtype=F32) / quarter)

    def sincos(pos):
        ang = pos.astype(F32)[:, None] * freqs[None, :]
        return jnp.concatenate([jnp.sin(ang), jnp.cos(ang)], axis=-1)

    er = sincos(jnp.arange(rows))
    ec = sincos(jnp.arange(GRID_W))
    pe = jnp.concatenate([jnp.broadcast_to(er[:, None, :], (rows, GRID_W, dim // 2)),
                          jnp.broadcast_to(ec[None, :, :], (rows, GRID_W, dim // 2))], axis=-1)
    return pe.reshape(rows * GRID_W, dim)


def kernel(x_prompt, x_sample, c, state_s5_re, state_s5_im, state_gla, c_ctx, norm_g, w_mod, b_mod, w_in, gla_wg_up,
           gla_bg, gla_norm_g, s5_lam_re, s5_lam_im, s5_log_dt, s5_b_re, s5_b_im, s5_c_re, s5_c_im, s5_d, w_glu,
           b_glu, w_pa, w_pb, w_o, final_norm_g):
    depth = w_in.shape[0]
    bp, lp, _ = x_prompt.shape
    bs, ls, _ = x_sample.shape

    cond8 = jnp.zeros((8, D_MODEL), F32).at[0].set(c_ctx).at[1:1 + bs].set(c)
    mod = _mod_call(cond8, w_mod, b_mod)

    a_re, a_im, w_bu, w_c = _s5_params(s5_lam_re, s5_lam_im, s5_log_dt, s5_b_re, s5_b_im, s5_c_re, s5_c_im)

    w_proj, w_gate = _split_w_in(w_in)
    wg_up = jnp.concatenate([gla_wg_up, jnp.zeros((depth, 2, LANES - GLA_RANK, GLA_KEY), F32)], axis=2).astype(BF16)
    w_glu_b, w_pa_b, w_pb_b, w_o_b = (w.astype(BF16) for w in (w_glu, w_pa, w_pb, w_o))

    vec = lambda a: a[:, None, :]
    mod = mod.reshape(depth, 8, 1, 3 * D_MODEL)
    ng, gng, bg = vec(norm_g), vec(gla_norm_g), gla_bg[:, :, None, :]
    common = (ng, w_gate, vec(s5_d), w_glu_b, vec(b_glu), w_pa_b, w_pb_b, w_o_b, final_norm_g[None])
    are_p, aim_p = jnp.repeat(a_re, bp, axis=1), jnp.repeat(a_im, bp, axis=1)
    are_s, aim_s = jnp.repeat(a_re, bs, axis=1), jnp.repeat(a_im, bs, axis=1)
    ctx_row = lambda b: 0
    cond_row = lambda b: b + 1

    pe = _grid_pos_embed(ls, D_MODEL)
    zero_h0 = jnp.zeros((2 * bp, 2 * S5_NC), F32)
    xp, xs = x_prompt, x_sample
    new_re, new_im, new_gla = [], [], []
    for l in range(depth):
        last = l == depth - 1

        u, q, k, v, gl = _proj_call(xp, None, mod, l, ctx_row, ng, w_proj, tm=lp, nsb=4)
        yf, yb, hfin = _s5_call(u.reshape(lp * bp, S5_WIDTH), bp, l, w_bu, w_c, are_p, aim_p, zero_h0, tc=8,
                                tiles_per_pass=2)
        yf, yb = yf.reshape(u.shape), yb.reshape(u.shape)
        o, sfin = _gla_call(q, k, v, gl, l, wg_up, bg, gng, None)
        new_gla.append(sfin)
        xp = _mix_call(xp, None, yf, yb, u, o, mod, l, ctx_row, *common, tm=lp, last=last)
        re_l, im_l = _s5_cols_to_state(hfin.reshape(2, bp, 2 * S5_NC))
        new_re.append(jnp.swapaxes(re_l, 0, 1))
        new_im.append(jnp.swapaxes(im_l, 0, 1))

        pe_l = pe if l == 0 else None
        u, q, k, v, gl = _proj_call(xs, pe_l, mod, l, cond_row, ng, w_proj, tm=1024)
        h0 = _s5_state_to_cols(jnp.swapaxes(state_s5_re[:, l], 0, 1), jnp.swapaxes(state_s5_im[:, l], 0, 1))
        yf, yb, _ = _s5_call(u.reshape(ls * bs, S5_WIDTH), bs, l, w_bu, w_c, are_s, aim_s,
                             h0.reshape(2 * bs, 2 * S5_NC), tc=32, tiles_per_pass=4)
        yf, yb = yf.reshape(u.shape), yb.reshape(u.shape)
        o, _ = _gla_call(q, k, v, gl, l, wg_up, bg, gng, state_gla)
        xs = _mix_call(xs, pe_l, yf, yb, u, o, mod, l, cond_row, *common, tm=512, last=last)

    return (xp, xs, jnp.stack(new_re, axis=1), jnp.stack(new_im, axis=1), jnp.stack(new_gla, axis=1))
```

```python
import functools
import math

import numpy as np
import jax
import jax.numpy as jnp
from jax import lax
from jax.experimental import pallas as pl
from jax.experimental.pallas import tpu as pltpu

F32 = jnp.float32
BF16 = jnp.bfloat16

D_MODEL = 1024
GRID_W = 64
S5_WIDTH = 512
S5_GROUPS = 32
S5_CH = 16
S5_STATE = 64
S5_NC = S5_GROUPS * S5_STATE
S5_SLABS = 4
GLA_HEADS = 4
GLA_DK = 64
GLA_DV = 128
GLA_KEY = GLA_HEADS * GLA_DK
GLA_WIDTH = GLA_HEADS * GLA_DV
GLA_RANK = 16
GLA_GATE_NORM = 16.0
GLA_CHUNK = 128
EPS = 1e-6
LANES = 128
PROJ_W = S5_WIDTH + 2 * GLA_KEY + GLA_WIDTH + LANES
GATE_W = 2 * S5_WIDTH + 2 * D_MODEL
VMEM_LIMIT = 52 * 1024 * 1024


def _bdot(a, b):
    return jnp.dot(a.astype(BF16), b.astype(BF16), preferred_element_type=F32)


def _bdot_t(a, bt):
    return lax.dot_general(a.astype(BF16), bt.astype(BF16), (((1,), (1,)), ((), ())), preferred_element_type=F32)


def _split2(x):
    hi = x.astype(BF16)
    lo = (x - hi.astype(F32)).astype(BF16)
    return hi, lo


def _sigmoid(x):
    return 1.0 / (1.0 + jnp.exp(-x))


def _silu(x):
    return x * _sigmoid(x)


def _gelu_tanh(x):
    c = math.sqrt(2.0 / math.pi)
    return 0.5 * x * (1.0 + jnp.tanh(c * (x + 0.044715 * (x * x * x))))


def _exact_zero_of(x):
    bits = lax.bitcast_convert_type(x, jnp.int32)
    return lax.shift_right_logical(lax.shift_right_logical(bits, 16), 16).astype(F32)


def _log_sigmoid(x):
    return jnp.minimum(x, 0.0) - jnp.log(1.0 + jnp.exp(-jnp.abs(x)))


def _modulated_norm(x, ng, scale1p, shift):
    ms = jnp.mean(x * x, axis=-1, keepdims=True)
    return (x * lax.rsqrt(ms + EPS) * ng) * scale1p + shift


def _mod_kernel(cond_ref, w_ref, b_ref, out_ref):
    c = cond_ref[...]
    s_hi, s_lo = _split2(_silu(c))
    w_hi, w_lo = _split2(w_ref[0])
    acc = jnp.dot(s_hi, w_hi, preferred_element_type=F32)
    acc += jnp.dot(s_lo, w_hi, preferred_element_type=F32)
    acc += jnp.dot(s_hi, w_lo, preferred_element_type=F32)
    out_ref[0] = acc + b_ref[0]


def _mod_call(cond8, w_mod, b_mod):
    depth = w_mod.shape[0]
    nb = 768
    return pl.pallas_call(
        _mod_kernel,
        grid=(depth, 3 * D_MODEL // nb),
        in_specs=[
            pl.BlockSpec((8, D_MODEL), lambda l, j: (0, 0)),
            pl.BlockSpec((1, D_MODEL, nb), lambda l, j: (l, 0, j)),
            pl.BlockSpec((1, 1, nb), lambda l, j: (l, 0, j)),
        ],
        out_specs=pl.BlockSpec((1, 8, nb), lambda l, j: (l, 0, j)),
        out_shape=jax.ShapeDtypeStruct((depth, 8, 3 * D_MODEL), F32),
        compiler_params=pltpu.CompilerParams(dimension_semantics=("arbitrary", "arbitrary")),
        name="adaln_mod",
    )(cond8, w_mod, b_mod.reshape(depth, 1, 3 * D_MODEL))


def _zoh(lre, lim, ldt):
    dt = jnp.exp(ldt)
    mag = jnp.exp(lre * dt)
    a_re = mag * jnp.cos(lim * dt)
    a_im = mag * jnp.sin(lim * dt)
    n_re = a_re - 1.0
    inv = 1.0 / (lre * lre + lim * lim)
    return a_re, a_im, (n_re * lre + a_im * lim) * inv, (a_im * lre - n_re * lim) * inv


def _zoh_kernel(lre_ref, lim_ref, ldt_ref, lre_x_ref, lim_x_ref, ldt_x_ref, bre_ref, bim_ref, cre_ref, cim_ref,
                are_ref, aim_ref, wbu_ref, wc_ref, *, depth):
    a_re, a_im, _, _ = _zoh(lre_ref[...], lim_ref[...], ldt_ref[...])
    are_ref[...] = a_re
    aim_ref[...] = a_im
    _, _, c_re, c_im = _zoh(lre_x_ref[...], lim_x_ref[...], ldt_x_ref[...])
    bre = bre_ref[...]
    bim = bim_ref[...]
    bbar = [c_re * bre - c_im * bim, c_re * bim + c_im * bre]
    cmat = [cre_ref[...], -cim_ref[...]]
    wbu_ref[...] = jnp.zeros_like(wbu_ref)
    wc_ref[...] = jnp.zeros_like(wc_ref)
    for l in range(depth):
        for s in range(S5_SLABS):
            for gl in range(8):
                g = 8 * s + gl
                for ri in range(2):
                    col = ri * 8 * S5_STATE + gl * S5_STATE
                    for d in range(2):
                        src = ((l * 2 + d) * S5_GROUPS + g) * S5_CH
                        row = (d * 8 + gl) * S5_CH
                        wbu_ref[l, s, row:row + S5_CH, col:col + S5_STATE] = bbar[ri][src:src + S5_CH, :].astype(BF16)
                    src = (l * S5_GROUPS + g) * S5_STATE
                    wc_ref[l, s, col:col + S5_STATE, gl * S5_CH:(gl + 1) * S5_CH] = (
                        cmat[ri][src:src + S5_STATE, :].astype(BF16))


def _s5_params(s5_lam_re, s5_lam_im, s5_log_dt, s5_b_re, s5_b_im, s5_c_re, s5_c_im):
    depth = s5_lam_re.shape[0]
    full = (depth, 2, S5_GROUPS, S5_CH, S5_STATE)
    expand = lambda a: jnp.broadcast_to(a, full).reshape(-1, S5_STATE)
    small = lambda a: a.reshape(depth * 2 * S5_GROUPS, -1)
    ct = lambda a: jnp.swapaxes(a, -1, -2).reshape(-1, S5_CH)
    a_re, a_im, w_bu, w_c = pl.pallas_call(
        functools.partial(_zoh_kernel, depth=depth),
        out_shape=(jax.ShapeDtypeStruct((depth * 2 * S5_GROUPS, S5_STATE), F32),
                   jax.ShapeDtypeStruct((depth * 2 * S5_GROUPS, S5_STATE), F32),
                   jax.ShapeDtypeStruct((depth, S5_SLABS, 2 * LANES, 2 * 8 * S5_STATE), BF16),
                   jax.ShapeDtypeStruct((depth, S5_SLABS, 2 * 8 * S5_STATE, LANES), BF16)),
        name="s5_zoh",
    )(small(s5_lam_re), small(s5_lam_im), small(s5_log_dt),
      expand(s5_lam_re[:, :, :, None, :]), expand(s5_lam_im[:, :, :, None, :]), expand(s5_log_dt[:, :, :, None, None]),
      expand(jnp.swapaxes(s5_b_re, -1, -2)[:, None]), expand(jnp.swapaxes(s5_b_im, -1, -2)[:, None]),
      ct(s5_c_re), ct(s5_c_im))
    return a_re.reshape(depth, 2, S5_NC), a_im.reshape(depth, 2, S5_NC), w_bu, w_c


def _s5_state_to_cols(re, im):
    lead = re.shape[:-2]
    st = jnp.stack([re.reshape(lead + (S5_SLABS, 512)), im.reshape(lead + (S5_SLABS, 512))], axis=-2)
    return st.reshape(lead + (2 * S5_NC,))


def _s5_cols_to_state(cols):
    lead = cols.shape[:-1]
    st = cols.reshape(lead + (S5_SLABS, 2, 512))
    re = st[..., 0, :].reshape(lead + (S5_GROUPS, S5_STATE))
    im = st[..., 1, :].reshape(lead + (S5_GROUPS, S5_STATE))
    return re, im


_IN_OFFS = [int(o) for o in np.cumsum([0, S5_WIDTH, S5_WIDTH, GLA_KEY, GLA_KEY, GLA_WIDTH, GLA_WIDTH, GLA_RANK,
                                       D_MODEL, D_MODEL])]


def _split_w_in_kernel(u_ref, qkv_ref, code_ref, ga_ref, gb_ref, m_ref, proj_ref, gate_ref):
    rows = 0
    for src in (u_ref, qkv_ref, code_ref):
        proj_ref[rows:rows + src.shape[1], :] = src[0].astype(BF16)
        rows += src.shape[1]
    proj_ref[rows:, :] = jnp.zeros((PROJ_W - rows, proj_ref.shape[1]), BF16)
    rows = 0
    for src in (ga_ref, gb_ref, m_ref):
        gate_ref[rows:rows + src.shape[1], :] = src[0].astype(BF16)
        rows += src.shape[1]


def _split_w_in(w_in):
    o = _IN_OFFS
    depth = w_in.shape[0]
    wt = jnp.swapaxes(w_in, 1, 2)
    tn = D_MODEL // 2
    rows = lambda lo, hi: pl.BlockSpec((pl.Element(1), pl.Element(hi - lo), pl.Element(tn)),
                                       lambda l, j: (l, lo, j * tn))
    out = lambda n: pl.BlockSpec((None, n, tn), lambda l, j: (l, 0, j))
    return pl.pallas_call(
        _split_w_in_kernel,
        grid=(depth, D_MODEL // tn),
        in_specs=[rows(o[0], o[1]), rows(o[2], o[5]), rows(o[6], o[7]), rows(o[1], o[2]), rows(o[5], o[6]),
                  rows(o[7], o[9])],
        out_specs=(out(PROJ_W), out(GATE_W)),
        out_shape=(jax.ShapeDtypeStruct((depth, PROJ_W, D_MODEL), BF16),
                   jax.ShapeDtypeStruct((depth, GATE_W, D_MODEL), BF16)),
        compiler_params=pltpu.CompilerParams(dimension_semantics=("arbitrary", "arbitrary"),
                                             vmem_limit_bytes=VMEM_LIMIT),
        name="split_w_in",
    )(wt, wt, wt, wt, wt, wt)


def _proj_kernel(*refs, has_pe):
    if has_pe:
        x_ref, pe_ref, shift_ref, scale_ref, ng_ref, w_ref, u_ref, q_ref, k_ref, v_ref, gl_ref = refs
        x = x_ref[...] + pe_ref[...]
    else:
        x_ref, shift_ref, scale_ref, ng_ref, w_ref, u_ref, q_ref, k_ref, v_ref, gl_ref = refs
        x = x_ref[...]
    nsb, tm, _ = x.shape
    x = x.reshape(nsb * tm, D_MODEL)
    h = _modulated_norm(x, ng_ref[...], 1.0 + scale_ref[...], shift_ref[...])
    p = _bdot_t(h, w_ref[...])
    o = 0
    for ref, width in ((u_ref, S5_WIDTH), (q_ref, GLA_KEY), (k_ref, GLA_KEY), (v_ref, GLA_WIDTH), (gl_ref, LANES)):
        ref[...] = p[:, o:o + width].reshape(nsb, tm, width).astype(ref.dtype)
        o += width


def _mod_spec(l, cond_row, part):
    return pl.BlockSpec((None, None, 1, D_MODEL), lambda b, j: (l, cond_row(b), 0, part))


def _layer_spec(l, shape):
    return pl.BlockSpec((None,) + shape, lambda b, j: (l,) + (0,) * len(shape))


def _proj_call(x, pe, mod, l, cond_row, ng, w, tm, nsb=1):
    nseq, length, _ = x.shape
    row = lambda width: pl.BlockSpec((nsb, tm, width), lambda b, j: (b, j, 0))
    in_specs = [row(D_MODEL)]
    args = [x]
    if pe is not None:
        in_specs.append(pl.BlockSpec((tm, D_MODEL), lambda b, j: (j, 0)))
        args.append(pe)
    in_specs += [_mod_spec(l, cond_row, 0), _mod_spec(l, cond_row, 1), _layer_spec(l, (1, D_MODEL)),
                 _layer_spec(l, (PROJ_W, D_MODEL))]
    args += [mod, mod, ng, w]
    sds = lambda width, dt: jax.ShapeDtypeStruct((nseq, length, width), dt)
    return pl.pallas_call(
        functools.partial(_proj_kernel, has_pe=pe is not None),
        grid=(nseq // nsb, length // tm),
        in_specs=in_specs,
        out_specs=(row(S5_WIDTH), row(GLA_KEY), row(GLA_KEY), row(GLA_WIDTH), row(LANES)),
        out_shape=(sds(S5_WIDTH, F32), sds(GLA_KEY, F32), sds(GLA_KEY, F32), sds(GLA_WIDTH, BF16), sds(LANES, F32)),
        compiler_params=pltpu.CompilerParams(dimension_semantics=("arbitrary", "arbitrary"),
                                             vmem_limit_bytes=VMEM_LIMIT),
        name="proj",
    )(*args)


def _s5_perms(nb, tc):
    r = nb * tc
    pf = np.zeros((2 * r, r), np.float32)
    pb = np.zeros((2 * r, r), np.float32)
    for t in range(tc):
        for b in range(nb):
            pf[t * 2 * nb + b, b * tc + t] = 1.0
            pb[t * 2 * nb + nb + b, b * tc + (tc - 1 - t)] = 1.0
    return pf, pb


def _s5_store_unpacked(y2, yf_ref, yb_ref, t_f, t_b, nb, tc):
    rs = 2 * nb
    for t in range(tc):
        yf_ref[:, t_f + t, :] = y2[t * rs:t * rs + nb]
        yb_ref[:, t_b + tc - 1 - t, :] = y2[t * rs + nb:(t + 1) * rs]


def _s5_kernel(*refs, nb, tc, tiles_per_pass):
    use_perm = nb % 8 != 0
    uf1_ref, ub1_ref, uf2_ref, ub2_ref, uf0_ref, ub0_ref = refs[:6]
    pf_ref, pb_ref = refs[6:8] if use_perm else (None, None)
    wbu_ref, wc_ref, are_ref, aim_ref, h0_ref, yf_ref, yb_ref, hfin_ref, bua_ref, bub_ref, hb_ref, st_ref = refs[-12:]
    i = pl.program_id(0)
    rows = nb * tc
    rs = 2 * nb

    def packed_inputs(uf_ref, ub_ref):
        if not use_perm:
            return tuple(jnp.concatenate([r[:, t, :] for t in range(tc)], axis=0).astype(BF16) for r in (uf_ref, ub_ref))
        uf = uf_ref[...].reshape(rows, S5_WIDTH).astype(BF16)
        ub = ub_ref[...].reshape(rows, S5_WIDTH).astype(BF16)
        u2f = jnp.dot(pf_ref[...], uf, preferred_element_type=F32).astype(BF16)
        u2b = jnp.dot(pb_ref[...], ub, preferred_element_type=F32).astype(BF16)
        return [jnp.concatenate([u2f[:, s * LANES:(s + 1) * LANES], u2b[:, s * LANES:(s + 1) * LANES]], axis=1)
                for s in range(S5_SLABS)]

    def bu_slab(lhs, s, dst_ref):
        cols = slice(s * 1024, (s + 1) * 1024)
        if use_perm:
            dst_ref[:, cols] = jnp.dot(lhs[s], wbu_ref[s], preferred_element_type=F32)
            return
        uf, ub = lhs
        bf = jnp.dot(uf[:, s * LANES:(s + 1) * LANES], wbu_ref[s, :LANES, :], preferred_element_type=F32)
        bb = jnp.dot(ub[:, s * LANES:(s + 1) * LANES], wbu_ref[s, LANES:, :], preferred_element_type=F32)
        for t in range(tc):
            dst_ref[t * rs:t * rs + nb, cols] = bf[t * nb:(t + 1) * nb]
            dst_ref[t * rs + nb:(t + 1) * rs, cols] = bb[(tc - 1 - t) * nb:(tc - t) * nb]

    @pl.when(i == 0)
    def _():
        st_ref[...] = h0_ref[...]
        lhs0 = packed_inputs(uf0_ref, ub0_ref)
        for s in range(S5_SLABS):
            bu_slab(lhs0, s, bua_ref)

    half_step = functools.partial(_s5_half_step, wc_ref=wc_ref, are_ref=are_ref, aim_ref=aim_ref, hb_ref=hb_ref,
                                  st_ref=st_ref, bu_slab=bu_slab, rs=rs, tc=tc, tiles_per_pass=tiles_per_pass)
    y2, tail = half_step(bua_ref, bub_ref, packed_inputs(uf1_ref, ub1_ref))
    _s5_store_unpacked(y2, yf_ref, yb_ref, 0, tc, nb, tc)
    y2, _ = half_step(bub_ref, bua_ref, packed_inputs(uf2_ref, ub2_ref), after=tail)
    _s5_store_unpacked(y2, yf_ref, yb_ref, tc, 0, nb, tc)
    hfin_ref[...] = st_ref[...]


def _s5_half_step(cur_ref, nxt_ref, lhs_next, *, wc_ref, are_ref, aim_ref, hb_ref, st_ref, bu_slab,
                  rs, tc, tiles_per_pass, after=None):
    grp = max(1, 16 // rs)
    ys = []
    for s in range(S5_SLABS):
        bu_slab(lhs_next, s, nxt_ref)
        for c0 in range(4 * s, 4 * s + 4, tiles_per_pass):
            cts = list(range(c0, c0 + tiles_per_pass))
            cre = [(ct // 4) * 1024 + (ct % 4) * LANES for ct in cts]
            cim = [c + 512 for c in cre]
            a_re = [are_ref[:, ct * LANES:(ct + 1) * LANES] for ct in cts]
            a_im = [aim_ref[:, ct * LANES:(ct + 1) * LANES] for ct in cts]
            h_re = [st_ref[:, c:c + LANES] for c in cre]
            h_im = [st_ref[:, c:c + LANES] for c in cim]
            if after is not None and c0 == 0:
                h_re[0] = h_re[0] + _exact_zero_of(after)
            for t0 in range(0, tc, grp):
                out_re = [[] for _ in cts]
                out_im = [[] for _ in cts]
                for t in range(t0, t0 + grp):
                    r0 = t * rs
                    for j in range(len(cts)):
                        b_re = cur_ref[r0:r0 + rs, cre[j]:cre[j] + LANES]
                        b_im = cur_ref[r0:r0 + rs, cim[j]:cim[j] + LANES]
                        n_re = a_re[j] * h_re[j] - a_im[j] * h_im[j] + b_re
                        n_im = a_re[j] * h_im[j] + a_im[j] * h_re[j] + b_im
                        h_re[j], h_im[j] = n_re, n_im
                        out_re[j].append(n_re)
                        out_im[j].append(n_im)
                for j in range(len(cts)):
                    blk_re = out_re[j][0] if grp == 1 else jnp.concatenate(out_re[j], axis=0)
                    blk_im = out_im[j][0] if grp == 1 else jnp.concatenate(out_im[j], axis=0)
                    hb_ref[t0 * rs:(t0 + grp) * rs, cre[j]:cre[j] + LANES] = blk_re.astype(BF16)
                    hb_ref[t0 * rs:(t0 + grp) * rs, cim[j]:cim[j] + LANES] = blk_im.astype(BF16)
            for j in range(len(cts)):
                st_ref[:, cre[j]:cre[j] + LANES] = h_re[j]
                st_ref[:, cim[j]:cim[j] + LANES] = h_im[j]
        ys.append(jnp.dot(hb_ref[:, s * 1024:(s + 1) * 1024], wc_ref[s], preferred_element_type=F32))
    return jnp.concatenate(ys, axis=1), h_re[-1]


def _s5_call(u, l, w_bu, w_c, are, aim, h0, tc, tiles_per_pass):
    nb, length, _ = u.shape
    n = length // tc
    rows = nb * tc
    const = lambda shape: pl.BlockSpec(shape, lambda i: (0,) * len(shape))
    layer = lambda shape: pl.BlockSpec((None,) + shape, lambda i: (l,) + (0,) * len(shape))
    blk = lambda chunk: pl.BlockSpec((nb, tc, S5_WIDTH), lambda i: (0, chunk(i), 0))
    out_f = pl.BlockSpec((nb, 2 * tc, S5_WIDTH), lambda i: (0, i, 0))
    out_b = pl.BlockSpec((nb, 2 * tc, S5_WIDTH), lambda i: (0, n // 2 - 1 - i, 0))
    second = lambda i: jnp.minimum(2 * i + 2, n - 1)
    in_specs = [blk(lambda i: 2 * i + 1), blk(lambda i: n - 2 - 2 * i), blk(second), blk(lambda i: n - 1 - second(i)),
                blk(lambda i: 0), blk(lambda i: n - 1)]
    args = [u] * 6
    if nb % 8 != 0:
        pf, pb = _s5_perms(nb, tc)
        in_specs += [const((2 * rows, rows)), const((2 * rows, rows))]
        args += [jnp.asarray(pf, BF16), jnp.asarray(pb, BF16)]
    in_specs += [layer((S5_SLABS, 2 * LANES, 1024)), layer((S5_SLABS, 1024, LANES)),
                 layer((2 * nb, S5_NC)), layer((2 * nb, S5_NC)), const((2 * nb, 2 * S5_NC))]
    args += [w_bu, w_c, are, aim, h0]
    return pl.pallas_call(
        functools.partial(_s5_kernel, nb=nb, tc=tc, tiles_per_pass=tiles_per_pass),
        grid=(n // 2,),
        in_specs=in_specs,
        out_specs=(out_f, out_b, const((2 * nb, 2 * S5_NC))),
        out_shape=(jax.ShapeDtypeStruct(u.shape, F32), jax.ShapeDtypeStruct(u.shape, F32),
                   jax.ShapeDtypeStruct((2 * nb, 2 * S5_NC), F32)),
        scratch_shapes=[pltpu.VMEM((2 * rows, 2 * S5_NC), F32), pltpu.VMEM((2 * rows, 2 * S5_NC), F32),
                        pltpu.VMEM((2 * rows, 2 * S5_NC), BF16), pltpu.VMEM((2 * nb, 2 * S5_NC), F32)],
        compiler_params=pltpu.CompilerParams(dimension_semantics=("arbitrary",), vmem_limit_bytes=VMEM_LIMIT),
        name="s5_scan",
    )(*args)


def _gla_kernel(*refs, length, has_s0, nsq):
    q_ref, k_ref, v_ref, gl_ref, wg_ref, bg_ref, ng_ref = refs[:7]
    s0_ref = refs[7] if has_s0 else None
    o_ref, sfin_ref, sf_ref, sb_ref = refs[-4:]
    c = GLA_CHUNK
    n = length // c
    ri = lax.broadcasted_iota(jnp.int32, (c, c), 0)
    ci = lax.broadcasted_iota(jnp.int32, (c, c), 1)
    lane = lax.broadcasted_iota(jnp.int32, (1, GLA_KEY), 1)
    head_mask = [(lane >= h * GLA_DK) & (lane < (h + 1) * GLA_DK) for h in range(GLA_HEADS)]
    ng = ng_ref[...]
    causal = [ci <= ri, ci >= ri]
    tri = [m.astype(BF16) for m in causal]
    end = (c - 1, 0)
    chains = [(sq, d) for sq in range(nsq) for d in (0, 1)]
    key_rows = lambda h: slice(h * GLA_DK, (h + 1) * GLA_DK)
    val_cols = lambda h: slice(h * GLA_DV, (h + 1) * GLA_DV)

    sb_ref[...] = jnp.zeros_like(sb_ref)
    for sq, d in chains:
        for h in range(GLA_HEADS):
            if has_s0:
                sf_ref[sq, d, h] = s0_ref[sq, d, h]
                sb_ref[sq, d, key_rows(h), val_cols(h)] = s0_ref[sq, d, h].astype(BF16)
            else:
                sf_ref[sq, d, h] = jnp.zeros((GLA_DK, GLA_DV), F32)

    def chunk_group(r):
        qc = [q_ref[sq, pl.ds(r[d], c), :] * (GLA_DK ** -0.5) for sq, d in chains]
        kc = [k_ref[sq, pl.ds(r[d], c), :] for sq, d in chains]
        vc = [v_ref[sq, pl.ds(r[d], c), :] for sq, d in chains]
        logits = [_bdot(gl_ref[sq, pl.ds(r[d], c), :], wg_ref[d]) + bg_ref[d] for sq, d in chains]
        g = [_split2(_log_sigmoid(x) * (1.0 / GLA_GATE_NORM)) for x in logits]
        b = [jnp.dot(tri[d], hi, preferred_element_type=F32) + jnp.dot(tri[d], lo, preferred_element_type=F32)
             for (sq, d), (hi, lo) in zip(chains, g)]
        bt = [x.T for x in b]
        kt = [x.T for x in kc]
        b_mid = [x[c // 2:c // 2 + 1, :] for x in b]
        bt_mid = [x[:, c // 2:c // 2 + 1] for x in bt]
        bt_end = [x[:, end[d]:end[d] + 1] for (sq, d), x in zip(chains, bt)]
        q_in = [(x * jnp.exp(y)).astype(BF16) for x, y in zip(qc, b)]
        q_mid = [(x * jnp.exp(y - m)).astype(BF16) for x, y, m in zip(qc, b, b_mid)]
        k_mid_t = [(x * jnp.exp(m - y)).astype(BF16) for x, y, m in zip(kt, bt, bt_mid)]
        k_end_t = [(x * jnp.exp(e - y)).astype(BF16) for x, y, e in zip(kt, bt, bt_end)]
        q_st = [jnp.concatenate([jnp.where(head_mask[h], x, jnp.zeros_like(x)) for h in range(GLA_HEADS)], axis=0)
                for x in q_mid]
        a = [jnp.dot(x, y, preferred_element_type=F32) for x, y in zip(q_st, k_mid_t)]
        o_inter = [jnp.dot(x, sb_ref[sq, d], preferred_element_type=F32) for (sq, d), x in zip(chains, q_in)]
        parts = [[] for _ in chains]
        for h in range(GLA_HEADS):
            for i, (sq, d) in enumerate(chains):
                a_h = jnp.where(causal[d], a[i][h * c:(h + 1) * c, :], 0.0).astype(BF16)
                parts[i].append(jnp.dot(a_h, vc[i][:, val_cols(h)], preferred_element_type=F32))
        for h in range(GLA_HEADS):
            for i, (sq, d) in enumerate(chains):
                upd = jnp.dot(k_end_t[i][key_rows(h), :], vc[i][:, val_cols(h)], preferred_element_type=F32)
                s_new = jnp.exp(bt_end[i][key_rows(h), :]) * sf_ref[sq, d, h] + upd
                sf_ref[sq, d, h] = s_new
                sb_ref[sq, d, key_rows(h), val_cols(h)] = s_new.astype(BF16)
        return [x + jnp.concatenate(p, axis=1) for x, p in zip(o_inter, parts)]

    def finish(o, sq, r0):
        o = o + o_ref[sq, pl.ds(r0, c), :]
        normed = []
        for h in range(GLA_HEADS):
            oh = o[:, val_cols(h)]
            ms = jnp.mean(oh * oh, axis=-1, keepdims=True)
            normed.append(oh * lax.rsqrt(ms + EPS))
        o_ref[sq, pl.ds(r0, c), :] = jnp.concatenate(normed, axis=1) * ng

    def first_half(step, carry):
        r = (pl.multiple_of(step * c, c), pl.multiple_of((n - 1 - step) * c, c))
        for (sq, d), o in zip(chains, chunk_group(r)):
            o_ref[sq, pl.ds(r[d], c), :] = o
        return carry

    def second_half(step, carry):
        r = (pl.multiple_of(step * c, c), pl.multiple_of((n - 1 - step) * c, c))
        for (sq, d), o in zip(chains, chunk_group(r)):
            finish(o, sq, r[d])
        return carry

    lax.fori_loop(0, n // 2, first_half, 0)
    lax.fori_loop(n // 2, n, second_half, 0)
    for sq, d in chains:
        for h in range(GLA_HEADS):
            sfin_ref[sq, d, h] = sf_ref[sq, d, h]


def _gla_call(q, k, v, gl, l, wg, bg, ng, s0, nsq=2):
    nseq, length, _ = q.shape
    seq = lambda width: pl.BlockSpec((nsq, length, width), lambda b: (b, 0, 0))
    layer = lambda shape: pl.BlockSpec((None,) + shape, lambda b: (l,) + (0,) * len(shape))
    in_specs = [seq(GLA_KEY), seq(GLA_KEY), seq(GLA_WIDTH), seq(LANES),
                layer((2, LANES, GLA_KEY)), layer((2, 1, GLA_KEY)), layer((1, GLA_WIDTH))]
    args = [q, k, v, gl, wg, bg, ng]
    if s0 is not None:
        in_specs.append(pl.BlockSpec((nsq, None, 2, GLA_HEADS, GLA_DK, GLA_DV), lambda b: (b, l, 0, 0, 0, 0)))
        args.append(s0)
    return pl.pallas_call(
        functools.partial(_gla_kernel, length=length, has_s0=s0 is not None, nsq=nsq),
        grid=(nseq // nsq,),
        in_specs=in_specs,
        out_specs=(seq(GLA_WIDTH), pl.BlockSpec((nsq, 2, GLA_HEADS, GLA_DK, GLA_DV), lambda b: (b, 0, 0, 0, 0))),
        out_shape=(jax.ShapeDtypeStruct((nseq, length, GLA_WIDTH), F32),
                   jax.ShapeDtypeStruct((nseq, 2, GLA_HEADS, GLA_DK, GLA_DV), F32)),
        scratch_shapes=[pltpu.VMEM((nsq, 2, GLA_HEADS, GLA_DK, GLA_DV), F32),
                        pltpu.VMEM((nsq, 2, GLA_KEY, GLA_WIDTH), BF16)],
        compiler_params=pltpu.CompilerParams(dimension_semantics=("arbitrary",), vmem_limit_bytes=VMEM_LIMIT),
        name="gla",
    )(*args)


def _mix_kernel(*refs, has_pe, last):
    if has_pe:
        (x_ref, pe_ref, yf_ref, yb_ref, u_ref, o_ref, shift_ref, scale_ref, gate_ref, ng_ref, wg_ref, d_ref,
         wglu_ref, bglu_ref, wpa_ref, wpb_ref, wo_ref, fng_ref, out_ref) = refs
        x = x_ref[0] + pe_ref[...]
    else:
        (x_ref, yf_ref, yb_ref, u_ref, o_ref, shift_ref, scale_ref, gate_ref, ng_ref, wg_ref, d_ref,
         wglu_ref, bglu_ref, wpa_ref, wpb_ref, wo_ref, fng_ref, out_ref) = refs
        x = x_ref[0]
    h = _modulated_norm(x, ng_ref[...], 1.0 + scale_ref[...], shift_ref[...])
    gts = _bdot_t(h, wg_ref[...])
    gate_a = gts[:, :S5_WIDTH]
    gate_b = gts[:, S5_WIDTH:2 * S5_WIDTH]
    m_a = gts[:, 2 * S5_WIDTH:2 * S5_WIDTH + D_MODEL]
    m_b = gts[:, 2 * S5_WIDTH + D_MODEL:]
    y = _gelu_tanh(yf_ref[0] + yb_ref[0] + d_ref[...] * u_ref[0])
    y = y * _sigmoid(_bdot(y, wglu_ref[...]) + bglu_ref[...])
    y_a = y * _silu(gate_a)
    y_b = o_ref[0] * _silu(gate_b)
    merged = _sigmoid(m_a) * _bdot(y_a, wpa_ref[...]) + _sigmoid(m_b) * _bdot(y_b, wpb_ref[...])
    xn = x + gate_ref[...] * _bdot(merged, wo_ref[...])
    if last:
        ms = jnp.mean(xn * xn, axis=-1, keepdims=True)
        xn = xn * lax.rsqrt(ms + EPS) * fng_ref[...]
    out_ref[0] = xn


def _mix_call(x, pe, yf, yb, u, o, mod, l, cond_row, ng, wg, s5_d, wglu, bglu, wpa, wpb, wo, fng, tm, last):
    nseq, length, _ = x.shape
    row = lambda width: pl.BlockSpec((1, tm, width), lambda b, j: (b, j, 0))
    in_specs = [row(D_MODEL)]
    args = [x]
    if pe is not None:
        in_specs.append(pl.BlockSpec((tm, D_MODEL), lambda b, j: (j, 0)))
        args.append(pe)
    in_specs += [row(S5_WIDTH), row(S5_WIDTH), row(S5_WIDTH), row(GLA_WIDTH),
                 _mod_spec(l, cond_row, 0), _mod_spec(l, cond_row, 1), _mod_spec(l, cond_row, 2),
                 _layer_spec(l, (1, D_MODEL)), _layer_spec(l, (GATE_W, D_MODEL)), _layer_spec(l, (1, S5_WIDTH)),
                 _layer_spec(l, (S5_WIDTH, S5_WIDTH)), _layer_spec(l, (1, S5_WIDTH)),
                 _layer_spec(l, (S5_WIDTH, D_MODEL)), _layer_spec(l, (GLA_WIDTH, D_MODEL)),
                 _layer_spec(l, (D_MODEL, D_MODEL)), pl.BlockSpec((1, D_MODEL), lambda b, j: (0, 0))]
    args += [yf, yb, u, o, mod, mod, mod, ng, wg, s5_d, wglu, bglu, wpa, wpb, wo, fng]
    return pl.pallas_call(
        functools.partial(_mix_kernel, has_pe=pe is not None, last=last),
        grid=(nseq, length // tm),
        in_specs=in_specs,
        out_specs=row(D_MODEL),
        out_shape=jax.ShapeDtypeStruct(x.shape, F32),
        compiler_params=pltpu.CompilerParams(dimension_semantics=("arbitrary", "arbitrary"),
                                             vmem_limit_bytes=VMEM_LIMIT),
        name="mix",
    )(*args)


def _grid_pos_embed(length, dim):
    rows = length // GRID_W
    quarter = dim // 4
    freqs = jnp.exp(-math.log(10000.0) * jnp.arange(quarter, dtype=F32) / quarter)

    def sincos(pos):
        ang = pos.astype(F32)[:, None] * freqs[None, :]
        return jnp.concatenate([jnp.sin(ang), jnp.cos(ang)], axis=-1)

    er = sincos(jnp.arange(rows))
    ec = sincos(jnp.arange(GRID_W))
    pe = jnp.concatenate([jnp.broadcast_to(er[:, None, :], (rows, GRID_W, dim // 2)),
                          jnp.broadcast_to(ec[None, :, :], (rows, GRID_W, dim // 2))], axis=-1)
    return pe.reshape(rows * GRID_W, dim)


def kernel(x_prompt, x_sample, c, state_s5_re, state_s5_im, state_gla, c_ctx, norm_g, w_mod, b_mod, w_in, gla_wg_up,
           gla_bg, gla_norm_g, s5_lam_re, s5_lam_im, s5_log_dt, s5_b_re, s5_b_im, s5_c_re, s5_c_im, s5_d, w_glu,
           b_glu, w_pa, w_pb, w_o, final_norm_g):
    depth = w_in.shape[0]
    bp, lp, _ = x_prompt.shape
    bs, ls, _ = x_sample.shape

    cond8 = jnp.zeros((8, D_MODEL), F32).at[0].set(c_ctx).at[1:1 + bs].set(c)
    mod = _mod_call(cond8, w_mod, b_mod)

    a_re, a_im, w_bu, w_c = _s5_params(s5_lam_re, s5_lam_im, s5_log_dt, s5_b_re, s5_b_im, s5_c_re, s5_c_im)

    w_proj, w_gate = _split_w_in(w_in)
    wg_up = jnp.concatenate([gla_wg_up, jnp.zeros((depth, 2, LANES - GLA_RANK, GLA_KEY), F32)], axis=2).astype(BF16)
    w_glu_b, w_pa_b, w_pb_b, w_o_b = (w.astype(BF16) for w in (w_glu, w_pa, w_pb, w_o))

    vec = lambda a: a[:, None, :]
    mod = mod.reshape(depth, 8, 1, 3 * D_MODEL)
    ng, gng, bg = vec(norm_g), vec(gla_norm_g), gla_bg[:, :, None, :]
    common = (ng, w_gate, vec(s5_d), w_glu_b, vec(b_glu), w_pa_b, w_pb_b, w_o_b, final_norm_g[None])
    are_p, aim_p = jnp.repeat(a_re, bp, axis=1), jnp.repeat(a_im, bp, axis=1)
    are_s, aim_s = jnp.repeat(a_re, bs, axis=1), jnp.repeat(a_im, bs, axis=1)
    ctx_row = lambda b: 0
    cond_row = lambda b: b + 1

    pe = _grid_pos_embed(ls, D_MODEL)
    zero_h0 = jnp.zeros((2 * bp, 2 * S5_NC), F32)
    xp, xs = x_prompt, x_sample
    new_re, new_im, new_gla = [], [], []
    for l in range(depth):
        last = l == depth - 1

        u, q, k, v, gl = _proj_call(xp, None, mod, l, ctx_row, ng, w_proj, tm=lp, nsb=4)
        yf, yb, hfin = _s5_call(u, l, w_bu, w_c, are_p, aim_p, zero_h0, tc=8, tiles_per_pass=2)
        o, sfin = _gla_call(q, k, v, gl, l, wg_up, bg, gng, None)
        new_gla.append(sfin)
        xp = _mix_call(xp, None, yf, yb, u, o, mod, l, ctx_row, *common, tm=lp, last=last)
        re_l, im_l = _s5_cols_to_state(hfin.reshape(2, bp, 2 * S5_NC))
        new_re.append(jnp.swapaxes(re_l, 0, 1))
        new_im.append(jnp.swapaxes(im_l, 0, 1))

        pe_l = pe if l == 0 else None
        u, q, k, v, gl = _proj_call(xs, pe_l, mod, l, cond_row, ng, w_proj, tm=1024)
        h0 = _s5_state_to_cols(jnp.swapaxes(state_s5_re[:, l], 0, 1), jnp.swapaxes(state_s5_im[:, l], 0, 1))
        yf, yb, _ = _s5_call(u, l, w_bu, w_c, are_s, aim_s, h0.reshape(2 * bs, 2 * S5_NC), tc=32, tiles_per_pass=4)
        o, _ = _gla_call(q, k, v, gl, l, wg_up, bg, gng, state_gla)
        xs = _mix_call(xs, pe_l, yf, yb, u, o, mod, l, cond_row, *common, tm=512, last=last)

    return (xp, xs, jnp.stack(new_re, axis=1), jnp.stack(new_im, axis=1), jnp.stack(new_gla, axis=1))
```

```python
import functools
import math

import numpy as np
import jax
import jax.numpy as jnp
from jax import lax
from jax.experimental import pallas as pl
from jax.experimental.pallas import tpu as pltpu

F32 = jnp.float32
BF16 = jnp.bfloat16

D_MODEL = 1024
GRID_W = 64
S5_WIDTH = 512
S5_GROUPS = 32
S5_CH = 16
S5_STATE = 64
S5_NC = S5_GROUPS * S5_STATE
S5_SLABS = 4
GLA_HEADS = 4
GLA_DK = 64
GLA_DV = 128
GLA_KEY = GLA_HEADS * GLA_DK
GLA_WIDTH = GLA_HEADS * GLA_DV
GLA_RANK = 16
GLA_GATE_NORM = 16.0
GLA_CHUNK = 128
EPS = 1e-6
LANES = 128
PROJ_W = S5_WIDTH + 2 * GLA_KEY + GLA_WIDTH + LANES
GATE_W = 2 * S5_WIDTH + 2 * D_MODEL
VMEM_LIMIT = 52 * 1024 * 1024


def _bdot(a, b):
    return jnp.dot(a.astype(BF16), b.astype(BF16), preferred_element_type=F32)


def _bdot_t(a, bt):
    return lax.dot_general(a.astype(BF16), bt.astype(BF16), (((1,), (1,)), ((), ())), preferred_element_type=F32)


def _split2(x):
    hi = x.astype(BF16)
    lo = (x - hi.astype(F32)).astype(BF16)
    return hi, lo


def _sigmoid(x):
    return 1.0 / (1.0 + jnp.exp(-x))


def _silu(x):
    return x * _sigmoid(x)


def _gelu_tanh(x):
    c = math.sqrt(2.0 / math.pi)
    return 0.5 * x * (1.0 + jnp.tanh(c * (x + 0.044715 * (x * x * x))))


def _exact_zero_of(x):
    bits = lax.bitcast_convert_type(x, jnp.int32)
    return lax.shift_right_logical(lax.shift_right_logical(bits, 16), 16).astype(F32)


def _log_sigmoid(x):
    return jnp.minimum(x, 0.0) - jnp.log(1.0 + jnp.exp(-jnp.abs(x)))


def _modulated_norm(x, ng, scale1p, shift):
    ms = jnp.mean(x * x, axis=-1, keepdims=True)
    return (x * lax.rsqrt(ms + EPS) * ng) * scale1p + shift


def _mod_kernel(cond_ref, w_ref, b_ref, out_ref):
    c = cond_ref[...]
    s_hi, s_lo = _split2(_silu(c))
    w_hi, w_lo = _split2(w_ref[0])
    acc = jnp.dot(s_hi, w_hi, preferred_element_type=F32)
    acc += jnp.dot(s_lo, w_hi, preferred_element_type=F32)
    acc += jnp.dot(s_hi, w_lo, preferred_element_type=F32)
    out_ref[0] = acc + b_ref[0]


def _mod_call(cond8, w_mod, b_mod):
    depth = w_mod.shape[0]
    nb = 768
    return pl.pallas_call(
        _mod_kernel,
        grid=(depth, 3 * D_MODEL // nb),
        in_specs=[
            pl.BlockSpec((8, D_MODEL), lambda l, j: (0, 0)),
            pl.BlockSpec((1, D_MODEL, nb), lambda l, j: (l, 0, j)),
            pl.BlockSpec((1, 1, nb), lambda l, j: (l, 0, j)),
        ],
        out_specs=pl.BlockSpec((1, 8, nb), lambda l, j: (l, 0, j)),
        out_shape=jax.ShapeDtypeStruct((depth, 8, 3 * D_MODEL), F32),
        compiler_params=pltpu.CompilerParams(dimension_semantics=("arbitrary", "arbitrary")),
        name="adaln_mod",
    )(cond8, w_mod, b_mod.reshape(depth, 1, 3 * D_MODEL))


def _zoh(lre, lim, ldt):
    dt = jnp.exp(ldt)
    mag = jnp.exp(lre * dt)
    a_re = mag * jnp.cos(lim * dt)
    a_im = mag * jnp.sin(lim * dt)
    n_re = a_re - 1.0
    inv = 1.0 / (lre * lre + lim * lim)
    return a_re, a_im, (n_re * lre + a_im * lim) * inv, (a_im * lre - n_re * lim) * inv


def _zoh_kernel(lre_ref, lim_ref, ldt_ref, lre_x_ref, lim_x_ref, ldt_x_ref, bre_ref, bim_ref, cre_ref, cim_ref,
                are_ref, aim_ref, wbu_ref, wc_ref, *, depth):
    a_re, a_im, _, _ = _zoh(lre_ref[...], lim_ref[...], ldt_ref[...])
    are_ref[...] = a_re
    aim_ref[...] = a_im
    _, _, c_re, c_im = _zoh(lre_x_ref[...], lim_x_ref[...], ldt_x_ref[...])
    bre = bre_ref[...]
    bim = bim_ref[...]
    bbar = [c_re * bre - c_im * bim, c_re * bim + c_im * bre]
    cmat = [cre_ref[...], -cim_ref[...]]
    wbu_ref[...] = jnp.zeros_like(wbu_ref)
    wc_ref[...] = jnp.zeros_like(wc_ref)
    for l in range(depth):
        for s in range(S5_SLABS):
            for gl in range(8):
                g = 8 * s + gl
                for ri in range(2):
                    col = ri * 8 * S5_STATE + gl * S5_STATE
                    for d in range(2):
                        src = ((l * 2 + d) * S5_GROUPS + g) * S5_CH
                        row = (d * 8 + gl) * S5_CH
                        wbu_ref[l, s, row:row + S5_CH, col:col + S5_STATE] = bbar[ri][src:src + S5_CH, :].astype(BF16)
                    src = (l * S5_GROUPS + g) * S5_STATE
                    wc_ref[l, s, col:col + S5_STATE, gl * S5_CH:(gl + 1) * S5_CH] = (
                        cmat[ri][src:src + S5_STATE, :].astype(BF16))


def _s5_params(s5_lam_re, s5_lam_im, s5_log_dt, s5_b_re, s5_b_im, s5_c_re, s5_c_im):
    depth = s5_lam_re.shape[0]
    full = (depth, 2, S5_GROUPS, S5_CH, S5_STATE)
    expand = lambda a: jnp.broadcast_to(a, full).reshape(-1, S5_STATE)
    small = lambda a: a.reshape(depth * 2 * S5_GROUPS, -1)
    ct = lambda a: jnp.swapaxes(a, -1, -2).reshape(-1, S5_CH)
    a_re, a_im, w_bu, w_c = pl.pallas_call(
        functools.partial(_zoh_kernel, depth=depth),
        out_shape=(jax.ShapeDtypeStruct((depth * 2 * S5_GROUPS, S5_STATE), F32),
                   jax.ShapeDtypeStruct((depth * 2 * S5_GROUPS, S5_STATE), F32),
                   jax.ShapeDtypeStruct((depth, S5_SLABS, 2 * LANES, 2 * 8 * S5_STATE), BF16),
                   jax.ShapeDtypeStruct((depth, S5_SLABS, 2 * 8 * S5_STATE, LANES), BF16)),
        name="s5_zoh",
    )(small(s5_lam_re), small(s5_lam_im), small(s5_log_dt),
      expand(s5_lam_re[:, :, :, None, :]), expand(s5_lam_im[:, :, :, None, :]), expand(s5_log_dt[:, :, :, None, None]),
      expand(jnp.swapaxes(s5_b_re, -1, -2)[:, None]), expand(jnp.swapaxes(s5_b_im, -1, -2)[:, None]),
      ct(s5_c_re), ct(s5_c_im))
    return a_re.reshape(depth, 2, S5_NC), a_im.reshape(depth, 2, S5_NC), w_bu, w_c


def _s5_state_to_cols(re, im):
    lead = re.shape[:-2]
    st = jnp.stack([re.reshape(lead + (S5_SLABS, 512)), im.reshape(lead + (S5_SLABS, 512))], axis=-2)
    return st.reshape(lead + (2 * S5_NC,))


def _s5_cols_to_state(cols):
    lead = cols.shape[:-1]
    st = cols.reshape(lead + (S5_SLABS, 2, 512))
    re = st[..., 0, :].reshape(lead + (S5_GROUPS, S5_STATE))
    im = st[..., 1, :].reshape(lead + (S5_GROUPS, S5_STATE))
    return re, im


_IN_OFFS = [int(o) for o in np.cumsum([0, S5_WIDTH, S5_WIDTH, GLA_KEY, GLA_KEY, GLA_WIDTH, GLA_WIDTH, GLA_RANK,
                                       D_MODEL, D_MODEL])]


def _split_w_in_kernel(u_ref, qkv_ref, code_ref, ga_ref, gb_ref, m_ref, proj_ref, gate_ref):
    rows = 0
    for src in (u_ref, qkv_ref, code_ref):
        proj_ref[rows:rows + src.shape[1], :] = src[0].astype(BF16)
        rows += src.shape[1]
    proj_ref[rows:, :] = jnp.zeros((PROJ_W - rows, proj_ref.shape[1]), BF16)
    rows = 0
    for src in (ga_ref, gb_ref, m_ref):
        gate_ref[rows:rows + src.shape[1], :] = src[0].astype(BF16)
        rows += src.shape[1]


def _split_w_in(w_in):
    o = _IN_OFFS
    depth = w_in.shape[0]
    wt = jnp.swapaxes(w_in, 1, 2)
    tn = D_MODEL // 2
    rows = lambda lo, hi: pl.BlockSpec((pl.Element(1), pl.Element(hi - lo), pl.Element(tn)),
                                       lambda l, j: (l, lo, j * tn))
    out = lambda n: pl.BlockSpec((None, n, tn), lambda l, j: (l, 0, j))
    return pl.pallas_call(
        _split_w_in_kernel,
        grid=(depth, D_MODEL // tn),
        in_specs=[rows(o[0], o[1]), rows(o[2], o[5]), rows(o[6], o[7]), rows(o[1], o[2]), rows(o[5], o[6]),
                  rows(o[7], o[9])],
        out_specs=(out(PROJ_W), out(GATE_W)),
        out_shape=(jax.ShapeDtypeStruct((depth, PROJ_W, D_MODEL), BF16),
                   jax.ShapeDtypeStruct((depth, GATE_W, D_MODEL), BF16)),
        compiler_params=pltpu.CompilerParams(dimension_semantics=("arbitrary", "arbitrary"),
                                             vmem_limit_bytes=VMEM_LIMIT),
        name="split_w_in",
    )(wt, wt, wt, wt, wt, wt)


def _proj_kernel(*refs, has_pe):
    if has_pe:
        x_ref, pe_ref, shift_ref, scale_ref, ng_ref, w_ref, u_ref, q_ref, k_ref, v_ref, gl_ref = refs
        x = x_ref[...] + pe_ref[...]
    else:
        x_ref, shift_ref, scale_ref, ng_ref, w_ref, u_ref, q_ref, k_ref, v_ref, gl_ref = refs
        x = x_ref[...]
    nsb, tm, _ = x.shape
    x = x.reshape(nsb * tm, D_MODEL)
    h = _modulated_norm(x, ng_ref[...], 1.0 + scale_ref[...], shift_ref[...])
    p = _bdot_t(h, w_ref[...])
    o = 0
    for ref, width in ((u_ref, S5_WIDTH), (q_ref, GLA_KEY), (k_ref, GLA_KEY), (v_ref, GLA_WIDTH), (gl_ref, LANES)):
        ref[...] = p[:, o:o + width].reshape(nsb, tm, width).astype(ref.dtype)
        o += width


def _mod_spec(l, cond_row, part):
    return pl.BlockSpec((None, None, 1, D_MODEL), lambda b, j: (l, cond_row(b), 0, part))


def _layer_spec(l, shape):
    return pl.BlockSpec((None,) + shape, lambda b, j: (l,) + (0,) * len(shape))


def _proj_call(x, pe, mod, l, cond_row, ng, w, tm, nsb=1):
    nseq, length, _ = x.shape
    row = lambda width: pl.BlockSpec((nsb, tm, width), lambda b, j: (b, j, 0))
    in_specs = [row(D_MODEL)]
    args = [x]
    if pe is not None:
        in_specs.append(pl.BlockSpec((tm, D_MODEL), lambda b, j: (j, 0)))
        args.append(pe)
    in_specs += [_mod_spec(l, cond_row, 0), _mod_spec(l, cond_row, 1), _layer_spec(l, (1, D_MODEL)),
                 _layer_spec(l, (PROJ_W, D_MODEL))]
    args += [mod, mod, ng, w]
    sds = lambda width, dt: jax.ShapeDtypeStruct((nseq, length, width), dt)
    return pl.pallas_call(
        functools.partial(_proj_kernel, has_pe=pe is not None),
        grid=(nseq // nsb, length // tm),
        in_specs=in_specs,
        out_specs=(row(S5_WIDTH), row(GLA_KEY), row(GLA_KEY), row(GLA_WIDTH), row(LANES)),
        out_shape=(sds(S5_WIDTH, F32), sds(GLA_KEY, F32), sds(GLA_KEY, F32), sds(GLA_WIDTH, BF16), sds(LANES, F32)),
        compiler_params=pltpu.CompilerParams(dimension_semantics=("arbitrary", "arbitrary"),
                                             vmem_limit_bytes=VMEM_LIMIT),
        name="proj",
    )(*args)


def _s5_perms(nb, tc):
    r = nb * tc
    pf = np.zeros((2 * r, r), np.float32)
    pb = np.zeros((2 * r, r), np.float32)
    for t in range(tc):
        for b in range(nb):
            pf[t * 2 * nb + b, b * tc + t] = 1.0
            pb[t * 2 * nb + nb + b, b * tc + (tc - 1 - t)] = 1.0
    return pf, pb


def _s5_store_unpacked(y2, yf_ref, yb_ref, t_f, t_b, nb, tc):
    rs = 2 * nb
    for t in range(tc):
        yf_ref[:, t_f + t, :] = y2[t * rs:t * rs + nb]
        yb_ref[:, t_b + tc - 1 - t, :] = y2[t * rs + nb:(t + 1) * rs]


def _s5_kernel(*refs, nb, tc, tiles_per_pass):
    use_perm = nb % 8 != 0
    uf1_ref, ub1_ref, uf2_ref, ub2_ref, uf0_ref, ub0_ref = refs[:6]
    pf_ref, pb_ref = refs[6:8] if use_perm else (None, None)
    wbu_ref, wc_ref, are_ref, aim_ref, h0_ref, yf_ref, yb_ref, hfin_ref, bua_ref, bub_ref, hb_ref, st_ref = refs[-12:]
    i = pl.program_id(0)
    rows = nb * tc
    rs = 2 * nb

    def packed_inputs(uf_ref, ub_ref):
        if not use_perm:
            return tuple(jnp.concatenate([r[:, t, :] for t in range(tc)], axis=0).astype(BF16) for r in (uf_ref, ub_ref))
        uf = uf_ref[...].reshape(rows, S5_WIDTH).astype(BF16)
        ub = ub_ref[...].reshape(rows, S5_WIDTH).astype(BF16)
        u2f = jnp.dot(pf_ref[...], uf, preferred_element_type=F32).astype(BF16)
        u2b = jnp.dot(pb_ref[...], ub, preferred_element_type=F32).astype(BF16)
        return [jnp.concatenate([u2f[:, s * LANES:(s + 1) * LANES], u2b[:, s * LANES:(s + 1) * LANES]], axis=1)
                for s in range(S5_SLABS)]

    def bu_slab(lhs, s, dst_ref):
        cols = slice(s * 1024, (s + 1) * 1024)
        if use_perm:
            dst_ref[:, cols] = jnp.dot(lhs[s], wbu_ref[s], preferred_element_type=F32)
            return
        uf, ub = lhs
        bf = jnp.dot(uf[:, s * LANES:(s + 1) * LANES], wbu_ref[s, :LANES, :], preferred_element_type=F32)
        bb = jnp.dot(ub[:, s * LANES:(s + 1) * LANES], wbu_ref[s, LANES:, :], preferred_element_type=F32)
        for t in range(tc):
            dst_ref[t * rs:t * rs + nb, cols] = bf[t * nb:(t + 1) * nb]
            dst_ref[t * rs + nb:(t + 1) * rs, cols] = bb[(tc - 1 - t) * nb:(tc - t) * nb]

    @pl.when(i == 0)
    def _():
        st_ref[...] = h0_ref[...]
        lhs0 = packed_inputs(uf0_ref, ub0_ref)
        for s in range(S5_SLABS):
            bu_slab(lhs0, s, bua_ref)

    half_step = functools.partial(_s5_half_step, wc_ref=wc_ref, are_ref=are_ref, aim_ref=aim_ref, hb_ref=hb_ref,
                                  st_ref=st_ref, bu_slab=bu_slab, rs=rs, tc=tc, tiles_per_pass=tiles_per_pass)
    y2, tail = half_step(bua_ref, bub_ref, packed_inputs(uf1_ref, ub1_ref))
    _s5_store_unpacked(y2, yf_ref, yb_ref, 0, tc, nb, tc)
    y2, _ = half_step(bub_ref, bua_ref, packed_inputs(uf2_ref, ub2_ref), after=tail)
    _s5_store_unpacked(y2, yf_ref, yb_ref, tc, 0, nb, tc)
    hfin_ref[...] = st_ref[...]


def _s5_half_step(cur_ref, nxt_ref, lhs_next, *, wc_ref, are_ref, aim_ref, hb_ref, st_ref, bu_slab,
                  rs, tc, tiles_per_pass, after=None):
    grp = max(1, 16 // rs)
    ys = []
    for s in range(S5_SLABS):
        bu_slab(lhs_next, s, nxt_ref)
        for c0 in range(4 * s, 4 * s + 4, tiles_per_pass):
            cts = list(range(c0, c0 + tiles_per_pass))
            cre = [(ct // 4) * 1024 + (ct % 4) * LANES for ct in cts]
            cim = [c + 512 for c in cre]
            a_re = [are_ref[:, ct * LANES:(ct + 1) * LANES] for ct in cts]
            a_im = [aim_ref[:, ct * LANES:(ct + 1) * LANES] for ct in cts]
            h_re = [st_ref[:, c:c + LANES] for c in cre]
            h_im = [st_ref[:, c:c + LANES] for c in cim]
            if after is not None and c0 == 0:
                h_re[0] = h_re[0] + _exact_zero_of(after)
            for t0 in range(0, tc, grp):
                out_re = [[] for _ in cts]
                out_im = [[] for _ in cts]
                for t in range(t0, t0 + grp):
                    r0 = t * rs
                    for j in range(len(cts)):
                        b_re = cur_ref[r0:r0 + rs, cre[j]:cre[j] + LANES]
                        b_im = cur_ref[r0:r0 + rs, cim[j]:cim[j] + LANES]
                        n_re = a_re[j] * h_re[j] - a_im[j] * h_im[j] + b_re
                        n_im = a_re[j] * h_im[j] + a_im[j] * h_re[j] + b_im
                        h_re[j], h_im[j] = n_re, n_im
                        out_re[j].append(n_re)
                        out_im[j].append(n_im)
                for j in range(len(cts)):
                    blk_re = out_re[j][0] if grp == 1 else jnp.concatenate(out_re[j], axis=0)
                    blk_im = out_im[j][0] if grp == 1 else jnp.concatenate(out_im[j], axis=0)
                    hb_ref[t0 * rs:(t0 + grp) * rs, cre[j]:cre[j] + LANES] = blk_re.astype(BF16)
                    hb_ref[t0 * rs:(t0 + grp) * rs, cim[j]:cim[j] + LANES] = blk_im.astype(BF16)
            for j in range(len(cts)):
                st_ref[:, cre[j]:cre[j] + LANES] = h_re[j]
                st_ref[:, cim[j]:cim[j] + LANES] = h_im[j]
        ys.append(jnp.dot(hb_ref[:, s * 1024:(s + 1) * 1024], wc_ref[s], preferred_element_type=F32))
    return jnp.concatenate(ys, axis=1), h_re[-1]


def _s5_call(u, l, w_bu, w_c, are, aim, h0, tc, tiles_per_pass):
    nb, length, _ = u.shape
    n = length // tc
    rows = nb * tc
    const = lambda shape: pl.BlockSpec(shape, lambda i: (0,) * len(shape))
    layer = lambda shape: pl.BlockSpec((None,) + shape, lambda i: (l,) + (0,) * len(shape))
    blk = lambda chunk: pl.BlockSpec((nb, tc, S5_WIDTH), lambda i: (0, chunk(i), 0))
    out_f = pl.BlockSpec((nb, 2 * tc, S5_WIDTH), lambda i: (0, i, 0))
    out_b = pl.BlockSpec((nb, 2 * tc, S5_WIDTH), lambda i: (0, n // 2 - 1 - i, 0))
    second = lambda i: jnp.minimum(2 * i + 2, n - 1)
    in_specs = [blk(lambda i: 2 * i + 1), blk(lambda i: n - 2 - 2 * i), blk(second), blk(lambda i: n - 1 - second(i)),
                blk(lambda i: 0), blk(lambda i: n - 1)]
    args = [u] * 6
    if nb % 8 != 0:
        pf, pb = _s5_perms(nb, tc)
        in_specs += [const((2 * rows, rows)), const((2 * rows, rows))]
        args += [jnp.asarray(pf, BF16), jnp.asarray(pb, BF16)]
    in_specs += [layer((S5_SLABS, 2 * LANES, 1024)), layer((S5_SLABS, 1024, LANES)),
                 layer((2 * nb, S5_NC)), layer((2 * nb, S5_NC)), const((2 * nb, 2 * S5_NC))]
    args += [w_bu, w_c, are, aim, h0]
    return pl.pallas_call(
        functools.partial(_s5_kernel, nb=nb, tc=tc, tiles_per_pass=tiles_per_pass),
        grid=(n // 2,),
        in_specs=in_specs,
        out_specs=(out_f, out_b, const((2 * nb, 2 * S5_NC))),
        out_shape=(jax.ShapeDtypeStruct(u.shape, F32), jax.ShapeDtypeStruct(u.shape, F32),
                   jax.ShapeDtypeStruct((2 * nb, 2 * S5_NC), F32)),
        scratch_shapes=[pltpu.VMEM((2 * rows, 2 * S5_NC), F32), pltpu.VMEM((2 * rows, 2 * S5_NC), F32),
                        pltpu.VMEM((2 * rows, 2 * S5_NC), BF16), pltpu.VMEM((2 * nb, 2 * S5_NC), F32)],
        compiler_params=pltpu.CompilerParams(dimension_semantics=("arbitrary",), vmem_limit_bytes=VMEM_LIMIT),
        name="s5_scan",
    )(*args)


def _gla_kernel(*refs, length, has_s0, nsq):
    q_ref, k_ref, v_ref, gl_ref, wg_ref, bg_ref, ng_ref = refs[:7]
    s0_ref = refs[7] if has_s0 else None
    o_ref, sfin_ref, sf_ref, sb_ref = refs[-4:]
    c = GLA_CHUNK
    n = length // c
    ri = lax.broadcasted_iota(jnp.int32, (c, c), 0)
    ci = lax.broadcasted_iota(jnp.int32, (c, c), 1)
    lane = lax.broadcasted_iota(jnp.int32, (1, GLA_KEY), 1)
    head_mask = [(lane >= h * GLA_DK) & (lane < (h + 1) * GLA_DK) for h in range(GLA_HEADS)]
    ng = ng_ref[...]
    causal = [ci <= ri, ci >= ri]
    tri = [m.astype(BF16) for m in causal]
    end = (c - 1, 0)
    chains = [(sq, d) for sq in range(nsq) for d in (0, 1)]
    key_rows = lambda h: slice(h * GLA_DK, (h + 1) * GLA_DK)
    val_cols = lambda h: slice(h * GLA_DV, (h + 1) * GLA_DV)

    sb_ref[...] = jnp.zeros_like(sb_ref)
    for sq, d in chains:
        for h in range(GLA_HEADS):
            if has_s0:
                sf_ref[sq, d, h] = s0_ref[sq, d, h]
                sb_ref[sq, d, key_rows(h), val_cols(h)] = s0_ref[sq, d, h].astype(BF16)
            else:
                sf_ref[sq, d, h] = jnp.zeros((GLA_DK, GLA_DV), F32)

    def chunk_group(r):
        qc = [q_ref[sq, pl.ds(r[d], c), :] * (GLA_DK ** -0.5) for sq, d in chains]
        kc = [k_ref[sq, pl.ds(r[d], c), :] for sq, d in chains]
        vc = [v_ref[sq, pl.ds(r[d], c), :] for sq, d in chains]
        logits = [_bdot(gl_ref[sq, pl.ds(r[d], c), :], wg_ref[d]) + bg_ref[d] for sq, d in chains]
        g = [_split2(_log_sigmoid(x) * (1.0 / GLA_GATE_NORM)) for x in logits]
        b = [jnp.dot(tri[d], hi, preferred_element_type=F32) + jnp.dot(tri[d], lo, preferred_element_type=F32)
             for (sq, d), (hi, lo) in zip(chains, g)]
        bt = [x.T for x in b]
        kt = [x.T for x in kc]
        b_mid = [x[c // 2:c // 2 + 1, :] for x in b]
        bt_mid = [x[:, c // 2:c // 2 + 1] for x in bt]
        bt_end = [x[:, end[d]:end[d] + 1] for (sq, d), x in zip(chains, bt)]
        q_in = [(x * jnp.exp(y)).astype(BF16) for x, y in zip(qc, b)]
        q_mid = [(x * jnp.exp(y - m)).astype(BF16) for x, y, m in zip(qc, b, b_mid)]
        k_mid_t = [(x * jnp.exp(m - y)).astype(BF16) for x, y, m in zip(kt, bt, bt_mid)]
        k_end_t = [(x * jnp.exp(e - y)).astype(BF16) for x, y, e in zip(kt, bt, bt_end)]
        q_st = [jnp.concatenate([jnp.where(head_mask[h], x, jnp.zeros_like(x)) for h in range(GLA_HEADS)], axis=0)
                for x in q_mid]
        a = [jnp.dot(x, y, preferred_element_type=F32) for x, y in zip(q_st, k_mid_t)]
        o_inter = [jnp.dot(x, sb_ref[sq, d], preferred_element_type=F32) for (sq, d), x in zip(chains, q_in)]
        parts = [[] for _ in chains]
        for h in range(GLA_HEADS):
            for i, (sq, d) in enumerate(chains):
                a_h = jnp.where(causal[d], a[i][h * c:(h + 1) * c, :], 0.0).astype(BF16)
                parts[i].append(jnp.dot(a_h, vc[i][:, val_cols(h)], preferred_element_type=F32))
        for h in range(GLA_HEADS):
            for i, (sq, d) in enumerate(chains):
                upd = jnp.dot(k_end_t[i][key_rows(h), :], vc[i][:, val_cols(h)], preferred_element_type=F32)
                s_new = jnp.exp(bt_end[i][key_rows(h), :]) * sf_ref[sq, d, h] + upd
                sf_ref[sq, d, h] = s_new
                sb_ref[sq, d, key_rows(h), val_cols(h)] = s_new.astype(BF16)
        return [x + jnp.concatenate(p, axis=1) for x, p in zip(o_inter, parts)]

    def finish(o, sq, r0):
        o = o + o_ref[sq, pl.ds(r0, c), :]
        normed = []
        for h in range(GLA_HEADS):
            oh = o[:, val_cols(h)]
            ms = jnp.mean(oh * oh, axis=-1, keepdims=True)
            normed.append(oh * lax.rsqrt(ms + EPS))
        o_ref[sq, pl.ds(r0, c), :] = jnp.concatenate(normed, axis=1) * ng

    def first_half(step, carry):
        r = (pl.multiple_of(step * c, c), pl.multiple_of((n - 1 - step) * c, c))
        for (sq, d), o in zip(chains, chunk_group(r)):
            o_ref[sq, pl.ds(r[d], c), :] = o
        return carry

    def second_half(step, carry):
        r = (pl.multiple_of(step * c, c), pl.multiple_of((n - 1 - step) * c, c))
        for (sq, d), o in zip(chains, chunk_group(r)):
            finish(o, sq, r[d])
        return carry

    lax.fori_loop(0, n // 2, first_half, 0)
    lax.fori_loop(n // 2, n, second_half, 0)
    for sq, d in chains:
        for h in range(GLA_HEADS):
            sfin_ref[sq, d, h] = sf_ref[sq, d, h]


def _gla_call(q, k, v, gl, l, wg, bg, ng, s0, nsq=2):
    nseq, length, _ = q.shape
    seq = lambda width: pl.BlockSpec((nsq, length, width), lambda b: (b, 0, 0))
    layer = lambda shape: pl.BlockSpec((None,) + shape, lambda b: (l,) + (0,) * len(shape))
    in_specs = [seq(GLA_KEY), seq(GLA_KEY), seq(GLA_WIDTH), seq(LANES),
                layer((2, LANES, GLA_KEY)), layer((2, 1, GLA_KEY)), layer((1, GLA_WIDTH))]
    args = [q, k, v, gl, wg, bg, ng]
    if s0 is not None:
        in_specs.append(pl.BlockSpec((nsq, None, 2, GLA_HEADS, GLA_DK, GLA_DV), lambda b: (b, l, 0, 0, 0, 0)))
        args.append(s0)
    return pl.pallas_call(
        functools.partial(_gla_kernel, length=length, has_s0=s0 is not None, nsq=nsq),
        grid=(nseq // nsq,),
        in_specs=in_specs,
        out_specs=(seq(GLA_WIDTH), pl.BlockSpec((nsq, 2, GLA_HEADS, GLA_DK, GLA_DV), lambda b: (b, 0, 0, 0, 0))),
        out_shape=(jax.ShapeDtypeStruct((nseq, length, GLA_WIDTH), F32),
                   jax.ShapeDtypeStruct((nseq, 2, GLA_HEADS, GLA_DK, GLA_DV), F32)),
        scratch_shapes=[pltpu.VMEM((nsq, 2, GLA_HEADS, GLA_DK, GLA_DV), F32),
                        pltpu.VMEM((nsq, 2, GLA_KEY, GLA_WIDTH), BF16)],
        compiler_params=pltpu.CompilerParams(dimension_semantics=("arbitrary",), vmem_limit_bytes=VMEM_LIMIT),
        name="gla",
    )(*args)


def _mix_kernel(*refs, has_pe, last, row_groups):
    if has_pe:
        (x_ref, pe_ref, yf_ref, yb_ref, u_ref, o_ref, shift_ref, scale_ref, gate_ref, ng_ref, wg_ref, d_ref,
         wglu_ref, bglu_ref, wpa_ref, wpb_ref, wo_ref, fng_ref, out_ref) = refs
    else:
        (x_ref, yf_ref, yb_ref, u_ref, o_ref, shift_ref, scale_ref, gate_ref, ng_ref, wg_ref, d_ref,
         wglu_ref, bglu_ref, wpa_ref, wpb_ref, wo_ref, fng_ref, out_ref) = refs
    tm = x_ref.shape[1]
    grp = [slice(i * tm // row_groups, (i + 1) * tm // row_groups) for i in range(row_groups)]
    x = [x_ref[0, r, :] + pe_ref[r, :] if has_pe else x_ref[0, r, :] for r in grp]
    h = [_modulated_norm(v, ng_ref[...], 1.0 + scale_ref[...], shift_ref[...]).astype(BF16) for v in x]
    gts = [_bdot_t(v, wg_ref[...]) for v in h]
    y = [_gelu_tanh(yf_ref[0, r, :] + yb_ref[0, r, :] + d_ref[...] * u_ref[0, r, :]) for r in grp]
    glu = [_bdot(v, wglu_ref[...]) for v in y]
    y = [v * _sigmoid(g + bglu_ref[...]) for v, g in zip(y, glu)]
    y_a = [v * _silu(g[:, :S5_WIDTH]) for v, g in zip(y, gts)]
    y_b = [o_ref[0, r, :] * _silu(g[:, S5_WIDTH:2 * S5_WIDTH]) for r, g in zip(grp, gts)]
    p_a = [_bdot(v, wpa_ref[...]) for v in y_a]
    p_b = [_bdot(v, wpb_ref[...]) for v in y_b]
    merged = [_sigmoid(g[:, 2 * S5_WIDTH:2 * S5_WIDTH + D_MODEL]) * a + _sigmoid(g[:, 2 * S5_WIDTH + D_MODEL:]) * b
              for g, a, b in zip(gts, p_a, p_b)]
    xn = [v + gate_ref[...] * _bdot(m, wo_ref[...]) for v, m in zip(x, merged)]
    for r, v in zip(grp, xn):
        if last:
            ms = jnp.mean(v * v, axis=-1, keepdims=True)
            v = v * lax.rsqrt(ms + EPS) * fng_ref[...]
        out_ref[0, r, :] = v


def _mix_call(x, pe, yf, yb, u, o, mod, l, cond_row, ng, wg, s5_d, wglu, bglu, wpa, wpb, wo, fng, tm, last):
    nseq, length, _ = x.shape
    row = lambda width: pl.BlockSpec((1, tm, width), lambda b, j: (b, j, 0))
    in_specs = [row(D_MODEL)]
    args = [x]
    if pe is not None:
        in_specs.append(pl.BlockSpec((tm, D_MODEL), lambda b, j: (j, 0)))
        args.append(pe)
    in_specs += [row(S5_WIDTH), row(S5_WIDTH), row(S5_WIDTH), row(GLA_WIDTH),
                 _mod_spec(l, cond_row, 0), _mod_spec(l, cond_row, 1), _mod_spec(l, cond_row, 2),
                 _layer_spec(l, (1, D_MODEL)), _layer_spec(l, (GATE_W, D_MODEL)), _layer_spec(l, (1, S5_WIDTH)),
                 _layer_spec(l, (S5_WIDTH, S5_WIDTH)), _layer_spec(l, (1, S5_WIDTH)),
                 _layer_spec(l, (S5_WIDTH, D_MODEL)), _layer_spec(l, (GLA_WIDTH, D_MODEL)),
                 _layer_spec(l, (D_MODEL, D_MODEL)), pl.BlockSpec((1, D_MODEL), lambda b, j: (0, 0))]
    args += [yf, yb, u, o, mod, mod, mod, ng, wg, s5_d, wglu, bglu, wpa, wpb, wo, fng]
    return pl.pallas_call(
        functools.partial(_mix_kernel, has_pe=pe is not None, last=last, row_groups=max(1, tm // 256)),
        grid=(nseq, length // tm),
        in_specs=in_specs,
        out_specs=row(D_MODEL),
        out_shape=jax.ShapeDtypeStruct(x.shape, F32),
        compiler_params=pltpu.CompilerParams(dimension_semantics=("arbitrary", "arbitrary"),
                                             vmem_limit_bytes=VMEM_LIMIT),
        name="mix",
    )(*args)


def _grid_pos_embed(length, dim):
    rows = length // GRID_W
    quarter = dim // 4
    freqs = jnp.exp(-math.log(10000.0) * jnp.arange(quarter, dtype=F32) / quarter)

    def sincos(pos):
        ang = pos.astype(F32)[:, None] * freqs[None, :]
        return jnp.concatenate([jnp.sin(ang), jnp.cos(ang)], axis=-1)

    er = sincos(jnp.arange(rows))
    ec = sincos(jnp.arange(GRID_W))
    pe = jnp.concatenate([jnp.broadcast_to(er[:, None, :], (rows, GRID_W, dim // 2)),
                          jnp.broadcast_to(ec[None, :, :], (rows, GRID_W, dim // 2))], axis=-1)
    return pe.reshape(rows * GRID_W, dim)


def kernel(x_prompt, x_sample, c, state_s5_re, state_s5_im, state_gla, c_ctx, norm_g, w_mod, b_mod, w_in, gla_wg_up,
           gla_bg, gla_norm_g, s5_lam_re, s5_lam_im, s5_log_dt, s5_b_re, s5_b_im, s5_c_re, s5_c_im, s5_d, w_glu,
           b_glu, w_pa, w_pb, w_o, final_norm_g):
    depth = w_in.shape[0]
    bp, lp, _ = x_prompt.shape
    bs, ls, _ = x_sample.shape

    cond8 = jnp.zeros((8, D_MODEL), F32).at[0].set(c_ctx).at[1:1 + bs].set(c)
    mod = _mod_call(cond8, w_mod, b_mod)

    a_re, a_im, w_bu, w_c = _s5_params(s5_lam_re, s5_lam_im, s5_log_dt, s5_b_re, s5_b_im, s5_c_re, s5_c_im)

    w_proj, w_gate = _split_w_in(w_in)
    wg_up = jnp.concatenate([gla_wg_up, jnp.zeros((depth, 2, LANES - GLA_RANK, GLA_KEY), F32)], axis=2).astype(BF16)
    w_glu_b, w_pa_b, w_pb_b, w_o_b = (w.astype(BF16) for w in (w_glu, w_pa, w_pb, w_o))

    vec = lambda a: a[:, None, :]
    mod = mod.reshape(depth, 8, 1, 3 * D_MODEL)
    ng, gng, bg = vec(norm_g), vec(gla_norm_g), gla_bg[:, :, None, :]
    common = (ng, w_gate, vec(s5_d), w_glu_b, vec(b_glu), w_pa_b, w_pb_b, w_o_b, final_norm_g[None])
    are_p, aim_p = jnp.repeat(a_re, bp, axis=1), jnp.repeat(a_im, bp, axis=1)
    are_s, aim_s = jnp.repeat(a_re, bs, axis=1), jnp.repeat(a_im, bs, axis=1)
    ctx_row = lambda b: 0
    cond_row = lambda b: b + 1

    pe = _grid_pos_embed(ls, D_MODEL)
    zero_h0 = jnp.zeros((2 * bp, 2 * S5_NC), F32)
    xp, xs = x_prompt, x_sample
    new_re, new_im, new_gla = [], [], []
    for l in range(depth):
        last = l == depth - 1

        u, q, k, v, gl = _proj_call(xp, None, mod, l, ctx_row, ng, w_proj, tm=lp, nsb=4)
        yf, yb, hfin = _s5_call(u, l, w_bu, w_c, are_p, aim_p, zero_h0, tc=8, tiles_per_pass=2)
        o, sfin = _gla_call(q, k, v, gl, l, wg_up, bg, gng, None)
        new_gla.append(sfin)
        xp = _mix_call(xp, None, yf, yb, u, o, mod, l, ctx_row, *common, tm=lp, last=last)
        re_l, im_l = _s5_cols_to_state(hfin.reshape(2, bp, 2 * S5_NC))
        new_re.append(jnp.swapaxes(re_l, 0, 1))
        new_im.append(jnp.swapaxes(im_l, 0, 1))

        pe_l = pe if l == 0 else None
        u, q, k, v, gl = _proj_call(xs, pe_l, mod, l, cond_row, ng, w_proj, tm=1024)
        h0 = _s5_state_to_cols(jnp.swapaxes(state_s5_re[:, l], 0, 1), jnp.swapaxes(state_s5_im[:, l], 0, 1))
        yf, yb, _ = _s5_call(u, l, w_bu, w_c, are_s, aim_s, h0.reshape(2 * bs, 2 * S5_NC), tc=32, tiles_per_pass=4)
        o, _ = _gla_call(q, k, v, gl, l, wg_up, bg, gng, state_gla)
        xs = _mix_call(xs, pe_l, yf, yb, u, o, mod, l, cond_row, *common, tm=512, last=last)

    return (xp, xs, jnp.stack(new_re, axis=1), jnp.stack(new_im, axis=1), jnp.stack(new_gla, axis=1))
```

```python
import functools
import math

import numpy as np
import jax
import jax.numpy as jnp
from jax import lax
from jax.experimental import pallas as pl
from jax.experimental.pallas import tpu as pltpu

F32 = jnp.float32
BF16 = jnp.bfloat16

D_MODEL = 1024
GRID_W = 64
S5_WIDTH = 512
S5_GROUPS = 32
S5_CH = 16
S5_STATE = 64
S5_NC = S5_GROUPS * S5_STATE
S5_SLABS = 4
GLA_HEADS = 4
GLA_DK = 64
GLA_DV = 128
GLA_KEY = GLA_HEADS * GLA_DK
GLA_WIDTH = GLA_HEADS * GLA_DV
GLA_RANK = 16
GLA_GATE_NORM = 16.0
GLA_CHUNK = 128
EPS = 1e-6
LANES = 128
PROJ_W = S5_WIDTH + 2 * GLA_KEY + GLA_WIDTH + LANES
GATE_W = 2 * S5_WIDTH + 2 * D_MODEL
VMEM_LIMIT = 52 * 1024 * 1024


def _bdot(a, b):
    return jnp.dot(a.astype(BF16), b.astype(BF16), preferred_element_type=F32)


def _bdot_t(a, bt):
    return lax.dot_general(a.astype(BF16), bt.astype(BF16), (((1,), (1,)), ((), ())), preferred_element_type=F32)


def _split2(x):
    hi = x.astype(BF16)
    lo = (x - hi.astype(F32)).astype(BF16)
    return hi, lo


def _sigmoid(x):
    return 1.0 / (1.0 + jnp.exp(-x))


def _silu(x):
    return x * _sigmoid(x)


def _gelu_tanh(x):
    c = math.sqrt(2.0 / math.pi)
    return 0.5 * x * (1.0 + jnp.tanh(c * (x + 0.044715 * (x * x * x))))


def _exact_zero_of(x):
    bits = lax.bitcast_convert_type(x, jnp.int32)
    return lax.shift_right_logical(lax.shift_right_logical(bits, 16), 16).astype(F32)


def _log_sigmoid(x):
    return jnp.minimum(x, 0.0) - jnp.log(1.0 + jnp.exp(-jnp.abs(x)))


def _modulated_norm(x, ng, scale1p, shift):
    ms = jnp.mean(x * x, axis=-1, keepdims=True)
    return (x * lax.rsqrt(ms + EPS) * ng) * scale1p + shift


def _mod_kernel(cond_ref, w_ref, b_ref, out_ref):
    c = cond_ref[...]
    s_hi, s_lo = _split2(_silu(c))
    w_hi, w_lo = _split2(w_ref[0])
    acc = jnp.dot(s_hi, w_hi, preferred_element_type=F32)
    acc += jnp.dot(s_lo, w_hi, preferred_element_type=F32)
    acc += jnp.dot(s_hi, w_lo, preferred_element_type=F32)
    out_ref[0] = acc + b_ref[0]


def _mod_call(cond8, w_mod, b_mod):
    depth = w_mod.shape[0]
    nb = 768
    return pl.pallas_call(
        _mod_kernel,
        grid=(depth, 3 * D_MODEL // nb),
        in_specs=[
            pl.BlockSpec((8, D_MODEL), lambda l, j: (0, 0)),
            pl.BlockSpec((1, D_MODEL, nb), lambda l, j: (l, 0, j)),
            pl.BlockSpec((1, 1, nb), lambda l, j: (l, 0, j)),
        ],
        out_specs=pl.BlockSpec((1, 8, nb), lambda l, j: (l, 0, j)),
        out_shape=jax.ShapeDtypeStruct((depth, 8, 3 * D_MODEL), F32),
        compiler_params=pltpu.CompilerParams(dimension_semantics=("arbitrary", "arbitrary")),
        name="adaln_mod",
    )(cond8, w_mod, b_mod.reshape(depth, 1, 3 * D_MODEL))


def _zoh(lre, lim, ldt):
    dt = jnp.exp(ldt)
    mag = jnp.exp(lre * dt)
    a_re = mag * jnp.cos(lim * dt)
    a_im = mag * jnp.sin(lim * dt)
    n_re = a_re - 1.0
    inv = 1.0 / (lre * lre + lim * lim)
    return a_re, a_im, (n_re * lre + a_im * lim) * inv, (a_im * lre - n_re * lim) * inv


def _zoh_kernel(lre_ref, lim_ref, ldt_ref, lre_x_ref, lim_x_ref, ldt_x_ref, bre_ref, bim_ref, cre_ref, cim_ref,
                are_ref, aim_ref, wbu_ref, wc_ref, *, depth):
    a_re, a_im, _, _ = _zoh(lre_ref[...], lim_ref[...], ldt_ref[...])
    are_ref[...] = a_re
    aim_ref[...] = a_im
    _, _, c_re, c_im = _zoh(lre_x_ref[...], lim_x_ref[...], ldt_x_ref[...])
    bre = bre_ref[...]
    bim = bim_ref[...]
    bbar = [c_re * bre - c_im * bim, c_re * bim + c_im * bre]
    cmat = [cre_ref[...], -cim_ref[...]]
    wbu_ref[...] = jnp.zeros_like(wbu_ref)
    wc_ref[...] = jnp.zeros_like(wc_ref)
    for l in range(depth):
        for s in range(S5_SLABS):
            for gl in range(8):
                g = 8 * s + gl
                for ri in range(2):
                    col = ri * 8 * S5_STATE + gl * S5_STATE
                    for d in range(2):
                        src = ((l * 2 + d) * S5_GROUPS + g) * S5_CH
                        row = (d * 8 + gl) * S5_CH
                        wbu_ref[l, s, row:row + S5_CH, col:col + S5_STATE] = bbar[ri][src:src + S5_CH, :].astype(BF16)
                    src = (l * S5_GROUPS + g) * S5_STATE
                    wc_ref[l, s, col:col + S5_STATE, gl * S5_CH:(gl + 1) * S5_CH] = (
                        cmat[ri][src:src + S5_STATE, :].astype(BF16))


def _s5_params(s5_lam_re, s5_lam_im, s5_log_dt, s5_b_re, s5_b_im, s5_c_re, s5_c_im):
    depth = s5_lam_re.shape[0]
    full = (depth, 2, S5_GROUPS, S5_CH, S5_STATE)
    expand = lambda a: jnp.broadcast_to(a, full).reshape(-1, S5_STATE)
    small = lambda a: a.reshape(depth * 2 * S5_GROUPS, -1)
    ct = lambda a: jnp.swapaxes(a, -1, -2).reshape(-1, S5_CH)
    a_re, a_im, w_bu, w_c = pl.pallas_call(
        functools.partial(_zoh_kernel, depth=depth),
        out_shape=(jax.ShapeDtypeStruct((depth * 2 * S5_GROUPS, S5_STATE), F32),
                   jax.ShapeDtypeStruct((depth * 2 * S5_GROUPS, S5_STATE), F32),
                   jax.ShapeDtypeStruct((depth, S5_SLABS, 2 * LANES, 2 * 8 * S5_STATE), BF16),
                   jax.ShapeDtypeStruct((depth, S5_SLABS, 2 * 8 * S5_STATE, LANES), BF16)),
        name="s5_zoh",
    )(small(s5_lam_re), small(s5_lam_im), small(s5_log_dt),
      expand(s5_lam_re[:, :, :, None, :]), expand(s5_lam_im[:, :, :, None, :]), expand(s5_log_dt[:, :, :, None, None]),
      expand(jnp.swapaxes(s5_b_re, -1, -2)[:, None]), expand(jnp.swapaxes(s5_b_im, -1, -2)[:, None]),
      ct(s5_c_re), ct(s5_c_im))
    return a_re.reshape(depth, 2, S5_NC), a_im.reshape(depth, 2, S5_NC), w_bu, w_c


def _s5_state_to_cols(re, im):
    lead = re.shape[:-2]
    st = jnp.stack([re.reshape(lead + (S5_SLABS, 512)), im.reshape(lead + (S5_SLABS, 512))], axis=-2)
    return st.reshape(lead + (2 * S5_NC,))


def _s5_cols_to_state(cols):
    lead = cols.shape[:-1]
    st = cols.reshape(lead + (S5_SLABS, 2, 512))
    re = st[..., 0, :].reshape(lead + (S5_GROUPS, S5_STATE))
    im = st[..., 1, :].reshape(lead + (S5_GROUPS, S5_STATE))
    return re, im


_IN_OFFS = [int(o) for o in np.cumsum([0, S5_WIDTH, S5_WIDTH, GLA_KEY, GLA_KEY, GLA_WIDTH, GLA_WIDTH, GLA_RANK,
                                       D_MODEL, D_MODEL])]


def _split_w_in_kernel(u_ref, qkv_ref, code_ref, ga_ref, gb_ref, m_ref, proj_ref, gate_ref):
    rows = 0
    for src in (u_ref, qkv_ref, code_ref):
        proj_ref[rows:rows + src.shape[1], :] = src[0].astype(BF16)
        rows += src.shape[1]
    proj_ref[rows:, :] = jnp.zeros((PROJ_W - rows, proj_ref.shape[1]), BF16)
    rows = 0
    for src in (ga_ref, gb_ref, m_ref):
        gate_ref[rows:rows + src.shape[1], :] = src[0].astype(BF16)
        rows += src.shape[1]


def _split_w_in(w_in):
    o = _IN_OFFS
    depth = w_in.shape[0]
    wt = jnp.swapaxes(w_in, 1, 2)
    tn = D_MODEL // 2
    rows = lambda lo, hi: pl.BlockSpec((pl.Element(1), pl.Element(hi - lo), pl.Element(tn)),
                                       lambda l, j: (l, lo, j * tn))
    out = lambda n: pl.BlockSpec((None, n, tn), lambda l, j: (l, 0, j))
    return pl.pallas_call(
        _split_w_in_kernel,
        grid=(depth, D_MODEL // tn),
        in_specs=[rows(o[0], o[1]), rows(o[2], o[5]), rows(o[6], o[7]), rows(o[1], o[2]), rows(o[5], o[6]),
                  rows(o[7], o[9])],
        out_specs=(out(PROJ_W), out(GATE_W)),
        out_shape=(jax.ShapeDtypeStruct((depth, PROJ_W, D_MODEL), BF16),
                   jax.ShapeDtypeStruct((depth, GATE_W, D_MODEL), BF16)),
        compiler_params=pltpu.CompilerParams(dimension_semantics=("arbitrary", "arbitrary"),
                                             vmem_limit_bytes=VMEM_LIMIT),
        name="split_w_in",
    )(wt, wt, wt, wt, wt, wt)


def _proj_kernel(*refs, has_pe):
    if has_pe:
        x_ref, pe_ref, shift_ref, scale_ref, ng_ref, w_ref, u_ref, q_ref, k_ref, v_ref, gl_ref = refs
        x = x_ref[...] + pe_ref[...]
    else:
        x_ref, shift_ref, scale_ref, ng_ref, w_ref, u_ref, q_ref, k_ref, v_ref, gl_ref = refs
        x = x_ref[...]
    nsb, tm, _ = x.shape
    x = x.reshape(nsb * tm, D_MODEL)
    h = _modulated_norm(x, ng_ref[...], 1.0 + scale_ref[...], shift_ref[...])
    p = _bdot_t(h, w_ref[...])
    o = 0
    for ref, width in ((u_ref, S5_WIDTH), (q_ref, GLA_KEY), (k_ref, GLA_KEY), (v_ref, GLA_WIDTH), (gl_ref, LANES)):
        ref[...] = p[:, o:o + width].reshape(nsb, tm, width).astype(ref.dtype)
        o += width


def _mod_spec(l, cond_row, part):
    return pl.BlockSpec((None, None, 1, D_MODEL), lambda b, j: (l, cond_row(b), 0, part))


def _layer_spec(l, shape):
    return pl.BlockSpec((None,) + shape, lambda b, j: (l,) + (0,) * len(shape))


def _proj_call(x, pe, mod, l, cond_row, ng, w, tm, nsb=1):
    nseq, length, _ = x.shape
    row = lambda width: pl.BlockSpec((nsb, tm, width), lambda b, j: (b, j, 0))
    in_specs = [row(D_MODEL)]
    args = [x]
    if pe is not None:
        in_specs.append(pl.BlockSpec((tm, D_MODEL), lambda b, j: (j, 0)))
        args.append(pe)
    in_specs += [_mod_spec(l, cond_row, 0), _mod_spec(l, cond_row, 1), _layer_spec(l, (1, D_MODEL)),
                 _layer_spec(l, (PROJ_W, D_MODEL))]
    args += [mod, mod, ng, w]
    sds = lambda width, dt: jax.ShapeDtypeStruct((nseq, length, width), dt)
    return pl.pallas_call(
        functools.partial(_proj_kernel, has_pe=pe is not None),
        grid=(nseq // nsb, length // tm),
        in_specs=in_specs,
        out_specs=(row(S5_WIDTH), row(GLA_KEY), row(GLA_KEY), row(GLA_WIDTH), row(LANES)),
        out_shape=(sds(S5_WIDTH, F32), sds(GLA_KEY, F32), sds(GLA_KEY, F32), sds(GLA_WIDTH, BF16), sds(LANES, F32)),
        compiler_params=pltpu.CompilerParams(dimension_semantics=("arbitrary", "arbitrary"),
                                             vmem_limit_bytes=VMEM_LIMIT),
        name="proj",
    )(*args)


def _s5_perms(nb, tc):
    r = nb * tc
    pf = np.zeros((2 * r, r), np.float32)
    pb = np.zeros((2 * r, r), np.float32)
    for t in range(tc):
        for b in range(nb):
            pf[t * 2 * nb + b, b * tc + t] = 1.0
            pb[t * 2 * nb + nb + b, b * tc + (tc - 1 - t)] = 1.0
    return pf, pb


def _s5_store_unpacked(y2, yf_ref, yb_ref, t_f, t_b, nb, tc):
    rs = 2 * nb
    for t in range(tc):
        yf_ref[:, t_f + t, :] = y2[t * rs:t * rs + nb]
        yb_ref[:, t_b + tc - 1 - t, :] = y2[t * rs + nb:(t + 1) * rs]


def _s5_kernel(*refs, nb, tc, tiles_per_pass):
    use_perm = nb % 8 != 0
    uf1_ref, ub1_ref, uf2_ref, ub2_ref, uf0_ref, ub0_ref = refs[:6]
    pf_ref, pb_ref = refs[6:8] if use_perm else (None, None)
    wbu_ref, wc_ref, are_ref, aim_ref, h0_ref, yf_ref, yb_ref, hfin_ref, bua_ref, bub_ref, hb_ref, st_ref = refs[-12:]
    i = pl.program_id(0)
    rows = nb * tc
    rs = 2 * nb

    def packed_inputs(uf_ref, ub_ref):
        if not use_perm:
            return tuple(jnp.concatenate([r[:, t, :] for t in range(tc)], axis=0).astype(BF16) for r in (uf_ref, ub_ref))
        uf = uf_ref[...].reshape(rows, S5_WIDTH).astype(BF16)
        ub = ub_ref[...].reshape(rows, S5_WIDTH).astype(BF16)
        u2f = jnp.dot(pf_ref[...], uf, preferred_element_type=F32).astype(BF16)
        u2b = jnp.dot(pb_ref[...], ub, preferred_element_type=F32).astype(BF16)
        return [jnp.concatenate([u2f[:, s * LANES:(s + 1) * LANES], u2b[:, s * LANES:(s + 1) * LANES]], axis=1)
                for s in range(S5_SLABS)]

    def bu_slab(lhs, s, dst_ref):
        cols = slice(s * 1024, (s + 1) * 1024)
        if use_perm:
            dst_ref[:, cols] = jnp.dot(lhs[s], wbu_ref[s], preferred_element_type=F32)
            return
        uf, ub = lhs
        bf = jnp.dot(uf[:, s * LANES:(s + 1) * LANES], wbu_ref[s, :LANES, :], preferred_element_type=F32)
        bb = jnp.dot(ub[:, s * LANES:(s + 1) * LANES], wbu_ref[s, LANES:, :], preferred_element_type=F32)
        for t in range(tc):
            dst_ref[t * rs:t * rs + nb, cols] = bf[t * nb:(t + 1) * nb]
            dst_ref[t * rs + nb:(t + 1) * rs, cols] = bb[(tc - 1 - t) * nb:(tc - t) * nb]

    @pl.when(i == 0)
    def _():
        st_ref[...] = h0_ref[...]
        lhs0 = packed_inputs(uf0_ref, ub0_ref)
        for s in range(S5_SLABS):
            bu_slab(lhs0, s, bua_ref)

    half_step = functools.partial(_s5_half_step, wc_ref=wc_ref, are_ref=are_ref, aim_ref=aim_ref, hb_ref=hb_ref,
                                  st_ref=st_ref, bu_slab=bu_slab, rs=rs, tc=tc, tiles_per_pass=tiles_per_pass)
    y2, tail = half_step(bua_ref, bub_ref, packed_inputs(uf1_ref, ub1_ref))
    _s5_store_unpacked(y2, yf_ref, yb_ref, 0, tc, nb, tc)
    y2, _ = half_step(bub_ref, bua_ref, packed_inputs(uf2_ref, ub2_ref), after=tail)
    _s5_store_unpacked(y2, yf_ref, yb_ref, tc, 0, nb, tc)
    hfin_ref[...] = st_ref[...]


def _s5_half_step(cur_ref, nxt_ref, lhs_next, *, wc_ref, are_ref, aim_ref, hb_ref, st_ref, bu_slab,
                  rs, tc, tiles_per_pass, after=None):
    grp = max(1, 16 // rs)
    ys = []
    for s in range(S5_SLABS):
        bu_slab(lhs_next, s, nxt_ref)
        for c0 in range(4 * s, 4 * s + 4, tiles_per_pass):
            cts = list(range(c0, c0 + tiles_per_pass))
            cre = [(ct // 4) * 1024 + (ct % 4) * LANES for ct in cts]
            cim = [c + 512 for c in cre]
            a_re = [are_ref[:, ct * LANES:(ct + 1) * LANES] for ct in cts]
            a_im = [aim_ref[:, ct * LANES:(ct + 1) * LANES] for ct in cts]
            h_re = [st_ref[:, c:c + LANES] for c in cre]
            h_im = [st_ref[:, c:c + LANES] for c in cim]
            if after is not None and c0 == 0:
                h_re[0] = h_re[0] + _exact_zero_of(after)
            for t0 in range(0, tc, grp):
                out_re = [[] for _ in cts]
                out_im = [[] for _ in cts]
                for t in range(t0, t0 + grp):
                    r0 = t * rs
                    for j in range(len(cts)):
                        b_re = cur_ref[r0:r0 + rs, cre[j]:cre[j] + LANES]
                        b_im = cur_ref[r0:r0 + rs, cim[j]:cim[j] + LANES]
                        n_re = a_re[j] * h_re[j] - a_im[j] * h_im[j] + b_re
                        n_im = a_re[j] * h_im[j] + a_im[j] * h_re[j] + b_im
                        h_re[j], h_im[j] = n_re, n_im
                        out_re[j].append(n_re)
                        out_im[j].append(n_im)
                for j in range(len(cts)):
                    blk_re = out_re[j][0] if grp == 1 else jnp.concatenate(out_re[j], axis=0)
                    blk_im = out_im[j][0] if grp == 1 else jnp.concatenate(out_im[j], axis=0)
                    hb_ref[t0 * rs:(t0 + grp) * rs, cre[j]:cre[j] + LANES] = blk_re.astype(BF16)
                    hb_ref[t0 * rs:(t0 + grp) * rs, cim[j]:cim[j] + LANES] = blk_im.astype(BF16)
            for j in range(len(cts)):
                st_ref[:, cre[j]:cre[j] + LANES] = h_re[j]
                st_ref[:, cim[j]:cim[j] + LANES] = h_im[j]
        ys.append(jnp.dot(hb_ref[:, s * 1024:(s + 1) * 1024], wc_ref[s], preferred_element_type=F32))
    return jnp.concatenate(ys, axis=1), h_re[-1]


def _s5_call(u, l, w_bu, w_c, are, aim, h0, tc, tiles_per_pass):
    nb, length, _ = u.shape
    n = length // tc
    rows = nb * tc
    const = lambda shape: pl.BlockSpec(shape, lambda i: (0,) * len(shape))
    layer = lambda shape: pl.BlockSpec((None,) + shape, lambda i: (l,) + (0,) * len(shape))
    blk = lambda chunk: pl.BlockSpec((nb, tc, S5_WIDTH), lambda i: (0, chunk(i), 0))
    out_f = pl.BlockSpec((nb, 2 * tc, S5_WIDTH), lambda i: (0, i, 0))
    out_b = pl.BlockSpec((nb, 2 * tc, S5_WIDTH), lambda i: (0, n // 2 - 1 - i, 0))
    second = lambda i: jnp.minimum(2 * i + 2, n - 1)
    in_specs = [blk(lambda i: 2 * i + 1), blk(lambda i: n - 2 - 2 * i), blk(second), blk(lambda i: n - 1 - second(i)),
                blk(lambda i: 0), blk(lambda i: n - 1)]
    args = [u] * 6
    if nb % 8 != 0:
        pf, pb = _s5_perms(nb, tc)
        in_specs += [const((2 * rows, rows)), const((2 * rows, rows))]
        args += [jnp.asarray(pf, BF16), jnp.asarray(pb, BF16)]
    in_specs += [layer((S5_SLABS, 2 * LANES, 1024)), layer((S5_SLABS, 1024, LANES)),
                 layer((2 * nb, S5_NC)), layer((2 * nb, S5_NC)), const((2 * nb, 2 * S5_NC))]
    args += [w_bu, w_c, are, aim, h0]
    return pl.pallas_call(
        functools.partial(_s5_kernel, nb=nb, tc=tc, tiles_per_pass=tiles_per_pass),
        grid=(n // 2,),
        in_specs=in_specs,
        out_specs=(out_f, out_b, const((2 * nb, 2 * S5_NC))),
        out_shape=(jax.ShapeDtypeStruct(u.shape, F32), jax.ShapeDtypeStruct(u.shape, F32),
                   jax.ShapeDtypeStruct((2 * nb, 2 * S5_NC), F32)),
        scratch_shapes=[pltpu.VMEM((2 * rows, 2 * S5_NC), F32), pltpu.VMEM((2 * rows, 2 * S5_NC), F32),
                        pltpu.VMEM((2 * rows, 2 * S5_NC), BF16), pltpu.VMEM((2 * nb, 2 * S5_NC), F32)],
        compiler_params=pltpu.CompilerParams(dimension_semantics=("arbitrary",), vmem_limit_bytes=VMEM_LIMIT),
        name="s5_scan",
    )(*args)


def _gla_kernel(*refs, length, has_s0, nsq):
    q_ref, k_ref, v_ref, gl_ref, wg_ref, bg_ref, ng_ref = refs[:7]
    s0_ref = refs[7] if has_s0 else None
    o_ref, sfin_ref, sf_ref, sb_ref = refs[-4:]
    c = GLA_CHUNK
    n = length // c
    ri = lax.broadcasted_iota(jnp.int32, (c, c), 0)
    ci = lax.broadcasted_iota(jnp.int32, (c, c), 1)
    lane = lax.broadcasted_iota(jnp.int32, (1, GLA_KEY), 1)
    head_mask = [(lane >= h * GLA_DK) & (lane < (h + 1) * GLA_DK) for h in range(GLA_HEADS)]
    ng = ng_ref[...]
    causal = [ci <= ri, ci >= ri]
    tri = [m.astype(BF16) for m in causal]
    end = (c - 1, 0)
    chains = [(sq, d) for sq in range(nsq) for d in (0, 1)]
    key_rows = lambda h: slice(h * GLA_DK, (h + 1) * GLA_DK)
    val_cols = lambda h: slice(h * GLA_DV, (h + 1) * GLA_DV)

    sb_ref[...] = jnp.zeros_like(sb_ref)
    for sq, d in chains:
        for h in range(GLA_HEADS):
            if has_s0:
                sf_ref[sq, d, h] = s0_ref[sq, d, h]
                sb_ref[sq, d, key_rows(h), val_cols(h)] = s0_ref[sq, d, h].astype(BF16)
            else:
                sf_ref[sq, d, h] = jnp.zeros((GLA_DK, GLA_DV), F32)

    def chunk_group(r):
        qc = [q_ref[sq, pl.ds(r[d], c), :] * (GLA_DK ** -0.5) for sq, d in chains]
        kc = [k_ref[sq, pl.ds(r[d], c), :] for sq, d in chains]
        vc = [v_ref[sq, pl.ds(r[d], c), :] for sq, d in chains]
        logits = [_bdot(gl_ref[sq, pl.ds(r[d], c), :], wg_ref[d]) + bg_ref[d] for sq, d in chains]
        g = [_split2(_log_sigmoid(x) * (1.0 / GLA_GATE_NORM)) for x in logits]
        b = [jnp.dot(tri[d], hi, preferred_element_type=F32) + jnp.dot(tri[d], lo, preferred_element_type=F32)
             for (sq, d), (hi, lo) in zip(chains, g)]
        bt = [x.T for x in b]
        kt = [x.T for x in kc]
        b_mid = [x[c // 2:c // 2 + 1, :] for x in b]
        bt_mid = [x[:, c // 2:c // 2 + 1] for x in bt]
        bt_end = [x[:, end[d]:end[d] + 1] for (sq, d), x in zip(chains, bt)]
        q_in = [(x * jnp.exp(y)).astype(BF16) for x, y in zip(qc, b)]
        q_mid = [(x * jnp.exp(y - m)).astype(BF16) for x, y, m in zip(qc, b, b_mid)]
        k_mid_t = [(x * jnp.exp(m - y)).astype(BF16) for x, y, m in zip(kt, bt, bt_mid)]
        k_end_t = [(x * jnp.exp(e - y)).astype(BF16) for x, y, e in zip(kt, bt, bt_end)]
        q_st = [jnp.concatenate([jnp.where(head_mask[h], x, jnp.zeros_like(x)) for h in range(GLA_HEADS)], axis=0)
                for x in q_mid]
        a = [jnp.dot(x, y, preferred_element_type=F32) for x, y in zip(q_st, k_mid_t)]
        o_inter = [jnp.dot(x, sb_ref[sq, d], preferred_element_type=F32) for (sq, d), x in zip(chains, q_in)]
        parts = [[] for _ in chains]
        for h in range(GLA_HEADS):
            for i, (sq, d) in enumerate(chains):
                a_h = jnp.where(causal[d], a[i][h * c:(h + 1) * c, :], 0.0).astype(BF16)
                parts[i].append(jnp.dot(a_h, vc[i][:, val_cols(h)], preferred_element_type=F32))
        for h in range(GLA_HEADS):
            for i, (sq, d) in enumerate(chains):
                upd = jnp.dot(k_end_t[i][key_rows(h), :], vc[i][:, val_cols(h)], preferred_element_type=F32)
                s_new = jnp.exp(bt_end[i][key_rows(h), :]) * sf_ref[sq, d, h] + upd
                sf_ref[sq, d, h] = s_new
                sb_ref[sq, d, key_rows(h), val_cols(h)] = s_new.astype(BF16)
        return [x + jnp.concatenate(p, axis=1) for x, p in zip(o_inter, parts)]

    def finish(o, sq, r0):
        o = o + o_ref[sq, pl.ds(r0, c), :]
        normed = []
        for h in range(GLA_HEADS):
            oh = o[:, val_cols(h)]
            ms = jnp.mean(oh * oh, axis=-1, keepdims=True)
            normed.append(oh * lax.rsqrt(ms + EPS))
        o_ref[sq, pl.ds(r0, c), :] = jnp.concatenate(normed, axis=1) * ng

    def first_half(step, carry):
        r = (pl.multiple_of(step * c, c), pl.multiple_of((n - 1 - step) * c, c))
        for (sq, d), o in zip(chains, chunk_group(r)):
            o_ref[sq, pl.ds(r[d], c), :] = o
        return carry

    def second_half(step, carry):
        r = (pl.multiple_of(step * c, c), pl.multiple_of((n - 1 - step) * c, c))
        for (sq, d), o in zip(chains, chunk_group(r)):
            finish(o, sq, r[d])
        return carry

    lax.fori_loop(0, n // 2, first_half, 0)
    lax.fori_loop(n // 2, n, second_half, 0)
    for sq, d in chains:
        for h in range(GLA_HEADS):
            sfin_ref[sq, d, h] = sf_ref[sq, d, h]


def _gla_call(q, k, v, gl, l, wg, bg, ng, s0, nsq=2):
    nseq, length, _ = q.shape
    seq = lambda width: pl.BlockSpec((nsq, length, width), lambda b: (b, 0, 0))
    layer = lambda shape: pl.BlockSpec((None,) + shape, lambda b: (l,) + (0,) * len(shape))
    in_specs = [seq(GLA_KEY), seq(GLA_KEY), seq(GLA_WIDTH), seq(LANES),
                layer((2, LANES, GLA_KEY)), layer((2, 1, GLA_KEY)), layer((1, GLA_WIDTH))]
    args = [q, k, v, gl, wg, bg, ng]
    if s0 is not None:
        in_specs.append(pl.BlockSpec((nsq, None, 2, GLA_HEADS, GLA_DK, GLA_DV), lambda b: (b, l, 0, 0, 0, 0)))
        args.append(s0)
    return pl.pallas_call(
        functools.partial(_gla_kernel, length=length, has_s0=s0 is not None, nsq=nsq),
        grid=(nseq // nsq,),
        in_specs=in_specs,
        out_specs=(seq(GLA_WIDTH), pl.BlockSpec((nsq, 2, GLA_HEADS, GLA_DK, GLA_DV), lambda b: (b, 0, 0, 0, 0))),
        out_shape=(jax.ShapeDtypeStruct((nseq, length, GLA_WIDTH), F32),
                   jax.ShapeDtypeStruct((nseq, 2, GLA_HEADS, GLA_DK, GLA_DV), F32)),
        scratch_shapes=[pltpu.VMEM((nsq, 2, GLA_HEADS, GLA_DK, GLA_DV), F32),
                        pltpu.VMEM((nsq, 2, GLA_KEY, GLA_WIDTH), BF16)],
        compiler_params=pltpu.CompilerParams(dimension_semantics=("arbitrary",), vmem_limit_bytes=VMEM_LIMIT),
        name="gla",
    )(*args)


def _mix_kernel(*refs, has_pe, last, row_groups):
    if has_pe:
        (x_ref, pe_ref, yf_ref, yb_ref, u_ref, o_ref, shift_ref, scale_ref, gate_ref, ng_ref, wg_ref, d_ref,
         wglu_ref, bglu_ref, wpa_ref, wpb_ref, wo_ref, fng_ref, out_ref) = refs
    else:
        (x_ref, yf_ref, yb_ref, u_ref, o_ref, shift_ref, scale_ref, gate_ref, ng_ref, wg_ref, d_ref,
         wglu_ref, bglu_ref, wpa_ref, wpb_ref, wo_ref, fng_ref, out_ref) = refs
    nsb, tm, _ = x_ref.shape
    grp = [(i, slice(j * tm // row_groups, (j + 1) * tm // row_groups)) for i in range(nsb) for j in range(row_groups)]
    x = [x_ref[i, r, :] + pe_ref[r, :] if has_pe else x_ref[i, r, :] for i, r in grp]
    h = [_modulated_norm(v, ng_ref[...], 1.0 + scale_ref[...], shift_ref[...]).astype(BF16) for v in x]
    gts = [_bdot_t(v, wg_ref[...]) for v in h]
    y = [_gelu_tanh(yf_ref[i, r, :] + yb_ref[i, r, :] + d_ref[...] * u_ref[i, r, :]) for i, r in grp]
    glu = [_bdot(v, wglu_ref[...]) for v in y]
    y = [v * _sigmoid(g + bglu_ref[...]) for v, g in zip(y, glu)]
    y_a = [v * _silu(g[:, :S5_WIDTH]) for v, g in zip(y, gts)]
    y_b = [o_ref[i, r, :] * _silu(g[:, S5_WIDTH:2 * S5_WIDTH]) for (i, r), g in zip(grp, gts)]
    p_a = [_bdot(v, wpa_ref[...]) for v in y_a]
    p_b = [_bdot(v, wpb_ref[...]) for v in y_b]
    merged = [_sigmoid(g[:, 2 * S5_WIDTH:2 * S5_WIDTH + D_MODEL]) * a + _sigmoid(g[:, 2 * S5_WIDTH + D_MODEL:]) * b
              for g, a, b in zip(gts, p_a, p_b)]
    xn = [v + gate_ref[...] * _bdot(m, wo_ref[...]) for v, m in zip(x, merged)]
    for (i, r), v in zip(grp, xn):
        if last:
            ms = jnp.mean(v * v, axis=-1, keepdims=True)
            v = v * lax.rsqrt(ms + EPS) * fng_ref[...]
        out_ref[i, r, :] = v


def _mix_call(x, pe, yf, yb, u, o, mod, l, cond_row, ng, wg, s5_d, wglu, bglu, wpa, wpb, wo, fng, tm, last, nsb=1):
    nseq, length, _ = x.shape
    row = lambda width: pl.BlockSpec((nsb, tm, width), lambda b, j: (b, j, 0))
    in_specs = [row(D_MODEL)]
    args = [x]
    if pe is not None:
        in_specs.append(pl.BlockSpec((tm, D_MODEL), lambda b, j: (j, 0)))
        args.append(pe)
    in_specs += [row(S5_WIDTH), row(S5_WIDTH), row(S5_WIDTH), row(GLA_WIDTH),
                 _mod_spec(l, cond_row, 0), _mod_spec(l, cond_row, 1), _mod_spec(l, cond_row, 2),
                 _layer_spec(l, (1, D_MODEL)), _layer_spec(l, (GATE_W, D_MODEL)), _layer_spec(l, (1, S5_WIDTH)),
                 _layer_spec(l, (S5_WIDTH, S5_WIDTH)), _layer_spec(l, (1, S5_WIDTH)),
                 _layer_spec(l, (S5_WIDTH, D_MODEL)), _layer_spec(l, (GLA_WIDTH, D_MODEL)),
                 _layer_spec(l, (D_MODEL, D_MODEL)), pl.BlockSpec((1, D_MODEL), lambda b, j: (0, 0))]
    args += [yf, yb, u, o, mod, mod, mod, ng, wg, s5_d, wglu, bglu, wpa, wpb, wo, fng]
    return pl.pallas_call(
        functools.partial(_mix_kernel, has_pe=pe is not None, last=last, row_groups=max(1, tm // 256)),
        grid=(nseq // nsb, length // tm),
        in_specs=in_specs,
        out_specs=row(D_MODEL),
        out_shape=jax.ShapeDtypeStruct(x.shape, F32),
        compiler_params=pltpu.CompilerParams(dimension_semantics=("arbitrary", "arbitrary"),
                                             vmem_limit_bytes=VMEM_LIMIT),
        name="mix",
    )(*args)


def _grid_pos_embed(length, dim):
    rows = length // GRID_W
    quarter = dim // 4
    freqs = jnp.exp(-math.log(10000.0) * jnp.arange(quarter, dtype=F32) / quarter)

    def sincos(pos):
        ang = pos.astype(F32)[:, None] * freqs[None, :]
        return jnp.concatenate([jnp.sin(ang), jnp.cos(ang)], axis=-1)

    er = sincos(jnp.arange(rows))
    ec = sincos(jnp.arange(GRID_W))
    pe = jnp.concatenate([jnp.broadcast_to(er[:, None, :], (rows, GRID_W, dim // 2)),
                          jnp.broadcast_to(ec[None, :, :], (rows, GRID_W, dim // 2))], axis=-1)
    return pe.reshape(rows * GRID_W, dim)


def kernel(x_prompt, x_sample, c, state_s5_re, state_s5_im, state_gla, c_ctx, norm_g, w_mod, b_mod, w_in, gla_wg_up,
           gla_bg, gla_norm_g, s5_lam_re, s5_lam_im, s5_log_dt, s5_b_re, s5_b_im, s5_c_re, s5_c_im, s5_d, w_glu,
           b_glu, w_pa, w_pb, w_o, final_norm_g):
    depth = w_in.shape[0]
    bp, lp, _ = x_prompt.shape
    bs, ls, _ = x_sample.shape

    cond8 = jnp.zeros((8, D_MODEL), F32).at[0].set(c_ctx).at[1:1 + bs].set(c)
    mod = _mod_call(cond8, w_mod, b_mod)

    a_re, a_im, w_bu, w_c = _s5_params(s5_lam_re, s5_lam_im, s5_log_dt, s5_b_re, s5_b_im, s5_c_re, s5_c_im)

    w_proj, w_gate = _split_w_in(w_in)
    wg_up = jnp.concatenate([gla_wg_up, jnp.zeros((depth, 2, LANES - GLA_RANK, GLA_KEY), F32)], axis=2).astype(BF16)
    w_glu_b, w_pa_b, w_pb_b, w_o_b = (w.astype(BF16) for w in (w_glu, w_pa, w_pb, w_o))

    vec = lambda a: a[:, None, :]
    mod = mod.reshape(depth, 8, 1, 3 * D_MODEL)
    ng, gng, bg = vec(norm_g), vec(gla_norm_g), gla_bg[:, :, None, :]
    common = (ng, w_gate, vec(s5_d), w_glu_b, vec(b_glu), w_pa_b, w_pb_b, w_o_b, final_norm_g[None])
    are_p, aim_p = jnp.repeat(a_re, bp, axis=1), jnp.repeat(a_im, bp, axis=1)
    are_s, aim_s = jnp.repeat(a_re, bs, axis=1), jnp.repeat(a_im, bs, axis=1)
    ctx_row = lambda b: 0
    cond_row = lambda b: b + 1

    pe = _grid_pos_embed(ls, D_MODEL)
    zero_h0 = jnp.zeros((2 * bp, 2 * S5_NC), F32)
    xp, xs = x_prompt, x_sample
    new_re, new_im, new_gla = [], [], []
    for l in range(depth):
        last = l == depth - 1

        u, q, k, v, gl = _proj_call(xp, None, mod, l, ctx_row, ng, w_proj, tm=lp, nsb=4)
        yf, yb, hfin = _s5_call(u, l, w_bu, w_c, are_p, aim_p, zero_h0, tc=8, tiles_per_pass=2)
        o, sfin = _gla_call(q, k, v, gl, l, wg_up, bg, gng, None)
        new_gla.append(sfin)
        xp = _mix_call(xp, None, yf, yb, u, o, mod, l, ctx_row, *common, tm=lp, last=last, nsb=2)
        re_l, im_l = _s5_cols_to_state(hfin.reshape(2, bp, 2 * S5_NC))
        new_re.append(jnp.swapaxes(re_l, 0, 1))
        new_im.append(jnp.swapaxes(im_l, 0, 1))

        pe_l = pe if l == 0 else None
        u, q, k, v, gl = _proj_call(xs, pe_l, mod, l, cond_row, ng, w_proj, tm=1024)
        h0 = _s5_state_to_cols(jnp.swapaxes(state_s5_re[:, l], 0, 1), jnp.swapaxes(state_s5_im[:, l], 0, 1))
        yf, yb, _ = _s5_call(u, l, w_bu, w_c, are_s, aim_s, h0.reshape(2 * bs, 2 * S5_NC), tc=32, tiles_per_pass=4)
        o, _ = _gla_call(q, k, v, gl, l, wg_up, bg, gng, state_gla)
        xs = _mix_call(xs, pe_l, yf, yb, u, o, mod, l, cond_row, *common, tm=512, last=last)

    return (xp, xs, jnp.stack(new_re, axis=1), jnp.stack(new_im, axis=1), jnp.stack(new_gla, axis=1))
```

```python
import functools
import math

import numpy as np
import jax
import jax.numpy as jnp
from jax import lax
from jax.experimental import pallas as pl
from jax.experimental.pallas import tpu as pltpu

F32 = jnp.float32
BF16 = jnp.bfloat16

D_MODEL = 1024
GRID_W = 64
S5_WIDTH = 512
S5_GROUPS = 32
S5_CH = 16
S5_STATE = 64
S5_NC = S5_GROUPS * S5_STATE
S5_SLABS = 4
GLA_HEADS = 4
GLA_DK = 64
GLA_DV = 128
GLA_KEY = GLA_HEADS * GLA_DK
GLA_WIDTH = GLA_HEADS * GLA_DV
GLA_RANK = 16
GLA_GATE_NORM = 16.0
GLA_CHUNK = 128
EPS = 1e-6
LANES = 128
PROJ_W = S5_WIDTH + 2 * GLA_KEY + GLA_WIDTH + LANES
GATE_W = 2 * S5_WIDTH + 2 * D_MODEL
VMEM_LIMIT = 52 * 1024 * 1024


def _bdot(a, b):
    return jnp.dot(a.astype(BF16), b.astype(BF16), preferred_element_type=F32)


def _bdot_t(a, bt):
    return lax.dot_general(a.astype(BF16), bt.astype(BF16), (((1,), (1,)), ((), ())), preferred_element_type=F32)


def _split2(x):
    hi = x.astype(BF16)
    lo = (x - hi.astype(F32)).astype(BF16)
    return hi, lo


def _sigmoid(x):
    return 1.0 / (1.0 + jnp.exp(-x))


def _silu(x):
    return x * _sigmoid(x)


def _gelu_tanh(x):
    c = math.sqrt(2.0 / math.pi)
    return 0.5 * x * (1.0 + jnp.tanh(c * (x + 0.044715 * (x * x * x))))


def _exact_zero_of(x):
    bits = lax.bitcast_convert_type(x, jnp.int32)
    return lax.shift_right_logical(lax.shift_right_logical(bits, 16), 16).astype(F32)


def _log_sigmoid(x):
    return jnp.minimum(x, 0.0) - jnp.log(1.0 + jnp.exp(-jnp.abs(x)))


def _modulated_norm(x, ng, scale1p, shift):
    ms = jnp.mean(x * x, axis=-1, keepdims=True)
    return (x * lax.rsqrt(ms + EPS) * ng) * scale1p + shift


def _mod_kernel(cond_ref, w_ref, b_ref, out_ref):
    c = cond_ref[...]
    s_hi, s_lo = _split2(_silu(c))
    w_hi, w_lo = _split2(w_ref[0])
    acc = jnp.dot(s_hi, w_hi, preferred_element_type=F32)
    acc += jnp.dot(s_lo, w_hi, preferred_element_type=F32)
    acc += jnp.dot(s_hi, w_lo, preferred_element_type=F32)
    out_ref[0] = acc + b_ref[0]


def _mod_call(cond8, w_mod, b_mod):
    depth = w_mod.shape[0]
    nb = 768
    return pl.pallas_call(
        _mod_kernel,
        grid=(depth, 3 * D_MODEL // nb),
        in_specs=[
            pl.BlockSpec((8, D_MODEL), lambda l, j: (0, 0)),
            pl.BlockSpec((1, D_MODEL, nb), lambda l, j: (l, 0, j)),
            pl.BlockSpec((1, 1, nb), lambda l, j: (l, 0, j)),
        ],
        out_specs=pl.BlockSpec((1, 8, nb), lambda l, j: (l, 0, j)),
        out_shape=jax.ShapeDtypeStruct((depth, 8, 3 * D_MODEL), F32),
        compiler_params=pltpu.CompilerParams(dimension_semantics=("arbitrary", "arbitrary")),
        name="adaln_mod",
    )(cond8, w_mod, b_mod.reshape(depth, 1, 3 * D_MODEL))


def _zoh(lre, lim, ldt):
    dt = jnp.exp(ldt)
    mag = jnp.exp(lre * dt)
    a_re = mag * jnp.cos(lim * dt)
    a_im = mag * jnp.sin(lim * dt)
    n_re = a_re - 1.0
    inv = 1.0 / (lre * lre + lim * lim)
    return a_re, a_im, (n_re * lre + a_im * lim) * inv, (a_im * lre - n_re * lim) * inv


def _zoh_kernel(lre_ref, lim_ref, ldt_ref, lre_x_ref, lim_x_ref, ldt_x_ref, bre_ref, bim_ref, cre_ref, cim_ref,
                are_ref, aim_ref, wbu_ref, wc_ref, *, depth):
    a_re, a_im, _, _ = _zoh(lre_ref[...], lim_ref[...], ldt_ref[...])
    are_ref[...] = a_re
    aim_ref[...] = a_im
    _, _, c_re, c_im = _zoh(lre_x_ref[...], lim_x_ref[...], ldt_x_ref[...])
    bre = bre_ref[...]
    bim = bim_ref[...]
    bbar = [c_re * bre - c_im * bim, c_re * bim + c_im * bre]
    cmat = [cre_ref[...], -cim_ref[...]]
    wbu_ref[...] = jnp.zeros_like(wbu_ref)
    wc_ref[...] = jnp.zeros_like(wc_ref)
    for l in range(depth):
        for s in range(S5_SLABS):
            for gl in range(8):
                g = 8 * s + gl
                for ri in range(2):
                    col = ri * 8 * S5_STATE + gl * S5_STATE
                    for d in range(2):
                        src = ((l * 2 + d) * S5_GROUPS + g) * S5_CH
                        row = (d * 8 + gl) * S5_CH
                        wbu_ref[l, s, row:row + S5_CH, col:col + S5_STATE] = bbar[ri][src:src + S5_CH, :].astype(BF16)
                    src = (l * S5_GROUPS + g) * S5_STATE
                    wc_ref[l, s, col:col + S5_STATE, gl * S5_CH:(gl + 1) * S5_CH] = (
                        cmat[ri][src:src + S5_STATE, :].astype(BF16))


def _s5_params(s5_lam_re, s5_lam_im, s5_log_dt, s5_b_re, s5_b_im, s5_c_re, s5_c_im):
    depth = s5_lam_re.shape[0]
    full = (depth, 2, S5_GROUPS, S5_CH, S5_STATE)
    expand = lambda a: jnp.broadcast_to(a, full).reshape(-1, S5_STATE)
    small = lambda a: a.reshape(depth * 2 * S5_GROUPS, -1)
    ct = lambda a: jnp.swapaxes(a, -1, -2).reshape(-1, S5_CH)
    a_re, a_im, w_bu, w_c = pl.pallas_call(
        functools.partial(_zoh_kernel, depth=depth),
        out_shape=(jax.ShapeDtypeStruct((depth * 2 * S5_GROUPS, S5_STATE), F32),
                   jax.ShapeDtypeStruct((depth * 2 * S5_GROUPS, S5_STATE), F32),
                   jax.ShapeDtypeStruct((depth, S5_SLABS, 2 * LANES, 2 * 8 * S5_STATE), BF16),
                   jax.ShapeDtypeStruct((depth, S5_SLABS, 2 * 8 * S5_STATE, LANES), BF16)),
        name="s5_zoh",
    )(small(s5_lam_re), small(s5_lam_im), small(s5_log_dt),
      expand(s5_lam_re[:, :, :, None, :]), expand(s5_lam_im[:, :, :, None, :]), expand(s5_log_dt[:, :, :, None, None]),
      expand(jnp.swapaxes(s5_b_re, -1, -2)[:, None]), expand(jnp.swapaxes(s5_b_im, -1, -2)[:, None]),
      ct(s5_c_re), ct(s5_c_im))
    return a_re.reshape(depth, 2, S5_NC), a_im.reshape(depth, 2, S5_NC), w_bu, w_c


def _s5_state_to_cols(re, im):
    lead = re.shape[:-2]
    st = jnp.stack([re.reshape(lead + (S5_SLABS, 512)), im.reshape(lead + (S5_SLABS, 512))], axis=-2)
    return st.reshape(lead + (2 * S5_NC,))


def _s5_cols_to_state(cols):
    lead = cols.shape[:-1]
    st = cols.reshape(lead + (S5_SLABS, 2, 512))
    re = st[..., 0, :].reshape(lead + (S5_GROUPS, S5_STATE))
    im = st[..., 1, :].reshape(lead + (S5_GROUPS, S5_STATE))
    return re, im


_IN_OFFS = [int(o) for o in np.cumsum([0, S5_WIDTH, S5_WIDTH, GLA_KEY, GLA_KEY, GLA_WIDTH, GLA_WIDTH, GLA_RANK,
                                       D_MODEL, D_MODEL])]


def _split_w_in_kernel(u_ref, qkv_ref, code_ref, ga_ref, gb_ref, m_ref, proj_ref, gate_ref):
    rows = 0
    for src in (u_ref, qkv_ref, code_ref):
        proj_ref[rows:rows + src.shape[1], :] = src[0].astype(BF16)
        rows += src.shape[1]
    proj_ref[rows:, :] = jnp.zeros((PROJ_W - rows, proj_ref.shape[1]), BF16)
    rows = 0
    for src in (ga_ref, gb_ref, m_ref):
        gate_ref[rows:rows + src.shape[1], :] = src[0].astype(BF16)
        rows += src.shape[1]


def _split_w_in(w_in):
    o = _IN_OFFS
    depth = w_in.shape[0]
    wt = jnp.swapaxes(w_in, 1, 2)
    tn = D_MODEL // 2
    rows = lambda lo, hi: pl.BlockSpec((pl.Element(1), pl.Element(hi - lo), pl.Element(tn)),
                                       lambda l, j: (l, lo, j * tn))
    out = lambda n: pl.BlockSpec((None, n, tn), lambda l, j: (l, 0, j))
    return pl.pallas_call(
        _split_w_in_kernel,
        grid=(depth, D_MODEL // tn),
        in_specs=[rows(o[0], o[1]), rows(o[2], o[5]), rows(o[6], o[7]), rows(o[1], o[2]), rows(o[5], o[6]),
                  rows(o[7], o[9])],
        out_specs=(out(PROJ_W), out(GATE_W)),
        out_shape=(jax.ShapeDtypeStruct((depth, PROJ_W, D_MODEL), BF16),
                   jax.ShapeDtypeStruct((depth, GATE_W, D_MODEL), BF16)),
        compiler_params=pltpu.CompilerParams(dimension_semantics=("arbitrary", "arbitrary"),
                                             vmem_limit_bytes=VMEM_LIMIT),
        name="split_w_in",
    )(wt, wt, wt, wt, wt, wt)


def _proj_kernel(*refs, has_pe):
    if has_pe:
        x_ref, pe_ref, shift_ref, scale_ref, ng_ref, w_ref, u_ref, q_ref, k_ref, v_ref, gl_ref = refs
        x = x_ref[...] + pe_ref[...]
    else:
        x_ref, shift_ref, scale_ref, ng_ref, w_ref, u_ref, q_ref, k_ref, v_ref, gl_ref = refs
        x = x_ref[...]
    nsb, tm, _ = x.shape
    x = x.reshape(nsb * tm, D_MODEL)
    h = _modulated_norm(x, ng_ref[...], 1.0 + scale_ref[...], shift_ref[...])
    p = _bdot_t(h, w_ref[...])
    o = 0
    for ref, width in ((u_ref, S5_WIDTH), (q_ref, GLA_KEY), (k_ref, GLA_KEY), (v_ref, GLA_WIDTH), (gl_ref, LANES)):
        ref[...] = p[:, o:o + width].reshape(nsb, tm, width).astype(ref.dtype)
        o += width


def _mod_spec(l, cond_row, part):
    return pl.BlockSpec((None, None, 1, D_MODEL), lambda b, j: (l, cond_row(b), 0, part))


def _layer_spec(l, shape):
    return pl.BlockSpec((None,) + shape, lambda b, j: (l,) + (0,) * len(shape))


def _proj_call(x, pe, mod, l, cond_row, ng, w, tm, nsb=1):
    nseq, length, _ = x.shape
    row = lambda width: pl.BlockSpec((nsb, tm, width), lambda b, j: (b, j, 0))
    in_specs = [row(D_MODEL)]
    args = [x]
    if pe is not None:
        in_specs.append(pl.BlockSpec((tm, D_MODEL), lambda b, j: (j, 0)))
        args.append(pe)
    in_specs += [_mod_spec(l, cond_row, 0), _mod_spec(l, cond_row, 1), _layer_spec(l, (1, D_MODEL)),
                 _layer_spec(l, (PROJ_W, D_MODEL))]
    args += [mod, mod, ng, w]
    sds = lambda width, dt: jax.ShapeDtypeStruct((nseq, length, width), dt)
    return pl.pallas_call(
        functools.partial(_proj_kernel, has_pe=pe is not None),
        grid=(nseq // nsb, length // tm),
        in_specs=in_specs,
        out_specs=(row(S5_WIDTH), row(GLA_KEY), row(GLA_KEY), row(GLA_WIDTH), row(LANES)),
        out_shape=(sds(S5_WIDTH, F32), sds(GLA_KEY, F32), sds(GLA_KEY, F32), sds(GLA_WIDTH, BF16), sds(LANES, F32)),
        compiler_params=pltpu.CompilerParams(dimension_semantics=("arbitrary", "arbitrary"),
                                             vmem_limit_bytes=VMEM_LIMIT),
        name="proj",
    )(*args)


def _s5_perms(nb, tc):
    r = nb * tc
    pf = np.zeros((2 * r, r), np.float32)
    pb = np.zeros((2 * r, r), np.float32)
    for t in range(tc):
        for b in range(nb):
            pf[t * 2 * nb + b, b * tc + t] = 1.0
            pb[t * 2 * nb + nb + b, b * tc + (tc - 1 - t)] = 1.0
    return pf, pb


def _s5_store_unpacked(y2, yf_ref, yb_ref, t_f, t_b, nb, tc):
    rs = 2 * nb
    for t in range(tc):
        yf_ref[:, t_f + t, :] = y2[t * rs:t * rs + nb]
        yb_ref[:, t_b + tc - 1 - t, :] = y2[t * rs + nb:(t + 1) * rs]


def _s5_kernel(*refs, nb, tc, tiles_per_pass):
    use_perm = nb % 8 != 0
    uf1_ref, ub1_ref, uf2_ref, ub2_ref, uf0_ref, ub0_ref = refs[:6]
    pf_ref, pb_ref = refs[6:8] if use_perm else (None, None)
    wbu_ref, wc_ref, are_ref, aim_ref, h0_ref, yf_ref, yb_ref, hfin_ref, bua_ref, bub_ref, hb_ref, st_ref = refs[-12:]
    i = pl.program_id(0)
    rows = nb * tc
    rs = 2 * nb

    def packed_inputs(uf_ref, ub_ref):
        if not use_perm:
            return tuple(jnp.concatenate([r[:, t, :] for t in range(tc)], axis=0).astype(BF16) for r in (uf_ref, ub_ref))
        uf = uf_ref[...].reshape(rows, S5_WIDTH).astype(BF16)
        ub = ub_ref[...].reshape(rows, S5_WIDTH).astype(BF16)
        u2f = jnp.dot(pf_ref[...], uf, preferred_element_type=F32).astype(BF16)
        u2b = jnp.dot(pb_ref[...], ub, preferred_element_type=F32).astype(BF16)
        return [jnp.concatenate([u2f[:, s * LANES:(s + 1) * LANES], u2b[:, s * LANES:(s + 1) * LANES]], axis=1)
                for s in range(S5_SLABS)]

    def bu_slab(lhs, s, dst_ref):
        cols = slice(s * 1024, (s + 1) * 1024)
        if use_perm:
            dst_ref[:, cols] = jnp.dot(lhs[s], wbu_ref[s], preferred_element_type=F32)
            return
        uf, ub = lhs
        bf = jnp.dot(uf[:, s * LANES:(s + 1) * LANES], wbu_ref[s, :LANES, :], preferred_element_type=F32)
        bb = jnp.dot(ub[:, s * LANES:(s + 1) * LANES], wbu_ref[s, LANES:, :], preferred_element_type=F32)
        for t in range(tc):
            dst_ref[t * rs:t * rs + nb, cols] = bf[t * nb:(t + 1) * nb]
            dst_ref[t * rs + nb:(t + 1) * rs, cols] = bb[(tc - 1 - t) * nb:(tc - t) * nb]

    @pl.when(i == 0)
    def _():
        st_ref[...] = h0_ref[...]
        lhs0 = packed_inputs(uf0_ref, ub0_ref)
        for s in range(S5_SLABS):
            bu_slab(lhs0, s, bua_ref)

    half_step = functools.partial(_s5_half_step, wc_ref=wc_ref, are_ref=are_ref, aim_ref=aim_ref, hb_ref=hb_ref,
                                  st_ref=st_ref, bu_slab=bu_slab, rs=rs, tc=tc, tiles_per_pass=tiles_per_pass)
    y2, tail = half_step(bua_ref, bub_ref, packed_inputs(uf1_ref, ub1_ref))
    _s5_store_unpacked(y2, yf_ref, yb_ref, 0, tc, nb, tc)
    y2, _ = half_step(bub_ref, bua_ref, packed_inputs(uf2_ref, ub2_ref), after=tail)
    _s5_store_unpacked(y2, yf_ref, yb_ref, tc, 0, nb, tc)
    hfin_ref[...] = st_ref[...]


def _s5_half_step(cur_ref, nxt_ref, lhs_next, *, wc_ref, are_ref, aim_ref, hb_ref, st_ref, bu_slab,
                  rs, tc, tiles_per_pass, after=None):
    grp = max(1, 16 // rs)
    ys = []
    for s in range(S5_SLABS):
        bu_slab(lhs_next, s, nxt_ref)
        for c0 in range(4 * s, 4 * s + 4, tiles_per_pass):
            cts = list(range(c0, c0 + tiles_per_pass))
            cre = [(ct // 4) * 1024 + (ct % 4) * LANES for ct in cts]
            cim = [c + 512 for c in cre]
            a_re = [are_ref[:, ct * LANES:(ct + 1) * LANES] for ct in cts]
            a_im = [aim_ref[:, ct * LANES:(ct + 1) * LANES] for ct in cts]
            h_re = [st_ref[:, c:c + LANES] for c in cre]
            h_im = [st_ref[:, c:c + LANES] for c in cim]
            if after is not None and c0 == 0:
                h_re[0] = h_re[0] + _exact_zero_of(after)
            for t0 in range(0, tc, grp):
                out_re = [[] for _ in cts]
                out_im = [[] for _ in cts]
                for t in range(t0, t0 + grp):
                    r0 = t * rs
                    for j in range(len(cts)):
                        b_re = cur_ref[r0:r0 + rs, cre[j]:cre[j] + LANES]
                        b_im = cur_ref[r0:r0 + rs, cim[j]:cim[j] + LANES]
                        n_re = a_re[j] * h_re[j] - a_im[j] * h_im[j] + b_re
                        n_im = a_re[j] * h_im[j] + a_im[j] * h_re[j] + b_im
                        h_re[j], h_im[j] = n_re, n_im
                        out_re[j].append(n_re)
                        out_im[j].append(n_im)
                for j in range(len(cts)):
                    blk_re = out_re[j][0] if grp == 1 else jnp.concatenate(out_re[j], axis=0)
                    blk_im = out_im[j][0] if grp == 1 else jnp.concatenate(out_im[j], axis=0)
                    hb_ref[t0 * rs:(t0 + grp) * rs, cre[j]:cre[j] + LANES] = blk_re.astype(BF16)
                    hb_ref[t0 * rs:(t0 + grp) * rs, cim[j]:cim[j] + LANES] = blk_im.astype(BF16)
            for j in range(len(cts)):
                st_ref[:, cre[j]:cre[j] + LANES] = h_re[j]
                st_ref[:, cim[j]:cim[j] + LANES] = h_im[j]
        ys.append(jnp.dot(hb_ref[:, s * 1024:(s + 1) * 1024], wc_ref[s], preferred_element_type=F32))
    return jnp.concatenate(ys, axis=1), h_re[-1]


def _s5_call(u, l, w_bu, w_c, are, aim, h0, tc, tiles_per_pass):
    nb, length, _ = u.shape
    n = length // tc
    rows = nb * tc
    const = lambda shape: pl.BlockSpec(shape, lambda i: (0,) * len(shape))
    layer = lambda shape: pl.BlockSpec((None,) + shape, lambda i: (l,) + (0,) * len(shape))
    blk = lambda chunk: pl.BlockSpec((nb, tc, S5_WIDTH), lambda i: (0, chunk(i), 0))
    out_f = pl.BlockSpec((nb, 2 * tc, S5_WIDTH), lambda i: (0, i, 0))
    out_b = pl.BlockSpec((nb, 2 * tc, S5_WIDTH), lambda i: (0, n // 2 - 1 - i, 0))
    second = lambda i: jnp.minimum(2 * i + 2, n - 1)
    in_specs = [blk(lambda i: 2 * i + 1), blk(lambda i: n - 2 - 2 * i), blk(second), blk(lambda i: n - 1 - second(i)),
                blk(lambda i: 0), blk(lambda i: n - 1)]
    args = [u] * 6
    if nb % 8 != 0:
        pf, pb = _s5_perms(nb, tc)
        in_specs += [const((2 * rows, rows)), const((2 * rows, rows))]
        args += [jnp.asarray(pf, BF16), jnp.asarray(pb, BF16)]
    in_specs += [layer((S5_SLABS, 2 * LANES, 1024)), layer((S5_SLABS, 1024, LANES)),
                 layer((2 * nb, S5_NC)), layer((2 * nb, S5_NC)), const((2 * nb, 2 * S5_NC))]
    args += [w_bu, w_c, are, aim, h0]
    return pl.pallas_call(
        functools.partial(_s5_kernel, nb=nb, tc=tc, tiles_per_pass=tiles_per_pass),
        grid=(n // 2,),
        in_specs=in_specs,
        out_specs=(out_f, out_b, const((2 * nb, 2 * S5_NC))),
        out_shape=(jax.ShapeDtypeStruct(u.shape, F32), jax.ShapeDtypeStruct(u.shape, F32),
                   jax.ShapeDtypeStruct((2 * nb, 2 * S5_NC), F32)),
        scratch_shapes=[pltpu.VMEM((2 * rows, 2 * S5_NC), F32), pltpu.VMEM((2 * rows, 2 * S5_NC), F32),
                        pltpu.VMEM((2 * rows, 2 * S5_NC), BF16), pltpu.VMEM((2 * nb, 2 * S5_NC), F32)],
        compiler_params=pltpu.CompilerParams(dimension_semantics=("arbitrary",), vmem_limit_bytes=VMEM_LIMIT),
        name="s5_scan",
    )(*args)


def _gla_kernel(*refs, length, has_s0, nsq):
    q_ref, k_ref, v_ref, gl_ref, wg_ref, bg_ref, ng_ref = refs[:7]
    s0_ref = refs[7] if has_s0 else None
    o_ref, sfin_ref, sf_ref, sb_ref = refs[-4:]
    c = GLA_CHUNK
    n = length // c
    ri = lax.broadcasted_iota(jnp.int32, (c, c), 0)
    ci = lax.broadcasted_iota(jnp.int32, (c, c), 1)
    lane = lax.broadcasted_iota(jnp.int32, (1, GLA_KEY), 1)
    head_mask = [(lane >= h * GLA_DK) & (lane < (h + 1) * GLA_DK) for h in range(GLA_HEADS)]
    ng = ng_ref[...]
    causal = [ci <= ri, ci >= ri]
    tri = [m.astype(BF16) for m in causal]
    end = (c - 1, 0)
    chains = [(sq, d) for sq in range(nsq) for d in (0, 1)]
    key_rows = lambda h: slice(h * GLA_DK, (h + 1) * GLA_DK)
    val_cols = lambda h: slice(h * GLA_DV, (h + 1) * GLA_DV)

    sb_ref[...] = jnp.zeros_like(sb_ref)
    for sq, d in chains:
        for h in range(GLA_HEADS):
            if has_s0:
                sf_ref[sq, d, h] = s0_ref[sq, d, h]
                sb_ref[sq, d, key_rows(h), val_cols(h)] = s0_ref[sq, d, h].astype(BF16)
            else:
                sf_ref[sq, d, h] = jnp.zeros((GLA_DK, GLA_DV), F32)

    def chunk_group(r):
        qc = [q_ref[sq, pl.ds(r[d], c), :] * (GLA_DK ** -0.5) for sq, d in chains]
        kc = [k_ref[sq, pl.ds(r[d], c), :] for sq, d in chains]
        vc = [v_ref[sq, pl.ds(r[d], c), :] for sq, d in chains]
        logits = [_bdot(gl_ref[sq, pl.ds(r[d], c), :], wg_ref[d]) + bg_ref[d] for sq, d in chains]
        g = [_split2(_log_sigmoid(x) * (1.0 / GLA_GATE_NORM)) for x in logits]
        b = [jnp.dot(tri[d], hi, preferred_element_type=F32) + jnp.dot(tri[d], lo, preferred_element_type=F32)
             for (sq, d), (hi, lo) in zip(chains, g)]
        bt = [x.T for x in b]
        kt = [x.T for x in kc]
        b_mid = [x[c // 2:c // 2 + 1, :] for x in b]
        bt_mid = [x[:, c // 2:c // 2 + 1] for x in bt]
        bt_end = [x[:, end[d]:end[d] + 1] for (sq, d), x in zip(chains, bt)]
        q_in = [(x * jnp.exp(y)).astype(BF16) for x, y in zip(qc, b)]
        q_mid = [(x * jnp.exp(y - m)).astype(BF16) for x, y, m in zip(qc, b, b_mid)]
        k_mid_t = [(x * jnp.exp(m - y)).astype(BF16) for x, y, m in zip(kt, bt, bt_mid)]
        k_end_t = [(x * jnp.exp(e - y)).astype(BF16) for x, y, e in zip(kt, bt, bt_end)]
        q_st = [jnp.concatenate([jnp.where(head_mask[h], x, jnp.zeros_like(x)) for h in range(GLA_HEADS)], axis=0)
                for x in q_mid]
        a = [jnp.dot(x, y, preferred_element_type=F32) for x, y in zip(q_st, k_mid_t)]
        o_inter = [jnp.dot(x, sb_ref[sq, d], preferred_element_type=F32) for (sq, d), x in zip(chains, q_in)]
        parts = [[] for _ in chains]
        for h in range(GLA_HEADS):
            for i, (sq, d) in enumerate(chains):
                a_h = jnp.where(causal[d], a[i][h * c:(h + 1) * c, :], 0.0).astype(BF16)
                parts[i].append(jnp.dot(a_h, vc[i][:, val_cols(h)], preferred_element_type=F32))
        for h in range(GLA_HEADS):
            for i, (sq, d) in enumerate(chains):
                upd = jnp.dot(k_end_t[i][key_rows(h), :], vc[i][:, val_cols(h)], preferred_element_type=F32)
                s_new = jnp.exp(bt_end[i][key_rows(h), :]) * sf_ref[sq, d, h] + upd
                sf_ref[sq, d, h] = s_new
                sb_ref[sq, d, key_rows(h), val_cols(h)] = s_new.astype(BF16)
        return [x + jnp.concatenate(p, axis=1) for x, p in zip(o_inter, parts)]

    def finish(o, sq, r0):
        o = o + o_ref[sq, pl.ds(r0, c), :]
        normed = []
        for h in range(GLA_HEADS):
            oh = o[:, val_cols(h)]
            ms = jnp.mean(oh * oh, axis=-1, keepdims=True)
            normed.append(oh * lax.rsqrt(ms + EPS))
        o_ref[sq, pl.ds(r0, c), :] = jnp.concatenate(normed, axis=1) * ng

    def first_half(step, carry):
        r = (pl.multiple_of(step * c, c), pl.multiple_of((n - 1 - step) * c, c))
        for (sq, d), o in zip(chains, chunk_group(r)):
            o_ref[sq, pl.ds(r[d], c), :] = o
        return carry

    def second_half(step, carry):
        r = (pl.multiple_of(step * c, c), pl.multiple_of((n - 1 - step) * c, c))
        for (sq, d), o in zip(chains, chunk_group(r)):
            finish(o, sq, r[d])
        return carry

    lax.fori_loop(0, n // 2, first_half, 0)
    lax.fori_loop(n // 2, n, second_half, 0)
    for sq, d in chains:
        for h in range(GLA_HEADS):
            sfin_ref[sq, d, h] = sf_ref[sq, d, h]


def _gla_call(q, k, v, gl, l, wg, bg, ng, s0, nsq=2):
    nseq, length, _ = q.shape
    seq = lambda width: pl.BlockSpec((nsq, length, width), lambda b: (b, 0, 0))
    layer = lambda shape: pl.BlockSpec((None,) + shape, lambda b: (l,) + (0,) * len(shape))
    in_specs = [seq(GLA_KEY), seq(GLA_KEY), seq(GLA_WIDTH), seq(LANES),
                layer((2, LANES, GLA_KEY)), layer((2, 1, GLA_KEY)), layer((1, GLA_WIDTH))]
    args = [q, k, v, gl, wg, bg, ng]
    if s0 is not None:
        in_specs.append(pl.BlockSpec((nsq, None, 2, GLA_HEADS, GLA_DK, GLA_DV), lambda b: (b, l, 0, 0, 0, 0)))
        args.append(s0)
    return pl.pallas_call(
        functools.partial(_gla_kernel, length=length, has_s0=s0 is not None, nsq=nsq),
        grid=(nseq // nsq,),
        in_specs=in_specs,
        out_specs=(seq(GLA_WIDTH), pl.BlockSpec((nsq, 2, GLA_HEADS, GLA_DK, GLA_DV), lambda b: (b, 0, 0, 0, 0))),
        out_shape=(jax.ShapeDtypeStruct((nseq, length, GLA_WIDTH), F32),
                   jax.ShapeDtypeStruct((nseq, 2, GLA_HEADS, GLA_DK, GLA_DV), F32)),
        scratch_shapes=[pltpu.VMEM((nsq, 2, GLA_HEADS, GLA_DK, GLA_DV), F32),
                        pltpu.VMEM((nsq, 2, GLA_KEY, GLA_WIDTH), BF16)],
        compiler_params=pltpu.CompilerParams(dimension_semantics=("arbitrary",), vmem_limit_bytes=VMEM_LIMIT),
        name="gla",
    )(*args)


def _mix_kernel(*refs, has_pe, last, row_groups):
    if has_pe:
        (x_ref, pe_ref, yf_ref, yb_ref, u_ref, o_ref, shift_ref, scale_ref, gate_ref, ng_ref, wg_ref, d_ref,
         wglu_ref, bglu_ref, wpa_ref, wpb_ref, wo_ref, fng_ref, out_ref) = refs
    else:
        (x_ref, yf_ref, yb_ref, u_ref, o_ref, shift_ref, scale_ref, gate_ref, ng_ref, wg_ref, d_ref,
         wglu_ref, bglu_ref, wpa_ref, wpb_ref, wo_ref, fng_ref, out_ref) = refs
    nsb, tm, _ = x_ref.shape
    grp = [(i, slice(j * tm // row_groups, (j + 1) * tm // row_groups)) for i in range(nsb) for j in range(row_groups)]
    x = [x_ref[i, r, :] + pe_ref[r, :] if has_pe else x_ref[i, r, :] for i, r in grp]
    h = [_modulated_norm(v, ng_ref[...], 1.0 + scale_ref[...], shift_ref[...]).astype(BF16) for v in x]
    gts = [_bdot_t(v, wg_ref[...]) for v in h]
    y = [_gelu_tanh(yf_ref[i, r, :] + yb_ref[i, r, :] + d_ref[...] * u_ref[i, r, :]) for i, r in grp]
    glu = [_bdot(v, wglu_ref[...]) for v in y]
    y = [v * _sigmoid(g + bglu_ref[...]) for v, g in zip(y, glu)]
    y_a = [v * _silu(g[:, :S5_WIDTH]) for v, g in zip(y, gts)]
    y_b = [o_ref[i, r, :] * _silu(g[:, S5_WIDTH:2 * S5_WIDTH]) for (i, r), g in zip(grp, gts)]
    p_a = [_bdot(v, wpa_ref[...]) for v in y_a]
    p_b = [_bdot(v, wpb_ref[...]) for v in y_b]
    merged = [_sigmoid(g[:, 2 * S5_WIDTH:2 * S5_WIDTH + D_MODEL]) * a + _sigmoid(g[:, 2 * S5_WIDTH + D_MODEL:]) * b
              for g, a, b in zip(gts, p_a, p_b)]
    xn = [v + gate_ref[...] * _bdot(m, wo_ref[...]) for v, m in zip(x, merged)]
    for (i, r), v in zip(grp, xn):
        if last:
            ms = jnp.mean(v * v, axis=-1, keepdims=True)
            v = v * lax.rsqrt(ms + EPS) * fng_ref[...]
        out_ref[i, r, :] = v


def _mix_call(x, pe, yf, yb, u, o, mod, l, cond_row, ng, wg, s5_d, wglu, bglu, wpa, wpb, wo, fng, tm, last, nsb=1):
    nseq, length, _ = x.shape
    row = lambda width: pl.BlockSpec((nsb, tm, width), lambda b, j: (b, j, 0))
    in_specs = [row(D_MODEL)]
    args = [x]
    if pe is not None:
        in_specs.append(pl.BlockSpec((tm, D_MODEL), lambda b, j: (j, 0)))
        args.append(pe)
    in_specs += [row(S5_WIDTH), row(S5_WIDTH), row(S5_WIDTH), row(GLA_WIDTH),
                 _mod_spec(l, cond_row, 0), _mod_spec(l, cond_row, 1), _mod_spec(l, cond_row, 2),
                 _layer_spec(l, (1, D_MODEL)), _layer_spec(l, (GATE_W, D_MODEL)), _layer_spec(l, (1, S5_WIDTH)),
                 _layer_spec(l, (S5_WIDTH, S5_WIDTH)), _layer_spec(l, (1, S5_WIDTH)),
                 _layer_spec(l, (S5_WIDTH, D_MODEL)), _layer_spec(l, (GLA_WIDTH, D_MODEL)),
                 _layer_spec(l, (D_MODEL, D_MODEL)), pl.BlockSpec((1, D_MODEL), lambda b, j: (0, 0))]
    args += [yf, yb, u, o, mod, mod, mod, ng, wg, s5_d, wglu, bglu, wpa, wpb, wo, fng]
    return pl.pallas_call(
        functools.partial(_mix_kernel, has_pe=pe is not None, last=last, row_groups=max(1, tm // 256)),
        grid=(nseq // nsb, length // tm),
        in_specs=in_specs,
        out_specs=row(D_MODEL),
        out_shape=jax.ShapeDtypeStruct(x.shape, F32),
        compiler_params=pltpu.CompilerParams(dimension_semantics=("arbitrary", "arbitrary"),
                                             vmem_limit_bytes=VMEM_LIMIT),
        name="mix",
    )(*args)


def _grid_pos_embed(length, dim):
    rows = length // GRID_W
    quarter = dim // 4
    freqs = jnp.exp(-math.log(10000.0) * jnp.arange(quarter, dtype=F32) / quarter)

    def sincos(pos):
        ang = pos.astype(F32)[:, None] * freqs[None, :]
        return jnp.concatenate([jnp.sin(ang), jnp.cos(ang)], axis=-1)

    er = sincos(jnp.arange(rows))
    ec = sincos(jnp.arange(GRID_W))
    pe = jnp.concatenate([jnp.broadcast_to(er[:, None, :], (rows, GRID_W, dim // 2)),
                          jnp.broadcast_to(ec[None, :, :], (rows, GRID_W, dim // 2))], axis=-1)
    return pe.reshape(rows * GRID_W, dim)


def kernel(x_prompt, x_sample, c, state_s5_re, state_s5_im, state_gla, c_ctx, norm_g, w_mod, b_mod, w_in, gla_wg_up,
           gla_bg, gla_norm_g, s5_lam_re, s5_lam_im, s5_log_dt, s5_b_re, s5_b_im, s5_c_re, s5_c_im, s5_d, w_glu,
           b_glu, w_pa, w_pb, w_o, final_norm_g):
    depth = w_in.shape[0]
    bp, lp, _ = x_prompt.shape
    bs, ls, _ = x_sample.shape

    cond8 = jnp.zeros((8, D_MODEL), F32).at[0].set(c_ctx).at[1:1 + bs].set(c)
    mod = _mod_call(cond8, w_mod, b_mod)

    a_re, a_im, w_bu, w_c = _s5_params(s5_lam_re, s5_lam_im, s5_log_dt, s5_b_re, s5_b_im, s5_c_re, s5_c_im)

    w_proj, w_gate = _split_w_in(w_in)
    wg_up = jnp.concatenate([gla_wg_up, jnp.zeros((depth, 2, LANES - GLA_RANK, GLA_KEY), F32)], axis=2).astype(BF16)
    w_glu_b, w_pa_b, w_pb_b, w_o_b = (w.astype(BF16) for w in (w_glu, w_pa, w_pb, w_o))

    vec = lambda a: a[:, None, :]
    mod = mod.reshape(depth, 8, 1, 3 * D_MODEL)
    ng, gng, bg = vec(norm_g), vec(gla_norm_g), gla_bg[:, :, None, :]
    common = (ng, w_gate, vec(s5_d), w_glu_b, vec(b_glu), w_pa_b, w_pb_b, w_o_b, final_norm_g[None])
    are_p, aim_p = jnp.repeat(a_re, bp, axis=1), jnp.repeat(a_im, bp, axis=1)
    are_s, aim_s = jnp.repeat(a_re, bs, axis=1), jnp.repeat(a_im, bs, axis=1)
    ctx_row = lambda b: 0
    cond_row = lambda b: b + 1

    pe = _grid_pos_embed(ls, D_MODEL)
    zero_h0 = jnp.zeros((2 * bp, 2 * S5_NC), F32)
    xp, xs = x_prompt, x_sample
    new_re, new_im, new_gla = [], [], []
    for l in range(depth):
        last = l == depth - 1

        u, q, k, v, gl = _proj_call(xp, None, mod, l, ctx_row, ng, w_proj, tm=lp, nsb=4)
        yf, yb, hfin = _s5_call(u, l, w_bu, w_c, are_p, aim_p, zero_h0, tc=16, tiles_per_pass=2)
        o, sfin = _gla_call(q, k, v, gl, l, wg_up, bg, gng, None)
        new_gla.append(sfin)
        xp = _mix_call(xp, None, yf, yb, u, o, mod, l, ctx_row, *common, tm=lp, last=last, nsb=2)
        re_l, im_l = _s5_cols_to_state(hfin.reshape(2, bp, 2 * S5_NC))
        new_re.append(jnp.swapaxes(re_l, 0, 1))
        new_im.append(jnp.swapaxes(im_l, 0, 1))

        pe_l = pe if l == 0 else None
        u, q, k, v, gl = _proj_call(xs, pe_l, mod, l, cond_row, ng, w_proj, tm=1024)
        h0 = _s5_state_to_cols(jnp.swapaxes(state_s5_re[:, l], 0, 1), jnp.swapaxes(state_s5_im[:, l], 0, 1))
        yf, yb, _ = _s5_call(u, l, w_bu, w_c, are_s, aim_s, h0.reshape(2 * bs, 2 * S5_NC), tc=64, tiles_per_pass=4)
        o, _ = _gla_call(q, k, v, gl, l, wg_up, bg, gng, state_gla)
        xs = _mix_call(xs, pe_l, yf, yb, u, o, mod, l, cond_row, *common, tm=512, last=last)

    return (xp, xs, jnp.stack(new_re, axis=1), jnp.stack(new_im, axis=1), jnp.stack(new_gla, axis=1))
```

```python
import functools
import math

import numpy as np
import jax
import jax.numpy as jnp
from jax import lax
from jax.experimental import pallas as pl
from jax.experimental.pallas import tpu as pltpu

F32 = jnp.float32
BF16 = jnp.bfloat16

D_MODEL = 1024
GRID_W = 64
S5_WIDTH = 512
S5_GROUPS = 32
S5_CH = 16
S5_STATE = 64
S5_NC = S5_GROUPS * S5_STATE
S5_SLABS = 4
GLA_HEADS = 4
GLA_DK = 64
GLA_DV = 128
GLA_KEY = GLA_HEADS * GLA_DK
GLA_WIDTH = GLA_HEADS * GLA_DV
GLA_RANK = 16
GLA_GATE_NORM = 16.0
GLA_CHUNK = 128
EPS = 1e-6
LANES = 128
PROJ_W = S5_WIDTH + 2 * GLA_KEY + GLA_WIDTH + LANES
GATE_W = 2 * S5_WIDTH + 2 * D_MODEL
VMEM_LIMIT = 52 * 1024 * 1024


def _bdot(a, b):
    return jnp.dot(a.astype(BF16), b.astype(BF16), preferred_element_type=F32)


def _bdot_t(a, bt):
    return lax.dot_general(a.astype(BF16), bt.astype(BF16), (((1,), (1,)), ((), ())), preferred_element_type=F32)


def _split2(x):
    hi = x.astype(BF16)
    lo = (x - hi.astype(F32)).astype(BF16)
    return hi, lo


def _sigmoid(x):
    return 1.0 / (1.0 + jnp.exp(-x))


def _silu(x):
    return x * _sigmoid(x)


def _gelu_tanh(x):
    c = math.sqrt(2.0 / math.pi)
    return 0.5 * x * (1.0 + jnp.tanh(c * (x + 0.044715 * (x * x * x))))


def _exact_zero_of(x):
    bits = lax.bitcast_convert_type(x, jnp.int32)
    return lax.shift_right_logical(lax.shift_right_logical(bits, 16), 16).astype(F32)


def _log_sigmoid(x):
    return jnp.minimum(x, 0.0) - jnp.log(1.0 + jnp.exp(-jnp.abs(x)))


def _modulated_norm(x, ng, scale1p, shift):
    ms = jnp.mean(x * x, axis=-1, keepdims=True)
    return (x * lax.rsqrt(ms + EPS) * ng) * scale1p + shift


def _mod_kernel(cond_ref, w_ref, b_ref, out_ref):
    c = cond_ref[...]
    s_hi, s_lo = _split2(_silu(c))
    w_hi, w_lo = _split2(w_ref[0])
    acc = jnp.dot(s_hi, w_hi, preferred_element_type=F32)
    acc += jnp.dot(s_lo, w_hi, preferred_element_type=F32)
    acc += jnp.dot(s_hi, w_lo, preferred_element_type=F32)
    out_ref[0] = acc + b_ref[0]


def _mod_call(cond8, w_mod, b_mod):
    depth = w_mod.shape[0]
    nb = 1536
    return pl.pallas_call(
        _mod_kernel,
        grid=(depth, 3 * D_MODEL // nb),
        in_specs=[
            pl.BlockSpec((8, D_MODEL), lambda l, j: (0, 0)),
            pl.BlockSpec((1, D_MODEL, nb), lambda l, j: (l, 0, j)),
            pl.BlockSpec((1, 1, nb), lambda l, j: (l, 0, j)),
        ],
        out_specs=pl.BlockSpec((1, 8, nb), lambda l, j: (l, 0, j)),
        out_shape=jax.ShapeDtypeStruct((depth, 8, 3 * D_MODEL), F32),
        compiler_params=pltpu.CompilerParams(dimension_semantics=("arbitrary", "arbitrary")),
        name="adaln_mod",
    )(cond8, w_mod, b_mod.reshape(depth, 1, 3 * D_MODEL))


def _zoh(lre, lim, ldt):
    dt = jnp.exp(ldt)
    mag = jnp.exp(lre * dt)
    a_re = mag * jnp.cos(lim * dt)
    a_im = mag * jnp.sin(lim * dt)
    n_re = a_re - 1.0
    inv = 1.0 / (lre * lre + lim * lim)
    return a_re, a_im, (n_re * lre + a_im * lim) * inv, (a_im * lre - n_re * lim) * inv


def _zoh_kernel(lre_ref, lim_ref, ldt_ref, lre_x_ref, lim_x_ref, ldt_x_ref, bre_ref, bim_ref, cre_ref, cim_ref,
                are_ref, aim_ref, wbu_ref, wc_ref, *, depth):
    a_re, a_im, _, _ = _zoh(lre_ref[...], lim_ref[...], ldt_ref[...])
    are_ref[...] = a_re
    aim_ref[...] = a_im
    _, _, c_re, c_im = _zoh(lre_x_ref[...], lim_x_ref[...], ldt_x_ref[...])
    bre = bre_ref[...]
    bim = bim_ref[...]
    bbar = [c_re * bre - c_im * bim, c_re * bim + c_im * bre]
    cmat = [cre_ref[...], -cim_ref[...]]
    wbu_ref[...] = jnp.zeros_like(wbu_ref)
    wc_ref[...] = jnp.zeros_like(wc_ref)
    for l in range(depth):
        for s in range(S5_SLABS):
            for gl in range(8):
                g = 8 * s + gl
                for ri in range(2):
                    col = ri * 8 * S5_STATE + gl * S5_STATE
                    for d in range(2):
                        src = ((l * 2 + d) * S5_GROUPS + g) * S5_CH
                        row = (d * 8 + gl) * S5_CH
                        wbu_ref[l, s, row:row + S5_CH, col:col + S5_STATE] = bbar[ri][src:src + S5_CH, :].astype(BF16)
                    src = (l * S5_GROUPS + g) * S5_STATE
                    wc_ref[l, s, col:col + S5_STATE, gl * S5_CH:(gl + 1) * S5_CH] = (
                        cmat[ri][src:src + S5_STATE, :].astype(BF16))


def _s5_params(s5_lam_re, s5_lam_im, s5_log_dt, s5_b_re, s5_b_im, s5_c_re, s5_c_im):
    depth = s5_lam_re.shape[0]
    full = (depth, 2, S5_GROUPS, S5_CH, S5_STATE)
    expand = lambda a: jnp.broadcast_to(a, full).reshape(-1, S5_STATE)
    small = lambda a: a.reshape(depth * 2 * S5_GROUPS, -1)
    ct = lambda a: jnp.swapaxes(a, -1, -2).reshape(-1, S5_CH)
    a_re, a_im, w_bu, w_c = pl.pallas_call(
        functools.partial(_zoh_kernel, depth=depth),
        out_shape=(jax.ShapeDtypeStruct((depth * 2 * S5_GROUPS, S5_STATE), F32),
                   jax.ShapeDtypeStruct((depth * 2 * S5_GROUPS, S5_STATE), F32),
                   jax.ShapeDtypeStruct((depth, S5_SLABS, 2 * LANES, 2 * 8 * S5_STATE), BF16),
                   jax.ShapeDtypeStruct((depth, S5_SLABS, 2 * 8 * S5_STATE, LANES), BF16)),
        name="s5_zoh",
    )(small(s5_lam_re), small(s5_lam_im), small(s5_log_dt),
      expand(s5_lam_re[:, :, :, None, :]), expand(s5_lam_im[:, :, :, None, :]), expand(s5_log_dt[:, :, :, None, None]),
      expand(jnp.swapaxes(s5_b_re, -1, -2)[:, None]), expand(jnp.swapaxes(s5_b_im, -1, -2)[:, None]),
      ct(s5_c_re), ct(s5_c_im))
    return a_re.reshape(depth, 2, S5_NC), a_im.reshape(depth, 2, S5_NC), w_bu, w_c


def _s5_state_to_cols(re, im):
    lead = re.shape[:-2]
    st = jnp.stack([re.reshape(lead + (S5_SLABS, 512)), im.reshape(lead + (S5_SLABS, 512))], axis=-2)
    return st.reshape(lead + (2 * S5_NC,))


def _s5_cols_to_state(cols):
    lead = cols.shape[:-1]
    st = cols.reshape(lead + (S5_SLABS, 2, 512))
    re = st[..., 0, :].reshape(lead + (S5_GROUPS, S5_STATE))
    im = st[..., 1, :].reshape(lead + (S5_GROUPS, S5_STATE))
    return re, im


_IN_OFFS = [int(o) for o in np.cumsum([0, S5_WIDTH, S5_WIDTH, GLA_KEY, GLA_KEY, GLA_WIDTH, GLA_WIDTH, GLA_RANK,
                                       D_MODEL, D_MODEL])]


def _split_w_in_kernel(u_ref, qkv_ref, code_ref, ga_ref, gb_ref, m_ref, proj_ref, gate_ref):
    rows = 0
    for src in (u_ref, qkv_ref, code_ref):
        proj_ref[rows:rows + src.shape[1], :] = src[0].astype(BF16)
        rows += src.shape[1]
    proj_ref[rows:, :] = jnp.zeros((PROJ_W - rows, proj_ref.shape[1]), BF16)
    rows = 0
    for src in (ga_ref, gb_ref, m_ref):
        gate_ref[rows:rows + src.shape[1], :] = src[0].astype(BF16)
        rows += src.shape[1]


def _split_w_in(w_in):
    o = _IN_OFFS
    depth = w_in.shape[0]
    wt = jnp.swapaxes(w_in, 1, 2)
    tn = D_MODEL // 2
    rows = lambda lo, hi: pl.BlockSpec((pl.Element(1), pl.Element(hi - lo), pl.Element(tn)),
                                       lambda l, j: (l, lo, j * tn))
    out = lambda n: pl.BlockSpec((None, n, tn), lambda l, j: (l, 0, j))
    return pl.pallas_call(
        _split_w_in_kernel,
        grid=(depth, D_MODEL // tn),
        in_specs=[rows(o[0], o[1]), rows(o[2], o[5]), rows(o[6], o[7]), rows(o[1], o[2]), rows(o[5], o[6]),
                  rows(o[7], o[9])],
        out_specs=(out(PROJ_W), out(GATE_W)),
        out_shape=(jax.ShapeDtypeStruct((depth, PROJ_W, D_MODEL), BF16),
                   jax.ShapeDtypeStruct((depth, GATE_W, D_MODEL), BF16)),
        compiler_params=pltpu.CompilerParams(dimension_semantics=("arbitrary", "arbitrary"),
                                             vmem_limit_bytes=VMEM_LIMIT),
        name="split_w_in",
    )(wt, wt, wt, wt, wt, wt)


def _proj_kernel(*refs, has_pe):
    if has_pe:
        x_ref, pe_ref, shift_ref, scale_ref, ng_ref, w_ref, u_ref, q_ref, k_ref, v_ref, gl_ref = refs
    else:
        x_ref, shift_ref, scale_ref, ng_ref, w_ref, u_ref, q_ref, k_ref, v_ref, gl_ref = refs
    nsb, tm, _ = x_ref.shape
    rg = max(1, tm // 512)
    grp = [(i, slice(j * tm // rg, (j + 1) * tm // rg)) for i in range(nsb) for j in range(rg)]
    x = [x_ref[i, r, :] + pe_ref[r, :] if has_pe else x_ref[i, r, :] for i, r in grp]
    h = [_modulated_norm(v, ng_ref[...], 1.0 + scale_ref[...], shift_ref[...]) for v in x]
    p = [_bdot_t(v, w_ref[...]) for v in h]
    for (i, r), v in zip(grp, p):
        o = 0
        for ref, width in ((u_ref, S5_WIDTH), (q_ref, GLA_KEY), (k_ref, GLA_KEY), (v_ref, GLA_WIDTH), (gl_ref, LANES)):
            ref[i, r, :] = v[:, o:o + width].astype(ref.dtype)
            o += width


def _mod_spec(l, cond_row, part):
    return pl.BlockSpec((None, None, 1, D_MODEL), lambda b, j: (l, cond_row(b), 0, part))


def _layer_spec(l, shape):
    return pl.BlockSpec((None,) + shape, lambda b, j: (l,) + (0,) * len(shape))


def _proj_call(x, pe, mod, l, cond_row, ng, w, tm, nsb=1):
    nseq, length, _ = x.shape
    row = lambda width: pl.BlockSpec((nsb, tm, width), lambda b, j: (b, j, 0))
    in_specs = [row(D_MODEL)]
    args = [x]
    if pe is not None:
        in_specs.append(pl.BlockSpec((tm, D_MODEL), lambda b, j: (j, 0)))
        args.append(pe)
    in_specs += [_mod_spec(l, cond_row, 0), _mod_spec(l, cond_row, 1), _layer_spec(l, (1, D_MODEL)),
                 _layer_spec(l, (PROJ_W, D_MODEL))]
    args += [mod, mod, ng, w]
    sds = lambda width, dt: jax.ShapeDtypeStruct((nseq, length, width), dt)
    return pl.pallas_call(
        functools.partial(_proj_kernel, has_pe=pe is not None),
        grid=(nseq // nsb, length // tm),
        in_specs=in_specs,
        out_specs=(row(S5_WIDTH), row(GLA_KEY), row(GLA_KEY), row(GLA_WIDTH), row(LANES)),
        out_shape=(sds(S5_WIDTH, F32), sds(GLA_KEY, F32), sds(GLA_KEY, F32), sds(GLA_WIDTH, BF16), sds(LANES, F32)),
        compiler_params=pltpu.CompilerParams(dimension_semantics=("arbitrary", "arbitrary"),
                                             vmem_limit_bytes=VMEM_LIMIT),
        name="proj",
    )(*args)


def _s5_perms(nb, tc):
    r = nb * tc
    pf = np.zeros((2 * r, r), np.float32)
    pb = np.zeros((2 * r, r), np.float32)
    for t in range(tc):
        for b in range(nb):
            pf[t * 2 * nb + b, b * tc + t] = 1.0
            pb[t * 2 * nb + nb + b, b * tc + (tc - 1 - t)] = 1.0
    return pf, pb


def _s5_store_unpacked(y2, yf_ref, yb_ref, t_f, t_b, nb, tc):
    rs = 2 * nb
    for t in range(tc):
        yf_ref[:, t_f + t, :] = y2[t * rs:t * rs + nb]
        yb_ref[:, t_b + tc - 1 - t, :] = y2[t * rs + nb:(t + 1) * rs]


def _s5_kernel(*refs, nb, tc, tiles_per_pass):
    use_perm = nb % 8 != 0
    uf1_ref, ub1_ref, uf2_ref, ub2_ref, uf0_ref, ub0_ref = refs[:6]
    pf_ref, pb_ref = refs[6:8] if use_perm else (None, None)
    wbu_ref, wc_ref, are_ref, aim_ref, h0_ref, yf_ref, yb_ref, hfin_ref, bua_ref, bub_ref, hb_ref, st_ref = refs[-12:]
    i = pl.program_id(0)
    rows = nb * tc
    rs = 2 * nb

    def packed_inputs(uf_ref, ub_ref):
        if not use_perm:
            return tuple(jnp.concatenate([r[:, t, :] for t in range(tc)], axis=0).astype(BF16) for r in (uf_ref, ub_ref))
        uf = uf_ref[...].reshape(rows, S5_WIDTH).astype(BF16)
        ub = ub_ref[...].reshape(rows, S5_WIDTH).astype(BF16)
        u2f = jnp.dot(pf_ref[...], uf, preferred_element_type=F32).astype(BF16)
        u2b = jnp.dot(pb_ref[...], ub, preferred_element_type=F32).astype(BF16)
        return [jnp.concatenate([u2f[:, s * LANES:(s + 1) * LANES], u2b[:, s * LANES:(s + 1) * LANES]], axis=1)
                for s in range(S5_SLABS)]

    def bu_slab(lhs, s, dst_ref):
        cols = slice(s * 1024, (s + 1) * 1024)
        if use_perm:
            dst_ref[:, cols] = jnp.dot(lhs[s], wbu_ref[s], preferred_element_type=F32)
            return
        uf, ub = lhs
        bf = jnp.dot(uf[:, s * LANES:(s + 1) * LANES], wbu_ref[s, :LANES, :], preferred_element_type=F32)
        bb = jnp.dot(ub[:, s * LANES:(s + 1) * LANES], wbu_ref[s, LANES:, :], preferred_element_type=F32)
        for t in range(tc):
            dst_ref[t * rs:t * rs + nb, cols] = bf[t * nb:(t + 1) * nb]
            dst_ref[t * rs + nb:(t + 1) * rs, cols] = bb[(tc - 1 - t) * nb:(tc - t) * nb]

    @pl.when(i == 0)
    def _():
        st_ref[...] = h0_ref[...]
        lhs0 = packed_inputs(uf0_ref, ub0_ref)
        for s in range(S5_SLABS):
            bu_slab(lhs0, s, bua_ref)

    half_step = functools.partial(_s5_half_step, wc_ref=wc_ref, are_ref=are_ref, aim_ref=aim_ref, hb_ref=hb_ref,
                                  st_ref=st_ref, bu_slab=bu_slab, rs=rs, tc=tc, tiles_per_pass=tiles_per_pass)
    y2, tail = half_step(bua_ref, bub_ref, packed_inputs(uf1_ref, ub1_ref))
    _s5_store_unpacked(y2, yf_ref, yb_ref, 0, tc, nb, tc)
    y2, _ = half_step(bub_ref, bua_ref, packed_inputs(uf2_ref, ub2_ref), after=tail)
    _s5_store_unpacked(y2, yf_ref, yb_ref, tc, 0, nb, tc)
    hfin_ref[...] = st_ref[...]


def _s5_half_step(cur_ref, nxt_ref, lhs_next, *, wc_ref, are_ref, aim_ref, hb_ref, st_ref, bu_slab,
                  rs, tc, tiles_per_pass, after=None):
    grp = max(1, 16 // rs)
    ys = []
    for s in range(S5_SLABS):
        bu_slab(lhs_next, s, nxt_ref)
        for c0 in range(4 * s, 4 * s + 4, tiles_per_pass):
            cts = list(range(c0, c0 + tiles_per_pass))
            cre = [(ct // 4) * 1024 + (ct % 4) * LANES for ct in cts]
            cim = [c + 512 for c in cre]
            a_re = [are_ref[:, ct * LANES:(ct + 1) * LANES] for ct in cts]
            a_im = [aim_ref[:, ct * LANES:(ct + 1) * LANES] for ct in cts]
            h_re = [st_ref[:, c:c + LANES] for c in cre]
            h_im = [st_ref[:, c:c + LANES] for c in cim]
            if after is not None and c0 == 0:
                h_re[0] = h_re[0] + _exact_zero_of(after)
            for t0 in range(0, tc, grp):
                out_re = [[] for _ in cts]
                out_im = [[] for _ in cts]
                for t in range(t0, t0 + grp):
                    r0 = t * rs
                    for j in range(len(cts)):
                        b_re = cur_ref[r0:r0 + rs, cre[j]:cre[j] + LANES]
                        b_im = cur_ref[r0:r0 + rs, cim[j]:cim[j] + LANES]
                        n_re = a_re[j] * h_re[j] - a_im[j] * h_im[j] + b_re
                        n_im = a_re[j] * h_im[j] + a_im[j] * h_re[j] + b_im
                        h_re[j], h_im[j] = n_re, n_im
                        out_re[j].append(n_re)
                        out_im[j].append(n_im)
                for j in range(len(cts)):
                    blk_re = out_re[j][0] if grp == 1 else jnp.concatenate(out_re[j], axis=0)
                    blk_im = out_im[j][0] if grp == 1 else jnp.concatenate(out_im[j], axis=0)
                    hb_ref[t0 * rs:(t0 + grp) * rs, cre[j]:cre[j] + LANES] = blk_re.astype(BF16)
                    hb_ref[t0 * rs:(t0 + grp) * rs, cim[j]:cim[j] + LANES] = blk_im.astype(BF16)
            for j in range(len(cts)):
                st_ref[:, cre[j]:cre[j] + LANES] = h_re[j]
                st_ref[:, cim[j]:cim[j] + LANES] = h_im[j]
        ys.append(jnp.dot(hb_ref[:, s * 1024:(s + 1) * 1024], wc_ref[s], preferred_element_type=F32))
    return jnp.concatenate(ys, axis=1), h_re[-1]


def _s5_call(u, l, w_bu, w_c, are, aim, h0, tc, tiles_per_pass):
    nb, length, _ = u.shape
    n = length // tc
    rows = nb * tc
    const = lambda shape: pl.BlockSpec(shape, lambda i: (0,) * len(shape))
    layer = lambda shape: pl.BlockSpec((None,) + shape, lambda i: (l,) + (0,) * len(shape))
    blk = lambda chunk: pl.BlockSpec((nb, tc, S5_WIDTH), lambda i: (0, chunk(i), 0))
    out_f = pl.BlockSpec((nb, 2 * tc, S5_WIDTH), lambda i: (0, i, 0))
    out_b = pl.BlockSpec((nb, 2 * tc, S5_WIDTH), lambda i: (0, n // 2 - 1 - i, 0))
    second = lambda i: jnp.minimum(2 * i + 2, n - 1)
    in_specs = [blk(lambda i: 2 * i + 1), blk(lambda i: n - 2 - 2 * i), blk(second), blk(lambda i: n - 1 - second(i)),
                blk(lambda i: 0), blk(lambda i: n - 1)]
    args = [u] * 6
    if nb % 8 != 0:
        pf, pb = _s5_perms(nb, tc)
        in_specs += [const((2 * rows, rows)), const((2 * rows, rows))]
        args += [jnp.asarray(pf, BF16), jnp.asarray(pb, BF16)]
    in_specs += [layer((S5_SLABS, 2 * LANES, 1024)), layer((S5_SLABS, 1024, LANES)),
                 layer((2 * nb, S5_NC)), layer((2 * nb, S5_NC)), const((2 * nb, 2 * S5_NC))]
    args += [w_bu, w_c, are, aim, h0]
    return pl.pallas_call(
        functools.partial(_s5_kernel, nb=nb, tc=tc, tiles_per_pass=tiles_per_pass),
        grid=(n // 2,),
        in_specs=in_specs,
        out_specs=(out_f, out_b, const((2 * nb, 2 * S5_NC))),
        out_shape=(jax.ShapeDtypeStruct(u.shape, F32), jax.ShapeDtypeStruct(u.shape, F32),
                   jax.ShapeDtypeStruct((2 * nb, 2 * S5_NC), F32)),
        scratch_shapes=[pltpu.VMEM((2 * rows, 2 * S5_NC), F32), pltpu.VMEM((2 * rows, 2 * S5_NC), F32),
                        pltpu.VMEM((2 * rows, 2 * S5_NC), BF16), pltpu.VMEM((2 * nb, 2 * S5_NC), F32)],
        compiler_params=pltpu.CompilerParams(dimension_semantics=("arbitrary",), vmem_limit_bytes=VMEM_LIMIT),
        name="s5_scan",
    )(*args)


def _gla_kernel(*refs, length, has_s0, nsq):
    q_ref, k_ref, v_ref, gl_ref, wg_ref, bg_ref, ng_ref = refs[:7]
    s0_ref = refs[7] if has_s0 else None
    o_ref, sfin_ref, sf_ref, sb_ref = refs[-4:]
    c = GLA_CHUNK
    n = length // c
    ri = lax.broadcasted_iota(jnp.int32, (c, c), 0)
    ci = lax.broadcasted_iota(jnp.int32, (c, c), 1)
    lane = lax.broadcasted_iota(jnp.int32, (1, GLA_KEY), 1)
    head_mask = [(lane >= h * GLA_DK) & (lane < (h + 1) * GLA_DK) for h in range(GLA_HEADS)]
    ng = ng_ref[...]
    causal = [ci <= ri, ci >= ri]
    tri = [m.astype(BF16) for m in causal]
    end = (c - 1, 0)
    chains = [(sq, d) for sq in range(nsq) for d in (0, 1)]
    key_rows = lambda h: slice(h * GLA_DK, (h + 1) * GLA_DK)
    val_cols = lambda h: slice(h * GLA_DV, (h + 1) * GLA_DV)

    sb_ref[...] = jnp.zeros_like(sb_ref)
    for sq, d in chains:
        for h in range(GLA_HEADS):
            if has_s0:
                sf_ref[sq, d, h] = s0_ref[sq, d, h]
                sb_ref[sq, d, key_rows(h), val_cols(h)] = s0_ref[sq, d, h].astype(BF16)
            else:
                sf_ref[sq, d, h] = jnp.zeros((GLA_DK, GLA_DV), F32)

    def chunk_group(r):
        qc = [q_ref[sq, pl.ds(r[d], c), :] * (GLA_DK ** -0.5) for sq, d in chains]
        kc = [k_ref[sq, pl.ds(r[d], c), :] for sq, d in chains]
        vc = [v_ref[sq, pl.ds(r[d], c), :] for sq, d in chains]
        logits = [_bdot(gl_ref[sq, pl.ds(r[d], c), :], wg_ref[d]) + bg_ref[d] for sq, d in chains]
        g = [_split2(_log_sigmoid(x) * (1.0 / GLA_GATE_NORM)) for x in logits]
        b = [jnp.dot(tri[d], hi, preferred_element_type=F32) + jnp.dot(tri[d], lo, preferred_element_type=F32)
             for (sq, d), (hi, lo) in zip(chains, g)]
        bt = [x.T for x in b]
        kt = [x.T for x in kc]
        b_mid = [x[c // 2:c // 2 + 1, :] for x in b]
        bt_mid = [x[:, c // 2:c // 2 + 1] for x in bt]
        bt_end = [x[:, end[d]:end[d] + 1] for (sq, d), x in zip(chains, bt)]
        q_in = [(x * jnp.exp(y)).astype(BF16) for x, y in zip(qc, b)]
        q_mid = [(x * jnp.exp(y - m)).astype(BF16) for x, y, m in zip(qc, b, b_mid)]
        k_mid_t = [(x * jnp.exp(m - y)).astype(BF16) for x, y, m in zip(kt, bt, bt_mid)]
        k_end_t = [(x * jnp.exp(e - y)).astype(BF16) for x, y, e in zip(kt, bt, bt_end)]
        q_st = [jnp.concatenate([jnp.where(head_mask[h], x, jnp.zeros_like(x)) for h in range(GLA_HEADS)], axis=0)
                for x in q_mid]
        a = [jnp.dot(x, y, preferred_element_type=F32) for x, y in zip(q_st, k_mid_t)]
        o_inter = [jnp.dot(x, sb_ref[sq, d], preferred_element_type=F32) for (sq, d), x in zip(chains, q_in)]
        parts = [[] for _ in chains]
        for h in range(GLA_HEADS):
            for i, (sq, d) in enumerate(chains):
                a_h = jnp.where(causal[d], a[i][h * c:(h + 1) * c, :], 0.0).astype(BF16)
                parts[i].append(jnp.dot(a_h, vc[i][:, val_cols(h)], preferred_element_type=F32))
        for h in range(GLA_HEADS):
            for i, (sq, d) in enumerate(chains):
                upd = jnp.dot(k_end_t[i][key_rows(h), :], vc[i][:, val_cols(h)], preferred_element_type=F32)
                s_new = jnp.exp(bt_end[i][key_rows(h), :]) * sf_ref[sq, d, h] + upd
                sf_ref[sq, d, h] = s_new
                sb_ref[sq, d, key_rows(h), val_cols(h)] = s_new.astype(BF16)
        return [x + jnp.concatenate(p, axis=1) for x, p in zip(o_inter, parts)]

    def finish(o, sq, r0):
        o = o + o_ref[sq, pl.ds(r0, c), :]
        normed = []
        for h in range(GLA_HEADS):
            oh = o[:, val_cols(h)]
            ms = jnp.mean(oh * oh, axis=-1, keepdims=True)
            normed.append(oh * lax.rsqrt(ms + EPS))
        o_ref[sq, pl.ds(r0, c), :] = jnp.concatenate(normed, axis=1) * ng

    def first_half(step, carry):
        r = (pl.multiple_of(step * c, c), pl.multiple_of((n - 1 - step) * c, c))
        for (sq, d), o in zip(chains, chunk_group(r)):
            o_ref[sq, pl.ds(r[d], c), :] = o
        return carry

    def second_half(step, carry):
        r = (pl.multiple_of(step * c, c), pl.multiple_of((n - 1 - step) * c, c))
        for (sq, d), o in zip(chains, chunk_group(r)):
            finish(o, sq, r[d])
        return carry

    lax.fori_loop(0, n // 2, first_half, 0)
    lax.fori_loop(n // 2, n, second_half, 0)
    for sq, d in chains:
        for h in range(GLA_HEADS):
            sfin_ref[sq, d, h] = sf_ref[sq, d, h]


def _gla_call(q, k, v, gl, l, wg, bg, ng, s0, nsq=2):
    nseq, length, _ = q.shape
    seq = lambda width: pl.BlockSpec((nsq, length, width), lambda b: (b, 0, 0))
    layer = lambda shape: pl.BlockSpec((None,) + shape, lambda b: (l,) + (0,) * len(shape))
    in_specs = [seq(GLA_KEY), seq(GLA_KEY), seq(GLA_WIDTH), seq(LANES),
                layer((2, LANES, GLA_KEY)), layer((2, 1, GLA_KEY)), layer((1, GLA_WIDTH))]
    args = [q, k, v, gl, wg, bg, ng]
    if s0 is not None:
        in_specs.append(pl.BlockSpec((nsq, None, 2, GLA_HEADS, GLA_DK, GLA_DV), lambda b: (b, l, 0, 0, 0, 0)))
        args.append(s0)
    return pl.pallas_call(
        functools.partial(_gla_kernel, length=length, has_s0=s0 is not None, nsq=nsq),
        grid=(nseq // nsq,),
        in_specs=in_specs,
        out_specs=(seq(GLA_WIDTH), pl.BlockSpec((nsq, 2, GLA_HEADS, GLA_DK, GLA_DV), lambda b: (b, 0, 0, 0, 0))),
        out_shape=(jax.ShapeDtypeStruct((nseq, length, GLA_WIDTH), F32),
                   jax.ShapeDtypeStruct((nseq, 2, GLA_HEADS, GLA_DK, GLA_DV), F32)),
        scratch_shapes=[pltpu.VMEM((nsq, 2, GLA_HEADS, GLA_DK, GLA_DV), F32),
                        pltpu.VMEM((nsq, 2, GLA_KEY, GLA_WIDTH), BF16)],
        compiler_params=pltpu.CompilerParams(dimension_semantics=("arbitrary",), vmem_limit_bytes=VMEM_LIMIT),
        name="gla",
    )(*args)


def _mix_kernel(*refs, has_pe, last, row_groups):
    if has_pe:
        (x_ref, pe_ref, yf_ref, yb_ref, u_ref, o_ref, shift_ref, scale_ref, gate_ref, ng_ref, wg_ref, d_ref,
         wglu_ref, bglu_ref, wpa_ref, wpb_ref, wo_ref, fng_ref, out_ref) = refs
    else:
        (x_ref, yf_ref, yb_ref, u_ref, o_ref, shift_ref, scale_ref, gate_ref, ng_ref, wg_ref, d_ref,
         wglu_ref, bglu_ref, wpa_ref, wpb_ref, wo_ref, fng_ref, out_ref) = refs
    nsb, tm, _ = x_ref.shape
    grp = [(i, slice(j * tm // row_groups, (j + 1) * tm // row_groups)) for i in range(nsb) for j in range(row_groups)]
    x = [x_ref[i, r, :] + pe_ref[r, :] if has_pe else x_ref[i, r, :] for i, r in grp]
    h = [_modulated_norm(v, ng_ref[...], 1.0 + scale_ref[...], shift_ref[...]).astype(BF16) for v in x]
    gts = [_bdot_t(v, wg_ref[...]) for v in h]
    y = [_gelu_tanh(yf_ref[i, r, :] + yb_ref[i, r, :] + d_ref[...] * u_ref[i, r, :]) for i, r in grp]
    glu = [_bdot(v, wglu_ref[...]) for v in y]
    y = [v * _sigmoid(g + bglu_ref[...]) for v, g in zip(y, glu)]
    y_a = [v * _silu(g[:, :S5_WIDTH]) for v, g in zip(y, gts)]
    y_b = [o_ref[i, r, :] * _silu(g[:, S5_WIDTH:2 * S5_WIDTH]) for (i, r), g in zip(grp, gts)]
    p_a = [_bdot(v, wpa_ref[...]) for v in y_a]
    p_b = [_bdot(v, wpb_ref[...]) for v in y_b]
    merged = [_sigmoid(g[:, 2 * S5_WIDTH:2 * S5_WIDTH + D_MODEL]) * a + _sigmoid(g[:, 2 * S5_WIDTH + D_MODEL:]) * b
              for g, a, b in zip(gts, p_a, p_b)]
    xn = [v + gate_ref[...] * _bdot(m, wo_ref[...]) for v, m in zip(x, merged)]
    for (i, r), v in zip(grp, xn):
        if last:
            ms = jnp.mean(v * v, axis=-1, keepdims=True)
            v = v * lax.rsqrt(ms + EPS) * fng_ref[...]
        out_ref[i, r, :] = v


def _mix_call(x, pe, yf, yb, u, o, mod, l, cond_row, ng, wg, s5_d, wglu, bglu, wpa, wpb, wo, fng, tm, last, nsb=1):
    nseq, length, _ = x.shape
    row = lambda width: pl.BlockSpec((nsb, tm, width), lambda b, j: (b, j, 0))
    in_specs = [row(D_MODEL)]
    args = [x]
    if pe is not None:
        in_specs.append(pl.BlockSpec((tm, D_MODEL), lambda b, j: (j, 0)))
        args.append(pe)
    in_specs += [row(S5_WIDTH), row(S5_WIDTH), row(S5_WIDTH), row(GLA_WIDTH),
                 _mod_spec(l, cond_row, 0), _mod_spec(l, cond_row, 1), _mod_spec(l, cond_row, 2),
                 _layer_spec(l, (1, D_MODEL)), _layer_spec(l, (GATE_W, D_MODEL)), _layer_spec(l, (1, S5_WIDTH)),
                 _layer_spec(l, (S5_WIDTH, S5_WIDTH)), _layer_spec(l, (1, S5_WIDTH)),
                 _layer_spec(l, (S5_WIDTH, D_MODEL)), _layer_spec(l, (GLA_WIDTH, D_MODEL)),
                 _layer_spec(l, (D_MODEL, D_MODEL)), pl.BlockSpec((1, D_MODEL), lambda b, j: (0, 0))]
    args += [yf, yb, u, o, mod, mod, mod, ng, wg, s5_d, wglu, bglu, wpa, wpb, wo, fng]
    return pl.pallas_call(
        functools.partial(_mix_kernel, has_pe=pe is not None, last=last, row_groups=max(1, tm // 256)),
        grid=(nseq // nsb, length // tm),
        in_specs=in_specs,
        out_specs=row(D_MODEL),
        out_shape=jax.ShapeDtypeStruct(x.shape, F32),
        compiler_params=pltpu.CompilerParams(dimension_semantics=("arbitrary", "arbitrary"),
                                             vmem_limit_bytes=VMEM_LIMIT),
        name="mix",
    )(*args)


def _grid_pos_embed(length, dim):
    rows = length // GRID_W
    quarter = dim // 4
    freqs = jnp.exp(-math.log(10000.0) * jnp.arange(quarter, dtype=F32) / quarter)

    def sincos(pos):
        ang = pos.astype(F32)[:, None] * freqs[None, :]
        return jnp.concatenate([jnp.sin(ang), jnp.cos(ang)], axis=-1)

    er = sincos(jnp.arange(rows))
    ec = sincos(jnp.arange(GRID_W))
    pe = jnp.concatenate([jnp.broadcast_to(er[:, None, :], (rows, GRID_W, dim // 2)),
                          jnp.broadcast_to(ec[None, :, :], (rows, GRID_W, dim // 2))], axis=-1)
    return pe.reshape(rows * GRID_W, dim)


def kernel(x_prompt, x_sample, c, state_s5_re, state_s5_im, state_gla, c_ctx, norm_g, w_mod, b_mod, w_in, gla_wg_up,
           gla_bg, gla_norm_g, s5_lam_re, s5_lam_im, s5_log_dt, s5_b_re, s5_b_im, s5_c_re, s5_c_im, s5_d, w_glu,
           b_glu, w_pa, w_pb, w_o, final_norm_g):
    depth = w_in.shape[0]
    bp, lp, _ = x_prompt.shape
    bs, ls, _ = x_sample.shape

    cond8 = jnp.zeros((8, D_MODEL), F32).at[0].set(c_ctx).at[1:1 + bs].set(c)
    mod = _mod_call(cond8, w_mod, b_mod)

    a_re, a_im, w_bu, w_c = _s5_params(s5_lam_re, s5_lam_im, s5_log_dt, s5_b_re, s5_b_im, s5_c_re, s5_c_im)

    w_proj, w_gate = _split_w_in(w_in)
    wg_up = jnp.concatenate([gla_wg_up, jnp.zeros((depth, 2, LANES - GLA_RANK, GLA_KEY), F32)], axis=2).astype(BF16)
    w_glu_b, w_pa_b, w_pb_b, w_o_b = (w.astype(BF16) for w in (w_glu, w_pa, w_pb, w_o))

    vec = lambda a: a[:, None, :]
    mod = mod.reshape(depth, 8, 1, 3 * D_MODEL)
    ng, gng, bg = vec(norm_g), vec(gla_norm_g), gla_bg[:, :, None, :]
    common = (ng, w_gate, vec(s5_d), w_glu_b, vec(b_glu), w_pa_b, w_pb_b, w_o_b, final_norm_g[None])
    are_p, aim_p = jnp.repeat(a_re, bp, axis=1), jnp.repeat(a_im, bp, axis=1)
    are_s, aim_s = jnp.repeat(a_re, bs, axis=1), jnp.repeat(a_im, bs, axis=1)
    ctx_row = lambda b: 0
    cond_row = lambda b: b + 1

    pe = _grid_pos_embed(ls, D_MODEL)
    zero_h0 = jnp.zeros((2 * bp, 2 * S5_NC), F32)
    xp, xs = x_prompt, x_sample
    new_re, new_im, new_gla = [], [], []
    for l in range(depth):
        last = l == depth - 1

        u, q, k, v, gl = _proj_call(xp, None, mod, l, ctx_row, ng, w_proj, tm=lp, nsb=4)
        yf, yb, hfin = _s5_call(u, l, w_bu, w_c, are_p, aim_p, zero_h0, tc=16, tiles_per_pass=2)
        o, sfin = _gla_call(q, k, v, gl, l, wg_up, bg, gng, None)
        new_gla.append(sfin)
        xp = _mix_call(xp, None, yf, yb, u, o, mod, l, ctx_row, *common, tm=lp, last=last, nsb=2)
        re_l, im_l = _s5_cols_to_state(hfin.reshape(2, bp, 2 * S5_NC))
        new_re.append(jnp.swapaxes(re_l, 0, 1))
        new_im.append(jnp.swapaxes(im_l, 0, 1))

        pe_l = pe if l == 0 else None
        u, q, k, v, gl = _proj_call(xs, pe_l, mod, l, cond_row, ng, w_proj, tm=1024)
        h0 = _s5_state_to_cols(jnp.swapaxes(state_s5_re[:, l], 0, 1), jnp.swapaxes(state_s5_im[:, l], 0, 1))
        yf, yb, _ = _s5_call(u, l, w_bu, w_c, are_s, aim_s, h0.reshape(2 * bs, 2 * S5_NC), tc=32, tiles_per_pass=4)
        o, _ = _gla_call(q, k, v, gl, l, wg_up, bg, gng, state_gla)
        xs = _mix_call(xs, pe_l, yf, yb, u, o, mod, l, cond_row, *common, tm=512, last=last)

    return (xp, xs, jnp.stack(new_re, axis=1), jnp.stack(new_im, axis=1), jnp.stack(new_gla, axis=1))
```

```python
import functools
import math

import numpy as np
import jax
import jax.numpy as jnp
from jax import lax
from jax.experimental import pallas as pl
from jax.experimental.pallas import tpu as pltpu

F32 = jnp.float32
BF16 = jnp.bfloat16

D_MODEL = 1024
GRID_W = 64
S5_WIDTH = 512
S5_GROUPS = 32
S5_CH = 16
S5_STATE = 64
S5_NC = S5_GROUPS * S5_STATE
S5_SLABS = 4
GLA_HEADS = 4
GLA_DK = 64
GLA_DV = 128
GLA_KEY = GLA_HEADS * GLA_DK
GLA_WIDTH = GLA_HEADS * GLA_DV
GLA_RANK = 16
GLA_GATE_NORM = 16.0
GLA_CHUNK = 128
EPS = 1e-6
LANES = 128
PROJ_W = S5_WIDTH + 2 * GLA_KEY + GLA_WIDTH + LANES
GATE_W = 2 * S5_WIDTH + 2 * D_MODEL
VMEM_LIMIT = 60 * 1024 * 1024


def _bdot(a, b):
    return jnp.dot(a.astype(BF16), b.astype(BF16), preferred_element_type=F32)


def _bdot_t(a, bt):
    return lax.dot_general(a.astype(BF16), bt.astype(BF16), (((1,), (1,)), ((), ())), preferred_element_type=F32)


def _split2(x):
    hi = x.astype(BF16)
    lo = (x - hi.astype(F32)).astype(BF16)
    return hi, lo


def _sigmoid(x):
    return 1.0 / (1.0 + jnp.exp(-x))


def _silu(x):
    return x * _sigmoid(x)


def _gelu_tanh(x):
    c = math.sqrt(2.0 / math.pi)
    return 0.5 * x * (1.0 + jnp.tanh(c * (x + 0.044715 * (x * x * x))))


def _exact_zero_of(x):
    bits = lax.bitcast_convert_type(x, jnp.int32)
    return lax.shift_right_logical(lax.shift_right_logical(bits, 16), 16).astype(F32)


def _log_sigmoid(x):
    return jnp.minimum(x, 0.0) - jnp.log(1.0 + jnp.exp(-jnp.abs(x)))


def _modulated_norm(x, ng, scale1p, shift):
    ms = jnp.mean(x * x, axis=-1, keepdims=True)
    return (x * lax.rsqrt(ms + EPS) * ng) * scale1p + shift


def _mod_kernel(cond_ref, w_ref, b_ref, out_ref):
    c = cond_ref[...]
    s_hi, s_lo = _split2(_silu(c))
    w_hi, w_lo = _split2(w_ref[0])
    acc = jnp.dot(s_hi, w_hi, preferred_element_type=F32)
    acc += jnp.dot(s_lo, w_hi, preferred_element_type=F32)
    acc += jnp.dot(s_hi, w_lo, preferred_element_type=F32)
    out_ref[0] = acc + b_ref[0]


def _mod_call(cond8, w_mod, b_mod):
    depth = w_mod.shape[0]
    nb = 1536
    return pl.pallas_call(
        _mod_kernel,
        grid=(depth, 3 * D_MODEL // nb),
        in_specs=[
            pl.BlockSpec((8, D_MODEL), lambda l, j: (0, 0)),
            pl.BlockSpec((1, D_MODEL, nb), lambda l, j: (l, 0, j)),
            pl.BlockSpec((1, 1, nb), lambda l, j: (l, 0, j)),
        ],
        out_specs=pl.BlockSpec((1, 8, nb), lambda l, j: (l, 0, j)),
        out_shape=jax.ShapeDtypeStruct((depth, 8, 3 * D_MODEL), F32),
        compiler_params=pltpu.CompilerParams(dimension_semantics=("arbitrary", "arbitrary")),
        name="adaln_mod",
    )(cond8, w_mod, b_mod.reshape(depth, 1, 3 * D_MODEL))


def _zoh(lre, lim, ldt):
    dt = jnp.exp(ldt)
    mag = jnp.exp(lre * dt)
    a_re = mag * jnp.cos(lim * dt)
    a_im = mag * jnp.sin(lim * dt)
    n_re = a_re - 1.0
    inv = 1.0 / (lre * lre + lim * lim)
    return a_re, a_im, (n_re * lre + a_im * lim) * inv, (a_im * lre - n_re * lim) * inv


def _zoh_kernel(lre_ref, lim_ref, ldt_ref, lre_x_ref, lim_x_ref, ldt_x_ref, bre_ref, bim_ref, cre_ref, cim_ref,
                are_ref, aim_ref, wbu_ref, wc_ref, *, depth):
    a_re, a_im, _, _ = _zoh(lre_ref[...], lim_ref[...], ldt_ref[...])
    are_ref[...] = a_re
    aim_ref[...] = a_im
    _, _, c_re, c_im = _zoh(lre_x_ref[...], lim_x_ref[...], ldt_x_ref[...])
    bre = bre_ref[...]
    bim = bim_ref[...]
    bbar = [c_re * bre - c_im * bim, c_re * bim + c_im * bre]
    cmat = [cre_ref[...], -cim_ref[...]]
    wbu_ref[...] = jnp.zeros_like(wbu_ref)
    wc_ref[...] = jnp.zeros_like(wc_ref)
    for l in range(depth):
        for s in range(S5_SLABS):
            for gl in range(8):
                g = 8 * s + gl
                for ri in range(2):
                    col = ri * 8 * S5_STATE + gl * S5_STATE
                    for d in range(2):
                        src = ((l * 2 + d) * S5_GROUPS + g) * S5_CH
                        row = (d * 8 + gl) * S5_CH
                        wbu_ref[l, s, row:row + S5_CH, col:col + S5_STATE] = bbar[ri][src:src + S5_CH, :].astype(BF16)
                    src = (l * S5_GROUPS + g) * S5_STATE
                    wc_ref[l, s, col:col + S5_STATE, gl * S5_CH:(gl + 1) * S5_CH] = (
                        cmat[ri][src:src + S5_STATE, :].astype(BF16))


def _s5_params(s5_lam_re, s5_lam_im, s5_log_dt, s5_b_re, s5_b_im, s5_c_re, s5_c_im):
    depth = s5_lam_re.shape[0]
    full = (depth, 2, S5_GROUPS, S5_CH, S5_STATE)
    expand = lambda a: jnp.broadcast_to(a, full).reshape(-1, S5_STATE)
    small = lambda a: a.reshape(depth * 2 * S5_GROUPS, -1)
    ct = lambda a: jnp.swapaxes(a, -1, -2).reshape(-1, S5_CH)
    a_re, a_im, w_bu, w_c = pl.pallas_call(
        functools.partial(_zoh_kernel, depth=depth),
        out_shape=(jax.ShapeDtypeStruct((depth * 2 * S5_GROUPS, S5_STATE), F32),
                   jax.ShapeDtypeStruct((depth * 2 * S5_GROUPS, S5_STATE), F32),
                   jax.ShapeDtypeStruct((depth, S5_SLABS, 2 * LANES, 2 * 8 * S5_STATE), BF16),
                   jax.ShapeDtypeStruct((depth, S5_SLABS, 2 * 8 * S5_STATE, LANES), BF16)),
        name="s5_zoh",
    )(small(s5_lam_re), small(s5_lam_im), small(s5_log_dt),
      expand(s5_lam_re[:, :, :, None, :]), expand(s5_lam_im[:, :, :, None, :]), expand(s5_log_dt[:, :, :, None, None]),
      expand(jnp.swapaxes(s5_b_re, -1, -2)[:, None]), expand(jnp.swapaxes(s5_b_im, -1, -2)[:, None]),
      ct(s5_c_re), ct(s5_c_im))
    return a_re.reshape(depth, 2, S5_NC), a_im.reshape(depth, 2, S5_NC), w_bu, w_c


def _s5_state_to_cols(re, im):
    lead = re.shape[:-2]
    st = jnp.stack([re.reshape(lead + (S5_SLABS, 512)), im.reshape(lead + (S5_SLABS, 512))], axis=-2)
    return st.reshape(lead + (2 * S5_NC,))


def _s5_cols_to_state(cols):
    lead = cols.shape[:-1]
    st = cols.reshape(lead + (S5_SLABS, 2, 512))
    re = st[..., 0, :].reshape(lead + (S5_GROUPS, S5_STATE))
    im = st[..., 1, :].reshape(lead + (S5_GROUPS, S5_STATE))
    return re, im


_IN_OFFS = [int(o) for o in np.cumsum([0, S5_WIDTH, S5_WIDTH, GLA_KEY, GLA_KEY, GLA_WIDTH, GLA_WIDTH, GLA_RANK,
                                       D_MODEL, D_MODEL])]


def _split_w_in_kernel(u_ref, qkv_ref, code_ref, ga_ref, gb_ref, m_ref, proj_ref, gate_ref):
    rows = 0
    for src in (u_ref, qkv_ref, code_ref):
        proj_ref[rows:rows + src.shape[1], :] = src[0].astype(BF16)
        rows += src.shape[1]
    proj_ref[rows:, :] = jnp.zeros((PROJ_W - rows, proj_ref.shape[1]), BF16)
    rows = 0
    for src in (ga_ref, gb_ref, m_ref):
        gate_ref[rows:rows + src.shape[1], :] = src[0].astype(BF16)
        rows += src.shape[1]


def _split_w_in(w_in):
    o = _IN_OFFS
    depth = w_in.shape[0]
    wt = jnp.swapaxes(w_in, 1, 2)
    tn = D_MODEL // 2
    rows = lambda lo, hi: pl.BlockSpec((pl.Element(1), pl.Element(hi - lo), pl.Element(tn)),
                                       lambda l, j: (l, lo, j * tn))
    out = lambda n: pl.BlockSpec((None, n, tn), lambda l, j: (l, 0, j))
    return pl.pallas_call(
        _split_w_in_kernel,
        grid=(depth, D_MODEL // tn),
        in_specs=[rows(o[0], o[1]), rows(o[2], o[5]), rows(o[6], o[7]), rows(o[1], o[2]), rows(o[5], o[6]),
                  rows(o[7], o[9])],
        out_specs=(out(PROJ_W), out(GATE_W)),
        out_shape=(jax.ShapeDtypeStruct((depth, PROJ_W, D_MODEL), BF16),
                   jax.ShapeDtypeStruct((depth, GATE_W, D_MODEL), BF16)),
        compiler_params=pltpu.CompilerParams(dimension_semantics=("arbitrary", "arbitrary"),
                                             vmem_limit_bytes=VMEM_LIMIT),
        name="split_w_in",
    )(wt, wt, wt, wt, wt, wt)


def _proj_kernel(*refs, has_pe):
    if has_pe:
        x_ref, pe_ref, shift_ref, scale_ref, ng_ref, w_ref, u_ref, q_ref, k_ref, v_ref, gl_ref = refs
    else:
        x_ref, shift_ref, scale_ref, ng_ref, w_ref, u_ref, q_ref, k_ref, v_ref, gl_ref = refs
    nsb, tm, _ = x_ref.shape
    rg = max(1, tm // 512)
    grp = [(i, slice(j * tm // rg, (j + 1) * tm // rg)) for i in range(nsb) for j in range(rg)]
    x = [x_ref[i, r, :] + pe_ref[r, :] if has_pe else x_ref[i, r, :] for i, r in grp]
    h = [_modulated_norm(v, ng_ref[...], 1.0 + scale_ref[...], shift_ref[...]) for v in x]
    p = [_bdot_t(v, w_ref[...]) for v in h]
    for (i, r), v in zip(grp, p):
        o = 0
        for ref, width in ((u_ref, S5_WIDTH), (q_ref, GLA_KEY), (k_ref, GLA_KEY), (v_ref, GLA_WIDTH), (gl_ref, LANES)):
            ref[i, r, :] = v[:, o:o + width].astype(ref.dtype)
            o += width


def _mod_spec(l, cond_row, part):
    return pl.BlockSpec((None, None, 1, D_MODEL), lambda b, j: (l, cond_row(b), 0, part))


def _layer_spec(l, shape):
    return pl.BlockSpec((None,) + shape, lambda b, j: (l,) + (0,) * len(shape))


def _proj_call(x, pe, mod, l, cond_row, ng, w, tm, nsb=1):
    nseq, length, _ = x.shape
    row = lambda width: pl.BlockSpec((nsb, tm, width), lambda b, j: (b, j, 0))
    in_specs = [row(D_MODEL)]
    args = [x]
    if pe is not None:
        in_specs.append(pl.BlockSpec((tm, D_MODEL), lambda b, j: (j, 0)))
        args.append(pe)
    in_specs += [_mod_spec(l, cond_row, 0), _mod_spec(l, cond_row, 1), _layer_spec(l, (1, D_MODEL)),
                 _layer_spec(l, (PROJ_W, D_MODEL))]
    args += [mod, mod, ng, w]
    sds = lambda width, dt: jax.ShapeDtypeStruct((nseq, length, width), dt)
    return pl.pallas_call(
        functools.partial(_proj_kernel, has_pe=pe is not None),
        grid=(nseq // nsb, length // tm),
        in_specs=in_specs,
        out_specs=(row(S5_WIDTH), row(GLA_KEY), row(GLA_KEY), row(GLA_WIDTH), row(LANES)),
        out_shape=(sds(S5_WIDTH, F32), sds(GLA_KEY, F32), sds(GLA_KEY, F32), sds(GLA_WIDTH, BF16), sds(LANES, F32)),
        compiler_params=pltpu.CompilerParams(dimension_semantics=("arbitrary", "arbitrary"),
                                             vmem_limit_bytes=VMEM_LIMIT),
        name="proj",
    )(*args)


def _s5_perms(nb, tc):
    r = nb * tc
    pf = np.zeros((2 * r, r), np.float32)
    pb = np.zeros((2 * r, r), np.float32)
    for t in range(tc):
        for b in range(nb):
            pf[t * 2 * nb + b, b * tc + t] = 1.0
            pb[t * 2 * nb + nb + b, b * tc + (tc - 1 - t)] = 1.0
    return pf, pb


def _s5_store_unpacked(y2, yf_ref, yb_ref, t_f, t_b, nb, tc):
    rs = 2 * nb
    for t in range(tc):
        yf_ref[:, t_f + t, :] = y2[t * rs:t * rs + nb]
        yb_ref[:, t_b + tc - 1 - t, :] = y2[t * rs + nb:(t + 1) * rs]


def _s5_kernel(*refs, nb, tc, tiles_per_pass):
    use_perm = nb % 8 != 0
    uf1_ref, ub1_ref, uf2_ref, ub2_ref, uf0_ref, ub0_ref = refs[:6]
    pf_ref, pb_ref = refs[6:8] if use_perm else (None, None)
    wbu_ref, wc_ref, are_ref, aim_ref, h0_ref, yf_ref, yb_ref, hfin_ref, bua_ref, bub_ref, hb_ref, st_ref = refs[-12:]
    i = pl.program_id(0)
    rows = nb * tc
    rs = 2 * nb

    def packed_inputs(uf_ref, ub_ref):
        if not use_perm:
            return tuple(jnp.concatenate([r[:, t, :] for t in range(tc)], axis=0).astype(BF16) for r in (uf_ref, ub_ref))
        uf = uf_ref[...].reshape(rows, S5_WIDTH).astype(BF16)
        ub = ub_ref[...].reshape(rows, S5_WIDTH).astype(BF16)
        u2f = jnp.dot(pf_ref[...], uf, preferred_element_type=F32).astype(BF16)
        u2b = jnp.dot(pb_ref[...], ub, preferred_element_type=F32).astype(BF16)
        return [jnp.concatenate([u2f[:, s * LANES:(s + 1) * LANES], u2b[:, s * LANES:(s + 1) * LANES]], axis=1)
                for s in range(S5_SLABS)]

    def bu_slab(lhs, s, dst_ref):
        cols = slice(s * 1024, (s + 1) * 1024)
        if use_perm:
            dst_ref[:, cols] = jnp.dot(lhs[s], wbu_ref[s], preferred_element_type=F32)
            return
        uf, ub = lhs
        bf = jnp.dot(uf[:, s * LANES:(s + 1) * LANES], wbu_ref[s, :LANES, :], preferred_element_type=F32)
        bb = jnp.dot(ub[:, s * LANES:(s + 1) * LANES], wbu_ref[s, LANES:, :], preferred_element_type=F32)
        for t in range(tc):
            dst_ref[t * rs:t * rs + nb, cols] = bf[t * nb:(t + 1) * nb]
            dst_ref[t * rs + nb:(t + 1) * rs, cols] = bb[(tc - 1 - t) * nb:(tc - t) * nb]

    @pl.when(i == 0)
    def _():
        st_ref[...] = h0_ref[...]
        lhs0 = packed_inputs(uf0_ref, ub0_ref)
        for s in range(S5_SLABS):
            bu_slab(lhs0, s, bua_ref)

    half_step = functools.partial(_s5_half_step, wc_ref=wc_ref, are_ref=are_ref, aim_ref=aim_ref, hb_ref=hb_ref,
                                  st_ref=st_ref, bu_slab=bu_slab, rs=rs, tc=tc, tiles_per_pass=tiles_per_pass)
    y2, tail = half_step(bua_ref, bub_ref, packed_inputs(uf1_ref, ub1_ref))
    _s5_store_unpacked(y2, yf_ref, yb_ref, 0, tc, nb, tc)
    y2, _ = half_step(bub_ref, bua_ref, packed_inputs(uf2_ref, ub2_ref), after=tail)
    _s5_store_unpacked(y2, yf_ref, yb_ref, tc, 0, nb, tc)
    hfin_ref[...] = st_ref[...]


def _s5_half_step(cur_ref, nxt_ref, lhs_next, *, wc_ref, are_ref, aim_ref, hb_ref, st_ref, bu_slab,
                  rs, tc, tiles_per_pass, after=None):
    grp = max(1, 16 // rs)
    ys = []
    for s in range(S5_SLABS):
        bu_slab(lhs_next, s, nxt_ref)
        for c0 in range(4 * s, 4 * s + 4, tiles_per_pass):
            cts = list(range(c0, c0 + tiles_per_pass))
            cre = [(ct // 4) * 1024 + (ct % 4) * LANES for ct in cts]
            cim = [c + 512 for c in cre]
            a_re = [are_ref[:, ct * LANES:(ct + 1) * LANES] for ct in cts]
            a_im = [aim_ref[:, ct * LANES:(ct + 1) * LANES] for ct in cts]
            h_re = [st_ref[:, c:c + LANES] for c in cre]
            h_im = [st_ref[:, c:c + LANES] for c in cim]
            if after is not None and c0 == 0:
                h_re[0] = h_re[0] + _exact_zero_of(after)
            for t0 in range(0, tc, grp):
                out_re = [[] for _ in cts]
                out_im = [[] for _ in cts]
                for t in range(t0, t0 + grp):
                    r0 = t * rs
                    for j in range(len(cts)):
                        b_re = cur_ref[r0:r0 + rs, cre[j]:cre[j] + LANES]
                        b_im = cur_ref[r0:r0 + rs, cim[j]:cim[j] + LANES]
                        n_re = a_re[j] * h_re[j] - a_im[j] * h_im[j] + b_re
                        n_im = a_re[j] * h_im[j] + a_im[j] * h_re[j] + b_im
                        h_re[j], h_im[j] = n_re, n_im
                        out_re[j].append(n_re)
                        out_im[j].append(n_im)
                for j in range(len(cts)):
                    blk_re = out_re[j][0] if grp == 1 else jnp.concatenate(out_re[j], axis=0)
                    blk_im = out_im[j][0] if grp == 1 else jnp.concatenate(out_im[j], axis=0)
                    hb_ref[t0 * rs:(t0 + grp) * rs, cre[j]:cre[j] + LANES] = blk_re.astype(BF16)
                    hb_ref[t0 * rs:(t0 + grp) * rs, cim[j]:cim[j] + LANES] = blk_im.astype(BF16)
            for j in range(len(cts)):
                st_ref[:, cre[j]:cre[j] + LANES] = h_re[j]
                st_ref[:, cim[j]:cim[j] + LANES] = h_im[j]
        ys.append(jnp.dot(hb_ref[:, s * 1024:(s + 1) * 1024], wc_ref[s], preferred_element_type=F32))
    return jnp.concatenate(ys, axis=1), h_re[-1]


def _s5_call(u, l, w_bu, w_c, are, aim, h0, tc, tiles_per_pass):
    nb, length, _ = u.shape
    n = length // tc
    rows = nb * tc
    const = lambda shape: pl.BlockSpec(shape, lambda i: (0,) * len(shape))
    layer = lambda shape: pl.BlockSpec((None,) + shape, lambda i: (l,) + (0,) * len(shape))
    blk = lambda chunk: pl.BlockSpec((nb, tc, S5_WIDTH), lambda i: (0, chunk(i), 0))
    out_f = pl.BlockSpec((nb, 2 * tc, S5_WIDTH), lambda i: (0, i, 0))
    out_b = pl.BlockSpec((nb, 2 * tc, S5_WIDTH), lambda i: (0, n // 2 - 1 - i, 0))
    second = lambda i: jnp.minimum(2 * i + 2, n - 1)
    in_specs = [blk(lambda i: 2 * i + 1), blk(lambda i: n - 2 - 2 * i), blk(second), blk(lambda i: n - 1 - second(i)),
                blk(lambda i: 0), blk(lambda i: n - 1)]
    args = [u] * 6
    if nb % 8 != 0:
        pf, pb = _s5_perms(nb, tc)
        in_specs += [const((2 * rows, rows)), const((2 * rows, rows))]
        args += [jnp.asarray(pf, BF16), jnp.asarray(pb, BF16)]
    in_specs += [layer((S5_SLABS, 2 * LANES, 1024)), layer((S5_SLABS, 1024, LANES)),
                 layer((2 * nb, S5_NC)), layer((2 * nb, S5_NC)), const((2 * nb, 2 * S5_NC))]
    args += [w_bu, w_c, are, aim, h0]
    return pl.pallas_call(
        functools.partial(_s5_kernel, nb=nb, tc=tc, tiles_per_pass=tiles_per_pass),
        grid=(n // 2,),
        in_specs=in_specs,
        out_specs=(out_f, out_b, const((2 * nb, 2 * S5_NC))),
        out_shape=(jax.ShapeDtypeStruct(u.shape, F32), jax.ShapeDtypeStruct(u.shape, F32),
                   jax.ShapeDtypeStruct((2 * nb, 2 * S5_NC), F32)),
        scratch_shapes=[pltpu.VMEM((2 * rows, 2 * S5_NC), F32), pltpu.VMEM((2 * rows, 2 * S5_NC), F32),
                        pltpu.VMEM((2 * rows, 2 * S5_NC), BF16), pltpu.VMEM((2 * nb, 2 * S5_NC), F32)],
        compiler_params=pltpu.CompilerParams(dimension_semantics=("arbitrary",), vmem_limit_bytes=VMEM_LIMIT),
        name="s5_scan",
    )(*args)


def _gla_kernel(*refs, length, has_s0, nsq, cpi):
    q_ref, k_ref, v_ref, gl_ref, wg_ref, bg_ref, ng_ref = refs[:7]
    s0_ref = refs[7] if has_s0 else None
    o_ref, sfin_ref, sf_ref, sb_ref = refs[-4:]
    c = GLA_CHUNK
    n = length // c
    ri = lax.broadcasted_iota(jnp.int32, (c, c), 0)
    ci = lax.broadcasted_iota(jnp.int32, (c, c), 1)
    lane = lax.broadcasted_iota(jnp.int32, (1, GLA_KEY), 1)
    head_mask = [(lane >= h * GLA_DK) & (lane < (h + 1) * GLA_DK) for h in range(GLA_HEADS)]
    ng = ng_ref[...]
    causal = [ci <= ri, ci >= ri]
    tri = [m.astype(BF16) for m in causal]
    end = (c - 1, 0)
    chains = [(sq, d, j) for j in range(cpi) for sq in range(nsq) for d in (0, 1)]
    key_rows = lambda h: slice(h * GLA_DK, (h + 1) * GLA_DK)
    val_cols = lambda h: slice(h * GLA_DV, (h + 1) * GLA_DV)

    sb_ref[...] = jnp.zeros_like(sb_ref)
    for sq in range(nsq):
        for d in (0, 1):
            for h in range(GLA_HEADS):
                if has_s0:
                    sf_ref[sq, d, h] = s0_ref[sq, d, h]
                    sb_ref[sq, d, key_rows(h), val_cols(h)] = s0_ref[sq, d, h].astype(BF16)
                else:
                    sf_ref[sq, d, h] = jnp.zeros((GLA_DK, GLA_DV), F32)

    def chunk_group(r):
        row = lambda d, j: r[d][j]
        qc = [q_ref[sq, pl.ds(row(d, j), c), :] * (GLA_DK ** -0.5) for sq, d, j in chains]
        kc = [k_ref[sq, pl.ds(row(d, j), c), :] for sq, d, j in chains]
        vc = [v_ref[sq, pl.ds(row(d, j), c), :] for sq, d, j in chains]
        logits = [_bdot(gl_ref[sq, pl.ds(row(d, j), c), :], wg_ref[d]) + bg_ref[d] for sq, d, j in chains]
        g = [_split2(_log_sigmoid(x) * (1.0 / GLA_GATE_NORM)) for x in logits]
        b = [jnp.dot(tri[d], hi, preferred_element_type=F32) + jnp.dot(tri[d], lo, preferred_element_type=F32)
             for (sq, d, j), (hi, lo) in zip(chains, g)]
        bt = [x.T for x in b]
        kt = [x.T for x in kc]
        b_mid = [x[c // 2:c // 2 + 1, :] for x in b]
        bt_mid = [x[:, c // 2:c // 2 + 1] for x in bt]
        bt_end = [x[:, end[d]:end[d] + 1] for (sq, d, j), x in zip(chains, bt)]
        q_in = [(x * jnp.exp(y)).astype(BF16) for x, y in zip(qc, b)]
        q_mid = [(x * jnp.exp(y - m)).astype(BF16) for x, y, m in zip(qc, b, b_mid)]
        k_mid_t = [(x * jnp.exp(m - y)).astype(BF16) for x, y, m in zip(kt, bt, bt_mid)]
        k_end_t = [(x * jnp.exp(e - y)).astype(BF16) for x, y, e in zip(kt, bt, bt_end)]
        q_st = [jnp.concatenate([jnp.where(head_mask[h], x, jnp.zeros_like(x)) for h in range(GLA_HEADS)], axis=0)
                for x in q_mid]
        a = [jnp.dot(x, y, preferred_element_type=F32) for x, y in zip(q_st, k_mid_t)]
        parts = [[] for _ in chains]
        for h in range(GLA_HEADS):
            for i, (sq, d, j) in enumerate(chains):
                a_h = jnp.where(causal[d], a[i][h * c:(h + 1) * c, :], 0.0).astype(BF16)
                parts[i].append(jnp.dot(a_h, vc[i][:, val_cols(h)], preferred_element_type=F32))
        upd = [[jnp.dot(k_end_t[i][key_rows(h), :], vc[i][:, val_cols(h)], preferred_element_type=F32)
                for h in range(GLA_HEADS)] for i in range(len(chains))]
        o_inter = [None] * len(chains)
        for i, (sq, d, j) in enumerate(chains):
            o_inter[i] = jnp.dot(q_in[i], sb_ref[sq, d], preferred_element_type=F32)
            for h in range(GLA_HEADS):
                s_new = jnp.exp(bt_end[i][key_rows(h), :]) * sf_ref[sq, d, h] + upd[i][h]
                sf_ref[sq, d, h] = s_new
                sb_ref[sq, d, key_rows(h), val_cols(h)] = s_new.astype(BF16)
        return [x + jnp.concatenate(p, axis=1) for x, p in zip(o_inter, parts)]

    def finish(o, sq, r0):
        o = o + o_ref[sq, pl.ds(r0, c), :]
        normed = []
        for h in range(GLA_HEADS):
            oh = o[:, val_cols(h)]
            ms = jnp.mean(oh * oh, axis=-1, keepdims=True)
            normed.append(oh * lax.rsqrt(ms + EPS))
        o_ref[sq, pl.ds(r0, c), :] = jnp.concatenate(normed, axis=1) * ng

    def rows_of(step):
        fwd = [pl.multiple_of((step * cpi + j) * c, c) for j in range(cpi)]
        bwd = [pl.multiple_of((n - 1 - step * cpi - j) * c, c) for j in range(cpi)]
        return fwd, bwd

    def first_half(step, carry):
        r = rows_of(step)
        for (sq, d, j), o in zip(chains, chunk_group(r)):
            o_ref[sq, pl.ds(r[d][j], c), :] = o
        return carry

    def second_half(step, carry):
        r = rows_of(step)
        for (sq, d, j), o in zip(chains, chunk_group(r)):
            finish(o, sq, r[d][j])
        return carry

    steps = n // cpi
    lax.fori_loop(0, steps // 2, first_half, 0)
    lax.fori_loop(steps // 2, steps, second_half, 0)
    for sq in range(nsq):
        for d in (0, 1):
            for h in range(GLA_HEADS):
                sfin_ref[sq, d, h] = sf_ref[sq, d, h]


def _gla_call(q, k, v, gl, l, wg, bg, ng, s0, nsq, cpi):
    nseq, length, _ = q.shape
    assert (length // GLA_CHUNK) % (2 * cpi) == 0 and nseq % nsq == 0
    seq = lambda width: pl.BlockSpec((nsq, length, width), lambda b: (b, 0, 0))
    layer = lambda shape: pl.BlockSpec((None,) + shape, lambda b: (l,) + (0,) * len(shape))
    in_specs = [seq(GLA_KEY), seq(GLA_KEY), seq(GLA_WIDTH), seq(LANES),
                layer((2, LANES, GLA_KEY)), layer((2, 1, GLA_KEY)), layer((1, GLA_WIDTH))]
    args = [q, k, v, gl, wg, bg, ng]
    if s0 is not None:
        in_specs.append(pl.BlockSpec((nsq, None, 2, GLA_HEADS, GLA_DK, GLA_DV), lambda b: (b, l, 0, 0, 0, 0)))
        args.append(s0)
    return pl.pallas_call(
        functools.partial(_gla_kernel, length=length, has_s0=s0 is not None, nsq=nsq, cpi=cpi),
        grid=(nseq // nsq,),
        in_specs=in_specs,
        out_specs=(seq(GLA_WIDTH), pl.BlockSpec((nsq, 2, GLA_HEADS, GLA_DK, GLA_DV), lambda b: (b, 0, 0, 0, 0))),
        out_shape=(jax.ShapeDtypeStruct((nseq, length, GLA_WIDTH), F32),
                   jax.ShapeDtypeStruct((nseq, 2, GLA_HEADS, GLA_DK, GLA_DV), F32)),
        scratch_shapes=[pltpu.VMEM((nsq, 2, GLA_HEADS, GLA_DK, GLA_DV), F32),
                        pltpu.VMEM((nsq, 2, GLA_KEY, GLA_WIDTH), BF16)],
        compiler_params=pltpu.CompilerParams(dimension_semantics=("arbitrary",), vmem_limit_bytes=VMEM_LIMIT),
        name="gla",
    )(*args)


def _mix_kernel(*refs, has_pe, last, row_groups):
    if has_pe:
        (x_ref, pe_ref, yf_ref, yb_ref, u_ref, o_ref, shift_ref, scale_ref, gate_ref, ng_ref, wg_ref, d_ref,
         wglu_ref, bglu_ref, wpa_ref, wpb_ref, wo_ref, fng_ref, out_ref) = refs
    else:
        (x_ref, yf_ref, yb_ref, u_ref, o_ref, shift_ref, scale_ref, gate_ref, ng_ref, wg_ref, d_ref,
         wglu_ref, bglu_ref, wpa_ref, wpb_ref, wo_ref, fng_ref, out_ref) = refs
    nsb, tm, _ = x_ref.shape
    grp = [(i, slice(j * tm // row_groups, (j + 1) * tm // row_groups)) for i in range(nsb) for j in range(row_groups)]
    x = [x_ref[i, r, :] + pe_ref[r, :] if has_pe else x_ref[i, r, :] for i, r in grp]
    h = [_modulated_norm(v, ng_ref[...], 1.0 + scale_ref[...], shift_ref[...]).astype(BF16) for v in x]
    gts = [_bdot_t(v, wg_ref[...]) for v in h]
    y = [_gelu_tanh(yf_ref[i, r, :] + yb_ref[i, r, :] + d_ref[...] * u_ref[i, r, :]) for i, r in grp]
    glu = [_bdot(v, wglu_ref[...]) for v in y]
    y = [v * _sigmoid(g + bglu_ref[...]) for v, g in zip(y, glu)]
    y_a = [v * _silu(g[:, :S5_WIDTH]) for v, g in zip(y, gts)]
    y_b = [o_ref[i, r, :] * _silu(g[:, S5_WIDTH:2 * S5_WIDTH]) for (i, r), g in zip(grp, gts)]
    p_a = [_bdot(v, wpa_ref[...]) for v in y_a]
    p_b = [_bdot(v, wpb_ref[...]) for v in y_b]
    merged = [_sigmoid(g[:, 2 * S5_WIDTH:2 * S5_WIDTH + D_MODEL]) * a + _sigmoid(g[:, 2 * S5_WIDTH + D_MODEL:]) * b
              for g, a, b in zip(gts, p_a, p_b)]
    xn = [v + gate_ref[...] * _bdot(m, wo_ref[...]) for v, m in zip(x, merged)]
    for (i, r), v in zip(grp, xn):
        if last:
            ms = jnp.mean(v * v, axis=-1, keepdims=True)
            v = v * lax.rsqrt(ms + EPS) * fng_ref[...]
        out_ref[i, r, :] = v


def _mix_call(x, pe, yf, yb, u, o, mod, l, cond_row, ng, wg, s5_d, wglu, bglu, wpa, wpb, wo, fng, tm, last, nsb=1):
    nseq, length, _ = x.shape
    row = lambda width: pl.BlockSpec((nsb, tm, width), lambda b, j: (b, j, 0))
    in_specs = [row(D_MODEL)]
    args = [x]
    if pe is not None:
        in_specs.append(pl.BlockSpec((tm, D_MODEL), lambda b, j: (j, 0)))
        args.append(pe)
    in_specs += [row(S5_WIDTH), row(S5_WIDTH), row(S5_WIDTH), row(GLA_WIDTH),
                 _mod_spec(l, cond_row, 0), _mod_spec(l, cond_row, 1), _mod_spec(l, cond_row, 2),
                 _layer_spec(l, (1, D_MODEL)), _layer_spec(l, (GATE_W, D_MODEL)), _layer_spec(l, (1, S5_WIDTH)),
                 _layer_spec(l, (S5_WIDTH, S5_WIDTH)), _layer_spec(l, (1, S5_WIDTH)),
                 _layer_spec(l, (S5_WIDTH, D_MODEL)), _layer_spec(l, (GLA_WIDTH, D_MODEL)),
                 _layer_spec(l, (D_MODEL, D_MODEL)), pl.BlockSpec((1, D_MODEL), lambda b, j: (0, 0))]
    args += [yf, yb, u, o, mod, mod, mod, ng, wg, s5_d, wglu, bglu, wpa, wpb, wo, fng]
    return pl.pallas_call(
        functools.partial(_mix_kernel, has_pe=pe is not None, last=last, row_groups=max(1, tm // 256)),
        grid=(nseq // nsb, length // tm),
        in_specs=in_specs,
        out_specs=row(D_MODEL),
        out_shape=jax.ShapeDtypeStruct(x.shape, F32),
        compiler_params=pltpu.CompilerParams(dimension_semantics=("arbitrary", "arbitrary"),
                                             vmem_limit_bytes=VMEM_LIMIT),
        name="mix",
    )(*args)


def _grid_pos_embed(length, dim):
    rows = length // GRID_W
    quarter = dim // 4
    freqs = jnp.exp(-math.log(10000.0) * jnp.arange(quarter, dtype=F32) / quarter)

    def sincos(pos):
        ang = pos.astype(F32)[:, None] * freqs[None, :]
        return jnp.concatenate([jnp.sin(ang), jnp.cos(ang)], axis=-1)

    er = sincos(jnp.arange(rows))
    ec = sincos(jnp.arange(GRID_W))
    pe = jnp.concatenate([jnp.broadcast_to(er[:, None, :], (rows, GRID_W, dim // 2)),
                          jnp.broadcast_to(ec[None, :, :], (rows, GRID_W, dim // 2))], axis=-1)
    return pe.reshape(rows * GRID_W, dim)


def kernel(x_prompt, x_sample, c, state_s5_re, state_s5_im, state_gla, c_ctx, norm_g, w_mod, b_mod, w_in, gla_wg_up,
           gla_bg, gla_norm_g, s5_lam_re, s5_lam_im, s5_log_dt, s5_b_re, s5_b_im, s5_c_re, s5_c_im, s5_d, w_glu,
           b_glu, w_pa, w_pb, w_o, final_norm_g):
    depth = w_in.shape[0]
    bp, lp, _ = x_prompt.shape
    bs, ls, _ = x_sample.shape

    cond8 = jnp.zeros((8, D_MODEL), F32).at[0].set(c_ctx).at[1:1 + bs].set(c)
    mod = _mod_call(cond8, w_mod, b_mod)

    a_re, a_im, w_bu, w_c = _s5_params(s5_lam_re, s5_lam_im, s5_log_dt, s5_b_re, s5_b_im, s5_c_re, s5_c_im)

    w_proj, w_gate = _split_w_in(w_in)
    wg_up = jnp.concatenate([gla_wg_up, jnp.zeros((depth, 2, LANES - GLA_RANK, GLA_KEY), F32)], axis=2).astype(BF16)
    w_glu_b, w_pa_b, w_pb_b, w_o_b = (w.astype(BF16) for w in (w_glu, w_pa, w_pb, w_o))

    vec = lambda a: a[:, None, :]
    mod = mod.reshape(depth, 8, 1, 3 * D_MODEL)
    ng, gng, bg = vec(norm_g), vec(gla_norm_g), gla_bg[:, :, None, :]
    common = (ng, w_gate, vec(s5_d), w_glu_b, vec(b_glu), w_pa_b, w_pb_b, w_o_b, final_norm_g[None])
    are_p, aim_p = jnp.repeat(a_re, bp, axis=1), jnp.repeat(a_im, bp, axis=1)
    are_s, aim_s = jnp.repeat(a_re, bs, axis=1), jnp.repeat(a_im, bs, axis=1)
    ctx_row = lambda b: 0
    cond_row = lambda b: b + 1

    pe = _grid_pos_embed(ls, D_MODEL)
    zero_h0 = jnp.zeros((2 * bp, 2 * S5_NC), F32)
    xp, xs = x_prompt, x_sample
    new_re, new_im, new_gla = [], [], []
    for l in range(depth):
        last = l == depth - 1

        u, q, k, v, gl = _proj_call(xp, None, mod, l, ctx_row, ng, w_proj, tm=lp, nsb=4)
        yf, yb, hfin = _s5_call(u, l, w_bu, w_c, are_p, aim_p, zero_h0, tc=16, tiles_per_pass=2)
        o, sfin = _gla_call(q, k, v, gl, l, wg_up, bg, gng, None, nsq=4, cpi=1)
        new_gla.append(sfin)
        xp = _mix_call(xp, None, yf, yb, u, o, mod, l, ctx_row, *common, tm=lp, last=last, nsb=2)
        re_l, im_l = _s5_cols_to_state(hfin.reshape(2, bp, 2 * S5_NC))
        new_re.append(jnp.swapaxes(re_l, 0, 1))
        new_im.append(jnp.swapaxes(im_l, 0, 1))

        pe_l = pe if l == 0 else None
        u, q, k, v, gl = _proj_call(xs, pe_l, mod, l, cond_row, ng, w_proj, tm=1024)
        h0 = _s5_state_to_cols(jnp.swapaxes(state_s5_re[:, l], 0, 1), jnp.swapaxes(state_s5_im[:, l], 0, 1))
        yf, yb, _ = _s5_call(u, l, w_bu, w_c, are_s, aim_s, h0.reshape(2 * bs, 2 * S5_NC), tc=32, tiles_per_pass=4)
        o, _ = _gla_call(q, k, v, gl, l, wg_up, bg, gng, state_gla, nsq=2, cpi=2)
        xs = _mix_call(xs, pe_l, yf, yb, u, o, mod, l, cond_row, *common, tm=512, last=last)

    return (xp, xs, jnp.stack(new_re, axis=1), jnp.stack(new_im, axis=1), jnp.stack(new_gla, axis=1))
```

```python
import functools
import math

import numpy as np
import jax
import jax.numpy as jnp
from jax import lax
from jax.experimental import pallas as pl
from jax.experimental.pallas import tpu as pltpu

F32 = jnp.float32
BF16 = jnp.bfloat16

D_MODEL = 1024
GRID_W = 64
S5_WIDTH = 512
S5_GROUPS = 32
S5_CH = 16
S5_STATE = 64
S5_NC = S5_GROUPS * S5_STATE
S5_SLAB_GROUPS = 8
S5_SLABS = S5_GROUPS // S5_SLAB_GROUPS
S5_SLAB_NC = S5_SLAB_GROUPS * S5_STATE
S5_SLAB_COLS = 2 * S5_SLAB_NC
GLA_HEADS = 4
GLA_DK = 64
GLA_DV = 128
GLA_KEY = GLA_HEADS * GLA_DK
GLA_WIDTH = GLA_HEADS * GLA_DV
GLA_RANK = 16
GLA_GATE_NORM = 16.0
GLA_CHUNK = 128
EPS = 1e-6
LANES = 128
PROJ_W = S5_WIDTH + 2 * GLA_KEY + GLA_WIDTH + LANES
GATE_W = 2 * S5_WIDTH + 2 * D_MODEL
VMEM_LIMIT = 60 * 1024 * 1024
MXU_TILE = 256


def _bdot(a, b):
    return jnp.dot(a.astype(BF16), b.astype(BF16), preferred_element_type=F32)


def _bdot_t(a, bt):
    return lax.dot_general(a.astype(BF16), bt.astype(BF16), (((1,), (1,)), ((), ())), preferred_element_type=F32)


def _split2(x):
    hi = x.astype(BF16)
    lo = (x - hi.astype(F32)).astype(BF16)
    return hi, lo


def _sigmoid(x):
    return 1.0 / (1.0 + jnp.exp(-x))


def _silu(x):
    return x * _sigmoid(x)


def _gelu_tanh(x):
    c = math.sqrt(2.0 / math.pi)
    return 0.5 * x * (1.0 + jnp.tanh(c * (x + 0.044715 * (x * x * x))))


def _log_sigmoid(x):
    return jnp.minimum(x, 0.0) - jnp.log(1.0 + jnp.exp(-jnp.abs(x)))


def _modulated_norm(x, ng, scale1p, shift):
    ms = jnp.mean(x * x, axis=-1, keepdims=True)
    return (x * lax.rsqrt(ms + EPS) * ng) * scale1p + shift


def _mod_kernel(cond_ref, w_ref, b_ref, out_ref):
    c = cond_ref[...]
    s_hi, s_lo = _split2(_silu(c))
    w_hi, w_lo = _split2(w_ref[0])
    acc = jnp.dot(s_hi, w_hi, preferred_element_type=F32)
    acc += jnp.dot(s_lo, w_hi, preferred_element_type=F32)
    acc += jnp.dot(s_hi, w_lo, preferred_element_type=F32)
    out_ref[0] = acc + b_ref[0]


def _mod_call(cond8, w_mod, b_mod):
    depth = w_mod.shape[0]
    nb = 1536
    return pl.pallas_call(
        _mod_kernel,
        grid=(depth, 3 * D_MODEL // nb),
        in_specs=[
            pl.BlockSpec((8, D_MODEL), lambda l, j: (0, 0)),
            pl.BlockSpec((1, D_MODEL, nb), lambda l, j: (l, 0, j)),
            pl.BlockSpec((1, 1, nb), lambda l, j: (l, 0, j)),
        ],
        out_specs=pl.BlockSpec((1, 8, nb), lambda l, j: (l, 0, j)),
        out_shape=jax.ShapeDtypeStruct((depth, 8, 3 * D_MODEL), F32),
        compiler_params=pltpu.CompilerParams(dimension_semantics=("arbitrary", "arbitrary")),
        name="adaln_mod",
    )(cond8, w_mod, b_mod.reshape(depth, 1, 3 * D_MODEL))


def _zoh(lre, lim, ldt):
    dt = jnp.exp(ldt)
    mag = jnp.exp(lre * dt)
    a_re = mag * jnp.cos(lim * dt)
    a_im = mag * jnp.sin(lim * dt)
    n_re = a_re - 1.0
    inv = 1.0 / (lre * lre + lim * lim)
    return a_re, a_im, (n_re * lre + a_im * lim) * inv, (a_im * lre - n_re * lim) * inv


def _zoh_kernel(lre_ref, lim_ref, ldt_ref, lre_x_ref, lim_x_ref, ldt_x_ref, bre_ref, bim_ref, cre_ref, cim_ref,
                are_ref, aim_ref, wbu_ref, wc_ref, *, depth):
    a_re, a_im, _, _ = _zoh(lre_ref[...], lim_ref[...], ldt_ref[...])
    are_ref[...] = a_re
    aim_ref[...] = a_im
    _, _, c_re, c_im = _zoh(lre_x_ref[...], lim_x_ref[...], ldt_x_ref[...])
    bre = bre_ref[...]
    bim = bim_ref[...]
    bbar = [c_re * bre - c_im * bim, c_re * bim + c_im * bre]
    cmat = [cre_ref[...], -cim_ref[...]]
    wbu_ref[...] = jnp.zeros_like(wbu_ref)
    wc_ref[...] = jnp.zeros_like(wc_ref)
    for l in range(depth):
        for s in range(S5_SLABS):
            for gl in range(S5_SLAB_GROUPS):
                g = S5_SLAB_GROUPS * s + gl
                for ri in range(2):
                    col = ri * S5_SLAB_NC + gl * S5_STATE
                    for d in range(2):
                        src = ((l * 2 + d) * S5_GROUPS + g) * S5_CH
                        row = (d * S5_SLAB_GROUPS + gl) * S5_CH
                        wbu_ref[l, s, row:row + S5_CH, col:col + S5_STATE] = bbar[ri][src:src + S5_CH, :].astype(BF16)
                    src = (l * S5_GROUPS + g) * S5_STATE
                    wc_ref[l, s, col:col + S5_STATE, gl * S5_CH:(gl + 1) * S5_CH] = (
                        cmat[ri][src:src + S5_STATE, :].astype(BF16))


def _s5_params(s5_lam_re, s5_lam_im, s5_log_dt, s5_b_re, s5_b_im, s5_c_re, s5_c_im):
    depth = s5_lam_re.shape[0]
    full = (depth, 2, S5_GROUPS, S5_CH, S5_STATE)
    expand = lambda a: jnp.broadcast_to(a, full).reshape(-1, S5_STATE)
    small = lambda a: a.reshape(depth * 2 * S5_GROUPS, -1)
    ct = lambda a: jnp.swapaxes(a, -1, -2).reshape(-1, S5_CH)
    a_re, a_im, w_bu, w_c = pl.pallas_call(
        functools.partial(_zoh_kernel, depth=depth),
        out_shape=(jax.ShapeDtypeStruct((depth * 2 * S5_GROUPS, S5_STATE), F32),
                   jax.ShapeDtypeStruct((depth * 2 * S5_GROUPS, S5_STATE), F32),
                   jax.ShapeDtypeStruct((depth, S5_SLABS, 2 * LANES, S5_SLAB_COLS), BF16),
                   jax.ShapeDtypeStruct((depth, S5_SLABS, S5_SLAB_COLS, LANES), BF16)),
        name="s5_zoh",
    )(small(s5_lam_re), small(s5_lam_im), small(s5_log_dt),
      expand(s5_lam_re[:, :, :, None, :]), expand(s5_lam_im[:, :, :, None, :]), expand(s5_log_dt[:, :, :, None, None]),
      expand(jnp.swapaxes(s5_b_re, -1, -2)[:, None]), expand(jnp.swapaxes(s5_b_im, -1, -2)[:, None]),
      ct(s5_c_re), ct(s5_c_im))
    return a_re.reshape(depth, 2, S5_NC), a_im.reshape(depth, 2, S5_NC), w_bu, w_c


def _s5_state_to_cols(re, im):
    lead = re.shape[:-2]
    st = jnp.stack([re.reshape(lead + (S5_SLABS, S5_SLAB_NC)), im.reshape(lead + (S5_SLABS, S5_SLAB_NC))], axis=-2)
    return st.reshape(lead + (2 * S5_NC,))


def _s5_cols_to_state(cols):
    lead = cols.shape[:-1]
    st = cols.reshape(lead + (S5_SLABS, 2, S5_SLAB_NC))
    re = st[..., 0, :].reshape(lead + (S5_GROUPS, S5_STATE))
    im = st[..., 1, :].reshape(lead + (S5_GROUPS, S5_STATE))
    return re, im


_IN_OFFS = [int(o) for o in np.cumsum([0, S5_WIDTH, S5_WIDTH, GLA_KEY, GLA_KEY, GLA_WIDTH, GLA_WIDTH, GLA_RANK,
                                       D_MODEL, D_MODEL])]


def _split_w_in_kernel(u_ref, qkv_ref, code_ref, ga_ref, gb_ref, m_ref, proj_ref, gate_ref):
    rows = 0
    for src in (u_ref, qkv_ref, code_ref):
        proj_ref[rows:rows + src.shape[1], :] = src[0].astype(BF16)
        rows += src.shape[1]
    proj_ref[rows:, :] = jnp.zeros((PROJ_W - rows, proj_ref.shape[1]), BF16)
    rows = 0
    for src in (ga_ref, gb_ref, m_ref):
        gate_ref[rows:rows + src.shape[1], :] = src[0].astype(BF16)
        rows += src.shape[1]


def _split_w_in(w_in):
    o = _IN_OFFS
    depth = w_in.shape[0]
    wt = jnp.swapaxes(w_in, 1, 2)
    tn = D_MODEL // 2
    rows = lambda lo, hi: pl.BlockSpec((pl.Element(1), pl.Element(hi - lo), pl.Element(tn)),
                                       lambda l, j: (l, lo, j * tn))
    out = lambda n: pl.BlockSpec((None, n, tn), lambda l, j: (l, 0, j))
    return pl.pallas_call(
        _split_w_in_kernel,
        grid=(depth, D_MODEL // tn),
        in_specs=[rows(o[0], o[1]), rows(o[2], o[5]), rows(o[6], o[7]), rows(o[1], o[2]), rows(o[5], o[6]),
                  rows(o[7], o[9])],
        out_specs=(out(PROJ_W), out(GATE_W)),
        out_shape=(jax.ShapeDtypeStruct((depth, PROJ_W, D_MODEL), BF16),
                   jax.ShapeDtypeStruct((depth, GATE_W, D_MODEL), BF16)),
        compiler_params=pltpu.CompilerParams(dimension_semantics=("arbitrary", "arbitrary"),
                                             vmem_limit_bytes=VMEM_LIMIT),
        name="split_w_in",
    )(wt, wt, wt, wt, wt, wt)


def _proj_kernel(*refs, has_pe):
    if has_pe:
        x_ref, pe_ref, shift_ref, scale_ref, ng_ref, w_ref, u_ref, q_ref, k_ref, v_ref, gl_ref = refs
    else:
        x_ref, shift_ref, scale_ref, ng_ref, w_ref, u_ref, q_ref, k_ref, v_ref, gl_ref = refs
    nsb, tm, _ = x_ref.shape
    rg = max(1, tm // (2 * MXU_TILE))
    grp = [(i, slice(j * tm // rg, (j + 1) * tm // rg)) for i in range(nsb) for j in range(rg)]
    x = [x_ref[i, r, :] + pe_ref[r, :] if has_pe else x_ref[i, r, :] for i, r in grp]
    h = [_modulated_norm(v, ng_ref[...], 1.0 + scale_ref[...], shift_ref[...]) for v in x]
    p = [_bdot_t(v, w_ref[...]) for v in h]
    for (i, r), v in zip(grp, p):
        o = 0
        for ref, width in ((u_ref, S5_WIDTH), (q_ref, GLA_KEY), (k_ref, GLA_KEY), (v_ref, GLA_WIDTH), (gl_ref, LANES)):
            ref[i, r, :] = v[:, o:o + width].astype(ref.dtype)
            o += width


def _mod_spec(l, cond_row, part):
    return pl.BlockSpec((None, None, 1, D_MODEL), lambda b, j: (l, cond_row(b), 0, part))


def _layer_spec(l, shape):
    return pl.BlockSpec((None,) + shape, lambda b, j: (l,) + (0,) * len(shape))


def _proj_call(x, pe, mod, l, cond_row, ng, w, tm, nsb=1):
    nseq, length, _ = x.shape
    row = lambda width: pl.BlockSpec((nsb, tm, width), lambda b, j: (b, j, 0))
    in_specs = [row(D_MODEL)]
    args = [x]
    if pe is not None:
        in_specs.append(pl.BlockSpec((tm, D_MODEL), lambda b, j: (j, 0)))
        args.append(pe)
    in_specs += [_mod_spec(l, cond_row, 0), _mod_spec(l, cond_row, 1), _layer_spec(l, (1, D_MODEL)),
                 _layer_spec(l, (PROJ_W, D_MODEL))]
    args += [mod, mod, ng, w]
    sds = lambda width, dt: jax.ShapeDtypeStruct((nseq, length, width), dt)
    return pl.pallas_call(
        functools.partial(_proj_kernel, has_pe=pe is not None),
        grid=(nseq // nsb, length // tm),
        in_specs=in_specs,
        out_specs=(row(S5_WIDTH), row(GLA_KEY), row(GLA_KEY), row(GLA_WIDTH), row(LANES)),
        out_shape=(sds(S5_WIDTH, F32), sds(GLA_KEY, F32), sds(GLA_KEY, F32), sds(GLA_WIDTH, BF16), sds(LANES, F32)),
        compiler_params=pltpu.CompilerParams(dimension_semantics=("arbitrary", "arbitrary"),
                                             vmem_limit_bytes=VMEM_LIMIT),
        name="proj",
    )(*args)


def _s5_perms(nb, tc):
    r = nb * tc
    pf = np.zeros((2 * r, r), np.float32)
    pb = np.zeros((2 * r, r), np.float32)
    for t in range(tc):
        for b in range(nb):
            pf[t * 2 * nb + b, b * tc + t] = 1.0
            pb[t * 2 * nb + nb + b, b * tc + (tc - 1 - t)] = 1.0
    return pf, pb


def _s5_store_unpacked(y2, yf_ref, yb_ref, t_f, t_b, nb, tc):
    rs = 2 * nb
    for t in range(tc):
        yf_ref[:, t_f + t, :] = y2[t * rs:t * rs + nb]
        yb_ref[:, t_b + tc - 1 - t, :] = y2[t * rs + nb:(t + 1) * rs]


def _s5_kernel(*refs, nb, tc, tiles_per_pass):
    use_perm = nb % 8 != 0
    uf1_ref, ub1_ref, uf2_ref, ub2_ref, uf0_ref, ub0_ref = refs[:6]
    pf_ref, pb_ref = refs[6:8] if use_perm else (None, None)
    wbu_ref, wc_ref, are_ref, aim_ref, h0_ref, yf_ref, yb_ref, hfin_ref, bua_ref, bub_ref, hb_ref, st_ref = refs[-12:]
    i = pl.program_id(0)
    rows = nb * tc
    rs = 2 * nb

    def packed_inputs(uf_ref, ub_ref):
        if not use_perm:
            return tuple(jnp.concatenate([r[:, t, :] for t in range(tc)], axis=0).astype(BF16) for r in (uf_ref, ub_ref))
        uf = uf_ref[...].reshape(rows, S5_WIDTH).astype(BF16)
        ub = ub_ref[...].reshape(rows, S5_WIDTH).astype(BF16)
        u2f = jnp.dot(pf_ref[...], uf, preferred_element_type=F32).astype(BF16)
        u2b = jnp.dot(pb_ref[...], ub, preferred_element_type=F32).astype(BF16)
        return [jnp.concatenate([u2f[:, s * LANES:(s + 1) * LANES], u2b[:, s * LANES:(s + 1) * LANES]], axis=1)
                for s in range(S5_SLABS)]

    def bu_slab(lhs, s, dst_ref):
        cols = slice(s * S5_SLAB_COLS, (s + 1) * S5_SLAB_COLS)
        if use_perm:
            dst_ref[:, cols] = jnp.dot(lhs[s], wbu_ref[s], preferred_element_type=F32)
            return
        uf, ub = lhs
        bf = jnp.dot(uf[:, s * LANES:(s + 1) * LANES], wbu_ref[s, :LANES, :], preferred_element_type=F32)
        bb = jnp.dot(ub[:, s * LANES:(s + 1) * LANES], wbu_ref[s, LANES:, :], preferred_element_type=F32)
        for t in range(tc):
            dst_ref[t * rs:t * rs + nb, cols] = bf[t * nb:(t + 1) * nb]
            dst_ref[t * rs + nb:(t + 1) * rs, cols] = bb[(tc - 1 - t) * nb:(tc - t) * nb]

    @pl.when(i == 0)
    def _():
        st_ref[...] = h0_ref[...]
        lhs0 = packed_inputs(uf0_ref, ub0_ref)
        for s in range(S5_SLABS):
            bu_slab(lhs0, s, bua_ref)

    half_step = functools.partial(_s5_half_step, wc_ref=wc_ref, are_ref=are_ref, aim_ref=aim_ref, hb_ref=hb_ref,
                                  st_ref=st_ref, bu_slab=bu_slab, rs=rs, tc=tc, tiles_per_pass=tiles_per_pass)
    y2 = half_step(bua_ref, bub_ref, packed_inputs(uf1_ref, ub1_ref))
    _s5_store_unpacked(y2, yf_ref, yb_ref, 0, tc, nb, tc)
    y2 = half_step(bub_ref, bua_ref, packed_inputs(uf2_ref, ub2_ref))
    _s5_store_unpacked(y2, yf_ref, yb_ref, tc, 0, nb, tc)
    hfin_ref[...] = st_ref[...]


def _s5_half_step(cur_ref, nxt_ref, lhs_next, *, wc_ref, are_ref, aim_ref, hb_ref, st_ref, bu_slab,
                  rs, tc, tiles_per_pass):
    slab_tiles = S5_SLAB_NC // LANES
    grp = max(1, 16 // rs)
    ys = []
    for s in range(S5_SLABS):
        bu_slab(lhs_next, s, nxt_ref)
        for c0 in range(slab_tiles * s, slab_tiles * (s + 1), tiles_per_pass):
            cts = list(range(c0, c0 + tiles_per_pass))
            cre = [(ct // slab_tiles) * S5_SLAB_COLS + (ct % slab_tiles) * LANES for ct in cts]
            cim = [c + S5_SLAB_NC for c in cre]
            a_re = [are_ref[:, ct * LANES:(ct + 1) * LANES] for ct in cts]
            a_im = [aim_ref[:, ct * LANES:(ct + 1) * LANES] for ct in cts]
            h_re = [st_ref[:, c:c + LANES] for c in cre]
            h_im = [st_ref[:, c:c + LANES] for c in cim]
            for t0 in range(0, tc, grp):
                out_re = [[] for _ in cts]
                out_im = [[] for _ in cts]
                for t in range(t0, t0 + grp):
                    r0 = t * rs
                    for j in range(len(cts)):
                        b_re = cur_ref[r0:r0 + rs, cre[j]:cre[j] + LANES]
                        b_im = cur_ref[r0:r0 + rs, cim[j]:cim[j] + LANES]
                        n_re = a_re[j] * h_re[j] - a_im[j] * h_im[j] + b_re
                        n_im = a_re[j] * h_im[j] + a_im[j] * h_re[j] + b_im
                        h_re[j], h_im[j] = n_re, n_im
                        out_re[j].append(n_re)
                        out_im[j].append(n_im)
                for j in range(len(cts)):
                    blk_re = out_re[j][0] if grp == 1 else jnp.concatenate(out_re[j], axis=0)
                    blk_im = out_im[j][0] if grp == 1 else jnp.concatenate(out_im[j], axis=0)
                    hb_ref[t0 * rs:(t0 + grp) * rs, cre[j]:cre[j] + LANES] = blk_re.astype(BF16)
                    hb_ref[t0 * rs:(t0 + grp) * rs, cim[j]:cim[j] + LANES] = blk_im.astype(BF16)
            for j in range(len(cts)):
                st_ref[:, cre[j]:cre[j] + LANES] = h_re[j]
                st_ref[:, cim[j]:cim[j] + LANES] = h_im[j]
        ys.append(jnp.dot(hb_ref[:, s * S5_SLAB_COLS:(s + 1) * S5_SLAB_COLS], wc_ref[s], preferred_element_type=F32))
    return jnp.concatenate(ys, axis=1)


def _s5_call(u, l, w_bu, w_c, are, aim, h0, tc, tiles_per_pass):
    nb, length, _ = u.shape
    n = length // tc
    rows = nb * tc
    const = lambda shape: pl.BlockSpec(shape, lambda i: (0,) * len(shape))
    layer = lambda shape: pl.BlockSpec((None,) + shape, lambda i: (l,) + (0,) * len(shape))
    blk = lambda chunk: pl.BlockSpec((nb, tc, S5_WIDTH), lambda i: (0, chunk(i), 0))
    out_f = pl.BlockSpec((nb, 2 * tc, S5_WIDTH), lambda i: (0, i, 0))
    out_b = pl.BlockSpec((nb, 2 * tc, S5_WIDTH), lambda i: (0, n // 2 - 1 - i, 0))
    second = lambda i: jnp.minimum(2 * i + 2, n - 1)
    in_specs = [blk(lambda i: 2 * i + 1), blk(lambda i: n - 2 - 2 * i), blk(second), blk(lambda i: n - 1 - second(i)),
                blk(lambda i: 0), blk(lambda i: n - 1)]
    args = [u] * 6
    if nb % 8 != 0:
        pf, pb = _s5_perms(nb, tc)
        in_specs += [const((2 * rows, rows)), const((2 * rows, rows))]
        args += [jnp.asarray(pf, BF16), jnp.asarray(pb, BF16)]
    in_specs += [layer((S5_SLABS, 2 * LANES, S5_SLAB_COLS)), layer((S5_SLABS, S5_SLAB_COLS, LANES)),
                 layer((2 * nb, S5_NC)), layer((2 * nb, S5_NC)), const((2 * nb, 2 * S5_NC))]
    args += [w_bu, w_c, are, aim, h0]
    return pl.pallas_call(
        functools.partial(_s5_kernel, nb=nb, tc=tc, tiles_per_pass=tiles_per_pass),
        grid=(n // 2,),
        in_specs=in_specs,
        out_specs=(out_f, out_b, const((2 * nb, 2 * S5_NC))),
        out_shape=(jax.ShapeDtypeStruct(u.shape, F32), jax.ShapeDtypeStruct(u.shape, F32),
                   jax.ShapeDtypeStruct((2 * nb, 2 * S5_NC), F32)),
        scratch_shapes=[pltpu.VMEM((2 * rows, 2 * S5_NC), F32), pltpu.VMEM((2 * rows, 2 * S5_NC), F32),
                        pltpu.VMEM((2 * rows, 2 * S5_NC), BF16), pltpu.VMEM((2 * nb, 2 * S5_NC), F32)],
        compiler_params=pltpu.CompilerParams(dimension_semantics=("arbitrary",), vmem_limit_bytes=VMEM_LIMIT),
        name="s5_scan",
    )(*args)


def _gla_kernel(*refs, length, has_s0, nsq, cpi):
    q_ref, k_ref, v_ref, gl_ref, wg_ref, bg_ref, ng_ref = refs[:7]
    s0_ref = refs[7] if has_s0 else None
    o_ref, sfin_ref, sf_ref, sb_ref = refs[-4:]
    c = GLA_CHUNK
    n = length // c
    ri = lax.broadcasted_iota(jnp.int32, (c, c), 0)
    ci = lax.broadcasted_iota(jnp.int32, (c, c), 1)
    lane = lax.broadcasted_iota(jnp.int32, (1, GLA_KEY), 1)
    head_mask = [(lane >= h * GLA_DK) & (lane < (h + 1) * GLA_DK) for h in range(GLA_HEADS)]
    ng = ng_ref[...]
    causal = [ci <= ri, ci >= ri]
    tri = [m.astype(BF16) for m in causal]
    end = (c - 1, 0)
    chains = [(sq, d, j) for j in range(cpi) for sq in range(nsq) for d in (0, 1)]
    key_rows = lambda h: slice(h * GLA_DK, (h + 1) * GLA_DK)
    val_cols = lambda h: slice(h * GLA_DV, (h + 1) * GLA_DV)

    sb_ref[...] = jnp.zeros_like(sb_ref)
    for sq in range(nsq):
        for d in (0, 1):
            for h in range(GLA_HEADS):
                if has_s0:
                    sf_ref[sq, d, h] = s0_ref[sq, d, h]
                    sb_ref[sq, d, key_rows(h), val_cols(h)] = s0_ref[sq, d, h].astype(BF16)
                else:
                    sf_ref[sq, d, h] = jnp.zeros((GLA_DK, GLA_DV), F32)

    def chunk_group(r):
        row = lambda d, j: r[d][j]
        qc = [q_ref[sq, pl.ds(row(d, j), c), :] * (GLA_DK ** -0.5) for sq, d, j in chains]
        kc = [k_ref[sq, pl.ds(row(d, j), c), :] for sq, d, j in chains]
        vc = [v_ref[sq, pl.ds(row(d, j), c), :] for sq, d, j in chains]
        logits = [_bdot(gl_ref[sq, pl.ds(row(d, j), c), :], wg_ref[d]) + bg_ref[d] for sq, d, j in chains]
        g = [_split2(_log_sigmoid(x) * (1.0 / GLA_GATE_NORM)) for x in logits]
        b = [jnp.dot(tri[d], hi, preferred_element_type=F32) + jnp.dot(tri[d], lo, preferred_element_type=F32)
             for (sq, d, j), (hi, lo) in zip(chains, g)]
        bt = [x.T for x in b]
        kt = [x.T for x in kc]
        b_mid = [x[c // 2:c // 2 + 1, :] for x in b]
        bt_mid = [x[:, c // 2:c // 2 + 1] for x in bt]
        bt_end = [x[:, end[d]:end[d] + 1] for (sq, d, j), x in zip(chains, bt)]
        q_in = [(x * jnp.exp(y)).astype(BF16) for x, y in zip(qc, b)]
        q_mid = [(x * jnp.exp(y - m)).astype(BF16) for x, y, m in zip(qc, b, b_mid)]
        k_mid_t = [(x * jnp.exp(m - y)).astype(BF16) for x, y, m in zip(kt, bt, bt_mid)]
        k_end_t = [(x * jnp.exp(e - y)).astype(BF16) for x, y, e in zip(kt, bt, bt_end)]
        q_st = [jnp.concatenate([jnp.where(head_mask[h], x, jnp.zeros_like(x)) for h in range(GLA_HEADS)], axis=0)
                for x in q_mid]
        a = [jnp.dot(x, y, preferred_element_type=F32) for x, y in zip(q_st, k_mid_t)]
        parts = [[] for _ in chains]
        for h in range(GLA_HEADS):
            for i, (sq, d, j) in enumerate(chains):
                a_h = jnp.where(causal[d], a[i][h * c:(h + 1) * c, :], 0.0).astype(BF16)
                parts[i].append(jnp.dot(a_h, vc[i][:, val_cols(h)], preferred_element_type=F32))
        upd = [[jnp.dot(k_end_t[i][key_rows(h), :], vc[i][:, val_cols(h)], preferred_element_type=F32)
                for h in range(GLA_HEADS)] for i in range(len(chains))]
        o_inter = [None] * len(chains)
        for i, (sq, d, j) in enumerate(chains):
            o_inter[i] = jnp.dot(q_in[i], sb_ref[sq, d], preferred_element_type=F32)
            for h in range(GLA_HEADS):
                s_new = jnp.exp(bt_end[i][key_rows(h), :]) * sf_ref[sq, d, h] + upd[i][h]
                sf_ref[sq, d, h] = s_new
                sb_ref[sq, d, key_rows(h), val_cols(h)] = s_new.astype(BF16)
        return [x + jnp.concatenate(p, axis=1) for x, p in zip(o_inter, parts)]

    def finish(o, sq, r0):
        o = o + o_ref[sq, pl.ds(r0, c), :]
        normed = []
        for h in range(GLA_HEADS):
            oh = o[:, val_cols(h)]
            ms = jnp.mean(oh * oh, axis=-1, keepdims=True)
            normed.append(oh * lax.rsqrt(ms + EPS))
        o_ref[sq, pl.ds(r0, c), :] = jnp.concatenate(normed, axis=1) * ng

    def rows_of(step):
        fwd = [pl.multiple_of((step * cpi + j) * c, c) for j in range(cpi)]
        bwd = [pl.multiple_of((n - 1 - step * cpi - j) * c, c) for j in range(cpi)]
        return fwd, bwd

    def first_half(step, carry):
        r = rows_of(step)
        for (sq, d, j), o in zip(chains, chunk_group(r)):
            o_ref[sq, pl.ds(r[d][j], c), :] = o
        return carry

    def second_half(step, carry):
        r = rows_of(step)
        for (sq, d, j), o in zip(chains, chunk_group(r)):
            finish(o, sq, r[d][j])
        return carry

    steps = n // cpi
    lax.fori_loop(0, steps // 2, first_half, 0)
    lax.fori_loop(steps // 2, steps, second_half, 0)
    for sq in range(nsq):
        for d in (0, 1):
            for h in range(GLA_HEADS):
                sfin_ref[sq, d, h] = sf_ref[sq, d, h]


def _gla_call(q, k, v, gl, l, wg, bg, ng, s0, nsq, cpi):
    nseq, length, _ = q.shape
    assert (length // GLA_CHUNK) % (2 * cpi) == 0 and nseq % nsq == 0
    seq = lambda width: pl.BlockSpec((nsq, length, width), lambda b: (b, 0, 0))
    layer = lambda shape: pl.BlockSpec((None,) + shape, lambda b: (l,) + (0,) * len(shape))
    in_specs = [seq(GLA_KEY), seq(GLA_KEY), seq(GLA_WIDTH), seq(LANES),
                layer((2, LANES, GLA_KEY)), layer((2, 1, GLA_KEY)), layer((1, GLA_WIDTH))]
    args = [q, k, v, gl, wg, bg, ng]
    if s0 is not None:
        in_specs.append(pl.BlockSpec((nsq, None, 2, GLA_HEADS, GLA_DK, GLA_DV), lambda b: (b, l, 0, 0, 0, 0)))
        args.append(s0)
    return pl.pallas_call(
        functools.partial(_gla_kernel, length=length, has_s0=s0 is not None, nsq=nsq, cpi=cpi),
        grid=(nseq // nsq,),
        in_specs=in_specs,
        out_specs=(seq(GLA_WIDTH), pl.BlockSpec((nsq, 2, GLA_HEADS, GLA_DK, GLA_DV), lambda b: (b, 0, 0, 0, 0))),
        out_shape=(jax.ShapeDtypeStruct((nseq, length, GLA_WIDTH), F32),
                   jax.ShapeDtypeStruct((nseq, 2, GLA_HEADS, GLA_DK, GLA_DV), F32)),
        scratch_shapes=[pltpu.VMEM((nsq, 2, GLA_HEADS, GLA_DK, GLA_DV), F32),
                        pltpu.VMEM((nsq, 2, GLA_KEY, GLA_WIDTH), BF16)],
        compiler_params=pltpu.CompilerParams(dimension_semantics=("arbitrary",), vmem_limit_bytes=VMEM_LIMIT),
        name="gla",
    )(*args)


def _mix_kernel(*refs, has_pe, last, row_groups):
    if has_pe:
        (x_ref, pe_ref, yf_ref, yb_ref, u_ref, o_ref, shift_ref, scale_ref, gate_ref, ng_ref, wg_ref, d_ref,
         wglu_ref, bglu_ref, wpa_ref, wpb_ref, wo_ref, fng_ref, out_ref) = refs
    else:
        (x_ref, yf_ref, yb_ref, u_ref, o_ref, shift_ref, scale_ref, gate_ref, ng_ref, wg_ref, d_ref,
         wglu_ref, bglu_ref, wpa_ref, wpb_ref, wo_ref, fng_ref, out_ref) = refs
    nsb, tm, _ = x_ref.shape
    grp = [(i, slice(j * tm // row_groups, (j + 1) * tm // row_groups)) for i in range(nsb) for j in range(row_groups)]
    x = [x_ref[i, r, :] + pe_ref[r, :] if has_pe else x_ref[i, r, :] for i, r in grp]
    h = [_modulated_norm(v, ng_ref[...], 1.0 + scale_ref[...], shift_ref[...]).astype(BF16) for v in x]
    gts = [_bdot_t(v, wg_ref[...]) for v in h]
    y = [_gelu_tanh(yf_ref[i, r, :] + yb_ref[i, r, :] + d_ref[...] * u_ref[i, r, :]) for i, r in grp]
    glu = [_bdot(v, wglu_ref[...]) for v in y]
    y = [v * _sigmoid(g + bglu_ref[...]) for v, g in zip(y, glu)]
    y_a = [v * _silu(g[:, :S5_WIDTH]) for v, g in zip(y, gts)]
    y_b = [o_ref[i, r, :] * _silu(g[:, S5_WIDTH:2 * S5_WIDTH]) for (i, r), g in zip(grp, gts)]
    p_a = [_bdot(v, wpa_ref[...]) for v in y_a]
    p_b = [_bdot(v, wpb_ref[...]) for v in y_b]
    merged = [_sigmoid(g[:, 2 * S5_WIDTH:2 * S5_WIDTH + D_MODEL]) * a + _sigmoid(g[:, 2 * S5_WIDTH + D_MODEL:]) * b
              for g, a, b in zip(gts, p_a, p_b)]
    xn = [v + gate_ref[...] * _bdot(m, wo_ref[...]) for v, m in zip(x, merged)]
    for (i, r), v in zip(grp, xn):
        if last:
            ms = jnp.mean(v * v, axis=-1, keepdims=True)
            v = v * lax.rsqrt(ms + EPS) * fng_ref[...]
        out_ref[i, r, :] = v


def _mix_call(x, pe, yf, yb, u, o, mod, l, cond_row, ng, wg, s5_d, wglu, bglu, wpa, wpb, wo, fng, tm, last, nsb=1):
    nseq, length, _ = x.shape
    row = lambda width: pl.BlockSpec((nsb, tm, width), lambda b, j: (b, j, 0))
    in_specs = [row(D_MODEL)]
    args = [x]
    if pe is not None:
        in_specs.append(pl.BlockSpec((tm, D_MODEL), lambda b, j: (j, 0)))
        args.append(pe)
    in_specs += [row(S5_WIDTH), row(S5_WIDTH), row(S5_WIDTH), row(GLA_WIDTH),
                 _mod_spec(l, cond_row, 0), _mod_spec(l, cond_row, 1), _mod_spec(l, cond_row, 2),
                 _layer_spec(l, (1, D_MODEL)), _layer_spec(l, (GATE_W, D_MODEL)), _layer_spec(l, (1, S5_WIDTH)),
                 _layer_spec(l, (S5_WIDTH, S5_WIDTH)), _layer_spec(l, (1, S5_WIDTH)),
                 _layer_spec(l, (S5_WIDTH, D_MODEL)), _layer_spec(l, (GLA_WIDTH, D_MODEL)),
                 _layer_spec(l, (D_MODEL, D_MODEL)), pl.BlockSpec((1, D_MODEL), lambda b, j: (0, 0))]
    args += [yf, yb, u, o, mod, mod, mod, ng, wg, s5_d, wglu, bglu, wpa, wpb, wo, fng]
    return pl.pallas_call(
        functools.partial(_mix_kernel, has_pe=pe is not None, last=last, row_groups=max(1, tm // MXU_TILE)),
        grid=(nseq // nsb, length // tm),
        in_specs=in_specs,
        out_specs=row(D_MODEL),
        out_shape=jax.ShapeDtypeStruct(x.shape, F32),
        compiler_params=pltpu.CompilerParams(dimension_semantics=("arbitrary", "arbitrary"),
                                             vmem_limit_bytes=VMEM_LIMIT),
        name="mix",
    )(*args)


def _grid_pos_embed(length, dim):
    rows = length // GRID_W
    quarter = dim // 4
    freqs = jnp.exp(-math.log(10000.0) * jnp.arange(quarter, dtype=F32) / quarter)

    def sincos(pos):
        ang = pos.astype(F32)[:, None] * freqs[None, :]
        return jnp.concatenate([jnp.sin(ang), jnp.cos(ang)], axis=-1)

    er = sincos(jnp.arange(rows))
    ec = sincos(jnp.arange(GRID_W))
    pe = jnp.concatenate([jnp.broadcast_to(er[:, None, :], (rows, GRID_W, dim // 2)),
                          jnp.broadcast_to(ec[None, :, :], (rows, GRID_W, dim // 2))], axis=-1)
    return pe.reshape(rows * GRID_W, dim)


def kernel(x_prompt, x_sample, c, state_s5_re, state_s5_im, state_gla, c_ctx, norm_g, w_mod, b_mod, w_in, gla_wg_up,
           gla_bg, gla_norm_g, s5_lam_re, s5_lam_im, s5_log_dt, s5_b_re, s5_b_im, s5_c_re, s5_c_im, s5_d, w_glu,
           b_glu, w_pa, w_pb, w_o, final_norm_g):
    depth = w_in.shape[0]
    bp, lp, _ = x_prompt.shape
    bs, ls, _ = x_sample.shape

    cond8 = jnp.zeros((8, D_MODEL), F32).at[0].set(c_ctx).at[1:1 + bs].set(c)
    mod = _mod_call(cond8, w_mod, b_mod)

    a_re, a_im, w_bu, w_c = _s5_params(s5_lam_re, s5_lam_im, s5_log_dt, s5_b_re, s5_b_im, s5_c_re, s5_c_im)

    w_proj, w_gate = _split_w_in(w_in)
    wg_up = jnp.concatenate([gla_wg_up, jnp.zeros((depth, 2, LANES - GLA_RANK, GLA_KEY), F32)], axis=2).astype(BF16)
    w_glu_b, w_pa_b, w_pb_b, w_o_b = (w.astype(BF16) for w in (w_glu, w_pa, w_pb, w_o))

    vec = lambda a: a[:, None, :]
    mod = mod.reshape(depth, 8, 1, 3 * D_MODEL)
    ng, gng, bg = vec(norm_g), vec(gla_norm_g), gla_bg[:, :, None, :]
    common = (ng, w_gate, vec(s5_d), w_glu_b, vec(b_glu), w_pa_b, w_pb_b, w_o_b, final_norm_g[None])
    are_p, aim_p = jnp.repeat(a_re, bp, axis=1), jnp.repeat(a_im, bp, axis=1)
    are_s, aim_s = jnp.repeat(a_re, bs, axis=1), jnp.repeat(a_im, bs, axis=1)
    ctx_row = lambda b: 0
    cond_row = lambda b: b + 1

    pe = _grid_pos_embed(ls, D_MODEL)
    zero_h0 = jnp.zeros((2 * bp, 2 * S5_NC), F32)
    xp, xs = x_prompt, x_sample
    new_re, new_im, new_gla = [], [], []
    for l in range(depth):
        last = l == depth - 1

        u, q, k, v, gl = _proj_call(xp, None, mod, l, ctx_row, ng, w_proj, tm=lp, nsb=4)
        yf, yb, hfin = _s5_call(u, l, w_bu, w_c, are_p, aim_p, zero_h0, tc=16, tiles_per_pass=2)
        o, sfin = _gla_call(q, k, v, gl, l, wg_up, bg, gng, None, nsq=4, cpi=1)
        new_gla.append(sfin)
        xp = _mix_call(xp, None, yf, yb, u, o, mod, l, ctx_row, *common, tm=lp, last=last, nsb=2)
        re_l, im_l = _s5_cols_to_state(hfin.reshape(2, bp, 2 * S5_NC))
        new_re.append(jnp.swapaxes(re_l, 0, 1))
        new_im.append(jnp.swapaxes(im_l, 0, 1))

        pe_l = pe if l == 0 else None
        u, q, k, v, gl = _proj_call(xs, pe_l, mod, l, cond_row, ng, w_proj, tm=1024)
        h0 = _s5_state_to_cols(jnp.swapaxes(state_s5_re[:, l], 0, 1), jnp.swapaxes(state_s5_im[:, l], 0, 1))
        yf, yb, _ = _s5_call(u, l, w_bu, w_c, are_s, aim_s, h0.reshape(2 * bs, 2 * S5_NC), tc=32, tiles_per_pass=4)
        o, _ = _gla_call(q, k, v, gl, l, wg_up, bg, gng, state_gla, nsq=2, cpi=2)
        xs = _mix_call(xs, pe_l, yf, yb, u, o, mod, l, cond_row, *common, tm=512, last=last)

    return (xp, xs, jnp.stack(new_re, axis=1), jnp.stack(new_im, axis=1), jnp.stack(new_gla, axis=1))
```

```python
import functools
import math

import numpy as np
import jax
import jax.numpy as jnp
from jax import lax
from jax.experimental import pallas as pl
from jax.experimental.pallas import tpu as pltpu

F32 = jnp.float32
BF16 = jnp.bfloat16

D_MODEL = 1024
GRID_W = 64
S5_WIDTH = 512
S5_GROUPS = 32
S5_CH = 16
S5_STATE = 64
S5_NC = S5_GROUPS * S5_STATE
S5_SLAB_GROUPS = 8
S5_SLABS = S5_GROUPS // S5_SLAB_GROUPS
S5_SLAB_NC = S5_SLAB_GROUPS * S5_STATE
S5_SLAB_COLS = 2 * S5_SLAB_NC
GLA_HEADS = 4
GLA_DK = 64
GLA_DV = 128
GLA_KEY = GLA_HEADS * GLA_DK
GLA_WIDTH = GLA_HEADS * GLA_DV
GLA_RANK = 16
GLA_GATE_NORM = 16.0
GLA_CHUNK = 128
EPS = 1e-6
LANES = 128
PROJ_W = S5_WIDTH + 2 * GLA_KEY + GLA_WIDTH + LANES
GATE_W = 2 * S5_WIDTH + 2 * D_MODEL
VMEM_LIMIT = 60 * 1024 * 1024
MXU_TILE = 256


def _bdot(a, b):
    return jnp.dot(a.astype(BF16), b.astype(BF16), preferred_element_type=F32)


def _bdot_t(a, bt):
    return lax.dot_general(a.astype(BF16), bt.astype(BF16), (((1,), (1,)), ((), ())), preferred_element_type=F32)


def _split2(x):
    hi = x.astype(BF16)
    lo = (x - hi.astype(F32)).astype(BF16)
    return hi, lo


def _sigmoid(x):
    return 1.0 / (1.0 + jnp.exp(-x))


def _silu(x):
    return x * _sigmoid(x)


def _gelu_tanh(x):
    c = math.sqrt(2.0 / math.pi)
    return 0.5 * x * (1.0 + jnp.tanh(c * (x + 0.044715 * (x * x * x))))


def _exact_zero_of(x):
    bits = lax.bitcast_convert_type(x, jnp.int32)
    return lax.shift_right_logical(lax.shift_right_logical(bits, 16), 16).astype(F32)


def _log_sigmoid(x):
    return jnp.minimum(x, 0.0) - jnp.log(1.0 + jnp.exp(-jnp.abs(x)))


def _modulated_norm(x, ng, scale1p, shift):
    ms = jnp.mean(x * x, axis=-1, keepdims=True)
    return (x * lax.rsqrt(ms + EPS) * ng) * scale1p + shift


def _mod_kernel(cond_ref, w_ref, b_ref, out_ref):
    c = cond_ref[...]
    s_hi, s_lo = _split2(_silu(c))
    w_hi, w_lo = _split2(w_ref[0])
    acc = jnp.dot(s_hi, w_hi, preferred_element_type=F32)
    acc += jnp.dot(s_lo, w_hi, preferred_element_type=F32)
    acc += jnp.dot(s_hi, w_lo, preferred_element_type=F32)
    out_ref[0] = acc + b_ref[0]


def _mod_call(cond8, w_mod, b_mod):
    depth = w_mod.shape[0]
    nb = 1536
    return pl.pallas_call(
        _mod_kernel,
        grid=(depth, 3 * D_MODEL // nb),
        in_specs=[
            pl.BlockSpec((8, D_MODEL), lambda l, j: (0, 0)),
            pl.BlockSpec((1, D_MODEL, nb), lambda l, j: (l, 0, j)),
            pl.BlockSpec((1, 1, nb), lambda l, j: (l, 0, j)),
        ],
        out_specs=pl.BlockSpec((1, 8, nb), lambda l, j: (l, 0, j)),
        out_shape=jax.ShapeDtypeStruct((depth, 8, 3 * D_MODEL), F32),
        compiler_params=pltpu.CompilerParams(dimension_semantics=("arbitrary", "arbitrary")),
        name="adaln_mod",
    )(cond8, w_mod, b_mod.reshape(depth, 1, 3 * D_MODEL))


def _zoh(lre, lim, ldt):
    dt = jnp.exp(ldt)
    mag = jnp.exp(lre * dt)
    a_re = mag * jnp.cos(lim * dt)
    a_im = mag * jnp.sin(lim * dt)
    n_re = a_re - 1.0
    inv = 1.0 / (lre * lre + lim * lim)
    return a_re, a_im, (n_re * lre + a_im * lim) * inv, (a_im * lre - n_re * lim) * inv


def _zoh_kernel(lre_ref, lim_ref, ldt_ref, lre_x_ref, lim_x_ref, ldt_x_ref, bre_ref, bim_ref, cre_ref, cim_ref,
                are_ref, aim_ref, wbu_ref, wc_ref, *, depth):
    a_re, a_im, _, _ = _zoh(lre_ref[...], lim_ref[...], ldt_ref[...])
    are_ref[...] = a_re
    aim_ref[...] = a_im
    _, _, c_re, c_im = _zoh(lre_x_ref[...], lim_x_ref[...], ldt_x_ref[...])
    bre = bre_ref[...]
    bim = bim_ref[...]
    bbar = [c_re * bre - c_im * bim, c_re * bim + c_im * bre]
    cmat = [cre_ref[...], -cim_ref[...]]
    wbu_ref[...] = jnp.zeros_like(wbu_ref)
    wc_ref[...] = jnp.zeros_like(wc_ref)
    for l in range(depth):
        for s in range(S5_SLABS):
            for gl in range(S5_SLAB_GROUPS):
                g = S5_SLAB_GROUPS * s + gl
                for ri in range(2):
                    col = ri * S5_SLAB_NC + gl * S5_STATE
                    for d in range(2):
                        src = ((l * 2 + d) * S5_GROUPS + g) * S5_CH
                        row = (d * S5_SLAB_GROUPS + gl) * S5_CH
                        wbu_ref[l, s, row:row + S5_CH, col:col + S5_STATE] = bbar[ri][src:src + S5_CH, :].astype(BF16)
                    src = (l * S5_GROUPS + g) * S5_STATE
                    wc_ref[l, s, col:col + S5_STATE, gl * S5_CH:(gl + 1) * S5_CH] = (
                        cmat[ri][src:src + S5_STATE, :].astype(BF16))


def _s5_params(s5_lam_re, s5_lam_im, s5_log_dt, s5_b_re, s5_b_im, s5_c_re, s5_c_im):
    depth = s5_lam_re.shape[0]
    full = (depth, 2, S5_GROUPS, S5_CH, S5_STATE)
    expand = lambda a: jnp.broadcast_to(a, full).reshape(-1, S5_STATE)
    small = lambda a: a.reshape(depth * 2 * S5_GROUPS, -1)
    ct = lambda a: jnp.swapaxes(a, -1, -2).reshape(-1, S5_CH)
    a_re, a_im, w_bu, w_c = pl.pallas_call(
        functools.partial(_zoh_kernel, depth=depth),
        out_shape=(jax.ShapeDtypeStruct((depth * 2 * S5_GROUPS, S5_STATE), F32),
                   jax.ShapeDtypeStruct((depth * 2 * S5_GROUPS, S5_STATE), F32),
                   jax.ShapeDtypeStruct((depth, S5_SLABS, 2 * LANES, S5_SLAB_COLS), BF16),
                   jax.ShapeDtypeStruct((depth, S5_SLABS, S5_SLAB_COLS, LANES), BF16)),
        name="s5_zoh",
    )(small(s5_lam_re), small(s5_lam_im), small(s5_log_dt),
      expand(s5_lam_re[:, :, :, None, :]), expand(s5_lam_im[:, :, :, None, :]), expand(s5_log_dt[:, :, :, None, None]),
      expand(jnp.swapaxes(s5_b_re, -1, -2)[:, None]), expand(jnp.swapaxes(s5_b_im, -1, -2)[:, None]),
      ct(s5_c_re), ct(s5_c_im))
    return a_re.reshape(depth, 2, S5_NC), a_im.reshape(depth, 2, S5_NC), w_bu, w_c


def _s5_state_to_cols(re, im):
    lead = re.shape[:-2]
    st = jnp.stack([re.reshape(lead + (S5_SLABS, S5_SLAB_NC)), im.reshape(lead + (S5_SLABS, S5_SLAB_NC))], axis=-2)
    return st.reshape(lead + (2 * S5_NC,))


def _s5_cols_to_state(cols):
    lead = cols.shape[:-1]
    st = cols.reshape(lead + (S5_SLABS, 2, S5_SLAB_NC))
    re = st[..., 0, :].reshape(lead + (S5_GROUPS, S5_STATE))
    im = st[..., 1, :].reshape(lead + (S5_GROUPS, S5_STATE))
    return re, im


_IN_OFFS = [int(o) for o in np.cumsum([0, S5_WIDTH, S5_WIDTH, GLA_KEY, GLA_KEY, GLA_WIDTH, GLA_WIDTH, GLA_RANK,
                                       D_MODEL, D_MODEL])]


def _split_w_in_kernel(u_ref, qkv_ref, code_ref, ga_ref, gb_ref, m_ref, proj_ref, gate_ref):
    rows = 0
    for src in (u_ref, qkv_ref, code_ref):
        proj_ref[rows:rows + src.shape[1], :] = src[0].astype(BF16)
        rows += src.shape[1]
    proj_ref[rows:, :] = jnp.zeros((PROJ_W - rows, proj_ref.shape[1]), BF16)
    rows = 0
    for src in (ga_ref, gb_ref, m_ref):
        gate_ref[rows:rows + src.shape[1], :] = src[0].astype(BF16)
        rows += src.shape[1]


def _split_w_in(w_in):
    o = _IN_OFFS
    depth = w_in.shape[0]
    wt = jnp.swapaxes(w_in, 1, 2)
    tn = D_MODEL // 2
    rows = lambda lo, hi: pl.BlockSpec((pl.Element(1), pl.Element(hi - lo), pl.Element(tn)),
                                       lambda l, j: (l, lo, j * tn))
    out = lambda n: pl.BlockSpec((None, n, tn), lambda l, j: (l, 0, j))
    return pl.pallas_call(
        _split_w_in_kernel,
        grid=(depth, D_MODEL // tn),
        in_specs=[rows(o[0], o[1]), rows(o[2], o[5]), rows(o[6], o[7]), rows(o[1], o[2]), rows(o[5], o[6]),
                  rows(o[7], o[9])],
        out_specs=(out(PROJ_W), out(GATE_W)),
        out_shape=(jax.ShapeDtypeStruct((depth, PROJ_W, D_MODEL), BF16),
                   jax.ShapeDtypeStruct((depth, GATE_W, D_MODEL), BF16)),
        compiler_params=pltpu.CompilerParams(dimension_semantics=("arbitrary", "arbitrary"),
                                             vmem_limit_bytes=VMEM_LIMIT),
        name="split_w_in",
    )(wt, wt, wt, wt, wt, wt)


def _proj_kernel(*refs, has_pe):
    if has_pe:
        x_ref, pe_ref, shift_ref, scale_ref, ng_ref, w_ref, u_ref, q_ref, k_ref, v_ref, gl_ref = refs
    else:
        x_ref, shift_ref, scale_ref, ng_ref, w_ref, u_ref, q_ref, k_ref, v_ref, gl_ref = refs
    nsb, tm, _ = x_ref.shape
    rg = max(1, tm // (2 * MXU_TILE))
    grp = [(i, slice(j * tm // rg, (j + 1) * tm // rg)) for i in range(nsb) for j in range(rg)]
    x = [x_ref[i, r, :] + pe_ref[r, :] if has_pe else x_ref[i, r, :] for i, r in grp]
    h = [_modulated_norm(v, ng_ref[...], 1.0 + scale_ref[...], shift_ref[...]) for v in x]
    p = [_bdot_t(v, w_ref[...]) for v in h]
    for (i, r), v in zip(grp, p):
        o = 0
        for ref, width in ((u_ref, S5_WIDTH), (q_ref, GLA_KEY), (k_ref, GLA_KEY), (v_ref, GLA_WIDTH), (gl_ref, LANES)):
            ref[i, r, :] = v[:, o:o + width].astype(ref.dtype)
            o += width


def _mod_spec(l, cond_row, part):
    return pl.BlockSpec((None, None, 1, D_MODEL), lambda b, j: (l, cond_row(b), 0, part))


def _layer_spec(l, shape):
    return pl.BlockSpec((None,) + shape, lambda b, j: (l,) + (0,) * len(shape))


def _proj_call(x, pe, mod, l, cond_row, ng, w, tm, nsb=1):
    nseq, length, _ = x.shape
    row = lambda width: pl.BlockSpec((nsb, tm, width), lambda b, j: (b, j, 0))
    in_specs = [row(D_MODEL)]
    args = [x]
    if pe is not None:
        in_specs.append(pl.BlockSpec((tm, D_MODEL), lambda b, j: (j, 0)))
        args.append(pe)
    in_specs += [_mod_spec(l, cond_row, 0), _mod_spec(l, cond_row, 1), _layer_spec(l, (1, D_MODEL)),
                 _layer_spec(l, (PROJ_W, D_MODEL))]
    args += [mod, mod, ng, w]
    sds = lambda width, dt: jax.ShapeDtypeStruct((nseq, length, width), dt)
    return pl.pallas_call(
        functools.partial(_proj_kernel, has_pe=pe is not None),
        grid=(nseq // nsb, length // tm),
        in_specs=in_specs,
        out_specs=(row(S5_WIDTH), row(GLA_KEY), row(GLA_KEY), row(GLA_WIDTH), row(LANES)),
        out_shape=(sds(S5_WIDTH, F32), sds(GLA_KEY, F32), sds(GLA_KEY, F32), sds(GLA_WIDTH, BF16), sds(LANES, F32)),
        compiler_params=pltpu.CompilerParams(dimension_semantics=("arbitrary", "arbitrary"),
                                             vmem_limit_bytes=VMEM_LIMIT),
        name="proj",
    )(*args)


def _s5_perms(nb, tc):
    r = nb * tc
    pf = np.zeros((2 * r, r), np.float32)
    pb = np.zeros((2 * r, r), np.float32)
    for t in range(tc):
        for b in range(nb):
            pf[t * 2 * nb + b, b * tc + t] = 1.0
            pb[t * 2 * nb + nb + b, b * tc + (tc - 1 - t)] = 1.0
    return pf, pb


def _s5_store_unpacked(y2, yf_ref, yb_ref, t_f, t_b, nb, tc):
    rs = 2 * nb
    for t in range(tc):
        yf_ref[:, t_f + t, :] = y2[t * rs:t * rs + nb]
        yb_ref[:, t_b + tc - 1 - t, :] = y2[t * rs + nb:(t + 1) * rs]


def _s5_kernel(*refs, nb, tc, tiles_per_pass):
    use_perm = nb % 8 != 0
    uf1_ref, ub1_ref, uf2_ref, ub2_ref, uf0_ref, ub0_ref = refs[:6]
    pf_ref, pb_ref = refs[6:8] if use_perm else (None, None)
    wbu_ref, wc_ref, are_ref, aim_ref, h0_ref, yf_ref, yb_ref, hfin_ref, bua_ref, bub_ref, hb_ref, st_ref = refs[-12:]
    i = pl.program_id(0)
    rows = nb * tc
    rs = 2 * nb

    def packed_inputs(uf_ref, ub_ref):
        if not use_perm:
            return tuple(jnp.concatenate([r[:, t, :] for t in range(tc)], axis=0).astype(BF16) for r in (uf_ref, ub_ref))
        uf = uf_ref[...].reshape(rows, S5_WIDTH).astype(BF16)
        ub = ub_ref[...].reshape(rows, S5_WIDTH).astype(BF16)
        u2f = jnp.dot(pf_ref[...], uf, preferred_element_type=F32).astype(BF16)
        u2b = jnp.dot(pb_ref[...], ub, preferred_element_type=F32).astype(BF16)
        return [jnp.concatenate([u2f[:, s * LANES:(s + 1) * LANES], u2b[:, s * LANES:(s + 1) * LANES]], axis=1)
                for s in range(S5_SLABS)]

    def bu_slab(lhs, s, dst_ref):
        cols = slice(s * S5_SLAB_COLS, (s + 1) * S5_SLAB_COLS)
        if use_perm:
            dst_ref[:, cols] = jnp.dot(lhs[s], wbu_ref[s], preferred_element_type=F32)
            return
        uf, ub = lhs
        bf = jnp.dot(uf[:, s * LANES:(s + 1) * LANES], wbu_ref[s, :LANES, :], preferred_element_type=F32)
        bb = jnp.dot(ub[:, s * LANES:(s + 1) * LANES], wbu_ref[s, LANES:, :], preferred_element_type=F32)
        for t in range(tc):
            dst_ref[t * rs:t * rs + nb, cols] = bf[t * nb:(t + 1) * nb]
            dst_ref[t * rs + nb:(t + 1) * rs, cols] = bb[(tc - 1 - t) * nb:(tc - t) * nb]

    @pl.when(i == 0)
    def _():
        st_ref[...] = h0_ref[...]
        lhs0 = packed_inputs(uf0_ref, ub0_ref)
        for s in range(S5_SLABS):
            bu_slab(lhs0, s, bua_ref)

    half_step = functools.partial(_s5_half_step, wc_ref=wc_ref, are_ref=are_ref, aim_ref=aim_ref, hb_ref=hb_ref,
                                  st_ref=st_ref, bu_slab=bu_slab, rs=rs, tc=tc, tiles_per_pass=tiles_per_pass)
    y2, tail = half_step(bua_ref, bub_ref, packed_inputs(uf1_ref, ub1_ref))
    _s5_store_unpacked(y2, yf_ref, yb_ref, 0, tc, nb, tc)
    y2, _ = half_step(bub_ref, bua_ref, packed_inputs(uf2_ref, ub2_ref), after=tail)
    _s5_store_unpacked(y2, yf_ref, yb_ref, tc, 0, nb, tc)
    hfin_ref[...] = st_ref[...]


def _s5_half_step(cur_ref, nxt_ref, lhs_next, *, wc_ref, are_ref, aim_ref, hb_ref, st_ref, bu_slab,
                  rs, tc, tiles_per_pass, after=None):
    slab_tiles = S5_SLAB_NC // LANES
    grp = max(1, 16 // rs)
    ys = []
    for s in range(S5_SLABS):
        bu_slab(lhs_next, s, nxt_ref)
        for c0 in range(slab_tiles * s, slab_tiles * (s + 1), tiles_per_pass):
            cts = list(range(c0, c0 + tiles_per_pass))
            cre = [(ct // slab_tiles) * S5_SLAB_COLS + (ct % slab_tiles) * LANES for ct in cts]
            cim = [c + S5_SLAB_NC for c in cre]
            a_re = [are_ref[:, ct * LANES:(ct + 1) * LANES] for ct in cts]
            a_im = [aim_ref[:, ct * LANES:(ct + 1) * LANES] for ct in cts]
            h_re = [st_ref[:, c:c + LANES] for c in cre]
            h_im = [st_ref[:, c:c + LANES] for c in cim]
            if after is not None and c0 == 0:
                h_re[0] = h_re[0] + _exact_zero_of(after)
            for t0 in range(0, tc, grp):
                out_re = [[] for _ in cts]
                out_im = [[] for _ in cts]
                for t in range(t0, t0 + grp):
                    r0 = t * rs
                    for j in range(len(cts)):
                        b_re = cur_ref[r0:r0 + rs, cre[j]:cre[j] + LANES]
                        b_im = cur_ref[r0:r0 + rs, cim[j]:cim[j] + LANES]
                        n_re = a_re[j] * h_re[j] - a_im[j] * h_im[j] + b_re
                        n_im = a_re[j] * h_im[j] + a_im[j] * h_re[j] + b_im
                        h_re[j], h_im[j] = n_re, n_im
                        out_re[j].append(n_re)
                        out_im[j].append(n_im)
                for j in range(len(cts)):
                    blk_re = out_re[j][0] if grp == 1 else jnp.concatenate(out_re[j], axis=0)
                    blk_im = out_im[j][0] if grp == 1 else jnp.concatenate(out_im[j], axis=0)
                    hb_ref[t0 * rs:(t0 + grp) * rs, cre[j]:cre[j] + LANES] = blk_re.astype(BF16)
                    hb_ref[t0 * rs:(t0 + grp) * rs, cim[j]:cim[j] + LANES] = blk_im.astype(BF16)
            for j in range(len(cts)):
                st_ref[:, cre[j]:cre[j] + LANES] = h_re[j]
                st_ref[:, cim[j]:cim[j] + LANES] = h_im[j]
        ys.append(jnp.dot(hb_ref[:, s * S5_SLAB_COLS:(s + 1) * S5_SLAB_COLS], wc_ref[s], preferred_element_type=F32))
    return jnp.concatenate(ys, axis=1), h_re[-1]


def _s5_call(u, l, w_bu, w_c, are, aim, h0, tc, tiles_per_pass):
    nb, length, _ = u.shape
    n = length // tc
    rows = nb * tc
    const = lambda shape: pl.BlockSpec(shape, lambda i: (0,) * len(shape))
    layer = lambda shape: pl.BlockSpec((None,) + shape, lambda i: (l,) + (0,) * len(shape))
    blk = lambda chunk: pl.BlockSpec((nb, tc, S5_WIDTH), lambda i: (0, chunk(i), 0))
    out_f = pl.BlockSpec((nb, 2 * tc, S5_WIDTH), lambda i: (0, i, 0))
    out_b = pl.BlockSpec((nb, 2 * tc, S5_WIDTH), lambda i: (0, n // 2 - 1 - i, 0))
    second = lambda i: jnp.minimum(2 * i + 2, n - 1)
    in_specs = [blk(lambda i: 2 * i + 1), blk(lambda i: n - 2 - 2 * i), blk(second), blk(lambda i: n - 1 - second(i)),
                blk(lambda i: 0), blk(lambda i: n - 1)]
    args = [u] * 6
    if nb % 8 != 0:
        pf, pb = _s5_perms(nb, tc)
        in_specs += [const((2 * rows, rows)), const((2 * rows, rows))]
        args += [jnp.asarray(pf, BF16), jnp.asarray(pb, BF16)]
    in_specs += [layer((S5_SLABS, 2 * LANES, S5_SLAB_COLS)), layer((S5_SLABS, S5_SLAB_COLS, LANES)),
                 layer((2 * nb, S5_NC)), layer((2 * nb, S5_NC)), const((2 * nb, 2 * S5_NC))]
    args += [w_bu, w_c, are, aim, h0]
    return pl.pallas_call(
        functools.partial(_s5_kernel, nb=nb, tc=tc, tiles_per_pass=tiles_per_pass),
        grid=(n // 2,),
        in_specs=in_specs,
        out_specs=(out_f, out_b, const((2 * nb, 2 * S5_NC))),
        out_shape=(jax.ShapeDtypeStruct(u.shape, F32), jax.ShapeDtypeStruct(u.shape, F32),
                   jax.ShapeDtypeStruct((2 * nb, 2 * S5_NC), F32)),
        scratch_shapes=[pltpu.VMEM((2 * rows, 2 * S5_NC), F32), pltpu.VMEM((2 * rows, 2 * S5_NC), F32),
                        pltpu.VMEM((2 * rows, 2 * S5_NC), BF16), pltpu.VMEM((2 * nb, 2 * S5_NC), F32)],
        compiler_params=pltpu.CompilerParams(dimension_semantics=("arbitrary",), vmem_limit_bytes=VMEM_LIMIT),
        name="s5_scan",
    )(*args)


def _gla_kernel(*refs, length, has_s0, nsq, cpi):
    q_ref, k_ref, v_ref, gl_ref, wg_ref, bg_ref, ng_ref = refs[:7]
    s0_ref = refs[7] if has_s0 else None
    o_ref, sfin_ref, sf_ref, sb_ref = refs[-4:]
    c = GLA_CHUNK
    n = length // c
    ri = lax.broadcasted_iota(jnp.int32, (c, c), 0)
    ci = lax.broadcasted_iota(jnp.int32, (c, c), 1)
    lane = lax.broadcasted_iota(jnp.int32, (1, GLA_KEY), 1)
    head_mask = [(lane >= h * GLA_DK) & (lane < (h + 1) * GLA_DK) for h in range(GLA_HEADS)]
    ng = ng_ref[...]
    causal = [ci <= ri, ci >= ri]
    tri = [m.astype(BF16) for m in causal]
    end = (c - 1, 0)
    chains = [(sq, d, j) for j in range(cpi) for sq in range(nsq) for d in (0, 1)]
    key_rows = lambda h: slice(h * GLA_DK, (h + 1) * GLA_DK)
    val_cols = lambda h: slice(h * GLA_DV, (h + 1) * GLA_DV)

    sb_ref[...] = jnp.zeros_like(sb_ref)
    for sq in range(nsq):
        for d in (0, 1):
            for h in range(GLA_HEADS):
                if has_s0:
                    sf_ref[sq, d, h] = s0_ref[sq, d, h]
                    sb_ref[sq, d, key_rows(h), val_cols(h)] = s0_ref[sq, d, h].astype(BF16)
                else:
                    sf_ref[sq, d, h] = jnp.zeros((GLA_DK, GLA_DV), F32)

    def chunk_group(r):
        row = lambda d, j: r[d][j]
        qc = [q_ref[sq, pl.ds(row(d, j), c), :] * (GLA_DK ** -0.5) for sq, d, j in chains]
        kc = [k_ref[sq, pl.ds(row(d, j), c), :] for sq, d, j in chains]
        vc = [v_ref[sq, pl.ds(row(d, j), c), :] for sq, d, j in chains]
        logits = [_bdot(gl_ref[sq, pl.ds(row(d, j), c), :], wg_ref[d]) + bg_ref[d] for sq, d, j in chains]
        g = [_split2(_log_sigmoid(x) * (1.0 / GLA_GATE_NORM)) for x in logits]
        b = [jnp.dot(tri[d], hi, preferred_element_type=F32) + jnp.dot(tri[d], lo, preferred_element_type=F32)
             for (sq, d, j), (hi, lo) in zip(chains, g)]
        bt = [x.T for x in b]
        kt = [x.T for x in kc]
        b_mid = [x[c // 2:c // 2 + 1, :] for x in b]
        bt_mid = [x[:, c // 2:c // 2 + 1] for x in bt]
        bt_end = [x[:, end[d]:end[d] + 1] for (sq, d, j), x in zip(chains, bt)]
        q_in = [(x * jnp.exp(y)).astype(BF16) for x, y in zip(qc, b)]
        q_mid = [(x * jnp.exp(y - m)).astype(BF16) for x, y, m in zip(qc, b, b_mid)]
        k_mid_t = [(x * jnp.exp(m - y)).astype(BF16) for x, y, m in zip(kt, bt, bt_mid)]
        k_end_t = [(x * jnp.exp(e - y)).astype(BF16) for x, y, e in zip(kt, bt, bt_end)]
        q_st = [jnp.concatenate([jnp.where(head_mask[h], x, jnp.zeros_like(x)) for h in range(GLA_HEADS)], axis=0)
                for x in q_mid]
        a = [jnp.dot(x, y, preferred_element_type=F32) for x, y in zip(q_st, k_mid_t)]
        parts = [[] for _ in chains]
        for h in range(GLA_HEADS):
            for i, (sq, d, j) in enumerate(chains):
                a_h = jnp.where(causal[d], a[i][h * c:(h + 1) * c, :], 0.0).astype(BF16)
                parts[i].append(jnp.dot(a_h, vc[i][:, val_cols(h)], preferred_element_type=F32))
        upd = [[jnp.dot(k_end_t[i][key_rows(h), :], vc[i][:, val_cols(h)], preferred_element_type=F32)
                for h in range(GLA_HEADS)] for i in range(len(chains))]
        o_inter = [None] * len(chains)
        for i, (sq, d, j) in enumerate(chains):
            o_inter[i] = jnp.dot(q_in[i], sb_ref[sq, d], preferred_element_type=F32)
            for h in range(GLA_HEADS):
                s_new = jnp.exp(bt_end[i][key_rows(h), :]) * sf_ref[sq, d, h] + upd[i][h]
                sf_ref[sq, d, h] = s_new
                sb_ref[sq, d, key_rows(h), val_cols(h)] = s_new.astype(BF16)
        return [x + jnp.concatenate(p, axis=1) for x, p in zip(o_inter, parts)]

    def finish(o, sq, r0):
        o = o + o_ref[sq, pl.ds(r0, c), :]
        normed = []
        for h in range(GLA_HEADS):
            oh = o[:, val_cols(h)]
            ms = jnp.mean(oh * oh, axis=-1, keepdims=True)
            normed.append(oh * lax.rsqrt(ms + EPS))
        o_ref[sq, pl.ds(r0, c), :] = jnp.concatenate(normed, axis=1) * ng

    def rows_of(step):
        fwd = [pl.multiple_of((step * cpi + j) * c, c) for j in range(cpi)]
        bwd = [pl.multiple_of((n - 1 - step * cpi - j) * c, c) for j in range(cpi)]
        return fwd, bwd

    def first_half(step, carry):
        r = rows_of(step)
        for (sq, d, j), o in zip(chains, chunk_group(r)):
            o_ref[sq, pl.ds(r[d][j], c), :] = o
        return carry

    def second_half(step, carry):
        r = rows_of(step)
        for (sq, d, j), o in zip(chains, chunk_group(r)):
            finish(o, sq, r[d][j])
        return carry

    steps = n // cpi
    lax.fori_loop(0, steps // 2, first_half, 0)
    lax.fori_loop(steps // 2, steps, second_half, 0)
    for sq in range(nsq):
        for d in (0, 1):
            for h in range(GLA_HEADS):
                sfin_ref[sq, d, h] = sf_ref[sq, d, h]


def _gla_call(q, k, v, gl, l, wg, bg, ng, s0, nsq, cpi):
    nseq, length, _ = q.shape
    assert (length // GLA_CHUNK) % (2 * cpi) == 0 and nseq % nsq == 0
    seq = lambda width: pl.BlockSpec((nsq, length, width), lambda b: (b, 0, 0))
    layer = lambda shape: pl.BlockSpec((None,) + shape, lambda b: (l,) + (0,) * len(shape))
    in_specs = [seq(GLA_KEY), seq(GLA_KEY), seq(GLA_WIDTH), seq(LANES),
                layer((2, LANES, GLA_KEY)), layer((2, 1, GLA_KEY)), layer((1, GLA_WIDTH))]
    args = [q, k, v, gl, wg, bg, ng]
    if s0 is not None:
        in_specs.append(pl.BlockSpec((nsq, None, 2, GLA_HEADS, GLA_DK, GLA_DV), lambda b: (b, l, 0, 0, 0, 0)))
        args.append(s0)
    return pl.pallas_call(
        functools.partial(_gla_kernel, length=length, has_s0=s0 is not None, nsq=nsq, cpi=cpi),
        grid=(nseq // nsq,),
        in_specs=in_specs,
        out_specs=(seq(GLA_WIDTH), pl.BlockSpec((nsq, 2, GLA_HEADS, GLA_DK, GLA_DV), lambda b: (b, 0, 0, 0, 0))),
        out_shape=(jax.ShapeDtypeStruct((nseq, length, GLA_WIDTH), F32),
                   jax.ShapeDtypeStruct((nseq, 2, GLA_HEADS, GLA_DK, GLA_DV), F32)),
        scratch_shapes=[pltpu.VMEM((nsq, 2, GLA_HEADS, GLA_DK, GLA_DV), F32),
                        pltpu.VMEM((nsq, 2, GLA_KEY, GLA_WIDTH), BF16)],
        compiler_params=pltpu.CompilerParams(dimension_semantics=("arbitrary",), vmem_limit_bytes=VMEM_LIMIT),
        name="gla",
    )(*args)


def _mix_kernel(*refs, has_pe, last, row_groups):
    if has_pe:
        (x_ref, pe_ref, yf_ref, yb_ref, u_ref, o_ref, shift_ref, scale_ref, gate_ref, ng_ref, wg_ref, d_ref,
         wglu_ref, bglu_ref, wpa_ref, wpb_ref, wo_ref, fng_ref, out_ref) = refs
    else:
        (x_ref, yf_ref, yb_ref, u_ref, o_ref, shift_ref, scale_ref, gate_ref, ng_ref, wg_ref, d_ref,
         wglu_ref, bglu_ref, wpa_ref, wpb_ref, wo_ref, fng_ref, out_ref) = refs
    nsb, tm, _ = x_ref.shape
    grp = [(i, slice(j * tm // row_groups, (j + 1) * tm // row_groups)) for i in range(nsb) for j in range(row_groups)]
    x = [x_ref[i, r, :] + pe_ref[r, :] if has_pe else x_ref[i, r, :] for i, r in grp]
    h = [_modulated_norm(v, ng_ref[...], 1.0 + scale_ref[...], shift_ref[...]).astype(BF16) for v in x]
    gts = [_bdot_t(v, wg_ref[...]) for v in h]
    y = [_gelu_tanh(yf_ref[i, r, :] + yb_ref[i, r, :] + d_ref[...] * u_ref[i, r, :]) for i, r in grp]
    glu = [_bdot(v, wglu_ref[...]) for v in y]
    y = [v * _sigmoid(g + bglu_ref[...]) for v, g in zip(y, glu)]
    y_a = [v * _silu(g[:, :S5_WIDTH]) for v, g in zip(y, gts)]
    y_b = [o_ref[i, r, :] * _silu(g[:, S5_WIDTH:2 * S5_WIDTH]) for (i, r), g in zip(grp, gts)]
    p_a = [_bdot(v, wpa_ref[...]) for v in y_a]
    p_b = [_bdot(v, wpb_ref[...]) for v in y_b]
    merged = [_sigmoid(g[:, 2 * S5_WIDTH:2 * S5_WIDTH + D_MODEL]) * a + _sigmoid(g[:, 2 * S5_WIDTH + D_MODEL:]) * b
              for g, a, b in zip(gts, p_a, p_b)]
    xn = [v + gate_ref[...] * _bdot(m, wo_ref[...]) for v, m in zip(x, merged)]
    for (i, r), v in zip(grp, xn):
        if last:
            ms = jnp.mean(v * v, axis=-1, keepdims=True)
            v = v * lax.rsqrt(ms + EPS) * fng_ref[...]
        out_ref[i, r, :] = v


def _mix_call(x, pe, yf, yb, u, o, mod, l, cond_row, ng, wg, s5_d, wglu, bglu, wpa, wpb, wo, fng, tm, last, nsb=1):
    nseq, length, _ = x.shape
    row = lambda width: pl.BlockSpec((nsb, tm, width), lambda b, j: (b, j, 0))
    in_specs = [row(D_MODEL)]
    args = [x]
    if pe is not None:
        in_specs.append(pl.BlockSpec((tm, D_MODEL), lambda b, j: (j, 0)))
        args.append(pe)
    in_specs += [row(S5_WIDTH), row(S5_WIDTH), row(S5_WIDTH), row(GLA_WIDTH),
                 _mod_spec(l, cond_row, 0), _mod_spec(l, cond_row, 1), _mod_spec(l, cond_row, 2),
                 _layer_spec(l, (1, D_MODEL)), _layer_spec(l, (GATE_W, D_MODEL)), _layer_spec(l, (1, S5_WIDTH)),
                 _layer_spec(l, (S5_WIDTH, S5_WIDTH)), _layer_spec(l, (1, S5_WIDTH)),
                 _layer_spec(l, (S5_WIDTH, D_MODEL)), _layer_spec(l, (GLA_WIDTH, D_MODEL)),
                 _layer_spec(l, (D_MODEL, D_MODEL)), pl.BlockSpec((1, D_MODEL), lambda b, j: (0, 0))]
    args += [yf, yb, u, o, mod, mod, mod, ng, wg, s5_d, wglu, bglu, wpa, wpb, wo, fng]
    return pl.pallas_call(
        functools.partial(_mix_kernel, has_pe=pe is not None, last=last, row_groups=max(1, tm // MXU_TILE)),
        grid=(nseq // nsb, length // tm),
        in_specs=in_specs,
        out_specs=row(D_MODEL),
        out_shape=jax.ShapeDtypeStruct(x.shape, F32),
        compiler_params=pltpu.CompilerParams(dimension_semantics=("arbitrary", "arbitrary"),
                                             vmem_limit_bytes=VMEM_LIMIT),
        name="mix",
    )(*args)


def _grid_pos_embed(length, dim):
    rows = length // GRID_W
    quarter = dim // 4
    freqs = jnp.exp(-math.log(10000.0) * jnp.arange(quarter, dtype=F32) / quarter)

    def sincos(pos):
        ang = pos.astype(F32)[:, None] * freqs[None, :]
        return jnp.concatenate([jnp.sin(ang), jnp.cos(ang)], axis=-1)

    er = sincos(jnp.arange(rows))
    ec = sincos(jnp.arange(GRID_W))
    pe = jnp.concatenate([jnp.broadcast_to(er[:, None, :], (rows, GRID_W, dim // 2)),
                          jnp.broadcast_to(ec[None, :, :], (rows, GRID_W, dim // 2))], axis=-1)
    return pe.reshape(rows * GRID_W, dim)


def kernel(x_prompt, x_sample, c, state_s5_re, state_s5_im, state_gla, c_ctx, norm_g, w_mod, b_mod, w_in, gla_wg_up,
           gla_bg, gla_norm_g, s5_lam_re, s5_lam_im, s5_log_dt, s5_b_re, s5_b_im, s5_c_re, s5_c_im, s5_d, w_glu,
           b_glu, w_pa, w_pb, w_o, final_norm_g):
    depth = w_in.shape[0]
    bp, lp, _ = x_prompt.shape
    bs, ls, _ = x_sample.shape

    cond8 = jnp.zeros((8, D_MODEL), F32).at[0].set(c_ctx).at[1:1 + bs].set(c)
    mod = _mod_call(cond8, w_mod, b_mod)

    a_re, a_im, w_bu, w_c = _s5_params(s5_lam_re, s5_lam_im, s5_log_dt, s5_b_re, s5_b_im, s5_c_re, s5_c_im)

    w_proj, w_gate = _split_w_in(w_in)
    wg_up = jnp.concatenate([gla_wg_up, jnp.zeros((depth, 2, LANES - GLA_RANK, GLA_KEY), F32)], axis=2).astype(BF16)
    w_glu_b, w_pa_b, w_pb_b, w_o_b = (w.astype(BF16) for w in (w_glu, w_pa, w_pb, w_o))

    vec = lambda a: a[:, None, :]
    mod = mod.reshape(depth, 8, 1, 3 * D_MODEL)
    ng, gng, bg = vec(norm_g), vec(gla_norm_g), gla_bg[:, :, None, :]
    common = (ng, w_gate, vec(s5_d), w_glu_b, vec(b_glu), w_pa_b, w_pb_b, w_o_b, final_norm_g[None])
    are_p, aim_p = jnp.repeat(a_re, bp, axis=1), jnp.repeat(a_im, bp, axis=1)
    are_s, aim_s = jnp.repeat(a_re, bs, axis=1), jnp.repeat(a_im, bs, axis=1)
    ctx_row = lambda b: 0
    cond_row = lambda b: b + 1

    pe = _grid_pos_embed(ls, D_MODEL)
    zero_h0 = jnp.zeros((2 * bp, 2 * S5_NC), F32)
    xp, xs = x_prompt, x_sample
    new_re, new_im, new_gla = [], [], []
    for l in range(depth):
        last = l == depth - 1

        u, q, k, v, gl = _proj_call(xp, None, mod, l, ctx_row, ng, w_proj, tm=lp, nsb=4)
        yf, yb, hfin = _s5_call(u, l, w_bu, w_c, are_p, aim_p, zero_h0, tc=16, tiles_per_pass=2)
        o, sfin = _gla_call(q, k, v, gl, l, wg_up, bg, gng, None, nsq=4, cpi=1)
        new_gla.append(sfin)
        xp = _mix_call(xp, None, yf, yb, u, o, mod, l, ctx_row, *common, tm=lp, last=last, nsb=2)
        re_l, im_l = _s5_cols_to_state(hfin.reshape(2, bp, 2 * S5_NC))
        new_re.append(jnp.swapaxes(re_l, 0, 1))
        new_im.append(jnp.swapaxes(im_l, 0, 1))

        pe_l = pe if l == 0 else None
        u, q, k, v, gl = _proj_call(xs, pe_l, mod, l, cond_row, ng, w_proj, tm=1024)
        h0 = _s5_state_to_cols(jnp.swapaxes(state_s5_re[:, l], 0, 1), jnp.swapaxes(state_s5_im[:, l], 0, 1))
        yf, yb, _ = _s5_call(u, l, w_bu, w_c, are_s, aim_s, h0.reshape(2 * bs, 2 * S5_NC), tc=32, tiles_per_pass=4)
        o, _ = _gla_call(q, k, v, gl, l, wg_up, bg, gng, state_gla, nsq=2, cpi=2)
        xs = _mix_call(xs, pe_l, yf, yb, u, o, mod, l, cond_row, *common, tm=512, last=last)

    return (xp, xs, jnp.stack(new_re, axis=1), jnp.stack(new_im, axis=1), jnp.stack(new_gla, axis=1))
```

```python
import functools
import math

import numpy as np
import jax
import jax.numpy as jnp
from jax import lax
from jax.experimental import pallas as pl
from jax.experimental.pallas import tpu as pltpu

F32 = jnp.float32
BF16 = jnp.bfloat16

D_MODEL = 1024
GRID_W = 64
S5_WIDTH = 512
S5_GROUPS = 32
S5_CH = 16
S5_STATE = 64
S5_NC = S5_GROUPS * S5_STATE
S5_SLAB_GROUPS = 8
S5_SLABS = S5_GROUPS // S5_SLAB_GROUPS
S5_SLAB_NC = S5_SLAB_GROUPS * S5_STATE
S5_SLAB_COLS = 2 * S5_SLAB_NC
GLA_HEADS = 4
GLA_DK = 64
GLA_DV = 128
GLA_KEY = GLA_HEADS * GLA_DK
GLA_WIDTH = GLA_HEADS * GLA_DV
GLA_RANK = 16
GLA_GATE_NORM = 16.0
GLA_CHUNK = 128
EPS = 1e-6
LANES = 128
PROJ_W = S5_WIDTH + 2 * GLA_KEY + GLA_WIDTH + LANES
GATE_W = 2 * S5_WIDTH + 2 * D_MODEL
VMEM_LIMIT = 60 * 1024 * 1024
MXU_TILE = 256


def _bdot(a, b):
    return jnp.dot(a.astype(BF16), b.astype(BF16), preferred_element_type=F32)


def _bdot_t(a, bt):
    return lax.dot_general(a.astype(BF16), bt.astype(BF16), (((1,), (1,)), ((), ())), preferred_element_type=F32)


def _split2(x):
    hi = x.astype(BF16)
    lo = (x - hi.astype(F32)).astype(BF16)
    return hi, lo


def _sigmoid(x):
    return 1.0 / (1.0 + jnp.exp(-x))


def _silu(x):
    return x * _sigmoid(x)


def _gelu_tanh(x):
    c = math.sqrt(2.0 / math.pi)
    return 0.5 * x * (1.0 + jnp.tanh(c * (x + 0.044715 * (x * x * x))))


def _log_sigmoid(x):
    return jnp.minimum(x, 0.0) - jnp.log(1.0 + jnp.exp(-jnp.abs(x)))


def _modulated_norm(x, ng, scale1p, shift):
    ms = jnp.mean(x * x, axis=-1, keepdims=True)
    return (x * lax.rsqrt(ms + EPS) * ng) * scale1p + shift


def _mod_kernel(cond_ref, w_ref, b_ref, out_ref):
    c = cond_ref[...]
    s_hi, s_lo = _split2(_silu(c))
    w_hi, w_lo = _split2(w_ref[0])
    acc = jnp.dot(s_hi, w_hi, preferred_element_type=F32)
    acc += jnp.dot(s_lo, w_hi, preferred_element_type=F32)
    acc += jnp.dot(s_hi, w_lo, preferred_element_type=F32)
    out_ref[0] = acc + b_ref[0]


def _mod_call(cond8, w_mod, b_mod):
    depth = w_mod.shape[0]
    nb = 1536
    return pl.pallas_call(
        _mod_kernel,
        grid=(depth, 3 * D_MODEL // nb),
        in_specs=[
            pl.BlockSpec((8, D_MODEL), lambda l, j: (0, 0)),
            pl.BlockSpec((1, D_MODEL, nb), lambda l, j: (l, 0, j)),
            pl.BlockSpec((1, 1, nb), lambda l, j: (l, 0, j)),
        ],
        out_specs=pl.BlockSpec((1, 8, nb), lambda l, j: (l, 0, j)),
        out_shape=jax.ShapeDtypeStruct((depth, 8, 3 * D_MODEL), F32),
        compiler_params=pltpu.CompilerParams(dimension_semantics=("arbitrary", "arbitrary")),
        name="adaln_mod",
    )(cond8, w_mod, b_mod.reshape(depth, 1, 3 * D_MODEL))


def _zoh(lre, lim, ldt):
    dt = jnp.exp(ldt)
    mag = jnp.exp(lre * dt)
    a_re = mag * jnp.cos(lim * dt)
    a_im = mag * jnp.sin(lim * dt)
    n_re = a_re - 1.0
    inv = 1.0 / (lre * lre + lim * lim)
    return a_re, a_im, (n_re * lre + a_im * lim) * inv, (a_im * lre - n_re * lim) * inv


def _zoh_kernel(lre_ref, lim_ref, ldt_ref, lre_x_ref, lim_x_ref, ldt_x_ref, bre_ref, bim_ref, cre_ref, cim_ref,
                are_ref, aim_ref, wbu_ref, wc_ref, *, depth):
    a_re, a_im, _, _ = _zoh(lre_ref[...], lim_ref[...], ldt_ref[...])
    are_ref[...] = a_re
    aim_ref[...] = a_im
    _, _, c_re, c_im = _zoh(lre_x_ref[...], lim_x_ref[...], ldt_x_ref[...])
    bre = bre_ref[...]
    bim = bim_ref[...]
    bbar = [c_re * bre - c_im * bim, c_re * bim + c_im * bre]
    cmat = [cre_ref[...], -cim_ref[...]]
    wbu_ref[...] = jnp.zeros_like(wbu_ref)
    wc_ref[...] = jnp.zeros_like(wc_ref)
    for l in range(depth):
        for s in range(S5_SLABS):
            for gl in range(S5_SLAB_GROUPS):
                g = S5_SLAB_GROUPS * s + gl
                for ri in range(2):
                    col = ri * S5_SLAB_NC + gl * S5_STATE
                    for d in range(2):
                        src = ((l * 2 + d) * S5_GROUPS + g) * S5_CH
                        row = (d * S5_SLAB_GROUPS + gl) * S5_CH
                        wbu_ref[l, s, row:row + S5_CH, col:col + S5_STATE] = bbar[ri][src:src + S5_CH, :].astype(BF16)
                    src = (l * S5_GROUPS + g) * S5_STATE
                    wc_ref[l, s, col:col + S5_STATE, gl * S5_CH:(gl + 1) * S5_CH] = (
                        cmat[ri][src:src + S5_STATE, :].astype(BF16))


def _s5_params(s5_lam_re, s5_lam_im, s5_log_dt, s5_b_re, s5_b_im, s5_c_re, s5_c_im):
    depth = s5_lam_re.shape[0]
    full = (depth, 2, S5_GROUPS, S5_CH, S5_STATE)
    expand = lambda a: jnp.broadcast_to(a, full).reshape(-1, S5_STATE)
    small = lambda a: a.reshape(depth * 2 * S5_GROUPS, -1)
    ct = lambda a: jnp.swapaxes(a, -1, -2).reshape(-1, S5_CH)
    a_re, a_im, w_bu, w_c = pl.pallas_call(
        functools.partial(_zoh_kernel, depth=depth),
        out_shape=(jax.ShapeDtypeStruct((depth * 2 * S5_GROUPS, S5_STATE), F32),
                   jax.ShapeDtypeStruct((depth * 2 * S5_GROUPS, S5_STATE), F32),
                   jax.ShapeDtypeStruct((depth, S5_SLABS, 2 * LANES, S5_SLAB_COLS), BF16),
                   jax.ShapeDtypeStruct((depth, S5_SLABS, S5_SLAB_COLS, LANES), BF16)),
        name="s5_zoh",
    )(small(s5_lam_re), small(s5_lam_im), small(s5_log_dt),
      expand(s5_lam_re[:, :, :, None, :]), expand(s5_lam_im[:, :, :, None, :]), expand(s5_log_dt[:, :, :, None, None]),
      expand(jnp.swapaxes(s5_b_re, -1, -2)[:, None]), expand(jnp.swapaxes(s5_b_im, -1, -2)[:, None]),
      ct(s5_c_re), ct(s5_c_im))
    return a_re.reshape(depth, 2, S5_NC), a_im.reshape(depth, 2, S5_NC), w_bu, w_c


def _s5_state_to_cols(re, im):
    lead = re.shape[:-2]
    st = jnp.stack([re.reshape(lead + (S5_SLABS, S5_SLAB_NC)), im.reshape(lead + (S5_SLABS, S5_SLAB_NC))], axis=-2)
    return st.reshape(lead + (2 * S5_NC,))


def _s5_cols_to_state(cols):
    lead = cols.shape[:-1]
    st = cols.reshape(lead + (S5_SLABS, 2, S5_SLAB_NC))
    re = st[..., 0, :].reshape(lead + (S5_GROUPS, S5_STATE))
    im = st[..., 1, :].reshape(lead + (S5_GROUPS, S5_STATE))
    return re, im


_IN_OFFS = [int(o) for o in np.cumsum([0, S5_WIDTH, S5_WIDTH, GLA_KEY, GLA_KEY, GLA_WIDTH, GLA_WIDTH, GLA_RANK,
                                       D_MODEL, D_MODEL])]


def _split_w_in_kernel(u_ref, qkv_ref, code_ref, ga_ref, gb_ref, m_ref, proj_ref, gate_ref):
    rows = 0
    for src in (u_ref, qkv_ref, code_ref):
        proj_ref[rows:rows + src.shape[1], :] = src[0].astype(BF16)
        rows += src.shape[1]
    proj_ref[rows:, :] = jnp.zeros((PROJ_W - rows, proj_ref.shape[1]), BF16)
    rows = 0
    for src in (ga_ref, gb_ref, m_ref):
        gate_ref[rows:rows + src.shape[1], :] = src[0].astype(BF16)
        rows += src.shape[1]


def _split_w_in(w_in):
    o = _IN_OFFS
    depth = w_in.shape[0]
    wt = jnp.swapaxes(w_in, 1, 2)
    tn = D_MODEL // 2
    rows = lambda lo, hi: pl.BlockSpec((pl.Element(1), pl.Element(hi - lo), pl.Element(tn)),
                                       lambda l, j: (l, lo, j * tn))
    out = lambda n: pl.BlockSpec((None, n, tn), lambda l, j: (l, 0, j))
    return pl.pallas_call(
        _split_w_in_kernel,
        grid=(depth, D_MODEL // tn),
        in_specs=[rows(o[0], o[1]), rows(o[2], o[5]), rows(o[6], o[7]), rows(o[1], o[2]), rows(o[5], o[6]),
                  rows(o[7], o[9])],
        out_specs=(out(PROJ_W), out(GATE_W)),
        out_shape=(jax.ShapeDtypeStruct((depth, PROJ_W, D_MODEL), BF16),
                   jax.ShapeDtypeStruct((depth, GATE_W, D_MODEL), BF16)),
        compiler_params=pltpu.CompilerParams(dimension_semantics=("arbitrary", "arbitrary"),
                                             vmem_limit_bytes=VMEM_LIMIT),
        name="split_w_in",
    )(wt, wt, wt, wt, wt, wt)


def _proj_kernel(*refs, has_pe):
    if has_pe:
        x_ref, pe_ref, shift_ref, scale_ref, ng_ref, w_ref, u_ref, q_ref, k_ref, v_ref, gl_ref = refs
    else:
        x_ref, shift_ref, scale_ref, ng_ref, w_ref, u_ref, q_ref, k_ref, v_ref, gl_ref = refs
    nsb, tm, _ = x_ref.shape
    rg = max(1, tm // (2 * MXU_TILE))
    grp = [(i, slice(j * tm // rg, (j + 1) * tm // rg)) for i in range(nsb) for j in range(rg)]
    x = [x_ref[i, r, :] + pe_ref[r, :] if has_pe else x_ref[i, r, :] for i, r in grp]
    h = [_modulated_norm(v, ng_ref[...], 1.0 + scale_ref[...], shift_ref[...]) for v in x]
    p = [_bdot_t(v, w_ref[...]) for v in h]
    for (i, r), v in zip(grp, p):
        o = 0
        for ref, width in ((u_ref, S5_WIDTH), (q_ref, GLA_KEY), (k_ref, GLA_KEY), (v_ref, GLA_WIDTH), (gl_ref, LANES)):
            ref[i, r, :] = v[:, o:o + width].astype(ref.dtype)
            o += width


def _mod_spec(l, cond_row, part):
    return pl.BlockSpec((None, None, 1, D_MODEL), lambda b, j: (l, cond_row(b), 0, part))


def _layer_spec(l, shape):
    return pl.BlockSpec((None,) + shape, lambda b, j: (l,) + (0,) * len(shape))


def _proj_call(x, pe, mod, l, cond_row, ng, w, tm, nsb=1):
    nseq, length, _ = x.shape
    row = lambda width: pl.BlockSpec((nsb, tm, width), lambda b, j: (b, j, 0))
    in_specs = [row(D_MODEL)]
    args = [x]
    if pe is not None:
        in_specs.append(pl.BlockSpec((tm, D_MODEL), lambda b, j: (j, 0)))
        args.append(pe)
    in_specs += [_mod_spec(l, cond_row, 0), _mod_spec(l, cond_row, 1), _layer_spec(l, (1, D_MODEL)),
                 _layer_spec(l, (PROJ_W, D_MODEL))]
    args += [mod, mod, ng, w]
    sds = lambda width, dt: jax.ShapeDtypeStruct((nseq, length, width), dt)
    return pl.pallas_call(
        functools.partial(_proj_kernel, has_pe=pe is not None),
        grid=(nseq // nsb, length // tm),
        in_specs=in_specs,
        out_specs=(row(S5_WIDTH), row(GLA_KEY), row(GLA_KEY), row(GLA_WIDTH), row(LANES)),
        out_shape=(sds(S5_WIDTH, F32), sds(GLA_KEY, F32), sds(GLA_KEY, F32), sds(GLA_WIDTH, BF16), sds(LANES, F32)),
        compiler_params=pltpu.CompilerParams(dimension_semantics=("arbitrary", "arbitrary"),
                                             vmem_limit_bytes=VMEM_LIMIT),
        name="proj",
    )(*args)


def _s5_perms(nb, tc):
    r = nb * tc
    pf = np.zeros((2 * r, r), np.float32)
    pb = np.zeros((2 * r, r), np.float32)
    for t in range(tc):
        for b in range(nb):
            pf[t * 2 * nb + b, b * tc + t] = 1.0
            pb[t * 2 * nb + nb + b, b * tc + (tc - 1 - t)] = 1.0
    return pf, pb


def _s5_store_unpacked(y2, yf_ref, yb_ref, t_f, t_b, nb, tc):
    rs = 2 * nb
    for t in range(tc):
        yf_ref[:, t_f + t, :] = y2[t * rs:t * rs + nb]
        yb_ref[:, t_b + tc - 1 - t, :] = y2[t * rs + nb:(t + 1) * rs]


def _s5_kernel(*refs, nb, tc, tiles_per_pass):
    use_perm = nb % 8 != 0
    uf1_ref, ub1_ref, uf2_ref, ub2_ref, uf0_ref, ub0_ref = refs[:6]
    pf_ref, pb_ref = refs[6:8] if use_perm else (None, None)
    wbu_ref, wc_ref, are_ref, aim_ref, h0_ref, yf_ref, yb_ref, hfin_ref, bua_ref, bub_ref, hb_ref, st_ref = refs[-12:]
    i = pl.program_id(0)
    rows = nb * tc
    rs = 2 * nb

    def packed_inputs(uf_ref, ub_ref):
        if not use_perm:
            return tuple(jnp.concatenate([r[:, t, :] for t in range(tc)], axis=0).astype(BF16) for r in (uf_ref, ub_ref))
        uf = uf_ref[...].reshape(rows, S5_WIDTH).astype(BF16)
        ub = ub_ref[...].reshape(rows, S5_WIDTH).astype(BF16)
        u2f = jnp.dot(pf_ref[...], uf, preferred_element_type=F32).astype(BF16)
        u2b = jnp.dot(pb_ref[...], ub, preferred_element_type=F32).astype(BF16)
        return [jnp.concatenate([u2f[:, s * LANES:(s + 1) * LANES], u2b[:, s * LANES:(s + 1) * LANES]], axis=1)
                for s in range(S5_SLABS)]

    def bu_slab(lhs, s, dst_ref):
        cols = slice(s * S5_SLAB_COLS, (s + 1) * S5_SLAB_COLS)
        if use_perm:
            dst_ref[:, cols] = jnp.dot(lhs[s], wbu_ref[s], preferred_element_type=F32)
            return
        uf, ub = lhs
        bf = jnp.dot(uf[:, s * LANES:(s + 1) * LANES], wbu_ref[s, :LANES, :], preferred_element_type=F32)
        bb = jnp.dot(ub[:, s * LANES:(s + 1) * LANES], wbu_ref[s, LANES:, :], preferred_element_type=F32)
        for t in range(tc):
            dst_ref[t * rs:t * rs + nb, cols] = bf[t * nb:(t + 1) * nb]
            dst_ref[t * rs + nb:(t + 1) * rs, cols] = bb[(tc - 1 - t) * nb:(tc - t) * nb]

    @pl.when(i == 0)
    def _():
        st_ref[...] = h0_ref[...]
        lhs0 = packed_inputs(uf0_ref, ub0_ref)
        for s in range(S5_SLABS):
            bu_slab(lhs0, s, bua_ref)

    half_step = functools.partial(_s5_half_step, wc_ref=wc_ref, are_ref=are_ref, aim_ref=aim_ref, hb_ref=hb_ref,
                                  st_ref=st_ref, bu_slab=bu_slab, rs=rs, tc=tc, tiles_per_pass=tiles_per_pass)
    y2 = half_step(bua_ref, bub_ref, packed_inputs(uf1_ref, ub1_ref))
    _s5_store_unpacked(y2, yf_ref, yb_ref, 0, tc, nb, tc)
    y2 = half_step(bub_ref, bua_ref, packed_inputs(uf2_ref, ub2_ref))
    _s5_store_unpacked(y2, yf_ref, yb_ref, tc, 0, nb, tc)
    hfin_ref[...] = st_ref[...]


def _s5_half_step(cur_ref, nxt_ref, lhs_next, *, wc_ref, are_ref, aim_ref, hb_ref, st_ref, bu_slab,
                  rs, tc, tiles_per_pass):
    slab_tiles = S5_SLAB_NC // LANES
    grp = max(1, 16 // rs)
    ys = []
    for s in range(S5_SLABS):
        bu_slab(lhs_next, s, nxt_ref)
        for c0 in range(slab_tiles * s, slab_tiles * (s + 1), tiles_per_pass):
            cts = list(range(c0, c0 + tiles_per_pass))
            cre = [(ct // slab_tiles) * S5_SLAB_COLS + (ct % slab_tiles) * LANES for ct in cts]
            cim = [c + S5_SLAB_NC for c in cre]
            a_re = [are_ref[:, ct * LANES:(ct + 1) * LANES] for ct in cts]
            a_im = [aim_ref[:, ct * LANES:(ct + 1) * LANES] for ct in cts]
            h_re = [st_ref[:, c:c + LANES] for c in cre]
            h_im = [st_ref[:, c:c + LANES] for c in cim]
            for t0 in range(0, tc, grp):
                out_re = [[] for _ in cts]
                out_im = [[] for _ in cts]
                for t in range(t0, t0 + grp):
                    r0 = t * rs
                    for j in range(len(cts)):
                        b_re = cur_ref[r0:r0 + rs, cre[j]:cre[j] + LANES]
                        b_im = cur_ref[r0:r0 + rs, cim[j]:cim[j] + LANES]
                        n_re = a_re[j] * h_re[j] - a_im[j] * h_im[j] + b_re
                        n_im = a_re[j] * h_im[j] + a_im[j] * h_re[j] + b_im
                        h_re[j], h_im[j] = n_re, n_im
                        out_re[j].append(n_re)
                        out_im[j].append(n_im)
                for j in range(len(cts)):
                    blk_re = out_re[j][0] if grp == 1 else jnp.concatenate(out_re[j], axis=0)
                    blk_im = out_im[j][0] if grp == 1 else jnp.concatenate(out_im[j], axis=0)
                    hb_ref[t0 * rs:(t0 + grp) * rs, cre[j]:cre[j] + LANES] = blk_re.astype(BF16)
                    hb_ref[t0 * rs:(t0 + grp) * rs, cim[j]:cim[j] + LANES] = blk_im.astype(BF16)
            for j in range(len(cts)):
                st_ref[:, cre[j]:cre[j] + LANES] = h_re[j]
                st_ref[:, cim[j]:cim[j] + LANES] = h_im[j]
        ys.append(jnp.dot(hb_ref[:, s * S5_SLAB_COLS:(s + 1) * S5_SLAB_COLS], wc_ref[s], preferred_element_type=F32))
    return jnp.concatenate(ys, axis=1)


def _s5_call(u, l, w_bu, w_c, are, aim, h0, tc, tiles_per_pass):
    nb, length, _ = u.shape
    n = length // tc
    rows = nb * tc
    const = lambda shape: pl.BlockSpec(shape, lambda i: (0,) * len(shape))
    layer = lambda shape: pl.BlockSpec((None,) + shape, lambda i: (l,) + (0,) * len(shape))
    blk = lambda chunk: pl.BlockSpec((nb, tc, S5_WIDTH), lambda i: (0, chunk(i), 0))
    out_f = pl.BlockSpec((nb, 2 * tc, S5_WIDTH), lambda i: (0, i, 0))
    out_b = pl.BlockSpec((nb, 2 * tc, S5_WIDTH), lambda i: (0, n // 2 - 1 - i, 0))
    second = lambda i: jnp.minimum(2 * i + 2, n - 1)
    in_specs = [blk(lambda i: 2 * i + 1), blk(lambda i: n - 2 - 2 * i), blk(second), blk(lambda i: n - 1 - second(i)),
                blk(lambda i: 0), blk(lambda i: n - 1)]
    args = [u] * 6
    if nb % 8 != 0:
        pf, pb = _s5_perms(nb, tc)
        in_specs += [const((2 * rows, rows)), const((2 * rows, rows))]
        args += [jnp.asarray(pf, BF16), jnp.asarray(pb, BF16)]
    in_specs += [layer((S5_SLABS, 2 * LANES, S5_SLAB_COLS)), layer((S5_SLABS, S5_SLAB_COLS, LANES)),
                 layer((2 * nb, S5_NC)), layer((2 * nb, S5_NC)), const((2 * nb, 2 * S5_NC))]
    args += [w_bu, w_c, are, aim, h0]
    return pl.pallas_call(
        functools.partial(_s5_kernel, nb=nb, tc=tc, tiles_per_pass=tiles_per_pass),
        grid=(n // 2,),
        in_specs=in_specs,
        out_specs=(out_f, out_b, const((2 * nb, 2 * S5_NC))),
        out_shape=(jax.ShapeDtypeStruct(u.shape, F32), jax.ShapeDtypeStruct(u.shape, F32),
                   jax.ShapeDtypeStruct((2 * nb, 2 * S5_NC), F32)),
        scratch_shapes=[pltpu.VMEM((2 * rows, 2 * S5_NC), F32), pltpu.VMEM((2 * rows, 2 * S5_NC), F32),
                        pltpu.VMEM((2 * rows, 2 * S5_NC), BF16), pltpu.VMEM((2 * nb, 2 * S5_NC), F32)],
        compiler_params=pltpu.CompilerParams(dimension_semantics=("arbitrary",), vmem_limit_bytes=VMEM_LIMIT),
        name="s5_scan",
    )(*args)


def _gla_kernel(*refs, length, has_s0, nsq, cpi):
    q_ref, k_ref, v_ref, gl_ref, wg_ref, bg_ref = refs[:6]
    s0_ref = refs[6] if has_s0 else None
    o_ref, sfin_ref, sf_ref, sb_ref = refs[-4:]
    c = GLA_CHUNK
    n = length // c
    ri = lax.broadcasted_iota(jnp.int32, (c, c), 0)
    ci = lax.broadcasted_iota(jnp.int32, (c, c), 1)
    lane = lax.broadcasted_iota(jnp.int32, (1, GLA_KEY), 1)
    head_mask = [(lane >= h * GLA_DK) & (lane < (h + 1) * GLA_DK) for h in range(GLA_HEADS)]
    causal = [ci <= ri, ci >= ri]
    tri = [m.astype(BF16) for m in causal]
    end = (c - 1, 0)
    chains = [(sq, d, j) for j in range(cpi) for sq in range(nsq) for d in (0, 1)]
    key_rows = lambda h: slice(h * GLA_DK, (h + 1) * GLA_DK)
    val_cols = lambda h: slice(h * GLA_DV, (h + 1) * GLA_DV)

    sb_ref[...] = jnp.zeros_like(sb_ref)
    for sq in range(nsq):
        for d in (0, 1):
            for h in range(GLA_HEADS):
                if has_s0:
                    sf_ref[sq, d, h] = s0_ref[sq, d, h]
                    sb_ref[sq, d, key_rows(h), val_cols(h)] = s0_ref[sq, d, h].astype(BF16)
                else:
                    sf_ref[sq, d, h] = jnp.zeros((GLA_DK, GLA_DV), F32)

    def chunk_group(r):
        row = lambda d, j: r[d][j]
        qc = [q_ref[sq, pl.ds(row(d, j), c), :] * (GLA_DK ** -0.5) for sq, d, j in chains]
        kc = [k_ref[sq, pl.ds(row(d, j), c), :] for sq, d, j in chains]
        vc = [v_ref[sq, pl.ds(row(d, j), c), :] for sq, d, j in chains]
        logits = [_bdot(gl_ref[sq, pl.ds(row(d, j), c), :], wg_ref[d]) + bg_ref[d] for sq, d, j in chains]
        g = [_split2(_log_sigmoid(x) * (1.0 / GLA_GATE_NORM)) for x in logits]
        b = [jnp.dot(tri[d], hi, preferred_element_type=F32) + jnp.dot(tri[d], lo, preferred_element_type=F32)
             for (sq, d, j), (hi, lo) in zip(chains, g)]
        bt = [x.T for x in b]
        kt = [x.T for x in kc]
        b_mid = [x[c // 2:c // 2 + 1, :] for x in b]
        bt_mid = [x[:, c // 2:c // 2 + 1] for x in bt]
        bt_end = [x[:, end[d]:end[d] + 1] for (sq, d, j), x in zip(chains, bt)]
        q_in = [(x * jnp.exp(y)).astype(BF16) for x, y in zip(qc, b)]
        q_mid = [(x * jnp.exp(y - m)).astype(BF16) for x, y, m in zip(qc, b, b_mid)]
        k_mid_t = [(x * jnp.exp(m - y)).astype(BF16) for x, y, m in zip(kt, bt, bt_mid)]
        k_end_t = [(x * jnp.exp(e - y)).astype(BF16) for x, y, e in zip(kt, bt, bt_end)]
        q_st = [jnp.concatenate([jnp.where(head_mask[h], x, jnp.zeros_like(x)) for h in range(GLA_HEADS)], axis=0)
                for x in q_mid]
        a = [jnp.dot(x, y, preferred_element_type=F32) for x, y in zip(q_st, k_mid_t)]
        parts = [[] for _ in chains]
        for h in range(GLA_HEADS):
            for i, (sq, d, j) in enumerate(chains):
                a_h = jnp.where(causal[d], a[i][h * c:(h + 1) * c, :], 0.0).astype(BF16)
                parts[i].append(jnp.dot(a_h, vc[i][:, val_cols(h)], preferred_element_type=F32))
        upd = [[jnp.dot(k_end_t[i][key_rows(h), :], vc[i][:, val_cols(h)], preferred_element_type=F32)
                for h in range(GLA_HEADS)] for i in range(len(chains))]
        o_inter = [None] * len(chains)
        for i, (sq, d, j) in enumerate(chains):
            o_inter[i] = jnp.dot(q_in[i], sb_ref[sq, d], preferred_element_type=F32)
            for h in range(GLA_HEADS):
                s_new = jnp.exp(bt_end[i][key_rows(h), :]) * sf_ref[sq, d, h] + upd[i][h]
                sf_ref[sq, d, h] = s_new
                sb_ref[sq, d, key_rows(h), val_cols(h)] = s_new.astype(BF16)
        return [x + jnp.concatenate(p, axis=1) for x, p in zip(o_inter, parts)]

    def rows_of(step):
        fwd = [pl.multiple_of((step * cpi + j) * c, c) for j in range(cpi)]
        bwd = [pl.multiple_of((n - 1 - step * cpi - j) * c, c) for j in range(cpi)]
        return fwd, bwd

    def first_half(step, carry):
        r = rows_of(step)
        for (sq, d, j), o in zip(chains, chunk_group(r)):
            o_ref[sq, pl.ds(r[d][j], c), :] = o
        return carry

    def second_half(step, carry):
        r = rows_of(step)
        for (sq, d, j), o in zip(chains, chunk_group(r)):
            o_ref[sq, pl.ds(r[d][j], c), :] = o + o_ref[sq, pl.ds(r[d][j], c), :]
        return carry

    steps = n // cpi
    lax.fori_loop(0, steps // 2, first_half, 0)
    lax.fori_loop(steps // 2, steps, second_half, 0)
    for sq in range(nsq):
        for d in (0, 1):
            for h in range(GLA_HEADS):
                sfin_ref[sq, d, h] = sf_ref[sq, d, h]


def _gla_call(q, k, v, gl, l, wg, bg, s0, nsq, cpi):
    nseq, length, _ = q.shape
    assert (length // GLA_CHUNK) % (2 * cpi) == 0 and nseq % nsq == 0
    seq = lambda width: pl.BlockSpec((nsq, length, width), lambda b: (b, 0, 0))
    layer = lambda shape: pl.BlockSpec((None,) + shape, lambda b: (l,) + (0,) * len(shape))
    in_specs = [seq(GLA_KEY), seq(GLA_KEY), seq(GLA_WIDTH), seq(LANES),
                layer((2, LANES, GLA_KEY)), layer((2, 1, GLA_KEY))]
    args = [q, k, v, gl, wg, bg]
    if s0 is not None:
        in_specs.append(pl.BlockSpec((nsq, None, 2, GLA_HEADS, GLA_DK, GLA_DV), lambda b: (b, l, 0, 0, 0, 0)))
        args.append(s0)
    return pl.pallas_call(
        functools.partial(_gla_kernel, length=length, has_s0=s0 is not None, nsq=nsq, cpi=cpi),
        grid=(nseq // nsq,),
        in_specs=in_specs,
        out_specs=(seq(GLA_WIDTH), pl.BlockSpec((nsq, 2, GLA_HEADS, GLA_DK, GLA_DV), lambda b: (b, 0, 0, 0, 0))),
        out_shape=(jax.ShapeDtypeStruct((nseq, length, GLA_WIDTH), F32),
                   jax.ShapeDtypeStruct((nseq, 2, GLA_HEADS, GLA_DK, GLA_DV), F32)),
        scratch_shapes=[pltpu.VMEM((nsq, 2, GLA_HEADS, GLA_DK, GLA_DV), F32),
                        pltpu.VMEM((nsq, 2, GLA_KEY, GLA_WIDTH), BF16)],
        compiler_params=pltpu.CompilerParams(dimension_semantics=("arbitrary",), vmem_limit_bytes=VMEM_LIMIT),
        name="gla",
    )(*args)


def _mix_kernel(*refs, has_pe, last, row_groups):
    if has_pe:
        (x_ref, pe_ref, yf_ref, yb_ref, u_ref, o_ref, shift_ref, scale_ref, gate_ref, ng_ref, wg_ref, d_ref,
         wglu_ref, bglu_ref, wpa_ref, wpb_ref, wo_ref, fng_ref, gng_ref, out_ref) = refs
    else:
        (x_ref, yf_ref, yb_ref, u_ref, o_ref, shift_ref, scale_ref, gate_ref, ng_ref, wg_ref, d_ref,
         wglu_ref, bglu_ref, wpa_ref, wpb_ref, wo_ref, fng_ref, gng_ref, out_ref) = refs

    def head_norm(o):
        heads = [o[:, h * GLA_DV:(h + 1) * GLA_DV] for h in range(GLA_HEADS)]
        heads = [v * lax.rsqrt(jnp.mean(v * v, axis=-1, keepdims=True) + EPS) for v in heads]
        return jnp.concatenate(heads, axis=1) * gng_ref[...]

    nsb, tm, _ = x_ref.shape
    grp = [(i, slice(j * tm // row_groups, (j + 1) * tm // row_groups)) for i in range(nsb) for j in range(row_groups)]
    x = [x_ref[i, r, :] + pe_ref[r, :] if has_pe else x_ref[i, r, :] for i, r in grp]
    h = [_modulated_norm(v, ng_ref[...], 1.0 + scale_ref[...], shift_ref[...]).astype(BF16) for v in x]
    gts = [_bdot_t(v, wg_ref[...]) for v in h]
    y = [_gelu_tanh(yf_ref[i, r, :] + yb_ref[i, r, :] + d_ref[...] * u_ref[i, r, :]) for i, r in grp]
    glu = [_bdot(v, wglu_ref[...]) for v in y]
    y = [v * _sigmoid(g + bglu_ref[...]) for v, g in zip(y, glu)]
    y_a = [v * _silu(g[:, :S5_WIDTH]) for v, g in zip(y, gts)]
    y_b = [head_norm(o_ref[i, r, :]) * _silu(g[:, S5_WIDTH:2 * S5_WIDTH]) for (i, r), g in zip(grp, gts)]
    p_a = [_bdot(v, wpa_ref[...]) for v in y_a]
    p_b = [_bdot(v, wpb_ref[...]) for v in y_b]
    merged = [_sigmoid(g[:, 2 * S5_WIDTH:2 * S5_WIDTH + D_MODEL]) * a + _sigmoid(g[:, 2 * S5_WIDTH + D_MODEL:]) * b
              for g, a, b in zip(gts, p_a, p_b)]
    xn = [v + gate_ref[...] * _bdot(m, wo_ref[...]) for v, m in zip(x, merged)]
    for (i, r), v in zip(grp, xn):
        if last:
            ms = jnp.mean(v * v, axis=-1, keepdims=True)
            v = v * lax.rsqrt(ms + EPS) * fng_ref[...]
        out_ref[i, r, :] = v


def _mix_call(x, pe, yf, yb, u, o, mod, l, cond_row, ng, wg, s5_d, wglu, bglu, wpa, wpb, wo, fng, gng, tm, last,
              nsb=1):
    nseq, length, _ = x.shape
    row = lambda width: pl.BlockSpec((nsb, tm, width), lambda b, j: (b, j, 0))
    in_specs = [row(D_MODEL)]
    args = [x]
    if pe is not None:
        in_specs.append(pl.BlockSpec((tm, D_MODEL), lambda b, j: (j, 0)))
        args.append(pe)
    in_specs += [row(S5_WIDTH), row(S5_WIDTH), row(S5_WIDTH), row(GLA_WIDTH),
                 _mod_spec(l, cond_row, 0), _mod_spec(l, cond_row, 1), _mod_spec(l, cond_row, 2),
                 _layer_spec(l, (1, D_MODEL)), _layer_spec(l, (GATE_W, D_MODEL)), _layer_spec(l, (1, S5_WIDTH)),
                 _layer_spec(l, (S5_WIDTH, S5_WIDTH)), _layer_spec(l, (1, S5_WIDTH)),
                 _layer_spec(l, (S5_WIDTH, D_MODEL)), _layer_spec(l, (GLA_WIDTH, D_MODEL)),
                 _layer_spec(l, (D_MODEL, D_MODEL)), pl.BlockSpec((1, D_MODEL), lambda b, j: (0, 0)),
                 _layer_spec(l, (1, GLA_WIDTH))]
    args += [yf, yb, u, o, mod, mod, mod, ng, wg, s5_d, wglu, bglu, wpa, wpb, wo, fng, gng]
    return pl.pallas_call(
        functools.partial(_mix_kernel, has_pe=pe is not None, last=last, row_groups=max(1, tm // MXU_TILE)),
        grid=(nseq // nsb, length // tm),
        in_specs=in_specs,
        out_specs=row(D_MODEL),
        out_shape=jax.ShapeDtypeStruct(x.shape, F32),
        compiler_params=pltpu.CompilerParams(dimension_semantics=("arbitrary", "arbitrary"),
                                             vmem_limit_bytes=VMEM_LIMIT),
        name="mix",
    )(*args)


def _grid_pos_embed(length, dim):
    rows = length // GRID_W
    quarter = dim // 4
    freqs = jnp.exp(-math.log(10000.0) * jnp.arange(quarter, dtype=F32) / quarter)

    def sincos(pos):
        ang = pos.astype(F32)[:, None] * freqs[None, :]
        return jnp.concatenate([jnp.sin(ang), jnp.cos(ang)], axis=-1)

    er = sincos(jnp.arange(rows))
    ec = sincos(jnp.arange(GRID_W))
    pe = jnp.concatenate([jnp.broadcast_to(er[:, None, :], (rows, GRID_W, dim // 2)),
                          jnp.broadcast_to(ec[None, :, :], (rows, GRID_W, dim // 2))], axis=-1)
    return pe.reshape(rows * GRID_W, dim)


def kernel(x_prompt, x_sample, c, state_s5_re, state_s5_im, state_gla, c_ctx, norm_g, w_mod, b_mod, w_in, gla_wg_up,
           gla_bg, gla_norm_g, s5_lam_re, s5_lam_im, s5_log_dt, s5_b_re, s5_b_im, s5_c_re, s5_c_im, s5_d, w_glu,
           b_glu, w_pa, w_pb, w_o, final_norm_g):
    depth = w_in.shape[0]
    bp, lp, _ = x_prompt.shape
    bs, ls, _ = x_sample.shape

    cond8 = jnp.zeros((8, D_MODEL), F32).at[0].set(c_ctx).at[1:1 + bs].set(c)
    mod = _mod_call(cond8, w_mod, b_mod)

    a_re, a_im, w_bu, w_c = _s5_params(s5_lam_re, s5_lam_im, s5_log_dt, s5_b_re, s5_b_im, s5_c_re, s5_c_im)

    w_proj, w_gate = _split_w_in(w_in)
    wg_up = jnp.concatenate([gla_wg_up, jnp.zeros((depth, 2, LANES - GLA_RANK, GLA_KEY), F32)], axis=2).astype(BF16)
    w_glu_b, w_pa_b, w_pb_b, w_o_b = (w.astype(BF16) for w in (w_glu, w_pa, w_pb, w_o))

    vec = lambda a: a[:, None, :]
    mod = mod.reshape(depth, 8, 1, 3 * D_MODEL)
    ng, gng, bg = vec(norm_g), vec(gla_norm_g), gla_bg[:, :, None, :]
    common = (ng, w_gate, vec(s5_d), w_glu_b, vec(b_glu), w_pa_b, w_pb_b, w_o_b, final_norm_g[None], gng)
    are_p, aim_p = jnp.repeat(a_re, bp, axis=1), jnp.repeat(a_im, bp, axis=1)
    are_s, aim_s = jnp.repeat(a_re, bs, axis=1), jnp.repeat(a_im, bs, axis=1)
    ctx_row = lambda b: 0
    cond_row = lambda b: b + 1

    pe = _grid_pos_embed(ls, D_MODEL)
    zero_h0 = jnp.zeros((2 * bp, 2 * S5_NC), F32)
    xp, xs = x_prompt, x_sample
    new_re, new_im, new_gla = [], [], []
    for l in range(depth):
        last = l == depth - 1

        u, q, k, v, gl = _proj_call(xp, None, mod, l, ctx_row, ng, w_proj, tm=lp, nsb=4)
        yf, yb, hfin = _s5_call(u, l, w_bu, w_c, are_p, aim_p, zero_h0, tc=16, tiles_per_pass=2)
        o, sfin = _gla_call(q, k, v, gl, l, wg_up, bg, None, nsq=4, cpi=1)
        new_gla.append(sfin)
        xp = _mix_call(xp, None, yf, yb, u, o, mod, l, ctx_row, *common, tm=lp, last=last, nsb=2)
        re_l, im_l = _s5_cols_to_state(hfin.reshape(2, bp, 2 * S5_NC))
        new_re.append(jnp.swapaxes(re_l, 0, 1))
        new_im.append(jnp.swapaxes(im_l, 0, 1))

        pe_l = pe if l == 0 else None
        u, q, k, v, gl = _proj_call(xs, pe_l, mod, l, cond_row, ng, w_proj, tm=1024)
        h0 = _s5_state_to_cols(jnp.swapaxes(state_s5_re[:, l], 0, 1), jnp.swapaxes(state_s5_im[:, l], 0, 1))
        yf, yb, _ = _s5_call(u, l, w_bu, w_c, are_s, aim_s, h0.reshape(2 * bs, 2 * S5_NC), tc=32, tiles_per_pass=4)
        o, _ = _gla_call(q, k, v, gl, l, wg_up, bg, state_gla, nsq=2, cpi=2)
        xs = _mix_call(xs, pe_l, yf, yb, u, o, mod, l, cond_row, *common, tm=512, last=last)

    return (xp, xs, jnp.stack(new_re, axis=1), jnp.stack(new_im, axis=1), jnp.stack(new_gla, axis=1))
```

```python
import functools
import math

import numpy as np
import jax
import jax.numpy as jnp
from jax import lax
from jax.experimental import pallas as pl
from jax.experimental.pallas import tpu as pltpu

F32 = jnp.float32
BF16 = jnp.bfloat16

D_MODEL = 1024
GRID_W = 64
S5_WIDTH = 512
S5_GROUPS = 32
S5_CH = 16
S5_STATE = 64
S5_NC = S5_GROUPS * S5_STATE
S5_SLAB_GROUPS = 8
S5_SLABS = S5_GROUPS // S5_SLAB_GROUPS
S5_SLAB_NC = S5_SLAB_GROUPS * S5_STATE
S5_SLAB_COLS = 2 * S5_SLAB_NC
GLA_HEADS = 4
GLA_DK = 64
GLA_DV = 128
GLA_KEY = GLA_HEADS * GLA_DK
GLA_WIDTH = GLA_HEADS * GLA_DV
GLA_RANK = 16
GLA_GATE_NORM = 16.0
GLA_CHUNK = 128
EPS = 1e-6
LANES = 128
PROJ_W = S5_WIDTH + 2 * GLA_KEY + GLA_WIDTH + LANES
GATE_W = 2 * S5_WIDTH + 2 * D_MODEL
VMEM_LIMIT = 60 * 1024 * 1024
MXU_TILE = 256


def _bdot(a, b):
    return jnp.dot(a.astype(BF16), b.astype(BF16), preferred_element_type=F32)


def _bdot_t(a, bt):
    return lax.dot_general(a.astype(BF16), bt.astype(BF16), (((1,), (1,)), ((), ())), preferred_element_type=F32)


def _split2(x):
    hi = x.astype(BF16)
    lo = (x - hi.astype(F32)).astype(BF16)
    return hi, lo


def _sigmoid(x):
    return 1.0 / (1.0 + jnp.exp2(x * (-math.log2(math.e))))


def _silu(x):
    return x * _sigmoid(x)


def _gelu_tanh(x):
    c = math.sqrt(2.0 / math.pi)
    return 0.5 * x * (1.0 + jnp.tanh(c * (x + 0.044715 * (x * x * x))))


def _log_sigmoid(x):
    return jnp.minimum(x, 0.0) - jnp.log(1.0 + jnp.exp(-jnp.abs(x)))


def _modulated_norm(x, ng, scale1p, shift):
    ms = jnp.mean(x * x, axis=-1, keepdims=True)
    return (x * lax.rsqrt(ms + EPS) * ng) * scale1p + shift


def _mod_kernel(cond_ref, w_ref, b_ref, out_ref):
    c = cond_ref[...]
    s_hi, s_lo = _split2(_silu(c))
    w_hi, w_lo = _split2(w_ref[0])
    acc = jnp.dot(s_hi, w_hi, preferred_element_type=F32)
    acc += jnp.dot(s_lo, w_hi, preferred_element_type=F32)
    acc += jnp.dot(s_hi, w_lo, preferred_element_type=F32)
    out_ref[0] = acc + b_ref[0]


def _mod_call(cond8, w_mod, b_mod):
    depth = w_mod.shape[0]
    nb = 1536
    return pl.pallas_call(
        _mod_kernel,
        grid=(depth, 3 * D_MODEL // nb),
        in_specs=[
            pl.BlockSpec((8, D_MODEL), lambda l, j: (0, 0)),
            pl.BlockSpec((1, D_MODEL, nb), lambda l, j: (l, 0, j)),
            pl.BlockSpec((1, 1, nb), lambda l, j: (l, 0, j)),
        ],
        out_specs=pl.BlockSpec((1, 8, nb), lambda l, j: (l, 0, j)),
        out_shape=jax.ShapeDtypeStruct((depth, 8, 3 * D_MODEL), F32),
        compiler_params=pltpu.CompilerParams(dimension_semantics=("arbitrary", "arbitrary")),
        name="adaln_mod",
    )(cond8, w_mod, b_mod.reshape(depth, 1, 3 * D_MODEL))


def _zoh(lre, lim, ldt):
    dt = jnp.exp(ldt)
    mag = jnp.exp(lre * dt)
    a_re = mag * jnp.cos(lim * dt)
    a_im = mag * jnp.sin(lim * dt)
    n_re = a_re - 1.0
    inv = 1.0 / (lre * lre + lim * lim)
    return a_re, a_im, (n_re * lre + a_im * lim) * inv, (a_im * lre - n_re * lim) * inv


def _zoh_kernel(lre_ref, lim_ref, ldt_ref, lre_x_ref, lim_x_ref, ldt_x_ref, bre_ref, bim_ref, cre_ref, cim_ref,
                are_ref, aim_ref, wbu_ref, wc_ref, *, depth):
    a_re, a_im, _, _ = _zoh(lre_ref[...], lim_ref[...], ldt_ref[...])
    are_ref[...] = a_re
    aim_ref[...] = a_im
    _, _, c_re, c_im = _zoh(lre_x_ref[...], lim_x_ref[...], ldt_x_ref[...])
    bre = bre_ref[...]
    bim = bim_ref[...]
    bbar = [c_re * bre - c_im * bim, c_re * bim + c_im * bre]
    cmat = [cre_ref[...], -cim_ref[...]]
    wbu_ref[...] = jnp.zeros_like(wbu_ref)
    wc_ref[...] = jnp.zeros_like(wc_ref)
    for l in range(depth):
        for s in range(S5_SLABS):
            for gl in range(S5_SLAB_GROUPS):
                g = S5_SLAB_GROUPS * s + gl
                for ri in range(2):
                    col = ri * S5_SLAB_NC + gl * S5_STATE
                    for d in range(2):
                        src = ((l * 2 + d) * S5_GROUPS + g) * S5_CH
                        row = (d * S5_SLAB_GROUPS + gl) * S5_CH
                        wbu_ref[l, s, row:row + S5_CH, col:col + S5_STATE] = bbar[ri][src:src + S5_CH, :].astype(BF16)
                    src = (l * S5_GROUPS + g) * S5_STATE
                    wc_ref[l, s, col:col + S5_STATE, gl * S5_CH:(gl + 1) * S5_CH] = (
                        cmat[ri][src:src + S5_STATE, :].astype(BF16))


def _s5_params(s5_lam_re, s5_lam_im, s5_log_dt, s5_b_re, s5_b_im, s5_c_re, s5_c_im):
    depth = s5_lam_re.shape[0]
    full = (depth, 2, S5_GROUPS, S5_CH, S5_STATE)
    expand = lambda a: jnp.broadcast_to(a, full).reshape(-1, S5_STATE)
    small = lambda a: a.reshape(depth * 2 * S5_GROUPS, -1)
    ct = lambda a: jnp.swapaxes(a, -1, -2).reshape(-1, S5_CH)
    a_re, a_im, w_bu, w_c = pl.pallas_call(
        functools.partial(_zoh_kernel, depth=depth),
        out_shape=(jax.ShapeDtypeStruct((depth * 2 * S5_GROUPS, S5_STATE), F32),
                   jax.ShapeDtypeStruct((depth * 2 * S5_GROUPS, S5_STATE), F32),
                   jax.ShapeDtypeStruct((depth, S5_SLABS, 2 * LANES, S5_SLAB_COLS), BF16),
                   jax.ShapeDtypeStruct((depth, S5_SLABS, S5_SLAB_COLS, LANES), BF16)),
        name="s5_zoh",
    )(small(s5_lam_re), small(s5_lam_im), small(s5_log_dt),
      expand(s5_lam_re[:, :, :, None, :]), expand(s5_lam_im[:, :, :, None, :]), expand(s5_log_dt[:, :, :, None, None]),
      expand(jnp.swapaxes(s5_b_re, -1, -2)[:, None]), expand(jnp.swapaxes(s5_b_im, -1, -2)[:, None]),
      ct(s5_c_re), ct(s5_c_im))
    return a_re.reshape(depth, 2, S5_NC), a_im.reshape(depth, 2, S5_NC), w_bu, w_c


def _s5_state_to_cols(re, im):
    lead = re.shape[:-2]
    st = jnp.stack([re.reshape(lead + (S5_SLABS, S5_SLAB_NC)), im.reshape(lead + (S5_SLABS, S5_SLAB_NC))], axis=-2)
    return st.reshape(lead + (2 * S5_NC,))


def _s5_cols_to_state(cols):
    lead = cols.shape[:-1]
    st = cols.reshape(lead + (S5_SLABS, 2, S5_SLAB_NC))
    re = st[..., 0, :].reshape(lead + (S5_GROUPS, S5_STATE))
    im = st[..., 1, :].reshape(lead + (S5_GROUPS, S5_STATE))
    return re, im


_IN_OFFS = [int(o) for o in np.cumsum([0, S5_WIDTH, S5_WIDTH, GLA_KEY, GLA_KEY, GLA_WIDTH, GLA_WIDTH, GLA_RANK,
                                       D_MODEL, D_MODEL])]


def _split_w_in_kernel(u_ref, qkv_ref, code_ref, ga_ref, gb_ref, m_ref, proj_ref, gate_ref):
    rows = 0
    for src in (u_ref, qkv_ref, code_ref):
        proj_ref[rows:rows + src.shape[1], :] = src[0].astype(BF16)
        rows += src.shape[1]
    proj_ref[rows:, :] = jnp.zeros((PROJ_W - rows, proj_ref.shape[1]), BF16)
    rows = 0
    for src in (ga_ref, gb_ref, m_ref):
        gate_ref[rows:rows + src.shape[1], :] = src[0].astype(BF16)
        rows += src.shape[1]


def _split_w_in(w_in):
    o = _IN_OFFS
    depth = w_in.shape[0]
    wt = jnp.swapaxes(w_in, 1, 2)
    tn = D_MODEL // 2
    rows = lambda lo, hi: pl.BlockSpec((pl.Element(1), pl.Element(hi - lo), pl.Element(tn)),
                                       lambda l, j: (l, lo, j * tn))
    out = lambda n: pl.BlockSpec((None, n, tn), lambda l, j: (l, 0, j))
    return pl.pallas_call(
        _split_w_in_kernel,
        grid=(depth, D_MODEL // tn),
        in_specs=[rows(o[0], o[1]), rows(o[2], o[5]), rows(o[6], o[7]), rows(o[1], o[2]), rows(o[5], o[6]),
                  rows(o[7], o[9])],
        out_specs=(out(PROJ_W), out(GATE_W)),
        out_shape=(jax.ShapeDtypeStruct((depth, PROJ_W, D_MODEL), BF16),
                   jax.ShapeDtypeStruct((depth, GATE_W, D_MODEL), BF16)),
        compiler_params=pltpu.CompilerParams(dimension_semantics=("arbitrary", "arbitrary"),
                                             vmem_limit_bytes=VMEM_LIMIT),
        name="split_w_in",
    )(wt, wt, wt, wt, wt, wt)


def _proj_kernel(*refs, has_pe):
    if has_pe:
        x_ref, pe_ref, shift_ref, scale_ref, ng_ref, w_ref, u_ref, q_ref, k_ref, v_ref, gl_ref = refs
    else:
        x_ref, shift_ref, scale_ref, ng_ref, w_ref, u_ref, q_ref, k_ref, v_ref, gl_ref = refs
    nsb, tm, _ = x_ref.shape
    rg = max(1, tm // (2 * MXU_TILE))
    grp = [(i, slice(j * tm // rg, (j + 1) * tm // rg)) for i in range(nsb) for j in range(rg)]
    x = [x_ref[i, r, :] + pe_ref[r, :] if has_pe else x_ref[i, r, :] for i, r in grp]
    h = [_modulated_norm(v, ng_ref[...], 1.0 + scale_ref[...], shift_ref[...]) for v in x]
    p = [_bdot_t(v, w_ref[...]) for v in h]
    for (i, r), v in zip(grp, p):
        o = 0
        for ref, width in ((u_ref, S5_WIDTH), (q_ref, GLA_KEY), (k_ref, GLA_KEY), (v_ref, GLA_WIDTH), (gl_ref, LANES)):
            ref[i, r, :] = v[:, o:o + width].astype(ref.dtype)
            o += width


def _mod_spec(l, cond_row, part):
    return pl.BlockSpec((None, None, 1, D_MODEL), lambda b, j: (l, cond_row(b), 0, part))


def _layer_spec(l, shape):
    return pl.BlockSpec((None,) + shape, lambda b, j: (l,) + (0,) * len(shape))


def _proj_call(x, pe, mod, l, cond_row, ng, w, tm, nsb=1):
    nseq, length, _ = x.shape
    row = lambda width: pl.BlockSpec((nsb, tm, width), lambda b, j: (b, j, 0))
    in_specs = [row(D_MODEL)]
    args = [x]
    if pe is not None:
        in_specs.append(pl.BlockSpec((tm, D_MODEL), lambda b, j: (j, 0)))
        args.append(pe)
    in_specs += [_mod_spec(l, cond_row, 0), _mod_spec(l, cond_row, 1), _layer_spec(l, (1, D_MODEL)),
                 _layer_spec(l, (PROJ_W, D_MODEL))]
    args += [mod, mod, ng, w]
    sds = lambda width, dt: jax.ShapeDtypeStruct((nseq, length, width), dt)
    return pl.pallas_call(
        functools.partial(_proj_kernel, has_pe=pe is not None),
        grid=(nseq // nsb, length // tm),
        in_specs=in_specs,
        out_specs=(row(S5_WIDTH), row(GLA_KEY), row(GLA_KEY), row(GLA_WIDTH), row(LANES)),
        out_shape=(sds(S5_WIDTH, F32), sds(GLA_KEY, F32), sds(GLA_KEY, F32), sds(GLA_WIDTH, BF16), sds(LANES, F32)),
        compiler_params=pltpu.CompilerParams(dimension_semantics=("arbitrary", "arbitrary"),
                                             vmem_limit_bytes=VMEM_LIMIT),
        name="proj",
    )(*args)


def _s5_perms(nb, tc):
    r = nb * tc
    pf = np.zeros((2 * r, r), np.float32)
    pb = np.zeros((2 * r, r), np.float32)
    for t in range(tc):
        for b in range(nb):
            pf[t * 2 * nb + b, b * tc + t] = 1.0
            pb[t * 2 * nb + nb + b, b * tc + (tc - 1 - t)] = 1.0
    return pf, pb


def _s5_store_unpacked(y2, yf_ref, yb_ref, t_f, t_b, nb, tc):
    rs = 2 * nb
    for t in range(tc):
        yf_ref[:, t_f + t, :] = y2[t * rs:t * rs + nb]
        yb_ref[:, t_b + tc - 1 - t, :] = y2[t * rs + nb:(t + 1) * rs]


def _s5_kernel(*refs, nb, tc, tiles_per_pass):
    use_perm = nb % 8 != 0
    uf1_ref, ub1_ref, uf2_ref, ub2_ref, uf0_ref, ub0_ref = refs[:6]
    pf_ref, pb_ref = refs[6:8] if use_perm else (None, None)
    wbu_ref, wc_ref, are_ref, aim_ref, h0_ref, yf_ref, yb_ref, hfin_ref, bua_ref, bub_ref, hb_ref, st_ref = refs[-12:]
    i = pl.program_id(0)
    rows = nb * tc
    rs = 2 * nb

    def packed_inputs(uf_ref, ub_ref):
        if not use_perm:
            return tuple(jnp.concatenate([r[:, t, :] for t in range(tc)], axis=0).astype(BF16) for r in (uf_ref, ub_ref))
        uf = uf_ref[...].reshape(rows, S5_WIDTH).astype(BF16)
        ub = ub_ref[...].reshape(rows, S5_WIDTH).astype(BF16)
        u2f = jnp.dot(pf_ref[...], uf, preferred_element_type=F32).astype(BF16)
        u2b = jnp.dot(pb_ref[...], ub, preferred_element_type=F32).astype(BF16)
        return [jnp.concatenate([u2f[:, s * LANES:(s + 1) * LANES], u2b[:, s * LANES:(s + 1) * LANES]], axis=1)
                for s in range(S5_SLABS)]

    def bu_slab(lhs, s, dst_ref):
        cols = slice(s * S5_SLAB_COLS, (s + 1) * S5_SLAB_COLS)
        if use_perm:
            dst_ref[:, cols] = jnp.dot(lhs[s], wbu_ref[s], preferred_element_type=F32)
            return
        uf, ub = lhs
        bf = jnp.dot(uf[:, s * LANES:(s + 1) * LANES], wbu_ref[s, :LANES, :], preferred_element_type=F32)
        bb = jnp.dot(ub[:, s * LANES:(s + 1) * LANES], wbu_ref[s, LANES:, :], preferred_element_type=F32)
        for t in range(tc):
            dst_ref[t * rs:t * rs + nb, cols] = bf[t * nb:(t + 1) * nb]
            dst_ref[t * rs + nb:(t + 1) * rs, cols] = bb[(tc - 1 - t) * nb:(tc - t) * nb]

    @pl.when(i == 0)
    def _():
        st_ref[...] = h0_ref[...]
        lhs0 = packed_inputs(uf0_ref, ub0_ref)
        for s in range(S5_SLABS):
            bu_slab(lhs0, s, bua_ref)

    half_step = functools.partial(_s5_half_step, wc_ref=wc_ref, are_ref=are_ref, aim_ref=aim_ref, hb_ref=hb_ref,
                                  st_ref=st_ref, bu_slab=bu_slab, rs=rs, tc=tc, tiles_per_pass=tiles_per_pass)
    y2 = half_step(bua_ref, bub_ref, packed_inputs(uf1_ref, ub1_ref))
    _s5_store_unpacked(y2, yf_ref, yb_ref, 0, tc, nb, tc)
    y2 = half_step(bub_ref, bua_ref, packed_inputs(uf2_ref, ub2_ref))
    _s5_store_unpacked(y2, yf_ref, yb_ref, tc, 0, nb, tc)
    hfin_ref[...] = st_ref[...]


def _s5_half_step(cur_ref, nxt_ref, lhs_next, *, wc_ref, are_ref, aim_ref, hb_ref, st_ref, bu_slab,
                  rs, tc, tiles_per_pass):
    slab_tiles = S5_SLAB_NC // LANES
    grp = max(1, 16 // rs)
    ys = []
    for s in range(S5_SLABS):
        bu_slab(lhs_next, s, nxt_ref)
        for c0 in range(slab_tiles * s, slab_tiles * (s + 1), tiles_per_pass):
            cts = list(range(c0, c0 + tiles_per_pass))
            cre = [(ct // slab_tiles) * S5_SLAB_COLS + (ct % slab_tiles) * LANES for ct in cts]
            cim = [c + S5_SLAB_NC for c in cre]
            a_re = [are_ref[:, ct * LANES:(ct + 1) * LANES] for ct in cts]
            a_im = [aim_ref[:, ct * LANES:(ct + 1) * LANES] for ct in cts]
            h_re = [st_ref[:, c:c + LANES] for c in cre]
            h_im = [st_ref[:, c:c + LANES] for c in cim]
            for t0 in range(0, tc, grp):
                out_re = [[] for _ in cts]
                out_im = [[] for _ in cts]
                for t in range(t0, t0 + grp):
                    r0 = t * rs
                    for j in range(len(cts)):
                        b_re = cur_ref[r0:r0 + rs, cre[j]:cre[j] + LANES]
                        b_im = cur_ref[r0:r0 + rs, cim[j]:cim[j] + LANES]
                        n_re = a_re[j] * h_re[j] - a_im[j] * h_im[j] + b_re
                        n_im = a_re[j] * h_im[j] + a_im[j] * h_re[j] + b_im
                        h_re[j], h_im[j] = n_re, n_im
                        out_re[j].append(n_re)
                        out_im[j].append(n_im)
                for j in range(len(cts)):
                    blk_re = out_re[j][0] if grp == 1 else jnp.concatenate(out_re[j], axis=0)
                    blk_im = out_im[j][0] if grp == 1 else jnp.concatenate(out_im[j], axis=0)
                    hb_ref[t0 * rs:(t0 + grp) * rs, cre[j]:cre[j] + LANES] = blk_re.astype(BF16)
                    hb_ref[t0 * rs:(t0 + grp) * rs, cim[j]:cim[j] + LANES] = blk_im.astype(BF16)
            for j in range(len(cts)):
                st_ref[:, cre[j]:cre[j] + LANES] = h_re[j]
                st_ref[:, cim[j]:cim[j] + LANES] = h_im[j]
        ys.append(jnp.dot(hb_ref[:, s * S5_SLAB_COLS:(s + 1) * S5_SLAB_COLS], wc_ref[s], preferred_element_type=F32))
    return jnp.concatenate(ys, axis=1)


def _s5_call(u, l, w_bu, w_c, are, aim, h0, tc, tiles_per_pass):
    nb, length, _ = u.shape
    n = length // tc
    rows = nb * tc
    const = lambda shape: pl.BlockSpec(shape, lambda i: (0,) * len(shape))
    layer = lambda shape: pl.BlockSpec((None,) + shape, lambda i: (l,) + (0,) * len(shape))
    blk = lambda chunk: pl.BlockSpec((nb, tc, S5_WIDTH), lambda i: (0, chunk(i), 0))
    out_f = pl.BlockSpec((nb, 2 * tc, S5_WIDTH), lambda i: (0, i, 0))
    out_b = pl.BlockSpec((nb, 2 * tc, S5_WIDTH), lambda i: (0, n // 2 - 1 - i, 0))
    second = lambda i: jnp.minimum(2 * i + 2, n - 1)
    in_specs = [blk(lambda i: 2 * i + 1), blk(lambda i: n - 2 - 2 * i), blk(second), blk(lambda i: n - 1 - second(i)),
                blk(lambda i: 0), blk(lambda i: n - 1)]
    args = [u] * 6
    if nb % 8 != 0:
        pf, pb = _s5_perms(nb, tc)
        in_specs += [const((2 * rows, rows)), const((2 * rows, rows))]
        args += [jnp.asarray(pf, BF16), jnp.asarray(pb, BF16)]
    in_specs += [layer((S5_SLABS, 2 * LANES, S5_SLAB_COLS)), layer((S5_SLABS, S5_SLAB_COLS, LANES)),
                 layer((2 * nb, S5_NC)), layer((2 * nb, S5_NC)), const((2 * nb, 2 * S5_NC))]
    args += [w_bu, w_c, are, aim, h0]
    return pl.pallas_call(
        functools.partial(_s5_kernel, nb=nb, tc=tc, tiles_per_pass=tiles_per_pass),
        grid=(n // 2,),
        in_specs=in_specs,
        out_specs=(out_f, out_b, const((2 * nb, 2 * S5_NC))),
        out_shape=(jax.ShapeDtypeStruct(u.shape, F32), jax.ShapeDtypeStruct(u.shape, F32),
                   jax.ShapeDtypeStruct((2 * nb, 2 * S5_NC), F32)),
        scratch_shapes=[pltpu.VMEM((2 * rows, 2 * S5_NC), F32), pltpu.VMEM((2 * rows, 2 * S5_NC), F32),
                        pltpu.VMEM((2 * rows, 2 * S5_NC), BF16), pltpu.VMEM((2 * nb, 2 * S5_NC), F32)],
        compiler_params=pltpu.CompilerParams(dimension_semantics=("arbitrary",), vmem_limit_bytes=VMEM_LIMIT),
        name="s5_scan",
    )(*args)


def _gla_kernel(*refs, length, has_s0, nsq, cpi):
    q_ref, k_ref, v_ref, gl_ref, wg_ref, bg_ref = refs[:6]
    s0_ref = refs[6] if has_s0 else None
    o_ref, sfin_ref, sf_ref, sb_ref = refs[-4:]
    c = GLA_CHUNK
    n = length // c
    ri = lax.broadcasted_iota(jnp.int32, (c, c), 0)
    ci = lax.broadcasted_iota(jnp.int32, (c, c), 1)
    lane = lax.broadcasted_iota(jnp.int32, (1, GLA_KEY), 1)
    head_mask = [(lane >= h * GLA_DK) & (lane < (h + 1) * GLA_DK) for h in range(GLA_HEADS)]
    causal = [ci <= ri, ci >= ri]
    tri = [m.astype(BF16) for m in causal]
    end = (c - 1, 0)
    chains = [(sq, d, j) for j in range(cpi) for sq in range(nsq) for d in (0, 1)]
    key_rows = lambda h: slice(h * GLA_DK, (h + 1) * GLA_DK)
    val_cols = lambda h: slice(h * GLA_DV, (h + 1) * GLA_DV)

    sb_ref[...] = jnp.zeros_like(sb_ref)
    for sq in range(nsq):
        for d in (0, 1):
            for h in range(GLA_HEADS):
                if has_s0:
                    sf_ref[sq, d, h] = s0_ref[sq, d, h]
                    sb_ref[sq, d, key_rows(h), val_cols(h)] = s0_ref[sq, d, h].astype(BF16)
                else:
                    sf_ref[sq, d, h] = jnp.zeros((GLA_DK, GLA_DV), F32)

    def chunk_group(r):
        row = lambda d, j: r[d][j]
        qc = [q_ref[sq, pl.ds(row(d, j), c), :] * (GLA_DK ** -0.5) for sq, d, j in chains]
        kc = [k_ref[sq, pl.ds(row(d, j), c), :] for sq, d, j in chains]
        vc = [v_ref[sq, pl.ds(row(d, j), c), :] for sq, d, j in chains]
        logits = [_bdot(gl_ref[sq, pl.ds(row(d, j), c), :], wg_ref[d]) + bg_ref[d] for sq, d, j in chains]
        g = [_split2(_log_sigmoid(x) * (1.0 / GLA_GATE_NORM)) for x in logits]
        b = [jnp.dot(tri[d], hi, preferred_element_type=F32) + jnp.dot(tri[d], lo, preferred_element_type=F32)
             for (sq, d, j), (hi, lo) in zip(chains, g)]
        bt = [x.T for x in b]
        kt = [x.T for x in kc]
        b_mid = [x[c // 2:c // 2 + 1, :] for x in b]
        bt_mid = [x[:, c // 2:c // 2 + 1] for x in bt]
        bt_end = [x[:, end[d]:end[d] + 1] for (sq, d, j), x in zip(chains, bt)]
        q_in = [(x * jnp.exp(y)).astype(BF16) for x, y in zip(qc, b)]
        q_mid = [(x * jnp.exp(y - m)).astype(BF16) for x, y, m in zip(qc, b, b_mid)]
        k_mid_t = [(x * jnp.exp(m - y)).astype(BF16) for x, y, m in zip(kt, bt, bt_mid)]
        k_end_t = [(x * jnp.exp(e - y)).astype(BF16) for x, y, e in zip(kt, bt, bt_end)]
        q_st = [jnp.concatenate([jnp.where(head_mask[h], x, jnp.zeros_like(x)) for h in range(GLA_HEADS)], axis=0)
                for x in q_mid]
        a = [jnp.dot(x, y, preferred_element_type=F32) for x, y in zip(q_st, k_mid_t)]
        parts = [[] for _ in chains]
        for h in range(GLA_HEADS):
            for i, (sq, d, j) in enumerate(chains):
                a_h = jnp.where(causal[d], a[i][h * c:(h + 1) * c, :], 0.0).astype(BF16)
                parts[i].append(jnp.dot(a_h, vc[i][:, val_cols(h)], preferred_element_type=F32))
        upd = [[jnp.dot(k_end_t[i][key_rows(h), :], vc[i][:, val_cols(h)], preferred_element_type=F32)
                for h in range(GLA_HEADS)] for i in range(len(chains))]
        o_inter = [None] * len(chains)
        for i, (sq, d, j) in enumerate(chains):
            o_inter[i] = jnp.dot(q_in[i], sb_ref[sq, d], preferred_element_type=F32)
            for h in range(GLA_HEADS):
                s_new = jnp.exp(bt_end[i][key_rows(h), :]) * sf_ref[sq, d, h] + upd[i][h]
                sf_ref[sq, d, h] = s_new
                sb_ref[sq, d, key_rows(h), val_cols(h)] = s_new.astype(BF16)
        return [x + jnp.concatenate(p, axis=1) for x, p in zip(o_inter, parts)]

    def rows_of(step):
        fwd = [pl.multiple_of((step * cpi + j) * c, c) for j in range(cpi)]
        bwd = [pl.multiple_of((n - 1 - step * cpi - j) * c, c) for j in range(cpi)]
        return fwd, bwd

    def first_half(step, carry):
        r = rows_of(step)
        for (sq, d, j), o in zip(chains, chunk_group(r)):
            o_ref[sq, pl.ds(r[d][j], c), :] = o
        return carry

    def second_half(step, carry):
        r = rows_of(step)
        for (sq, d, j), o in zip(chains, chunk_group(r)):
            o_ref[sq, pl.ds(r[d][j], c), :] = o + o_ref[sq, pl.ds(r[d][j], c), :]
        return carry

    steps = n // cpi
    lax.fori_loop(0, steps // 2, first_half, 0)
    lax.fori_loop(steps // 2, steps, second_half, 0)
    for sq in range(nsq):
        for d in (0, 1):
            for h in range(GLA_HEADS):
                sfin_ref[sq, d, h] = sf_ref[sq, d, h]


def _gla_call(q, k, v, gl, l, wg, bg, s0, nsq, cpi):
    nseq, length, _ = q.shape
    assert (length // GLA_CHUNK) % (2 * cpi) == 0 and nseq % nsq == 0
    seq = lambda width: pl.BlockSpec((nsq, length, width), lambda b: (b, 0, 0))
    layer = lambda shape: pl.BlockSpec((None,) + shape, lambda b: (l,) + (0,) * len(shape))
    in_specs = [seq(GLA_KEY), seq(GLA_KEY), seq(GLA_WIDTH), seq(LANES),
                layer((2, LANES, GLA_KEY)), layer((2, 1, GLA_KEY))]
    args = [q, k, v, gl, wg, bg]
    if s0 is not None:
        in_specs.append(pl.BlockSpec((nsq, None, 2, GLA_HEADS, GLA_DK, GLA_DV), lambda b: (b, l, 0, 0, 0, 0)))
        args.append(s0)
    return pl.pallas_call(
        functools.partial(_gla_kernel, length=length, has_s0=s0 is not None, nsq=nsq, cpi=cpi),
        grid=(nseq // nsq,),
        in_specs=in_specs,
        out_specs=(seq(GLA_WIDTH), pl.BlockSpec((nsq, 2, GLA_HEADS, GLA_DK, GLA_DV), lambda b: (b, 0, 0, 0, 0))),
        out_shape=(jax.ShapeDtypeStruct((nseq, length, GLA_WIDTH), F32),
                   jax.ShapeDtypeStruct((nseq, 2, GLA_HEADS, GLA_DK, GLA_DV), F32)),
        scratch_shapes=[pltpu.VMEM((nsq, 2, GLA_HEADS, GLA_DK, GLA_DV), F32),
                        pltpu.VMEM((nsq, 2, GLA_KEY, GLA_WIDTH), BF16)],
        compiler_params=pltpu.CompilerParams(dimension_semantics=("arbitrary",), vmem_limit_bytes=VMEM_LIMIT),
        name="gla",
    )(*args)


def _mix_kernel(*refs, has_pe, last, row_groups):
    if has_pe:
        (x_ref, pe_ref, yf_ref, yb_ref, u_ref, o_ref, shift_ref, scale_ref, gate_ref, ng_ref, wg_ref, d_ref,
         wglu_ref, bglu_ref, wpa_ref, wpb_ref, wo_ref, fng_ref, gng_ref, out_ref) = refs
    else:
        (x_ref, yf_ref, yb_ref, u_ref, o_ref, shift_ref, scale_ref, gate_ref, ng_ref, wg_ref, d_ref,
         wglu_ref, bglu_ref, wpa_ref, wpb_ref, wo_ref, fng_ref, gng_ref, out_ref) = refs

    def head_norm(o):
        heads = [o[:, h * GLA_DV:(h + 1) * GLA_DV] for h in range(GLA_HEADS)]
        heads = [v * lax.rsqrt(jnp.mean(v * v, axis=-1, keepdims=True) + EPS) for v in heads]
        return jnp.concatenate(heads, axis=1) * gng_ref[...]

    nsb, tm, _ = x_ref.shape
    grp = [(i, slice(j * tm // row_groups, (j + 1) * tm // row_groups)) for i in range(nsb) for j in range(row_groups)]
    x = [x_ref[i, r, :] + pe_ref[r, :] if has_pe else x_ref[i, r, :] for i, r in grp]
    h = [_modulated_norm(v, ng_ref[...], 1.0 + scale_ref[...], shift_ref[...]).astype(BF16) for v in x]
    gts = [_bdot_t(v, wg_ref[...]) for v in h]
    y = [_gelu_tanh(yf_ref[i, r, :] + yb_ref[i, r, :] + d_ref[...] * u_ref[i, r, :]) for i, r in grp]
    glu = [_bdot(v, wglu_ref[...]) for v in y]
    y = [v * _sigmoid(g + bglu_ref[...]) for v, g in zip(y, glu)]
    y_a = [v * _silu(g[:, :S5_WIDTH]) for v, g in zip(y, gts)]
    y_b = [head_norm(o_ref[i, r, :]) * _silu(g[:, S5_WIDTH:2 * S5_WIDTH]) for (i, r), g in zip(grp, gts)]
    p_a = [_bdot(v, wpa_ref[...]) for v in y_a]
    p_b = [_bdot(v, wpb_ref[...]) for v in y_b]
    merged = [_sigmoid(g[:, 2 * S5_WIDTH:2 * S5_WIDTH + D_MODEL]) * a + _sigmoid(g[:, 2 * S5_WIDTH + D_MODEL:]) * b
              for g, a, b in zip(gts, p_a, p_b)]
    xn = [v + gate_ref[...] * _bdot(m, wo_ref[...]) for v, m in zip(x, merged)]
    for (i, r), v in zip(grp, xn):
        if last:
            ms = jnp.mean(v * v, axis=-1, keepdims=True)
            v = v * lax.rsqrt(ms + EPS) * fng_ref[...]
        out_ref[i, r, :] = v


def _mix_call(x, pe, yf, yb, u, o, mod, l, cond_row, ng, wg, s5_d, wglu, bglu, wpa, wpb, wo, fng, gng, tm, last,
              nsb=1):
    nseq, length, _ = x.shape
    row = lambda width: pl.BlockSpec((nsb, tm, width), lambda b, j: (b, j, 0))
    in_specs = [row(D_MODEL)]
    args = [x]
    if pe is not None:
        in_specs.append(pl.BlockSpec((tm, D_MODEL), lambda b, j: (j, 0)))
        args.append(pe)
    in_specs += [row(S5_WIDTH), row(S5_WIDTH), row(S5_WIDTH), row(GLA_WIDTH),
                 _mod_spec(l, cond_row, 0), _mod_spec(l, cond_row, 1), _mod_spec(l, cond_row, 2),
                 _layer_spec(l, (1, D_MODEL)), _layer_spec(l, (GATE_W, D_MODEL)), _layer_spec(l, (1, S5_WIDTH)),
                 _layer_spec(l, (S5_WIDTH, S5_WIDTH)), _layer_spec(l, (1, S5_WIDTH)),
                 _layer_spec(l, (S5_WIDTH, D_MODEL)), _layer_spec(l, (GLA_WIDTH, D_MODEL)),
                 _layer_spec(l, (D_MODEL, D_MODEL)), pl.BlockSpec((1, D_MODEL), lambda b, j: (0, 0)),
                 _layer_spec(l, (1, GLA_WIDTH))]
    args += [yf, yb, u, o, mod, mod, mod, ng, wg, s5_d, wglu, bglu, wpa, wpb, wo, fng, gng]
    return pl.pallas_call(
        functools.partial(_mix_kernel, has_pe=pe is not None, last=last, row_groups=max(1, tm // MXU_TILE)),
        grid=(nseq // nsb, length // tm),
        in_specs=in_specs,
        out_specs=row(D_MODEL),
        out_shape=jax.ShapeDtypeStruct(x.shape, F32),
        compiler_params=pltpu.CompilerParams(dimension_semantics=("arbitrary", "arbitrary"),
                                             vmem_limit_bytes=VMEM_LIMIT),
        name="mix",
    )(*args)


def _grid_pos_embed(length, dim):
    rows = length // GRID_W
    quarter = dim // 4
    freqs = jnp.exp(-math.log(10000.0) * jnp.arange(quarter, dtype=F32) / quarter)

    def sincos(pos):
        ang = pos.astype(F32)[:, None] * freqs[None, :]
        return jnp.concatenate([jnp.sin(ang), jnp.cos(ang)], axis=-1)

    er = sincos(jnp.arange(rows))
    ec = sincos(jnp.arange(GRID_W))
    pe = jnp.concatenate([jnp.broadcast_to(er[:, None, :], (rows, GRID_W, dim // 2)),
                          jnp.broadcast_to(ec[None, :, :], (rows, GRID_W, dim // 2))], axis=-1)
    return pe.reshape(rows * GRID_W, dim)


def kernel(x_prompt, x_sample, c, state_s5_re, state_s5_im, state_gla, c_ctx, norm_g, w_mod, b_mod, w_in, gla_wg_up,
           gla_bg, gla_norm_g, s5_lam_re, s5_lam_im, s5_log_dt, s5_b_re, s5_b_im, s5_c_re, s5_c_im, s5_d, w_glu,
           b_glu, w_pa, w_pb, w_o, final_norm_g):
    depth = w_in.shape[0]
    bp, lp, _ = x_prompt.shape
    bs, ls, _ = x_sample.shape

    cond8 = jnp.zeros((8, D_MODEL), F32).at[0].set(c_ctx).at[1:1 + bs].set(c)
    mod = _mod_call(cond8, w_mod, b_mod)

    a_re, a_im, w_bu, w_c = _s5_params(s5_lam_re, s5_lam_im, s5_log_dt, s5_b_re, s5_b_im, s5_c_re, s5_c_im)

    w_proj, w_gate = _split_w_in(w_in)
    wg_up = jnp.concatenate([gla_wg_up, jnp.zeros((depth, 2, LANES - GLA_RANK, GLA_KEY), F32)], axis=2).astype(BF16)
    w_glu_b, w_pa_b, w_pb_b, w_o_b = (w.astype(BF16) for w in (w_glu, w_pa, w_pb, w_o))

    vec = lambda a: a[:, None, :]
    mod = mod.reshape(depth, 8, 1, 3 * D_MODEL)
    ng, gng, bg = vec(norm_g), vec(gla_norm_g), gla_bg[:, :, None, :]
    common = (ng, w_gate, vec(s5_d), w_glu_b, vec(b_glu), w_pa_b, w_pb_b, w_o_b, final_norm_g[None], gng)
    are_p, aim_p = jnp.repeat(a_re, bp, axis=1), jnp.repeat(a_im, bp, axis=1)
    are_s, aim_s = jnp.repeat(a_re, bs, axis=1), jnp.repeat(a_im, bs, axis=1)
    ctx_row = lambda b: 0
    cond_row = lambda b: b + 1

    pe = _grid_pos_embed(ls, D_MODEL)
    zero_h0 = jnp.zeros((2 * bp, 2 * S5_NC), F32)
    xp, xs = x_prompt, x_sample
    new_re, new_im, new_gla = [], [], []
    for l in range(depth):
        last = l == depth - 1

        u, q, k, v, gl = _proj_call(xp, None, mod, l, ctx_row, ng, w_proj, tm=lp, nsb=4)
        yf, yb, hfin = _s5_call(u, l, w_bu, w_c, are_p, aim_p, zero_h0, tc=16, tiles_per_pass=2)
        o, sfin = _gla_call(q, k, v, gl, l, wg_up, bg, None, nsq=4, cpi=1)
        new_gla.append(sfin)
        xp = _mix_call(xp, None, yf, yb, u, o, mod, l, ctx_row, *common, tm=lp, last=last, nsb=2)
        re_l, im_l = _s5_cols_to_state(hfin.reshape(2, bp, 2 * S5_NC))
        new_re.append(jnp.swapaxes(re_l, 0, 1))
        new_im.append(jnp.swapaxes(im_l, 0, 1))

        pe_l = pe if l == 0 else None
        u, q, k, v, gl = _proj_call(xs, pe_l, mod, l, cond_row, ng, w_proj, tm=1024)
        h0 = _s5_state_to_cols(jnp.swapaxes(state_s5_re[:, l], 0, 1), jnp.swapaxes(state_s5_im[:, l], 0, 1))
        yf, yb, _ = _s5_call(u, l, w_bu, w_c, are_s, aim_s, h0.reshape(2 * bs, 2 * S5_NC), tc=32, tiles_per_pass=4)
        o, _ = _gla_call(q, k, v, gl, l, wg_up, bg, state_gla, nsq=2, cpi=2)
        xs = _mix_call(xs, pe_l, yf, yb, u, o, mod, l, cond_row, *common, tm=512, last=last)

    return (xp, xs, jnp.stack(new_re, axis=1), jnp.stack(new_im, axis=1), jnp.stack(new_gla, axis=1))
```

```python
import functools
import math

import numpy as np
import jax
import jax.numpy as jnp
from jax import lax
from jax.experimental import pallas as pl
from jax.experimental.pallas import tpu as pltpu

F32 = jnp.float32
BF16 = jnp.bfloat16

D_MODEL = 1024
GRID_W = 64
S5_WIDTH = 512
S5_GROUPS = 32
S5_CH = 16
S5_STATE = 64
S5_NC = S5_GROUPS * S5_STATE
S5_SLAB_GROUPS = 8
S5_SLABS = S5_GROUPS // S5_SLAB_GROUPS
S5_SLAB_NC = S5_SLAB_GROUPS * S5_STATE
S5_SLAB_COLS = 2 * S5_SLAB_NC
GLA_HEADS = 4
GLA_DK = 64
GLA_DV = 128
GLA_KEY = GLA_HEADS * GLA_DK
GLA_WIDTH = GLA_HEADS * GLA_DV
GLA_RANK = 16
GLA_GATE_NORM = 16.0
GLA_CHUNK = 128
EPS = 1e-6
LANES = 128
PROJ_W = S5_WIDTH + 2 * GLA_KEY + GLA_WIDTH + LANES
GATE_W = 2 * S5_WIDTH + 2 * D_MODEL
VMEM_LIMIT = 60 * 1024 * 1024
MXU_TILE = 256


def _bdot(a, b):
    return jnp.dot(a.astype(BF16), b.astype(BF16), preferred_element_type=F32)


def _bdot_t(a, bt):
    return lax.dot_general(a.astype(BF16), bt.astype(BF16), (((1,), (1,)), ((), ())), preferred_element_type=F32)


def _split2(x):
    hi = x.astype(BF16)
    lo = (x - hi.astype(F32)).astype(BF16)
    return hi, lo


def _sigmoid(x):
    return 1.0 / (1.0 + jnp.exp2(x * (-math.log2(math.e))))


def _silu(x):
    return x * _sigmoid(x)


def _gelu_tanh(x):
    c = math.sqrt(2.0 / math.pi)
    return 0.5 * x * (1.0 + jnp.tanh(c * (x + 0.044715 * (x * x * x))))


def _log_sigmoid(x):
    return jnp.minimum(x, 0.0) - jnp.log(1.0 + jnp.exp(-jnp.abs(x)))


def _modulated_norm(x, ng, scale1p, shift):
    ms = jnp.mean(x * x, axis=-1, keepdims=True)
    return (x * lax.rsqrt(ms + EPS) * ng) * scale1p + shift


def _mod_kernel(cond_ref, w_ref, b_ref, out_ref):
    c = cond_ref[...]
    s_hi, s_lo = _split2(_silu(c))
    w_hi, w_lo = _split2(w_ref[0])
    acc = jnp.dot(s_hi, w_hi, preferred_element_type=F32)
    acc += jnp.dot(s_lo, w_hi, preferred_element_type=F32)
    acc += jnp.dot(s_hi, w_lo, preferred_element_type=F32)
    out_ref[0] = acc + b_ref[0]


def _mod_call(cond8, w_mod, b_mod):
    depth = w_mod.shape[0]
    nb = 1536
    return pl.pallas_call(
        _mod_kernel,
        grid=(depth, 3 * D_MODEL // nb),
        in_specs=[
            pl.BlockSpec((8, D_MODEL), lambda l, j: (0, 0)),
            pl.BlockSpec((1, D_MODEL, nb), lambda l, j: (l, 0, j)),
            pl.BlockSpec((1, 1, nb), lambda l, j: (l, 0, j)),
        ],
        out_specs=pl.BlockSpec((1, 8, nb), lambda l, j: (l, 0, j)),
        out_shape=jax.ShapeDtypeStruct((depth, 8, 3 * D_MODEL), F32),
        compiler_params=pltpu.CompilerParams(dimension_semantics=("arbitrary", "arbitrary")),
        name="adaln_mod",
    )(cond8, w_mod, b_mod.reshape(depth, 1, 3 * D_MODEL))


def _zoh(lre, lim, ldt):
    dt = jnp.exp(ldt)
    mag = jnp.exp(lre * dt)
    a_re = mag * jnp.cos(lim * dt)
    a_im = mag * jnp.sin(lim * dt)
    n_re = a_re - 1.0
    inv = 1.0 / (lre * lre + lim * lim)
    return a_re, a_im, (n_re * lre + a_im * lim) * inv, (a_im * lre - n_re * lim) * inv


def _zoh_kernel(lre_ref, lim_ref, ldt_ref, bre_ref, bim_ref, cre_ref, cim_ref, are_ref, aim_ref, wbu_ref, wc_ref,
                *, depth):
    a_re, a_im, c_re, c_im = _zoh(lre_ref[...], lim_ref[...], ldt_ref[...])
    are_ref[...] = a_re
    aim_ref[...] = a_im
    cmat = [cre_ref[...], -cim_ref[...]]
    wbu_ref[...] = jnp.zeros_like(wbu_ref)
    wc_ref[...] = jnp.zeros_like(wc_ref)
    for l in range(depth):
        for s in range(S5_SLABS):
            for gl in range(S5_SLAB_GROUPS):
                g = S5_SLAB_GROUPS * s + gl
                src = (l * S5_GROUPS + g) * S5_CH
                bre = bre_ref[src:src + S5_CH, :]
                bim = bim_ref[src:src + S5_CH, :]
                for d in range(2):
                    r = (l * 2 + d) * S5_GROUPS + g
                    k_re, k_im = c_re[r:r + 1, :], c_im[r:r + 1, :]
                    bbar = [k_re * bre - k_im * bim, k_re * bim + k_im * bre]
                    row = (d * S5_SLAB_GROUPS + gl) * S5_CH
                    for ri in range(2):
                        col = ri * S5_SLAB_NC + gl * S5_STATE
                        wbu_ref[l, s, row:row + S5_CH, col:col + S5_STATE] = bbar[ri].astype(BF16)
                src = (l * S5_GROUPS + g) * S5_STATE
                for ri in range(2):
                    col = ri * S5_SLAB_NC + gl * S5_STATE
                    wc_ref[l, s, col:col + S5_STATE, gl * S5_CH:(gl + 1) * S5_CH] = (
                        cmat[ri][src:src + S5_STATE, :].astype(BF16))


def _s5_params(s5_lam_re, s5_lam_im, s5_log_dt, s5_b_re, s5_b_im, s5_c_re, s5_c_im):
    depth = s5_lam_re.shape[0]
    small = lambda a: a.reshape(depth * 2 * S5_GROUPS, -1)
    swapped = lambda a: jnp.swapaxes(a, -1, -2).reshape(-1, a.shape[-2])
    a_re, a_im, w_bu, w_c = pl.pallas_call(
        functools.partial(_zoh_kernel, depth=depth),
        out_shape=(jax.ShapeDtypeStruct((depth * 2 * S5_GROUPS, S5_STATE), F32),
                   jax.ShapeDtypeStruct((depth * 2 * S5_GROUPS, S5_STATE), F32),
                   jax.ShapeDtypeStruct((depth, S5_SLABS, 2 * LANES, S5_SLAB_COLS), BF16),
                   jax.ShapeDtypeStruct((depth, S5_SLABS, S5_SLAB_COLS, LANES), BF16)),
        name="s5_zoh",
    )(small(s5_lam_re), small(s5_lam_im), small(s5_log_dt), swapped(s5_b_re), swapped(s5_b_im),
      swapped(s5_c_re), swapped(s5_c_im))
    return a_re.reshape(depth, 2, S5_NC), a_im.reshape(depth, 2, S5_NC), w_bu, w_c


def _s5_state_to_cols(re, im):
    lead = re.shape[:-2]
    st = jnp.stack([re.reshape(lead + (S5_SLABS, S5_SLAB_NC)), im.reshape(lead + (S5_SLABS, S5_SLAB_NC))], axis=-2)
    return st.reshape(lead + (2 * S5_NC,))


def _s5_cols_to_state(cols):
    lead = cols.shape[:-1]
    st = cols.reshape(lead + (S5_SLABS, 2, S5_SLAB_NC))
    re = st[..., 0, :].reshape(lead + (S5_GROUPS, S5_STATE))
    im = st[..., 1, :].reshape(lead + (S5_GROUPS, S5_STATE))
    return re, im


_IN_OFFS = [int(o) for o in np.cumsum([0, S5_WIDTH, S5_WIDTH, GLA_KEY, GLA_KEY, GLA_WIDTH, GLA_WIDTH, GLA_RANK,
                                       D_MODEL, D_MODEL])]


def _split_w_in_kernel(u_ref, qkv_ref, code_ref, ga_ref, gb_ref, m_ref, proj_ref, gate_ref):
    rows = 0
    for src in (u_ref, qkv_ref, code_ref):
        proj_ref[rows:rows + src.shape[1], :] = src[0].astype(BF16)
        rows += src.shape[1]
    proj_ref[rows:, :] = jnp.zeros((PROJ_W - rows, proj_ref.shape[1]), BF16)
    rows = 0
    for src in (ga_ref, gb_ref, m_ref):
        gate_ref[rows:rows + src.shape[1], :] = src[0].astype(BF16)
        rows += src.shape[1]


def _split_w_in(w_in):
    o = _IN_OFFS
    depth = w_in.shape[0]
    wt = jnp.swapaxes(w_in, 1, 2)
    tn = D_MODEL // 2
    rows = lambda lo, hi: pl.BlockSpec((pl.Element(1), pl.Element(hi - lo), pl.Element(tn)),
                                       lambda l, j: (l, lo, j * tn))
    out = lambda n: pl.BlockSpec((None, n, tn), lambda l, j: (l, 0, j))
    return pl.pallas_call(
        _split_w_in_kernel,
        grid=(depth, D_MODEL // tn),
        in_specs=[rows(o[0], o[1]), rows(o[2], o[5]), rows(o[6], o[7]), rows(o[1], o[2]), rows(o[5], o[6]),
                  rows(o[7], o[9])],
        out_specs=(out(PROJ_W), out(GATE_W)),
        out_shape=(jax.ShapeDtypeStruct((depth, PROJ_W, D_MODEL), BF16),
                   jax.ShapeDtypeStruct((depth, GATE_W, D_MODEL), BF16)),
        compiler_params=pltpu.CompilerParams(dimension_semantics=("arbitrary", "arbitrary"),
                                             vmem_limit_bytes=VMEM_LIMIT),
        name="split_w_in",
    )(wt, wt, wt, wt, wt, wt)


def _proj_kernel(*refs, has_pe):
    if has_pe:
        x_ref, pe_ref, shift_ref, scale_ref, ng_ref, w_ref, u_ref, q_ref, k_ref, v_ref, gl_ref = refs
    else:
        x_ref, shift_ref, scale_ref, ng_ref, w_ref, u_ref, q_ref, k_ref, v_ref, gl_ref = refs
    nsb, tm, _ = x_ref.shape
    rg = max(1, tm // (2 * MXU_TILE))
    grp = [(i, slice(j * tm // rg, (j + 1) * tm // rg)) for i in range(nsb) for j in range(rg)]
    x = [x_ref[i, r, :] + pe_ref[r, :] if has_pe else x_ref[i, r, :] for i, r in grp]
    h = [_modulated_norm(v, ng_ref[...], 1.0 + scale_ref[...], shift_ref[...]) for v in x]
    p = [_bdot_t(v, w_ref[...]) for v in h]
    for (i, r), v in zip(grp, p):
        o = 0
        for ref, width in ((u_ref, S5_WIDTH), (q_ref, GLA_KEY), (k_ref, GLA_KEY), (v_ref, GLA_WIDTH), (gl_ref, LANES)):
            ref[i, r, :] = v[:, o:o + width].astype(ref.dtype)
            o += width


def _mod_spec(l, cond_row, part):
    return pl.BlockSpec((None, None, 1, D_MODEL), lambda b, j: (l, cond_row(b), 0, part))


def _layer_spec(l, shape):
    return pl.BlockSpec((None,) + shape, lambda b, j: (l,) + (0,) * len(shape))


def _proj_call(x, pe, mod, l, cond_row, ng, w, tm, nsb=1):
    nseq, length, _ = x.shape
    row = lambda width: pl.BlockSpec((nsb, tm, width), lambda b, j: (b, j, 0))
    in_specs = [row(D_MODEL)]
    args = [x]
    if pe is not None:
        in_specs.append(pl.BlockSpec((tm, D_MODEL), lambda b, j: (j, 0)))
        args.append(pe)
    in_specs += [_mod_spec(l, cond_row, 0), _mod_spec(l, cond_row, 1), _layer_spec(l, (1, D_MODEL)),
                 _layer_spec(l, (PROJ_W, D_MODEL))]
    args += [mod, mod, ng, w]
    sds = lambda width, dt: jax.ShapeDtypeStruct((nseq, length, width), dt)
    return pl.pallas_call(
        functools.partial(_proj_kernel, has_pe=pe is not None),
        grid=(nseq // nsb, length // tm),
        in_specs=in_specs,
        out_specs=(row(S5_WIDTH), row(GLA_KEY), row(GLA_KEY), row(GLA_WIDTH), row(LANES)),
        out_shape=(sds(S5_WIDTH, F32), sds(GLA_KEY, F32), sds(GLA_KEY, F32), sds(GLA_WIDTH, BF16), sds(LANES, F32)),
        compiler_params=pltpu.CompilerParams(dimension_semantics=("arbitrary", "arbitrary"),
                                             vmem_limit_bytes=VMEM_LIMIT),
        name="proj",
    )(*args)


def _s5_perms(nb, tc):
    r = nb * tc
    pf = np.zeros((2 * r, r), np.float32)
    pb = np.zeros((2 * r, r), np.float32)
    for t in range(tc):
        for b in range(nb):
            pf[t * 2 * nb + b, b * tc + t] = 1.0
            pb[t * 2 * nb + nb + b, b * tc + (tc - 1 - t)] = 1.0
    return pf, pb


def _s5_store_unpacked(y2, yf_ref, yb_ref, t_f, t_b, nb, tc):
    rs = 2 * nb
    for t in range(tc):
        yf_ref[:, t_f + t, :] = y2[t * rs:t * rs + nb]
        yb_ref[:, t_b + tc - 1 - t, :] = y2[t * rs + nb:(t + 1) * rs]


def _s5_kernel(*refs, nb, tc, tiles_per_pass):
    use_perm = nb % 8 != 0
    uf1_ref, ub1_ref, uf2_ref, ub2_ref, uf0_ref, ub0_ref = refs[:6]
    pf_ref, pb_ref = refs[6:8] if use_perm else (None, None)
    wbu_ref, wc_ref, are_ref, aim_ref, h0_ref, yf_ref, yb_ref, hfin_ref, bua_ref, bub_ref, hb_ref, st_ref = refs[-12:]
    i = pl.program_id(0)
    rows = nb * tc
    rs = 2 * nb

    def packed_inputs(uf_ref, ub_ref):
        if not use_perm:
            return tuple(jnp.concatenate([r[:, t, :] for t in range(tc)], axis=0).astype(BF16) for r in (uf_ref, ub_ref))
        uf = uf_ref[...].reshape(rows, S5_WIDTH).astype(BF16)
        ub = ub_ref[...].reshape(rows, S5_WIDTH).astype(BF16)
        u2f = jnp.dot(pf_ref[...], uf, preferred_element_type=F32).astype(BF16)
        u2b = jnp.dot(pb_ref[...], ub, preferred_element_type=F32).astype(BF16)
        return [jnp.concatenate([u2f[:, s * LANES:(s + 1) * LANES], u2b[:, s * LANES:(s + 1) * LANES]], axis=1)
                for s in range(S5_SLABS)]

    def bu_slab(lhs, s, dst_ref):
        cols = slice(s * S5_SLAB_COLS, (s + 1) * S5_SLAB_COLS)
        if use_perm:
            dst_ref[:, cols] = jnp.dot(lhs[s], wbu_ref[s], preferred_element_type=F32)
            return
        uf, ub = lhs
        bf = jnp.dot(uf[:, s * LANES:(s + 1) * LANES], wbu_ref[s, :LANES, :], preferred_element_type=F32)
        bb = jnp.dot(ub[:, s * LANES:(s + 1) * LANES], wbu_ref[s, LANES:, :], preferred_element_type=F32)
        for t in range(tc):
            dst_ref[t * rs:t * rs + nb, cols] = bf[t * nb:(t + 1) * nb]
            dst_ref[t * rs + nb:(t + 1) * rs, cols] = bb[(tc - 1 - t) * nb:(tc - t) * nb]

    @pl.when(i == 0)
    def _():
        st_ref[...] = h0_ref[...]
        lhs0 = packed_inputs(uf0_ref, ub0_ref)
        for s in range(S5_SLABS):
            bu_slab(lhs0, s, bua_ref)

    half_step = functools.partial(_s5_half_step, wc_ref=wc_ref, are_ref=are_ref, aim_ref=aim_ref, hb_ref=hb_ref,
                                  st_ref=st_ref, bu_slab=bu_slab, rs=rs, tc=tc, tiles_per_pass=tiles_per_pass)
    y2 = half_step(bua_ref, bub_ref, packed_inputs(uf1_ref, ub1_ref))
    _s5_store_unpacked(y2, yf_ref, yb_ref, 0, tc, nb, tc)
    y2 = half_step(bub_ref, bua_ref, packed_inputs(uf2_ref, ub2_ref))
    _s5_store_unpacked(y2, yf_ref, yb_ref, tc, 0, nb, tc)
    hfin_ref[...] = st_ref[...]


def _s5_half_step(cur_ref, nxt_ref, lhs_next, *, wc_ref, are_ref, aim_ref, hb_ref, st_ref, bu_slab,
                  rs, tc, tiles_per_pass):
    slab_tiles = S5_SLAB_NC // LANES
    grp = max(1, 16 // rs)
    ys = []
    for s in range(S5_SLABS):
        bu_slab(lhs_next, s, nxt_ref)
        for c0 in range(slab_tiles * s, slab_tiles * (s + 1), tiles_per_pass):
            cts = list(range(c0, c0 + tiles_per_pass))
            cre = [(ct // slab_tiles) * S5_SLAB_COLS + (ct % slab_tiles) * LANES for ct in cts]
            cim = [c + S5_SLAB_NC for c in cre]
            a_re = [are_ref[:, ct * LANES:(ct + 1) * LANES] for ct in cts]
            a_im = [aim_ref[:, ct * LANES:(ct + 1) * LANES] for ct in cts]
            h_re = [st_ref[:, c:c + LANES] for c in cre]
            h_im = [st_ref[:, c:c + LANES] for c in cim]
            for t0 in range(0, tc, grp):
                out_re = [[] for _ in cts]
                out_im = [[] for _ in cts]
                for t in range(t0, t0 + grp):
                    r0 = t * rs
                    for j in range(len(cts)):
                        b_re = cur_ref[r0:r0 + rs, cre[j]:cre[j] + LANES]
                        b_im = cur_ref[r0:r0 + rs, cim[j]:cim[j] + LANES]
                        n_re = a_re[j] * h_re[j] - a_im[j] * h_im[j] + b_re
                        n_im = a_re[j] * h_im[j] + a_im[j] * h_re[j] + b_im
                        h_re[j], h_im[j] = n_re, n_im
                        out_re[j].append(n_re)
                        out_im[j].append(n_im)
                for j in range(len(cts)):
                    blk_re = out_re[j][0] if grp == 1 else jnp.concatenate(out_re[j], axis=0)
                    blk_im = out_im[j][0] if grp == 1 else jnp.concatenate(out_im[j], axis=0)
                    hb_ref[t0 * rs:(t0 + grp) * rs, cre[j]:cre[j] + LANES] = blk_re.astype(BF16)
                    hb_ref[t0 * rs:(t0 + grp) * rs, cim[j]:cim[j] + LANES] = blk_im.astype(BF16)
            for j in range(len(cts)):
                st_ref[:, cre[j]:cre[j] + LANES] = h_re[j]
                st_ref[:, cim[j]:cim[j] + LANES] = h_im[j]
        ys.append(jnp.dot(hb_ref[:, s * S5_SLAB_COLS:(s + 1) * S5_SLAB_COLS], wc_ref[s], preferred_element_type=F32))
    return jnp.concatenate(ys, axis=1)


def _s5_call(u, l, w_bu, w_c, are, aim, h0, tc, tiles_per_pass):
    nb, length, _ = u.shape
    n = length // tc
    rows = nb * tc
    const = lambda shape: pl.BlockSpec(shape, lambda i: (0,) * len(shape))
    layer = lambda shape: pl.BlockSpec((None,) + shape, lambda i: (l,) + (0,) * len(shape))
    blk = lambda chunk: pl.BlockSpec((nb, tc, S5_WIDTH), lambda i: (0, chunk(i), 0))
    out_f = pl.BlockSpec((nb, 2 * tc, S5_WIDTH), lambda i: (0, i, 0))
    out_b = pl.BlockSpec((nb, 2 * tc, S5_WIDTH), lambda i: (0, n // 2 - 1 - i, 0))
    second = lambda i: jnp.minimum(2 * i + 2, n - 1)
    in_specs = [blk(lambda i: 2 * i + 1), blk(lambda i: n - 2 - 2 * i), blk(second), blk(lambda i: n - 1 - second(i)),
                blk(lambda i: 0), blk(lambda i: n - 1)]
    args = [u] * 6
    if nb % 8 != 0:
        pf, pb = _s5_perms(nb, tc)
        in_specs += [const((2 * rows, rows)), const((2 * rows, rows))]
        args += [jnp.asarray(pf, BF16), jnp.asarray(pb, BF16)]
    in_specs += [layer((S5_SLABS, 2 * LANES, S5_SLAB_COLS)), layer((S5_SLABS, S5_SLAB_COLS, LANES)),
                 layer((2 * nb, S5_NC)), layer((2 * nb, S5_NC)), const((2 * nb, 2 * S5_NC))]
    args += [w_bu, w_c, are, aim, h0]
    return pl.pallas_call(
        functools.partial(_s5_kernel, nb=nb, tc=tc, tiles_per_pass=tiles_per_pass),
        grid=(n // 2,),
        in_specs=in_specs,
        out_specs=(out_f, out_b, const((2 * nb, 2 * S5_NC))),
        out_shape=(jax.ShapeDtypeStruct(u.shape, F32), jax.ShapeDtypeStruct(u.shape, F32),
                   jax.ShapeDtypeStruct((2 * nb, 2 * S5_NC), F32)),
        scratch_shapes=[pltpu.VMEM((2 * rows, 2 * S5_NC), F32), pltpu.VMEM((2 * rows, 2 * S5_NC), F32),
                        pltpu.VMEM((2 * rows, 2 * S5_NC), BF16), pltpu.VMEM((2 * nb, 2 * S5_NC), F32)],
        compiler_params=pltpu.CompilerParams(dimension_semantics=("arbitrary",), vmem_limit_bytes=VMEM_LIMIT),
        name="s5_scan",
    )(*args)


def _gla_kernel(*refs, length, has_s0, nsq, cpi):
    q_ref, k_ref, v_ref, gl_ref, wg_ref, bg_ref = refs[:6]
    s0_ref = refs[6] if has_s0 else None
    o_ref, sfin_ref, sf_ref, sb_ref = refs[-4:]
    c = GLA_CHUNK
    n = length // c
    ri = lax.broadcasted_iota(jnp.int32, (c, c), 0)
    ci = lax.broadcasted_iota(jnp.int32, (c, c), 1)
    lane = lax.broadcasted_iota(jnp.int32, (1, GLA_KEY), 1)
    head_mask = [(lane >= h * GLA_DK) & (lane < (h + 1) * GLA_DK) for h in range(GLA_HEADS)]
    causal = [ci <= ri, ci >= ri]
    tri = [m.astype(BF16) for m in causal]
    end = (c - 1, 0)
    chains = [(sq, d, j) for j in range(cpi) for sq in range(nsq) for d in (0, 1)]
    key_rows = lambda h: slice(h * GLA_DK, (h + 1) * GLA_DK)
    val_cols = lambda h: slice(h * GLA_DV, (h + 1) * GLA_DV)

    sb_ref[...] = jnp.zeros_like(sb_ref)
    for sq in range(nsq):
        for d in (0, 1):
            for h in range(GLA_HEADS):
                if has_s0:
                    sf_ref[sq, d, h] = s0_ref[sq, d, h]
                    sb_ref[sq, d, key_rows(h), val_cols(h)] = s0_ref[sq, d, h].astype(BF16)
                else:
                    sf_ref[sq, d, h] = jnp.zeros((GLA_DK, GLA_DV), F32)

    def chunk_group(r):
        row = lambda d, j: r[d][j]
        qc = [q_ref[sq, pl.ds(row(d, j), c), :] * (GLA_DK ** -0.5) for sq, d, j in chains]
        kc = [k_ref[sq, pl.ds(row(d, j), c), :] for sq, d, j in chains]
        vc = [v_ref[sq, pl.ds(row(d, j), c), :] for sq, d, j in chains]
        logits = [_bdot(gl_ref[sq, pl.ds(row(d, j), c), :], wg_ref[d]) + bg_ref[d] for sq, d, j in chains]
        g = [_split2(_log_sigmoid(x) * (1.0 / GLA_GATE_NORM)) for x in logits]
        b = [jnp.dot(tri[d], hi, preferred_element_type=F32) + jnp.dot(tri[d], lo, preferred_element_type=F32)
             for (sq, d, j), (hi, lo) in zip(chains, g)]
        bt = [x.T for x in b]
        kt = [x.T for x in kc]
        b_mid = [x[c // 2:c // 2 + 1, :] for x in b]
        bt_mid = [x[:, c // 2:c // 2 + 1] for x in bt]
        bt_end = [x[:, end[d]:end[d] + 1] for (sq, d, j), x in zip(chains, bt)]
        q_in = [(x * jnp.exp(y)).astype(BF16) for x, y in zip(qc, b)]
        q_mid = [(x * jnp.exp(y - m)).astype(BF16) for x, y, m in zip(qc, b, b_mid)]
        k_mid_t = [(x * jnp.exp(m - y)).astype(BF16) for x, y, m in zip(kt, bt, bt_mid)]
        k_end_t = [(x * jnp.exp(e - y)).astype(BF16) for x, y, e in zip(kt, bt, bt_end)]
        q_st = [jnp.concatenate([jnp.where(head_mask[h], x, jnp.zeros_like(x)) for h in range(GLA_HEADS)], axis=0)
                for x in q_mid]
        a = [jnp.dot(x, y, preferred_element_type=F32) for x, y in zip(q_st, k_mid_t)]
        parts = [[] for _ in chains]
        for h in range(GLA_HEADS):
            for i, (sq, d, j) in enumerate(chains):
                a_h = jnp.where(causal[d], a[i][h * c:(h + 1) * c, :], 0.0).astype(BF16)
                parts[i].append(jnp.dot(a_h, vc[i][:, val_cols(h)], preferred_element_type=F32))
        upd = [[jnp.dot(k_end_t[i][key_rows(h), :], vc[i][:, val_cols(h)], preferred_element_type=F32)
                for h in range(GLA_HEADS)] for i in range(len(chains))]
        o_inter = [None] * len(chains)
        for i, (sq, d, j) in enumerate(chains):
            o_inter[i] = jnp.dot(q_in[i], sb_ref[sq, d], preferred_element_type=F32)
            for h in range(GLA_HEADS):
                s_new = jnp.exp(bt_end[i][key_rows(h), :]) * sf_ref[sq, d, h] + upd[i][h]
                sf_ref[sq, d, h] = s_new
                sb_ref[sq, d, key_rows(h), val_cols(h)] = s_new.astype(BF16)
        return [x + jnp.concatenate(p, axis=1) for x, p in zip(o_inter, parts)]

    def rows_of(step):
        fwd = [pl.multiple_of((step * cpi + j) * c, c) for j in range(cpi)]
        bwd = [pl.multiple_of((n - 1 - step * cpi - j) * c, c) for j in range(cpi)]
        return fwd, bwd

    def first_half(step, carry):
        r = rows_of(step)
        for (sq, d, j), o in zip(chains, chunk_group(r)):
            o_ref[sq, pl.ds(r[d][j], c), :] = o
        return carry

    def second_half(step, carry):
        r = rows_of(step)
        for (sq, d, j), o in zip(chains, chunk_group(r)):
            o_ref[sq, pl.ds(r[d][j], c), :] = o + o_ref[sq, pl.ds(r[d][j], c), :]
        return carry

    steps = n // cpi
    lax.fori_loop(0, steps // 2, first_half, 0)
    lax.fori_loop(steps // 2, steps, second_half, 0)
    for sq in range(nsq):
        for d in (0, 1):
            for h in range(GLA_HEADS):
                sfin_ref[sq, d, h] = sf_ref[sq, d, h]


def _gla_call(q, k, v, gl, l, wg, bg, s0, nsq, cpi):
    nseq, length, _ = q.shape
    assert (length // GLA_CHUNK) % (2 * cpi) == 0 and nseq % nsq == 0
    seq = lambda width: pl.BlockSpec((nsq, length, width), lambda b: (b, 0, 0))
    layer = lambda shape: pl.BlockSpec((None,) + shape, lambda b: (l,) + (0,) * len(shape))
    in_specs = [seq(GLA_KEY), seq(GLA_KEY), seq(GLA_WIDTH), seq(LANES),
                layer((2, LANES, GLA_KEY)), layer((2, 1, GLA_KEY))]
    args = [q, k, v, gl, wg, bg]
    if s0 is not None:
        in_specs.append(pl.BlockSpec((nsq, None, 2, GLA_HEADS, GLA_DK, GLA_DV), lambda b: (b, l, 0, 0, 0, 0)))
        args.append(s0)
    return pl.pallas_call(
        functools.partial(_gla_kernel, length=length, has_s0=s0 is not None, nsq=nsq, cpi=cpi),
        grid=(nseq // nsq,),
        in_specs=in_specs,
        out_specs=(seq(GLA_WIDTH), pl.BlockSpec((nsq, 2, GLA_HEADS, GLA_DK, GLA_DV), lambda b: (b, 0, 0, 0, 0))),
        out_shape=(jax.ShapeDtypeStruct((nseq, length, GLA_WIDTH), F32),
                   jax.ShapeDtypeStruct((nseq, 2, GLA_HEADS, GLA_DK, GLA_DV), F32)),
        scratch_shapes=[pltpu.VMEM((nsq, 2, GLA_HEADS, GLA_DK, GLA_DV), F32),
                        pltpu.VMEM((nsq, 2, GLA_KEY, GLA_WIDTH), BF16)],
        compiler_params=pltpu.CompilerParams(dimension_semantics=("arbitrary",), vmem_limit_bytes=VMEM_LIMIT),
        name="gla",
    )(*args)


def _mix_kernel(*refs, has_pe, last, row_groups):
    if has_pe:
        (x_ref, pe_ref, yf_ref, yb_ref, u_ref, o_ref, shift_ref, scale_ref, gate_ref, ng_ref, wg_ref, d_ref,
         wglu_ref, bglu_ref, wpa_ref, wpb_ref, wo_ref, fng_ref, gng_ref, out_ref) = refs
    else:
        (x_ref, yf_ref, yb_ref, u_ref, o_ref, shift_ref, scale_ref, gate_ref, ng_ref, wg_ref, d_ref,
         wglu_ref, bglu_ref, wpa_ref, wpb_ref, wo_ref, fng_ref, gng_ref, out_ref) = refs

    def head_norm(o):
        heads = [o[:, h * GLA_DV:(h + 1) * GLA_DV] for h in range(GLA_HEADS)]
        heads = [v * lax.rsqrt(jnp.mean(v * v, axis=-1, keepdims=True) + EPS) for v in heads]
        return jnp.concatenate(heads, axis=1) * gng_ref[...]

    nsb, tm, _ = x_ref.shape
    grp = [(i, slice(j * tm // row_groups, (j + 1) * tm // row_groups)) for i in range(nsb) for j in range(row_groups)]
    x = [x_ref[i, r, :] + pe_ref[r, :] if has_pe else x_ref[i, r, :] for i, r in grp]
    h = [_modulated_norm(v, ng_ref[...], 1.0 + scale_ref[...], shift_ref[...]).astype(BF16) for v in x]
    gts = [_bdot_t(v, wg_ref[...]) for v in h]
    y = [_gelu_tanh(yf_ref[i, r, :] + yb_ref[i, r, :] + d_ref[...] * u_ref[i, r, :]) for i, r in grp]
    glu = [_bdot(v, wglu_ref[...]) for v in y]
    y = [v * _sigmoid(g + bglu_ref[...]) for v, g in zip(y, glu)]
    y_a = [v * _silu(g[:, :S5_WIDTH]) for v, g in zip(y, gts)]
    y_b = [head_norm(o_ref[i, r, :]) * _silu(g[:, S5_WIDTH:2 * S5_WIDTH]) for (i, r), g in zip(grp, gts)]
    p_a = [_bdot(v, wpa_ref[...]) for v in y_a]
    p_b = [_bdot(v, wpb_ref[...]) for v in y_b]
    merged = [_sigmoid(g[:, 2 * S5_WIDTH:2 * S5_WIDTH + D_MODEL]) * a + _sigmoid(g[:, 2 * S5_WIDTH + D_MODEL:]) * b
              for g, a, b in zip(gts, p_a, p_b)]
    xn = [v + gate_ref[...] * _bdot(m, wo_ref[...]) for v, m in zip(x, merged)]
    for (i, r), v in zip(grp, xn):
        if last:
            ms = jnp.mean(v * v, axis=-1, keepdims=True)
            v = v * lax.rsqrt(ms + EPS) * fng_ref[...]
        out_ref[i, r, :] = v


def _mix_call(x, pe, yf, yb, u, o, mod, l, cond_row, ng, wg, s5_d, wglu, bglu, wpa, wpb, wo, fng, gng, tm, last,
              nsb=1):
    nseq, length, _ = x.shape
    row = lambda width: pl.BlockSpec((nsb, tm, width), lambda b, j: (b, j, 0))
    in_specs = [row(D_MODEL)]
    args = [x]
    if pe is not None:
        in_specs.append(pl.BlockSpec((tm, D_MODEL), lambda b, j: (j, 0)))
        args.append(pe)
    in_specs += [row(S5_WIDTH), row(S5_WIDTH), row(S5_WIDTH), row(GLA_WIDTH),
                 _mod_spec(l, cond_row, 0), _mod_spec(l, cond_row, 1), _mod_spec(l, cond_row, 2),
                 _layer_spec(l, (1, D_MODEL)), _layer_spec(l, (GATE_W, D_MODEL)), _layer_spec(l, (1, S5_WIDTH)),
                 _layer_spec(l, (S5_WIDTH, S5_WIDTH)), _layer_spec(l, (1, S5_WIDTH)),
                 _layer_spec(l, (S5_WIDTH, D_MODEL)), _layer_spec(l, (GLA_WIDTH, D_MODEL)),
                 _layer_spec(l, (D_MODEL, D_MODEL)), pl.BlockSpec((1, D_MODEL), lambda b, j: (0, 0)),
                 _layer_spec(l, (1, GLA_WIDTH))]
    args += [yf, yb, u, o, mod, mod, mod, ng, wg, s5_d, wglu, bglu, wpa, wpb, wo, fng, gng]
    return pl.pallas_call(
        functools.partial(_mix_kernel, has_pe=pe is not None, last=last, row_groups=max(1, tm // MXU_TILE)),
        grid=(nseq // nsb, length // tm),
        in_specs=in_specs,
        out_specs=row(D_MODEL),
        out_shape=jax.ShapeDtypeStruct(x.shape, F32),
        compiler_params=pltpu.CompilerParams(dimension_semantics=("arbitrary", "arbitrary"),
                                             vmem_limit_bytes=VMEM_LIMIT),
        name="mix",
    )(*args)


def _grid_pos_embed(length, dim):
    rows = length // GRID_W
    quarter = dim // 4
    freqs = jnp.exp(-math.log(10000.0) * jnp.arange(quarter, dtype=F32) / quarter)

    def sincos(pos):
        ang = pos.astype(F32)[:, None] * freqs[None, :]
        return jnp.concatenate([jnp.sin(ang), jnp.cos(ang)], axis=-1)

    er = sincos(jnp.arange(rows))
    ec = sincos(jnp.arange(GRID_W))
    pe = jnp.concatenate([jnp.broadcast_to(er[:, None, :], (rows, GRID_W, dim // 2)),
                          jnp.broadcast_to(ec[None, :, :], (rows, GRID_W, dim // 2))], axis=-1)
    return pe.reshape(rows * GRID_W, dim)


def kernel(x_prompt, x_sample, c, state_s5_re, state_s5_im, state_gla, c_ctx, norm_g, w_mod, b_mod, w_in, gla_wg_up,
           gla_bg, gla_norm_g, s5_lam_re, s5_lam_im, s5_log_dt, s5_b_re, s5_b_im, s5_c_re, s5_c_im, s5_d, w_glu,
           b_glu, w_pa, w_pb, w_o, final_norm_g):
    depth = w_in.shape[0]
    bp, lp, _ = x_prompt.shape
    bs, ls, _ = x_sample.shape

    cond8 = jnp.concatenate([c_ctx[None], c, jnp.zeros((8 - 1 - bs, D_MODEL), F32)], axis=0)
    mod = _mod_call(cond8, w_mod, b_mod)

    a_re, a_im, w_bu, w_c = _s5_params(s5_lam_re, s5_lam_im, s5_log_dt, s5_b_re, s5_b_im, s5_c_re, s5_c_im)

    w_proj, w_gate = _split_w_in(w_in)
    wg_up = jnp.concatenate([gla_wg_up, jnp.zeros((depth, 2, LANES - GLA_RANK, GLA_KEY), F32)], axis=2).astype(BF16)
    w_glu_b, w_pa_b, w_pb_b, w_o_b = (w.astype(BF16) for w in (w_glu, w_pa, w_pb, w_o))

    vec = lambda a: a[:, None, :]
    mod = mod.reshape(depth, 8, 1, 3 * D_MODEL)
    ng, gng, bg = vec(norm_g), vec(gla_norm_g), gla_bg[:, :, None, :]
    common = (ng, w_gate, vec(s5_d), w_glu_b, vec(b_glu), w_pa_b, w_pb_b, w_o_b, final_norm_g[None], gng)
    are_p, aim_p = jnp.repeat(a_re, bp, axis=1), jnp.repeat(a_im, bp, axis=1)
    are_s, aim_s = jnp.repeat(a_re, bs, axis=1), jnp.repeat(a_im, bs, axis=1)
    ctx_row = lambda b: 0
    cond_row = lambda b: b + 1

    pe = _grid_pos_embed(ls, D_MODEL)
    zero_h0 = jnp.zeros((2 * bp, 2 * S5_NC), F32)
    xp, xs = x_prompt, x_sample
    new_re, new_im, new_gla = [], [], []
    for l in range(depth):
        last = l == depth - 1

        u, q, k, v, gl = _proj_call(xp, None, mod, l, ctx_row, ng, w_proj, tm=lp, nsb=4)
        yf, yb, hfin = _s5_call(u, l, w_bu, w_c, are_p, aim_p, zero_h0, tc=16, tiles_per_pass=2)
        o, sfin = _gla_call(q, k, v, gl, l, wg_up, bg, None, nsq=4, cpi=1)
        new_gla.append(sfin)
        xp = _mix_call(xp, None, yf, yb, u, o, mod, l, ctx_row, *common, tm=lp, last=last, nsb=2)
        re_l, im_l = _s5_cols_to_state(hfin.reshape(2, bp, 2 * S5_NC))
        new_re.append(jnp.swapaxes(re_l, 0, 1))
        new_im.append(jnp.swapaxes(im_l, 0, 1))

        pe_l = pe if l == 0 else None
        u, q, k, v, gl = _proj_call(xs, pe_l, mod, l, cond_row, ng, w_proj, tm=1024)
        h0 = _s5_state_to_cols(jnp.swapaxes(state_s5_re[:, l], 0, 1), jnp.swapaxes(state_s5_im[:, l], 0, 1))
        yf, yb, _ = _s5_call(u, l, w_bu, w_c, are_s, aim_s, h0.reshape(2 * bs, 2 * S5_NC), tc=32, tiles_per_pass=4)
        o, _ = _gla_call(q, k, v, gl, l, wg_up, bg, state_gla, nsq=2, cpi=2)
        xs = _mix_call(xs, pe_l, yf, yb, u, o, mod, l, cond_row, *common, tm=512, last=last)

    return (xp, xs, jnp.stack(new_re, axis=1), jnp.stack(new_im, axis=1), jnp.stack(new_gla, axis=1))
```

```python
import functools
import math

import numpy as np
import jax
import jax.numpy as jnp
from jax import lax
from jax.experimental import pallas as pl
from jax.experimental.pallas import tpu as pltpu

F32 = jnp.float32
BF16 = jnp.bfloat16

D_MODEL = 1024
GRID_W = 64
S5_WIDTH = 512
S5_GROUPS = 32
S5_CH = 16
S5_STATE = 64
S5_NC = S5_GROUPS * S5_STATE
S5_SLAB_GROUPS = 8
S5_SLABS = S5_GROUPS // S5_SLAB_GROUPS
S5_SLAB_NC = S5_SLAB_GROUPS * S5_STATE
S5_SLAB_COLS = 2 * S5_SLAB_NC
GLA_HEADS = 4
GLA_DK = 64
GLA_DV = 128
GLA_KEY = GLA_HEADS * GLA_DK
GLA_WIDTH = GLA_HEADS * GLA_DV
GLA_RANK = 16
GLA_GATE_NORM = 16.0
GLA_CHUNK = 128
EPS = 1e-6
LANES = 128
PROJ_W = S5_WIDTH + 2 * GLA_KEY + GLA_WIDTH + LANES
GATE_W = 2 * S5_WIDTH + 2 * D_MODEL
VMEM_LIMIT = 60 * 1024 * 1024
MXU_TILE = 256


def _bdot(a, b):
    return jnp.dot(a.astype(BF16), b.astype(BF16), preferred_element_type=F32)


def _bdot_t(a, bt):
    return lax.dot_general(a.astype(BF16), bt.astype(BF16), (((1,), (1,)), ((), ())), preferred_element_type=F32)


def _split2(x):
    hi = x.astype(BF16)
    lo = (x - hi.astype(F32)).astype(BF16)
    return hi, lo


def _sigmoid(x):
    return 1.0 / (1.0 + jnp.exp2(x * (-math.log2(math.e))))


def _silu(x):
    return x * _sigmoid(x)


def _gelu_tanh(x):
    c = math.sqrt(2.0 / math.pi)
    return 0.5 * x * (1.0 + jnp.tanh(c * (x + 0.044715 * (x * x * x))))


def _log_sigmoid(x):
    return jnp.minimum(x, 0.0) - jnp.log(1.0 + jnp.exp(-jnp.abs(x)))


def _modulated_norm(x, ng, scale1p, shift):
    ms = jnp.mean(x * x, axis=-1, keepdims=True)
    return (x * lax.rsqrt(ms + EPS) * ng) * scale1p + shift


def _mod_kernel(cond_ref, w_ref, b_ref, out_ref):
    c = cond_ref[...]
    s_hi, s_lo = _split2(_silu(c))
    w_hi, w_lo = _split2(w_ref[0])
    acc = jnp.dot(s_hi, w_hi, preferred_element_type=F32)
    acc += jnp.dot(s_lo, w_hi, preferred_element_type=F32)
    acc += jnp.dot(s_hi, w_lo, preferred_element_type=F32)
    out_ref[0] = acc + b_ref[0]


def _mod_call(cond8, w_mod, b_mod):
    depth = w_mod.shape[0]
    nb = 1536
    return pl.pallas_call(
        _mod_kernel,
        grid=(depth, 3 * D_MODEL // nb),
        in_specs=[
            pl.BlockSpec((8, D_MODEL), lambda l, j: (0, 0)),
            pl.BlockSpec((1, D_MODEL, nb), lambda l, j: (l, 0, j)),
            pl.BlockSpec((1, 1, nb), lambda l, j: (l, 0, j)),
        ],
        out_specs=pl.BlockSpec((1, 8, nb), lambda l, j: (l, 0, j)),
        out_shape=jax.ShapeDtypeStruct((depth, 8, 3 * D_MODEL), F32),
        compiler_params=pltpu.CompilerParams(dimension_semantics=("arbitrary", "arbitrary")),
        name="adaln_mod",
    )(cond8, w_mod, b_mod.reshape(depth, 1, 3 * D_MODEL))


def _zoh(lre, lim, ldt):
    dt = jnp.exp(ldt)
    mag = jnp.exp(lre * dt)
    a_re = mag * jnp.cos(lim * dt)
    a_im = mag * jnp.sin(lim * dt)
    n_re = a_re - 1.0
    inv = 1.0 / (lre * lre + lim * lim)
    return a_re, a_im, (n_re * lre + a_im * lim) * inv, (a_im * lre - n_re * lim) * inv


def _zoh_kernel(lre_ref, lim_ref, ldt_ref, bre_ref, bim_ref, cre_ref, cim_ref, are_ref, aim_ref, wbu_ref, wc_ref,
                *, depth):
    a_re, a_im, c_re, c_im = _zoh(lre_ref[...], lim_ref[...], ldt_ref[...])
    are_ref[...] = a_re
    aim_ref[...] = a_im
    cmat = [cre_ref[...], -cim_ref[...]]
    wbu_ref[...] = jnp.zeros_like(wbu_ref)
    wc_ref[...] = jnp.zeros_like(wc_ref)
    for l in range(depth):
        for s in range(S5_SLABS):
            for gl in range(S5_SLAB_GROUPS):
                g = S5_SLAB_GROUPS * s + gl
                src = (l * S5_GROUPS + g) * S5_CH
                bre = bre_ref[src:src + S5_CH, :]
                bim = bim_ref[src:src + S5_CH, :]
                for d in range(2):
                    r = (l * 2 + d) * S5_GROUPS + g
                    k_re, k_im = c_re[r:r + 1, :], c_im[r:r + 1, :]
                    bbar = [k_re * bre - k_im * bim, k_re * bim + k_im * bre]
                    row = (d * S5_SLAB_GROUPS + gl) * S5_CH
                    for ri in range(2):
                        col = ri * S5_SLAB_NC + gl * S5_STATE
                        wbu_ref[l, s, row:row + S5_CH, col:col + S5_STATE] = bbar[ri].astype(BF16)
                src = (l * S5_GROUPS + g) * S5_STATE
                for ri in range(2):
                    col = ri * S5_SLAB_NC + gl * S5_STATE
                    wc_ref[l, s, col:col + S5_STATE, gl * S5_CH:(gl + 1) * S5_CH] = (
                        cmat[ri][src:src + S5_STATE, :].astype(BF16))


def _s5_params(s5_lam_re, s5_lam_im, s5_log_dt, s5_b_re, s5_b_im, s5_c_re, s5_c_im):
    depth = s5_lam_re.shape[0]
    small = lambda a: a.reshape(depth * 2 * S5_GROUPS, -1)
    swapped = lambda a: jnp.swapaxes(a, -1, -2).reshape(-1, a.shape[-2])
    a_re, a_im, w_bu, w_c = pl.pallas_call(
        functools.partial(_zoh_kernel, depth=depth),
        out_shape=(jax.ShapeDtypeStruct((depth * 2 * S5_GROUPS, S5_STATE), F32),
                   jax.ShapeDtypeStruct((depth * 2 * S5_GROUPS, S5_STATE), F32),
                   jax.ShapeDtypeStruct((depth, S5_SLABS, 2 * LANES, S5_SLAB_COLS), BF16),
                   jax.ShapeDtypeStruct((depth, S5_SLABS, S5_SLAB_COLS, LANES), BF16)),
        name="s5_zoh",
    )(small(s5_lam_re), small(s5_lam_im), small(s5_log_dt), swapped(s5_b_re), swapped(s5_b_im),
      swapped(s5_c_re), swapped(s5_c_im))
    return a_re.reshape(depth, 2, S5_NC), a_im.reshape(depth, 2, S5_NC), w_bu, w_c


def _s5_state_to_cols(re, im):
    lead = re.shape[:-2]
    st = jnp.stack([re.reshape(lead + (S5_SLABS, S5_SLAB_NC)), im.reshape(lead + (S5_SLABS, S5_SLAB_NC))], axis=-2)
    return st.reshape(lead + (2 * S5_NC,))


def _s5_cols_to_state(cols):
    lead = cols.shape[:-1]
    st = cols.reshape(lead + (S5_SLABS, 2, S5_SLAB_NC))
    re = st[..., 0, :].reshape(lead + (S5_GROUPS, S5_STATE))
    im = st[..., 1, :].reshape(lead + (S5_GROUPS, S5_STATE))
    return re, im


_IN_OFFS = [int(o) for o in np.cumsum([0, S5_WIDTH, S5_WIDTH, GLA_KEY, GLA_KEY, GLA_WIDTH, GLA_WIDTH, GLA_RANK,
                                       D_MODEL, D_MODEL])]


def _split_w_in_kernel(u_ref, qkv_ref, code_ref, ga_ref, gb_ref, m_ref, proj_ref, gate_ref):
    rows = 0
    for src in (u_ref, qkv_ref, code_ref):
        proj_ref[rows:rows + src.shape[1], :] = src[0].astype(BF16)
        rows += src.shape[1]
    proj_ref[rows:, :] = jnp.zeros((PROJ_W - rows, proj_ref.shape[1]), BF16)
    rows = 0
    for src in (ga_ref, gb_ref, m_ref):
        gate_ref[rows:rows + src.shape[1], :] = src[0].astype(BF16)
        rows += src.shape[1]


def _split_w_in(w_in):
    o = _IN_OFFS
    depth = w_in.shape[0]
    wt = jnp.swapaxes(w_in, 1, 2)
    tn = D_MODEL // 2
    rows = lambda lo, hi: pl.BlockSpec((pl.Element(1), pl.Element(hi - lo), pl.Element(tn)),
                                       lambda l, j: (l, lo, j * tn))
    out = lambda n: pl.BlockSpec((None, n, tn), lambda l, j: (l, 0, j))
    return pl.pallas_call(
        _split_w_in_kernel,
        grid=(depth, D_MODEL // tn),
        in_specs=[rows(o[0], o[1]), rows(o[2], o[5]), rows(o[6], o[7]), rows(o[1], o[2]), rows(o[5], o[6]),
                  rows(o[7], o[9])],
        out_specs=(out(PROJ_W), out(GATE_W)),
        out_shape=(jax.ShapeDtypeStruct((depth, PROJ_W, D_MODEL), BF16),
                   jax.ShapeDtypeStruct((depth, GATE_W, D_MODEL), BF16)),
        compiler_params=pltpu.CompilerParams(dimension_semantics=("arbitrary", "arbitrary"),
                                             vmem_limit_bytes=VMEM_LIMIT),
        name="split_w_in",
    )(wt, wt, wt, wt, wt, wt)


def _proj_kernel(*refs, has_pe):
    if has_pe:
        x_ref, pe_ref, shift_ref, scale_ref, ng_ref, w_ref, u_ref, q_ref, k_ref, v_ref, gl_ref = refs
    else:
        x_ref, shift_ref, scale_ref, ng_ref, w_ref, u_ref, q_ref, k_ref, v_ref, gl_ref = refs
    nsb, tm, _ = x_ref.shape
    rg = max(1, tm // (2 * MXU_TILE))
    grp = [(i, slice(j * tm // rg, (j + 1) * tm // rg)) for i in range(nsb) for j in range(rg)]
    x = [x_ref[i, r, :] + pe_ref[r, :] if has_pe else x_ref[i, r, :] for i, r in grp]
    h = [_modulated_norm(v, ng_ref[...], 1.0 + scale_ref[...], shift_ref[...]) for v in x]
    p = [_bdot_t(v, w_ref[...]) for v in h]
    for (i, r), v in zip(grp, p):
        o = 0
        for ref, width in ((u_ref, S5_WIDTH), (q_ref, GLA_KEY), (k_ref, GLA_KEY), (v_ref, GLA_WIDTH), (gl_ref, LANES)):
            ref[i, r, :] = v[:, o:o + width].astype(ref.dtype)
            o += width


def _mod_spec(l, cond_row, part):
    return pl.BlockSpec((None, None, 1, D_MODEL), lambda b, j: (l, cond_row(b), 0, part))


def _layer_spec(l, shape):
    return pl.BlockSpec((None,) + shape, lambda b, j: (l,) + (0,) * len(shape))


def _proj_call(x, pe, mod, l, cond_row, ng, w, tm, nsb=1):
    nseq, length, _ = x.shape
    row = lambda width: pl.BlockSpec((nsb, tm, width), lambda b, j: (b, j, 0))
    in_specs = [row(D_MODEL)]
    args = [x]
    if pe is not None:
        in_specs.append(pl.BlockSpec((tm, D_MODEL), lambda b, j: (j, 0)))
        args.append(pe)
    in_specs += [_mod_spec(l, cond_row, 0), _mod_spec(l, cond_row, 1), _layer_spec(l, (1, D_MODEL)),
                 _layer_spec(l, (PROJ_W, D_MODEL))]
    args += [mod, mod, ng, w]
    sds = lambda width, dt: jax.ShapeDtypeStruct((nseq, length, width), dt)
    return pl.pallas_call(
        functools.partial(_proj_kernel, has_pe=pe is not None),
        grid=(nseq // nsb, length // tm),
        in_specs=in_specs,
        out_specs=(row(S5_WIDTH), row(GLA_KEY), row(GLA_KEY), row(GLA_WIDTH), row(LANES)),
        out_shape=(sds(S5_WIDTH, F32), sds(GLA_KEY, F32), sds(GLA_KEY, F32), sds(GLA_WIDTH, BF16), sds(LANES, F32)),
        compiler_params=pltpu.CompilerParams(dimension_semantics=("arbitrary", "arbitrary"),
                                             vmem_limit_bytes=VMEM_LIMIT),
        name="proj",
    )(*args)


def _s5_perms(nb, tc):
    r = nb * tc
    pf = np.zeros((2 * r, r), np.float32)
    pb = np.zeros((2 * r, r), np.float32)
    for t in range(tc):
        for b in range(nb):
            pf[t * 2 * nb + b, b * tc + t] = 1.0
            pb[t * 2 * nb + nb + b, b * tc + (tc - 1 - t)] = 1.0
    return pf, pb


def _s5_store_unpacked(y2, yf_ref, yb_ref, t_f, t_b, nb, tc):
    rs = 2 * nb
    for t in range(tc):
        yf_ref[:, t_f + t, :] = y2[t * rs:t * rs + nb]
        yb_ref[:, t_b + tc - 1 - t, :] = y2[t * rs + nb:(t + 1) * rs]


def _s5_kernel(*refs, nb, tc, tiles_per_pass):
    use_perm = nb % 8 != 0
    uf1_ref, ub1_ref, uf2_ref, ub2_ref, uf0_ref, ub0_ref = refs[:6]
    pf_ref, pb_ref = refs[6:8] if use_perm else (None, None)
    wbu_ref, wc_ref, are_ref, aim_ref, h0_ref, yf_ref, yb_ref, hfin_ref, bua_ref, bub_ref, hb_ref, st_ref = refs[-12:]
    i = pl.program_id(0)
    rows = nb * tc
    rs = 2 * nb

    def packed_inputs(uf_ref, ub_ref):
        if not use_perm:
            return tuple(jnp.concatenate([r[:, t, :] for t in range(tc)], axis=0).astype(BF16) for r in (uf_ref, ub_ref))
        uf = uf_ref[...].reshape(rows, S5_WIDTH).astype(BF16)
        ub = ub_ref[...].reshape(rows, S5_WIDTH).astype(BF16)
        u2f = jnp.dot(pf_ref[...], uf, preferred_element_type=F32).astype(BF16)
        u2b = jnp.dot(pb_ref[...], ub, preferred_element_type=F32).astype(BF16)
        return [jnp.concatenate([u2f[:, s * LANES:(s + 1) * LANES], u2b[:, s * LANES:(s + 1) * LANES]], axis=1)
                for s in range(S5_SLABS)]

    def bu_slab(lhs, s, dst_ref):
        cols = slice(s * S5_SLAB_COLS, (s + 1) * S5_SLAB_COLS)
        if use_perm:
            dst_ref[:, cols] = jnp.dot(lhs[s], wbu_ref[s], preferred_element_type=F32)
            return
        uf, ub = lhs
        bf = jnp.dot(uf[:, s * LANES:(s + 1) * LANES], wbu_ref[s, :LANES, :], preferred_element_type=F32)
        bb = jnp.dot(ub[:, s * LANES:(s + 1) * LANES], wbu_ref[s, LANES:, :], preferred_element_type=F32)
        for t in range(tc):
            dst_ref[t * rs:t * rs + nb, cols] = bf[t * nb:(t + 1) * nb]
            dst_ref[t * rs + nb:(t + 1) * rs, cols] = bb[(tc - 1 - t) * nb:(tc - t) * nb]

    @pl.when(i == 0)
    def _():
        st_ref[...] = h0_ref[...]
        lhs0 = packed_inputs(uf0_ref, ub0_ref)
        for s in range(S5_SLABS):
            bu_slab(lhs0, s, bua_ref)

    half_step = functools.partial(_s5_half_step, wc_ref=wc_ref, are_ref=are_ref, aim_ref=aim_ref, hb_ref=hb_ref,
                                  st_ref=st_ref, bu_slab=bu_slab, rs=rs, tc=tc, tiles_per_pass=tiles_per_pass)
    y2 = half_step(bua_ref, bub_ref, packed_inputs(uf1_ref, ub1_ref))
    _s5_store_unpacked(y2, yf_ref, yb_ref, 0, tc, nb, tc)
    y2 = half_step(bub_ref, bua_ref, packed_inputs(uf2_ref, ub2_ref))
    _s5_store_unpacked(y2, yf_ref, yb_ref, tc, 0, nb, tc)
    hfin_ref[...] = st_ref[...]


def _s5_half_step(cur_ref, nxt_ref, lhs_next, *, wc_ref, are_ref, aim_ref, hb_ref, st_ref, bu_slab,
                  rs, tc, tiles_per_pass):
    slab_tiles = S5_SLAB_NC // LANES
    grp = max(1, 16 // rs)
    ys = []
    for s in range(S5_SLABS):
        bu_slab(lhs_next, s, nxt_ref)
        for c0 in range(slab_tiles * s, slab_tiles * (s + 1), tiles_per_pass):
            cts = list(range(c0, c0 + tiles_per_pass))
            cre = [(ct // slab_tiles) * S5_SLAB_COLS + (ct % slab_tiles) * LANES for ct in cts]
            cim = [c + S5_SLAB_NC for c in cre]
            a_re = [are_ref[:, ct * LANES:(ct + 1) * LANES] for ct in cts]
            a_im = [aim_ref[:, ct * LANES:(ct + 1) * LANES] for ct in cts]
            h_re = [st_ref[:, c:c + LANES] for c in cre]
            h_im = [st_ref[:, c:c + LANES] for c in cim]
            for t0 in range(0, tc, grp):
                out_re = [[] for _ in cts]
                out_im = [[] for _ in cts]
                for t in range(t0, t0 + grp):
                    r0 = t * rs
                    for j in range(len(cts)):
                        b_re = cur_ref[r0:r0 + rs, cre[j]:cre[j] + LANES]
                        b_im = cur_ref[r0:r0 + rs, cim[j]:cim[j] + LANES]
                        n_re = a_re[j] * h_re[j] - a_im[j] * h_im[j] + b_re
                        n_im = a_re[j] * h_im[j] + a_im[j] * h_re[j] + b_im
                        h_re[j], h_im[j] = n_re, n_im
                        out_re[j].append(n_re)
                        out_im[j].append(n_im)
                for j in range(len(cts)):
                    blk_re = out_re[j][0] if grp == 1 else jnp.concatenate(out_re[j], axis=0)
                    blk_im = out_im[j][0] if grp == 1 else jnp.concatenate(out_im[j], axis=0)
                    hb_ref[t0 * rs:(t0 + grp) * rs, cre[j]:cre[j] + LANES] = blk_re.astype(BF16)
                    hb_ref[t0 * rs:(t0 + grp) * rs, cim[j]:cim[j] + LANES] = blk_im.astype(BF16)
            for j in range(len(cts)):
                st_ref[:, cre[j]:cre[j] + LANES] = h_re[j]
                st_ref[:, cim[j]:cim[j] + LANES] = h_im[j]
        ys.append(jnp.dot(hb_ref[:, s * S5_SLAB_COLS:(s + 1) * S5_SLAB_COLS], wc_ref[s], preferred_element_type=F32))
    return jnp.concatenate(ys, axis=1)


def _s5_call(u, l, w_bu, w_c, are, aim, h0, tc, tiles_per_pass):
    nb, length, _ = u.shape
    n = length // tc
    rows = nb * tc
    const = lambda shape: pl.BlockSpec(shape, lambda i: (0,) * len(shape))
    layer = lambda shape: pl.BlockSpec((None,) + shape, lambda i: (l,) + (0,) * len(shape))
    blk = lambda chunk: pl.BlockSpec((nb, tc, S5_WIDTH), lambda i: (0, chunk(i), 0))
    out_f = pl.BlockSpec((nb, 2 * tc, S5_WIDTH), lambda i: (0, i, 0))
    out_b = pl.BlockSpec((nb, 2 * tc, S5_WIDTH), lambda i: (0, n // 2 - 1 - i, 0))
    second = lambda i: jnp.minimum(2 * i + 2, n - 1)
    in_specs = [blk(lambda i: 2 * i + 1), blk(lambda i: n - 2 - 2 * i), blk(second), blk(lambda i: n - 1 - second(i)),
                blk(lambda i: 0), blk(lambda i: n - 1)]
    args = [u] * 6
    if nb % 8 != 0:
        pf, pb = _s5_perms(nb, tc)
        in_specs += [const((2 * rows, rows)), const((2 * rows, rows))]
        args += [jnp.asarray(pf, BF16), jnp.asarray(pb, BF16)]
    in_specs += [layer((S5_SLABS, 2 * LANES, S5_SLAB_COLS)), layer((S5_SLABS, S5_SLAB_COLS, LANES)),
                 layer((2 * nb, S5_NC)), layer((2 * nb, S5_NC)), const((2 * nb, 2 * S5_NC))]
    args += [w_bu, w_c, are, aim, h0]
    return pl.pallas_call(
        functools.partial(_s5_kernel, nb=nb, tc=tc, tiles_per_pass=tiles_per_pass),
        grid=(n // 2,),
        in_specs=in_specs,
        out_specs=(out_f, out_b, const((2 * nb, 2 * S5_NC))),
        out_shape=(jax.ShapeDtypeStruct(u.shape, F32), jax.ShapeDtypeStruct(u.shape, F32),
                   jax.ShapeDtypeStruct((2 * nb, 2 * S5_NC), F32)),
        scratch_shapes=[pltpu.VMEM((2 * rows, 2 * S5_NC), F32), pltpu.VMEM((2 * rows, 2 * S5_NC), F32),
                        pltpu.VMEM((2 * rows, 2 * S5_NC), BF16), pltpu.VMEM((2 * nb, 2 * S5_NC), F32)],
        compiler_params=pltpu.CompilerParams(dimension_semantics=("arbitrary",), vmem_limit_bytes=VMEM_LIMIT),
        name="s5_scan",
    )(*args)


def _gla_kernel(*refs, length, has_s0, nsq, cpi):
    q_ref, k_ref, v_ref, gl_ref, wg_ref, bg_ref = refs[:6]
    s0_ref = refs[6] if has_s0 else None
    o_ref, sfin_ref, sf_ref, sb_ref = refs[-4:]
    c = GLA_CHUNK
    n = length // c
    ri = lax.broadcasted_iota(jnp.int32, (c, c), 0)
    ci = lax.broadcasted_iota(jnp.int32, (c, c), 1)
    lane = lax.broadcasted_iota(jnp.int32, (1, GLA_KEY), 1)
    head_mask = [(lane >= h * GLA_DK) & (lane < (h + 1) * GLA_DK) for h in range(GLA_HEADS)]
    causal = [ci <= ri, ci >= ri]
    tri = [m.astype(BF16) for m in causal]
    end = (c - 1, 0)
    chains = [(sq, d, j) for j in range(cpi) for sq in range(nsq) for d in (0, 1)]
    key_rows = lambda h: slice(h * GLA_DK, (h + 1) * GLA_DK)
    val_cols = lambda h: slice(h * GLA_DV, (h + 1) * GLA_DV)

    sb_ref[...] = jnp.zeros_like(sb_ref)
    for sq in range(nsq):
        for d in (0, 1):
            for h in range(GLA_HEADS):
                if has_s0:
                    sf_ref[sq, d, h] = s0_ref[sq, d, h]
                    sb_ref[sq, d, key_rows(h), val_cols(h)] = s0_ref[sq, d, h].astype(BF16)
                else:
                    sf_ref[sq, d, h] = jnp.zeros((GLA_DK, GLA_DV), F32)

    def chunk_group(r):
        row = lambda d, j: r[d][j]
        qc = [q_ref[sq, pl.ds(row(d, j), c), :] * (GLA_DK ** -0.5) for sq, d, j in chains]
        kc = [k_ref[sq, pl.ds(row(d, j), c), :] for sq, d, j in chains]
        vc = [v_ref[sq, pl.ds(row(d, j), c), :] for sq, d, j in chains]
        logits = [_bdot(gl_ref[sq, pl.ds(row(d, j), c), :], wg_ref[d]) + bg_ref[d] for sq, d, j in chains]
        g = [_split2(_log_sigmoid(x) * (1.0 / GLA_GATE_NORM)) for x in logits]
        b = [jnp.dot(tri[d], hi, preferred_element_type=F32) + jnp.dot(tri[d], lo, preferred_element_type=F32)
             for (sq, d, j), (hi, lo) in zip(chains, g)]
        bt = [x.T for x in b]
        kt = [x.T for x in kc]
        b_mid = [x[c // 2:c // 2 + 1, :] for x in b]
        bt_mid = [x[:, c // 2:c // 2 + 1] for x in bt]
        bt_end = [x[:, end[d]:end[d] + 1] for (sq, d, j), x in zip(chains, bt)]
        q_in = [(x * jnp.exp(y)).astype(BF16) for x, y in zip(qc, b)]
        q_mid = [(x * jnp.exp(y - m)).astype(BF16) for x, y, m in zip(qc, b, b_mid)]
        k_mid_t = [(x * jnp.exp(m - y)).astype(BF16) for x, y, m in zip(kt, bt, bt_mid)]
        k_end_t = [(x * jnp.exp(e - y)).astype(BF16) for x, y, e in zip(kt, bt, bt_end)]
        q_st = [jnp.concatenate([jnp.where(head_mask[h], x, jnp.zeros_like(x)) for h in range(GLA_HEADS)], axis=0)
                for x in q_mid]
        a = [jnp.dot(x, y, preferred_element_type=F32) for x, y in zip(q_st, k_mid_t)]
        parts = [[] for _ in chains]
        for h in range(GLA_HEADS):
            for i, (sq, d, j) in enumerate(chains):
                a_h = jnp.where(causal[d], a[i][h * c:(h + 1) * c, :], 0.0).astype(BF16)
                parts[i].append(jnp.dot(a_h, vc[i][:, val_cols(h)], preferred_element_type=F32))
        upd = [[jnp.dot(k_end_t[i][key_rows(h), :], vc[i][:, val_cols(h)], preferred_element_type=F32)
                for h in range(GLA_HEADS)] for i in range(len(chains))]
        o_inter = [None] * len(chains)
        for i, (sq, d, j) in enumerate(chains):
            o_inter[i] = jnp.dot(q_in[i], sb_ref[sq, d], preferred_element_type=F32)
            for h in range(GLA_HEADS):
                s_new = jnp.exp(bt_end[i][key_rows(h), :]) * sf_ref[sq, d, h] + upd[i][h]
                sf_ref[sq, d, h] = s_new
                sb_ref[sq, d, key_rows(h), val_cols(h)] = s_new.astype(BF16)
        return [x + jnp.concatenate(p, axis=1) for x, p in zip(o_inter, parts)]

    def rows_of(step):
        fwd = [pl.multiple_of((step * cpi + j) * c, c) for j in range(cpi)]
        bwd = [pl.multiple_of((n - 1 - step * cpi - j) * c, c) for j in range(cpi)]
        return fwd, bwd

    def first_half(step, carry):
        r = rows_of(step)
        for (sq, d, j), o in zip(chains, chunk_group(r)):
            o_ref[sq, pl.ds(r[d][j], c), :] = o
        return carry

    def second_half(step, carry):
        r = rows_of(step)
        for (sq, d, j), o in zip(chains, chunk_group(r)):
            o_ref[sq, pl.ds(r[d][j], c), :] = o + o_ref[sq, pl.ds(r[d][j], c), :]
        return carry

    steps = n // cpi
    lax.fori_loop(0, steps // 2, first_half, 0)
    lax.fori_loop(steps // 2, steps, second_half, 0)
    for sq in range(nsq):
        for d in (0, 1):
            for h in range(GLA_HEADS):
                sfin_ref[sq, d, h] = sf_ref[sq, d, h]


def _gla_call(q, k, v, gl, l, wg, bg, s0, nsq, cpi):
    nseq, length, _ = q.shape
    assert (length // GLA_CHUNK) % (2 * cpi) == 0 and nseq % nsq == 0
    seq = lambda width: pl.BlockSpec((nsq, length, width), lambda b: (b, 0, 0))
    layer = lambda shape: pl.BlockSpec((None,) + shape, lambda b: (l,) + (0,) * len(shape))
    in_specs = [seq(GLA_KEY), seq(GLA_KEY), seq(GLA_WIDTH), seq(LANES),
                layer((2, LANES, GLA_KEY)), layer((2, 1, GLA_KEY))]
    args = [q, k, v, gl, wg, bg]
    if s0 is not None:
        in_specs.append(pl.BlockSpec((nsq, None, 2, GLA_HEADS, GLA_DK, GLA_DV), lambda b: (b, l, 0, 0, 0, 0)))
        args.append(s0)
    return pl.pallas_call(
        functools.partial(_gla_kernel, length=length, has_s0=s0 is not None, nsq=nsq, cpi=cpi),
        grid=(nseq // nsq,),
        in_specs=in_specs,
        out_specs=(seq(GLA_WIDTH), pl.BlockSpec((nsq, 2, GLA_HEADS, GLA_DK, GLA_DV), lambda b: (b, 0, 0, 0, 0))),
        out_shape=(jax.ShapeDtypeStruct((nseq, length, GLA_WIDTH), F32),
                   jax.ShapeDtypeStruct((nseq, 2, GLA_HEADS, GLA_DK, GLA_DV), F32)),
        scratch_shapes=[pltpu.VMEM((nsq, 2, GLA_HEADS, GLA_DK, GLA_DV), F32),
                        pltpu.VMEM((nsq, 2, GLA_KEY, GLA_WIDTH), BF16)],
        compiler_params=pltpu.CompilerParams(dimension_semantics=("arbitrary",), vmem_limit_bytes=VMEM_LIMIT),
        name="gla",
    )(*args)


def _mix_kernel(*refs, has_pe, last, row_groups):
    if has_pe:
        (x_ref, pe_ref, yf_ref, yb_ref, u_ref, o_ref, shift_ref, scale_ref, gate_ref, ng_ref, wg_ref, d_ref,
         wglu_ref, bglu_ref, wpa_ref, wpb_ref, wo_ref, fng_ref, gng_ref, out_ref) = refs
    else:
        (x_ref, yf_ref, yb_ref, u_ref, o_ref, shift_ref, scale_ref, gate_ref, ng_ref, wg_ref, d_ref,
         wglu_ref, bglu_ref, wpa_ref, wpb_ref, wo_ref, fng_ref, gng_ref, out_ref) = refs

    def head_norm(o):
        heads = [o[:, h * GLA_DV:(h + 1) * GLA_DV] for h in range(GLA_HEADS)]
        heads = [v * lax.rsqrt(jnp.mean(v * v, axis=-1, keepdims=True) + EPS) for v in heads]
        return jnp.concatenate(heads, axis=1) * gng_ref[...]

    nsb, tm, _ = x_ref.shape
    grp = [(i, slice(j * tm // row_groups, (j + 1) * tm // row_groups)) for i in range(nsb) for j in range(row_groups)]
    x = [x_ref[i, r, :] + pe_ref[r, :] if has_pe else x_ref[i, r, :] for i, r in grp]
    h = [_modulated_norm(v, ng_ref[...], 1.0 + scale_ref[...], shift_ref[...]).astype(BF16) for v in x]
    gts = [_bdot_t(v, wg_ref[...]) for v in h]
    y = [_gelu_tanh(yf_ref[i, r, :] + yb_ref[i, r, :] + d_ref[...] * u_ref[i, r, :]) for i, r in grp]
    glu = [_bdot(v, wglu_ref[...]) for v in y]
    y = [v * _sigmoid(g + bglu_ref[...]) for v, g in zip(y, glu)]
    y_a = [v * _silu(g[:, :S5_WIDTH]) for v, g in zip(y, gts)]
    y_b = [head_norm(o_ref[i, r, :]) * _silu(g[:, S5_WIDTH:2 * S5_WIDTH]) for (i, r), g in zip(grp, gts)]
    p_a = [_bdot(v, wpa_ref[...]) for v in y_a]
    p_b = [_bdot(v, wpb_ref[...]) for v in y_b]
    merged = [_sigmoid(g[:, 2 * S5_WIDTH:2 * S5_WIDTH + D_MODEL]) * a + _sigmoid(g[:, 2 * S5_WIDTH + D_MODEL:]) * b
              for g, a, b in zip(gts, p_a, p_b)]
    xn = [v + gate_ref[...] * _bdot(m, wo_ref[...]) for v, m in zip(x, merged)]
    for (i, r), v in zip(grp, xn):
        if last:
            ms = jnp.mean(v * v, axis=-1, keepdims=True)
            v = v * lax.rsqrt(ms + EPS) * fng_ref[...]
        out_ref[i, r, :] = v


def _mix_call(x, pe, yf, yb, u, o, mod, l, cond_row, ng, wg, s5_d, wglu, bglu, wpa, wpb, wo, fng, gng, tm, last,
              nsb=1):
    nseq, length, _ = x.shape
    row = lambda width: pl.BlockSpec((nsb, tm, width), lambda b, j: (b, j, 0))
    in_specs = [row(D_MODEL)]
    args = [x]
    if pe is not None:
        in_specs.append(pl.BlockSpec((tm, D_MODEL), lambda b, j: (j, 0)))
        args.append(pe)
    in_specs += [row(S5_WIDTH), row(S5_WIDTH), row(S5_WIDTH), row(GLA_WIDTH),
                 _mod_spec(l, cond_row, 0), _mod_spec(l, cond_row, 1), _mod_spec(l, cond_row, 2),
                 _layer_spec(l, (1, D_MODEL)), _layer_spec(l, (GATE_W, D_MODEL)), _layer_spec(l, (1, S5_WIDTH)),
                 _layer_spec(l, (S5_WIDTH, S5_WIDTH)), _layer_spec(l, (1, S5_WIDTH)),
                 _layer_spec(l, (S5_WIDTH, D_MODEL)), _layer_spec(l, (GLA_WIDTH, D_MODEL)),
                 _layer_spec(l, (D_MODEL, D_MODEL)), pl.BlockSpec((1, D_MODEL), lambda b, j: (0, 0)),
                 _layer_spec(l, (1, GLA_WIDTH))]
    args += [yf, yb, u, o, mod, mod, mod, ng, wg, s5_d, wglu, bglu, wpa, wpb, wo, fng, gng]
    return pl.pallas_call(
        functools.partial(_mix_kernel, has_pe=pe is not None, last=last, row_groups=max(1, tm // MXU_TILE)),
        grid=(nseq // nsb, length // tm),
        in_specs=in_specs,
        out_specs=row(D_MODEL),
        out_shape=jax.ShapeDtypeStruct(x.shape, F32),
        compiler_params=pltpu.CompilerParams(dimension_semantics=("arbitrary", "arbitrary"),
                                             vmem_limit_bytes=VMEM_LIMIT),
        name="mix",
    )(*args)


def _grid_pos_embed(length, dim):
    rows = length // GRID_W
    quarter = dim // 4
    freqs = jnp.exp(-math.log(10000.0) * jnp.arange(quarter, dtype=F32) / quarter)

    def sincos(pos):
        ang = pos.astype(F32)[:, None] * freqs[None, :]
        return jnp.concatenate([jnp.sin(ang), jnp.cos(ang)], axis=-1)

    er = sincos(jnp.arange(rows))
    ec = sincos(jnp.arange(GRID_W))
    pe = jnp.concatenate([jnp.broadcast_to(er[:, None, :], (rows, GRID_W, dim // 2)),
                          jnp.broadcast_to(ec[None, :, :], (rows, GRID_W, dim // 2))], axis=-1)
    return pe.reshape(rows * GRID_W, dim)


def kernel(x_prompt, x_sample, c, state_s5_re, state_s5_im, state_gla, c_ctx, norm_g, w_mod, b_mod, w_in, gla_wg_up,
           gla_bg, gla_norm_g, s5_lam_re, s5_lam_im, s5_log_dt, s5_b_re, s5_b_im, s5_c_re, s5_c_im, s5_d, w_glu,
           b_glu, w_pa, w_pb, w_o, final_norm_g):
    depth = w_in.shape[0]
    bp, lp, _ = x_prompt.shape
    bs, ls, _ = x_sample.shape

    cond8 = jnp.concatenate([c_ctx[None], c, jnp.zeros((8 - 1 - bs, D_MODEL), F32)], axis=0)
    mod = _mod_call(cond8, w_mod, b_mod)

    a_re, a_im, w_bu, w_c = _s5_params(s5_lam_re, s5_lam_im, s5_log_dt, s5_b_re, s5_b_im, s5_c_re, s5_c_im)

    w_proj, w_gate = _split_w_in(w_in)
    wg_up = jnp.concatenate([gla_wg_up, jnp.zeros((depth, 2, LANES - GLA_RANK, GLA_KEY), F32)], axis=2).astype(BF16)
    w_glu_b, w_pa_b, w_pb_b, w_o_b = (w.astype(BF16) for w in (w_glu, w_pa, w_pb, w_o))

    vec = lambda a: a[:, None, :]
    mod = mod.reshape(depth, 8, 1, 3 * D_MODEL)
    ng, gng, bg = vec(norm_g), vec(gla_norm_g), gla_bg[:, :, None, :]
    common = (ng, w_gate, vec(s5_d), w_glu_b, vec(b_glu), w_pa_b, w_pb_b, w_o_b, final_norm_g[None], gng)
    are_p, aim_p = jnp.repeat(a_re, bp, axis=1), jnp.repeat(a_im, bp, axis=1)
    are_s, aim_s = jnp.repeat(a_re, bs, axis=1), jnp.repeat(a_im, bs, axis=1)
    ctx_row = lambda b: 0
    cond_row = lambda b: b + 1

    pe = _grid_pos_embed(ls, D_MODEL)
    zero_h0 = jnp.zeros((2 * bp, 2 * S5_NC), F32)
    xp, xs = x_prompt, x_sample
    new_re, new_im, new_gla = [], [], []
    for l in range(depth):
        last = l == depth - 1

        u, q, k, v, gl = _proj_call(xp, None, mod, l, ctx_row, ng, w_proj, tm=lp, nsb=4)
        yf, yb, hfin = _s5_call(u, l, w_bu, w_c, are_p, aim_p, zero_h0, tc=16, tiles_per_pass=2)
        o, sfin = _gla_call(q, k, v, gl, l, wg_up, bg, None, nsq=4, cpi=1)
        new_gla.append(sfin)
        xp = _mix_call(xp, None, yf, yb, u, o, mod, l, ctx_row, *common, tm=lp, last=last, nsb=2)
        re_l, im_l = _s5_cols_to_state(hfin.reshape(2, bp, 2 * S5_NC))
        new_re.append(jnp.swapaxes(re_l, 0, 1))
        new_im.append(jnp.swapaxes(im_l, 0, 1))

        pe_l = pe if l == 0 else None
        u, q, k, v, gl = _proj_call(xs, pe_l, mod, l, cond_row, ng, w_proj, tm=1024)
        h0 = _s5_state_to_cols(jnp.swapaxes(state_s5_re[:, l], 0, 1), jnp.swapaxes(state_s5_im[:, l], 0, 1))
        yf, yb, _ = _s5_call(u, l, w_bu, w_c, are_s, aim_s, h0.reshape(2 * bs, 2 * S5_NC), tc=32, tiles_per_pass=4)
        o, _ = _gla_call(q, k, v, gl, l, wg_up, bg, state_gla, nsq=1, cpi=4)
        xs = _mix_call(xs, pe_l, yf, yb, u, o, mod, l, cond_row, *common, tm=512, last=last)

    return (xp, xs, jnp.stack(new_re, axis=1), jnp.stack(new_im, axis=1), jnp.stack(new_gla, axis=1))
```

```python
import functools
import math

import numpy as np
import jax
import jax.numpy as jnp
from jax import lax
from jax.experimental import pallas as pl
from jax.experimental.pallas import tpu as pltpu

F32 = jnp.float32
BF16 = jnp.bfloat16

D_MODEL = 1024
GRID_W = 64
S5_WIDTH = 512
S5_GROUPS = 32
S5_CH = 16
S5_STATE = 64
S5_NC = S5_GROUPS * S5_STATE
S5_SLAB_GROUPS = 8
S5_SLABS = S5_GROUPS // S5_SLAB_GROUPS
S5_SLAB_NC = S5_SLAB_GROUPS * S5_STATE
S5_SLAB_COLS = 2 * S5_SLAB_NC
GLA_HEADS = 4
GLA_DK = 64
GLA_DV = 128
GLA_KEY = GLA_HEADS * GLA_DK
GLA_WIDTH = GLA_HEADS * GLA_DV
GLA_RANK = 16
GLA_GATE_NORM = 16.0
GLA_CHUNK = 128
EPS = 1e-6
LANES = 128
PROJ_W = S5_WIDTH + 2 * GLA_KEY + GLA_WIDTH + LANES
GATE_W = 2 * S5_WIDTH + 2 * D_MODEL
VMEM_LIMIT = 60 * 1024 * 1024
MXU_TILE = 256


def _bdot(a, b):
    return jnp.dot(a.astype(BF16), b.astype(BF16), preferred_element_type=F32)


def _bdot_t(a, bt):
    return lax.dot_general(a.astype(BF16), bt.astype(BF16), (((1,), (1,)), ((), ())), preferred_element_type=F32)


def _split2(x):
    hi = x.astype(BF16)
    lo = (x - hi.astype(F32)).astype(BF16)
    return hi, lo


def _sigmoid(x):
    return 1.0 / (1.0 + jnp.exp2(x * (-math.log2(math.e))))


def _silu(x):
    return x * _sigmoid(x)


def _gelu_tanh(x):
    c = math.sqrt(2.0 / math.pi)
    return 0.5 * x * (1.0 + jnp.tanh(c * (x + 0.044715 * (x * x * x))))


def _log_sigmoid(x):
    return jnp.minimum(x, 0.0) - jnp.log(1.0 + jnp.exp(-jnp.abs(x)))


def _modulated_norm(x, ng, scale1p, shift):
    ms = jnp.mean(x * x, axis=-1, keepdims=True)
    return (x * lax.rsqrt(ms + EPS) * ng) * scale1p + shift


def _mod_kernel(cond_ref, w_ref, b_ref, out_ref):
    c = cond_ref[...]
    s_hi, s_lo = _split2(_silu(c))
    w_hi, w_lo = _split2(w_ref[0])
    acc = jnp.dot(s_hi, w_hi, preferred_element_type=F32)
    acc += jnp.dot(s_lo, w_hi, preferred_element_type=F32)
    acc += jnp.dot(s_hi, w_lo, preferred_element_type=F32)
    out_ref[0] = acc + b_ref[0]


def _mod_call(cond8, w_mod, b_mod):
    depth = w_mod.shape[0]
    nb = 1536
    return pl.pallas_call(
        _mod_kernel,
        grid=(depth, 3 * D_MODEL // nb),
        in_specs=[
            pl.BlockSpec((8, D_MODEL), lambda l, j: (0, 0)),
            pl.BlockSpec((1, D_MODEL, nb), lambda l, j: (l, 0, j)),
            pl.BlockSpec((1, 1, nb), lambda l, j: (l, 0, j)),
        ],
        out_specs=pl.BlockSpec((1, 8, nb), lambda l, j: (l, 0, j)),
        out_shape=jax.ShapeDtypeStruct((depth, 8, 3 * D_MODEL), F32),
        compiler_params=pltpu.CompilerParams(dimension_semantics=("arbitrary", "arbitrary")),
        name="adaln_mod",
    )(cond8, w_mod, b_mod.reshape(depth, 1, 3 * D_MODEL))


def _zoh(lre, lim, ldt):
    dt = jnp.exp(ldt)
    mag = jnp.exp(lre * dt)
    a_re = mag * jnp.cos(lim * dt)
    a_im = mag * jnp.sin(lim * dt)
    n_re = a_re - 1.0
    inv = 1.0 / (lre * lre + lim * lim)
    return a_re, a_im, (n_re * lre + a_im * lim) * inv, (a_im * lre - n_re * lim) * inv


def _zoh_kernel(lre_ref, lim_ref, ldt_ref, bre_ref, bim_ref, cre_ref, cim_ref, are_ref, aim_ref, wbu_ref, wc_ref,
                *, depth):
    a_re, a_im, c_re, c_im = _zoh(lre_ref[...], lim_ref[...], ldt_ref[...])
    are_ref[...] = a_re
    aim_ref[...] = a_im
    cmat = [cre_ref[...], -cim_ref[...]]
    wbu_ref[...] = jnp.zeros_like(wbu_ref)
    wc_ref[...] = jnp.zeros_like(wc_ref)
    for l in range(depth):
        for s in range(S5_SLABS):
            for gl in range(S5_SLAB_GROUPS):
                g = S5_SLAB_GROUPS * s + gl
                src = (l * S5_GROUPS + g) * S5_CH
                bre = bre_ref[src:src + S5_CH, :]
                bim = bim_ref[src:src + S5_CH, :]
                for d in range(2):
                    r = (l * 2 + d) * S5_GROUPS + g
                    k_re, k_im = c_re[r:r + 1, :], c_im[r:r + 1, :]
                    bbar = [k_re * bre - k_im * bim, k_re * bim + k_im * bre]
                    row = (d * S5_SLAB_GROUPS + gl) * S5_CH
                    for ri in range(2):
                        col = ri * S5_SLAB_NC + gl * S5_STATE
                        wbu_ref[l, s, row:row + S5_CH, col:col + S5_STATE] = bbar[ri].astype(BF16)
                src = (l * S5_GROUPS + g) * S5_STATE
                for ri in range(2):
                    col = ri * S5_SLAB_NC + gl * S5_STATE
                    wc_ref[l, s, col:col + S5_STATE, gl * S5_CH:(gl + 1) * S5_CH] = (
                        cmat[ri][src:src + S5_STATE, :].astype(BF16))


def _s5_params(s5_lam_re, s5_lam_im, s5_log_dt, s5_b_re, s5_b_im, s5_c_re, s5_c_im):
    depth = s5_lam_re.shape[0]
    small = lambda a: a.reshape(depth * 2 * S5_GROUPS, -1)
    swapped = lambda a: jnp.swapaxes(a, -1, -2).reshape(-1, a.shape[-2])
    a_re, a_im, w_bu, w_c = pl.pallas_call(
        functools.partial(_zoh_kernel, depth=depth),
        out_shape=(jax.ShapeDtypeStruct((depth * 2 * S5_GROUPS, S5_STATE), F32),
                   jax.ShapeDtypeStruct((depth * 2 * S5_GROUPS, S5_STATE), F32),
                   jax.ShapeDtypeStruct((depth, S5_SLABS, 2 * LANES, S5_SLAB_COLS), BF16),
                   jax.ShapeDtypeStruct((depth, S5_SLABS, S5_SLAB_COLS, LANES), BF16)),
        name="s5_zoh",
    )(small(s5_lam_re), small(s5_lam_im), small(s5_log_dt), swapped(s5_b_re), swapped(s5_b_im),
      swapped(s5_c_re), swapped(s5_c_im))
    return a_re.reshape(depth, 2, S5_NC), a_im.reshape(depth, 2, S5_NC), w_bu, w_c


def _s5_state_to_cols(re, im):
    lead = re.shape[:-2]
    st = jnp.stack([re.reshape(lead + (S5_SLABS, S5_SLAB_NC)), im.reshape(lead + (S5_SLABS, S5_SLAB_NC))], axis=-2)
    return st.reshape(lead + (2 * S5_NC,))


def _s5_cols_to_state(cols):
    lead = cols.shape[:-1]
    st = cols.reshape(lead + (S5_SLABS, 2, S5_SLAB_NC))
    re = st[..., 0, :].reshape(lead + (S5_GROUPS, S5_STATE))
    im = st[..., 1, :].reshape(lead + (S5_GROUPS, S5_STATE))
    return re, im


_IN_OFFS = [int(o) for o in np.cumsum([0, S5_WIDTH, S5_WIDTH, GLA_KEY, GLA_KEY, GLA_WIDTH, GLA_WIDTH, GLA_RANK,
                                       D_MODEL, D_MODEL])]


def _split_w_in_kernel(u_ref, qkv_ref, code_ref, ga_ref, gb_ref, m_ref, proj_ref, gate_ref):
    rows = 0
    for src in (u_ref, qkv_ref, code_ref):
        proj_ref[rows:rows + src.shape[1], :] = src[0].astype(BF16)
        rows += src.shape[1]
    proj_ref[rows:, :] = jnp.zeros((PROJ_W - rows, proj_ref.shape[1]), BF16)
    rows = 0
    for src in (ga_ref, gb_ref, m_ref):
        gate_ref[rows:rows + src.shape[1], :] = src[0].astype(BF16)
        rows += src.shape[1]


def _split_w_in(w_in):
    o = _IN_OFFS
    depth = w_in.shape[0]
    wt = jnp.swapaxes(w_in, 1, 2)
    tn = D_MODEL // 2
    rows = lambda lo, hi: pl.BlockSpec((pl.Element(1), pl.Element(hi - lo), pl.Element(tn)),
                                       lambda l, j: (l, lo, j * tn))
    out = lambda n: pl.BlockSpec((None, n, tn), lambda l, j: (l, 0, j))
    return pl.pallas_call(
        _split_w_in_kernel,
        grid=(depth, D_MODEL // tn),
        in_specs=[rows(o[0], o[1]), rows(o[2], o[5]), rows(o[6], o[7]), rows(o[1], o[2]), rows(o[5], o[6]),
                  rows(o[7], o[9])],
        out_specs=(out(PROJ_W), out(GATE_W)),
        out_shape=(jax.ShapeDtypeStruct((depth, PROJ_W, D_MODEL), BF16),
                   jax.ShapeDtypeStruct((depth, GATE_W, D_MODEL), BF16)),
        compiler_params=pltpu.CompilerParams(dimension_semantics=("arbitrary", "arbitrary"),
                                             vmem_limit_bytes=VMEM_LIMIT),
        name="split_w_in",
    )(wt, wt, wt, wt, wt, wt)


def _proj_kernel(*refs, has_pe):
    if has_pe:
        x_ref, pe_ref, shift_ref, scale_ref, ng_ref, w_ref, u_ref, q_ref, k_ref, v_ref, gl_ref = refs
    else:
        x_ref, shift_ref, scale_ref, ng_ref, w_ref, u_ref, q_ref, k_ref, v_ref, gl_ref = refs
    nsb, tm, _ = x_ref.shape
    rg = max(1, tm // (2 * MXU_TILE))
    grp = [(i, slice(j * tm // rg, (j + 1) * tm // rg)) for i in range(nsb) for j in range(rg)]
    x = [x_ref[i, r, :] + pe_ref[r, :] if has_pe else x_ref[i, r, :] for i, r in grp]
    h = [_modulated_norm(v, ng_ref[...], 1.0 + scale_ref[...], shift_ref[...]) for v in x]
    p = [_bdot_t(v, w_ref[...]) for v in h]
    for (i, r), v in zip(grp, p):
        o = 0
        for ref, width in ((u_ref, S5_WIDTH), (q_ref, GLA_KEY), (k_ref, GLA_KEY), (v_ref, GLA_WIDTH), (gl_ref, LANES)):
            ref[i, r, :] = v[:, o:o + width].astype(ref.dtype)
            o += width


def _mod_spec(l, cond_row, part):
    return pl.BlockSpec((None, None, 1, D_MODEL), lambda b, j: (l, cond_row(b), 0, part))


def _layer_spec(l, shape):
    return pl.BlockSpec((None,) + shape, lambda b, j: (l,) + (0,) * len(shape))


def _proj_call(x, pe, mod, l, cond_row, ng, w, tm, nsb=1):
    nseq, length, _ = x.shape
    row = lambda width: pl.BlockSpec((nsb, tm, width), lambda b, j: (b, j, 0))
    in_specs = [row(D_MODEL)]
    args = [x]
    if pe is not None:
        in_specs.append(pl.BlockSpec((tm, D_MODEL), lambda b, j: (j, 0)))
        args.append(pe)
    in_specs += [_mod_spec(l, cond_row, 0), _mod_spec(l, cond_row, 1), _layer_spec(l, (1, D_MODEL)),
                 _layer_spec(l, (PROJ_W, D_MODEL))]
    args += [mod, mod, ng, w]
    sds = lambda width, dt: jax.ShapeDtypeStruct((nseq, length, width), dt)
    return pl.pallas_call(
        functools.partial(_proj_kernel, has_pe=pe is not None),
        grid=(nseq // nsb, length // tm),
        in_specs=in_specs,
        out_specs=(row(S5_WIDTH), row(GLA_KEY), row(GLA_KEY), row(GLA_WIDTH), row(LANES)),
        out_shape=(sds(S5_WIDTH, F32), sds(GLA_KEY, F32), sds(GLA_KEY, F32), sds(GLA_WIDTH, BF16), sds(LANES, F32)),
        compiler_params=pltpu.CompilerParams(dimension_semantics=("arbitrary", "arbitrary"),
                                             vmem_limit_bytes=VMEM_LIMIT),
        name="proj",
    )(*args)


def _s5_perms(nb, tc):
    r = nb * tc
    pf = np.zeros((2 * r, r), np.float32)
    pb = np.zeros((2 * r, r), np.float32)
    for t in range(tc):
        for b in range(nb):
            pf[t * 2 * nb + b, b * tc + t] = 1.0
            pb[t * 2 * nb + nb + b, b * tc + (tc - 1 - t)] = 1.0
    return pf, pb


def _s5_store_unpacked(y2, yf_ref, yb_ref, t_f, t_b, nb, tc):
    rs = 2 * nb
    for t in range(tc):
        yf_ref[:, t_f + t, :] = y2[t * rs:t * rs + nb]
        yb_ref[:, t_b + tc - 1 - t, :] = y2[t * rs + nb:(t + 1) * rs]


def _s5_kernel(*refs, nb, tc, tiles_per_pass):
    use_perm = nb % 8 != 0
    uf1_ref, ub1_ref, uf2_ref, ub2_ref, uf0_ref, ub0_ref = refs[:6]
    pf_ref, pb_ref = refs[6:8] if use_perm else (None, None)
    wbu_ref, wc_ref, are_ref, aim_ref, h0_ref, yf_ref, yb_ref, hfin_ref, bua_ref, bub_ref, hb_ref, st_ref = refs[-12:]
    i = pl.program_id(0)
    rows = nb * tc
    rs = 2 * nb

    def packed_inputs(uf_ref, ub_ref):
        if not use_perm:
            return tuple(jnp.concatenate([r[:, t, :] for t in range(tc)], axis=0).astype(BF16) for r in (uf_ref, ub_ref))
        uf = uf_ref[...].reshape(rows, S5_WIDTH).astype(BF16)
        ub = ub_ref[...].reshape(rows, S5_WIDTH).astype(BF16)
        u2f = jnp.dot(pf_ref[...], uf, preferred_element_type=F32).astype(BF16)
        u2b = jnp.dot(pb_ref[...], ub, preferred_element_type=F32).astype(BF16)
        return [jnp.concatenate([u2f[:, s * LANES:(s + 1) * LANES], u2b[:, s * LANES:(s + 1) * LANES]], axis=1)
                for s in range(S5_SLABS)]

    def bu_slab(lhs, s, dst_ref):
        cols = slice(s * S5_SLAB_COLS, (s + 1) * S5_SLAB_COLS)
        if use_perm:
            dst_ref[:, cols] = jnp.dot(lhs[s], wbu_ref[s], preferred_element_type=F32)
            return
        uf, ub = lhs
        bf = jnp.dot(uf[:, s * LANES:(s + 1) * LANES], wbu_ref[s, :LANES, :], preferred_element_type=F32)
        bb = jnp.dot(ub[:, s * LANES:(s + 1) * LANES], wbu_ref[s, LANES:, :], preferred_element_type=F32)
        for t in range(tc):
            dst_ref[t * rs:t * rs + nb, cols] = bf[t * nb:(t + 1) * nb]
            dst_ref[t * rs + nb:(t + 1) * rs, cols] = bb[(tc - 1 - t) * nb:(tc - t) * nb]

    @pl.when(i == 0)
    def _():
        st_ref[...] = h0_ref[...]
        lhs0 = packed_inputs(uf0_ref, ub0_ref)
        for s in range(S5_SLABS):
            bu_slab(lhs0, s, bua_ref)

    half_step = functools.partial(_s5_half_step, wc_ref=wc_ref, are_ref=are_ref, aim_ref=aim_ref, hb_ref=hb_ref,
                                  st_ref=st_ref, bu_slab=bu_slab, rs=rs, tc=tc, tiles_per_pass=tiles_per_pass)
    y2 = half_step(bua_ref, bub_ref, packed_inputs(uf1_ref, ub1_ref))
    _s5_store_unpacked(y2, yf_ref, yb_ref, 0, tc, nb, tc)
    y2 = half_step(bub_ref, bua_ref, packed_inputs(uf2_ref, ub2_ref))
    _s5_store_unpacked(y2, yf_ref, yb_ref, tc, 0, nb, tc)
    hfin_ref[...] = st_ref[...]


def _s5_half_step(cur_ref, nxt_ref, lhs_next, *, wc_ref, are_ref, aim_ref, hb_ref, st_ref, bu_slab,
                  rs, tc, tiles_per_pass):
    slab_tiles = S5_SLAB_NC // LANES
    grp = max(1, 16 // rs)
    ys = []
    for s in range(S5_SLABS):
        bu_slab(lhs_next, s, nxt_ref)
        for c0 in range(slab_tiles * s, slab_tiles * (s + 1), tiles_per_pass):
            cts = list(range(c0, c0 + tiles_per_pass))
            cre = [(ct // slab_tiles) * S5_SLAB_COLS + (ct % slab_tiles) * LANES for ct in cts]
            cim = [c + S5_SLAB_NC for c in cre]
            a_re = [are_ref[:, ct * LANES:(ct + 1) * LANES] for ct in cts]
            a_im = [aim_ref[:, ct * LANES:(ct + 1) * LANES] for ct in cts]
            h_re = [st_ref[:, c:c + LANES] for c in cre]
            h_im = [st_ref[:, c:c + LANES] for c in cim]
            for t0 in range(0, tc, grp):
                out_re = [[] for _ in cts]
                out_im = [[] for _ in cts]
                for t in range(t0, t0 + grp):
                    r0 = t * rs
                    for j in range(len(cts)):
                        b_re = cur_ref[r0:r0 + rs, cre[j]:cre[j] + LANES]
                        b_im = cur_ref[r0:r0 + rs, cim[j]:cim[j] + LANES]
                        n_re = a_re[j] * h_re[j] - a_im[j] * h_im[j] + b_re
                        n_im = a_re[j] * h_im[j] + a_im[j] * h_re[j] + b_im
                        h_re[j], h_im[j] = n_re, n_im
                        out_re[j].append(n_re)
                        out_im[j].append(n_im)
                for j in range(len(cts)):
                    blk_re = out_re[j][0] if grp == 1 else jnp.concatenate(out_re[j], axis=0)
                    blk_im = out_im[j][0] if grp == 1 else jnp.concatenate(out_im[j], axis=0)
                    hb_ref[t0 * rs:(t0 + grp) * rs, cre[j]:cre[j] + LANES] = blk_re.astype(BF16)
                    hb_ref[t0 * rs:(t0 + grp) * rs, cim[j]:cim[j] + LANES] = blk_im.astype(BF16)
            for j in range(len(cts)):
                st_ref[:, cre[j]:cre[j] + LANES] = h_re[j]
                st_ref[:, cim[j]:cim[j] + LANES] = h_im[j]
        ys.append(jnp.dot(hb_ref[:, s * S5_SLAB_COLS:(s + 1) * S5_SLAB_COLS], wc_ref[s], preferred_element_type=F32))
    return jnp.concatenate(ys, axis=1)


def _s5_call(u, l, w_bu, w_c, are, aim, h0, tc, tiles_per_pass):
    nb, length, _ = u.shape
    n = length // tc
    rows = nb * tc
    const = lambda shape: pl.BlockSpec(shape, lambda i: (0,) * len(shape))
    layer = lambda shape: pl.BlockSpec((None,) + shape, lambda i: (l,) + (0,) * len(shape))
    blk = lambda chunk: pl.BlockSpec((nb, tc, S5_WIDTH), lambda i: (0, chunk(i), 0))
    out_f = pl.BlockSpec((nb, 2 * tc, S5_WIDTH), lambda i: (0, i, 0))
    out_b = pl.BlockSpec((nb, 2 * tc, S5_WIDTH), lambda i: (0, n // 2 - 1 - i, 0))
    second = lambda i: jnp.minimum(2 * i + 2, n - 1)
    in_specs = [blk(lambda i: 2 * i + 1), blk(lambda i: n - 2 - 2 * i), blk(second), blk(lambda i: n - 1 - second(i)),
                blk(lambda i: 0), blk(lambda i: n - 1)]
    args = [u] * 6
    if nb % 8 != 0:
        pf, pb = _s5_perms(nb, tc)
        in_specs += [const((2 * rows, rows)), const((2 * rows, rows))]
        args += [jnp.asarray(pf, BF16), jnp.asarray(pb, BF16)]
    in_specs += [layer((S5_SLABS, 2 * LANES, S5_SLAB_COLS)), layer((S5_SLABS, S5_SLAB_COLS, LANES)),
                 layer((2 * nb, S5_NC)), layer((2 * nb, S5_NC)), const((2 * nb, 2 * S5_NC))]
    args += [w_bu, w_c, are, aim, h0]
    return pl.pallas_call(
        functools.partial(_s5_kernel, nb=nb, tc=tc, tiles_per_pass=tiles_per_pass),
        grid=(n // 2,),
        in_specs=in_specs,
        out_specs=(out_f, out_b, const((2 * nb, 2 * S5_NC))),
        out_shape=(jax.ShapeDtypeStruct(u.shape, F32), jax.ShapeDtypeStruct(u.shape, F32),
                   jax.ShapeDtypeStruct((2 * nb, 2 * S5_NC), F32)),
        scratch_shapes=[pltpu.VMEM((2 * rows, 2 * S5_NC), F32), pltpu.VMEM((2 * rows, 2 * S5_NC), F32),
                        pltpu.VMEM((2 * rows, 2 * S5_NC), BF16), pltpu.VMEM((2 * nb, 2 * S5_NC), F32)],
        compiler_params=pltpu.CompilerParams(dimension_semantics=("arbitrary",), vmem_limit_bytes=VMEM_LIMIT),
        name="s5_scan",
    )(*args)


def _gla_kernel(*refs, length, has_s0, n_prev, nsq, cpi):
    q_ref, k_ref, v_ref, gl_ref, wg_ref, bg_ref = refs[:6]
    s0_ref = refs[6] if has_s0 else None
    prev_refs = refs[6 + has_s0:6 + has_s0 + n_prev]
    o_ref, sfin_ref, sf_ref, sb_ref = refs[-4:]
    c = GLA_CHUNK
    n = length // c
    ri = lax.broadcasted_iota(jnp.int32, (c, c), 0)
    ci = lax.broadcasted_iota(jnp.int32, (c, c), 1)
    lane = lax.broadcasted_iota(jnp.int32, (1, GLA_KEY), 1)
    head_mask = [(lane >= h * GLA_DK) & (lane < (h + 1) * GLA_DK) for h in range(GLA_HEADS)]
    causal = [ci <= ri, ci >= ri]
    tri = [m.astype(BF16) for m in causal]
    end = (c - 1, 0)
    chains = [(sq, d, j) for j in range(cpi) for sq in range(nsq) for d in (0, 1)]
    key_rows = lambda h: slice(h * GLA_DK, (h + 1) * GLA_DK)
    val_cols = lambda h: slice(h * GLA_DV, (h + 1) * GLA_DV)

    sb_ref[...] = jnp.zeros_like(sb_ref)
    for sq in range(nsq):
        for d in (0, 1):
            for h in range(GLA_HEADS):
                if has_s0:
                    sf_ref[sq, d, h] = s0_ref[sq, d, h]
                    sb_ref[sq, d, key_rows(h), val_cols(h)] = s0_ref[sq, d, h].astype(BF16)
                else:
                    sf_ref[sq, d, h] = jnp.zeros((GLA_DK, GLA_DV), F32)

    def chunk_group(r):
        row = lambda d, j: r[d][j]
        qc = [q_ref[sq, pl.ds(row(d, j), c), :] * (GLA_DK ** -0.5) for sq, d, j in chains]
        kc = [k_ref[sq, pl.ds(row(d, j), c), :] for sq, d, j in chains]
        vc = [v_ref[sq, pl.ds(row(d, j), c), :] for sq, d, j in chains]
        logits = [_bdot(gl_ref[sq, pl.ds(row(d, j), c), :], wg_ref[d]) + bg_ref[d] for sq, d, j in chains]
        g = [_split2(_log_sigmoid(x) * (1.0 / GLA_GATE_NORM)) for x in logits]
        b = [jnp.dot(tri[d], hi, preferred_element_type=F32) + jnp.dot(tri[d], lo, preferred_element_type=F32)
             for (sq, d, j), (hi, lo) in zip(chains, g)]
        bt = [x.T for x in b]
        kt = [x.T for x in kc]
        b_mid = [x[c // 2:c // 2 + 1, :] for x in b]
        bt_mid = [x[:, c // 2:c // 2 + 1] for x in bt]
        bt_end = [x[:, end[d]:end[d] + 1] for (sq, d, j), x in zip(chains, bt)]
        q_in = [(x * jnp.exp(y)).astype(BF16) for x, y in zip(qc, b)]
        q_mid = [(x * jnp.exp(y - m)).astype(BF16) for x, y, m in zip(qc, b, b_mid)]
        k_mid_t = [(x * jnp.exp(m - y)).astype(BF16) for x, y, m in zip(kt, bt, bt_mid)]
        k_end_t = [(x * jnp.exp(e - y)).astype(BF16) for x, y, e in zip(kt, bt, bt_end)]
        q_st = [jnp.concatenate([jnp.where(head_mask[h], x, jnp.zeros_like(x)) for h in range(GLA_HEADS)], axis=0)
                for x in q_mid]
        a = [jnp.dot(x, y, preferred_element_type=F32) for x, y in zip(q_st, k_mid_t)]
        parts = [[] for _ in chains]
        for h in range(GLA_HEADS):
            for i, (sq, d, j) in enumerate(chains):
                a_h = jnp.where(causal[d], a[i][h * c:(h + 1) * c, :], 0.0).astype(BF16)
                parts[i].append(jnp.dot(a_h, vc[i][:, val_cols(h)], preferred_element_type=F32))
        upd = [[jnp.dot(k_end_t[i][key_rows(h), :], vc[i][:, val_cols(h)], preferred_element_type=F32)
                for h in range(GLA_HEADS)] for i in range(len(chains))]
        o_inter = [None] * len(chains)
        for i, (sq, d, j) in enumerate(chains):
            o_inter[i] = jnp.dot(q_in[i], sb_ref[sq, d], preferred_element_type=F32)
            for h in range(GLA_HEADS):
                s_new = jnp.exp(bt_end[i][key_rows(h), :]) * sf_ref[sq, d, h] + upd[i][h]
                sf_ref[sq, d, h] = s_new
                sb_ref[sq, d, key_rows(h), val_cols(h)] = s_new.astype(BF16)
        return [x + jnp.concatenate(p, axis=1) for x, p in zip(o_inter, parts)]

    def rows_of(step):
        fwd = [pl.multiple_of((step * cpi + j) * c, c) for j in range(cpi)]
        bwd = [pl.multiple_of((n - 1 - step * cpi - j) * c, c) for j in range(cpi)]
        return fwd, bwd

    def first_half(step, carry):
        r = rows_of(step)
        for (sq, d, j), o in zip(chains, chunk_group(r)):
            o_ref[sq, pl.ds(r[d][j], c), :] = o
        return carry

    def second_half(step, carry):
        r = rows_of(step)
        for (sq, d, j), o in zip(chains, chunk_group(r)):
            o_ref[sq, pl.ds(r[d][j], c), :] = o + o_ref[sq, pl.ds(r[d][j], c), :]
        return carry

    steps = n // cpi
    lax.fori_loop(0, steps // 2, first_half, 0)
    lax.fori_loop(steps // 2, steps, second_half, 0)
    if n_prev:
        for j, prev_ref in enumerate(prev_refs):
            sfin_ref[:, j] = prev_ref[...]
        sfin_ref[:, n_prev] = sf_ref[...]
    else:
        sfin_ref[...] = sf_ref[...]


def _gla_call(q, k, v, gl, l, wg, bg, s0, nsq, cpi, prev_states=()):
    nseq, length, _ = q.shape
    assert (length // GLA_CHUNK) % (2 * cpi) == 0 and nseq % nsq == 0
    seq = lambda width: pl.BlockSpec((nsq, length, width), lambda b: (b, 0, 0))
    layer = lambda shape: pl.BlockSpec((None,) + shape, lambda b: (l,) + (0,) * len(shape))
    in_specs = [seq(GLA_KEY), seq(GLA_KEY), seq(GLA_WIDTH), seq(LANES),
                layer((2, LANES, GLA_KEY)), layer((2, 1, GLA_KEY))]
    args = [q, k, v, gl, wg, bg]
    if s0 is not None:
        in_specs.append(pl.BlockSpec((nsq, None, 2, GLA_HEADS, GLA_DK, GLA_DV), lambda b: (b, l, 0, 0, 0, 0)))
        args.append(s0)
    state = (2, GLA_HEADS, GLA_DK, GLA_DV)
    state_spec = pl.BlockSpec((nsq,) + state, lambda b: (b, 0, 0, 0, 0))
    in_specs += [state_spec] * len(prev_states)
    args += list(prev_states)
    lead = (len(prev_states) + 1,) if prev_states else ()
    return pl.pallas_call(
        functools.partial(_gla_kernel, length=length, has_s0=s0 is not None, n_prev=len(prev_states), nsq=nsq, cpi=cpi),
        grid=(nseq // nsq,),
        in_specs=in_specs,
        out_specs=(seq(GLA_WIDTH), pl.BlockSpec((nsq,) + lead + state, lambda b: (b,) + (0,) * len(lead + state))),
        out_shape=(jax.ShapeDtypeStruct((nseq, length, GLA_WIDTH), F32),
                   jax.ShapeDtypeStruct((nseq,) + lead + state, F32)),
        scratch_shapes=[pltpu.VMEM((nsq, 2, GLA_HEADS, GLA_DK, GLA_DV), F32),
                        pltpu.VMEM((nsq, 2, GLA_KEY, GLA_WIDTH), BF16)],
        compiler_params=pltpu.CompilerParams(dimension_semantics=("arbitrary",), vmem_limit_bytes=VMEM_LIMIT),
        name="gla",
    )(*args)


def _mix_kernel(*refs, has_pe, last, row_groups):
    if has_pe:
        (x_ref, pe_ref, yf_ref, yb_ref, u_ref, o_ref, shift_ref, scale_ref, gate_ref, ng_ref, wg_ref, d_ref,
         wglu_ref, bglu_ref, wpa_ref, wpb_ref, wo_ref, fng_ref, gng_ref, out_ref) = refs
    else:
        (x_ref, yf_ref, yb_ref, u_ref, o_ref, shift_ref, scale_ref, gate_ref, ng_ref, wg_ref, d_ref,
         wglu_ref, bglu_ref, wpa_ref, wpb_ref, wo_ref, fng_ref, gng_ref, out_ref) = refs

    def head_norm(o):
        heads = [o[:, h * GLA_DV:(h + 1) * GLA_DV] for h in range(GLA_HEADS)]
        heads = [v * lax.rsqrt(jnp.mean(v * v, axis=-1, keepdims=True) + EPS) for v in heads]
        return jnp.concatenate(heads, axis=1) * gng_ref[...]

    nsb, tm, _ = x_ref.shape
    grp = [(i, slice(j * tm // row_groups, (j + 1) * tm // row_groups)) for i in range(nsb) for j in range(row_groups)]
    x = [x_ref[i, r, :] + pe_ref[r, :] if has_pe else x_ref[i, r, :] for i, r in grp]
    h = [_modulated_norm(v, ng_ref[...], 1.0 + scale_ref[...], shift_ref[...]).astype(BF16) for v in x]
    gts = [_bdot_t(v, wg_ref[...]) for v in h]
    y = [_gelu_tanh(yf_ref[i, r, :] + yb_ref[i, r, :] + d_ref[...] * u_ref[i, r, :]) for i, r in grp]
    glu = [_bdot(v, wglu_ref[...]) for v in y]
    y = [v * _sigmoid(g + bglu_ref[...]) for v, g in zip(y, glu)]
    y_a = [v * _silu(g[:, :S5_WIDTH]) for v, g in zip(y, gts)]
    y_b = [head_norm(o_ref[i, r, :]) * _silu(g[:, S5_WIDTH:2 * S5_WIDTH]) for (i, r), g in zip(grp, gts)]
    p_a = [_bdot(v, wpa_ref[...]) for v in y_a]
    p_b = [_bdot(v, wpb_ref[...]) for v in y_b]
    merged = [_sigmoid(g[:, 2 * S5_WIDTH:2 * S5_WIDTH + D_MODEL]) * a + _sigmoid(g[:, 2 * S5_WIDTH + D_MODEL:]) * b
              for g, a, b in zip(gts, p_a, p_b)]
    xn = [v + gate_ref[...] * _bdot(m, wo_ref[...]) for v, m in zip(x, merged)]
    for (i, r), v in zip(grp, xn):
        if last:
            ms = jnp.mean(v * v, axis=-1, keepdims=True)
            v = v * lax.rsqrt(ms + EPS) * fng_ref[...]
        out_ref[i, r, :] = v


def _mix_call(x, pe, yf, yb, u, o, mod, l, cond_row, ng, wg, s5_d, wglu, bglu, wpa, wpb, wo, fng, gng, tm, last,
              nsb=1):
    nseq, length, _ = x.shape
    row = lambda width: pl.BlockSpec((nsb, tm, width), lambda b, j: (b, j, 0))
    in_specs = [row(D_MODEL)]
    args = [x]
    if pe is not None:
        in_specs.append(pl.BlockSpec((tm, D_MODEL), lambda b, j: (j, 0)))
        args.append(pe)
    in_specs += [row(S5_WIDTH), row(S5_WIDTH), row(S5_WIDTH), row(GLA_WIDTH),
                 _mod_spec(l, cond_row, 0), _mod_spec(l, cond_row, 1), _mod_spec(l, cond_row, 2),
                 _layer_spec(l, (1, D_MODEL)), _layer_spec(l, (GATE_W, D_MODEL)), _layer_spec(l, (1, S5_WIDTH)),
                 _layer_spec(l, (S5_WIDTH, S5_WIDTH)), _layer_spec(l, (1, S5_WIDTH)),
                 _layer_spec(l, (S5_WIDTH, D_MODEL)), _layer_spec(l, (GLA_WIDTH, D_MODEL)),
                 _layer_spec(l, (D_MODEL, D_MODEL)), pl.BlockSpec((1, D_MODEL), lambda b, j: (0, 0)),
                 _layer_spec(l, (1, GLA_WIDTH))]
    args += [yf, yb, u, o, mod, mod, mod, ng, wg, s5_d, wglu, bglu, wpa, wpb, wo, fng, gng]
    return pl.pallas_call(
        functools.partial(_mix_kernel, has_pe=pe is not None, last=last, row_groups=max(1, tm // MXU_TILE)),
        grid=(nseq // nsb, length // tm),
        in_specs=in_specs,
        out_specs=row(D_MODEL),
        out_shape=jax.ShapeDtypeStruct(x.shape, F32),
        compiler_params=pltpu.CompilerParams(dimension_semantics=("arbitrary", "arbitrary"),
                                             vmem_limit_bytes=VMEM_LIMIT),
        name="mix",
    )(*args)


def _grid_pos_embed(length, dim):
    rows = length // GRID_W
    quarter = dim // 4
    freqs = jnp.exp(-math.log(10000.0) * jnp.arange(quarter, dtype=F32) / quarter)

    def sincos(pos):
        ang = pos.astype(F32)[:, None] * freqs[None, :]
        return jnp.concatenate([jnp.sin(ang), jnp.cos(ang)], axis=-1)

    er = sincos(jnp.arange(rows))
    ec = sincos(jnp.arange(GRID_W))
    pe = jnp.concatenate([jnp.broadcast_to(er[:, None, :], (rows, GRID_W, dim // 2)),
                          jnp.broadcast_to(ec[None, :, :], (rows, GRID_W, dim // 2))], axis=-1)
    return pe.reshape(rows * GRID_W, dim)


def kernel(x_prompt, x_sample, c, state_s5_re, state_s5_im, state_gla, c_ctx, norm_g, w_mod, b_mod, w_in, gla_wg_up,
           gla_bg, gla_norm_g, s5_lam_re, s5_lam_im, s5_log_dt, s5_b_re, s5_b_im, s5_c_re, s5_c_im, s5_d, w_glu,
           b_glu, w_pa, w_pb, w_o, final_norm_g):
    depth = w_in.shape[0]
    bp, lp, _ = x_prompt.shape
    bs, ls, _ = x_sample.shape

    cond8 = jnp.concatenate([c_ctx[None], c, jnp.zeros((8 - 1 - bs, D_MODEL), F32)], axis=0)
    mod = _mod_call(cond8, w_mod, b_mod)

    a_re, a_im, w_bu, w_c = _s5_params(s5_lam_re, s5_lam_im, s5_log_dt, s5_b_re, s5_b_im, s5_c_re, s5_c_im)

    w_proj, w_gate = _split_w_in(w_in)
    wg_up = jnp.concatenate([gla_wg_up, jnp.zeros((depth, 2, LANES - GLA_RANK, GLA_KEY), F32)], axis=2).astype(BF16)
    w_glu_b, w_pa_b, w_pb_b, w_o_b = (w.astype(BF16) for w in (w_glu, w_pa, w_pb, w_o))

    vec = lambda a: a[:, None, :]
    mod = mod.reshape(depth, 8, 1, 3 * D_MODEL)
    ng, gng, bg = vec(norm_g), vec(gla_norm_g), gla_bg[:, :, None, :]
    common = (ng, w_gate, vec(s5_d), w_glu_b, vec(b_glu), w_pa_b, w_pb_b, w_o_b, final_norm_g[None], gng)
    are_p, aim_p = jnp.repeat(a_re, bp, axis=1), jnp.repeat(a_im, bp, axis=1)
    are_s, aim_s = jnp.repeat(a_re, bs, axis=1), jnp.repeat(a_im, bs, axis=1)
    ctx_row = lambda b: 0
    cond_row = lambda b: b + 1

    pe = _grid_pos_embed(ls, D_MODEL)
    zero_h0 = jnp.zeros((2 * bp, 2 * S5_NC), F32)
    xp, xs = x_prompt, x_sample
    new_re, new_im, new_gla = [], [], []
    for l in range(depth):
        last = l == depth - 1

        u, q, k, v, gl = _proj_call(xp, None, mod, l, ctx_row, ng, w_proj, tm=lp, nsb=4)
        yf, yb, hfin = _s5_call(u, l, w_bu, w_c, are_p, aim_p, zero_h0, tc=16, tiles_per_pass=2)
        o, sfin = _gla_call(q, k, v, gl, l, wg_up, bg, None, nsq=4, cpi=1, prev_states=new_gla if last else ())
        new_gla.append(sfin)
        xp = _mix_call(xp, None, yf, yb, u, o, mod, l, ctx_row, *common, tm=lp, last=last, nsb=2)
        re_l, im_l = _s5_cols_to_state(hfin.reshape(2, bp, 2 * S5_NC))
        new_re.append(jnp.swapaxes(re_l, 0, 1))
        new_im.append(jnp.swapaxes(im_l, 0, 1))

        pe_l = pe if l == 0 else None
        u, q, k, v, gl = _proj_call(xs, pe_l, mod, l, cond_row, ng, w_proj, tm=1024)
        h0 = _s5_state_to_cols(jnp.swapaxes(state_s5_re[:, l], 0, 1), jnp.swapaxes(state_s5_im[:, l], 0, 1))
        yf, yb, _ = _s5_call(u, l, w_bu, w_c, are_s, aim_s, h0.reshape(2 * bs, 2 * S5_NC), tc=32, tiles_per_pass=4)
        o, _ = _gla_call(q, k, v, gl, l, wg_up, bg, state_gla, nsq=1, cpi=4)
        xs = _mix_call(xs, pe_l, yf, yb, u, o, mod, l, cond_row, *common, tm=512, last=last)

    new_gla = new_gla[-1] if depth > 1 else new_gla[0][:, None]
    return (xp, xs, jnp.stack(new_re, axis=1), jnp.stack(new_im, axis=1), new_gla)
```

```python
import functools
import math

import numpy as np
import jax
import jax.numpy as jnp
from jax import lax
from jax.experimental import pallas as pl
from jax.experimental.pallas import tpu as pltpu

F32 = jnp.float32
BF16 = jnp.bfloat16

D_MODEL = 1024
GRID_W = 64
S5_WIDTH = 512
S5_GROUPS = 32
S5_CH = 16
S5_STATE = 64
S5_NC = S5_GROUPS * S5_STATE
S5_SLAB_GROUPS = 8
S5_SLABS = S5_GROUPS // S5_SLAB_GROUPS
S5_SLAB_NC = S5_SLAB_GROUPS * S5_STATE
S5_SLAB_COLS = 2 * S5_SLAB_NC
GLA_HEADS = 4
GLA_DK = 64
GLA_DV = 128
GLA_KEY = GLA_HEADS * GLA_DK
GLA_WIDTH = GLA_HEADS * GLA_DV
GLA_RANK = 16
GLA_GATE_NORM = 16.0
GLA_CHUNK = 128
EPS = 1e-6
LANES = 128
PROJ_W = S5_WIDTH + 2 * GLA_KEY + GLA_WIDTH + LANES
GATE_W = 2 * S5_WIDTH + 2 * D_MODEL
VMEM_LIMIT = 60 * 1024 * 1024
MXU_TILE = 256


def _bdot(a, b):
    return jnp.dot(a.astype(BF16), b.astype(BF16), preferred_element_type=F32)


def _bdot_t(a, bt):
    return lax.dot_general(a.astype(BF16), bt.astype(BF16), (((1,), (1,)), ((), ())), preferred_element_type=F32)


def _split2(x):
    hi = x.astype(BF16)
    lo = (x - hi.astype(F32)).astype(BF16)
    return hi, lo


def _sigmoid(x):
    return 1.0 / (1.0 + jnp.exp2(x * (-math.log2(math.e))))


def _silu(x):
    return x * _sigmoid(x)


def _gelu_tanh(x):
    c = math.sqrt(2.0 / math.pi)
    return 0.5 * x * (1.0 + jnp.tanh(c * (x + 0.044715 * (x * x * x))))


def _log_sigmoid(x):
    return jnp.minimum(x, 0.0) - jnp.log(1.0 + jnp.exp(-jnp.abs(x)))


def _modulated_norm(x, ng, scale1p, shift):
    ms = jnp.mean(x * x, axis=-1, keepdims=True)
    return (x * lax.rsqrt(ms + EPS) * ng) * scale1p + shift


def _mod_kernel(cond_ref, w_ref, b_ref, out_ref):
    c = cond_ref[...]
    s_hi, s_lo = _split2(_silu(c))
    w_hi, w_lo = _split2(w_ref[0])
    acc = jnp.dot(s_hi, w_hi, preferred_element_type=F32)
    acc += jnp.dot(s_lo, w_hi, preferred_element_type=F32)
    acc += jnp.dot(s_hi, w_lo, preferred_element_type=F32)
    out_ref[0] = acc + b_ref[0]


def _mod_call(cond8, w_mod, b_mod):
    depth = w_mod.shape[0]
    nb = 1536
    return pl.pallas_call(
        _mod_kernel,
        grid=(depth, 3 * D_MODEL // nb),
        in_specs=[
            pl.BlockSpec((8, D_MODEL), lambda l, j: (0, 0)),
            pl.BlockSpec((1, D_MODEL, nb), lambda l, j: (l, 0, j)),
            pl.BlockSpec((1, 1, nb), lambda l, j: (l, 0, j)),
        ],
        out_specs=pl.BlockSpec((1, 8, nb), lambda l, j: (l, 0, j)),
        out_shape=jax.ShapeDtypeStruct((depth, 8, 3 * D_MODEL), F32),
        compiler_params=pltpu.CompilerParams(dimension_semantics=("arbitrary", "arbitrary")),
        name="adaln_mod",
    )(cond8, w_mod, b_mod.reshape(depth, 1, 3 * D_MODEL))


def _zoh(lre, lim, ldt):
    dt = jnp.exp(ldt)
    mag = jnp.exp(lre * dt)
    a_re = mag * jnp.cos(lim * dt)
    a_im = mag * jnp.sin(lim * dt)
    n_re = a_re - 1.0
    inv = 1.0 / (lre * lre + lim * lim)
    return a_re, a_im, (n_re * lre + a_im * lim) * inv, (a_im * lre - n_re * lim) * inv


def _zoh_kernel(lre_ref, lim_ref, ldt_ref, bre_ref, bim_ref, cre_ref, cim_ref, are_ref, aim_ref, wbu_ref, wc_ref,
                *, depth):
    a_re, a_im, c_re, c_im = _zoh(lre_ref[...], lim_ref[...], ldt_ref[...])
    are_ref[...] = a_re
    aim_ref[...] = a_im
    cmat = [cre_ref[...], -cim_ref[...]]
    wbu_ref[...] = jnp.zeros_like(wbu_ref)
    wc_ref[...] = jnp.zeros_like(wc_ref)
    for l in range(depth):
        for s in range(S5_SLABS):
            for gl in range(S5_SLAB_GROUPS):
                g = S5_SLAB_GROUPS * s + gl
                src = (l * S5_GROUPS + g) * S5_CH
                bre = bre_ref[src:src + S5_CH, :]
                bim = bim_ref[src:src + S5_CH, :]
                for d in range(2):
                    r = (l * 2 + d) * S5_GROUPS + g
                    k_re, k_im = c_re[r:r + 1, :], c_im[r:r + 1, :]
                    bbar = [k_re * bre - k_im * bim, k_re * bim + k_im * bre]
                    row = (d * S5_SLAB_GROUPS + gl) * S5_CH
                    for ri in range(2):
                        col = ri * S5_SLAB_NC + gl * S5_STATE
                        wbu_ref[l, s, row:row + S5_CH, col:col + S5_STATE] = bbar[ri].astype(BF16)
                src = (l * S5_GROUPS + g) * S5_STATE
                for ri in range(2):
                    col = ri * S5_SLAB_NC + gl * S5_STATE
                    wc_ref[l, s, col:col + S5_STATE, gl * S5_CH:(gl + 1) * S5_CH] = (
                        cmat[ri][src:src + S5_STATE, :].astype(BF16))


def _s5_params(s5_lam_re, s5_lam_im, s5_log_dt, s5_b_re, s5_b_im, s5_c_re, s5_c_im):
    depth = s5_lam_re.shape[0]
    small = lambda a: a.reshape(depth * 2 * S5_GROUPS, -1)
    swapped = lambda a: jnp.swapaxes(a, -1, -2).reshape(-1, a.shape[-2])
    a_re, a_im, w_bu, w_c = pl.pallas_call(
        functools.partial(_zoh_kernel, depth=depth),
        out_shape=(jax.ShapeDtypeStruct((depth * 2 * S5_GROUPS, S5_STATE), F32),
                   jax.ShapeDtypeStruct((depth * 2 * S5_GROUPS, S5_STATE), F32),
                   jax.ShapeDtypeStruct((depth, S5_SLABS, 2 * LANES, S5_SLAB_COLS), BF16),
                   jax.ShapeDtypeStruct((depth, S5_SLABS, S5_SLAB_COLS, LANES), BF16)),
        name="s5_zoh",
    )(small(s5_lam_re), small(s5_lam_im), small(s5_log_dt), swapped(s5_b_re), swapped(s5_b_im),
      swapped(s5_c_re), swapped(s5_c_im))
    return a_re.reshape(depth, 2, S5_NC), a_im.reshape(depth, 2, S5_NC), w_bu, w_c


def _s5_state_to_cols(re, im):
    lead = re.shape[:-2]
    st = jnp.stack([re.reshape(lead + (S5_SLABS, S5_SLAB_NC)), im.reshape(lead + (S5_SLABS, S5_SLAB_NC))], axis=-2)
    return st.reshape(lead + (2 * S5_NC,))


def _s5_cols_to_state(cols):
    lead = cols.shape[:-1]
    st = cols.reshape(lead + (S5_SLABS, 2, S5_SLAB_NC))
    re = st[..., 0, :].reshape(lead + (S5_GROUPS, S5_STATE))
    im = st[..., 1, :].reshape(lead + (S5_GROUPS, S5_STATE))
    return re, im


_IN_OFFS = [int(o) for o in np.cumsum([0, S5_WIDTH, S5_WIDTH, GLA_KEY, GLA_KEY, GLA_WIDTH, GLA_WIDTH, GLA_RANK,
                                       D_MODEL, D_MODEL])]


def _split_w_in_kernel(u_ref, qkv_ref, code_ref, ga_ref, gb_ref, m_ref, proj_ref, gate_ref):
    rows = 0
    for src in (u_ref, qkv_ref, code_ref):
        proj_ref[rows:rows + src.shape[1], :] = src[0].astype(BF16)
        rows += src.shape[1]
    proj_ref[rows:, :] = jnp.zeros((PROJ_W - rows, proj_ref.shape[1]), BF16)
    rows = 0
    for src in (ga_ref, gb_ref, m_ref):
        gate_ref[rows:rows + src.shape[1], :] = src[0].astype(BF16)
        rows += src.shape[1]


def _split_w_in(w_in):
    o = _IN_OFFS
    depth = w_in.shape[0]
    wt = jnp.swapaxes(w_in, 1, 2)
    tn = D_MODEL // 2
    rows = lambda lo, hi: pl.BlockSpec((pl.Element(1), pl.Element(hi - lo), pl.Element(tn)),
                                       lambda l, j: (l, lo, j * tn))
    out = lambda n: pl.BlockSpec((None, n, tn), lambda l, j: (l, 0, j))
    return pl.pallas_call(
        _split_w_in_kernel,
        grid=(depth, D_MODEL // tn),
        in_specs=[rows(o[0], o[1]), rows(o[2], o[5]), rows(o[6], o[7]), rows(o[1], o[2]), rows(o[5], o[6]),
                  rows(o[7], o[9])],
        out_specs=(out(PROJ_W), out(GATE_W)),
        out_shape=(jax.ShapeDtypeStruct((depth, PROJ_W, D_MODEL), BF16),
                   jax.ShapeDtypeStruct((depth, GATE_W, D_MODEL), BF16)),
        compiler_params=pltpu.CompilerParams(dimension_semantics=("arbitrary", "arbitrary"),
                                             vmem_limit_bytes=VMEM_LIMIT),
        name="split_w_in",
    )(wt, wt, wt, wt, wt, wt)


def _proj_kernel(*refs, has_pe):
    if has_pe:
        x_ref, pe_ref, shift_ref, scale_ref, ng_ref, w_ref, u_ref, q_ref, k_ref, v_ref, gl_ref = refs
    else:
        x_ref, shift_ref, scale_ref, ng_ref, w_ref, u_ref, q_ref, k_ref, v_ref, gl_ref = refs
    nsb, tm, _ = x_ref.shape
    rg = max(1, tm // (2 * MXU_TILE))
    grp = [(i, slice(j * tm // rg, (j + 1) * tm // rg)) for i in range(nsb) for j in range(rg)]
    x = [x_ref[i, r, :] + pe_ref[r, :] if has_pe else x_ref[i, r, :] for i, r in grp]
    h = [_modulated_norm(v, ng_ref[...], 1.0 + scale_ref[...], shift_ref[...]) for v in x]
    p = [_bdot_t(v, w_ref[...]) for v in h]
    for (i, r), v in zip(grp, p):
        o = 0
        for ref, width in ((u_ref, S5_WIDTH), (q_ref, GLA_KEY), (k_ref, GLA_KEY), (v_ref, GLA_WIDTH), (gl_ref, LANES)):
            ref[i, r, :] = v[:, o:o + width].astype(ref.dtype)
            o += width


def _mod_spec(l, cond_row, part):
    return pl.BlockSpec((None, None, 1, D_MODEL), lambda b, j: (l, cond_row(b), 0, part))


def _layer_spec(l, shape):
    return pl.BlockSpec((None,) + shape, lambda b, j: (l,) + (0,) * len(shape))


def _proj_call(x, pe, mod, l, cond_row, ng, w, tm, nsb=1):
    nseq, length, _ = x.shape
    row = lambda width: pl.BlockSpec((nsb, tm, width), lambda b, j: (b, j, 0))
    in_specs = [row(D_MODEL)]
    args = [x]
    if pe is not None:
        in_specs.append(pl.BlockSpec((tm, D_MODEL), lambda b, j: (j, 0)))
        args.append(pe)
    in_specs += [_mod_spec(l, cond_row, 0), _mod_spec(l, cond_row, 1), _layer_spec(l, (1, D_MODEL)),
                 _layer_spec(l, (PROJ_W, D_MODEL))]
    args += [mod, mod, ng, w]
    sds = lambda width, dt: jax.ShapeDtypeStruct((nseq, length, width), dt)
    return pl.pallas_call(
        functools.partial(_proj_kernel, has_pe=pe is not None),
        grid=(nseq // nsb, length // tm),
        in_specs=in_specs,
        out_specs=(row(S5_WIDTH), row(GLA_KEY), row(GLA_KEY), row(GLA_WIDTH), row(LANES)),
        out_shape=(sds(S5_WIDTH, F32), sds(GLA_KEY, F32), sds(GLA_KEY, F32), sds(GLA_WIDTH, BF16), sds(LANES, F32)),
        compiler_params=pltpu.CompilerParams(dimension_semantics=("arbitrary", "arbitrary"),
                                             vmem_limit_bytes=VMEM_LIMIT),
        name="proj",
    )(*args)


def _s5_perms(nb, tc):
    r = nb * tc
    pf = np.zeros((2 * r, r), np.float32)
    pb = np.zeros((2 * r, r), np.float32)
    for t in range(tc):
        for b in range(nb):
            pf[t * 2 * nb + b, b * tc + t] = 1.0
            pb[t * 2 * nb + nb + b, b * tc + (tc - 1 - t)] = 1.0
    return pf, pb


def _s5_store_unpacked(y2, yf_ref, yb_ref, t_f, t_b, nb, tc):
    rs = 2 * nb
    for t in range(tc):
        yf_ref[:, t_f + t, :] = y2[t * rs:t * rs + nb]
        yb_ref[:, t_b + tc - 1 - t, :] = y2[t * rs + nb:(t + 1) * rs]


def _s5_kernel(*refs, nb, tc, tiles_per_pass):
    use_perm = nb % 8 != 0
    uf1_ref, ub1_ref, uf2_ref, ub2_ref, uf0_ref, ub0_ref = refs[:6]
    pf_ref, pb_ref = refs[6:8] if use_perm else (None, None)
    wbu_ref, wc_ref, are_ref, aim_ref, h0_ref, yf_ref, yb_ref, hfin_ref, bua_ref, bub_ref, hb_ref, st_ref = refs[-12:]
    i = pl.program_id(0)
    rows = nb * tc
    rs = 2 * nb

    def packed_inputs(uf_ref, ub_ref):
        if not use_perm:
            return tuple(jnp.concatenate([r[:, t, :] for t in range(tc)], axis=0).astype(BF16) for r in (uf_ref, ub_ref))
        uf = uf_ref[...].reshape(rows, S5_WIDTH).astype(BF16)
        ub = ub_ref[...].reshape(rows, S5_WIDTH).astype(BF16)
        u2f = jnp.dot(pf_ref[...], uf, preferred_element_type=F32).astype(BF16)
        u2b = jnp.dot(pb_ref[...], ub, preferred_element_type=F32).astype(BF16)
        return [jnp.concatenate([u2f[:, s * LANES:(s + 1) * LANES], u2b[:, s * LANES:(s + 1) * LANES]], axis=1)
                for s in range(S5_SLABS)]

    def bu_slab(lhs, s, dst_ref):
        cols = slice(s * S5_SLAB_COLS, (s + 1) * S5_SLAB_COLS)
        if use_perm:
            dst_ref[:, cols] = jnp.dot(lhs[s], wbu_ref[s], preferred_element_type=F32)
            return
        uf, ub = lhs
        bf = jnp.dot(uf[:, s * LANES:(s + 1) * LANES], wbu_ref[s, :LANES, :], preferred_element_type=F32)
        bb = jnp.dot(ub[:, s * LANES:(s + 1) * LANES], wbu_ref[s, LANES:, :], preferred_element_type=F32)
        for t in range(tc):
            dst_ref[t * rs:t * rs + nb, cols] = bf[t * nb:(t + 1) * nb]
            dst_ref[t * rs + nb:(t + 1) * rs, cols] = bb[(tc - 1 - t) * nb:(tc - t) * nb]

    @pl.when(i == 0)
    def _():
        st_ref[...] = h0_ref[...]
        lhs0 = packed_inputs(uf0_ref, ub0_ref)
        for s in range(S5_SLABS):
            bu_slab(lhs0, s, bua_ref)

    half_step = functools.partial(_s5_half_step, wc_ref=wc_ref, are_ref=are_ref, aim_ref=aim_ref, hb_ref=hb_ref,
                                  st_ref=st_ref, bu_slab=bu_slab, rs=rs, tc=tc, tiles_per_pass=tiles_per_pass)
    y2 = half_step(bua_ref, bub_ref, packed_inputs(uf1_ref, ub1_ref))
    _s5_store_unpacked(y2, yf_ref, yb_ref, 0, tc, nb, tc)
    y2 = half_step(bub_ref, bua_ref, packed_inputs(uf2_ref, ub2_ref))
    _s5_store_unpacked(y2, yf_ref, yb_ref, tc, 0, nb, tc)
    hfin_ref[...] = st_ref[...]


def _s5_half_step(cur_ref, nxt_ref, lhs_next, *, wc_ref, are_ref, aim_ref, hb_ref, st_ref, bu_slab,
                  rs, tc, tiles_per_pass):
    slab_tiles = S5_SLAB_NC // LANES
    grp = max(1, 16 // rs)
    ys = []
    for s in range(S5_SLABS):
        bu_slab(lhs_next, s, nxt_ref)
        for c0 in range(slab_tiles * s, slab_tiles * (s + 1), tiles_per_pass):
            cts = list(range(c0, c0 + tiles_per_pass))
            cre = [(ct // slab_tiles) * S5_SLAB_COLS + (ct % slab_tiles) * LANES for ct in cts]
            cim = [c + S5_SLAB_NC for c in cre]
            a_re = [are_ref[:, ct * LANES:(ct + 1) * LANES] for ct in cts]
            a_im = [aim_ref[:, ct * LANES:(ct + 1) * LANES] for ct in cts]
            h_re = [st_ref[:, c:c + LANES] for c in cre]
            h_im = [st_ref[:, c:c + LANES] for c in cim]
            for t0 in range(0, tc, grp):
                out_re = [[] for _ in cts]
                out_im = [[] for _ in cts]
                for t in range(t0, t0 + grp):
                    r0 = t * rs
                    for j in range(len(cts)):
                        b_re = cur_ref[r0:r0 + rs, cre[j]:cre[j] + LANES]
                        b_im = cur_ref[r0:r0 + rs, cim[j]:cim[j] + LANES]
                        n_re = a_re[j] * h_re[j] - a_im[j] * h_im[j] + b_re
                        n_im = a_re[j] * h_im[j] + a_im[j] * h_re[j] + b_im
                        h_re[j], h_im[j] = n_re, n_im
                        out_re[j].append(n_re)
                        out_im[j].append(n_im)
                for j in range(len(cts)):
                    blk_re = out_re[j][0] if grp == 1 else jnp.concatenate(out_re[j], axis=0)
                    blk_im = out_im[j][0] if grp == 1 else jnp.concatenate(out_im[j], axis=0)
                    hb_ref[t0 * rs:(t0 + grp) * rs, cre[j]:cre[j] + LANES] = blk_re.astype(BF16)
                    hb_ref[t0 * rs:(t0 + grp) * rs, cim[j]:cim[j] + LANES] = blk_im.astype(BF16)
            for j in range(len(cts)):
                st_ref[:, cre[j]:cre[j] + LANES] = h_re[j]
                st_ref[:, cim[j]:cim[j] + LANES] = h_im[j]
        ys.append(jnp.dot(hb_ref[:, s * S5_SLAB_COLS:(s + 1) * S5_SLAB_COLS], wc_ref[s], preferred_element_type=F32))
    return jnp.concatenate(ys, axis=1)


def _s5_call(u, l, w_bu, w_c, are, aim, h0, tc, tiles_per_pass):
    nb, length, _ = u.shape
    n = length // tc
    rows = nb * tc
    const = lambda shape: pl.BlockSpec(shape, lambda i: (0,) * len(shape))
    layer = lambda shape: pl.BlockSpec((None,) + shape, lambda i: (l,) + (0,) * len(shape))
    blk = lambda chunk: pl.BlockSpec((nb, tc, S5_WIDTH), lambda i: (0, chunk(i), 0))
    out_f = pl.BlockSpec((nb, 2 * tc, S5_WIDTH), lambda i: (0, i, 0))
    out_b = pl.BlockSpec((nb, 2 * tc, S5_WIDTH), lambda i: (0, n // 2 - 1 - i, 0))
    second = lambda i: jnp.minimum(2 * i + 2, n - 1)
    in_specs = [blk(lambda i: 2 * i + 1), blk(lambda i: n - 2 - 2 * i), blk(second), blk(lambda i: n - 1 - second(i)),
                blk(lambda i: 0), blk(lambda i: n - 1)]
    args = [u] * 6
    if nb % 8 != 0:
        pf, pb = _s5_perms(nb, tc)
        in_specs += [const((2 * rows, rows)), const((2 * rows, rows))]
        args += [jnp.asarray(pf, BF16), jnp.asarray(pb, BF16)]
    in_specs += [layer((S5_SLABS, 2 * LANES, S5_SLAB_COLS)), layer((S5_SLABS, S5_SLAB_COLS, LANES)),
                 layer((2 * nb, S5_NC)), layer((2 * nb, S5_NC)), const((2 * nb, 2 * S5_NC))]
    args += [w_bu, w_c, are, aim, h0]
    return pl.pallas_call(
        functools.partial(_s5_kernel, nb=nb, tc=tc, tiles_per_pass=tiles_per_pass),
        grid=(n // 2,),
        in_specs=in_specs,
        out_specs=(out_f, out_b, const((2 * nb, 2 * S5_NC))),
        out_shape=(jax.ShapeDtypeStruct(u.shape, F32), jax.ShapeDtypeStruct(u.shape, F32),
                   jax.ShapeDtypeStruct((2 * nb, 2 * S5_NC), F32)),
        scratch_shapes=[pltpu.VMEM((2 * rows, 2 * S5_NC), F32), pltpu.VMEM((2 * rows, 2 * S5_NC), F32),
                        pltpu.VMEM((2 * rows, 2 * S5_NC), BF16), pltpu.VMEM((2 * nb, 2 * S5_NC), F32)],
        compiler_params=pltpu.CompilerParams(dimension_semantics=("arbitrary",), vmem_limit_bytes=VMEM_LIMIT),
        name="s5_scan",
    )(*args)


def _gla_kernel(*refs, length, has_s0, n_prev, nsq, cpi):
    q_ref, k_ref, v_ref, gl_ref, wg_ref, bg_ref = refs[:6]
    s0_ref = refs[6] if has_s0 else None
    prev_refs = refs[6 + has_s0:6 + has_s0 + n_prev]
    o_ref, sfin_ref, sf_ref, sb_ref = refs[-4:]
    c = GLA_CHUNK
    n = length // c
    ri = lax.broadcasted_iota(jnp.int32, (c, c), 0)
    ci = lax.broadcasted_iota(jnp.int32, (c, c), 1)
    lane = lax.broadcasted_iota(jnp.int32, (1, GLA_KEY), 1)
    head_mask = [(lane >= h * GLA_DK) & (lane < (h + 1) * GLA_DK) for h in range(GLA_HEADS)]
    causal = [ci <= ri, ci >= ri]
    tri = [m.astype(BF16) for m in causal]
    end = (c - 1, 0)
    chains = [(sq, d, j) for j in range(cpi) for sq in range(nsq) for d in (0, 1)]
    key_rows = lambda h: slice(h * GLA_DK, (h + 1) * GLA_DK)
    val_cols = lambda h: slice(h * GLA_DV, (h + 1) * GLA_DV)

    sb_ref[...] = jnp.zeros_like(sb_ref)
    for sq in range(nsq):
        for d in (0, 1):
            for h in range(GLA_HEADS):
                if has_s0:
                    sf_ref[sq, d, h] = s0_ref[sq, d, h]
                    sb_ref[sq, d, key_rows(h), val_cols(h)] = s0_ref[sq, d, h].astype(BF16)
                else:
                    sf_ref[sq, d, h] = jnp.zeros((GLA_DK, GLA_DV), F32)

    def chunk_group(r):
        row = lambda d, j: r[d][j]
        qc = [q_ref[sq, pl.ds(row(d, j), c), :] * (GLA_DK ** -0.5) for sq, d, j in chains]
        kc = [k_ref[sq, pl.ds(row(d, j), c), :] for sq, d, j in chains]
        vc = [v_ref[sq, pl.ds(row(d, j), c), :] for sq, d, j in chains]
        logits = [_bdot(gl_ref[sq, pl.ds(row(d, j), c), :], wg_ref[d]) + bg_ref[d] for sq, d, j in chains]
        g = [_split2(_log_sigmoid(x) * (1.0 / GLA_GATE_NORM)) for x in logits]
        b = [jnp.dot(tri[d], hi, preferred_element_type=F32) + jnp.dot(tri[d], lo, preferred_element_type=F32)
             for (sq, d, j), (hi, lo) in zip(chains, g)]
        bt = [x.T for x in b]
        kt = [x.T for x in kc]
        b_mid = [x[c // 2:c // 2 + 1, :] for x in b]
        bt_mid = [x[:, c // 2:c // 2 + 1] for x in bt]
        bt_end = [x[:, end[d]:end[d] + 1] for (sq, d, j), x in zip(chains, bt)]
        q_in = [(x * jnp.exp(y)).astype(BF16) for x, y in zip(qc, b)]
        q_mid = [(x * jnp.exp(y - m)).astype(BF16) for x, y, m in zip(qc, b, b_mid)]
        k_mid_t = [(x * jnp.exp(m - y)).astype(BF16) for x, y, m in zip(kt, bt, bt_mid)]
        k_end_t = [(x * jnp.exp(e - y)).astype(BF16) for x, y, e in zip(kt, bt, bt_end)]
        q_st = [jnp.concatenate([jnp.where(head_mask[h], x, jnp.zeros_like(x)) for h in range(GLA_HEADS)], axis=0)
                for x in q_mid]
        a = [jnp.dot(x, y, preferred_element_type=F32) for x, y in zip(q_st, k_mid_t)]
        parts = [[] for _ in chains]
        for h in range(GLA_HEADS):
            for i, (sq, d, j) in enumerate(chains):
                a_h = jnp.where(causal[d], a[i][h * c:(h + 1) * c, :], 0.0).astype(BF16)
                parts[i].append(jnp.dot(a_h, vc[i][:, val_cols(h)], preferred_element_type=F32))
        upd = [[jnp.dot(k_end_t[i][key_rows(h), :], vc[i][:, val_cols(h)], preferred_element_type=F32)
                for h in range(GLA_HEADS)] for i in range(len(chains))]
        o_inter = [None] * len(chains)
        for i, (sq, d, j) in enumerate(chains):
            o_inter[i] = jnp.dot(q_in[i], sb_ref[sq, d], preferred_element_type=F32)
            for h in range(GLA_HEADS):
                s_new = jnp.exp(bt_end[i][key_rows(h), :]) * sf_ref[sq, d, h] + upd[i][h]
                sf_ref[sq, d, h] = s_new
                sb_ref[sq, d, key_rows(h), val_cols(h)] = s_new.astype(BF16)
        return [x + jnp.concatenate(p, axis=1) for x, p in zip(o_inter, parts)]

    def rows_of(step):
        fwd = [pl.multiple_of((step * cpi + j) * c, c) for j in range(cpi)]
        bwd = [pl.multiple_of((n - 1 - step * cpi - j) * c, c) for j in range(cpi)]
        return fwd, bwd

    def first_half(step, carry):
        r = rows_of(step)
        for (sq, d, j), o in zip(chains, chunk_group(r)):
            o_ref[sq, pl.ds(r[d][j], c), :] = o
        return carry

    def second_half(step, carry):
        r = rows_of(step)
        for (sq, d, j), o in zip(chains, chunk_group(r)):
            o_ref[sq, pl.ds(r[d][j], c), :] = o + o_ref[sq, pl.ds(r[d][j], c), :]
        return carry

    steps = n // cpi
    lax.fori_loop(0, steps // 2, first_half, 0)
    lax.fori_loop(steps // 2, steps, second_half, 0)
    if n_prev:
        for j, prev_ref in enumerate(prev_refs):
            sfin_ref[:, j] = prev_ref[...]
        sfin_ref[:, n_prev] = sf_ref[...]
    else:
        sfin_ref[...] = sf_ref[...]


def _gla_call(q, k, v, gl, l, wg, bg, s0, nsq, cpi, prev_states=()):
    nseq, length, _ = q.shape
    assert (length // GLA_CHUNK) % (2 * cpi) == 0 and nseq % nsq == 0
    seq = lambda width: pl.BlockSpec((nsq, length, width), lambda b: (b, 0, 0))
    layer = lambda shape: pl.BlockSpec((None,) + shape, lambda b: (l,) + (0,) * len(shape))
    in_specs = [seq(GLA_KEY), seq(GLA_KEY), seq(GLA_WIDTH), seq(LANES),
                layer((2, LANES, GLA_KEY)), layer((2, 1, GLA_KEY))]
    args = [q, k, v, gl, wg, bg]
    if s0 is not None:
        in_specs.append(pl.BlockSpec((nsq, None, 2, GLA_HEADS, GLA_DK, GLA_DV), lambda b: (b, l, 0, 0, 0, 0)))
        args.append(s0)
    state = (2, GLA_HEADS, GLA_DK, GLA_DV)
    state_spec = pl.BlockSpec((nsq,) + state, lambda b: (b, 0, 0, 0, 0))
    in_specs += [state_spec] * len(prev_states)
    args += list(prev_states)
    lead = (len(prev_states) + 1,) if prev_states else ()
    return pl.pallas_call(
        functools.partial(_gla_kernel, length=length, has_s0=s0 is not None, n_prev=len(prev_states), nsq=nsq, cpi=cpi),
        grid=(nseq // nsq,),
        in_specs=in_specs,
        out_specs=(seq(GLA_WIDTH), pl.BlockSpec((nsq,) + lead + state, lambda b: (b,) + (0,) * len(lead + state))),
        out_shape=(jax.ShapeDtypeStruct((nseq, length, GLA_WIDTH), F32),
                   jax.ShapeDtypeStruct((nseq,) + lead + state, F32)),
        scratch_shapes=[pltpu.VMEM((nsq, 2, GLA_HEADS, GLA_DK, GLA_DV), F32),
                        pltpu.VMEM((nsq, 2, GLA_KEY, GLA_WIDTH), BF16)],
        compiler_params=pltpu.CompilerParams(dimension_semantics=("arbitrary",), vmem_limit_bytes=VMEM_LIMIT),
        name="gla",
    )(*args)


def _mix_kernel(*refs, has_pe, last, row_groups):
    if has_pe:
        (x_ref, pe_ref, yf_ref, yb_ref, u_ref, o_ref, shift_ref, scale_ref, gate_ref, ng_ref, wg_ref, d_ref,
         wglu_ref, bglu_ref, wpa_ref, wpb_ref, wo_ref, fng_ref, gng_ref, out_ref) = refs
    else:
        (x_ref, yf_ref, yb_ref, u_ref, o_ref, shift_ref, scale_ref, gate_ref, ng_ref, wg_ref, d_ref,
         wglu_ref, bglu_ref, wpa_ref, wpb_ref, wo_ref, fng_ref, gng_ref, out_ref) = refs

    def head_norm(o):
        heads = [o[:, h * GLA_DV:(h + 1) * GLA_DV] for h in range(GLA_HEADS)]
        heads = [v * lax.rsqrt(jnp.mean(v * v, axis=-1, keepdims=True) + EPS) for v in heads]
        return jnp.concatenate(heads, axis=1) * gng_ref[...]

    nsb, tm, _ = x_ref.shape
    grp = [(i, slice(j * tm // row_groups, (j + 1) * tm // row_groups)) for i in range(nsb) for j in range(row_groups)]
    x = [x_ref[i, r, :] + pe_ref[r, :] if has_pe else x_ref[i, r, :] for i, r in grp]
    h = [_modulated_norm(v, ng_ref[...], 1.0 + scale_ref[...], shift_ref[...]).astype(BF16) for v in x]
    gts = [_bdot_t(v, wg_ref[...]) for v in h]
    y = [_gelu_tanh(yf_ref[i, r, :] + yb_ref[i, r, :] + d_ref[...] * u_ref[i, r, :]) for i, r in grp]
    glu = [_bdot(v, wglu_ref[...]) for v in y]
    y = [v * _sigmoid(g + bglu_ref[...]) for v, g in zip(y, glu)]
    y_a = [v * _silu(g[:, :S5_WIDTH]) for v, g in zip(y, gts)]
    y_b = [head_norm(o_ref[i, r, :]) * _silu(g[:, S5_WIDTH:2 * S5_WIDTH]) for (i, r), g in zip(grp, gts)]
    p_a = [_bdot(v, wpa_ref[...]) for v in y_a]
    p_b = [_bdot(v, wpb_ref[...]) for v in y_b]
    merged = [_sigmoid(g[:, 2 * S5_WIDTH:2 * S5_WIDTH + D_MODEL]) * a + _sigmoid(g[:, 2 * S5_WIDTH + D_MODEL:]) * b
              for g, a, b in zip(gts, p_a, p_b)]
    xn = [v + gate_ref[...] * _bdot(m, wo_ref[...]) for v, m in zip(x, merged)]
    for (i, r), v in zip(grp, xn):
        if last:
            ms = jnp.mean(v * v, axis=-1, keepdims=True)
            v = v * lax.rsqrt(ms + EPS) * fng_ref[...]
        out_ref[i, r, :] = v


def _mix_call(x, pe, yf, yb, u, o, mod, l, cond_row, ng, wg, s5_d, wglu, bglu, wpa, wpb, wo, fng, gng, tm, last,
              nsb=1):
    nseq, length, _ = x.shape
    row = lambda width: pl.BlockSpec((nsb, tm, width), lambda b, j: (b, j, 0))
    in_specs = [row(D_MODEL)]
    args = [x]
    if pe is not None:
        in_specs.append(pl.BlockSpec((tm, D_MODEL), lambda b, j: (j, 0)))
        args.append(pe)
    in_specs += [row(S5_WIDTH), row(S5_WIDTH), row(S5_WIDTH), row(GLA_WIDTH),
                 _mod_spec(l, cond_row, 0), _mod_spec(l, cond_row, 1), _mod_spec(l, cond_row, 2),
                 _layer_spec(l, (1, D_MODEL)), _layer_spec(l, (GATE_W, D_MODEL)), _layer_spec(l, (1, S5_WIDTH)),
                 _layer_spec(l, (S5_WIDTH, S5_WIDTH)), _layer_spec(l, (1, S5_WIDTH)),
                 _layer_spec(l, (S5_WIDTH, D_MODEL)), _layer_spec(l, (GLA_WIDTH, D_MODEL)),
                 _layer_spec(l, (D_MODEL, D_MODEL)), pl.BlockSpec((1, D_MODEL), lambda b, j: (0, 0)),
                 _layer_spec(l, (1, GLA_WIDTH))]
    args += [yf, yb, u, o, mod, mod, mod, ng, wg, s5_d, wglu, bglu, wpa, wpb, wo, fng, gng]
    return pl.pallas_call(
        functools.partial(_mix_kernel, has_pe=pe is not None, last=last, row_groups=max(1, tm // MXU_TILE)),
        grid=(nseq // nsb, length // tm),
        in_specs=in_specs,
        out_specs=row(D_MODEL),
        out_shape=jax.ShapeDtypeStruct(x.shape, F32),
        compiler_params=pltpu.CompilerParams(dimension_semantics=("arbitrary", "arbitrary"),
                                             vmem_limit_bytes=VMEM_LIMIT),
        name="mix",
    )(*args)


_CONTEXT_BLOCKS = dict(proj=dict(tm=None, nsb=4), s5=dict(tc=16, tiles_per_pass=2), gla=dict(nsq=4, cpi=1),
                       mix=dict(tm=None, nsb=2))
_LATENT_BLOCKS = dict(proj=dict(tm=1024), s5=dict(tc=32, tiles_per_pass=4), gla=dict(nsq=1, cpi=4), mix=dict(tm=512))


def _blocks(choice, length):
    return {k: dict(v, tm=v["tm"] or length) if "tm" in v else v for k, v in choice.items()}


def _grid_pos_embed(length, dim):
    rows = length // GRID_W
    quarter = dim // 4
    freqs = jnp.exp(-math.log(10000.0) * jnp.arange(quarter, dtype=F32) / quarter)

    def sincos(pos):
        ang = pos.astype(F32)[:, None] * freqs[None, :]
        return jnp.concatenate([jnp.sin(ang), jnp.cos(ang)], axis=-1)

    er = sincos(jnp.arange(rows))
    ec = sincos(jnp.arange(GRID_W))
    pe = jnp.concatenate([jnp.broadcast_to(er[:, None, :], (rows, GRID_W, dim // 2)),
                          jnp.broadcast_to(ec[None, :, :], (rows, GRID_W, dim // 2))], axis=-1)
    return pe.reshape(rows * GRID_W, dim)


def kernel(x_prompt, x_sample, c, state_s5_re, state_s5_im, state_gla, c_ctx, norm_g, w_mod, b_mod, w_in, gla_wg_up,
           gla_bg, gla_norm_g, s5_lam_re, s5_lam_im, s5_log_dt, s5_b_re, s5_b_im, s5_c_re, s5_c_im, s5_d, w_glu,
           b_glu, w_pa, w_pb, w_o, final_norm_g):
    depth = w_in.shape[0]
    bp, lp, _ = x_prompt.shape
    bs, ls, _ = x_sample.shape

    cond8 = jnp.concatenate([c_ctx[None], c, jnp.zeros((8 - 1 - bs, D_MODEL), F32)], axis=0)
    mod = _mod_call(cond8, w_mod, b_mod)

    a_re, a_im, w_bu, w_c = _s5_params(s5_lam_re, s5_lam_im, s5_log_dt, s5_b_re, s5_b_im, s5_c_re, s5_c_im)

    w_proj, w_gate = _split_w_in(w_in)
    wg_up = jnp.concatenate([gla_wg_up, jnp.zeros((depth, 2, LANES - GLA_RANK, GLA_KEY), F32)], axis=2).astype(BF16)
    w_glu_b, w_pa_b, w_pb_b, w_o_b = (w.astype(BF16) for w in (w_glu, w_pa, w_pb, w_o))

    vec = lambda a: a[:, None, :]
    mod = mod.reshape(depth, 8, 1, 3 * D_MODEL)
    ng, gng, bg = vec(norm_g), vec(gla_norm_g), gla_bg[:, :, None, :]
    common = (ng, w_gate, vec(s5_d), w_glu_b, vec(b_glu), w_pa_b, w_pb_b, w_o_b, final_norm_g[None], gng)
    are_p, aim_p = jnp.repeat(a_re, bp, axis=1), jnp.repeat(a_im, bp, axis=1)
    are_s, aim_s = jnp.repeat(a_re, bs, axis=1), jnp.repeat(a_im, bs, axis=1)
    ctx_row = lambda b: 0
    cond_row = lambda b: b + 1

    pe = _grid_pos_embed(ls, D_MODEL)
    blk_p, blk_s = _blocks(_CONTEXT_BLOCKS, lp), _blocks(_LATENT_BLOCKS, ls)
    zero_h0 = jnp.zeros((2 * bp, 2 * S5_NC), F32)
    xp, xs = x_prompt, x_sample
    new_re, new_im, new_gla = [], [], []
    for l in range(depth):
        last = l == depth - 1

        u, q, k, v, gl = _proj_call(xp, None, mod, l, ctx_row, ng, w_proj, **blk_p["proj"])
        yf, yb, hfin = _s5_call(u, l, w_bu, w_c, are_p, aim_p, zero_h0, **blk_p["s5"])
        o, sfin = _gla_call(q, k, v, gl, l, wg_up, bg, None, prev_states=new_gla if last else (), **blk_p["gla"])
        new_gla.append(sfin)
        xp = _mix_call(xp, None, yf, yb, u, o, mod, l, ctx_row, *common, last=last, **blk_p["mix"])
        re_l, im_l = _s5_cols_to_state(hfin.reshape(2, bp, 2 * S5_NC))
        new_re.append(jnp.swapaxes(re_l, 0, 1))
        new_im.append(jnp.swapaxes(im_l, 0, 1))

        pe_l = pe if l == 0 else None
        u, q, k, v, gl = _proj_call(xs, pe_l, mod, l, cond_row, ng, w_proj, **blk_s["proj"])
        h0 = _s5_state_to_cols(jnp.swapaxes(state_s5_re[:, l], 0, 1), jnp.swapaxes(state_s5_im[:, l], 0, 1))
        yf, yb, _ = _s5_call(u, l, w_bu, w_c, are_s, aim_s, h0.reshape(2 * bs, 2 * S5_NC), **blk_s["s5"])
        o, _ = _gla_call(q, k, v, gl, l, wg_up, bg, state_gla, **blk_s["gla"])
        xs = _mix_call(xs, pe_l, yf, yb, u, o, mod, l, cond_row, *common, last=last, **blk_s["mix"])

    new_gla = new_gla[-1] if depth > 1 else new_gla[0][:, None]
    return (xp, xs, jnp.stack(new_re, axis=1), jnp.stack(new_im, axis=1), new_gla)
```

```python
import functools
import math

import numpy as np
import jax
import jax.numpy as jnp
from jax import lax
from jax.experimental import pallas as pl
from jax.experimental.pallas import tpu as pltpu

F32 = jnp.float32
BF16 = jnp.bfloat16

D_MODEL = 1024
GRID_W = 64
S5_WIDTH = 512
S5_GROUPS = 32
S5_CH = 16
S5_STATE = 64
S5_NC = S5_GROUPS * S5_STATE
S5_SLAB_GROUPS = 8
S5_SLABS = S5_GROUPS // S5_SLAB_GROUPS
S5_SLAB_NC = S5_SLAB_GROUPS * S5_STATE
S5_SLAB_COLS = 2 * S5_SLAB_NC
GLA_HEADS = 4
GLA_DK = 64
GLA_DV = 128
GLA_KEY = GLA_HEADS * GLA_DK
GLA_WIDTH = GLA_HEADS * GLA_DV
GLA_RANK = 16
GLA_GATE_NORM = 16.0
GLA_CHUNK = 128
EPS = 1e-6
LANES = 128
PROJ_W = S5_WIDTH + 2 * GLA_KEY + GLA_WIDTH + LANES
GATE_W = 2 * S5_WIDTH + 2 * D_MODEL
VMEM_LIMIT = 60 * 1024 * 1024
MXU_TILE = 256


def _bdot(a, b):
    return jnp.dot(a.astype(BF16), b.astype(BF16), preferred_element_type=F32)


def _bdot_t(a, bt):
    return lax.dot_general(a.astype(BF16), bt.astype(BF16), (((1,), (1,)), ((), ())), preferred_element_type=F32)


def _split2(x):
    hi = x.astype(BF16)
    lo = (x - hi.astype(F32)).astype(BF16)
    return hi, lo


def _sigmoid(x):
    return 1.0 / (1.0 + jnp.exp2(x * (-math.log2(math.e))))


def _silu(x):
    return x * _sigmoid(x)


def _gelu_tanh(x):
    c = math.sqrt(2.0 / math.pi)
    return 0.5 * x * (1.0 + jnp.tanh(c * (x + 0.044715 * (x * x * x))))


def _log_sigmoid(x):
    return jnp.minimum(x, 0.0) - jnp.log(1.0 + jnp.exp(-jnp.abs(x)))


def _modulated_norm(x, ng, scale1p, shift):
    ms = jnp.mean(x * x, axis=-1, keepdims=True)
    return (x * lax.rsqrt(ms + EPS) * ng) * scale1p + shift


def _mod_kernel(cond_ref, w_ref, b_ref, out_ref):
    c = cond_ref[...]
    s_hi, s_lo = _split2(_silu(c))
    w_hi, w_lo = _split2(w_ref[0])
    acc = jnp.dot(s_hi, w_hi, preferred_element_type=F32)
    acc += jnp.dot(s_lo, w_hi, preferred_element_type=F32)
    acc += jnp.dot(s_hi, w_lo, preferred_element_type=F32)
    out_ref[0] = acc + b_ref[0]


def _mod_call(cond8, w_mod, b_mod):
    depth = w_mod.shape[0]
    nb = 1536
    return pl.pallas_call(
        _mod_kernel,
        grid=(depth, 3 * D_MODEL // nb),
        in_specs=[
            pl.BlockSpec((8, D_MODEL), lambda l, j: (0, 0)),
            pl.BlockSpec((1, D_MODEL, nb), lambda l, j: (l, 0, j)),
            pl.BlockSpec((1, 1, nb), lambda l, j: (l, 0, j)),
        ],
        out_specs=pl.BlockSpec((1, 8, nb), lambda l, j: (l, 0, j)),
        out_shape=jax.ShapeDtypeStruct((depth, 8, 3 * D_MODEL), F32),
        compiler_params=pltpu.CompilerParams(dimension_semantics=("arbitrary", "arbitrary")),
        name="adaln_mod",
    )(cond8, w_mod, b_mod.reshape(depth, 1, 3 * D_MODEL))


def _zoh(lre, lim, ldt):
    dt = jnp.exp(ldt)
    mag = jnp.exp(lre * dt)
    a_re = mag * jnp.cos(lim * dt)
    a_im = mag * jnp.sin(lim * dt)
    n_re = a_re - 1.0
    inv = 1.0 / (lre * lre + lim * lim)
    return a_re, a_im, (n_re * lre + a_im * lim) * inv, (a_im * lre - n_re * lim) * inv


def _zoh_kernel(lre_ref, lim_ref, ldt_ref, bre_ref, bim_ref, cre_ref, cim_ref, are_ref, aim_ref, wbu_ref, wc_ref,
                *, depth):
    a_re, a_im, c_re, c_im = _zoh(lre_ref[...], lim_ref[...], ldt_ref[...])
    are_ref[...] = a_re
    aim_ref[...] = a_im
    cmat = [cre_ref[...], -cim_ref[...]]
    wbu_ref[...] = jnp.zeros_like(wbu_ref)
    wc_ref[...] = jnp.zeros_like(wc_ref)
    for l in range(depth):
        for s in range(S5_SLABS):
            for gl in range(S5_SLAB_GROUPS):
                g = S5_SLAB_GROUPS * s + gl
                src = (l * S5_GROUPS + g) * S5_CH
                bre = bre_ref[src:src + S5_CH, :]
                bim = bim_ref[src:src + S5_CH, :]
                for d in range(2):
                    r = (l * 2 + d) * S5_GROUPS + g
                    k_re, k_im = c_re[r:r + 1, :], c_im[r:r + 1, :]
                    bbar = [k_re * bre - k_im * bim, k_re * bim + k_im * bre]
                    row = (d * S5_SLAB_GROUPS + gl) * S5_CH
                    for ri in range(2):
                        col = ri * S5_SLAB_NC + gl * S5_STATE
                        wbu_ref[l, s, row:row + S5_CH, col:col + S5_STATE] = bbar[ri].astype(BF16)
                src = (l * S5_GROUPS + g) * S5_STATE
                for ri in range(2):
                    col = ri * S5_SLAB_NC + gl * S5_STATE
                    wc_ref[l, s, col:col + S5_STATE, gl * S5_CH:(gl + 1) * S5_CH] = (
                        cmat[ri][src:src + S5_STATE, :].astype(BF16))


def _s5_params(s5_lam_re, s5_lam_im, s5_log_dt, s5_b_re, s5_b_im, s5_c_re, s5_c_im):
    depth = s5_lam_re.shape[0]
    small = lambda a: a.reshape(depth * 2 * S5_GROUPS, -1)
    swapped = lambda a: jnp.swapaxes(a, -1, -2).reshape(-1, a.shape[-2])
    a_re, a_im, w_bu, w_c = pl.pallas_call(
        functools.partial(_zoh_kernel, depth=depth),
        out_shape=(jax.ShapeDtypeStruct((depth * 2 * S5_GROUPS, S5_STATE), F32),
                   jax.ShapeDtypeStruct((depth * 2 * S5_GROUPS, S5_STATE), F32),
                   jax.ShapeDtypeStruct((depth, S5_SLABS, 2 * LANES, S5_SLAB_COLS), BF16),
                   jax.ShapeDtypeStruct((depth, S5_SLABS, S5_SLAB_COLS, LANES), BF16)),
        name="s5_zoh",
    )(small(s5_lam_re), small(s5_lam_im), small(s5_log_dt), swapped(s5_b_re), swapped(s5_b_im),
      swapped(s5_c_re), swapped(s5_c_im))
    return a_re.reshape(depth, 2, S5_NC), a_im.reshape(depth, 2, S5_NC), w_bu, w_c


def _s5_state_to_cols(re, im):
    lead = re.shape[:-2]
    st = jnp.stack([re.reshape(lead + (S5_SLABS, S5_SLAB_NC)), im.reshape(lead + (S5_SLABS, S5_SLAB_NC))], axis=-2)
    return st.reshape(lead + (2 * S5_NC,))


def _s5_cols_to_state(cols):
    lead = cols.shape[:-1]
    st = cols.reshape(lead + (S5_SLABS, 2, S5_SLAB_NC))
    re = st[..., 0, :].reshape(lead + (S5_GROUPS, S5_STATE))
    im = st[..., 1, :].reshape(lead + (S5_GROUPS, S5_STATE))
    return re, im


_IN_OFFS = [int(o) for o in np.cumsum([0, S5_WIDTH, S5_WIDTH, GLA_KEY, GLA_KEY, GLA_WIDTH, GLA_WIDTH, GLA_RANK,
                                       D_MODEL, D_MODEL])]


def _split_w_in_kernel(u_ref, qkv_ref, code_ref, ga_ref, gb_ref, m_ref, proj_ref, gate_ref):
    rows = 0
    for src in (u_ref, qkv_ref, code_ref):
        proj_ref[rows:rows + src.shape[1], :] = src[0].astype(BF16)
        rows += src.shape[1]
    proj_ref[rows:, :] = jnp.zeros((PROJ_W - rows, proj_ref.shape[1]), BF16)
    rows = 0
    for src in (ga_ref, gb_ref, m_ref):
        gate_ref[rows:rows + src.shape[1], :] = src[0].astype(BF16)
        rows += src.shape[1]


def _split_w_in(w_in):
    o = _IN_OFFS
    depth = w_in.shape[0]
    wt = jnp.swapaxes(w_in, 1, 2)
    tn = D_MODEL // 2
    rows = lambda lo, hi: pl.BlockSpec((pl.Element(1), pl.Element(hi - lo), pl.Element(tn)),
                                       lambda l, j: (l, lo, j * tn))
    out = lambda n: pl.BlockSpec((None, n, tn), lambda l, j: (l, 0, j))
    return pl.pallas_call(
        _split_w_in_kernel,
        grid=(depth, D_MODEL // tn),
        in_specs=[rows(o[0], o[1]), rows(o[2], o[5]), rows(o[6], o[7]), rows(o[1], o[2]), rows(o[5], o[6]),
                  rows(o[7], o[9])],
        out_specs=(out(PROJ_W), out(GATE_W)),
        out_shape=(jax.ShapeDtypeStruct((depth, PROJ_W, D_MODEL), BF16),
                   jax.ShapeDtypeStruct((depth, GATE_W, D_MODEL), BF16)),
        compiler_params=pltpu.CompilerParams(dimension_semantics=("arbitrary", "arbitrary"),
                                             vmem_limit_bytes=VMEM_LIMIT),
        name="split_w_in",
    )(wt, wt, wt, wt, wt, wt)


def _proj_kernel(*refs, has_pe):
    if has_pe:
        x_ref, pe_ref, shift_ref, scale_ref, ng_ref, w_ref, u_ref, q_ref, k_ref, v_ref, gl_ref = refs
    else:
        x_ref, shift_ref, scale_ref, ng_ref, w_ref, u_ref, q_ref, k_ref, v_ref, gl_ref = refs
    nsb, tm, _ = x_ref.shape
    rg = max(1, tm // (2 * MXU_TILE))
    grp = [(i, slice(j * tm // rg, (j + 1) * tm // rg)) for i in range(nsb) for j in range(rg)]
    x = [x_ref[i, r, :] + pe_ref[r, :] if has_pe else x_ref[i, r, :] for i, r in grp]
    h = [_modulated_norm(v, ng_ref[...], 1.0 + scale_ref[...], shift_ref[...]) for v in x]
    p = [_bdot_t(v, w_ref[...]) for v in h]
    for (i, r), v in zip(grp, p):
        o = 0
        for ref, width in ((u_ref, S5_WIDTH), (q_ref, GLA_KEY), (k_ref, GLA_KEY), (v_ref, GLA_WIDTH), (gl_ref, LANES)):
            ref[i, r, :] = v[:, o:o + width].astype(ref.dtype)
            o += width


def _mod_spec(l, cond_row, part):
    return pl.BlockSpec((None, None, 1, D_MODEL), lambda b, j: (l, cond_row(b), 0, part))


def _layer_spec(l, shape):
    return pl.BlockSpec((None,) + shape, lambda b, j: (l,) + (0,) * len(shape))


def _proj_call(x, pe, mod, l, cond_row, ng, w, tm, nsb=1):
    nseq, length, _ = x.shape
    row = lambda width: pl.BlockSpec((nsb, tm, width), lambda b, j: (b, j, 0))
    in_specs = [row(D_MODEL)]
    args = [x]
    if pe is not None:
        in_specs.append(pl.BlockSpec((tm, D_MODEL), lambda b, j: (j, 0)))
        args.append(pe)
    in_specs += [_mod_spec(l, cond_row, 0), _mod_spec(l, cond_row, 1), _layer_spec(l, (1, D_MODEL)),
                 _layer_spec(l, (PROJ_W, D_MODEL))]
    args += [mod, mod, ng, w]
    sds = lambda width, dt: jax.ShapeDtypeStruct((nseq, length, width), dt)
    return pl.pallas_call(
        functools.partial(_proj_kernel, has_pe=pe is not None),
        grid=(nseq // nsb, length // tm),
        in_specs=in_specs,
        out_specs=(row(S5_WIDTH), row(GLA_KEY), row(GLA_KEY), row(GLA_WIDTH), row(LANES)),
        out_shape=(sds(S5_WIDTH, F32), sds(GLA_KEY, F32), sds(GLA_KEY, F32), sds(GLA_WIDTH, BF16), sds(LANES, F32)),
        compiler_params=pltpu.CompilerParams(dimension_semantics=("arbitrary", "arbitrary"),
                                             vmem_limit_bytes=VMEM_LIMIT),
        name="proj",
    )(*args)


def _s5_perms(nb, tc):
    r = nb * tc
    pf = np.zeros((2 * r, r), np.float32)
    pb = np.zeros((2 * r, r), np.float32)
    for t in range(tc):
        for b in range(nb):
            pf[t * 2 * nb + b, b * tc + t] = 1.0
            pb[t * 2 * nb + nb + b, b * tc + (tc - 1 - t)] = 1.0
    return pf, pb


def _s5_store_unpacked(y2, yf_ref, yb_ref, t_f, t_b, nb, tc):
    rs = 2 * nb
    for t in range(tc):
        yf_ref[:, t_f + t, :S5_WIDTH] = y2[t * rs:t * rs + nb]
        yb_ref[:, t_b + tc - 1 - t, :S5_WIDTH] = y2[t * rs + nb:(t + 1) * rs]


def _s5_kernel(*refs, nb, tc, tiles_per_pass):
    use_perm = nb % 8 != 0
    uf1_ref, ub1_ref, uf2_ref, ub2_ref, uf0_ref, ub0_ref = refs[:6]
    pf_ref, pb_ref = refs[6:8] if use_perm else (None, None)
    (wbu_ref, wc_ref, are_ref, aim_ref, h0_ref, yf_ref, yb_ref, hfin_ref, bua_ref, bub_ref, hb_ref, st_ref,
     sin_ref, sof_ref, sob_ref) = refs[-15:]
    i = pl.program_id(0)
    rows = nb * tc
    rs = 2 * nb

    def packed_inputs(uf_ref, ub_ref):
        if not use_perm:
            out = []
            for r in (uf_ref, ub_ref):
                sin_ref[:, :tc, :S5_WIDTH] = r[...]
                out.append(jnp.concatenate([sin_ref[:, t, :S5_WIDTH] for t in range(tc)], axis=0).astype(BF16))
            return tuple(out)
        uf = uf_ref[...].reshape(rows, S5_WIDTH).astype(BF16)
        ub = ub_ref[...].reshape(rows, S5_WIDTH).astype(BF16)
        u2f = jnp.dot(pf_ref[...], uf, preferred_element_type=F32).astype(BF16)
        u2b = jnp.dot(pb_ref[...], ub, preferred_element_type=F32).astype(BF16)
        return [jnp.concatenate([u2f[:, s * LANES:(s + 1) * LANES], u2b[:, s * LANES:(s + 1) * LANES]], axis=1)
                for s in range(S5_SLABS)]

    def bu_slab(lhs, s, dst_ref):
        cols = slice(s * S5_SLAB_COLS, (s + 1) * S5_SLAB_COLS)
        if use_perm:
            dst_ref[:, cols] = jnp.dot(lhs[s], wbu_ref[s], preferred_element_type=F32)
            return
        uf, ub = lhs
        bf = jnp.dot(uf[:, s * LANES:(s + 1) * LANES], wbu_ref[s, :LANES, :], preferred_element_type=F32)
        bb = jnp.dot(ub[:, s * LANES:(s + 1) * LANES], wbu_ref[s, LANES:, :], preferred_element_type=F32)
        for t in range(tc):
            dst_ref[t * rs:t * rs + nb, cols] = bf[t * nb:(t + 1) * nb]
            dst_ref[t * rs + nb:(t + 1) * rs, cols] = bb[(tc - 1 - t) * nb:(tc - t) * nb]

    @pl.when(i == 0)
    def _():
        st_ref[...] = h0_ref[...]
        lhs0 = packed_inputs(uf0_ref, ub0_ref)
        for s in range(S5_SLABS):
            bu_slab(lhs0, s, bua_ref)

    half_step = functools.partial(_s5_half_step, wc_ref=wc_ref, are_ref=are_ref, aim_ref=aim_ref, hb_ref=hb_ref,
                                  st_ref=st_ref, bu_slab=bu_slab, rs=rs, tc=tc, tiles_per_pass=tiles_per_pass)
    y2 = half_step(bua_ref, bub_ref, packed_inputs(uf1_ref, ub1_ref))
    _s5_store_unpacked(y2, sof_ref, sob_ref, 0, tc, nb, tc)
    y2 = half_step(bub_ref, bua_ref, packed_inputs(uf2_ref, ub2_ref))
    _s5_store_unpacked(y2, sof_ref, sob_ref, tc, 0, nb, tc)
    yf_ref[...] = sof_ref[:, :2 * tc, :S5_WIDTH]
    yb_ref[...] = sob_ref[:, :2 * tc, :S5_WIDTH]
    hfin_ref[...] = st_ref[...]


def _s5_half_step(cur_ref, nxt_ref, lhs_next, *, wc_ref, are_ref, aim_ref, hb_ref, st_ref, bu_slab,
                  rs, tc, tiles_per_pass):
    slab_tiles = S5_SLAB_NC // LANES
    grp = max(1, 16 // rs)
    ys = []
    for s in range(S5_SLABS):
        bu_slab(lhs_next, s, nxt_ref)
        for c0 in range(slab_tiles * s, slab_tiles * (s + 1), tiles_per_pass):
            cts = list(range(c0, c0 + tiles_per_pass))
            cre = [(ct // slab_tiles) * S5_SLAB_COLS + (ct % slab_tiles) * LANES for ct in cts]
            cim = [c + S5_SLAB_NC for c in cre]
            a_re = [are_ref[:, ct * LANES:(ct + 1) * LANES] for ct in cts]
            a_im = [aim_ref[:, ct * LANES:(ct + 1) * LANES] for ct in cts]
            h_re = [st_ref[:, c:c + LANES] for c in cre]
            h_im = [st_ref[:, c:c + LANES] for c in cim]
            for t0 in range(0, tc, grp):
                out_re = [[] for _ in cts]
                out_im = [[] for _ in cts]
                for t in range(t0, t0 + grp):
                    r0 = t * rs
                    for j in range(len(cts)):
                        b_re = cur_ref[r0:r0 + rs, cre[j]:cre[j] + LANES]
                        b_im = cur_ref[r0:r0 + rs, cim[j]:cim[j] + LANES]
                        n_re = a_re[j] * h_re[j] - a_im[j] * h_im[j] + b_re
                        n_im = a_re[j] * h_im[j] + a_im[j] * h_re[j] + b_im
                        h_re[j], h_im[j] = n_re, n_im
                        out_re[j].append(n_re)
                        out_im[j].append(n_im)
                for j in range(len(cts)):
                    blk_re = out_re[j][0] if grp == 1 else jnp.concatenate(out_re[j], axis=0)
                    blk_im = out_im[j][0] if grp == 1 else jnp.concatenate(out_im[j], axis=0)
                    hb_ref[t0 * rs:(t0 + grp) * rs, cre[j]:cre[j] + LANES] = blk_re.astype(BF16)
                    hb_ref[t0 * rs:(t0 + grp) * rs, cim[j]:cim[j] + LANES] = blk_im.astype(BF16)
            for j in range(len(cts)):
                st_ref[:, cre[j]:cre[j] + LANES] = h_re[j]
                st_ref[:, cim[j]:cim[j] + LANES] = h_im[j]
        ys.append(jnp.dot(hb_ref[:, s * S5_SLAB_COLS:(s + 1) * S5_SLAB_COLS], wc_ref[s], preferred_element_type=F32))
    return jnp.concatenate(ys, axis=1)


def _s5_call(u, l, w_bu, w_c, are, aim, h0, tc, tiles_per_pass):
    nb, length, _ = u.shape
    n = length // tc
    rows = nb * tc
    const = lambda shape: pl.BlockSpec(shape, lambda i: (0,) * len(shape))
    layer = lambda shape: pl.BlockSpec((None,) + shape, lambda i: (l,) + (0,) * len(shape))
    blk = lambda chunk: pl.BlockSpec((nb, tc, S5_WIDTH), lambda i: (0, chunk(i), 0))
    out_f = pl.BlockSpec((nb, 2 * tc, S5_WIDTH), lambda i: (0, i, 0))
    out_b = pl.BlockSpec((nb, 2 * tc, S5_WIDTH), lambda i: (0, n // 2 - 1 - i, 0))
    second = lambda i: jnp.minimum(2 * i + 2, n - 1)
    in_specs = [blk(lambda i: 2 * i + 1), blk(lambda i: n - 2 - 2 * i), blk(second), blk(lambda i: n - 1 - second(i)),
                blk(lambda i: 0), blk(lambda i: n - 1)]
    args = [u] * 6
    if nb % 8 != 0:
        pf, pb = _s5_perms(nb, tc)
        in_specs += [const((2 * rows, rows)), const((2 * rows, rows))]
        args += [jnp.asarray(pf, BF16), jnp.asarray(pb, BF16)]
    in_specs += [layer((S5_SLABS, 2 * LANES, S5_SLAB_COLS)), layer((S5_SLABS, S5_SLAB_COLS, LANES)),
                 layer((2 * nb, S5_NC)), layer((2 * nb, S5_NC)), const((2 * nb, 2 * S5_NC))]
    args += [w_bu, w_c, are, aim, h0]
    return pl.pallas_call(
        functools.partial(_s5_kernel, nb=nb, tc=tc, tiles_per_pass=tiles_per_pass),
        grid=(n // 2,),
        in_specs=in_specs,
        out_specs=(out_f, out_b, const((2 * nb, 2 * S5_NC))),
        out_shape=(jax.ShapeDtypeStruct(u.shape, F32), jax.ShapeDtypeStruct(u.shape, F32),
                   jax.ShapeDtypeStruct((2 * nb, 2 * S5_NC), F32)),
        scratch_shapes=[pltpu.VMEM((2 * rows, 2 * S5_NC), F32), pltpu.VMEM((2 * rows, 2 * S5_NC), F32),
                        pltpu.VMEM((2 * rows, 2 * S5_NC), BF16), pltpu.VMEM((2 * nb, 2 * S5_NC), F32),
                        pltpu.VMEM((nb, tc + 8, S5_WIDTH + LANES), F32),
                        pltpu.VMEM((nb, 2 * tc + 8, S5_WIDTH + LANES), F32),
                        pltpu.VMEM((nb, 2 * tc + 8, S5_WIDTH + LANES), F32)],
        compiler_params=pltpu.CompilerParams(dimension_semantics=("arbitrary",), vmem_limit_bytes=VMEM_LIMIT),
        name="s5_scan",
    )(*args)


def _gla_kernel(*refs, length, has_s0, n_prev, nsq, cpi):
    q_ref, k_ref, v_ref, gl_ref, wg_ref, bg_ref = refs[:6]
    s0_ref = refs[6] if has_s0 else None
    prev_refs = refs[6 + has_s0:6 + has_s0 + n_prev]
    o_ref, sfin_ref, sf_ref, sb_ref = refs[-4:]
    c = GLA_CHUNK
    n = length // c
    ri = lax.broadcasted_iota(jnp.int32, (c, c), 0)
    ci = lax.broadcasted_iota(jnp.int32, (c, c), 1)
    lane = lax.broadcasted_iota(jnp.int32, (1, GLA_KEY), 1)
    head_mask = [(lane >= h * GLA_DK) & (lane < (h + 1) * GLA_DK) for h in range(GLA_HEADS)]
    causal = [ci <= ri, ci >= ri]
    tri = [m.astype(BF16) for m in causal]
    end = (c - 1, 0)
    chains = [(sq, d, j) for j in range(cpi) for sq in range(nsq) for d in (0, 1)]
    key_rows = lambda h: slice(h * GLA_DK, (h + 1) * GLA_DK)
    val_cols = lambda h: slice(h * GLA_DV, (h + 1) * GLA_DV)

    sb_ref[...] = jnp.zeros_like(sb_ref)
    for sq in range(nsq):
        for d in (0, 1):
            for h in range(GLA_HEADS):
                if has_s0:
                    sf_ref[sq, d, h] = s0_ref[sq, d, h]
                    sb_ref[sq, d, key_rows(h), val_cols(h)] = s0_ref[sq, d, h].astype(BF16)
                else:
                    sf_ref[sq, d, h] = jnp.zeros((GLA_DK, GLA_DV), F32)

    def chunk_group(r):
        row = lambda d, j: r[d][j]
        qc = [q_ref[sq, pl.ds(row(d, j), c), :] * (GLA_DK ** -0.5) for sq, d, j in chains]
        kc = [k_ref[sq, pl.ds(row(d, j), c), :] for sq, d, j in chains]
        vc = [v_ref[sq, pl.ds(row(d, j), c), :] for sq, d, j in chains]
        logits = [_bdot(gl_ref[sq, pl.ds(row(d, j), c), :], wg_ref[d]) + bg_ref[d] for sq, d, j in chains]
        g = [_split2(_log_sigmoid(x) * (1.0 / GLA_GATE_NORM)) for x in logits]
        b = [jnp.dot(tri[d], hi, preferred_element_type=F32) + jnp.dot(tri[d], lo, preferred_element_type=F32)
             for (sq, d, j), (hi, lo) in zip(chains, g)]
        bt = [x.T for x in b]
        kt = [x.T for x in kc]
        b_mid = [x[c // 2:c // 2 + 1, :] for x in b]
        bt_mid = [x[:, c // 2:c // 2 + 1] for x in bt]
        bt_end = [x[:, end[d]:end[d] + 1] for (sq, d, j), x in zip(chains, bt)]
        q_in = [(x * jnp.exp(y)).astype(BF16) for x, y in zip(qc, b)]
        q_mid = [(x * jnp.exp(y - m)).astype(BF16) for x, y, m in zip(qc, b, b_mid)]
        k_mid_t = [(x * jnp.exp(m - y)).astype(BF16) for x, y, m in zip(kt, bt, bt_mid)]
        k_end_t = [(x * jnp.exp(e - y)).astype(BF16) for x, y, e in zip(kt, bt, bt_end)]
        q_st = [jnp.concatenate([jnp.where(head_mask[h], x, jnp.zeros_like(x)) for h in range(GLA_HEADS)], axis=0)
                for x in q_mid]
        a = [jnp.dot(x, y, preferred_element_type=F32) for x, y in zip(q_st, k_mid_t)]
        parts = [[] for _ in chains]
        for h in range(GLA_HEADS):
            for i, (sq, d, j) in enumerate(chains):
                a_h = jnp.where(causal[d], a[i][h * c:(h + 1) * c, :], 0.0).astype(BF16)
                parts[i].append(jnp.dot(a_h, vc[i][:, val_cols(h)], preferred_element_type=F32))
        upd = [[jnp.dot(k_end_t[i][key_rows(h), :], vc[i][:, val_cols(h)], preferred_element_type=F32)
                for h in range(GLA_HEADS)] for i in range(len(chains))]
        o_inter = [None] * len(chains)
        for i, (sq, d, j) in enumerate(chains):
            o_inter[i] = jnp.dot(q_in[i], sb_ref[sq, d], preferred_element_type=F32)
            for h in range(GLA_HEADS):
                s_new = jnp.exp(bt_end[i][key_rows(h), :]) * sf_ref[sq, d, h] + upd[i][h]
                sf_ref[sq, d, h] = s_new
                sb_ref[sq, d, key_rows(h), val_cols(h)] = s_new.astype(BF16)
        return [x + jnp.concatenate(p, axis=1) for x, p in zip(o_inter, parts)]

    def rows_of(step):
        fwd = [pl.multiple_of((step * cpi + j) * c, c) for j in range(cpi)]
        bwd = [pl.multiple_of((n - 1 - step * cpi - j) * c, c) for j in range(cpi)]
        return fwd, bwd

    def first_half(step, carry):
        r = rows_of(step)
        for (sq, d, j), o in zip(chains, chunk_group(r)):
            o_ref[sq, pl.ds(r[d][j], c), :] = o
        return carry

    def second_half(step, carry):
        r = rows_of(step)
        for (sq, d, j), o in zip(chains, chunk_group(r)):
            o_ref[sq, pl.ds(r[d][j], c), :] = o + o_ref[sq, pl.ds(r[d][j], c), :]
        return carry

    steps = n // cpi
    lax.fori_loop(0, steps // 2, first_half, 0)
    lax.fori_loop(steps // 2, steps, second_half, 0)
    if n_prev:
        for j, prev_ref in enumerate(prev_refs):
            sfin_ref[:, j] = prev_ref[...]
        sfin_ref[:, n_prev] = sf_ref[...]
    else:
        sfin_ref[...] = sf_ref[...]


def _gla_call(q, k, v, gl, l, wg, bg, s0, nsq, cpi, prev_states=()):
    nseq, length, _ = q.shape
    assert (length // GLA_CHUNK) % (2 * cpi) == 0 and nseq % nsq == 0
    seq = lambda width: pl.BlockSpec((nsq, length, width), lambda b: (b, 0, 0))
    layer = lambda shape: pl.BlockSpec((None,) + shape, lambda b: (l,) + (0,) * len(shape))
    in_specs = [seq(GLA_KEY), seq(GLA_KEY), seq(GLA_WIDTH), seq(LANES),
                layer((2, LANES, GLA_KEY)), layer((2, 1, GLA_KEY))]
    args = [q, k, v, gl, wg, bg]
    if s0 is not None:
        in_specs.append(pl.BlockSpec((nsq, None, 2, GLA_HEADS, GLA_DK, GLA_DV), lambda b: (b, l, 0, 0, 0, 0)))
        args.append(s0)
    state = (2, GLA_HEADS, GLA_DK, GLA_DV)
    state_spec = pl.BlockSpec((nsq,) + state, lambda b: (b, 0, 0, 0, 0))
    in_specs += [state_spec] * len(prev_states)
    args += list(prev_states)
    lead = (len(prev_states) + 1,) if prev_states else ()
    return pl.pallas_call(
        functools.partial(_gla_kernel, length=length, has_s0=s0 is not None, n_prev=len(prev_states), nsq=nsq, cpi=cpi),
        grid=(nseq // nsq,),
        in_specs=in_specs,
        out_specs=(seq(GLA_WIDTH), pl.BlockSpec((nsq,) + lead + state, lambda b: (b,) + (0,) * len(lead + state))),
        out_shape=(jax.ShapeDtypeStruct((nseq, length, GLA_WIDTH), F32),
                   jax.ShapeDtypeStruct((nseq,) + lead + state, F32)),
        scratch_shapes=[pltpu.VMEM((nsq, 2, GLA_HEADS, GLA_DK, GLA_DV), F32),
                        pltpu.VMEM((nsq, 2, GLA_KEY, GLA_WIDTH), BF16)],
        compiler_params=pltpu.CompilerParams(dimension_semantics=("arbitrary",), vmem_limit_bytes=VMEM_LIMIT),
        name="gla",
    )(*args)


def _mix_kernel(*refs, has_pe, last, row_groups):
    if has_pe:
        (x_ref, pe_ref, yf_ref, yb_ref, u_ref, o_ref, shift_ref, scale_ref, gate_ref, ng_ref, wg_ref, d_ref,
         wglu_ref, bglu_ref, wpa_ref, wpb_ref, wo_ref, fng_ref, gng_ref, out_ref) = refs
    else:
        (x_ref, yf_ref, yb_ref, u_ref, o_ref, shift_ref, scale_ref, gate_ref, ng_ref, wg_ref, d_ref,
         wglu_ref, bglu_ref, wpa_ref, wpb_ref, wo_ref, fng_ref, gng_ref, out_ref) = refs

    def head_norm(o):
        heads = [o[:, h * GLA_DV:(h + 1) * GLA_DV] for h in range(GLA_HEADS)]
        heads = [v * lax.rsqrt(jnp.mean(v * v, axis=-1, keepdims=True) + EPS) for v in heads]
        return jnp.concatenate(heads, axis=1) * gng_ref[...]

    nsb, tm, _ = x_ref.shape
    grp = [(i, slice(j * tm // row_groups, (j + 1) * tm // row_groups)) for i in range(nsb) for j in range(row_groups)]
    x = [x_ref[i, r, :] + pe_ref[r, :] if has_pe else x_ref[i, r, :] for i, r in grp]
    h = [_modulated_norm(v, ng_ref[...], 1.0 + scale_ref[...], shift_ref[...]).astype(BF16) for v in x]
    gts = [_bdot_t(v, wg_ref[...]) for v in h]
    y = [_gelu_tanh(yf_ref[i, r, :] + yb_ref[i, r, :] + d_ref[...] * u_ref[i, r, :]) for i, r in grp]
    glu = [_bdot(v, wglu_ref[...]) for v in y]
    y = [v * _sigmoid(g + bglu_ref[...]) for v, g in zip(y, glu)]
    y_a = [v * _silu(g[:, :S5_WIDTH]) for v, g in zip(y, gts)]
    y_b = [head_norm(o_ref[i, r, :]) * _silu(g[:, S5_WIDTH:2 * S5_WIDTH]) for (i, r), g in zip(grp, gts)]
    p_a = [_bdot(v, wpa_ref[...]) for v in y_a]
    p_b = [_bdot(v, wpb_ref[...]) for v in y_b]
    merged = [_sigmoid(g[:, 2 * S5_WIDTH:2 * S5_WIDTH + D_MODEL]) * a + _sigmoid(g[:, 2 * S5_WIDTH + D_MODEL:]) * b
              for g, a, b in zip(gts, p_a, p_b)]
    xn = [v + gate_ref[...] * _bdot(m, wo_ref[...]) for v, m in zip(x, merged)]
    for (i, r), v in zip(grp, xn):
        if last:
            ms = jnp.mean(v * v, axis=-1, keepdims=True)
            v = v * lax.rsqrt(ms + EPS) * fng_ref[...]
        out_ref[i, r, :] = v


def _mix_call(x, pe, yf, yb, u, o, mod, l, cond_row, ng, wg, s5_d, wglu, bglu, wpa, wpb, wo, fng, gng, tm, last,
              nsb=1):
    nseq, length, _ = x.shape
    row = lambda width: pl.BlockSpec((nsb, tm, width), lambda b, j: (b, j, 0))
    in_specs = [row(D_MODEL)]
    args = [x]
    if pe is not None:
        in_specs.append(pl.BlockSpec((tm, D_MODEL), lambda b, j: (j, 0)))
        args.append(pe)
    in_specs += [row(S5_WIDTH), row(S5_WIDTH), row(S5_WIDTH), row(GLA_WIDTH),
                 _mod_spec(l, cond_row, 0), _mod_spec(l, cond_row, 1), _mod_spec(l, cond_row, 2),
                 _layer_spec(l, (1, D_MODEL)), _layer_spec(l, (GATE_W, D_MODEL)), _layer_spec(l, (1, S5_WIDTH)),
                 _layer_spec(l, (S5_WIDTH, S5_WIDTH)), _layer_spec(l, (1, S5_WIDTH)),
                 _layer_spec(l, (S5_WIDTH, D_MODEL)), _layer_spec(l, (GLA_WIDTH, D_MODEL)),
                 _layer_spec(l, (D_MODEL, D_MODEL)), pl.BlockSpec((1, D_MODEL), lambda b, j: (0, 0)),
                 _layer_spec(l, (1, GLA_WIDTH))]
    args += [yf, yb, u, o, mod, mod, mod, ng, wg, s5_d, wglu, bglu, wpa, wpb, wo, fng, gng]
    return pl.pallas_call(
        functools.partial(_mix_kernel, has_pe=pe is not None, last=last, row_groups=max(1, tm // MXU_TILE)),
        grid=(nseq // nsb, length // tm),
        in_specs=in_specs,
        out_specs=row(D_MODEL),
        out_shape=jax.ShapeDtypeStruct(x.shape, F32),
        compiler_params=pltpu.CompilerParams(dimension_semantics=("arbitrary", "arbitrary"),
                                             vmem_limit_bytes=VMEM_LIMIT),
        name="mix",
    )(*args)


_CONTEXT_BLOCKS = dict(proj=dict(tm=None, nsb=4), s5=dict(tc=16, tiles_per_pass=2), gla=dict(nsq=4, cpi=1),
                       mix=dict(tm=None, nsb=2))
_LATENT_BLOCKS = dict(proj=dict(tm=1024), s5=dict(tc=32, tiles_per_pass=4), gla=dict(nsq=1, cpi=4), mix=dict(tm=512))


def _blocks(choice, length):
    return {k: dict(v, tm=v["tm"] or length) if "tm" in v else v for k, v in choice.items()}


def _grid_pos_embed(length, dim):
    rows = length // GRID_W
    quarter = dim // 4
    freqs = jnp.exp(-math.log(10000.0) * jnp.arange(quarter, dtype=F32) / quarter)

    def sincos(pos):
        ang = pos.astype(F32)[:, None] * freqs[None, :]
        return jnp.concatenate([jnp.sin(ang), jnp.cos(ang)], axis=-1)

    er = sincos(jnp.arange(rows))
    ec = sincos(jnp.arange(GRID_W))
    pe = jnp.concatenate([jnp.broadcast_to(er[:, None, :], (rows, GRID_W, dim // 2)),
                          jnp.broadcast_to(ec[None, :, :], (rows, GRID_W, dim // 2))], axis=-1)
    return pe.reshape(rows * GRID_W, dim)


def kernel(x_prompt, x_sample, c, state_s5_re, state_s5_im, state_gla, c_ctx, norm_g, w_mod, b_mod, w_in, gla_wg_up,
           gla_bg, gla_norm_g, s5_lam_re, s5_lam_im, s5_log_dt, s5_b_re, s5_b_im, s5_c_re, s5_c_im, s5_d, w_glu,
           b_glu, w_pa, w_pb, w_o, final_norm_g):
    depth = w_in.shape[0]
    bp, lp, _ = x_prompt.shape
    bs, ls, _ = x_sample.shape

    cond8 = jnp.concatenate([c_ctx[None], c, jnp.zeros((8 - 1 - bs, D_MODEL), F32)], axis=0)
    mod = _mod_call(cond8, w_mod, b_mod)

    a_re, a_im, w_bu, w_c = _s5_params(s5_lam_re, s5_lam_im, s5_log_dt, s5_b_re, s5_b_im, s5_c_re, s5_c_im)

    w_proj, w_gate = _split_w_in(w_in)
    wg_up = jnp.concatenate([gla_wg_up, jnp.zeros((depth, 2, LANES - GLA_RANK, GLA_KEY), F32)], axis=2).astype(BF16)
    w_glu_b, w_pa_b, w_pb_b, w_o_b = (w.astype(BF16) for w in (w_glu, w_pa, w_pb, w_o))

    vec = lambda a: a[:, None, :]
    mod = mod.reshape(depth, 8, 1, 3 * D_MODEL)
    ng, gng, bg = vec(norm_g), vec(gla_norm_g), gla_bg[:, :, None, :]
    common = (ng, w_gate, vec(s5_d), w_glu_b, vec(b_glu), w_pa_b, w_pb_b, w_o_b, final_norm_g[None], gng)
    are_p, aim_p = jnp.repeat(a_re, bp, axis=1), jnp.repeat(a_im, bp, axis=1)
    are_s, aim_s = jnp.repeat(a_re, bs, axis=1), jnp.repeat(a_im, bs, axis=1)
    ctx_row = lambda b: 0
    cond_row = lambda b: b + 1

    pe = _grid_pos_embed(ls, D_MODEL)
    blk_p, blk_s = _blocks(_CONTEXT_BLOCKS, lp), _blocks(_LATENT_BLOCKS, ls)
    zero_h0 = jnp.zeros((2 * bp, 2 * S5_NC), F32)
    xp, xs = x_prompt, x_sample
    new_re, new_im, new_gla = [], [], []
    for l in range(depth):
        last = l == depth - 1

        u, q, k, v, gl = _proj_call(xp, None, mod, l, ctx_row, ng, w_proj, **blk_p["proj"])
        yf, yb, hfin = _s5_call(u, l, w_bu, w_c, are_p, aim_p, zero_h0, **blk_p["s5"])
        o, sfin = _gla_call(q, k, v, gl, l, wg_up, bg, None, prev_states=new_gla if last else (), **blk_p["gla"])
        new_gla.append(sfin)
        xp = _mix_call(xp, None, yf, yb, u, o, mod, l, ctx_row, *common, last=last, **blk_p["mix"])
        re_l, im_l = _s5_cols_to_state(hfin.reshape(2, bp, 2 * S5_NC))
        new_re.append(jnp.swapaxes(re_l, 0, 1))
        new_im.append(jnp.swapaxes(im_l, 0, 1))

        pe_l = pe if l == 0 else None
        u, q, k, v, gl = _proj_call(xs, pe_l, mod, l, cond_row, ng, w_proj, **blk_s["proj"])
        h0 = _s5_state_to_cols(jnp.swapaxes(state_s5_re[:, l], 0, 1), jnp.swapaxes(state_s5_im[:, l], 0, 1))
        yf, yb, _ = _s5_call(u, l, w_bu, w_c, are_s, aim_s, h0.reshape(2 * bs, 2 * S5_NC), **blk_s["s5"])
        o, _ = _gla_call(q, k, v, gl, l, wg_up, bg, state_gla, **blk_s["gla"])
        xs = _mix_call(xs, pe_l, yf, yb, u, o, mod, l, cond_row, *common, last=last, **blk_s["mix"])

    new_gla = new_gla[-1] if depth > 1 else new_gla[0][:, None]
    return (xp, xs, jnp.stack(new_re, axis=1), jnp.stack(new_im, axis=1), new_gla)
```

```python
import functools
import math

import numpy as np
import jax
import jax.numpy as jnp
from jax import lax
from jax.experimental import pallas as pl
from jax.experimental.pallas import tpu as pltpu

F32 = jnp.float32
BF16 = jnp.bfloat16

D_MODEL = 1024
GRID_W = 64
S5_WIDTH = 512
S5_GROUPS = 32
S5_CH = 16
S5_STATE = 64
S5_NC = S5_GROUPS * S5_STATE
S5_SLAB_GROUPS = 8
S5_SLABS = S5_GROUPS // S5_SLAB_GROUPS
S5_SLAB_NC = S5_SLAB_GROUPS * S5_STATE
S5_SLAB_COLS = 2 * S5_SLAB_NC
GLA_HEADS = 4
GLA_DK = 64
GLA_DV = 128
GLA_KEY = GLA_HEADS * GLA_DK
GLA_WIDTH = GLA_HEADS * GLA_DV
GLA_RANK = 16
GLA_GATE_NORM = 16.0
GLA_CHUNK = 128
EPS = 1e-6
LANES = 128
PROJ_W = S5_WIDTH + 2 * GLA_KEY + GLA_WIDTH + LANES
GATE_W = 2 * S5_WIDTH + 2 * D_MODEL
VMEM_LIMIT = 60 * 1024 * 1024
MXU_TILE = 256


def _bdot(a, b):
    return jnp.dot(a.astype(BF16), b.astype(BF16), preferred_element_type=F32)


def _bdot_t(a, bt):
    return lax.dot_general(a.astype(BF16), bt.astype(BF16), (((1,), (1,)), ((), ())), preferred_element_type=F32)


def _split2(x):
    hi = x.astype(BF16)
    lo = (x - hi.astype(F32)).astype(BF16)
    return hi, lo


def _sigmoid(x):
    return 1.0 / (1.0 + jnp.exp2(x * (-math.log2(math.e))))


def _silu(x):
    return x * _sigmoid(x)


def _gelu_tanh(x):
    c = math.sqrt(2.0 / math.pi)
    return 0.5 * x * (1.0 + jnp.tanh(c * (x + 0.044715 * (x * x * x))))


def _log_sigmoid(x):
    return jnp.minimum(x, 0.0) - jnp.log(1.0 + jnp.exp(-jnp.abs(x)))


def _modulated_norm(x, ng, scale1p, shift):
    ms = jnp.mean(x * x, axis=-1, keepdims=True)
    return x * lax.rsqrt(ms + EPS) * (ng * scale1p) + shift


def _mod_kernel(cond_ref, w_ref, b_ref, out_ref):
    c = cond_ref[...]
    s_hi, s_lo = _split2(_silu(c))
    w_hi, w_lo = _split2(w_ref[0])
    acc = jnp.dot(s_hi, w_hi, preferred_element_type=F32)
    acc += jnp.dot(s_lo, w_hi, preferred_element_type=F32)
    acc += jnp.dot(s_hi, w_lo, preferred_element_type=F32)
    out_ref[0] = acc + b_ref[0]


def _mod_call(cond8, w_mod, b_mod):
    depth = w_mod.shape[0]
    nb = 1536
    return pl.pallas_call(
        _mod_kernel,
        grid=(depth, 3 * D_MODEL // nb),
        in_specs=[
            pl.BlockSpec((8, D_MODEL), lambda l, j: (0, 0)),
            pl.BlockSpec((1, D_MODEL, nb), lambda l, j: (l, 0, j)),
            pl.BlockSpec((1, 1, nb), lambda l, j: (l, 0, j)),
        ],
        out_specs=pl.BlockSpec((1, 8, nb), lambda l, j: (l, 0, j)),
        out_shape=jax.ShapeDtypeStruct((depth, 8, 3 * D_MODEL), F32),
        compiler_params=pltpu.CompilerParams(dimension_semantics=("arbitrary", "arbitrary")),
        name="adaln_mod",
    )(cond8, w_mod, b_mod.reshape(depth, 1, 3 * D_MODEL))


def _zoh(lre, lim, ldt):
    dt = jnp.exp(ldt)
    mag = jnp.exp(lre * dt)
    a_re = mag * jnp.cos(lim * dt)
    a_im = mag * jnp.sin(lim * dt)
    n_re = a_re - 1.0
    inv = 1.0 / (lre * lre + lim * lim)
    return a_re, a_im, (n_re * lre + a_im * lim) * inv, (a_im * lre - n_re * lim) * inv


def _zoh_kernel(lre_ref, lim_ref, ldt_ref, bre_ref, bim_ref, cre_ref, cim_ref, are_ref, aim_ref, wbu_ref, wc_ref,
                *, depth):
    a_re, a_im, c_re, c_im = _zoh(lre_ref[...], lim_ref[...], ldt_ref[...])
    are_ref[...] = a_re
    aim_ref[...] = a_im
    cmat = [cre_ref[...], -cim_ref[...]]
    wbu_ref[...] = jnp.zeros_like(wbu_ref)
    wc_ref[...] = jnp.zeros_like(wc_ref)
    for l in range(depth):
        for s in range(S5_SLABS):
            for gl in range(S5_SLAB_GROUPS):
                g = S5_SLAB_GROUPS * s + gl
                src = (l * S5_GROUPS + g) * S5_CH
                bre = bre_ref[src:src + S5_CH, :]
                bim = bim_ref[src:src + S5_CH, :]
                for d in range(2):
                    r = (l * 2 + d) * S5_GROUPS + g
                    k_re, k_im = c_re[r:r + 1, :], c_im[r:r + 1, :]
                    bbar = [k_re * bre - k_im * bim, k_re * bim + k_im * bre]
                    row = (d * S5_SLAB_GROUPS + gl) * S5_CH
                    for ri in range(2):
                        col = ri * S5_SLAB_NC + gl * S5_STATE
                        wbu_ref[l, s, row:row + S5_CH, col:col + S5_STATE] = bbar[ri].astype(BF16)
                src = (l * S5_GROUPS + g) * S5_STATE
                for ri in range(2):
                    col = ri * S5_SLAB_NC + gl * S5_STATE
                    wc_ref[l, s, col:col + S5_STATE, gl * S5_CH:(gl + 1) * S5_CH] = (
                        cmat[ri][src:src + S5_STATE, :].astype(BF16))


def _s5_params(s5_lam_re, s5_lam_im, s5_log_dt, s5_b_re, s5_b_im, s5_c_re, s5_c_im):
    depth = s5_lam_re.shape[0]
    small = lambda a: a.reshape(depth * 2 * S5_GROUPS, -1)
    swapped = lambda a: jnp.swapaxes(a, -1, -2).reshape(-1, a.shape[-2])
    a_re, a_im, w_bu, w_c = pl.pallas_call(
        functools.partial(_zoh_kernel, depth=depth),
        out_shape=(jax.ShapeDtypeStruct((depth * 2 * S5_GROUPS, S5_STATE), F32),
                   jax.ShapeDtypeStruct((depth * 2 * S5_GROUPS, S5_STATE), F32),
                   jax.ShapeDtypeStruct((depth, S5_SLABS, 2 * LANES, S5_SLAB_COLS), BF16),
                   jax.ShapeDtypeStruct((depth, S5_SLABS, S5_SLAB_COLS, LANES), BF16)),
        name="s5_zoh",
    )(small(s5_lam_re), small(s5_lam_im), small(s5_log_dt), swapped(s5_b_re), swapped(s5_b_im),
      swapped(s5_c_re), swapped(s5_c_im))
    return a_re.reshape(depth, 2, S5_NC), a_im.reshape(depth, 2, S5_NC), w_bu, w_c


def _s5_state_to_cols(re, im):
    lead = re.shape[:-2]
    st = jnp.stack([re.reshape(lead + (S5_SLABS, S5_SLAB_NC)), im.reshape(lead + (S5_SLABS, S5_SLAB_NC))], axis=-2)
    return st.reshape(lead + (2 * S5_NC,))


def _s5_cols_to_state(cols):
    lead = cols.shape[:-1]
    st = cols.reshape(lead + (S5_SLABS, 2, S5_SLAB_NC))
    re = st[..., 0, :].reshape(lead + (S5_GROUPS, S5_STATE))
    im = st[..., 1, :].reshape(lead + (S5_GROUPS, S5_STATE))
    return re, im


_IN_OFFS = [int(o) for o in np.cumsum([0, S5_WIDTH, S5_WIDTH, GLA_KEY, GLA_KEY, GLA_WIDTH, GLA_WIDTH, GLA_RANK,
                                       D_MODEL, D_MODEL])]


def _split_w_in_kernel(u_ref, qkv_ref, code_ref, ga_ref, gb_ref, m_ref, proj_ref, gate_ref):
    rows = 0
    for src in (u_ref, qkv_ref, code_ref):
        proj_ref[rows:rows + src.shape[1], :] = src[0].astype(BF16)
        rows += src.shape[1]
    proj_ref[rows:, :] = jnp.zeros((PROJ_W - rows, proj_ref.shape[1]), BF16)
    rows = 0
    for src in (ga_ref, gb_ref, m_ref):
        gate_ref[rows:rows + src.shape[1], :] = src[0].astype(BF16)
        rows += src.shape[1]


def _split_w_in(w_in):
    o = _IN_OFFS
    depth = w_in.shape[0]
    wt = jnp.swapaxes(w_in, 1, 2)
    tn = D_MODEL // 2
    rows = lambda lo, hi: pl.BlockSpec((pl.Element(1), pl.Element(hi - lo), pl.Element(tn)),
                                       lambda l, j: (l, lo, j * tn))
    out = lambda n: pl.BlockSpec((None, n, tn), lambda l, j: (l, 0, j))
    return pl.pallas_call(
        _split_w_in_kernel,
        grid=(depth, D_MODEL // tn),
        in_specs=[rows(o[0], o[1]), rows(o[2], o[5]), rows(o[6], o[7]), rows(o[1], o[2]), rows(o[5], o[6]),
                  rows(o[7], o[9])],
        out_specs=(out(PROJ_W), out(GATE_W)),
        out_shape=(jax.ShapeDtypeStruct((depth, PROJ_W, D_MODEL), BF16),
                   jax.ShapeDtypeStruct((depth, GATE_W, D_MODEL), BF16)),
        compiler_params=pltpu.CompilerParams(dimension_semantics=("arbitrary", "arbitrary"),
                                             vmem_limit_bytes=VMEM_LIMIT),
        name="split_w_in",
    )(wt, wt, wt, wt, wt, wt)


def _proj_kernel(*refs, has_pe):
    if has_pe:
        x_ref, pe_ref, shift_ref, scale_ref, ng_ref, w_ref, u_ref, q_ref, k_ref, v_ref, gl_ref = refs
    else:
        x_ref, shift_ref, scale_ref, ng_ref, w_ref, u_ref, q_ref, k_ref, v_ref, gl_ref = refs
    nsb, tm, _ = x_ref.shape
    rg = max(1, tm // (2 * MXU_TILE))
    grp = [(i, slice(j * tm // rg, (j + 1) * tm // rg)) for i in range(nsb) for j in range(rg)]
    x = [x_ref[i, r, :] + pe_ref[r, :] if has_pe else x_ref[i, r, :] for i, r in grp]
    h = [_modulated_norm(v, ng_ref[...], 1.0 + scale_ref[...], shift_ref[...]) for v in x]
    p = [_bdot_t(v, w_ref[...]) for v in h]
    for (i, r), v in zip(grp, p):
        o = 0
        for ref, width in ((u_ref, S5_WIDTH), (q_ref, GLA_KEY), (k_ref, GLA_KEY), (v_ref, GLA_WIDTH), (gl_ref, LANES)):
            ref[i, r, :] = v[:, o:o + width].astype(ref.dtype)
            o += width


def _mod_spec(l, cond_row, part):
    return pl.BlockSpec((None, None, 1, D_MODEL), lambda b, j: (l, cond_row(b), 0, part))


def _layer_spec(l, shape):
    return pl.BlockSpec((None,) + shape, lambda b, j: (l,) + (0,) * len(shape))


def _proj_call(x, pe, mod, l, cond_row, ng, w, tm, nsb=1):
    nseq, length, _ = x.shape
    row = lambda width: pl.BlockSpec((nsb, tm, width), lambda b, j: (b, j, 0))
    in_specs = [row(D_MODEL)]
    args = [x]
    if pe is not None:
        in_specs.append(pl.BlockSpec((tm, D_MODEL), lambda b, j: (j, 0)))
        args.append(pe)
    in_specs += [_mod_spec(l, cond_row, 0), _mod_spec(l, cond_row, 1), _layer_spec(l, (1, D_MODEL)),
                 _layer_spec(l, (PROJ_W, D_MODEL))]
    args += [mod, mod, ng, w]
    sds = lambda width, dt: jax.ShapeDtypeStruct((nseq, length, width), dt)
    return pl.pallas_call(
        functools.partial(_proj_kernel, has_pe=pe is not None),
        grid=(nseq // nsb, length // tm),
        in_specs=in_specs,
        out_specs=(row(S5_WIDTH), row(GLA_KEY), row(GLA_KEY), row(GLA_WIDTH), row(LANES)),
        out_shape=(sds(S5_WIDTH, F32), sds(GLA_KEY, F32), sds(GLA_KEY, F32), sds(GLA_WIDTH, BF16), sds(LANES, F32)),
        compiler_params=pltpu.CompilerParams(dimension_semantics=("arbitrary", "arbitrary"),
                                             vmem_limit_bytes=VMEM_LIMIT),
        name="proj",
    )(*args)


def _s5_perms(nb, tc):
    r = nb * tc
    pf = np.zeros((2 * r, r), np.float32)
    pb = np.zeros((2 * r, r), np.float32)
    for t in range(tc):
        for b in range(nb):
            pf[t * 2 * nb + b, b * tc + t] = 1.0
            pb[t * 2 * nb + nb + b, b * tc + (tc - 1 - t)] = 1.0
    return pf, pb


def _s5_store_unpacked(y2, yf_ref, yb_ref, t_f, t_b, nb, tc):
    rs = 2 * nb
    for t in range(tc):
        yf_ref[:, t_f + t, :] = y2[t * rs:t * rs + nb]
        yb_ref[:, t_b + tc - 1 - t, :] = y2[t * rs + nb:(t + 1) * rs]


def _s5_kernel(*refs, nb, tc, tiles_per_pass):
    use_perm = nb % 8 != 0
    uf1_ref, ub1_ref, uf2_ref, ub2_ref, uf0_ref, ub0_ref = refs[:6]
    pf_ref, pb_ref = refs[6:8] if use_perm else (None, None)
    wbu_ref, wc_ref, are_ref, aim_ref, h0_ref, yf_ref, yb_ref, hfin_ref, bua_ref, bub_ref, hb_ref, st_ref = refs[-12:]
    i = pl.program_id(0)
    rows = nb * tc
    rs = 2 * nb

    def packed_inputs(uf_ref, ub_ref):
        if not use_perm:
            return tuple(jnp.concatenate([r[:, t, :] for t in range(tc)], axis=0).astype(BF16) for r in (uf_ref, ub_ref))
        uf = uf_ref[...].reshape(rows, S5_WIDTH).astype(BF16)
        ub = ub_ref[...].reshape(rows, S5_WIDTH).astype(BF16)
        u2f = jnp.dot(pf_ref[...], uf, preferred_element_type=F32).astype(BF16)
        u2b = jnp.dot(pb_ref[...], ub, preferred_element_type=F32).astype(BF16)
        return [jnp.concatenate([u2f[:, s * LANES:(s + 1) * LANES], u2b[:, s * LANES:(s + 1) * LANES]], axis=1)
                for s in range(S5_SLABS)]

    def bu_slab(lhs, s, dst_ref):
        cols = slice(s * S5_SLAB_COLS, (s + 1) * S5_SLAB_COLS)
        if use_perm:
            dst_ref[:, cols] = jnp.dot(lhs[s], wbu_ref[s], preferred_element_type=F32)
            return
        uf, ub = lhs
        bf = jnp.dot(uf[:, s * LANES:(s + 1) * LANES], wbu_ref[s, :LANES, :], preferred_element_type=F32)
        bb = jnp.dot(ub[:, s * LANES:(s + 1) * LANES], wbu_ref[s, LANES:, :], preferred_element_type=F32)
        for t in range(tc):
            dst_ref[t * rs:t * rs + nb, cols] = bf[t * nb:(t + 1) * nb]
            dst_ref[t * rs + nb:(t + 1) * rs, cols] = bb[(tc - 1 - t) * nb:(tc - t) * nb]

    @pl.when(i == 0)
    def _():
        st_ref[...] = h0_ref[...]
        lhs0 = packed_inputs(uf0_ref, ub0_ref)
        for s in range(S5_SLABS):
            bu_slab(lhs0, s, bua_ref)

    half_step = functools.partial(_s5_half_step, wc_ref=wc_ref, are_ref=are_ref, aim_ref=aim_ref, hb_ref=hb_ref,
                                  st_ref=st_ref, bu_slab=bu_slab, rs=rs, tc=tc, tiles_per_pass=tiles_per_pass)
    y2 = half_step(bua_ref, bub_ref, packed_inputs(uf1_ref, ub1_ref))
    _s5_store_unpacked(y2, yf_ref, yb_ref, 0, tc, nb, tc)
    y2 = half_step(bub_ref, bua_ref, packed_inputs(uf2_ref, ub2_ref))
    _s5_store_unpacked(y2, yf_ref, yb_ref, tc, 0, nb, tc)
    hfin_ref[...] = st_ref[...]


def _s5_half_step(cur_ref, nxt_ref, lhs_next, *, wc_ref, are_ref, aim_ref, hb_ref, st_ref, bu_slab,
                  rs, tc, tiles_per_pass):
    slab_tiles = S5_SLAB_NC // LANES
    grp = max(1, 16 // rs)
    ys = []
    for s in range(S5_SLABS):
        bu_slab(lhs_next, s, nxt_ref)
        for c0 in range(slab_tiles * s, slab_tiles * (s + 1), tiles_per_pass):
            cts = list(range(c0, c0 + tiles_per_pass))
            cre = [(ct // slab_tiles) * S5_SLAB_COLS + (ct % slab_tiles) * LANES for ct in cts]
            cim = [c + S5_SLAB_NC for c in cre]
            a_re = [are_ref[:, ct * LANES:(ct + 1) * LANES] for ct in cts]
            a_im = [aim_ref[:, ct * LANES:(ct + 1) * LANES] for ct in cts]
            h_re = [st_ref[:, c:c + LANES] for c in cre]
            h_im = [st_ref[:, c:c + LANES] for c in cim]
            for t0 in range(0, tc, grp):
                out_re = [[] for _ in cts]
                out_im = [[] for _ in cts]
                for t in range(t0, t0 + grp):
                    r0 = t * rs
                    for j in range(len(cts)):
                        b_re = cur_ref[r0:r0 + rs, cre[j]:cre[j] + LANES]
                        b_im = cur_ref[r0:r0 + rs, cim[j]:cim[j] + LANES]
                        n_re = a_re[j] * h_re[j] - a_im[j] * h_im[j] + b_re
                        n_im = a_re[j] * h_im[j] + a_im[j] * h_re[j] + b_im
                        h_re[j], h_im[j] = n_re, n_im
                        out_re[j].append(n_re)
                        out_im[j].append(n_im)
                for j in range(len(cts)):
                    blk_re = out_re[j][0] if grp == 1 else jnp.concatenate(out_re[j], axis=0)
                    blk_im = out_im[j][0] if grp == 1 else jnp.concatenate(out_im[j], axis=0)
                    hb_ref[t0 * rs:(t0 + grp) * rs, cre[j]:cre[j] + LANES] = blk_re.astype(BF16)
                    hb_ref[t0 * rs:(t0 + grp) * rs, cim[j]:cim[j] + LANES] = blk_im.astype(BF16)
            for j in range(len(cts)):
                st_ref[:, cre[j]:cre[j] + LANES] = h_re[j]
                st_ref[:, cim[j]:cim[j] + LANES] = h_im[j]
        ys.append(jnp.dot(hb_ref[:, s * S5_SLAB_COLS:(s + 1) * S5_SLAB_COLS], wc_ref[s], preferred_element_type=F32))
    return jnp.concatenate(ys, axis=1)


def _s5_call(u, l, w_bu, w_c, are, aim, h0, tc, tiles_per_pass):
    nb, length, _ = u.shape
    n = length // tc
    rows = nb * tc
    const = lambda shape: pl.BlockSpec(shape, lambda i: (0,) * len(shape))
    layer = lambda shape: pl.BlockSpec((None,) + shape, lambda i: (l,) + (0,) * len(shape))
    blk = lambda chunk: pl.BlockSpec((nb, tc, S5_WIDTH), lambda i: (0, chunk(i), 0))
    out_f = pl.BlockSpec((nb, 2 * tc, S5_WIDTH), lambda i: (0, i, 0))
    out_b = pl.BlockSpec((nb, 2 * tc, S5_WIDTH), lambda i: (0, n // 2 - 1 - i, 0))
    second = lambda i: jnp.minimum(2 * i + 2, n - 1)
    in_specs = [blk(lambda i: 2 * i + 1), blk(lambda i: n - 2 - 2 * i), blk(second), blk(lambda i: n - 1 - second(i)),
                blk(lambda i: 0), blk(lambda i: n - 1)]
    args = [u] * 6
    if nb % 8 != 0:
        pf, pb = _s5_perms(nb, tc)
        in_specs += [const((2 * rows, rows)), const((2 * rows, rows))]
        args += [jnp.asarray(pf, BF16), jnp.asarray(pb, BF16)]
    in_specs += [layer((S5_SLABS, 2 * LANES, S5_SLAB_COLS)), layer((S5_SLABS, S5_SLAB_COLS, LANES)),
                 layer((2 * nb, S5_NC)), layer((2 * nb, S5_NC)), const((2 * nb, 2 * S5_NC))]
    args += [w_bu, w_c, are, aim, h0]
    return pl.pallas_call(
        functools.partial(_s5_kernel, nb=nb, tc=tc, tiles_per_pass=tiles_per_pass),
        grid=(n // 2,),
        in_specs=in_specs,
        out_specs=(out_f, out_b, const((2 * nb, 2 * S5_NC))),
        out_shape=(jax.ShapeDtypeStruct(u.shape, F32), jax.ShapeDtypeStruct(u.shape, F32),
                   jax.ShapeDtypeStruct((2 * nb, 2 * S5_NC), F32)),
        scratch_shapes=[pltpu.VMEM((2 * rows, 2 * S5_NC), F32), pltpu.VMEM((2 * rows, 2 * S5_NC), F32),
                        pltpu.VMEM((2 * rows, 2 * S5_NC), BF16), pltpu.VMEM((2 * nb, 2 * S5_NC), F32)],
        compiler_params=pltpu.CompilerParams(dimension_semantics=("arbitrary",), vmem_limit_bytes=VMEM_LIMIT),
        name="s5_scan",
    )(*args)


def _gla_kernel(*refs, length, has_s0, n_prev, nsq, cpi):
    q_ref, k_ref, v_ref, gl_ref, wg_ref, bg_ref = refs[:6]
    s0_ref = refs[6] if has_s0 else None
    prev_refs = refs[6 + has_s0:6 + has_s0 + n_prev]
    o_ref, sfin_ref, sf_ref, sb_ref = refs[-4:]
    c = GLA_CHUNK
    n = length // c
    ri = lax.broadcasted_iota(jnp.int32, (c, c), 0)
    ci = lax.broadcasted_iota(jnp.int32, (c, c), 1)
    lane = lax.broadcasted_iota(jnp.int32, (1, GLA_KEY), 1)
    head_mask = [(lane >= h * GLA_DK) & (lane < (h + 1) * GLA_DK) for h in range(GLA_HEADS)]
    causal = [ci <= ri, ci >= ri]
    tri = [m.astype(BF16) for m in causal]
    end = (c - 1, 0)
    chains = [(sq, d, j) for j in range(cpi) for sq in range(nsq) for d in (0, 1)]
    key_rows = lambda h: slice(h * GLA_DK, (h + 1) * GLA_DK)
    val_cols = lambda h: slice(h * GLA_DV, (h + 1) * GLA_DV)

    sb_ref[...] = jnp.zeros_like(sb_ref)
    for sq in range(nsq):
        for d in (0, 1):
            for h in range(GLA_HEADS):
                if has_s0:
                    sf_ref[sq, d, h] = s0_ref[sq, d, h]
                    sb_ref[sq, d, key_rows(h), val_cols(h)] = s0_ref[sq, d, h].astype(BF16)
                else:
                    sf_ref[sq, d, h] = jnp.zeros((GLA_DK, GLA_DV), F32)

    def chunk_group(r):
        row = lambda d, j: r[d][j]
        qc = [q_ref[sq, pl.ds(row(d, j), c), :] * (GLA_DK ** -0.5) for sq, d, j in chains]
        kc = [k_ref[sq, pl.ds(row(d, j), c), :] for sq, d, j in chains]
        vc = [v_ref[sq, pl.ds(row(d, j), c), :] for sq, d, j in chains]
        logits = [_bdot(gl_ref[sq, pl.ds(row(d, j), c), :], wg_ref[d]) + bg_ref[d] for sq, d, j in chains]
        g = [_split2(_log_sigmoid(x) * (1.0 / GLA_GATE_NORM)) for x in logits]
        b = [jnp.dot(tri[d], hi, preferred_element_type=F32) + jnp.dot(tri[d], lo, preferred_element_type=F32)
             for (sq, d, j), (hi, lo) in zip(chains, g)]
        bt = [x.T for x in b]
        kt = [x.T for x in kc]
        b_mid = [x[c // 2:c // 2 + 1, :] for x in b]
        bt_mid = [x[:, c // 2:c // 2 + 1] for x in bt]
        bt_end = [x[:, end[d]:end[d] + 1] for (sq, d, j), x in zip(chains, bt)]
        q_in = [(x * jnp.exp(y)).astype(BF16) for x, y in zip(qc, b)]
        q_mid = [(x * jnp.exp(y - m)).astype(BF16) for x, y, m in zip(qc, b, b_mid)]
        k_mid_t = [(x * jnp.exp(m - y)).astype(BF16) for x, y, m in zip(kt, bt, bt_mid)]
        k_end_t = [(x * jnp.exp(e - y)).astype(BF16) for x, y, e in zip(kt, bt, bt_end)]
        q_st = [jnp.concatenate([jnp.where(head_mask[h], x, jnp.zeros_like(x)) for h in range(GLA_HEADS)], axis=0)
                for x in q_mid]
        a = [jnp.dot(x, y, preferred_element_type=F32) for x, y in zip(q_st, k_mid_t)]
        parts = [[] for _ in chains]
        for h in range(GLA_HEADS):
            for i, (sq, d, j) in enumerate(chains):
                a_h = jnp.where(causal[d], a[i][h * c:(h + 1) * c, :], 0.0).astype(BF16)
                parts[i].append(jnp.dot(a_h, vc[i][:, val_cols(h)], preferred_element_type=F32))
        upd = [[jnp.dot(k_end_t[i][key_rows(h), :], vc[i][:, val_cols(h)], preferred_element_type=F32)
                for h in range(GLA_HEADS)] for i in range(len(chains))]
        o_inter = [None] * len(chains)
        for i, (sq, d, j) in enumerate(chains):
            o_inter[i] = jnp.dot(q_in[i], sb_ref[sq, d], preferred_element_type=F32)
            for h in range(GLA_HEADS):
                s_new = jnp.exp(bt_end[i][key_rows(h), :]) * sf_ref[sq, d, h] + upd[i][h]
                sf_ref[sq, d, h] = s_new
                sb_ref[sq, d, key_rows(h), val_cols(h)] = s_new.astype(BF16)
        return [x + jnp.concatenate(p, axis=1) for x, p in zip(o_inter, parts)]

    def rows_of(step):
        fwd = [pl.multiple_of((step * cpi + j) * c, c) for j in range(cpi)]
        bwd = [pl.multiple_of((n - 1 - step * cpi - j) * c, c) for j in range(cpi)]
        return fwd, bwd

    def first_half(step, carry):
        r = rows_of(step)
        for (sq, d, j), o in zip(chains, chunk_group(r)):
            o_ref[sq, pl.ds(r[d][j], c), :] = o
        return carry

    def second_half(step, carry):
        r = rows_of(step)
        for (sq, d, j), o in zip(chains, chunk_group(r)):
            o_ref[sq, pl.ds(r[d][j], c), :] = o + o_ref[sq, pl.ds(r[d][j], c), :]
        return carry

    steps = n // cpi
    lax.fori_loop(0, steps // 2, first_half, 0)
    lax.fori_loop(steps // 2, steps, second_half, 0)
    if n_prev:
        for j, prev_ref in enumerate(prev_refs):
            sfin_ref[:, j] = prev_ref[...]
        sfin_ref[:, n_prev] = sf_ref[...]
    else:
        sfin_ref[...] = sf_ref[...]


def _gla_call(q, k, v, gl, l, wg, bg, s0, nsq, cpi, prev_states=()):
    nseq, length, _ = q.shape
    assert (length // GLA_CHUNK) % (2 * cpi) == 0 and nseq % nsq == 0
    seq = lambda width: pl.BlockSpec((nsq, length, width), lambda b: (b, 0, 0))
    layer = lambda shape: pl.BlockSpec((None,) + shape, lambda b: (l,) + (0,) * len(shape))
    in_specs = [seq(GLA_KEY), seq(GLA_KEY), seq(GLA_WIDTH), seq(LANES),
                layer((2, LANES, GLA_KEY)), layer((2, 1, GLA_KEY))]
    args = [q, k, v, gl, wg, bg]
    if s0 is not None:
        in_specs.append(pl.BlockSpec((nsq, None, 2, GLA_HEADS, GLA_DK, GLA_DV), lambda b: (b, l, 0, 0, 0, 0)))
        args.append(s0)
    state = (2, GLA_HEADS, GLA_DK, GLA_DV)
    state_spec = pl.BlockSpec((nsq,) + state, lambda b: (b, 0, 0, 0, 0))
    in_specs += [state_spec] * len(prev_states)
    args += list(prev_states)
    lead = (len(prev_states) + 1,) if prev_states else ()
    return pl.pallas_call(
        functools.partial(_gla_kernel, length=length, has_s0=s0 is not None, n_prev=len(prev_states), nsq=nsq, cpi=cpi),
        grid=(nseq // nsq,),
        in_specs=in_specs,
        out_specs=(seq(GLA_WIDTH), pl.BlockSpec((nsq,) + lead + state, lambda b: (b,) + (0,) * len(lead + state))),
        out_shape=(jax.ShapeDtypeStruct((nseq, length, GLA_WIDTH), F32),
                   jax.ShapeDtypeStruct((nseq,) + lead + state, F32)),
        scratch_shapes=[pltpu.VMEM((nsq, 2, GLA_HEADS, GLA_DK, GLA_DV), F32),
                        pltpu.VMEM((nsq, 2, GLA_KEY, GLA_WIDTH), BF16)],
        compiler_params=pltpu.CompilerParams(dimension_semantics=("arbitrary",), vmem_limit_bytes=VMEM_LIMIT),
        name="gla",
    )(*args)


def _mix_kernel(*refs, has_pe, last, row_groups):
    if has_pe:
        (x_ref, pe_ref, yf_ref, yb_ref, u_ref, o_ref, shift_ref, scale_ref, gate_ref, ng_ref, wg_ref, d_ref,
         wglu_ref, bglu_ref, wpa_ref, wpb_ref, wo_ref, fng_ref, gng_ref, out_ref) = refs
    else:
        (x_ref, yf_ref, yb_ref, u_ref, o_ref, shift_ref, scale_ref, gate_ref, ng_ref, wg_ref, d_ref,
         wglu_ref, bglu_ref, wpa_ref, wpb_ref, wo_ref, fng_ref, gng_ref, out_ref) = refs

    def head_norm(o):
        heads = [o[:, h * GLA_DV:(h + 1) * GLA_DV] for h in range(GLA_HEADS)]
        heads = [v * lax.rsqrt(jnp.mean(v * v, axis=-1, keepdims=True) + EPS) for v in heads]
        return jnp.concatenate(heads, axis=1) * gng_ref[...]

    nsb, tm, _ = x_ref.shape
    grp = [(i, slice(j * tm // row_groups, (j + 1) * tm // row_groups)) for i in range(nsb) for j in range(row_groups)]
    x = [x_ref[i, r, :] + pe_ref[r, :] if has_pe else x_ref[i, r, :] for i, r in grp]
    h = [_modulated_norm(v, ng_ref[...], 1.0 + scale_ref[...], shift_ref[...]).astype(BF16) for v in x]
    gts = [_bdot_t(v, wg_ref[...]) for v in h]
    y = [_gelu_tanh(yf_ref[i, r, :] + yb_ref[i, r, :] + d_ref[...] * u_ref[i, r, :]) for i, r in grp]
    glu = [_bdot(v, wglu_ref[...]) for v in y]
    y = [v * _sigmoid(g + bglu_ref[...]) for v, g in zip(y, glu)]
    y_a = [v * _silu(g[:, :S5_WIDTH]) for v, g in zip(y, gts)]
    y_b = [head_norm(o_ref[i, r, :]) * _silu(g[:, S5_WIDTH:2 * S5_WIDTH]) for (i, r), g in zip(grp, gts)]
    p_a = [_bdot(v, wpa_ref[...]) for v in y_a]
    p_b = [_bdot(v, wpb_ref[...]) for v in y_b]
    merged = [_sigmoid(g[:, 2 * S5_WIDTH:2 * S5_WIDTH + D_MODEL]) * a + _sigmoid(g[:, 2 * S5_WIDTH + D_MODEL:]) * b
              for g, a, b in zip(gts, p_a, p_b)]
    xn = [v + gate_ref[...] * _bdot(m, wo_ref[...]) for v, m in zip(x, merged)]
    for (i, r), v in zip(grp, xn):
        if last:
            ms = jnp.mean(v * v, axis=-1, keepdims=True)
            v = v * lax.rsqrt(ms + EPS) * fng_ref[...]
        out_ref[i, r, :] = v


def _mix_call(x, pe, yf, yb, u, o, mod, l, cond_row, ng, wg, s5_d, wglu, bglu, wpa, wpb, wo, fng, gng, tm, last,
              nsb=1):
    nseq, length, _ = x.shape
    row = lambda width: pl.BlockSpec((nsb, tm, width), lambda b, j: (b, j, 0))
    in_specs = [row(D_MODEL)]
    args = [x]
    if pe is not None:
        in_specs.append(pl.BlockSpec((tm, D_MODEL), lambda b, j: (j, 0)))
        args.append(pe)
    in_specs += [row(S5_WIDTH), row(S5_WIDTH), row(S5_WIDTH), row(GLA_WIDTH),
                 _mod_spec(l, cond_row, 0), _mod_spec(l, cond_row, 1), _mod_spec(l, cond_row, 2),
                 _layer_spec(l, (1, D_MODEL)), _layer_spec(l, (GATE_W, D_MODEL)), _layer_spec(l, (1, S5_WIDTH)),
                 _layer_spec(l, (S5_WIDTH, S5_WIDTH)), _layer_spec(l, (1, S5_WIDTH)),
                 _layer_spec(l, (S5_WIDTH, D_MODEL)), _layer_spec(l, (GLA_WIDTH, D_MODEL)),
                 _layer_spec(l, (D_MODEL, D_MODEL)), pl.BlockSpec((1, D_MODEL), lambda b, j: (0, 0)),
                 _layer_spec(l, (1, GLA_WIDTH))]
    args += [yf, yb, u, o, mod, mod, mod, ng, wg, s5_d, wglu, bglu, wpa, wpb, wo, fng, gng]
    return pl.pallas_call(
        functools.partial(_mix_kernel, has_pe=pe is not None, last=last, row_groups=max(1, tm // MXU_TILE)),
        grid=(nseq // nsb, length // tm),
        in_specs=in_specs,
        out_specs=row(D_MODEL),
        out_shape=jax.ShapeDtypeStruct(x.shape, F32),
        compiler_params=pltpu.CompilerParams(dimension_semantics=("arbitrary", "arbitrary"),
                                             vmem_limit_bytes=VMEM_LIMIT),
        name="mix",
    )(*args)


_CONTEXT_BLOCKS = dict(proj=dict(tm=None, nsb=4), s5=dict(tc=16, tiles_per_pass=2), gla=dict(nsq=4, cpi=1),
                       mix=dict(tm=None, nsb=2))
_LATENT_BLOCKS = dict(proj=dict(tm=1024), s5=dict(tc=32, tiles_per_pass=4), gla=dict(nsq=1, cpi=4), mix=dict(tm=512))


def _blocks(choice, length):
    return {k: dict(v, tm=v["tm"] or length) if "tm" in v else v for k, v in choice.items()}


def _grid_pos_embed(length, dim):
    rows = length // GRID_W
    quarter = dim // 4
    freqs = jnp.exp(-math.log(10000.0) * jnp.arange(quarter, dtype=F32) / quarter)

    def sincos(pos):
        ang = pos.astype(F32)[:, None] * freqs[None, :]
        return jnp.concatenate([jnp.sin(ang), jnp.cos(ang)], axis=-1)

    er = sincos(jnp.arange(rows))
    ec = sincos(jnp.arange(GRID_W))
    pe = jnp.concatenate([jnp.broadcast_to(er[:, None, :], (rows, GRID_W, dim // 2)),
                          jnp.broadcast_to(ec[None, :, :], (rows, GRID_W, dim // 2))], axis=-1)
    return pe.reshape(rows * GRID_W, dim)


def kernel(x_prompt, x_sample, c, state_s5_re, state_s5_im, state_gla, c_ctx, norm_g, w_mod, b_mod, w_in, gla_wg_up,
           gla_bg, gla_norm_g, s5_lam_re, s5_lam_im, s5_log_dt, s5_b_re, s5_b_im, s5_c_re, s5_c_im, s5_d, w_glu,
           b_glu, w_pa, w_pb, w_o, final_norm_g):
    depth = w_in.shape[0]
    bp, lp, _ = x_prompt.shape
    bs, ls, _ = x_sample.shape

    cond8 = jnp.concatenate([c_ctx[None], c, jnp.zeros((8 - 1 - bs, D_MODEL), F32)], axis=0)
    mod = _mod_call(cond8, w_mod, b_mod)

    a_re, a_im, w_bu, w_c = _s5_params(s5_lam_re, s5_lam_im, s5_log_dt, s5_b_re, s5_b_im, s5_c_re, s5_c_im)

    w_proj, w_gate = _split_w_in(w_in)
    wg_up = jnp.concatenate([gla_wg_up, jnp.zeros((depth, 2, LANES - GLA_RANK, GLA_KEY), F32)], axis=2).astype(BF16)
    w_glu_b, w_pa_b, w_pb_b, w_o_b = (w.astype(BF16) for w in (w_glu, w_pa, w_pb, w_o))

    vec = lambda a: a[:, None, :]
    mod = mod.reshape(depth, 8, 1, 3 * D_MODEL)
    ng, gng, bg = vec(norm_g), vec(gla_norm_g), gla_bg[:, :, None, :]
    common = (ng, w_gate, vec(s5_d), w_glu_b, vec(b_glu), w_pa_b, w_pb_b, w_o_b, final_norm_g[None], gng)
    are_p, aim_p = jnp.repeat(a_re, bp, axis=1), jnp.repeat(a_im, bp, axis=1)
    are_s, aim_s = jnp.repeat(a_re, bs, axis=1), jnp.repeat(a_im, bs, axis=1)
    ctx_row = lambda b: 0
    cond_row = lambda b: b + 1

    pe = _grid_pos_embed(ls, D_MODEL)
    blk_p, blk_s = _blocks(_CONTEXT_BLOCKS, lp), _blocks(_LATENT_BLOCKS, ls)
    zero_h0 = jnp.zeros((2 * bp, 2 * S5_NC), F32)
    xp, xs = x_prompt, x_sample
    new_re, new_im, new_gla = [], [], []
    for l in range(depth):
        last = l == depth - 1

        u, q, k, v, gl = _proj_call(xp, None, mod, l, ctx_row, ng, w_proj, **blk_p["proj"])
        yf, yb, hfin = _s5_call(u, l, w_bu, w_c, are_p, aim_p, zero_h0, **blk_p["s5"])
        o, sfin = _gla_call(q, k, v, gl, l, wg_up, bg, None, prev_states=new_gla if last else (), **blk_p["gla"])
        new_gla.append(sfin)
        xp = _mix_call(xp, None, yf, yb, u, o, mod, l, ctx_row, *common, last=last, **blk_p["mix"])
        re_l, im_l = _s5_cols_to_state(hfin.reshape(2, bp, 2 * S5_NC))
        new_re.append(jnp.swapaxes(re_l, 0, 1))
        new_im.append(jnp.swapaxes(im_l, 0, 1))

        pe_l = pe if l == 0 else None
        u, q, k, v, gl = _proj_call(xs, pe_l, mod, l, cond_row, ng, w_proj, **blk_s["proj"])
        h0 = _s5_state_to_cols(jnp.swapaxes(state_s5_re[:, l], 0, 1), jnp.swapaxes(state_s5_im[:, l], 0, 1))
        yf, yb, _ = _s5_call(u, l, w_bu, w_c, are_s, aim_s, h0.reshape(2 * bs, 2 * S5_NC), **blk_s["s5"])
        o, _ = _gla_call(q, k, v, gl, l, wg_up, bg, state_gla, **blk_s["gla"])
        xs = _mix_call(xs, pe_l, yf, yb, u, o, mod, l, cond_row, *common, last=last, **blk_s["mix"])

    new_gla = new_gla[-1] if depth > 1 else new_gla[0][:, None]
    return (xp, xs, jnp.stack(new_re, axis=1), jnp.stack(new_im, axis=1), new_gla)
```

```python
import functools
import math

import numpy as np
import jax
import jax.numpy as jnp
from jax import lax
from jax.experimental import pallas as pl
from jax.experimental.pallas import tpu as pltpu

F32 = jnp.float32
BF16 = jnp.bfloat16

D_MODEL = 1024
GRID_W = 64
S5_WIDTH = 512
S5_GROUPS = 32
S5_CH = 16
S5_STATE = 64
S5_NC = S5_GROUPS * S5_STATE
S5_SLAB_GROUPS = 8
S5_SLABS = S5_GROUPS // S5_SLAB_GROUPS
S5_SLAB_NC = S5_SLAB_GROUPS * S5_STATE
S5_SLAB_COLS = 2 * S5_SLAB_NC
GLA_HEADS = 4
GLA_DK = 64
GLA_DV = 128
GLA_KEY = GLA_HEADS * GLA_DK
GLA_WIDTH = GLA_HEADS * GLA_DV
GLA_RANK = 16
GLA_GATE_NORM = 16.0
GLA_CHUNK = 128
EPS = 1e-6
LANES = 128
PROJ_W = S5_WIDTH + 2 * GLA_KEY + GLA_WIDTH + LANES
GATE_W = 2 * S5_WIDTH + 2 * D_MODEL
VMEM_LIMIT = 60 * 1024 * 1024
MXU_TILE = 256


def _bdot(a, b):
    return jnp.dot(a.astype(BF16), b.astype(BF16), preferred_element_type=F32)


def _bdot_t(a, bt):
    return lax.dot_general(a.astype(BF16), bt.astype(BF16), (((1,), (1,)), ((), ())), preferred_element_type=F32)


def _split2(x):
    hi = x.astype(BF16)
    lo = (x - hi.astype(F32)).astype(BF16)
    return hi, lo


def _sigmoid(x):
    return 1.0 / (1.0 + jnp.exp2(x * (-math.log2(math.e))))


def _silu(x):
    return x * _sigmoid(x)


def _gelu_tanh(x):
    c = math.sqrt(2.0 / math.pi)
    return 0.5 * x * (1.0 + jnp.tanh(c * (x + 0.044715 * (x * x * x))))


def _log_sigmoid(x):
    return jnp.minimum(x, 0.0) - jnp.log(1.0 + jnp.exp(-jnp.abs(x)))


def _modulated_norm(x, ng, scale1p, shift):
    ms = jnp.mean(x * x, axis=-1, keepdims=True)
    return (x * lax.rsqrt(ms + EPS) * ng) * scale1p + shift


def _mod_kernel(cond_ref, w_ref, b_ref, out_ref):
    c = cond_ref[...]
    s_hi, s_lo = _split2(_silu(c))
    w_hi, w_lo = _split2(w_ref[0])
    acc = jnp.dot(s_hi, w_hi, preferred_element_type=F32)
    acc += jnp.dot(s_lo, w_hi, preferred_element_type=F32)
    acc += jnp.dot(s_hi, w_lo, preferred_element_type=F32)
    out_ref[0] = acc + b_ref[0]


def _mod_call(cond8, w_mod, b_mod):
    depth = w_mod.shape[0]
    nb = 1536
    return pl.pallas_call(
        _mod_kernel,
        grid=(depth, 3 * D_MODEL // nb),
        in_specs=[
            pl.BlockSpec((8, D_MODEL), lambda l, j: (0, 0)),
            pl.BlockSpec((1, D_MODEL, nb), lambda l, j: (l, 0, j)),
            pl.BlockSpec((1, 1, nb), lambda l, j: (l, 0, j)),
        ],
        out_specs=pl.BlockSpec((1, 8, nb), lambda l, j: (l, 0, j)),
        out_shape=jax.ShapeDtypeStruct((depth, 8, 3 * D_MODEL), F32),
        compiler_params=pltpu.CompilerParams(dimension_semantics=("arbitrary", "arbitrary")),
        name="adaln_mod",
    )(cond8, w_mod, b_mod.reshape(depth, 1, 3 * D_MODEL))


def _zoh(lre, lim, ldt):
    dt = jnp.exp(ldt)
    mag = jnp.exp(lre * dt)
    a_re = mag * jnp.cos(lim * dt)
    a_im = mag * jnp.sin(lim * dt)
    n_re = a_re - 1.0
    inv = 1.0 / (lre * lre + lim * lim)
    return a_re, a_im, (n_re * lre + a_im * lim) * inv, (a_im * lre - n_re * lim) * inv


def _zoh_kernel(lre_ref, lim_ref, ldt_ref, bre_ref, bim_ref, cre_ref, cim_ref, are_ref, aim_ref, wbu_ref, wc_ref,
                *, depth):
    a_re, a_im, c_re, c_im = _zoh(lre_ref[...], lim_ref[...], ldt_ref[...])
    are_ref[...] = a_re
    aim_ref[...] = a_im
    cmat = [cre_ref[...], -cim_ref[...]]
    wbu_ref[...] = jnp.zeros_like(wbu_ref)
    wc_ref[...] = jnp.zeros_like(wc_ref)
    for l in range(depth):
        for s in range(S5_SLABS):
            for gl in range(S5_SLAB_GROUPS):
                g = S5_SLAB_GROUPS * s + gl
                src = (l * S5_GROUPS + g) * S5_CH
                bre = bre_ref[src:src + S5_CH, :]
                bim = bim_ref[src:src + S5_CH, :]
                for d in range(2):
                    r = (l * 2 + d) * S5_GROUPS + g
                    k_re, k_im = c_re[r:r + 1, :], c_im[r:r + 1, :]
                    bbar = [k_re * bre - k_im * bim, k_re * bim + k_im * bre]
                    row = (d * S5_SLAB_GROUPS + gl) * S5_CH
                    for ri in range(2):
                        col = ri * S5_SLAB_NC + gl * S5_STATE
                        wbu_ref[l, s, row:row + S5_CH, col:col + S5_STATE] = bbar[ri].astype(BF16)
                src = (l * S5_GROUPS + g) * S5_STATE
                for ri in range(2):
                    col = ri * S5_SLAB_NC + gl * S5_STATE
                    wc_ref[l, s, col:col + S5_STATE, gl * S5_CH:(gl + 1) * S5_CH] = (
                        cmat[ri][src:src + S5_STATE, :].astype(BF16))


def _s5_params(s5_lam_re, s5_lam_im, s5_log_dt, s5_b_re, s5_b_im, s5_c_re, s5_c_im):
    depth = s5_lam_re.shape[0]
    small = lambda a: a.reshape(depth * 2 * S5_GROUPS, -1)
    swapped = lambda a: jnp.swapaxes(a, -1, -2).reshape(-1, a.shape[-2])
    a_re, a_im, w_bu, w_c = pl.pallas_call(
        functools.partial(_zoh_kernel, depth=depth),
        out_shape=(jax.ShapeDtypeStruct((depth * 2 * S5_GROUPS, S5_STATE), F32),
                   jax.ShapeDtypeStruct((depth * 2 * S5_GROUPS, S5_STATE), F32),
                   jax.ShapeDtypeStruct((depth, S5_SLABS, 2 * LANES, S5_SLAB_COLS), BF16),
                   jax.ShapeDtypeStruct((depth, S5_SLABS, S5_SLAB_COLS, LANES), BF16)),
        name="s5_zoh",
    )(small(s5_lam_re), small(s5_lam_im), small(s5_log_dt), swapped(s5_b_re), swapped(s5_b_im),
      swapped(s5_c_re), swapped(s5_c_im))
    return a_re.reshape(depth, 2, S5_NC), a_im.reshape(depth, 2, S5_NC), w_bu, w_c


def _s5_state_to_cols(re, im):
    lead = re.shape[:-2]
    st = jnp.stack([re.reshape(lead + (S5_SLABS, S5_SLAB_NC)), im.reshape(lead + (S5_SLABS, S5_SLAB_NC))], axis=-2)
    return st.reshape(lead + (2 * S5_NC,))


def _s5_cols_to_state(cols):
    lead = cols.shape[:-1]
    st = cols.reshape(lead + (S5_SLABS, 2, S5_SLAB_NC))
    re = st[..., 0, :].reshape(lead + (S5_GROUPS, S5_STATE))
    im = st[..., 1, :].reshape(lead + (S5_GROUPS, S5_STATE))
    return re, im


_IN_OFFS = [int(o) for o in np.cumsum([0, S5_WIDTH, S5_WIDTH, GLA_KEY, GLA_KEY, GLA_WIDTH, GLA_WIDTH, GLA_RANK,
                                       D_MODEL, D_MODEL])]


def _split_w_in_kernel(u_ref, qkv_ref, code_ref, ga_ref, gb_ref, m_ref, proj_ref, gate_ref):
    rows = 0
    for src in (u_ref, qkv_ref, code_ref):
        proj_ref[rows:rows + src.shape[1], :] = src[0].astype(BF16)
        rows += src.shape[1]
    proj_ref[rows:, :] = jnp.zeros((PROJ_W - rows, proj_ref.shape[1]), BF16)
    rows = 0
    for src in (ga_ref, gb_ref, m_ref):
        gate_ref[rows:rows + src.shape[1], :] = src[0].astype(BF16)
        rows += src.shape[1]


def _split_w_in(w_in):
    o = _IN_OFFS
    depth = w_in.shape[0]
    wt = jnp.swapaxes(w_in, 1, 2)
    tn = D_MODEL // 2
    rows = lambda lo, hi: pl.BlockSpec((pl.Element(1), pl.Element(hi - lo), pl.Element(tn)),
                                       lambda l, j: (l, lo, j * tn))
    out = lambda n: pl.BlockSpec((None, n, tn), lambda l, j: (l, 0, j))
    return pl.pallas_call(
        _split_w_in_kernel,
        grid=(depth, D_MODEL // tn),
        in_specs=[rows(o[0], o[1]), rows(o[2], o[5]), rows(o[6], o[7]), rows(o[1], o[2]), rows(o[5], o[6]),
                  rows(o[7], o[9])],
        out_specs=(out(PROJ_W), out(GATE_W)),
        out_shape=(jax.ShapeDtypeStruct((depth, PROJ_W, D_MODEL), BF16),
                   jax.ShapeDtypeStruct((depth, GATE_W, D_MODEL), BF16)),
        compiler_params=pltpu.CompilerParams(dimension_semantics=("arbitrary", "arbitrary"),
                                             vmem_limit_bytes=VMEM_LIMIT),
        name="split_w_in",
    )(wt, wt, wt, wt, wt, wt)


def _proj_kernel(*refs, has_pe):
    if has_pe:
        x_ref, pe_ref, shift_ref, scale_ref, ng_ref, w_ref, u_ref, q_ref, k_ref, v_ref, gl_ref = refs
    else:
        x_ref, shift_ref, scale_ref, ng_ref, w_ref, u_ref, q_ref, k_ref, v_ref, gl_ref = refs
    nsb, tm, _ = x_ref.shape
    rg = max(1, tm // (2 * MXU_TILE))
    grp = [(i, slice(j * tm // rg, (j + 1) * tm // rg)) for i in range(nsb) for j in range(rg)]
    x = [x_ref[i, r, :] + pe_ref[r, :] if has_pe else x_ref[i, r, :] for i, r in grp]
    h = [_modulated_norm(v, ng_ref[...], 1.0 + scale_ref[...], shift_ref[...]) for v in x]
    p = [_bdot_t(v, w_ref[...]) for v in h]
    for (i, r), v in zip(grp, p):
        o = 0
        for ref, width in ((u_ref, S5_WIDTH), (q_ref, GLA_KEY), (k_ref, GLA_KEY), (v_ref, GLA_WIDTH), (gl_ref, LANES)):
            ref[i, r, :] = v[:, o:o + width].astype(ref.dtype)
            o += width


def _mod_spec(l, cond_row, part):
    return pl.BlockSpec((None, None, 1, D_MODEL), lambda j, b: (l, cond_row(b), 0, part))


def _layer_spec(l, shape):
    return pl.BlockSpec((None,) + shape, lambda j, b: (l,) + (0,) * len(shape))


def _proj_call(x, pe, mod, l, cond_row, ng, w, tm, nsb=1):
    nseq, length, _ = x.shape
    row = lambda width: pl.BlockSpec((nsb, tm, width), lambda j, b: (b, j, 0))
    in_specs = [row(D_MODEL)]
    args = [x]
    if pe is not None:
        in_specs.append(pl.BlockSpec((tm, D_MODEL), lambda j, b: (j, 0)))
        args.append(pe)
    in_specs += [_mod_spec(l, cond_row, 0), _mod_spec(l, cond_row, 1), _layer_spec(l, (1, D_MODEL)),
                 _layer_spec(l, (PROJ_W, D_MODEL))]
    args += [mod, mod, ng, w]
    sds = lambda width, dt: jax.ShapeDtypeStruct((nseq, length, width), dt)
    return pl.pallas_call(
        functools.partial(_proj_kernel, has_pe=pe is not None),
        grid=(length // tm, nseq // nsb),
        in_specs=in_specs,
        out_specs=(row(S5_WIDTH), row(GLA_KEY), row(GLA_KEY), row(GLA_WIDTH), row(LANES)),
        out_shape=(sds(S5_WIDTH, F32), sds(GLA_KEY, F32), sds(GLA_KEY, F32), sds(GLA_WIDTH, BF16), sds(LANES, F32)),
        compiler_params=pltpu.CompilerParams(dimension_semantics=("arbitrary", "arbitrary"),
                                             vmem_limit_bytes=VMEM_LIMIT),
        name="proj",
    )(*args)


def _s5_perms(nb, tc):
    r = nb * tc
    pf = np.zeros((2 * r, r), np.float32)
    pb = np.zeros((2 * r, r), np.float32)
    for t in range(tc):
        for b in range(nb):
            pf[t * 2 * nb + b, b * tc + t] = 1.0
            pb[t * 2 * nb + nb + b, b * tc + (tc - 1 - t)] = 1.0
    return pf, pb


def _s5_store_unpacked(y2, yf_ref, yb_ref, t_f, t_b, nb, tc):
    rs = 2 * nb
    for t in range(tc):
        yf_ref[:, t_f + t, :] = y2[t * rs:t * rs + nb]
        yb_ref[:, t_b + tc - 1 - t, :] = y2[t * rs + nb:(t + 1) * rs]


def _s5_kernel(*refs, nb, tc, tiles_per_pass):
    use_perm = nb % 8 != 0
    uf1_ref, ub1_ref, uf2_ref, ub2_ref, uf0_ref, ub0_ref = refs[:6]
    pf_ref, pb_ref = refs[6:8] if use_perm else (None, None)
    wbu_ref, wc_ref, are_ref, aim_ref, h0_ref, yf_ref, yb_ref, hfin_ref, bua_ref, bub_ref, hb_ref, st_ref = refs[-12:]
    i = pl.program_id(0)
    rows = nb * tc
    rs = 2 * nb

    def packed_inputs(uf_ref, ub_ref):
        if not use_perm:
            return tuple(jnp.concatenate([r[:, t, :] for t in range(tc)], axis=0).astype(BF16) for r in (uf_ref, ub_ref))
        uf = uf_ref[...].reshape(rows, S5_WIDTH).astype(BF16)
        ub = ub_ref[...].reshape(rows, S5_WIDTH).astype(BF16)
        u2f = jnp.dot(pf_ref[...], uf, preferred_element_type=F32).astype(BF16)
        u2b = jnp.dot(pb_ref[...], ub, preferred_element_type=F32).astype(BF16)
        return [jnp.concatenate([u2f[:, s * LANES:(s + 1) * LANES], u2b[:, s * LANES:(s + 1) * LANES]], axis=1)
                for s in range(S5_SLABS)]

    def bu_slab(lhs, s, dst_ref):
        cols = slice(s * S5_SLAB_COLS, (s + 1) * S5_SLAB_COLS)
        if use_perm:
            dst_ref[:, cols] = jnp.dot(lhs[s], wbu_ref[s], preferred_element_type=F32)
            return
        uf, ub = lhs
        bf = jnp.dot(uf[:, s * LANES:(s + 1) * LANES], wbu_ref[s, :LANES, :], preferred_element_type=F32)
        bb = jnp.dot(ub[:, s * LANES:(s + 1) * LANES], wbu_ref[s, LANES:, :], preferred_element_type=F32)
        for t in range(tc):
            dst_ref[t * rs:t * rs + nb, cols] = bf[t * nb:(t + 1) * nb]
            dst_ref[t * rs + nb:(t + 1) * rs, cols] = bb[(tc - 1 - t) * nb:(tc - t) * nb]

    @pl.when(i == 0)
    def _():
        st_ref[...] = h0_ref[...]
        lhs0 = packed_inputs(uf0_ref, ub0_ref)
        for s in range(S5_SLABS):
            bu_slab(lhs0, s, bua_ref)

    half_step = functools.partial(_s5_half_step, wc_ref=wc_ref, are_ref=are_ref, aim_ref=aim_ref, hb_ref=hb_ref,
                                  st_ref=st_ref, bu_slab=bu_slab, rs=rs, tc=tc, tiles_per_pass=tiles_per_pass)
    y2 = half_step(bua_ref, bub_ref, packed_inputs(uf1_ref, ub1_ref))
    _s5_store_unpacked(y2, yf_ref, yb_ref, 0, tc, nb, tc)
    y2 = half_step(bub_ref, bua_ref, packed_inputs(uf2_ref, ub2_ref))
    _s5_store_unpacked(y2, yf_ref, yb_ref, tc, 0, nb, tc)
    hfin_ref[...] = st_ref[...]


def _s5_half_step(cur_ref, nxt_ref, lhs_next, *, wc_ref, are_ref, aim_ref, hb_ref, st_ref, bu_slab,
                  rs, tc, tiles_per_pass):
    slab_tiles = S5_SLAB_NC // LANES
    grp = max(1, 16 // rs)
    ys = []
    for s in range(S5_SLABS):
        bu_slab(lhs_next, s, nxt_ref)
        for c0 in range(slab_tiles * s, slab_tiles * (s + 1), tiles_per_pass):
            cts = list(range(c0, c0 + tiles_per_pass))
            cre = [(ct // slab_tiles) * S5_SLAB_COLS + (ct % slab_tiles) * LANES for ct in cts]
            cim = [c + S5_SLAB_NC for c in cre]
            a_re = [are_ref[:, ct * LANES:(ct + 1) * LANES] for ct in cts]
            a_im = [aim_ref[:, ct * LANES:(ct + 1) * LANES] for ct in cts]
            h_re = [st_ref[:, c:c + LANES] for c in cre]
            h_im = [st_ref[:, c:c + LANES] for c in cim]
            for t0 in range(0, tc, grp):
                out_re = [[] for _ in cts]
                out_im = [[] for _ in cts]
                for t in range(t0, t0 + grp):
                    r0 = t * rs
                    for j in range(len(cts)):
                        b_re = cur_ref[r0:r0 + rs, cre[j]:cre[j] + LANES]
                        b_im = cur_ref[r0:r0 + rs, cim[j]:cim[j] + LANES]
                        n_re = a_re[j] * h_re[j] - a_im[j] * h_im[j] + b_re
                        n_im = a_re[j] * h_im[j] + a_im[j] * h_re[j] + b_im
                        h_re[j], h_im[j] = n_re, n_im
                        out_re[j].append(n_re)
                        out_im[j].append(n_im)
                for j in range(len(cts)):
                    blk_re = out_re[j][0] if grp == 1 else jnp.concatenate(out_re[j], axis=0)
                    blk_im = out_im[j][0] if grp == 1 else jnp.concatenate(out_im[j], axis=0)
                    hb_ref[t0 * rs:(t0 + grp) * rs, cre[j]:cre[j] + LANES] = blk_re.astype(BF16)
                    hb_ref[t0 * rs:(t0 + grp) * rs, cim[j]:cim[j] + LANES] = blk_im.astype(BF16)
            for j in range(len(cts)):
                st_ref[:, cre[j]:cre[j] + LANES] = h_re[j]
                st_ref[:, cim[j]:cim[j] + LANES] = h_im[j]
        ys.append(jnp.dot(hb_ref[:, s * S5_SLAB_COLS:(s + 1) * S5_SLAB_COLS], wc_ref[s], preferred_element_type=F32))
    return jnp.concatenate(ys, axis=1)


def _s5_call(u, l, w_bu, w_c, are, aim, h0, tc, tiles_per_pass):
    nb, length, _ = u.shape
    n = length // tc
    rows = nb * tc
    const = lambda shape: pl.BlockSpec(shape, lambda i: (0,) * len(shape))
    layer = lambda shape: pl.BlockSpec((None,) + shape, lambda i: (l,) + (0,) * len(shape))
    blk = lambda chunk: pl.BlockSpec((nb, tc, S5_WIDTH), lambda i: (0, chunk(i), 0))
    out_f = pl.BlockSpec((nb, 2 * tc, S5_WIDTH), lambda i: (0, i, 0))
    out_b = pl.BlockSpec((nb, 2 * tc, S5_WIDTH), lambda i: (0, n // 2 - 1 - i, 0))
    second = lambda i: jnp.minimum(2 * i + 2, n - 1)
    in_specs = [blk(lambda i: 2 * i + 1), blk(lambda i: n - 2 - 2 * i), blk(second), blk(lambda i: n - 1 - second(i)),
                blk(lambda i: 0), blk(lambda i: n - 1)]
    args = [u] * 6
    if nb % 8 != 0:
        pf, pb = _s5_perms(nb, tc)
        in_specs += [const((2 * rows, rows)), const((2 * rows, rows))]
        args += [jnp.asarray(pf, BF16), jnp.asarray(pb, BF16)]
    in_specs += [layer((S5_SLABS, 2 * LANES, S5_SLAB_COLS)), layer((S5_SLABS, S5_SLAB_COLS, LANES)),
                 layer((2 * nb, S5_NC)), layer((2 * nb, S5_NC)), const((2 * nb, 2 * S5_NC))]
    args += [w_bu, w_c, are, aim, h0]
    return pl.pallas_call(
        functools.partial(_s5_kernel, nb=nb, tc=tc, tiles_per_pass=tiles_per_pass),
        grid=(n // 2,),
        in_specs=in_specs,
        out_specs=(out_f, out_b, const((2 * nb, 2 * S5_NC))),
        out_shape=(jax.ShapeDtypeStruct(u.shape, F32), jax.ShapeDtypeStruct(u.shape, F32),
                   jax.ShapeDtypeStruct((2 * nb, 2 * S5_NC), F32)),
        scratch_shapes=[pltpu.VMEM((2 * rows, 2 * S5_NC), F32), pltpu.VMEM((2 * rows, 2 * S5_NC), F32),
                        pltpu.VMEM((2 * rows, 2 * S5_NC), BF16), pltpu.VMEM((2 * nb, 2 * S5_NC), F32)],
        compiler_params=pltpu.CompilerParams(dimension_semantics=("arbitrary",), vmem_limit_bytes=VMEM_LIMIT),
        name="s5_scan",
    )(*args)


def _gla_kernel(*refs, length, has_s0, n_prev, nsq, cpi):
    q_ref, k_ref, v_ref, gl_ref, wg_ref, bg_ref = refs[:6]
    s0_ref = refs[6] if has_s0 else None
    prev_refs = refs[6 + has_s0:6 + has_s0 + n_prev]
    o_ref, sfin_ref, sf_ref, sb_ref = refs[-4:]
    c = GLA_CHUNK
    n = length // c
    ri = lax.broadcasted_iota(jnp.int32, (c, c), 0)
    ci = lax.broadcasted_iota(jnp.int32, (c, c), 1)
    lane = lax.broadcasted_iota(jnp.int32, (1, GLA_KEY), 1)
    head_mask = [(lane >= h * GLA_DK) & (lane < (h + 1) * GLA_DK) for h in range(GLA_HEADS)]
    causal = [ci <= ri, ci >= ri]
    tri = [m.astype(BF16) for m in causal]
    end = (c - 1, 0)
    chains = [(sq, d, j) for j in range(cpi) for sq in range(nsq) for d in (0, 1)]
    key_rows = lambda h: slice(h * GLA_DK, (h + 1) * GLA_DK)
    val_cols = lambda h: slice(h * GLA_DV, (h + 1) * GLA_DV)

    sb_ref[...] = jnp.zeros_like(sb_ref)
    for sq in range(nsq):
        for d in (0, 1):
            for h in range(GLA_HEADS):
                if has_s0:
                    sf_ref[sq, d, h] = s0_ref[sq, d, h]
                    sb_ref[sq, d, key_rows(h), val_cols(h)] = s0_ref[sq, d, h].astype(BF16)
                else:
                    sf_ref[sq, d, h] = jnp.zeros((GLA_DK, GLA_DV), F32)

    def chunk_group(r):
        row = lambda d, j: r[d][j]
        qc = [q_ref[sq, pl.ds(row(d, j), c), :] * (GLA_DK ** -0.5) for sq, d, j in chains]
        kc = [k_ref[sq, pl.ds(row(d, j), c), :] for sq, d, j in chains]
        vc = [v_ref[sq, pl.ds(row(d, j), c), :] for sq, d, j in chains]
        logits = [_bdot(gl_ref[sq, pl.ds(row(d, j), c), :], wg_ref[d]) + bg_ref[d] for sq, d, j in chains]
        g = [_split2(_log_sigmoid(x) * (1.0 / GLA_GATE_NORM)) for x in logits]
        b = [jnp.dot(tri[d], hi, preferred_element_type=F32) + jnp.dot(tri[d], lo, preferred_element_type=F32)
             for (sq, d, j), (hi, lo) in zip(chains, g)]
        bt = [x.T for x in b]
        kt = [x.T for x in kc]
        b_mid = [x[c // 2:c // 2 + 1, :] for x in b]
        bt_mid = [x[:, c // 2:c // 2 + 1] for x in bt]
        bt_end = [x[:, end[d]:end[d] + 1] for (sq, d, j), x in zip(chains, bt)]
        q_in = [(x * jnp.exp(y)).astype(BF16) for x, y in zip(qc, b)]
        q_mid = [(x * jnp.exp(y - m)).astype(BF16) for x, y, m in zip(qc, b, b_mid)]
        k_mid_t = [(x * jnp.exp(m - y)).astype(BF16) for x, y, m in zip(kt, bt, bt_mid)]
        k_end_t = [(x * jnp.exp(e - y)).astype(BF16) for x, y, e in zip(kt, bt, bt_end)]
        q_st = [jnp.concatenate([jnp.where(head_mask[h], x, jnp.zeros_like(x)) for h in range(GLA_HEADS)], axis=0)
                for x in q_mid]
        a = [jnp.dot(x, y, preferred_element_type=F32) for x, y in zip(q_st, k_mid_t)]
        parts = [[] for _ in chains]
        for h in range(GLA_HEADS):
            for i, (sq, d, j) in enumerate(chains):
                a_h = jnp.where(causal[d], a[i][h * c:(h + 1) * c, :], 0.0).astype(BF16)
                parts[i].append(jnp.dot(a_h, vc[i][:, val_cols(h)], preferred_element_type=F32))
        upd = [[jnp.dot(k_end_t[i][key_rows(h), :], vc[i][:, val_cols(h)], preferred_element_type=F32)
                for h in range(GLA_HEADS)] for i in range(len(chains))]
        o_inter = [None] * len(chains)
        for i, (sq, d, j) in enumerate(chains):
            o_inter[i] = jnp.dot(q_in[i], sb_ref[sq, d], preferred_element_type=F32)
            for h in range(GLA_HEADS):
                s_new = jnp.exp(bt_end[i][key_rows(h), :]) * sf_ref[sq, d, h] + upd[i][h]
                sf_ref[sq, d, h] = s_new
                sb_ref[sq, d, key_rows(h), val_cols(h)] = s_new.astype(BF16)
        return [x + jnp.concatenate(p, axis=1) for x, p in zip(o_inter, parts)]

    def rows_of(step):
        fwd = [pl.multiple_of((step * cpi + j) * c, c) for j in range(cpi)]
        bwd = [pl.multiple_of((n - 1 - step * cpi - j) * c, c) for j in range(cpi)]
        return fwd, bwd

    def first_half(step, carry):
        r = rows_of(step)
        for (sq, d, j), o in zip(chains, chunk_group(r)):
            o_ref[sq, pl.ds(r[d][j], c), :] = o
        return carry

    def second_half(step, carry):
        r = rows_of(step)
        for (sq, d, j), o in zip(chains, chunk_group(r)):
            o_ref[sq, pl.ds(r[d][j], c), :] = o + o_ref[sq, pl.ds(r[d][j], c), :]
        return carry

    steps = n // cpi
    lax.fori_loop(0, steps // 2, first_half, 0)
    lax.fori_loop(steps // 2, steps, second_half, 0)
    if n_prev:
        for j, prev_ref in enumerate(prev_refs):
            sfin_ref[:, j] = prev_ref[...]
        sfin_ref[:, n_prev] = sf_ref[...]
    else:
        sfin_ref[...] = sf_ref[...]


def _gla_call(q, k, v, gl, l, wg, bg, s0, nsq, cpi, prev_states=()):
    nseq, length, _ = q.shape
    assert (length // GLA_CHUNK) % (2 * cpi) == 0 and nseq % nsq == 0
    seq = lambda width: pl.BlockSpec((nsq, length, width), lambda b: (b, 0, 0))
    layer = lambda shape: pl.BlockSpec((None,) + shape, lambda b: (l,) + (0,) * len(shape))
    in_specs = [seq(GLA_KEY), seq(GLA_KEY), seq(GLA_WIDTH), seq(LANES),
                layer((2, LANES, GLA_KEY)), layer((2, 1, GLA_KEY))]
    args = [q, k, v, gl, wg, bg]
    if s0 is not None:
        in_specs.append(pl.BlockSpec((nsq, None, 2, GLA_HEADS, GLA_DK, GLA_DV), lambda b: (b, l, 0, 0, 0, 0)))
        args.append(s0)
    state = (2, GLA_HEADS, GLA_DK, GLA_DV)
    state_spec = pl.BlockSpec((nsq,) + state, lambda b: (b, 0, 0, 0, 0))
    in_specs += [state_spec] * len(prev_states)
    args += list(prev_states)
    lead = (len(prev_states) + 1,) if prev_states else ()
    return pl.pallas_call(
        functools.partial(_gla_kernel, length=length, has_s0=s0 is not None, n_prev=len(prev_states), nsq=nsq, cpi=cpi),
        grid=(nseq // nsq,),
        in_specs=in_specs,
        out_specs=(seq(GLA_WIDTH), pl.BlockSpec((nsq,) + lead + state, lambda b: (b,) + (0,) * len(lead + state))),
        out_shape=(jax.ShapeDtypeStruct((nseq, length, GLA_WIDTH), F32),
                   jax.ShapeDtypeStruct((nseq,) + lead + state, F32)),
        scratch_shapes=[pltpu.VMEM((nsq, 2, GLA_HEADS, GLA_DK, GLA_DV), F32),
                        pltpu.VMEM((nsq, 2, GLA_KEY, GLA_WIDTH), BF16)],
        compiler_params=pltpu.CompilerParams(dimension_semantics=("arbitrary",), vmem_limit_bytes=VMEM_LIMIT),
        name="gla",
    )(*args)


def _mix_kernel(*refs, has_pe, last, row_groups):
    if has_pe:
        (x_ref, pe_ref, yf_ref, yb_ref, u_ref, o_ref, shift_ref, scale_ref, gate_ref, ng_ref, wg_ref, d_ref,
         wglu_ref, bglu_ref, wpa_ref, wpb_ref, wo_ref, fng_ref, gng_ref, out_ref) = refs
    else:
        (x_ref, yf_ref, yb_ref, u_ref, o_ref, shift_ref, scale_ref, gate_ref, ng_ref, wg_ref, d_ref,
         wglu_ref, bglu_ref, wpa_ref, wpb_ref, wo_ref, fng_ref, gng_ref, out_ref) = refs

    def head_norm(o):
        heads = [o[:, h * GLA_DV:(h + 1) * GLA_DV] for h in range(GLA_HEADS)]
        heads = [v * lax.rsqrt(jnp.mean(v * v, axis=-1, keepdims=True) + EPS) for v in heads]
        return jnp.concatenate(heads, axis=1) * gng_ref[...]

    nsb, tm, _ = x_ref.shape
    grp = [(i, slice(j * tm // row_groups, (j + 1) * tm // row_groups)) for i in range(nsb) for j in range(row_groups)]
    x = [x_ref[i, r, :] + pe_ref[r, :] if has_pe else x_ref[i, r, :] for i, r in grp]
    h = [_modulated_norm(v, ng_ref[...], 1.0 + scale_ref[...], shift_ref[...]).astype(BF16) for v in x]
    gts = [_bdot_t(v, wg_ref[...]) for v in h]
    y = [_gelu_tanh(yf_ref[i, r, :] + yb_ref[i, r, :] + d_ref[...] * u_ref[i, r, :]) for i, r in grp]
    glu = [_bdot(v, wglu_ref[...]) for v in y]
    y = [v * _sigmoid(g + bglu_ref[...]) for v, g in zip(y, glu)]
    y_a = [v * _silu(g[:, :S5_WIDTH]) for v, g in zip(y, gts)]
    y_b = [head_norm(o_ref[i, r, :]) * _silu(g[:, S5_WIDTH:2 * S5_WIDTH]) for (i, r), g in zip(grp, gts)]
    p_a = [_bdot(v, wpa_ref[...]) for v in y_a]
    p_b = [_bdot(v, wpb_ref[...]) for v in y_b]
    merged = [_sigmoid(g[:, 2 * S5_WIDTH:2 * S5_WIDTH + D_MODEL]) * a + _sigmoid(g[:, 2 * S5_WIDTH + D_MODEL:]) * b
              for g, a, b in zip(gts, p_a, p_b)]
    xn = [v + gate_ref[...] * _bdot(m, wo_ref[...]) for v, m in zip(x, merged)]
    for (i, r), v in zip(grp, xn):
        if last:
            ms = jnp.mean(v * v, axis=-1, keepdims=True)
            v = v * lax.rsqrt(ms + EPS) * fng_ref[...]
        out_ref[i, r, :] = v


def _mix_call(x, pe, yf, yb, u, o, mod, l, cond_row, ng, wg, s5_d, wglu, bglu, wpa, wpb, wo, fng, gng, tm, last,
              nsb=1):
    nseq, length, _ = x.shape
    row = lambda width: pl.BlockSpec((nsb, tm, width), lambda j, b: (b, j, 0))
    in_specs = [row(D_MODEL)]
    args = [x]
    if pe is not None:
        in_specs.append(pl.BlockSpec((tm, D_MODEL), lambda j, b: (j, 0)))
        args.append(pe)
    in_specs += [row(S5_WIDTH), row(S5_WIDTH), row(S5_WIDTH), row(GLA_WIDTH),
                 _mod_spec(l, cond_row, 0), _mod_spec(l, cond_row, 1), _mod_spec(l, cond_row, 2),
                 _layer_spec(l, (1, D_MODEL)), _layer_spec(l, (GATE_W, D_MODEL)), _layer_spec(l, (1, S5_WIDTH)),
                 _layer_spec(l, (S5_WIDTH, S5_WIDTH)), _layer_spec(l, (1, S5_WIDTH)),
                 _layer_spec(l, (S5_WIDTH, D_MODEL)), _layer_spec(l, (GLA_WIDTH, D_MODEL)),
                 _layer_spec(l, (D_MODEL, D_MODEL)), pl.BlockSpec((1, D_MODEL), lambda j, b: (0, 0)),
                 _layer_spec(l, (1, GLA_WIDTH))]
    args += [yf, yb, u, o, mod, mod, mod, ng, wg, s5_d, wglu, bglu, wpa, wpb, wo, fng, gng]
    return pl.pallas_call(
        functools.partial(_mix_kernel, has_pe=pe is not None, last=last, row_groups=max(1, tm // MXU_TILE)),
        grid=(length // tm, nseq // nsb),
        in_specs=in_specs,
        out_specs=row(D_MODEL),
        out_shape=jax.ShapeDtypeStruct(x.shape, F32),
        compiler_params=pltpu.CompilerParams(dimension_semantics=("arbitrary", "arbitrary"),
                                             vmem_limit_bytes=VMEM_LIMIT),
        name="mix",
    )(*args)


_CONTEXT_BLOCKS = dict(proj=dict(tm=None, nsb=4), s5=dict(tc=16, tiles_per_pass=2), gla=dict(nsq=4, cpi=1),
                       mix=dict(tm=None, nsb=2))
_LATENT_BLOCKS = dict(proj=dict(tm=1024), s5=dict(tc=32, tiles_per_pass=4), gla=dict(nsq=1, cpi=4), mix=dict(tm=512))


def _blocks(choice, length):
    return {k: dict(v, tm=v["tm"] or length) if "tm" in v else v for k, v in choice.items()}


def _grid_pos_embed(length, dim):
    rows = length // GRID_W
    quarter = dim // 4
    freqs = jnp.exp(-math.log(10000.0) * jnp.arange(quarter, dtype=F32) / quarter)

    def sincos(pos):
        ang = pos.astype(F32)[:, None] * freqs[None, :]
        return jnp.concatenate([jnp.sin(ang), jnp.cos(ang)], axis=-1)

    er = sincos(jnp.arange(rows))
    ec = sincos(jnp.arange(GRID_W))
    pe = jnp.concatenate([jnp.broadcast_to(er[:, None, :], (rows, GRID_W, dim // 2)),
                          jnp.broadcast_to(ec[None, :, :], (rows, GRID_W, dim // 2))], axis=-1)
    return pe.reshape(rows * GRID_W, dim)


def kernel(x_prompt, x_sample, c, state_s5_re, state_s5_im, state_gla, c_ctx, norm_g, w_mod, b_mod, w_in, gla_wg_up,
           gla_bg, gla_norm_g, s5_lam_re, s5_lam_im, s5_log_dt, s5_b_re, s5_b_im, s5_c_re, s5_c_im, s5_d, w_glu,
           b_glu, w_pa, w_pb, w_o, final_norm_g):
    depth = w_in.shape[0]
    bp, lp, _ = x_prompt.shape
    bs, ls, _ = x_sample.shape

    cond8 = jnp.concatenate([c_ctx[None], c, jnp.zeros((8 - 1 - bs, D_MODEL), F32)], axis=0)
    mod = _mod_call(cond8, w_mod, b_mod)

    a_re, a_im, w_bu, w_c = _s5_params(s5_lam_re, s5_lam_im, s5_log_dt, s5_b_re, s5_b_im, s5_c_re, s5_c_im)

    w_proj, w_gate = _split_w_in(w_in)
    wg_up = jnp.concatenate([gla_wg_up, jnp.zeros((depth, 2, LANES - GLA_RANK, GLA_KEY), F32)], axis=2).astype(BF16)
    w_glu_b, w_pa_b, w_pb_b, w_o_b = (w.astype(BF16) for w in (w_glu, w_pa, w_pb, w_o))

    vec = lambda a: a[:, None, :]
    mod = mod.reshape(depth, 8, 1, 3 * D_MODEL)
    ng, gng, bg = vec(norm_g), vec(gla_norm_g), gla_bg[:, :, None, :]
    common = (ng, w_gate, vec(s5_d), w_glu_b, vec(b_glu), w_pa_b, w_pb_b, w_o_b, final_norm_g[None], gng)
    are_p, aim_p = jnp.repeat(a_re, bp, axis=1), jnp.repeat(a_im, bp, axis=1)
    are_s, aim_s = jnp.repeat(a_re, bs, axis=1), jnp.repeat(a_im, bs, axis=1)
    ctx_row = lambda b: 0
    cond_row = lambda b: b + 1

    pe = _grid_pos_embed(ls, D_MODEL)
    blk_p, blk_s = _blocks(_CONTEXT_BLOCKS, lp), _blocks(_LATENT_BLOCKS, ls)
    zero_h0 = jnp.zeros((2 * bp, 2 * S5_NC), F32)
    xp, xs = x_prompt, x_sample
    new_re, new_im, new_gla = [], [], []
    for l in range(depth):
        last = l == depth - 1

        u, q, k, v, gl = _proj_call(xp, None, mod, l, ctx_row, ng, w_proj, **blk_p["proj"])
        yf, yb, hfin = _s5_call(u, l, w_bu, w_c, are_p, aim_p, zero_h0, **blk_p["s5"])
        o, sfin = _gla_call(q, k, v, gl, l, wg_up, bg, None, prev_states=new_gla if last else (), **blk_p["gla"])
        new_gla.append(sfin)
        xp = _mix_call(xp, None, yf, yb, u, o, mod, l, ctx_row, *common, last=last, **blk_p["mix"])
        re_l, im_l = _s5_cols_to_state(hfin.reshape(2, bp, 2 * S5_NC))
        new_re.append(jnp.swapaxes(re_l, 0, 1))
        new_im.append(jnp.swapaxes(im_l, 0, 1))

        pe_l = pe if l == 0 else None
        u, q, k, v, gl = _proj_call(xs, pe_l, mod, l, cond_row, ng, w_proj, **blk_s["proj"])
        h0 = _s5_state_to_cols(jnp.swapaxes(state_s5_re[:, l], 0, 1), jnp.swapaxes(state_s5_im[:, l], 0, 1))
        yf, yb, _ = _s5_call(u, l, w_bu, w_c, are_s, aim_s, h0.reshape(2 * bs, 2 * S5_NC), **blk_s["s5"])
        o, _ = _gla_call(q, k, v, gl, l, wg_up, bg, state_gla, **blk_s["gla"])
        xs = _mix_call(xs, pe_l, yf, yb, u, o, mod, l, cond_row, *common, last=last, **blk_s["mix"])

    new_gla = new_gla[-1] if depth > 1 else new_gla[0][:, None]
    return (xp, xs, jnp.stack(new_re, axis=1), jnp.stack(new_im, axis=1), new_gla)
```
